```python
import jax, jax.numpy as jnp
from jax import lax
import numpy as np

D_MODEL = 1024
BATCH = 16
SEQ = 4096
DEPTH = 1

ATTN_HEADS = 8
ATTN_KV_HEADS = 2
ATTN_HEAD_DIM = 64
ATTN_GROUP = ATTN_HEADS // ATTN_KV_HEADS
WINDOW = 128
ATTN_BLOCK = 128
ROT_DIM = ATTN_HEAD_DIM // 4
ROPE_THETA = 500000.0
ATTN_Q_WIDTH = ATTN_HEADS * ATTN_HEAD_DIM
ATTN_KV_WIDTH = ATTN_KV_HEADS * ATTN_HEAD_DIM

DN_HEADS = 4
DN_KEY_DIM = 128
DN_VAL_DIM = 128
DN_CONV = 4
DN_CHUNK = 64
DN_K_WIDTH = DN_HEADS * DN_KEY_DIM
DN_V_WIDTH = DN_HEADS * DN_VAL_DIM
DN_CONV_DIM = 2 * DN_K_WIDTH + DN_V_WIDTH

MIX_WIDTH = ATTN_Q_WIDTH + DN_V_WIDTH

FFN_HIDDEN = ((-(-8 * D_MODEL // 3) + 255) // 256) * 256

NORM_EPS = 1e-6

IN_SPLIT_SIZES = (ATTN_Q_WIDTH, ATTN_KV_WIDTH, ATTN_KV_WIDTH, DN_CONV_DIM,
                  DN_HEADS, DN_HEADS, DN_V_WIDTH, D_MODEL, D_MODEL)
IN_WIDTH = sum(IN_SPLIT_SIZES)

kernel_name = "hybrid_swa_sink_gdn_swiglu_adaln"

F32 = jnp.float32


def rms_norm(x, gain):
    xf = x.astype(F32)
    y = xf * lax.rsqrt(jnp.mean(xf * xf, axis=-1, keepdims=True) + NORM_EPS)
    return (y * gain.astype(F32)).astype(x.dtype)


def l2_norm(x):
    return x * lax.rsqrt(jnp.sum(x * x, axis=-1, keepdims=True) + NORM_EPS)


def modulate(h, shift, scale):
    return h * (1 + scale[:, None, :]) + shift[:, None, :]


def partial_rope(x, cos, sin):
    half = ROT_DIM // 2
    x1 = x[..., :half].astype(F32)
    x2 = x[..., half:ROT_DIM].astype(F32)
    rot = jnp.concatenate([x1 * cos - x2 * sin, x2 * cos + x1 * sin], axis=-1)
    return jnp.concatenate([rot.astype(x.dtype), x[..., ROT_DIM:]], axis=-1)


def sliding_window_attention(q, k, v, sinks):
    B, S = q.shape[0], q.shape[1]
    nb = S // ATTN_BLOCK
    qb = q.astype(F32).reshape(B, nb, ATTN_BLOCK, ATTN_KV_HEADS, ATTN_GROUP, ATTN_HEAD_DIM)

    def with_prev(t):
        tb = t.astype(F32).reshape(B, nb, ATTN_BLOCK, ATTN_KV_HEADS, ATTN_HEAD_DIM)
        prev = jnp.pad(tb, ((0, 0), (1, 0), (0, 0), (0, 0), (0, 0)))[:, :-1]
        return jnp.concatenate([prev, tb], axis=2)

    kb = with_prev(k)
    vb = with_prev(v)
    s = jnp.einsum('bnqhgd,bnkhd->bnhgqk', qb, kb) * (ATTN_HEAD_DIM ** -0.5)
    qi = jnp.arange(ATTN_BLOCK)[:, None] + ATTN_BLOCK
    kj = jnp.arange(2 * ATTN_BLOCK)[None, :]
    dist = qi - kj
    band = (dist >= 0) & (dist < WINDOW)
    key_pos = jnp.arange(nb)[:, None] * ATTN_BLOCK + kj - ATTN_BLOCK
    valid = band[None] & (key_pos >= 0)[:, None, :]
    s = jnp.where(valid[None, :, None, None], s, -jnp.inf)
    sink = sinks.astype(F32).reshape(ATTN_KV_HEADS, ATTN_GROUP)[None, None, :, :, None, None]
    m = jnp.maximum(jnp.max(s, axis=-1, keepdims=True), sink)
    p = jnp.exp(s - m)
    probs = p / (jnp.sum(p, axis=-1, keepdims=True) + jnp.exp(sink - m))
    o = jnp.einsum('bnhgqk,bnkhd->bnqhgd', probs, vb)
    return o.reshape(B, S, ATTN_Q_WIDTH)


def causal_short_conv(x, w):
    y = lax.conv_general_dilated(x, w, window_strides=(1,), padding=[(DN_CONV - 1, 0)],
                                 dimension_numbers=('NWC', 'WIO', 'NWC'),
                                 feature_group_count=x.shape[-1])
    return jax.nn.silu(y)


def gated_delta_rule(q, k, v, g, beta):
    B, S, H, dk = q.shape
    dv = v.shape[-1]
    n = S // DN_CHUNK
    C = DN_CHUNK

    def chunk(t):
        return t.reshape(B, n, C, H, -1).transpose(0, 1, 3, 2, 4)

    q = chunk(q) * (dk ** -0.5)
    k = chunk(k)
    v = chunk(v)
    g = g.reshape(B, n, C, H).transpose(0, 1, 3, 2)
    beta = beta.reshape(B, n, C, H).transpose(0, 1, 3, 2)
    gc = jnp.cumsum(g, axis=-1)
    incl = jnp.tril(jnp.ones((C, C), dtype=bool))
    strict = jnp.tril(jnp.ones((C, C), dtype=bool), -1)
    decay = jnp.exp(jnp.where(incl, gc[..., :, None] - gc[..., None, :], -jnp.inf))
    kb = k * beta[..., None]
    L = jnp.where(strict, jnp.einsum('bnhid,bnhjd->bnhij', kb, k) * decay, 0.0)
    eye = jnp.eye(C, dtype=F32)
    T = lax.linalg.triangular_solve(L + eye, jnp.broadcast_to(eye, L.shape), left_side=True,
                                    lower=True, unit_diagonal=True)
    u = jnp.einsum('bnhij,bnhjd->bnhid', T, v * beta[..., None])
    w = jnp.einsum('bnhij,bnhjd->bnhid', T, kb * jnp.exp(gc)[..., None])
    a_intra = jnp.einsum('bnhid,bnhjd->bnhij', q, k) * decay
    q_dec = q * jnp.exp(gc)[..., None]
    g_last = gc[..., -1:]
    k_dec = k * jnp.exp(g_last - gc)[..., None]
    chunk_decay = jnp.exp(g_last[..., 0])

    def step(state, xs):
        u_c, w_c, q_c, k_c, a_c, d_c = xs
        v_new = u_c - jnp.einsum('bhcd,bhde->bhce', w_c, state)
        o_c = (jnp.einsum('bhcd,bhde->bhce', q_c, state)
               + jnp.einsum('bhij,bhje->bhie', a_c, v_new))
        state = state * d_c[..., None, None] + jnp.einsum('bhcd,bhce->bhde', k_c, v_new)
        return state, o_c

    xs = (jnp.moveaxis(u, 1, 0), jnp.moveaxis(w, 1, 0), jnp.moveaxis(q_dec, 1, 0),
          jnp.moveaxis(k_dec, 1, 0), jnp.moveaxis(a_intra, 1, 0), jnp.moveaxis(chunk_decay, 1, 0))
    state0 = jnp.zeros((B, H, dk, dv), dtype=F32)
    _, o = lax.scan(step, state0, xs)
    return o.transpose(1, 0, 3, 2, 4).reshape(B, S, H, dv)


def token_mixers(h, positions, w_in, conv_w, q_norm_g, k_norm_g, sinks, a_log, dt_bias,
                 dn_norm_g, w_branch, w_out):
    B, S, _ = h.shape
    split_idx = np.cumsum(IN_SPLIT_SIZES)[:-1].tolist()
    proj = h @ w_in
    aq, ak, av, dn_qkv, dn_b, dn_a, dn_z, gate_a, gate_d = jnp.split(proj, split_idx, axis=-1)

    aq = rms_norm(aq.reshape(B, S, ATTN_HEADS, ATTN_HEAD_DIM), q_norm_g)
    ak = rms_norm(ak.reshape(B, S, ATTN_KV_HEADS, ATTN_HEAD_DIM), k_norm_g)
    av = av.reshape(B, S, ATTN_KV_HEADS, ATTN_HEAD_DIM)
    inv_freq = ROPE_THETA ** (-jnp.arange(0, ROT_DIM, 2, dtype=F32) / ROT_DIM)
    ang = positions.astype(F32)[..., None] * inv_freq
    cos = jnp.cos(ang)[:, :, None, :]
    sin = jnp.sin(ang)[:, :, None, :]
    aq = partial_rope(aq, cos, sin)
    ak = partial_rope(ak, cos, sin)
    o_attn = sliding_window_attention(aq, ak, av, sinks)

    qkv = causal_short_conv(dn_qkv, conv_w)
    dq, dk_, dv_ = jnp.split(qkv, [DN_K_WIDTH, 2 * DN_K_WIDTH], axis=-1)
    dq = l2_norm(dq.reshape(B, S, DN_HEADS, DN_KEY_DIM).astype(F32))
    dk_ = l2_norm(dk_.reshape(B, S, DN_HEADS, DN_KEY_DIM).astype(F32))
    dv_ = dv_.reshape(B, S, DN_HEADS, DN_VAL_DIM).astype(F32)
    beta = jax.nn.sigmoid(dn_b.astype(F32))
    g = -jnp.exp(a_log.astype(F32)) * jax.nn.softplus(dn_a.astype(F32) + dt_bias.astype(F32))
    o_dn = gated_delta_rule(dq, dk_, dv_, g, beta)
    o_dn = rms_norm(o_dn, dn_norm_g) * jax.nn.silu(
        dn_z.astype(F32).reshape(B, S, DN_HEADS, DN_VAL_DIM))
    o_dn = o_dn.reshape(B, S, DN_V_WIDTH)

    y_attn = o_attn.astype(h.dtype) @ w_branch[:ATTN_Q_WIDTH]
    y_dn = o_dn.astype(h.dtype) @ w_branch[ATTN_Q_WIDTH:]
    merged = jax.nn.sigmoid(gate_a) * y_attn + jax.nn.sigmoid(gate_d) * y_dn
    return merged @ w_out


def swiglu(h, w_gate_up, w_down):
    gate, up = jnp.split(h @ w_gate_up, 2, axis=-1)
    return (jax.nn.silu(gate) * up) @ w_down


def _fwd_setup_inputs(seed: int = 0) -> dict:
    key = jax.random.key(seed)
    ks = jax.random.split(key, 24)
    D = D_MODEL

    def nrm(k, shape, fan_in):
        return jax.random.normal(k, shape, F32) * (fan_in ** -0.5)

    def gain(k, shape):
        return 1.0 + 0.02 * jax.random.normal(k, shape, F32)

    x = jax.random.normal(ks[0], (BATCH, SEQ, D), F32)
    c = jax.random.normal(ks[1], (BATCH, D), F32)
    offset = jax.random.randint(ks[2], (BATCH, 1), 0, 4096, dtype=jnp.int32)
    positions = offset + jnp.arange(SEQ, dtype=jnp.int32)[None, :]
    dt = jnp.exp(jax.random.uniform(ks[13], (DEPTH, DN_HEADS), F32,
                                    jnp.log(0.001), jnp.log(0.1)))
    return {
        "x": x,
        "c": c,
        "positions": positions,
        "ada_w": nrm(ks[3], (DEPTH, D, 6 * D), D),
        "ada_b": 0.01 * jax.random.normal(ks[4], (DEPTH, 6 * D), F32),
        "norm1_g": gain(ks[5], (DEPTH, D)),
        "w_in": nrm(ks[6], (DEPTH, D, IN_WIDTH), D),
        "conv_w": nrm(ks[7], (DEPTH, DN_CONV, 1, DN_CONV_DIM), DN_CONV),
        "q_norm_g": gain(ks[8], (DEPTH, ATTN_HEAD_DIM)),
        "k_norm_g": gain(ks[9], (DEPTH, ATTN_HEAD_DIM)),
        "sinks": 0.5 * jax.random.normal(ks[10], (DEPTH, ATTN_HEADS), F32),
        "a_log": jnp.log(jax.random.uniform(ks[11], (DEPTH, DN_HEADS), F32, 1.0, 16.0)),
        "dt_bias": dt + jnp.log(-jnp.expm1(-dt)),
        "dn_norm_g": gain(ks[12], (DEPTH, DN_VAL_DIM)),
        "w_branch": nrm(ks[14], (DEPTH, MIX_WIDTH, D), ATTN_Q_WIDTH),
        "w_out": nrm(ks[15], (DEPTH, D, D), D),
        "norm2_g": gain(ks[16], (DEPTH, D)),
        "w_gate_up": nrm(ks[17], (DEPTH, D, 2 * FFN_HIDDEN), D),
        "w_down": nrm(ks[18], (DEPTH, FFN_HIDDEN, D), FFN_HIDDEN),
    }


def _fwd_reference(x, c, positions, ada_w, ada_b, norm1_g, w_in, conv_w, q_norm_g, k_norm_g,
              sinks, a_log, dt_bias, dn_norm_g, w_branch, w_out, norm2_g, w_gate_up, w_down):
    cond = jax.nn.silu(c)
    for l in range(DEPTH):
        mod = cond @ ada_w[l] + ada_b[l]
        shift1, scale1, gate1, shift2, scale2, gate2 = jnp.split(mod, 6, axis=-1)
        h = modulate(rms_norm(x, norm1_g[l]), shift1, scale1)
        x = x + gate1[:, None, :] * token_mixers(
            h, positions, w_in[l], conv_w[l], q_norm_g[l], k_norm_g[l], sinks[l], a_log[l],
            dt_bias[l], dn_norm_g[l], w_branch[l], w_out[l])
        h = modulate(rms_norm(x, norm2_g[l]), shift2, scale2)
        x = x + gate2[:, None, :] * swiglu(h, w_gate_up[l], w_down[l])
    return x


import jax as _jax
import jax.numpy as _jnp

TWIN_FORMAT = 'train_step'
FWD_PARAMS = ['x', 'c', 'positions', 'ada_w', 'ada_b', 'norm1_g', 'w_in', 'conv_w', 'q_norm_g', 'k_norm_g', 'sinks', 'a_log', 'dt_bias', 'dn_norm_g', 'w_branch', 'w_out', 'norm2_g', 'w_gate_up', 'w_down']
TWIN_WEIGHTS = ['ada_w', 'ada_b', 'norm1_g', 'w_in', 'conv_w', 'q_norm_g', 'k_norm_g', 'sinks', 'a_log', 'dt_bias', 'dn_norm_g', 'w_branch', 'w_out', 'norm2_g', 'w_gate_up', 'w_down']
TWIN_DIFF_INPUT = 'x'
TWIN_INPUTS = ['x', 'c', 'positions', 'ada_w', 'ada_b', 'norm1_g', 'w_in', 'conv_w', 'q_norm_g', 'k_norm_g', 'sinks', 'a_log', 'dt_bias', 'dn_norm_g', 'w_branch', 'w_out', 'norm2_g', 'w_gate_up', 'w_down', 'loss_target', 'm_ada_w', 'm_ada_b', 'm_norm1_g', 'm_w_in', 'm_conv_w', 'm_q_norm_g', 'm_k_norm_g', 'm_sinks', 'm_a_log', 'm_dt_bias', 'm_dn_norm_g', 'm_w_branch', 'm_w_out', 'm_norm2_g', 'm_w_gate_up', 'm_w_down', 'v_ada_w', 'v_ada_b', 'v_norm1_g', 'v_w_in', 'v_conv_w', 'v_q_norm_g', 'v_k_norm_g', 'v_sinks', 'v_a_log', 'v_dt_bias', 'v_dn_norm_g', 'v_w_branch', 'v_w_out', 'v_norm2_g', 'v_w_gate_up', 'v_w_down']
TWIN_OUTPUTS = ['loss', 'grad_x', 'grad_ada_w', 'grad_ada_b', 'grad_norm1_g', 'grad_w_in', 'grad_conv_w', 'grad_q_norm_g', 'grad_k_norm_g', 'grad_sinks', 'grad_a_log', 'grad_dt_bias', 'grad_dn_norm_g', 'grad_w_branch', 'grad_w_out', 'grad_norm2_g', 'grad_w_gate_up', 'grad_w_down', 'delta_ada_w', 'delta_ada_b', 'delta_norm1_g', 'delta_w_in', 'delta_conv_w', 'delta_q_norm_g', 'delta_k_norm_g', 'delta_sinks', 'delta_a_log', 'delta_dt_bias', 'delta_dn_norm_g', 'delta_w_branch', 'delta_w_out', 'delta_norm2_g', 'delta_w_gate_up', 'delta_w_down', 'new_m_ada_w', 'new_m_ada_b', 'new_m_norm1_g', 'new_m_w_in', 'new_m_conv_w', 'new_m_q_norm_g', 'new_m_k_norm_g', 'new_m_sinks', 'new_m_a_log', 'new_m_dt_bias', 'new_m_dn_norm_g', 'new_m_w_branch', 'new_m_w_out', 'new_m_norm2_g', 'new_m_w_gate_up', 'new_m_w_down', 'new_v_ada_w', 'new_v_ada_b', 'new_v_norm1_g', 'new_v_w_in', 'new_v_conv_w', 'new_v_q_norm_g', 'new_v_k_norm_g', 'new_v_sinks', 'new_v_a_log', 'new_v_dt_bias', 'new_v_dn_norm_g', 'new_v_w_branch', 'new_v_w_out', 'new_v_norm2_g', 'new_v_w_gate_up', 'new_v_w_down']
TWIN_LEAF_KINDS = {'loss': 'loss', 'grad_x': 'grad_x', 'grad_ada_w': 'grad_w', 'grad_ada_b': 'grad_w', 'grad_norm1_g': 'grad_w', 'grad_w_in': 'grad_w', 'grad_conv_w': 'grad_w', 'grad_q_norm_g': 'grad_w', 'grad_k_norm_g': 'grad_w', 'grad_sinks': 'grad_w', 'grad_a_log': 'grad_w', 'grad_dt_bias': 'grad_w', 'grad_dn_norm_g': 'grad_w', 'grad_w_branch': 'grad_w', 'grad_w_out': 'grad_w', 'grad_norm2_g': 'grad_w', 'grad_w_gate_up': 'grad_w', 'grad_w_down': 'grad_w', 'delta_ada_w': 'delta_w', 'delta_ada_b': 'delta_w', 'delta_norm1_g': 'delta_w', 'delta_w_in': 'delta_w', 'delta_conv_w': 'delta_w', 'delta_q_norm_g': 'delta_w', 'delta_k_norm_g': 'delta_w', 'delta_sinks': 'delta_w', 'delta_a_log': 'delta_w', 'delta_dt_bias': 'delta_w', 'delta_dn_norm_g': 'delta_w', 'delta_w_branch': 'delta_w', 'delta_w_out': 'delta_w', 'delta_norm2_g': 'delta_w', 'delta_w_gate_up': 'delta_w', 'delta_w_down': 'delta_w', 'new_m_ada_w': 'new_m', 'new_m_ada_b': 'new_m', 'new_m_norm1_g': 'new_m', 'new_m_w_in': 'new_m', 'new_m_conv_w': 'new_m', 'new_m_q_norm_g': 'new_m', 'new_m_k_norm_g': 'new_m', 'new_m_sinks': 'new_m', 'new_m_a_log': 'new_m', 'new_m_dt_bias': 'new_m', 'new_m_dn_norm_g': 'new_m', 'new_m_w_branch': 'new_m', 'new_m_w_out': 'new_m', 'new_m_norm2_g': 'new_m', 'new_m_w_gate_up': 'new_m', 'new_m_w_down': 'new_m', 'new_v_ada_w': 'new_v', 'new_v_ada_b': 'new_v', 'new_v_norm1_g': 'new_v', 'new_v_w_in': 'new_v', 'new_v_conv_w': 'new_v', 'new_v_q_norm_g': 'new_v', 'new_v_k_norm_g': 'new_v', 'new_v_sinks': 'new_v', 'new_v_a_log': 'new_v', 'new_v_dt_bias': 'new_v', 'new_v_dn_norm_g': 'new_v', 'new_v_w_branch': 'new_v', 'new_v_w_out': 'new_v', 'new_v_norm2_g': 'new_v', 'new_v_w_gate_up': 'new_v', 'new_v_w_down': 'new_v'}


def _forward(args):
    return _fwd_reference(*[args[k] for k in FWD_PARAMS])


def _output_shape():
    out = _jax.eval_shape(lambda: _forward(_fwd_setup_inputs(0)))
    return out.shape, out.dtype

N_MICROBATCH = 1
ADAM_LR = 0.001
ADAM_B1 = 0.9
ADAM_B2 = 0.999
ADAM_EPS = 1e-08
ADAM_WD = 0.01
ADAM_STEP = 10
PER_EXAMPLE_BATCH_AXIS = {'x': 0, 'c': 0, 'positions': 0, 'loss_target': 0}
SHARED_INPUTS = []
_WEIGHT_DTYPES = {'ada_w': _jnp.float32, 'ada_b': _jnp.float32, 'norm1_g': _jnp.float32, 'w_in': _jnp.float32, 'conv_w': _jnp.float32, 'q_norm_g': _jnp.float32, 'k_norm_g': _jnp.float32, 'sinks': _jnp.float32, 'a_log': _jnp.float32, 'dt_bias': _jnp.float32, 'dn_norm_g': _jnp.float32, 'w_branch': _jnp.float32, 'w_out': _jnp.float32, 'norm2_g': _jnp.float32, 'w_gate_up': _jnp.float32, 'w_down': _jnp.float32}
MOMENT_SCALE = {'ada_w': 1.083240e+01, 'ada_b': 2.308198e+01, 'norm1_g': 5.429342e+00, 'w_in': 3.248417e+00, 'conv_w': 2.312361e+00, 'q_norm_g': 1.494727e+00, 'k_norm_g': 1.470013e+00, 'sinks': 1.307992e+00, 'a_log': 1.887562e+01, 'dt_bias': 1.775521e+01, 'dn_norm_g': 4.201162e+01, 'w_branch': 3.662916e+00, 'w_out': 4.969933e+00, 'norm2_g': 5.939367e+01, 'w_gate_up': 2.665867e+00, 'w_down': 2.156837e+00}


def _to_microbatches(a, axis):
    t = _jnp.moveaxis(a, axis, 0)
    t = t.reshape((N_MICROBATCH, t.shape[0] // N_MICROBATCH) + t.shape[1:])
    return _jnp.moveaxis(t, 1, axis + 1)


def setup_inputs(seed: int = 0) -> dict:
    inp = _fwd_setup_inputs(seed)
    key = _jax.random.fold_in(_jax.random.key(seed), 7919)
    shape, _ = _output_shape()
    out = dict(inp)
    out["loss_target"] = _jax.random.normal(_jax.random.fold_in(key, 0), shape, _jnp.float32)
    for i, name in enumerate(TWIN_WEIGHTS):
        w = inp[name].astype(_jnp.float32)
        if MOMENT_SCALE is None:
            s = _jnp.sqrt(_jnp.mean(_jnp.square(w)) + 1e-30)
        else:
            s = MOMENT_SCALE[name]
        km, kv = _jax.random.split(_jax.random.fold_in(key, i + 1))
        out[name] = w
        out["m_" + name] = s * _jax.random.normal(km, w.shape, _jnp.float32)
        out["v_" + name] = (s * s) * _jax.random.uniform(kv, w.shape, _jnp.float32, 0.5, 1.5)
    if N_MICROBATCH > 1:
        for name, axis in PER_EXAMPLE_BATCH_AXIS.items():
            out[name] = _to_microbatches(out[name], axis)
    return {'x': out['x'], 'c': out['c'], 'positions': out['positions'], 'ada_w': out['ada_w'], 'ada_b': out['ada_b'], 'norm1_g': out['norm1_g'], 'w_in': out['w_in'], 'conv_w': out['conv_w'], 'q_norm_g': out['q_norm_g'], 'k_norm_g': out['k_norm_g'], 'sinks': out['sinks'], 'a_log': out['a_log'], 'dt_bias': out['dt_bias'], 'dn_norm_g': out['dn_norm_g'], 'w_branch': out['w_branch'], 'w_out': out['w_out'], 'norm2_g': out['norm2_g'], 'w_gate_up': out['w_gate_up'], 'w_down': out['w_down'], 'loss_target': out['loss_target'], 'm_ada_w': out['m_ada_w'], 'm_ada_b': out['m_ada_b'], 'm_norm1_g': out['m_norm1_g'], 'm_w_in': out['m_w_in'], 'm_conv_w': out['m_conv_w'], 'm_q_norm_g': out['m_q_norm_g'], 'm_k_norm_g': out['m_k_norm_g'], 'm_sinks': out['m_sinks'], 'm_a_log': out['m_a_log'], 'm_dt_bias': out['m_dt_bias'], 'm_dn_norm_g': out['m_dn_norm_g'], 'm_w_branch': out['m_w_branch'], 'm_w_out': out['m_w_out'], 'm_norm2_g': out['m_norm2_g'], 'm_w_gate_up': out['m_w_gate_up'], 'm_w_down': out['m_w_down'], 'v_ada_w': out['v_ada_w'], 'v_ada_b': out['v_ada_b'], 'v_norm1_g': out['v_norm1_g'], 'v_w_in': out['v_w_in'], 'v_conv_w': out['v_conv_w'], 'v_q_norm_g': out['v_q_norm_g'], 'v_k_norm_g': out['v_k_norm_g'], 'v_sinks': out['v_sinks'], 'v_a_log': out['v_a_log'], 'v_dt_bias': out['v_dt_bias'], 'v_dn_norm_g': out['v_dn_norm_g'], 'v_w_branch': out['v_w_branch'], 'v_w_out': out['v_w_out'], 'v_norm2_g': out['v_norm2_g'], 'v_w_gate_up': out['v_w_gate_up'], 'v_w_down': out['v_w_down']}


def _loss(weights, diff, rest, loss_target):
    with _jax.named_scope("forward"):
        args = {**rest, TWIN_DIFF_INPUT: diff, **{k: w.astype(_WEIGHT_DTYPES[k]) for k, w in weights.items()}}
        y = _forward(args)
    with _jax.named_scope("loss_head"):
        err = _jnp.square(y.astype(_jnp.float32) - loss_target)
        return 0.5 * _jnp.sum(_jnp.mean(err, axis=-1)) if err.ndim else 0.5 * err


def _adamw(w, g, m, v):
    m = ADAM_B1 * m + (1.0 - ADAM_B1) * g
    v = ADAM_B2 * v + (1.0 - ADAM_B2) * _jnp.square(g)
    m_hat = m / (1.0 - ADAM_B1 ** ADAM_STEP)
    v_hat = v / (1.0 - ADAM_B2 ** ADAM_STEP)
    delta = -ADAM_LR * (m_hat / (_jnp.sqrt(v_hat) + ADAM_EPS) + ADAM_WD * w)
    return delta, m, v


def reference(x, c, positions, ada_w, ada_b, norm1_g, w_in, conv_w, q_norm_g, k_norm_g, sinks, a_log, dt_bias, dn_norm_g, w_branch, w_out, norm2_g, w_gate_up, w_down, loss_target, m_ada_w, m_ada_b, m_norm1_g, m_w_in, m_conv_w, m_q_norm_g, m_k_norm_g, m_sinks, m_a_log, m_dt_bias, m_dn_norm_g, m_w_branch, m_w_out, m_norm2_g, m_w_gate_up, m_w_down, v_ada_w, v_ada_b, v_norm1_g, v_w_in, v_conv_w, v_q_norm_g, v_k_norm_g, v_sinks, v_a_log, v_dt_bias, v_dn_norm_g, v_w_branch, v_w_out, v_norm2_g, v_w_gate_up, v_w_down):
    given = dict(x=x, c=c, positions=positions, ada_w=ada_w, ada_b=ada_b, norm1_g=norm1_g, w_in=w_in, conv_w=conv_w, q_norm_g=q_norm_g, k_norm_g=k_norm_g, sinks=sinks, a_log=a_log, dt_bias=dt_bias, dn_norm_g=dn_norm_g, w_branch=w_branch, w_out=w_out, norm2_g=norm2_g, w_gate_up=w_gate_up, w_down=w_down, loss_target=loss_target, m_ada_w=m_ada_w, m_ada_b=m_ada_b, m_norm1_g=m_norm1_g, m_w_in=m_w_in, m_conv_w=m_conv_w, m_q_norm_g=m_q_norm_g, m_k_norm_g=m_k_norm_g, m_sinks=m_sinks, m_a_log=m_a_log, m_dt_bias=m_dt_bias, m_dn_norm_g=m_dn_norm_g, m_w_branch=m_w_branch, m_w_out=m_w_out, m_norm2_g=m_norm2_g, m_w_gate_up=m_w_gate_up, m_w_down=m_w_down, v_ada_w=v_ada_w, v_ada_b=v_ada_b, v_norm1_g=v_norm1_g, v_w_in=v_w_in, v_conv_w=v_conv_w, v_q_norm_g=v_q_norm_g, v_k_norm_g=v_k_norm_g, v_sinks=v_sinks, v_a_log=v_a_log, v_dt_bias=v_dt_bias, v_dn_norm_g=v_dn_norm_g, v_w_branch=v_w_branch, v_w_out=v_w_out, v_norm2_g=v_norm2_g, v_w_gate_up=v_w_gate_up, v_w_down=v_w_down)
    weights = {n: given[n] for n in TWIN_WEIGHTS}
    shared = {n: given[n] for n in SHARED_INPUTS}
    per_example = {n: given[n] for n in ['x', 'c', 'positions']}
    grad_fn = _jax.value_and_grad(_loss, argnums=(0, 1))

    def one_microbatch(ex, loss_target):
        ex = dict(ex)
        diff = ex.pop(TWIN_DIFF_INPUT)
        return grad_fn(weights, diff, {**shared, **ex}, loss_target)

    if N_MICROBATCH == 1:
        loss, (grad_w, grad_x) = one_microbatch(per_example, given["loss_target"])
    else:
        def body(carry, xs):
            loss_sum, grad_sum = carry
            l_k, (gw_k, gx_k) = one_microbatch(xs[0], xs[1])
            with _jax.named_scope("update"):
                return (loss_sum + l_k, _jax.tree.map(_jnp.add, grad_sum, gw_k)), gx_k

        init = (_jnp.zeros((), _jnp.float32), _jax.tree.map(_jnp.zeros_like, weights))
        (loss, grad_w), grad_x = _jax.lax.scan(body, init, (per_example, given["loss_target"]))
    with _jax.named_scope("update"):
        delta_w, new_m, new_v = {}, {}, {}
        for n in TWIN_WEIGHTS:
            delta_w[n], new_m[n], new_v[n] = _adamw(weights[n], grad_w[n], given["m_" + n], given["v_" + n])
    return (loss, grad_x, *[grad_w[n] for n in TWIN_WEIGHTS], *[delta_w[n] for n in TWIN_WEIGHTS],
            *[new_m[n] for n in TWIN_WEIGHTS], *[new_v[n] for n in TWIN_WEIGHTS])
```

```python
import functools

import jax
import jax.numpy as jnp
import numpy as np
from jax import lax
from jax.experimental import pallas as pl
from jax.experimental.pallas import tpu as pltpu

F32 = jnp.float32
BF = jnp.bfloat16

D = 1024
HEADS = 8
KV_HEADS = 2
GROUP = 4
HD = 64
BLK = 128
ROT = 16
THETA = 500000.0
QW = 512
KVW = 128
DN_H = 4
DN_D = 128
CONV = 4
CHUNK = 64
DNW = 512
CONVW = 1536
FFN = 2816
EPS = 1e-6
IN_WIDTH = 4872
IN_PAD = 4992
C_KV = 512
C_DN = 768
C_Z = 2304
C_GA = 2816
C_GB = 3840
C_BA = 4864
NEG = -1e30
N_DEV = 8
N_CHIP = 4

ADAM_LR = 0.001
ADAM_B1 = 0.9
ADAM_B2 = 0.999
ADAM_EPS = 1e-08
ADAM_WD = 0.01
ADAM_STEP = 10

VMEM_LIMIT = 60 * 1024 * 1024


def _cparams(**kw):
    return pltpu.CompilerParams(vmem_limit_bytes=VMEM_LIMIT, **kw)


def _dg(a, b, ca, cb):
    return lax.dot_general(a.astype(BF), b.astype(BF), (((ca,), (cb,)), ((), ())),
                           preferred_element_type=F32)


@jax.custom_vjp
def _mm(a, b):
    return _dg(a, b, 1, 0)


def _mm_fwd(a, b):
    return _dg(a, b, 1, 0), (a, b)


def _mm_bwd(res, dy):
    a, b = res
    return _dg(dy, b, 1, 1).astype(a.dtype), _dg(a, dy, 0, 0).astype(b.dtype)


_mm.defvjp(_mm_fwd, _mm_bwd)


@jax.custom_vjp
def _mm_nt(a, b):
    return _dg(a, b, 1, 1)


def _mm_nt_fwd(a, b):
    return _dg(a, b, 1, 1), (a, b)


def _mm_nt_bwd(res, dy):
    a, b = res
    return _dg(dy, b, 1, 0).astype(a.dtype), _dg(dy, a, 0, 0).astype(b.dtype)


_mm_nt.defvjp(_mm_nt_fwd, _mm_nt_bwd)


@jax.custom_vjp
def _mm_tn(a, b):
    return _dg(a, b, 0, 0)


def _mm_tn_fwd(a, b):
    return _dg(a, b, 0, 0), (a, b)


def _mm_tn_bwd(res, dy):
    a, b = res
    return _dg(b, dy, 1, 1).astype(a.dtype), _dg(a, dy, 1, 0).astype(b.dtype)


_mm_tn.defvjp(_mm_tn_fwd, _mm_tn_bwd)


def _mmx(a, b):
    return jnp.dot(a, b, precision=lax.Precision.HIGHEST, preferred_element_type=F32)


def _mmx_nt(a, b):
    return lax.dot_general(a, b, (((1,), (1,)), ((), ())), precision=lax.Precision.HIGHEST,
                           preferred_element_type=F32)


def _iota(shape, dim):
    return lax.broadcasted_iota(jnp.int32, shape, dim)


def _sigmoid(x):
    return 1.0 / (1.0 + jnp.exp(-x))


def _silu(x):
    return x * _sigmoid(x)


def _softplus(x):
    return jnp.maximum(x, 0.0) + jnp.log(1.0 + jnp.exp(-jnp.abs(x)))


def _rms(x, gain):
    return x * lax.rsqrt(jnp.mean(x * x, axis=-1, keepdims=True) + EPS) * gain


def _norm_mod(x, gain, shift, scale):
    return _rms(x, gain) * (1.0 + scale) + shift


def _rope_perm():
    r = _iota((HD, HD), 0)
    c = _iota((HD, HD), 1)
    half = ROT // 2
    hit = ((c < half) & (r == c + half)) | ((c >= half) & (c < ROT) & (r == c - half))
    return jnp.where(hit, 1.0, 0.0).astype(F32)


def _rope(x, cos, sin, perm):
    return x * cos + _mmx(x, perm) * sin


def _attn_block(qs, kc, kp, vc, vp, qg, kg, sinks, cos_c, sin_c, cos_p, sin_p, has_prev):
    perm = _rope_perm()
    rows = GROUP * BLK
    qi = _iota((rows, 2 * BLK), 0) % BLK + BLK
    kj = _iota((rows, 2 * BLK), 1)
    dist = qi - kj
    valid = (dist >= 0) & (dist < BLK) & ((kj >= BLK) | has_prev)
    grp = _iota((rows, HEADS), 0) // BLK
    col = _iota((rows, HEADS), 1)
    outs = []
    for h in range(KV_HEADS):
        q = jnp.concatenate([_rope(_rms(qs[h * GROUP + g], qg), cos_c, sin_c, perm)
                             for g in range(GROUP)], axis=0)
        k = jnp.concatenate([_rope(_rms(kp[h], kg), cos_p, sin_p, perm),
                             _rope(_rms(kc[h], kg), cos_c, sin_c, perm)], axis=0)
        v = jnp.concatenate([vp[h], vc[h]], axis=0)
        s = _mm_nt(q, k) * (HD ** -0.5)
        s = jnp.where(valid, s, NEG)
        sink = jnp.sum(jnp.where(col == h * GROUP + grp, sinks, 0.0), axis=-1, keepdims=True)
        m = lax.stop_gradient(jnp.maximum(jnp.max(s, axis=-1, keepdims=True), sink))
        p = jnp.exp(s - m)
        probs = p / (jnp.sum(p, axis=-1, keepdims=True) + jnp.exp(sink - m))
        o = _mm(probs, v)
        outs += [o[g * BLK:(g + 1) * BLK] for g in range(GROUP)]
    return outs


def _dn_tail(ys, ba, alog, dtb):
    def l2(t):
        return t * lax.rsqrt(jnp.sum(t * t, axis=-1, keepdims=True) + EPS)
    s = [_silu(y) for y in ys]
    out = [l2(t) for t in s[:2 * DN_H]] + s[2 * DN_H:]
    lane = _iota(ba.shape, 1)
    beta = _sigmoid(ba)
    g = -jnp.exp(alog) * _softplus(ba + dtb)
    bg = jnp.where(lane < DN_H, beta, jnp.where(lane < 2 * DN_H, g, 0.0))
    return out, bg


def _delta_chunk(state, qs, ks, vs, bg):
    C = CHUNK
    r = _iota((C, C), 0)
    c = _iota((C, C), 1)
    incl = r >= c
    strict = r > c
    eye = jnp.where(r == c, 1.0, 0.0).astype(F32)
    tri = jnp.where(incl, 1.0, 0.0).astype(F32)
    gc_all = _mmx(tri, bg)
    sel = jnp.where(_iota((8, DN_D), 0) == _iota((8, DN_D), 1), 1.0, 0.0).astype(F32)
    gc_t = _mmx_nt(sel, gc_all)
    lane = _iota((C, DN_D), 1)
    row8 = _iota((8, C), 0)
    last = _iota((C, 1), 0) == C - 1
    new_state, outs = [], []
    for h in range(DN_H):
        beta = jnp.sum(jnp.where(lane == h, bg, 0.0), axis=-1, keepdims=True)
        gcol = jnp.sum(jnp.where(lane == DN_H + h, gc_all, 0.0), axis=-1, keepdims=True)
        grow = jnp.sum(jnp.where(row8 == DN_H + h, gc_t, 0.0), axis=0, keepdims=True)
        glast = jnp.sum(jnp.where(last, gcol, 0.0), axis=0, keepdims=True)
        decay = jnp.exp(jnp.where(incl, gcol - grow, NEG))
        q = qs[h] * (DN_D ** -0.5)
        k = ks[h]
        kb = k * beta
        lmat = jnp.where(strict, _mm_nt(kb, k) * decay, 0.0)
        pw = -lmat
        tinv = eye + pw
        for _ in range(5):
            pw = _mmx(pw, pw)
            tinv = tinv + _mmx(tinv, pw)
        egc = jnp.exp(gcol)
        u = _mm(tinv, vs[h] * beta)
        w = _mm(tinv, kb * egc)
        a_intra = _mm_nt(q, k) * decay
        q_dec = q * egc
        k_dec = k * jnp.exp(glast - gcol)
        v_new = u - _mm(w, state[h])
        outs.append(_mm(q_dec, state[h]) + _mm(a_intra, v_new))
        new_state.append(state[h] * jnp.exp(glast) + _mm_tn(k_dec, v_new))
    return new_state, outs


def _mix_tile(o_attn, o_raw, zs, ga, gb, x, gate1, dn_g, wb_a, wb_d, w_out, p_ya, p_yd, p_out):
    o_dn = jnp.concatenate([_rms(o_raw[h], dn_g) * _silu(zs[h]) for h in range(DN_H)], axis=-1)
    y_a = _mm(o_attn, wb_a) + p_ya
    y_d = _mm(o_dn, wb_d) + p_yd
    merged = _sigmoid(ga) * y_a + _sigmoid(gb) * y_d
    out = _mm(merged, w_out) + p_out
    return x + gate1 * out, o_dn, merged


def _mlp_tile(x1, gain, shift, scale, gate2, w_gu, w_dn, tgt, p_gu, p_yy):
    h2 = _norm_mod(x1, gain, shift, scale)
    gu = _mm(h2, w_gu) + p_gu
    act = _silu(gu[:, :FFN]) * gu[:, FFN:]
    yy = _mm(act, w_dn) + p_yy
    y = x1 + gate2 * yy
    err = y - tgt
    return 0.5 * jnp.sum(err * err) * (1.0 / D), (h2, act)


def _tok(bt, f):
    return pl.BlockSpec((None, bt, f), lambda b, i: (b, i, 0))


def _full(shape):
    return pl.BlockSpec(shape, lambda b, i: (0,) * len(shape))


def _per_batch(f):
    return pl.BlockSpec((None, 1, f), lambda b, i: (b, 0, 0))


def _sds(shape, dtype):
    return jax.ShapeDtypeStruct(shape, dtype)


def _acc(ref, val, first):
    @pl.when(first)
    def _():
        ref[...] = val

    @pl.when(jnp.logical_not(first))
    def _():
        ref[...] += val


def _in_proj(x, mod, norm1_g, w_in, bt):
    B, S, _ = x.shape

    def body(x_ref, mod_ref, g_ref, w_ref, q_ref, kv_ref, dn_ref, z_ref, ga_ref, gb_ref, ba_ref, h_ref):
        h = _norm_mod(x_ref[...], g_ref[...], mod_ref[:, 0:D], mod_ref[:, D:2 * D]).astype(BF)
        h_ref[...] = h

        def proj(c0, c1):
            return jnp.dot(h, w_ref[:, c0:c1], preferred_element_type=F32)
        q_ref[...] = proj(0, C_KV).astype(BF)
        kv_ref[...] = proj(C_KV, C_DN).astype(BF)
        dn_ref[...] = proj(C_DN, C_Z).astype(BF)
        z_ref[...] = proj(C_Z, C_GA).astype(BF)
        ga_ref[...] = proj(C_GA, C_GB).astype(BF)
        gb_ref[...] = proj(C_GB, C_BA).astype(BF)
        ba_ref[...] = proj(C_BA, IN_PAD)

    widths = (QW, 2 * KVW, CONVW, DNW, D, D)
    return pl.pallas_call(
        body, name="in_proj", grid=(B, S // bt),
        in_specs=[_tok(bt, D), _per_batch(6 * D), _full((1, D)), _full((D, IN_PAD))],
        out_specs=[_tok(bt, w) for w in widths] + [_tok(bt, 128), _tok(bt, D)],
        out_shape=[_sds((B, S, w), BF) for w in widths] + [_sds((B, S, 128), F32), _sds((B, S, D), BF)],
        compiler_params=_cparams(dimension_semantics=("parallel", "parallel")),
    )(x, mod, norm1_g, w_in)


def _prev_blk(bt, f):
    return pl.BlockSpec((None, bt, f), lambda b, i: (b, jnp.maximum(i - 1, 0), 0))


def _attn_load(q_ref, kvc_ref, kvp_ref):
    qs = [q_ref[:, h * HD:(h + 1) * HD].astype(F32) for h in range(HEADS)]
    kc = [kvc_ref[:, h * HD:(h + 1) * HD].astype(F32) for h in range(KV_HEADS)]
    kp = [kvp_ref[:, h * HD:(h + 1) * HD].astype(F32) for h in range(KV_HEADS)]
    vc = [kvc_ref[:, KVW + h * HD:KVW + (h + 1) * HD].astype(F32) for h in range(KV_HEADS)]
    vp = [kvp_ref[:, KVW + h * HD:KVW + (h + 1) * HD].astype(F32) for h in range(KV_HEADS)]
    return qs, kc, kp, vc, vp


def _attn_fwd(q, kv, rope_t, q_norm_g, k_norm_g, sinks):
    B, S, _ = q.shape

    def body(q_ref, kvc_ref, kvp_ref, rc_ref, rp_ref, qg_ref, kg_ref, sk_ref, o_ref):
        qs, kc, kp, vc, vp = _attn_load(q_ref, kvc_ref, kvp_ref)
        outs = _attn_block(qs, kc, kp, vc, vp, qg_ref[...], kg_ref[...], sk_ref[...],
                           rc_ref[:, 0:HD], rc_ref[:, HD:2 * HD], rp_ref[:, 0:HD], rp_ref[:, HD:2 * HD],
                           pl.program_id(1) > 0)
        for h in range(HEADS):
            o_ref[:, h * HD:(h + 1) * HD] = outs[h].astype(BF)

    return pl.pallas_call(
        body, name="attn_fwd", grid=(B, S // BLK),
        in_specs=[_tok(BLK, QW), _tok(BLK, 2 * KVW), _prev_blk(BLK, 2 * KVW), _tok(BLK, 2 * HD), _prev_blk(BLK, 2 * HD),
                  _full((1, HD)), _full((1, HD)), _full((1, HEADS))],
        out_specs=_tok(BLK, QW), out_shape=_sds((B, S, QW), BF),
        compiler_params=_cparams(dimension_semantics=("parallel", "parallel")),
    )(q, kv, kv, rope_t, rope_t, q_norm_g, k_norm_g, sinks)


def _conv_fwd_tile(xe_ref, x_ref, halo_ref, cw_ref, first, bt):
    halo = halo_ref[...].astype(F32)
    xe_ref[0:8, :] = jnp.where(first, 0.0, halo)
    xe_ref[8:bt + 8, :] = x_ref[...].astype(F32)
    y = cw_ref[0:1, :] * xe_ref[5:bt + 5, :]
    for j in range(1, CONV):
        y = y + cw_ref[j:j + 1, :] * xe_ref[5 + j:bt + 5 + j, :]
    return y


def _halo_spec(bt):
    return pl.BlockSpec((None, 8, CONVW), lambda b, i: (b, jnp.maximum(i * (bt // 8) - 1, 0), 0))


def _dn_prep(dn, ba, conv_w, alog, dtb, bt):
    B, S, _ = dn.shape

    def body(x_ref, halo_ref, ba_ref, cw_ref, al_ref, dt_ref, qkv_ref, bg_ref, xe_ref):
        y = _conv_fwd_tile(xe_ref, x_ref, halo_ref, cw_ref, pl.program_id(1) == 0, bt)
        ys = [y[:, j * DN_D:(j + 1) * DN_D] for j in range(3 * DN_H)]
        out, bg = _dn_tail(ys, ba_ref[...], al_ref[...], dt_ref[...])
        for j in range(3 * DN_H):
            qkv_ref[:, j * DN_D:(j + 1) * DN_D] = out[j]
        bg_ref[...] = bg

    return pl.pallas_call(
        body, name="dn_prep", grid=(B, S // bt),
        in_specs=[_tok(bt, CONVW), _halo_spec(bt), _tok(bt, 128), _full((CONV, CONVW)), _full((1, 128)), _full((1, 128))],
        out_specs=[_tok(bt, CONVW), _tok(bt, 128)],
        out_shape=[_sds((B, S, CONVW), F32), _sds((B, S, 128), F32)],
        scratch_shapes=[pltpu.VMEM((bt + 8, CONVW), F32)],
        compiler_params=_cparams(dimension_semantics=("parallel", "arbitrary")),
    )(dn, dn, ba, conv_w, alog, dtb)


def _dn_load(qkv_ref):
    qs = [qkv_ref[:, h * DN_D:(h + 1) * DN_D] for h in range(DN_H)]
    ks = [qkv_ref[:, DNW + h * DN_D:DNW + (h + 1) * DN_D] for h in range(DN_H)]
    vs = [qkv_ref[:, 2 * DNW + h * DN_D:2 * DNW + (h + 1) * DN_D] for h in range(DN_H)]
    return qs, ks, vs


def _delta_fwd(qkv, bg):
    B, S, _ = qkv.shape
    nc = S // CHUNK

    def body(qkv_ref, bg_ref, o_ref, st_ref, s_ref):
        @pl.when(pl.program_id(1) == 0)
        def _():
            s_ref[...] = jnp.zeros_like(s_ref)
        state = [s_ref[h] for h in range(DN_H)]
        for h in range(DN_H):
            st_ref[h] = state[h]
        qs, ks, vs = _dn_load(qkv_ref)
        new_state, outs = _delta_chunk(state, qs, ks, vs, bg_ref[...])
        for h in range(DN_H):
            s_ref[h] = new_state[h]
            o_ref[:, h * DN_D:(h + 1) * DN_D] = outs[h]

    return pl.pallas_call(
        body, name="delta_fwd", grid=(B, nc),
        in_specs=[_tok(CHUNK, CONVW), _tok(CHUNK, 128)],
        out_specs=[_tok(CHUNK, DNW), pl.BlockSpec((None, None, DN_H, DN_D, DN_D), lambda b, i: (b, i, 0, 0, 0))],
        out_shape=[_sds((B, S, DNW), F32), _sds((B, nc, DN_H, DN_D, DN_D), F32)],
        scratch_shapes=[pltpu.VMEM((DN_H, DN_D, DN_D), F32)],
        compiler_params=_cparams(dimension_semantics=("parallel", "arbitrary")),
    )(qkv, bg)


def _mix_load(oa_ref, or_ref, z_ref):
    o_raw = [or_ref[:, h * DN_D:(h + 1) * DN_D] for h in range(DN_H)]
    zs = [z_ref[:, h * DN_D:(h + 1) * DN_D].astype(F32) for h in range(DN_H)]
    return oa_ref[...].astype(F32), o_raw, zs


def _mix_fwd(o_attn, o_raw, z, ga, gb, x, mod, dn_g, w_branch, w_out, bt):
    B, S, _ = x.shape

    def body(oa_ref, or_ref, z_ref, ga_ref, gb_ref, x_ref, mod_ref, dg_ref, wb_ref, wo_ref, x1_ref, od_ref, mg_ref):
        oa, o_r, zs = _mix_load(oa_ref, or_ref, z_ref)
        x1, o_dn, merged = _mix_tile(oa, o_r, zs, ga_ref[...].astype(F32), gb_ref[...].astype(F32), x_ref[...],
                                     mod_ref[:, 2 * D:3 * D], dg_ref[...], wb_ref[0:QW, :], wb_ref[QW:2 * QW, :],
                                     wo_ref[...], 0.0, 0.0, 0.0)
        x1_ref[...] = x1
        od_ref[...] = o_dn.astype(BF)
        mg_ref[...] = merged.astype(BF)

    return pl.pallas_call(
        body, name="mix_fwd", grid=(B, S // bt),
        in_specs=[_tok(bt, QW), _tok(bt, DNW), _tok(bt, DNW), _tok(bt, D), _tok(bt, D), _tok(bt, D), _per_batch(6 * D),
                  _full((1, DN_D)), _full((D, D)), _full((D, D))],
        out_specs=[_tok(bt, D), _tok(bt, DNW), _tok(bt, D)],
        out_shape=[_sds((B, S, D), F32), _sds((B, S, DNW), BF), _sds((B, S, D), BF)],
        compiler_params=_cparams(dimension_semantics=("parallel", "parallel")),
    )(o_attn, o_raw, z, ga, gb, x, mod, dn_g, w_branch, w_out)


def _mlp(x1, tgt, mod, norm2_g, w_gu, w_dn, bt):
    B, S, _ = x1.shape

    def body(x1_ref, t_ref, mod_ref, g_ref, wgu_ref, wdn_ref,
             dx1_ref, h2_ref, act_ref, dgu_ref, dyy_ref, loss_ref, dmod_ref, dg_ref):
        w_gu_v, w_dn_v, t = wgu_ref[...], wdn_ref[...], t_ref[...]

        def f(x1, gain, shift, scale, gate2, p_gu, p_yy):
            return _mlp_tile(x1, gain, shift, scale, gate2, w_gu_v, w_dn_v, t, p_gu, p_yy)
        zero_gu = jnp.zeros((bt, 2 * FFN), F32)
        zero_yy = jnp.zeros((bt, D), F32)
        loss, vjp, (h2, act) = jax.vjp(f, x1_ref[...], g_ref[...], mod_ref[:, 3 * D:4 * D], mod_ref[:, 4 * D:5 * D],
                                       mod_ref[:, 5 * D:6 * D], zero_gu, zero_yy, has_aux=True)
        dx1, dgain, dshift, dscale, dgate2, dgu, dyy = vjp(jnp.ones((), F32))
        dx1_ref[...] = dx1
        h2_ref[...] = h2.astype(BF)
        act_ref[...] = act.astype(BF)
        dgu_ref[...] = dgu.astype(BF)
        dyy_ref[...] = dyy.astype(BF)
        first = pl.program_id(1) == 0
        _acc(loss_ref, jnp.reshape(loss, (1, 1)), first)
        _acc(dmod_ref, jnp.concatenate([dshift, dscale, dgate2], axis=-1), first)
        _acc(dg_ref, dgain, first)

    return pl.pallas_call(
        body, name="mlp", grid=(B, S // bt),
        in_specs=[_tok(bt, D), _tok(bt, D), _per_batch(6 * D), _full((1, D)), _full((D, 2 * FFN)), _full((FFN, D))],
        out_specs=[_tok(bt, D), _tok(bt, D), _tok(bt, FFN), _tok(bt, 2 * FFN), _tok(bt, D),
                   _per_batch(1), _per_batch(3 * D), _per_batch(D)],
        out_shape=[_sds((B, S, D), F32), _sds((B, S, D), BF), _sds((B, S, FFN), BF), _sds((B, S, 2 * FFN), BF),
                   _sds((B, S, D), BF), _sds((B, 1, 1), F32), _sds((B, 1, 3 * D), F32), _sds((B, 1, D), F32)],
        compiler_params=_cparams(dimension_semantics=("parallel", "arbitrary")),
    )(x1, tgt, mod, norm2_g, w_gu, w_dn)


def _mix_bwd(o_attn, o_raw, z, ga, gb, x, mod, dn_g, w_branch, w_out, dx1, bt):
    B, S, _ = x.shape

    def body(oa_ref, or_ref, z_ref, ga_ref, gb_ref, x_ref, mod_ref, dg_ref, wb_ref, wo_ref, dx1_ref,
             doa_ref, dor_ref, dz_ref, dga_ref, dgb_ref, dya_ref, dyd_ref, dout_ref, dgate_ref, ddg_ref):
        oa, o_r, zs = _mix_load(oa_ref, or_ref, z_ref)
        wb_a, wb_d, wo = wb_ref[0:QW, :], wb_ref[QW:2 * QW, :], wo_ref[...]

        def f(oa, o_r, zs, ga, gb, gate1, dn_g, p_ya, p_yd, p_out):
            return _mix_tile(oa, o_r, zs, ga, gb, x_ref[...], gate1, dn_g, wb_a, wb_d, wo, p_ya, p_yd, p_out)[0]
        zero = jnp.zeros((bt, D), F32)
        _, vjp = jax.vjp(f, oa, o_r, zs, ga_ref[...].astype(F32), gb_ref[...].astype(F32), mod_ref[:, 2 * D:3 * D],
                         dg_ref[...], zero, zero, zero)
        doa, dor, dzs, dga, dgb, dgate1, ddn_g, dya, dyd, dout = vjp(dx1_ref[...])
        doa_ref[...] = doa
        for h in range(DN_H):
            dor_ref[:, h * DN_D:(h + 1) * DN_D] = dor[h]
            dz_ref[:, h * DN_D:(h + 1) * DN_D] = dzs[h].astype(BF)
        dga_ref[...] = dga.astype(BF)
        dgb_ref[...] = dgb.astype(BF)
        dya_ref[...] = dya.astype(BF)
        dyd_ref[...] = dyd.astype(BF)
        dout_ref[...] = dout.astype(BF)
        first = pl.program_id(1) == 0
        _acc(dgate_ref, dgate1, first)
        _acc(ddg_ref, ddn_g, first)

    return pl.pallas_call(
        body, name="mix_bwd", grid=(B, S // bt),
        in_specs=[_tok(bt, QW), _tok(bt, DNW), _tok(bt, DNW), _tok(bt, D), _tok(bt, D), _tok(bt, D), _per_batch(6 * D),
                  _full((1, DN_D)), _full((D, D)), _full((D, D)), _tok(bt, D)],
        out_specs=[_tok(bt, QW), _tok(bt, DNW), _tok(bt, DNW), _tok(bt, D), _tok(bt, D), _tok(bt, D), _tok(bt, D), _tok(bt, D),
                   _per_batch(D), _per_batch(DN_D)],
        out_shape=[_sds((B, S, QW), F32), _sds((B, S, DNW), F32), _sds((B, S, DNW), BF), _sds((B, S, D), BF),
                   _sds((B, S, D), BF), _sds((B, S, D), BF), _sds((B, S, D), BF), _sds((B, S, D), BF),
                   _sds((B, 1, D), F32), _sds((B, 1, DN_D), F32)],
        compiler_params=_cparams(dimension_semantics=("parallel", "arbitrary")),
    )(o_attn, o_raw, z, ga, gb, x, mod, dn_g, w_branch, w_out, dx1)


def _delta_bwd(qkv, bg, states, d_o):
    B, S, _ = qkv.shape
    nc = S // CHUNK

    def rev(bt, f):
        return pl.BlockSpec((None, bt, f), lambda b, i: (b, nc - 1 - i, 0))

    def body(qkv_ref, bg_ref, st_ref, do_ref, dqkv_ref, dbg_ref, ds_ref):
        @pl.when(pl.program_id(1) == 0)
        def _():
            ds_ref[...] = jnp.zeros_like(ds_ref)
        state = [st_ref[h] for h in range(DN_H)]
        qs, ks, vs = _dn_load(qkv_ref)
        _, vjp = jax.vjp(_delta_chunk, state, qs, ks, vs, bg_ref[...])
        d_state = [ds_ref[h] for h in range(DN_H)]
        d_outs = [do_ref[:, h * DN_D:(h + 1) * DN_D] for h in range(DN_H)]
        dst, dq, dk, dv, dbg = vjp((d_state, d_outs))
        for h in range(DN_H):
            ds_ref[h] = dst[h]
            dqkv_ref[:, h * DN_D:(h + 1) * DN_D] = dq[h]
            dqkv_ref[:, DNW + h * DN_D:DNW + (h + 1) * DN_D] = dk[h]
            dqkv_ref[:, 2 * DNW + h * DN_D:2 * DNW + (h + 1) * DN_D] = dv[h]
        dbg_ref[...] = dbg

    return pl.pallas_call(
        body, name="delta_bwd", grid=(B, nc),
        in_specs=[rev(CHUNK, CONVW), rev(CHUNK, 128),
                  pl.BlockSpec((None, None, DN_H, DN_D, DN_D), lambda b, i: (b, nc - 1 - i, 0, 0, 0)), rev(CHUNK, DNW)],
        out_specs=[rev(CHUNK, CONVW), rev(CHUNK, 128)],
        out_shape=[_sds((B, S, CONVW), F32), _sds((B, S, 128), F32)],
        scratch_shapes=[pltpu.VMEM((DN_H, DN_D, DN_D), F32)],
        compiler_params=_cparams(dimension_semantics=("parallel", "arbitrary")),
    )(qkv, bg, states, d_o)


def _dn_prep_bwd(dn, ba, conv_w, alog, dtb, dqkv, dbg, bt):
    B, S, _ = dn.shape
    nt = S // bt

    def rev(f):
        return pl.BlockSpec((None, bt, f), lambda b, i: (b, nt - 1 - i, 0))

    halo = pl.BlockSpec((None, 8, CONVW), lambda b, i: (b, jnp.maximum((nt - 1 - i) * (bt // 8) - 1, 0), 0))

    def body(x_ref, halo_ref, ba_ref, cw_ref, al_ref, dt_ref, dqkv_ref, dbg_ref,
             dx_ref, dba_ref, dcw_ref, dal_ref, ddt_ref, xe_ref, dye_ref):
        i = pl.program_id(1)
        y = _conv_fwd_tile(xe_ref, x_ref, halo_ref, cw_ref, i == nt - 1, bt)
        ys = [y[:, j * DN_D:(j + 1) * DN_D] for j in range(3 * DN_H)]
        _, vjp = jax.vjp(_dn_tail, ys, ba_ref[...], al_ref[...], dt_ref[...])
        d_out = [dqkv_ref[:, j * DN_D:(j + 1) * DN_D] for j in range(3 * DN_H)]
        dys, dba, dal, ddt = vjp((d_out, dbg_ref[...]))
        @pl.when(i == 0)
        def _():
            dye_ref[bt:bt + 8, :] = jnp.zeros((8, CONVW), F32)

        @pl.when(i > 0)
        def _():
            dye_ref[bt:bt + 8, :] = dye_ref[0:8, :]
        for j in range(3 * DN_H):
            dye_ref[0:bt, j * DN_D:(j + 1) * DN_D] = dys[j]
        dx = cw_ref[0:1, :] * dye_ref[3:bt + 3, :]
        for j in range(1, CONV):
            dx = dx + cw_ref[j:j + 1, :] * dye_ref[3 - j:bt + 3 - j, :]
        dx_ref[...] = dx.astype(BF)
        dy = dye_ref[0:bt, :]
        dcw = jnp.concatenate([jnp.sum(dy * xe_ref[5 + j:bt + 5 + j, :], axis=0, keepdims=True) for j in range(CONV)], axis=0)
        first = (i == 0) & (pl.program_id(0) == 0)
        dba_ref[...] = dba
        _acc(dcw_ref, dcw, first)
        _acc(dal_ref, dal, first)
        _acc(ddt_ref, ddt, first)

    return pl.pallas_call(
        body, name="dn_prep_bwd", grid=(B, nt),
        in_specs=[rev(CONVW), halo, rev(128), _full((CONV, CONVW)), _full((1, 128)), _full((1, 128)), rev(CONVW), rev(128)],
        out_specs=[rev(CONVW), rev(128), _full((CONV, CONVW)), _full((1, 128)), _full((1, 128))],
        out_shape=[_sds((B, S, CONVW), BF), _sds((B, S, 128), F32), _sds((CONV, CONVW), F32), _sds((1, 128), F32),
                   _sds((1, 128), F32)],
        scratch_shapes=[pltpu.VMEM((bt + 8, CONVW), F32), pltpu.VMEM((bt + 8, CONVW), F32)],
        compiler_params=_cparams(dimension_semantics=("arbitrary", "arbitrary")),
    )(dn, dn, ba, conv_w, alog, dtb, dqkv, dbg)


def _attn_bwd(q, kv, rope_t, q_norm_g, k_norm_g, sinks, d_o):
    B, S, _ = q.shape
    nb = S // BLK

    def cur(f):
        return pl.BlockSpec((None, BLK, f), lambda b, i: (b, jnp.minimum(i, nb - 1), 0))

    def prev(f):
        return pl.BlockSpec((None, BLK, f), lambda b, i: (b, jnp.maximum(jnp.minimum(i, nb - 1) - 1, 0), 0))

    def out_prev(f):
        return pl.BlockSpec((None, BLK, f), lambda b, i: (b, jnp.maximum(i - 1, 0), 0))

    def body(q_ref, kvc_ref, kvp_ref, rc_ref, rp_ref, qg_ref, kg_ref, sk_ref, do_ref,
             dq_ref, dkv_ref, dqg_ref, dkg_ref, dsk_ref, carry_ref):
        n = pl.program_id(1)
        first = (n == 0) & (pl.program_id(0) == 0)

        @pl.when(n < nb)
        def _():
            qs, kc, kp, vc, vp = _attn_load(q_ref, kvc_ref, kvp_ref)
            cos_c, sin_c, cos_p, sin_p = rc_ref[:, 0:HD], rc_ref[:, HD:2 * HD], rp_ref[:, 0:HD], rp_ref[:, HD:2 * HD]

            def f(qs, kc, kp, vc, vp, qg, kg, sk):
                return _attn_block(qs, kc, kp, vc, vp, qg, kg, sk, cos_c, sin_c, cos_p, sin_p, n > 0)
            _, vjp = jax.vjp(f, qs, kc, kp, vc, vp, qg_ref[...], kg_ref[...], sk_ref[...])
            d_outs = [do_ref[:, h * HD:(h + 1) * HD] for h in range(HEADS)]
            dqs, dkc, dkp, dvc, dvp, dqg, dkg, dsk = vjp(d_outs)
            for h in range(HEADS):
                dq_ref[:, h * HD:(h + 1) * HD] = dqs[h].astype(BF)
            for h in range(KV_HEADS):
                ksl = slice(h * HD, (h + 1) * HD)
                vsl = slice(KVW + h * HD, KVW + (h + 1) * HD)
                dkv_ref[:, ksl] = (carry_ref[:, ksl] + dkp[h]).astype(BF)
                dkv_ref[:, vsl] = (carry_ref[:, vsl] + dvp[h]).astype(BF)
                carry_ref[:, ksl] = dkc[h]
                carry_ref[:, vsl] = dvc[h]
            _acc(dqg_ref, dqg, first)
            _acc(dkg_ref, dkg, first)
            _acc(dsk_ref, dsk, first)

        @pl.when(n == nb)
        def _():
            dkv_ref[...] = carry_ref[...].astype(BF)

    return pl.pallas_call(
        body, name="attn_bwd", grid=(B, nb + 1),
        in_specs=[cur(QW), cur(2 * KVW), prev(2 * KVW), cur(2 * HD), prev(2 * HD),
                  _full((1, HD)), _full((1, HD)), _full((1, HEADS)), cur(QW)],
        out_specs=[cur(QW), out_prev(2 * KVW), _full((1, HD)), _full((1, HD)), _full((1, HEADS))],
        out_shape=[_sds((B, S, QW), BF), _sds((B, S, 2 * KVW), BF), _sds((1, HD), F32), _sds((1, HD), F32),
                   _sds((1, HEADS), F32)],
        scratch_shapes=[pltpu.VMEM((BLK, 2 * KVW), F32)],
        compiler_params=_cparams(dimension_semantics=("arbitrary", "arbitrary")),
    )(q, kv, kv, rope_t, rope_t, q_norm_g, k_norm_g, sinks, d_o)


def _in_proj_bwd(x, mod, norm1_g, w_in, pieces, dba, dx1, bt):
    B, S, _ = x.shape
    widths = (QW, 2 * KVW, CONVW, DNW, D, D)

    def body(x_ref, mod_ref, g_ref, w_ref, dq_ref, dkv_ref, ddn_ref, dz_ref, dga_ref, dgb_ref, dba_ref, dx1_ref,
             gx_ref, dp_ref, dmod_ref, dg_ref):
        dp = jnp.concatenate([r[...] for r in (dq_ref, dkv_ref, ddn_ref, dz_ref, dga_ref, dgb_ref)]
                             + [dba_ref[...].astype(BF)], axis=-1)
        dp_ref[...] = dp
        dh = lax.dot_general(dp, w_ref[...], (((1,), (1,)), ((), ())), preferred_element_type=F32)
        _, vjp = jax.vjp(_norm_mod, x_ref[...], g_ref[...], mod_ref[:, 0:D], mod_ref[:, D:2 * D])
        dx, dgain, dshift, dscale = vjp(dh)
        gx_ref[...] = dx + dx1_ref[...]
        first = pl.program_id(1) == 0
        _acc(dmod_ref, jnp.concatenate([dshift, dscale], axis=-1), first)
        _acc(dg_ref, dgain, first)

    return pl.pallas_call(
        body, name="in_proj_bwd", grid=(B, S // bt),
        in_specs=[_tok(bt, D), _per_batch(6 * D), _full((1, D)), _full((D, IN_PAD))] + [_tok(bt, w) for w in widths]
        + [_tok(bt, 128), _tok(bt, D)],
        out_specs=[_tok(bt, D), _tok(bt, IN_PAD), _per_batch(2 * D), _per_batch(D)],
        out_shape=[_sds((B, S, D), F32), _sds((B, S, IN_PAD), BF), _sds((B, 1, 2 * D), F32), _sds((B, 1, D), F32)],
        compiler_params=_cparams(dimension_semantics=("parallel", "arbitrary")),
    )(x, mod, norm1_g, w_in, *pieces, dba, dx1)


def _matmul_tn(tag, a, b, bk, bn, bt):
    T, K = a.shape
    N = b.shape[1]
    nt = T // bt

    def body(a_ref, b_ref, o_ref, acc_ref):
        t = pl.program_id(2)

        @pl.when(t == 0)
        def _():
            acc_ref[...] = jnp.zeros_like(acc_ref)
        acc_ref[...] += lax.dot_general(a_ref[...], b_ref[...], (((0,), (0,)), ((), ())), preferred_element_type=F32)

        @pl.when(t == nt - 1)
        def _():
            o_ref[...] = acc_ref[...]

    return pl.pallas_call(
        body, name=f"grad_{tag}", grid=(K // bk, N // bn, nt),
        in_specs=[pl.BlockSpec((bt, bk), lambda i, j, t: (t, i)), pl.BlockSpec((bt, bn), lambda i, j, t: (t, j))],
        out_specs=pl.BlockSpec((bk, bn), lambda i, j, t: (i, j)),
        out_shape=_sds((K, N), F32),
        scratch_shapes=[pltpu.VMEM((bk, bn), F32)],
        compiler_params=_cparams(dimension_semantics=("parallel", "parallel", "arbitrary")),
    )(a, b)


def _rope_table(positions):
    inv_freq = THETA ** (-jnp.arange(0, ROT, 2, dtype=F32) / ROT)
    ang = positions.astype(F32)[..., None] * inv_freq
    cos, sin = jnp.cos(ang), jnp.sin(ang)
    ones = jnp.ones(ang.shape[:-1] + (HD - ROT,), F32)
    cos_full = jnp.concatenate([cos, cos, ones], axis=-1)
    sin_full = jnp.concatenate([-sin, sin, 0.0 * ones], axis=-1)
    return jnp.concatenate([cos_full, sin_full], axis=-1)


def _lane_pad(v, offset, width=128):
    return jnp.zeros((1, width), F32).at[0, offset:offset + v.shape[-1]].set(v.reshape(-1))


def _tile(S, want):
    return min(S, want)


def _local_step(x, mod, positions, tgt, norm1_g, w_in_pad, conv_w, q_norm_g, k_norm_g, sinks, a_log, dt_bias,
                dn_norm_g, w_branch, w_out, norm2_g, w_gu, w_dn):
    B, S, _ = x.shape
    T = B * S
    rope_t = _rope_table(positions)
    alog = _lane_pad(a_log, DN_H)
    dtb = _lane_pad(dt_bias, DN_H)
    conv2 = conv_w.reshape(CONV, CONVW)
    bt = _tile(S, 512)
    bt_mlp = _tile(S, 256)

    q, kv, dn, z, ga, gb, ba, h1 = _in_proj(x, mod, norm1_g, w_in_pad, bt)
    o_attn = _attn_fwd(q, kv, rope_t, q_norm_g, k_norm_g, sinks)
    dqkv, bg = _dn_prep(dn, ba, conv2, alog, dtb, bt)
    o_raw, states = _delta_fwd(dqkv, bg)
    x1, o_dn, merged = _mix_fwd(o_attn, o_raw, z, ga, gb, x, mod, dn_norm_g, w_branch, w_out, bt)
    dx1, h2, act, dgu, dyy, loss, dmod2, dnorm2 = _mlp(x1, tgt, mod, norm2_g, w_gu, w_dn, bt_mlp)

    def flat(t):
        return t.reshape(T, t.shape[-1])
    tn = functools.partial(_matmul_tn, bt=_tile(T, 512))
    g_w_dn = tn("w_down", flat(act), flat(dyy), bk=FFN, bn=D // 2)
    g_w_gu = tn("w_gate_up", flat(h2), flat(dgu), bk=D, bn=2 * FFN // 4)

    d_oa, d_or, dz, dga, dgb, dya, dyd, dout, dgate1, ddn_g = _mix_bwd(
        o_attn, o_raw, z, ga, gb, x, mod, dn_norm_g, w_branch, w_out, dx1, bt_mlp)
    g_w_out = tn("w_out", flat(merged), flat(dout), bk=D, bn=D)
    g_w_br = jnp.concatenate([tn("w_branch_attn", flat(o_attn), flat(dya), bk=QW, bn=D),
                              tn("w_branch_dn", flat(o_dn), flat(dyd), bk=DNW, bn=D)], axis=0)

    d_dqkv, dbg = _delta_bwd(dqkv, bg, states, d_or)
    d_dn, dba, dconv, dalog, ddtb = _dn_prep_bwd(dn, ba, conv2, alog, dtb, d_dqkv, dbg, bt)
    dq, dkv, dqg, dkg, dsk = _attn_bwd(q, kv, rope_t, q_norm_g, k_norm_g, sinks, d_oa)
    grad_x, dproj, dmod1, dnorm1 = _in_proj_bwd(x, mod, norm1_g, w_in_pad, (dq, dkv, d_dn, dz, dga, dgb), dba, dx1, bt)
    g_w_in = tn("w_in", flat(h1), flat(dproj), bk=D, bn=IN_PAD // 3)

    dmod = jnp.concatenate([dmod1, dgate1, dmod2], axis=-1)
    small = dict(norm1_g=jnp.sum(dnorm1, axis=0), norm2_g=jnp.sum(dnorm2, axis=0), q_norm_g=dqg, k_norm_g=dkg,
                 sinks=dsk, a_log=dalog[:, DN_H:2 * DN_H], dt_bias=ddtb[:, DN_H:2 * DN_H],
                 dn_norm_g=jnp.sum(ddn_g, axis=0), conv_w=dconv)
    return jnp.sum(loss), grad_x, dmod, small, (g_w_in, g_w_br, g_w_out, g_w_gu, g_w_dn)


def _me():
    return lax.axis_index("x"), lax.axis_index("y"), lax.axis_index("c")


def _flip(me, f):
    return (me[0] ^ ((f >> 2) & 1), me[1] ^ ((f >> 1) & 1), me[2] ^ (f & 1))


def _comm_call(name, ins, out_shapes, n_remote, n_local, plan):
    n_in = len(ins)
    n_out = len(out_shapes)

    def body(*refs):
        in_refs, out_refs = refs[:n_in], refs[n_in:n_in + n_out]
        send_sems, recv_sems, local_sems = refs[n_in + n_out:]
        remote, local = plan(_me(), in_refs, out_refs)
        assert len(remote) == n_remote and len(local) == n_local
        locs = [pltpu.make_async_copy(src, dst, local_sems.at[i]) for i, (src, dst) in enumerate(local)]
        cps = [pltpu.make_async_remote_copy(src_ref=src, dst_ref=dst, send_sem=send_sems.at[i], recv_sem=recv_sems.at[i],
                                            device_id=peer, device_id_type=pl.DeviceIdType.MESH)
               for i, (src, dst, peer) in enumerate(remote)]
        for cp in locs + cps:
            cp.start()
        for cp in cps:
            cp.wait_recv()
        for cp in cps:
            cp.wait_send()
        for cp in locs:
            cp.wait()

    any_spec = pl.BlockSpec(memory_space=pl.ANY)
    return pl.pallas_call(
        body, name=name, in_specs=[any_spec] * n_in, out_specs=[any_spec] * n_out, out_shape=out_shapes,
        scratch_shapes=[pltpu.SemaphoreType.DMA((n_remote,)), pltpu.SemaphoreType.DMA((n_remote,)),
                        pltpu.SemaphoreType.DMA((max(n_local, 1),))],
    )(*ins)


def _gather_devices(name, arrs):
    def plan(me, in_refs, out_refs):
        idx = 4 * me[0] + 2 * me[1] + me[2]
        remote = [(a, o.at[idx], _flip(me, f)) for a, o in zip(in_refs, out_refs) for f in range(1, N_DEV)]
        local = [(a, o.at[idx]) for a, o in zip(in_refs, out_refs)]
        return remote, local
    outs = [_sds((N_DEV,) + a.shape, a.dtype) for a in arrs]
    return _comm_call(name, arrs, outs, (N_DEV - 1) * len(arrs), len(arrs), plan)


def _gather_chips(name, arrs):
    def plan(me, in_refs, out_refs):
        chip = 2 * me[0] + me[1]
        remote = [(a, o.at[chip], _flip(me, f)) for a, o in zip(in_refs, out_refs) for f in (2, 4, 6)]
        local = [(a, o.at[chip]) for a, o in zip(in_refs, out_refs)]
        return remote, local
    outs = [_sds((N_CHIP,) + a.shape, a.dtype) for a in arrs]
    return _comm_call(name, arrs, outs, (N_CHIP - 1) * len(arrs), len(arrs), plan)


def _gather_weights(shards):
    def plan_ici(me, in_refs, out_refs):
        chip = 2 * me[0] + me[1]
        remote, local = [], []
        for a, o in zip(in_refs, out_refs):
            half = a.shape[0] // 2
            mine = a.at[pl.ds(me[2] * half, half)]
            remote += [(mine, o.at[chip], _flip(me, f)) for f in (2, 4, 6)]
            local.append((mine, o.at[chip]))
        return remote, local
    halves = _comm_call("weights_ici", shards, [_sds((N_CHIP, a.shape[0] // 2, a.shape[1]), a.dtype) for a in shards],
                        3 * len(shards), len(shards), plan_ici)

    def plan_d2d(me, in_refs, out_refs):
        remote = [(g, o.at[:, me[2]], _flip(me, 1)) for g, o in zip(in_refs, out_refs)]
        local = [(g, o.at[:, me[2]]) for g, o in zip(in_refs, out_refs)]
        return remote, local
    full = _comm_call("weights_d2d", halves, [_sds((N_CHIP, 2) + h.shape[1:], h.dtype) for h in halves],
                      len(halves), len(halves), plan_d2d)
    return [f.reshape(N_CHIP, 2 * f.shape[2], f.shape[3]) for f in full]


def _rows(r):
    for br in (512, 352, 256, 128, 64, 32, 16, 8):
        if r % br == 0:
            return br
    raise ValueError(r)


def _pair_add(g, recv, c):
    n, r, cols = g.shape
    half = r // 2
    br = _rows(half)
    nb = half // br

    def body(c_ref, g_ref, r_ref, o_ref):
        o_ref[...] = (g_ref[...] + r_ref[...]).astype(BF)

    return pl.pallas_call(
        body, name=f"pair_add_{r}x{cols}",
        grid_spec=pltpu.PrefetchScalarGridSpec(
            num_scalar_prefetch=1, grid=(n, nb),
            in_specs=[pl.BlockSpec((None, br, cols), lambda k, i, c_ref: (k, c_ref[0] * nb + i, 0)),
                      pl.BlockSpec((None, br, cols), lambda k, i, c_ref: (k, i, 0))],
            out_specs=pl.BlockSpec((None, br, cols), lambda k, i, c_ref: (k, i, 0))),
        out_shape=_sds((n, half, cols), BF),
        compiler_params=_cparams(dimension_semantics=("parallel", "parallel")),
    )(c, g, recv)


def _sum_chips(q):
    n, r, cols = q.shape
    br = _rows(r)

    def body(q_ref, o_ref):
        acc = q_ref[0].astype(F32)
        for k in range(1, n):
            acc = acc + q_ref[k].astype(F32)
        o_ref[...] = acc

    return pl.pallas_call(
        body, name=f"sum_chips_{r}x{cols}", grid=(r // br,),
        in_specs=[pl.BlockSpec((n, br, cols), lambda i: (0, i, 0))],
        out_specs=pl.BlockSpec((br, cols), lambda i: (i, 0)), out_shape=_sds((r, cols), F32),
        compiler_params=_cparams(dimension_semantics=("parallel",)),
    )(q)


def _reduce_grads(grads, c_arr):
    def plan_pair(me, in_refs, out_refs):
        remote = []
        for g, o in zip(in_refs, out_refs):
            half = g.shape[1] // 2
            remote += [(g.at[k, pl.ds((1 - me[2]) * half, half)], o.at[k], _flip(me, 1)) for k in range(N_CHIP)]
        return remote, []
    recv = _comm_call("grads_pair", grads, [_sds((N_CHIP, g.shape[1] // 2, g.shape[2]), F32) for g in grads],
                      N_CHIP * len(grads), 0, plan_pair)
    pair = [_pair_add(g, r, c_arr) for g, r in zip(grads, recv)]

    def plan_chips(me, in_refs, out_refs):
        chip = 2 * me[0] + me[1]
        remote, local = [], []
        for p, o in zip(in_refs, out_refs):
            for f in (2, 4, 6):
                peer = _flip(me, f)
                remote.append((p.at[2 * peer[0] + peer[1]], o.at[chip], peer))
            local.append((p.at[chip], o.at[chip]))
        return remote, local
    parts = _comm_call("grads_chips", pair, [_sds(p.shape, BF) for p in pair], 3 * len(pair), len(pair), plan_chips)
    halves = [_sum_chips(q) for q in parts]

    def plan_swap(me, in_refs, out_refs):
        remote = [(h, o.at[me[2]], _flip(me, 1)) for h, o in zip(in_refs, out_refs)]
        local = [(h, o.at[me[2]]) for h, o in zip(in_refs, out_refs)]
        return remote, local
    both = _comm_call("grads_swap", halves, [_sds((2,) + h.shape, F32) for h in halves], len(halves), len(halves), plan_swap)
    return [b.reshape(2 * b.shape[1], b.shape[2]) for b in both]


def _adamw_math(w, g, m, v):
    m = ADAM_B1 * m + (1.0 - ADAM_B1) * g
    v = ADAM_B2 * v + (1.0 - ADAM_B2) * (g * g)
    m_hat = m / (1.0 - ADAM_B1 ** ADAM_STEP)
    v_hat = v / (1.0 - ADAM_B2 ** ADAM_STEP)
    delta = -ADAM_LR * (m_hat / (jnp.sqrt(v_hat) + ADAM_EPS) + ADAM_WD * w)
    return delta, m, v


def _adamw(name, w, g, m, v):
    r, cols = w.shape
    br = _rows(r)
    if br * cols * 4 > (1 << 20) and br % 16 == 0:
        br //= 2

    def body(w_ref, g_ref, m_ref, v_ref, d_ref, mo_ref, vo_ref):
        d_ref[...], mo_ref[...], vo_ref[...] = _adamw_math(w_ref[...], g_ref[...], m_ref[...], v_ref[...])

    spec = pl.BlockSpec((br, cols), lambda i: (i, 0))
    return pl.pallas_call(
        body, name=f"adamw_{name}", grid=(r // br,), in_specs=[spec] * 4, out_specs=[spec] * 3,
        out_shape=[_sds((r, cols), F32)] * 3,
        compiler_params=_cparams(dimension_semantics=("parallel",)),
    )(w, g, m, v)


def _ada_fwd(c_all, ada_w, ada_b_cols):
    n = c_all.shape[0]

    def body(c_ref, w_ref, b_ref, o_ref):
        o_ref[...] = _mmx(_silu(c_ref[...]), w_ref[...]) + b_ref[...]

    return pl.pallas_call(
        body, name="ada_fwd", out_shape=_sds((n, ada_w.shape[1]), F32), compiler_params=_cparams(),
    )(c_all, ada_w, ada_b_cols)


def _ada_bwd(c_all, dmod_cols, w, m, v):
    n = c_all.shape[0]
    r, cols = w.shape
    br = 128

    def body(c_ref, d_ref, w_ref, m_ref, v_ref, g_ref, dl_ref, mo_ref, vo_ref):
        cond = _silu(c_ref[...])
        g = lax.dot_general(cond, d_ref[...], (((0,), (0,)), ((), ())), precision=lax.Precision.HIGHEST,
                            preferred_element_type=F32)
        g_ref[...] = g
        dl_ref[...], mo_ref[...], vo_ref[...] = _adamw_math(w_ref[...], g, m_ref[...], v_ref[...])

    spec = pl.BlockSpec((br, cols), lambda i: (i, 0))
    return pl.pallas_call(
        body, name="ada_bwd", grid=(r // br,),
        in_specs=[pl.BlockSpec((n, br), lambda i: (0, i)), pl.BlockSpec((n, cols), lambda i: (0, 0)), spec, spec, spec],
        out_specs=[spec] * 4, out_shape=[_sds((r, cols), F32)] * 4,
        compiler_params=_cparams(dimension_semantics=("parallel",)),
    )(c_all, dmod_cols, w, m, v)


def _sum_devices(parts):
    n, r, cols = parts.shape

    def body(p_ref, o_ref):
        acc = p_ref[0]
        for k in range(1, n):
            acc = acc + p_ref[k]
        o_ref[...] = acc

    return pl.pallas_call(body, name="sum_devices", out_shape=_sds((r, cols), F32), compiler_params=_cparams())(parts)


SMALL_ROWS = 16
_SMALL_SLOTS = dict(norm1_g=(0, 0, D), norm2_g=(1, 0, D), q_norm_g=(2, 0, HD), k_norm_g=(2, 128, HD), sinks=(2, 256, HEADS),
                    a_log=(2, 384, DN_H), dt_bias=(2, 512, DN_H), dn_norm_g=(2, 640, DN_D))
_CONV_ROW = 4
_ADA_B_ROW = 8


def _pack_small(vals, conv, ada_b):
    sheet = jnp.zeros((SMALL_ROWS, CONVW), F32)
    for name, (row, col, n) in _SMALL_SLOTS.items():
        sheet = sheet.at[row, col:col + n].set(vals[name].reshape(n))
    sheet = sheet.at[_CONV_ROW:_CONV_ROW + CONV, 0:conv.shape[1]].set(conv)
    return sheet.at[_ADA_B_ROW:_ADA_B_ROW + 4, :].set(ada_b.reshape(4, CONVW))


def _unpack_small(sheet, conv_cols):
    out = {name: sheet[row, col:col + n].reshape(1, n) for name, (row, col, n) in _SMALL_SLOTS.items()}
    out["conv_w"] = sheet[_CONV_ROW:_CONV_ROW + CONV, 0:conv_cols].reshape(1, CONV, 1, conv_cols)
    out["ada_b"] = sheet[_ADA_B_ROW:_ADA_B_ROW + 4, :].reshape(1, 6 * D)
    return out


def _pad_w_in(w):
    return jnp.concatenate([w[:, :C_Z], w[:, C_Z + 2 * DN_H:IN_WIDTH], w[:, C_Z:C_Z + 2 * DN_H],
                            jnp.zeros((w.shape[0], IN_PAD - IN_WIDTH), w.dtype)], axis=1)


def _unpad_w_in(g):
    return jnp.concatenate([g[:, :C_Z], g[:, C_BA:C_BA + 2 * DN_H], g[:, C_Z:C_BA]], axis=1)


def _cols_to_blocks(g):
    r = g.shape[0]
    return g.reshape(r, N_CHIP, g.shape[1] // N_CHIP).transpose(1, 0, 2)


def _blocks_to_cols(f):
    return f.transpose(1, 0, 2).reshape(f.shape[1], N_CHIP * f.shape[2])


def kernel(x, c, positions, ada_w, ada_b, norm1_g, w_in, conv_w, q_norm_g, k_norm_g, sinks, a_log, dt_bias, dn_norm_g, w_branch, w_out, norm2_g, w_gate_up, w_down, loss_target, m_ada_w, m_ada_b, m_norm1_g, m_w_in, m_conv_w, m_q_norm_g, m_k_norm_g, m_sinks, m_a_log, m_dt_bias, m_dn_norm_g, m_w_branch, m_w_out, m_norm2_g, m_w_gate_up, m_w_down, v_ada_w, v_ada_b, v_norm1_g, v_w_in, v_conv_w, v_q_norm_g, v_k_norm_g, v_sinks, v_a_log, v_dt_bias, v_dn_norm_g, v_w_branch, v_w_out, v_norm2_g, v_w_gate_up, v_w_down):
    ix, iy, ic = lax.axis_index("x"), lax.axis_index("y"), lax.axis_index("c")
    dev = 4 * ix + 2 * iy + ic
    chip = 2 * ix + iy
    n_seq = x.shape[0]
    conv_cols = conv_w.shape[-1]

    c_all, conv_all = _gather_devices("gather_cond", [c, conv_w.reshape(CONV, conv_cols)])
    c_all = c_all.reshape(N_DEV * n_seq, D)
    ada_cols = ada_w.shape[-1]
    ada_b_cols = lax.dynamic_slice(ada_b, (0, chip * ada_cols), (1, ada_cols))
    mod_cols = _ada_fwd(c_all, ada_w[0], ada_b_cols)
    (mod_blocks,) = _gather_chips("gather_mod", [mod_cols])
    mod_all = _blocks_to_cols(mod_blocks)
    mod = lax.dynamic_slice(mod_all, (dev * n_seq, 0), (n_seq, 6 * D)).reshape(n_seq, 1, 6 * D)
    conv_full = _blocks_to_cols(conv_all[0::2])

    shards = [w_in[0].astype(BF), w_branch[0].astype(BF), w_out[0].astype(BF), w_gate_up[0].astype(BF), w_down[0].astype(BF)]
    f_in, f_br, f_out, f_gu, f_dn = _gather_weights(shards)
    w_in_pad = _pad_w_in(_blocks_to_cols(f_in))
    w_gu_full = _blocks_to_cols(f_gu)
    w_br_full, w_out_full, w_dn_full = (f.reshape(N_CHIP * f.shape[1], f.shape[2]) for f in (f_br, f_out, f_dn))

    loss, grad_x, dmod, small, (g_in, g_br, g_out, g_gu, g_dn) = _local_step(
        x, mod, positions, loss_target, norm1_g, w_in_pad, conv_full.reshape(CONV, 1, CONVW), q_norm_g, k_norm_g, sinks,
        a_log, dt_bias, dn_norm_g, w_br_full, w_out_full, norm2_g, w_gu_full, w_dn_full)
    loss = lax.psum(loss, ("x", "y", "c"))

    blocks = [_cols_to_blocks(_unpad_w_in(g_in)), g_br.reshape(N_CHIP, -1, D), g_out.reshape(N_CHIP, -1, D),
              _cols_to_blocks(g_gu), g_dn.reshape(N_CHIP, -1, D)]
    r_in, r_br, r_out, r_gu, r_dn = _reduce_grads(blocks, ic.reshape(1).astype(jnp.int32))
    big = {}
    for name, w, g, m, v in (("w_in", w_in, r_in, m_w_in, v_w_in), ("w_branch", w_branch, r_br, m_w_branch, v_w_branch),
                             ("w_out", w_out, r_out, m_w_out, v_w_out), ("w_gate_up", w_gate_up, r_gu, m_w_gate_up, v_w_gate_up),
                             ("w_down", w_down, r_dn, m_w_down, v_w_down)):
        big[name] = (g,) + tuple(_adamw(name, w[0], g, m[0], v[0]))

    part = _pack_small(small, small["conv_w"], jnp.sum(dmod, axis=(0, 1)).reshape(1, 6 * D))
    dmod_all, parts = _gather_devices("gather_small", [dmod.reshape(n_seq, 6 * D), part])
    dmod_all = dmod_all.reshape(N_DEV * n_seq, 6 * D)
    g_small = _unpack_small(_sum_devices(parts), CONVW)
    g_conv = lax.dynamic_slice(g_small["conv_w"].reshape(CONV, CONVW), (0, chip * conv_cols), (CONV, conv_cols))
    g_small["conv_w"] = g_conv.reshape(1, CONV, 1, conv_cols)

    given = dict(norm1_g=(norm1_g, m_norm1_g, v_norm1_g), norm2_g=(norm2_g, m_norm2_g, v_norm2_g),
                 q_norm_g=(q_norm_g, m_q_norm_g, v_q_norm_g), k_norm_g=(k_norm_g, m_k_norm_g, v_k_norm_g),
                 sinks=(sinks, m_sinks, v_sinks), a_log=(a_log, m_a_log, v_a_log), dt_bias=(dt_bias, m_dt_bias, v_dt_bias),
                 dn_norm_g=(dn_norm_g, m_dn_norm_g, v_dn_norm_g))
    sheets = [_pack_small({k: t[j] for k, t in given.items()}, cw.reshape(CONV, conv_cols), ab)
              for j, (cw, ab) in enumerate(((conv_w, ada_b), (m_conv_w, m_ada_b), (v_conv_w, v_ada_b)))]
    g_local = _pack_small(g_small, g_conv, g_small["ada_b"])
    upd = [_unpack_small(s, conv_cols) for s in _adamw("small", sheets[0], g_local, sheets[1], sheets[2])]

    dmod_cols = lax.dynamic_slice(dmod_all, (0, chip * ada_cols), (N_DEV * n_seq, ada_cols))
    ada = _ada_bwd(c_all, dmod_cols, ada_w[0], m_ada_w[0], v_ada_w[0])

    names = ["ada_w", "ada_b", "norm1_g", "w_in", "conv_w", "q_norm_g", "k_norm_g", "sinks", "a_log", "dt_bias", "dn_norm_g",
             "w_branch", "w_out", "norm2_g", "w_gate_up", "w_down"]

    def leaf(name, j):
        if name == "ada_w":
            return ada[j][None]
        if name in big:
            return big[name][j][None]
        return g_small[name] if j == 0 else upd[j - 1][name]

    return (loss, grad_x) + tuple(leaf(n, j) for j in range(4) for n in names)
```

```python
import functools

import jax
import jax.numpy as jnp
import numpy as np
from jax import lax
from jax.experimental import pallas as pl
from jax.experimental.pallas import tpu as pltpu

F32 = jnp.float32
BF = jnp.bfloat16

D = 1024
HEADS = 8
KV_HEADS = 2
GROUP = 4
HD = 64
BLK = 128
ROT = 16
THETA = 500000.0
QW = 512
KVW = 128
DN_H = 4
DN_D = 128
CONV = 4
CHUNK = 64
DNW = 512
CONVW = 1536
FFN = 2816
EPS = 1e-6
IN_WIDTH = 4872
IN_PAD = 4992
C_KV = 512
C_DN = 768
C_Z = 2304
C_GA = 2816
C_GB = 3840
C_BA = 4864
NEG = -1e30
N_DEV = 8
N_CHIP = 4

ADAM_LR = 0.001
ADAM_B1 = 0.9
ADAM_B2 = 0.999
ADAM_EPS = 1e-08
ADAM_WD = 0.01
ADAM_STEP = 10

VMEM_LIMIT = 60 * 1024 * 1024


def _cparams(**kw):
    return pltpu.CompilerParams(vmem_limit_bytes=VMEM_LIMIT, **kw)


def _dg(a, b, ca, cb):
    return lax.dot_general(a.astype(BF), b.astype(BF), (((ca,), (cb,)), ((), ())),
                           preferred_element_type=F32)


@jax.custom_vjp
def _mm(a, b):
    return _dg(a, b, 1, 0)


def _mm_fwd(a, b):
    return _dg(a, b, 1, 0), (a, b)


def _mm_bwd(res, dy):
    a, b = res
    return _dg(dy, b, 1, 1).astype(a.dtype), _dg(a, dy, 0, 0).astype(b.dtype)


_mm.defvjp(_mm_fwd, _mm_bwd)


@jax.custom_vjp
def _mm_nt(a, b):
    return _dg(a, b, 1, 1)


def _mm_nt_fwd(a, b):
    return _dg(a, b, 1, 1), (a, b)


def _mm_nt_bwd(res, dy):
    a, b = res
    return _dg(dy, b, 1, 0).astype(a.dtype), _dg(dy, a, 0, 0).astype(b.dtype)


_mm_nt.defvjp(_mm_nt_fwd, _mm_nt_bwd)


@jax.custom_vjp
def _mm_tn(a, b):
    return _dg(a, b, 0, 0)


def _mm_tn_fwd(a, b):
    return _dg(a, b, 0, 0), (a, b)


def _mm_tn_bwd(res, dy):
    a, b = res
    return _dg(b, dy, 1, 1).astype(a.dtype), _dg(a, dy, 1, 0).astype(b.dtype)


_mm_tn.defvjp(_mm_tn_fwd, _mm_tn_bwd)


def _mmx(a, b):
    return jnp.dot(a, b, precision=lax.Precision.HIGHEST, preferred_element_type=F32)


def _mmx_nt(a, b):
    return lax.dot_general(a, b, (((1,), (1,)), ((), ())), precision=lax.Precision.HIGHEST,
                           preferred_element_type=F32)


def _iota(shape, dim):
    return lax.broadcasted_iota(jnp.int32, shape, dim)


def _sigmoid(x):
    return 1.0 / (1.0 + jnp.exp(-x))


def _silu(x):
    return x * _sigmoid(x)


def _softplus(x):
    return jnp.maximum(x, 0.0) + jnp.log(1.0 + jnp.exp(-jnp.abs(x)))


def _rms(x, gain):
    return x * lax.rsqrt(jnp.mean(x * x, axis=-1, keepdims=True) + EPS) * gain


def _norm_mod(x, gain, shift, scale):
    return _rms(x, gain) * (1.0 + scale) + shift


def _rope_perm():
    r = _iota((HD, HD), 0)
    c = _iota((HD, HD), 1)
    half = ROT // 2
    hit = ((c < half) & (r == c + half)) | ((c >= half) & (c < ROT) & (r == c - half))
    return jnp.where(hit, 1.0, 0.0).astype(F32)


def _rope(x, cos, sin, perm):
    return x * cos + _mmx(x, perm) * sin


def _attn_block(qs, kc, kp, vc, vp, qg, kg, sinks, cos_c, sin_c, cos_p, sin_p, has_prev):
    perm = _rope_perm()
    rows = GROUP * BLK
    qi = _iota((rows, 2 * BLK), 0) % BLK + BLK
    kj = _iota((rows, 2 * BLK), 1)
    dist = qi - kj
    valid = (dist >= 0) & (dist < BLK) & ((kj >= BLK) | has_prev)
    grp = _iota((rows, HEADS), 0) // BLK
    col = _iota((rows, HEADS), 1)
    outs = []
    for h in range(KV_HEADS):
        q = jnp.concatenate([_rope(_rms(qs[h * GROUP + g], qg), cos_c, sin_c, perm)
                             for g in range(GROUP)], axis=0)
        k = jnp.concatenate([_rope(_rms(kp[h], kg), cos_p, sin_p, perm),
                             _rope(_rms(kc[h], kg), cos_c, sin_c, perm)], axis=0)
        v = jnp.concatenate([vp[h], vc[h]], axis=0)
        s = _mm_nt(q, k) * (HD ** -0.5)
        s = jnp.where(valid, s, NEG)
        sink = jnp.sum(jnp.where(col == h * GROUP + grp, sinks, 0.0), axis=-1, keepdims=True)
        m = lax.stop_gradient(jnp.maximum(jnp.max(s, axis=-1, keepdims=True), sink))
        p = jnp.exp(s - m)
        probs = p / (jnp.sum(p, axis=-1, keepdims=True) + jnp.exp(sink - m))
        o = _mm(probs, v)
        outs += [o[g * BLK:(g + 1) * BLK] for g in range(GROUP)]
    return outs


def _dn_tail(ys, ba, alog, dtb):
    def l2(t):
        return t * lax.rsqrt(jnp.sum(t * t, axis=-1, keepdims=True) + EPS)
    s = [_silu(y) for y in ys]
    out = [l2(t) for t in s[:2 * DN_H]] + s[2 * DN_H:]
    lane = _iota(ba.shape, 1)
    beta = _sigmoid(ba)
    g = -jnp.exp(alog) * _softplus(ba + dtb)
    bg = jnp.where(lane < DN_H, beta, jnp.where(lane < 2 * DN_H, g, 0.0))
    return out, bg


def _dn_intra(qs, ks, vs, bg):
    C = CHUNK
    r = _iota((C, C), 0)
    c = _iota((C, C), 1)
    incl = r >= c
    strict = r > c
    eye = jnp.where(r == c, 1.0, 0.0).astype(F32)
    tri = jnp.where(incl, 1.0, 0.0).astype(F32)
    gc_all = _mmx(tri, bg)
    sel = jnp.where(_iota((8, DN_D), 0) == _iota((8, DN_D), 1), 1.0, 0.0).astype(F32)
    gc_t = _mmx_nt(sel, gc_all)
    lane = _iota((C, DN_D), 1)
    row8 = _iota((8, C), 0)
    last = _iota((C, 1), 0) == C - 1
    lane_row = _iota((1, DN_D), 1)
    us, ws, qds, kds, ats = [], [], [], [], []
    cd = jnp.zeros((1, DN_D), F32)
    for h in range(DN_H):
        beta = jnp.sum(jnp.where(lane == h, bg, 0.0), axis=-1, keepdims=True)
        gcol = jnp.sum(jnp.where(lane == DN_H + h, gc_all, 0.0), axis=-1, keepdims=True)
        grow = jnp.sum(jnp.where(row8 == DN_H + h, gc_t, 0.0), axis=0, keepdims=True)
        glast = jnp.sum(jnp.where(last, gcol, 0.0), axis=0, keepdims=True)
        decay = jnp.exp(jnp.where(incl, gcol - grow, NEG))
        q = qs[h] * (DN_D ** -0.5)
        k = ks[h]
        kb = k * beta
        lmat = jnp.where(strict, _mm_nt(kb, k) * decay, 0.0)
        pw = -lmat
        tinv = eye + pw
        for _ in range(5):
            pw = _mmx(pw, pw)
            tinv = tinv + _mmx(tinv, pw)
        egc = jnp.exp(gcol)
        us.append(_mm(tinv, vs[h] * beta))
        ws.append(_mm(tinv, kb * egc))
        ats.append(_mm_nt(q, k) * decay)
        qds.append(q * egc)
        kds.append(k * jnp.exp(glast - gcol))
        cd = cd + jnp.where(lane_row == h, jnp.exp(glast), 0.0)
    return us, ws, qds, kds, ats, cd


def _dn_rec(state, us, ws, qds, kds, ats, cd):
    lane_row = _iota((1, DN_D), 1)
    new_state, outs = [], []
    for h in range(DN_H):
        cdh = jnp.sum(jnp.where(lane_row == h, cd, 0.0), axis=-1, keepdims=True)
        v_new = us[h] - _mm(ws[h], state[h])
        outs.append(_mm(qds[h], state[h]) + _mm(ats[h], v_new))
        new_state.append(state[h] * cdh + _mm_tn(kds[h], v_new))
    return new_state, outs


def _mix_tile(o_attn, o_raw, zs, ga, gb, x, gate1, dn_g, wb_a, wb_d, w_out, p_ya, p_yd, p_out):
    o_dn = jnp.concatenate([_rms(o_raw[h], dn_g) * _silu(zs[h]) for h in range(DN_H)], axis=-1)
    y_a = _mm(o_attn, wb_a) + p_ya
    y_d = _mm(o_dn, wb_d) + p_yd
    merged = _sigmoid(ga) * y_a + _sigmoid(gb) * y_d
    out = _mm(merged, w_out) + p_out
    return x + gate1 * out, o_dn, merged


def _mlp_tile(x1, gain, shift, scale, gate2, w_gu, w_dn, tgt, p_gu, p_yy):
    h2 = _norm_mod(x1, gain, shift, scale)
    gu = _mm(h2, w_gu) + p_gu
    act = _silu(gu[:, :FFN]) * gu[:, FFN:]
    yy = _mm(act, w_dn) + p_yy
    y = x1 + gate2 * yy
    err = y - tgt
    return 0.5 * jnp.sum(err * err) * (1.0 / D), (h2, act)


def _tok(bt, f):
    return pl.BlockSpec((None, bt, f), lambda b, i: (b, i, 0))


def _full(shape):
    return pl.BlockSpec(shape, lambda b, i: (0,) * len(shape))


def _per_batch(f):
    return pl.BlockSpec((None, 1, f), lambda b, i: (b, 0, 0))


def _sds(shape, dtype):
    return jax.ShapeDtypeStruct(shape, dtype)


def _acc(ref, val, first):
    @pl.when(first)
    def _():
        ref[...] = val

    @pl.when(jnp.logical_not(first))
    def _():
        ref[...] += val


def _in_proj(x, mod, norm1_g, w_in, bt):
    B, S, _ = x.shape

    def body(x_ref, mod_ref, g_ref, w_ref, q_ref, kv_ref, dn_ref, z_ref, ga_ref, gb_ref, ba_ref, h_ref):
        h = _norm_mod(x_ref[...], g_ref[...], mod_ref[:, 0:D], mod_ref[:, D:2 * D]).astype(BF)
        h_ref[...] = h

        def proj(c0, c1):
            return jnp.dot(h, w_ref[:, c0:c1], preferred_element_type=F32)
        q_ref[...] = proj(0, C_KV).astype(BF)
        kv_ref[...] = proj(C_KV, C_DN).astype(BF)
        dn_ref[...] = proj(C_DN, C_Z).astype(BF)
        z_ref[...] = proj(C_Z, C_GA).astype(BF)
        ga_ref[...] = proj(C_GA, C_GB).astype(BF)
        gb_ref[...] = proj(C_GB, C_BA).astype(BF)
        ba_ref[...] = proj(C_BA, IN_PAD)

    widths = (QW, 2 * KVW, CONVW, DNW, D, D)
    return pl.pallas_call(
        body, name="in_proj", grid=(B, S // bt),
        in_specs=[_tok(bt, D), _per_batch(6 * D), _full((1, D)), _full((D, IN_PAD))],
        out_specs=[_tok(bt, w) for w in widths] + [_tok(bt, 128), _tok(bt, D)],
        out_shape=[_sds((B, S, w), BF) for w in widths] + [_sds((B, S, 128), F32), _sds((B, S, D), BF)],
        compiler_params=_cparams(dimension_semantics=("parallel", "parallel")),
    )(x, mod, norm1_g, w_in)


def _prev_blk(bt, f):
    return pl.BlockSpec((None, bt, f), lambda b, i: (b, jnp.maximum(i - 1, 0), 0))


def _attn_load(q_ref, kvc_ref, kvp_ref):
    qs = [q_ref[:, h * HD:(h + 1) * HD].astype(F32) for h in range(HEADS)]
    kc = [kvc_ref[:, h * HD:(h + 1) * HD].astype(F32) for h in range(KV_HEADS)]
    kp = [kvp_ref[:, h * HD:(h + 1) * HD].astype(F32) for h in range(KV_HEADS)]
    vc = [kvc_ref[:, KVW + h * HD:KVW + (h + 1) * HD].astype(F32) for h in range(KV_HEADS)]
    vp = [kvp_ref[:, KVW + h * HD:KVW + (h + 1) * HD].astype(F32) for h in range(KV_HEADS)]
    return qs, kc, kp, vc, vp


def _attn_fwd(q, kv, rope_t, q_norm_g, k_norm_g, sinks):
    B, S, _ = q.shape

    def body(q_ref, kvc_ref, kvp_ref, rc_ref, rp_ref, qg_ref, kg_ref, sk_ref, o_ref):
        qs, kc, kp, vc, vp = _attn_load(q_ref, kvc_ref, kvp_ref)
        outs = _attn_block(qs, kc, kp, vc, vp, qg_ref[...], kg_ref[...], sk_ref[...],
                           rc_ref[:, 0:HD], rc_ref[:, HD:2 * HD], rp_ref[:, 0:HD], rp_ref[:, HD:2 * HD],
                           pl.program_id(1) > 0)
        for h in range(HEADS):
            o_ref[:, h * HD:(h + 1) * HD] = outs[h].astype(BF)

    return pl.pallas_call(
        body, name="attn_fwd", grid=(B, S // BLK),
        in_specs=[_tok(BLK, QW), _tok(BLK, 2 * KVW), _prev_blk(BLK, 2 * KVW), _tok(BLK, 2 * HD), _prev_blk(BLK, 2 * HD),
                  _full((1, HD)), _full((1, HD)), _full((1, HEADS))],
        out_specs=_tok(BLK, QW), out_shape=_sds((B, S, QW), BF),
        compiler_params=_cparams(dimension_semantics=("parallel", "parallel")),
    )(q, kv, kv, rope_t, rope_t, q_norm_g, k_norm_g, sinks)


def _conv_fwd_tile(xe_ref, x_ref, halo_ref, cw_ref, first, bt):
    halo = halo_ref[...].astype(F32)
    xe_ref[0:8, :] = jnp.where(first, 0.0, halo)
    xe_ref[8:bt + 8, :] = x_ref[...].astype(F32)
    y = cw_ref[0:1, :] * xe_ref[5:bt + 5, :]
    for j in range(1, CONV):
        y = y + cw_ref[j:j + 1, :] * xe_ref[5 + j:bt + 5 + j, :]
    return y


def _halo_spec(bt):
    return pl.BlockSpec((None, 8, CONVW), lambda b, i: (b, jnp.maximum(i * (bt // 8) - 1, 0), 0))


def _dn_prep(dn, ba, conv_w, alog, dtb, bt):
    B, S, _ = dn.shape

    def body(x_ref, halo_ref, ba_ref, cw_ref, al_ref, dt_ref, qkv_ref, bg_ref, xe_ref):
        y = _conv_fwd_tile(xe_ref, x_ref, halo_ref, cw_ref, pl.program_id(1) == 0, bt)
        ys = [y[:, j * DN_D:(j + 1) * DN_D] for j in range(3 * DN_H)]
        out, bg = _dn_tail(ys, ba_ref[...], al_ref[...], dt_ref[...])
        for j in range(3 * DN_H):
            qkv_ref[:, j * DN_D:(j + 1) * DN_D] = out[j]
        bg_ref[...] = bg

    return pl.pallas_call(
        body, name="dn_prep", grid=(B, S // bt),
        in_specs=[_tok(bt, CONVW), _halo_spec(bt), _tok(bt, 128), _full((CONV, CONVW)), _full((1, 128)), _full((1, 128))],
        out_specs=[_tok(bt, CONVW), _tok(bt, 128)],
        out_shape=[_sds((B, S, CONVW), F32), _sds((B, S, 128), F32)],
        scratch_shapes=[pltpu.VMEM((bt + 8, CONVW), F32)],
        compiler_params=_cparams(dimension_semantics=("parallel", "arbitrary")),
    )(dn, dn, ba, conv_w, alog, dtb)


def _dn_load(qkv_ref):
    qs = [qkv_ref[:, h * DN_D:(h + 1) * DN_D] for h in range(DN_H)]
    ks = [qkv_ref[:, DNW + h * DN_D:DNW + (h + 1) * DN_D] for h in range(DN_H)]
    vs = [qkv_ref[:, 2 * DNW + h * DN_D:2 * DNW + (h + 1) * DN_D] for h in range(DN_H)]
    return qs, ks, vs


DN_GROUP = 4
AW = DN_H * CHUNK


def _dn_load_rows(qkv_ref, r0):
    rows = slice(r0, r0 + CHUNK)
    return ([qkv_ref[rows, h * DN_D:(h + 1) * DN_D] for h in range(DN_H)],
            [qkv_ref[rows, DNW + h * DN_D:DNW + (h + 1) * DN_D] for h in range(DN_H)],
            [qkv_ref[rows, 2 * DNW + h * DN_D:2 * DNW + (h + 1) * DN_D] for h in range(DN_H)])


def _cd_spec(n):
    return pl.BlockSpec((None, n, 1, DN_D), lambda b, i: (b, i, 0, 0))


def _dn_intra_fwd(qkv, bg):
    B, S, _ = qkv.shape
    nc = S // CHUNK
    G = min(DN_GROUP, nc)
    rows = G * CHUNK

    def body(qkv_ref, bg_ref, u_ref, w_ref, qd_ref, kd_ref, a_ref, cd_ref):
        for g in range(G):
            r0 = g * CHUNK
            qs, ks, vs = _dn_load_rows(qkv_ref, r0)
            us, ws, qds, kds, ats, cd = _dn_intra(qs, ks, vs, bg_ref[r0:r0 + CHUNK, :])
            for h in range(DN_H):
                cols = slice(h * DN_D, (h + 1) * DN_D)
                u_ref[r0:r0 + CHUNK, cols] = us[h]
                w_ref[r0:r0 + CHUNK, cols] = ws[h].astype(BF)
                qd_ref[r0:r0 + CHUNK, cols] = qds[h].astype(BF)
                kd_ref[r0:r0 + CHUNK, cols] = kds[h].astype(BF)
                a_ref[r0:r0 + CHUNK, h * CHUNK:(h + 1) * CHUNK] = ats[h].astype(BF)
            cd_ref[g] = cd

    return pl.pallas_call(
        body, name="dn_intra_fwd", grid=(B, nc // G),
        in_specs=[_tok(rows, CONVW), _tok(rows, 128)],
        out_specs=[_tok(rows, DNW)] * 4 + [_tok(rows, AW), _cd_spec(G)],
        out_shape=[_sds((B, S, DNW), F32)] + [_sds((B, S, DNW), BF)] * 3 + [_sds((B, S, AW), BF), _sds((B, nc, 1, DN_D), F32)],
        compiler_params=_cparams(dimension_semantics=("parallel", "parallel")),
    )(qkv, bg)


def _rec_load(u_ref, w_ref, qd_ref, kd_ref, a_ref):
    def heads(ref, width):
        return [ref[:, h * width:(h + 1) * width].astype(F32) for h in range(DN_H)]
    return heads(u_ref, DN_D), heads(w_ref, DN_D), heads(qd_ref, DN_D), heads(kd_ref, DN_D), heads(a_ref, CHUNK)


def _dn_rec_fwd(u, w, qd, kd, a, cd):
    B, S, _ = u.shape
    nc = S // CHUNK

    def body(u_ref, w_ref, qd_ref, kd_ref, a_ref, cd_ref, o_ref, st_ref, s_ref):
        @pl.when(pl.program_id(1) == 0)
        def _():
            s_ref[...] = jnp.zeros_like(s_ref)
        state = [s_ref[h] for h in range(DN_H)]
        for h in range(DN_H):
            st_ref[h] = state[h]
        new_state, outs = _dn_rec(state, *_rec_load(u_ref, w_ref, qd_ref, kd_ref, a_ref), cd_ref[0])
        for h in range(DN_H):
            s_ref[h] = new_state[h]
            o_ref[:, h * DN_D:(h + 1) * DN_D] = outs[h]

    return pl.pallas_call(
        body, name="dn_rec_fwd", grid=(B, nc),
        in_specs=[_tok(CHUNK, DNW)] * 4 + [_tok(CHUNK, AW), _cd_spec(1)],
        out_specs=[_tok(CHUNK, DNW), pl.BlockSpec((None, None, DN_H, DN_D, DN_D), lambda b, i: (b, i, 0, 0, 0))],
        out_shape=[_sds((B, S, DNW), F32), _sds((B, nc, DN_H, DN_D, DN_D), F32)],
        scratch_shapes=[pltpu.VMEM((DN_H, DN_D, DN_D), F32)],
        compiler_params=_cparams(dimension_semantics=("parallel", "arbitrary")),
    )(u, w, qd, kd, a, cd)


def _mix_load(oa_ref, or_ref, z_ref):
    o_raw = [or_ref[:, h * DN_D:(h + 1) * DN_D] for h in range(DN_H)]
    zs = [z_ref[:, h * DN_D:(h + 1) * DN_D].astype(F32) for h in range(DN_H)]
    return oa_ref[...].astype(F32), o_raw, zs


def _mix_fwd(o_attn, o_raw, z, ga, gb, x, mod, dn_g, w_branch, w_out, bt):
    B, S, _ = x.shape

    def body(oa_ref, or_ref, z_ref, ga_ref, gb_ref, x_ref, mod_ref, dg_ref, wb_ref, wo_ref, x1_ref, od_ref, mg_ref):
        oa, o_r, zs = _mix_load(oa_ref, or_ref, z_ref)
        x1, o_dn, merged = _mix_tile(oa, o_r, zs, ga_ref[...].astype(F32), gb_ref[...].astype(F32), x_ref[...],
                                     mod_ref[:, 2 * D:3 * D], dg_ref[...], wb_ref[0:QW, :], wb_ref[QW:2 * QW, :],
                                     wo_ref[...], 0.0, 0.0, 0.0)
        x1_ref[...] = x1
        od_ref[...] = o_dn.astype(BF)
        mg_ref[...] = merged.astype(BF)

    return pl.pallas_call(
        body, name="mix_fwd", grid=(B, S // bt),
        in_specs=[_tok(bt, QW), _tok(bt, DNW), _tok(bt, DNW), _tok(bt, D), _tok(bt, D), _tok(bt, D), _per_batch(6 * D),
                  _full((1, DN_D)), _full((D, D)), _full((D, D))],
        out_specs=[_tok(bt, D), _tok(bt, DNW), _tok(bt, D)],
        out_shape=[_sds((B, S, D), F32), _sds((B, S, DNW), BF), _sds((B, S, D), BF)],
        compiler_params=_cparams(dimension_semantics=("parallel", "parallel")),
    )(o_attn, o_raw, z, ga, gb, x, mod, dn_g, w_branch, w_out)


def _mlp(x1, tgt, mod, norm2_g, w_gu, w_dn, bt):
    B, S, _ = x1.shape

    def body(x1_ref, t_ref, mod_ref, g_ref, wgu_ref, wdn_ref,
             dx1_ref, h2_ref, act_ref, dgu_ref, dyy_ref, loss_ref, dmod_ref, dg_ref):
        w_gu_v, w_dn_v, t = wgu_ref[...], wdn_ref[...], t_ref[...]

        def f(x1, gain, shift, scale, gate2, p_gu, p_yy):
            return _mlp_tile(x1, gain, shift, scale, gate2, w_gu_v, w_dn_v, t, p_gu, p_yy)
        zero_gu = jnp.zeros((bt, 2 * FFN), F32)
        zero_yy = jnp.zeros((bt, D), F32)
        loss, vjp, (h2, act) = jax.vjp(f, x1_ref[...], g_ref[...], mod_ref[:, 3 * D:4 * D], mod_ref[:, 4 * D:5 * D],
                                       mod_ref[:, 5 * D:6 * D], zero_gu, zero_yy, has_aux=True)
        dx1, dgain, dshift, dscale, dgate2, dgu, dyy = vjp(jnp.ones((), F32))
        dx1_ref[...] = dx1
        h2_ref[...] = h2.astype(BF)
        act_ref[...] = act.astype(BF)
        dgu_ref[...] = dgu.astype(BF)
        dyy_ref[...] = dyy.astype(BF)
        first = pl.program_id(1) == 0
        _acc(loss_ref, jnp.reshape(loss, (1, 1)), first)
        _acc(dmod_ref, jnp.concatenate([dshift, dscale, dgate2], axis=-1), first)
        _acc(dg_ref, dgain, first)

    return pl.pallas_call(
        body, name="mlp", grid=(B, S // bt),
        in_specs=[_tok(bt, D), _tok(bt, D), _per_batch(6 * D), _full((1, D)), _full((D, 2 * FFN)), _full((FFN, D))],
        out_specs=[_tok(bt, D), _tok(bt, D), _tok(bt, FFN), _tok(bt, 2 * FFN), _tok(bt, D),
                   _per_batch(1), _per_batch(3 * D), _per_batch(D)],
        out_shape=[_sds((B, S, D), F32), _sds((B, S, D), BF), _sds((B, S, FFN), BF), _sds((B, S, 2 * FFN), BF),
                   _sds((B, S, D), BF), _sds((B, 1, 1), F32), _sds((B, 1, 3 * D), F32), _sds((B, 1, D), F32)],
        compiler_params=_cparams(dimension_semantics=("parallel", "arbitrary")),
    )(x1, tgt, mod, norm2_g, w_gu, w_dn)


def _mix_bwd(o_attn, o_raw, z, ga, gb, x, mod, dn_g, w_branch, w_out, dx1, bt):
    B, S, _ = x.shape

    def body(oa_ref, or_ref, z_ref, ga_ref, gb_ref, x_ref, mod_ref, dg_ref, wb_ref, wo_ref, dx1_ref,
             doa_ref, dor_ref, dz_ref, dga_ref, dgb_ref, dya_ref, dyd_ref, dout_ref, dgate_ref, ddg_ref):
        oa, o_r, zs = _mix_load(oa_ref, or_ref, z_ref)
        wb_a, wb_d, wo = wb_ref[0:QW, :], wb_ref[QW:2 * QW, :], wo_ref[...]

        def f(oa, o_r, zs, ga, gb, gate1, dn_g, p_ya, p_yd, p_out):
            return _mix_tile(oa, o_r, zs, ga, gb, x_ref[...], gate1, dn_g, wb_a, wb_d, wo, p_ya, p_yd, p_out)[0]
        zero = jnp.zeros((bt, D), F32)
        _, vjp = jax.vjp(f, oa, o_r, zs, ga_ref[...].astype(F32), gb_ref[...].astype(F32), mod_ref[:, 2 * D:3 * D],
                         dg_ref[...], zero, zero, zero)
        doa, dor, dzs, dga, dgb, dgate1, ddn_g, dya, dyd, dout = vjp(dx1_ref[...])
        doa_ref[...] = doa
        for h in range(DN_H):
            dor_ref[:, h * DN_D:(h + 1) * DN_D] = dor[h]
            dz_ref[:, h * DN_D:(h + 1) * DN_D] = dzs[h].astype(BF)
        dga_ref[...] = dga.astype(BF)
        dgb_ref[...] = dgb.astype(BF)
        dya_ref[...] = dya.astype(BF)
        dyd_ref[...] = dyd.astype(BF)
        dout_ref[...] = dout.astype(BF)
        first = pl.program_id(1) == 0
        _acc(dgate_ref, dgate1, first)
        _acc(ddg_ref, ddn_g, first)

    return pl.pallas_call(
        body, name="mix_bwd", grid=(B, S // bt),
        in_specs=[_tok(bt, QW), _tok(bt, DNW), _tok(bt, DNW), _tok(bt, D), _tok(bt, D), _tok(bt, D), _per_batch(6 * D),
                  _full((1, DN_D)), _full((D, D)), _full((D, D)), _tok(bt, D)],
        out_specs=[_tok(bt, QW), _tok(bt, DNW), _tok(bt, DNW), _tok(bt, D), _tok(bt, D), _tok(bt, D), _tok(bt, D), _tok(bt, D),
                   _per_batch(D), _per_batch(DN_D)],
        out_shape=[_sds((B, S, QW), F32), _sds((B, S, DNW), F32), _sds((B, S, DNW), BF), _sds((B, S, D), BF),
                   _sds((B, S, D), BF), _sds((B, S, D), BF), _sds((B, S, D), BF), _sds((B, S, D), BF),
                   _sds((B, 1, D), F32), _sds((B, 1, DN_D), F32)],
        compiler_params=_cparams(dimension_semantics=("parallel", "arbitrary")),
    )(o_attn, o_raw, z, ga, gb, x, mod, dn_g, w_branch, w_out, dx1)


def _dn_rec_bwd(u, w, qd, kd, a, cd, states, d_o):
    B, S, _ = u.shape
    nc = S // CHUNK

    def rev(f):
        return pl.BlockSpec((None, CHUNK, f), lambda b, i: (b, nc - 1 - i, 0))

    cd_rev = pl.BlockSpec((None, 1, 1, DN_D), lambda b, i: (b, nc - 1 - i, 0, 0))

    def body(u_ref, w_ref, qd_ref, kd_ref, a_ref, cd_ref, st_ref, do_ref,
             du_ref, dw_ref, dqd_ref, dkd_ref, da_ref, dcd_ref, ds_ref):
        @pl.when(pl.program_id(1) == 0)
        def _():
            ds_ref[...] = jnp.zeros_like(ds_ref)
        state = [st_ref[h] for h in range(DN_H)]
        _, vjp = jax.vjp(_dn_rec, state, *_rec_load(u_ref, w_ref, qd_ref, kd_ref, a_ref), cd_ref[0])
        d_state = [ds_ref[h] for h in range(DN_H)]
        d_outs = [do_ref[:, h * DN_D:(h + 1) * DN_D] for h in range(DN_H)]
        dst, du, dw, dqd, dkd, da, dcd = vjp((d_state, d_outs))
        for h in range(DN_H):
            cols = slice(h * DN_D, (h + 1) * DN_D)
            ds_ref[h] = dst[h]
            du_ref[:, cols] = du[h]
            dw_ref[:, cols] = dw[h]
            dqd_ref[:, cols] = dqd[h]
            dkd_ref[:, cols] = dkd[h]
            da_ref[:, h * CHUNK:(h + 1) * CHUNK] = da[h]
        dcd_ref[0] = dcd

    return pl.pallas_call(
        body, name="dn_rec_bwd", grid=(B, nc),
        in_specs=[rev(DNW)] * 4 + [rev(AW), cd_rev,
                  pl.BlockSpec((None, None, DN_H, DN_D, DN_D), lambda b, i: (b, nc - 1 - i, 0, 0, 0)), rev(DNW)],
        out_specs=[rev(DNW)] * 4 + [rev(AW), cd_rev],
        out_shape=[_sds((B, S, DNW), F32)] * 4 + [_sds((B, S, AW), F32), _sds((B, nc, 1, DN_D), F32)],
        scratch_shapes=[pltpu.VMEM((DN_H, DN_D, DN_D), F32)],
        compiler_params=_cparams(dimension_semantics=("parallel", "arbitrary")),
    )(u, w, qd, kd, a, cd, states, d_o)


def _dn_intra_bwd(qkv, bg, du, dw, dqd, dkd, da, dcd):
    B, S, _ = qkv.shape
    nc = S // CHUNK
    G = min(DN_GROUP, nc)
    rows = G * CHUNK

    def body(qkv_ref, bg_ref, du_ref, dw_ref, dqd_ref, dkd_ref, da_ref, dcd_ref, dqkv_ref, dbg_ref):
        for g in range(G):
            r0 = g * CHUNK
            rs = slice(r0, r0 + CHUNK)
            qs, ks, vs = _dn_load_rows(qkv_ref, r0)
            _, vjp = jax.vjp(_dn_intra, qs, ks, vs, bg_ref[rs, :])

            def heads(ref, width):
                return [ref[rs, h * width:(h + 1) * width] for h in range(DN_H)]
            dq, dk, dv, dbg = vjp((heads(du_ref, DN_D), heads(dw_ref, DN_D), heads(dqd_ref, DN_D), heads(dkd_ref, DN_D),
                                   heads(da_ref, CHUNK), dcd_ref[g]))
            for h in range(DN_H):
                dqkv_ref[rs, h * DN_D:(h + 1) * DN_D] = dq[h]
                dqkv_ref[rs, DNW + h * DN_D:DNW + (h + 1) * DN_D] = dk[h]
                dqkv_ref[rs, 2 * DNW + h * DN_D:2 * DNW + (h + 1) * DN_D] = dv[h]
            dbg_ref[rs, :] = dbg

    return pl.pallas_call(
        body, name="dn_intra_bwd", grid=(B, nc // G),
        in_specs=[_tok(rows, CONVW), _tok(rows, 128)] + [_tok(rows, DNW)] * 4 + [_tok(rows, AW), _cd_spec(G)],
        out_specs=[_tok(rows, CONVW), _tok(rows, 128)],
        out_shape=[_sds((B, S, CONVW), F32), _sds((B, S, 128), F32)],
        compiler_params=_cparams(dimension_semantics=("parallel", "parallel")),
    )(qkv, bg, du, dw, dqd, dkd, da, dcd)


def _dn_prep_bwd(dn, ba, conv_w, alog, dtb, dqkv, dbg, bt):
    B, S, _ = dn.shape
    nt = S // bt

    def rev(f):
        return pl.BlockSpec((None, bt, f), lambda b, i: (b, nt - 1 - i, 0))

    halo = pl.BlockSpec((None, 8, CONVW), lambda b, i: (b, jnp.maximum((nt - 1 - i) * (bt // 8) - 1, 0), 0))

    def body(x_ref, halo_ref, ba_ref, cw_ref, al_ref, dt_ref, dqkv_ref, dbg_ref,
             dx_ref, dba_ref, dcw_ref, dal_ref, ddt_ref, xe_ref, dye_ref):
        i = pl.program_id(1)
        y = _conv_fwd_tile(xe_ref, x_ref, halo_ref, cw_ref, i == nt - 1, bt)
        ys = [y[:, j * DN_D:(j + 1) * DN_D] for j in range(3 * DN_H)]
        _, vjp = jax.vjp(_dn_tail, ys, ba_ref[...], al_ref[...], dt_ref[...])
        d_out = [dqkv_ref[:, j * DN_D:(j + 1) * DN_D] for j in range(3 * DN_H)]
        dys, dba, dal, ddt = vjp((d_out, dbg_ref[...]))
        @pl.when(i == 0)
        def _():
            dye_ref[bt:bt + 8, :] = jnp.zeros((8, CONVW), F32)

        @pl.when(i > 0)
        def _():
            dye_ref[bt:bt + 8, :] = dye_ref[0:8, :]
        for j in range(3 * DN_H):
            dye_ref[0:bt, j * DN_D:(j + 1) * DN_D] = dys[j]
        dx = cw_ref[0:1, :] * dye_ref[3:bt + 3, :]
        for j in range(1, CONV):
            dx = dx + cw_ref[j:j + 1, :] * dye_ref[3 - j:bt + 3 - j, :]
        dx_ref[...] = dx.astype(BF)
        dy = dye_ref[0:bt, :]
        dcw = jnp.concatenate([jnp.sum(dy * xe_ref[5 + j:bt + 5 + j, :], axis=0, keepdims=True) for j in range(CONV)], axis=0)
        first = (i == 0) & (pl.program_id(0) == 0)
        dba_ref[...] = dba
        _acc(dcw_ref, dcw, first)
        _acc(dal_ref, dal, first)
        _acc(ddt_ref, ddt, first)

    return pl.pallas_call(
        body, name="dn_prep_bwd", grid=(B, nt),
        in_specs=[rev(CONVW), halo, rev(128), _full((CONV, CONVW)), _full((1, 128)), _full((1, 128)), rev(CONVW), rev(128)],
        out_specs=[rev(CONVW), rev(128), _full((CONV, CONVW)), _full((1, 128)), _full((1, 128))],
        out_shape=[_sds((B, S, CONVW), BF), _sds((B, S, 128), F32), _sds((CONV, CONVW), F32), _sds((1, 128), F32),
                   _sds((1, 128), F32)],
        scratch_shapes=[pltpu.VMEM((bt + 8, CONVW), F32), pltpu.VMEM((bt + 8, CONVW), F32)],
        compiler_params=_cparams(dimension_semantics=("arbitrary", "arbitrary")),
    )(dn, dn, ba, conv_w, alog, dtb, dqkv, dbg)


def _attn_bwd(q, kv, rope_t, q_norm_g, k_norm_g, sinks, d_o):
    B, S, _ = q.shape
    nb = S // BLK

    def cur(f):
        return pl.BlockSpec((None, BLK, f), lambda b, i: (b, jnp.minimum(i, nb - 1), 0))

    def prev(f):
        return pl.BlockSpec((None, BLK, f), lambda b, i: (b, jnp.maximum(jnp.minimum(i, nb - 1) - 1, 0), 0))

    def out_prev(f):
        return pl.BlockSpec((None, BLK, f), lambda b, i: (b, jnp.maximum(i - 1, 0), 0))

    def body(q_ref, kvc_ref, kvp_ref, rc_ref, rp_ref, qg_ref, kg_ref, sk_ref, do_ref,
             dq_ref, dkv_ref, dqg_ref, dkg_ref, dsk_ref, carry_ref):
        n = pl.program_id(1)
        first = (n == 0) & (pl.program_id(0) == 0)

        @pl.when(n < nb)
        def _():
            qs, kc, kp, vc, vp = _attn_load(q_ref, kvc_ref, kvp_ref)
            cos_c, sin_c, cos_p, sin_p = rc_ref[:, 0:HD], rc_ref[:, HD:2 * HD], rp_ref[:, 0:HD], rp_ref[:, HD:2 * HD]

            def f(qs, kc, kp, vc, vp, qg, kg, sk):
                return _attn_block(qs, kc, kp, vc, vp, qg, kg, sk, cos_c, sin_c, cos_p, sin_p, n > 0)
            _, vjp = jax.vjp(f, qs, kc, kp, vc, vp, qg_ref[...], kg_ref[...], sk_ref[...])
            d_outs = [do_ref[:, h * HD:(h + 1) * HD] for h in range(HEADS)]
            dqs, dkc, dkp, dvc, dvp, dqg, dkg, dsk = vjp(d_outs)
            for h in range(HEADS):
                dq_ref[:, h * HD:(h + 1) * HD] = dqs[h].astype(BF)
            for h in range(KV_HEADS):
                ksl = slice(h * HD, (h + 1) * HD)
                vsl = slice(KVW + h * HD, KVW + (h + 1) * HD)
                dkv_ref[:, ksl] = (carry_ref[:, ksl] + dkp[h]).astype(BF)
                dkv_ref[:, vsl] = (carry_ref[:, vsl] + dvp[h]).astype(BF)
                carry_ref[:, ksl] = dkc[h]
                carry_ref[:, vsl] = dvc[h]
            _acc(dqg_ref, dqg, first)
            _acc(dkg_ref, dkg, first)
            _acc(dsk_ref, dsk, first)

        @pl.when(n == nb)
        def _():
            dkv_ref[...] = carry_ref[...].astype(BF)

    return pl.pallas_call(
        body, name="attn_bwd", grid=(B, nb + 1),
        in_specs=[cur(QW), cur(2 * KVW), prev(2 * KVW), cur(2 * HD), prev(2 * HD),
                  _full((1, HD)), _full((1, HD)), _full((1, HEADS)), cur(QW)],
        out_specs=[cur(QW), out_prev(2 * KVW), _full((1, HD)), _full((1, HD)), _full((1, HEADS))],
        out_shape=[_sds((B, S, QW), BF), _sds((B, S, 2 * KVW), BF), _sds((1, HD), F32), _sds((1, HD), F32),
                   _sds((1, HEADS), F32)],
        scratch_shapes=[pltpu.VMEM((BLK, 2 * KVW), F32)],
        compiler_params=_cparams(dimension_semantics=("arbitrary", "arbitrary")),
    )(q, kv, kv, rope_t, rope_t, q_norm_g, k_norm_g, sinks, d_o)


def _in_proj_bwd(x, mod, norm1_g, w_in, pieces, dba, dx1, bt):
    B, S, _ = x.shape
    widths = (QW, 2 * KVW, CONVW, DNW, D, D)

    def body(x_ref, mod_ref, g_ref, w_ref, dq_ref, dkv_ref, ddn_ref, dz_ref, dga_ref, dgb_ref, dba_ref, dx1_ref,
             gx_ref, dp_ref, dmod_ref, dg_ref):
        dp = jnp.concatenate([r[...] for r in (dq_ref, dkv_ref, ddn_ref, dz_ref, dga_ref, dgb_ref)]
                             + [dba_ref[...].astype(BF)], axis=-1)
        dp_ref[...] = dp
        dh = lax.dot_general(dp, w_ref[...], (((1,), (1,)), ((), ())), preferred_element_type=F32)
        _, vjp = jax.vjp(_norm_mod, x_ref[...], g_ref[...], mod_ref[:, 0:D], mod_ref[:, D:2 * D])
        dx, dgain, dshift, dscale = vjp(dh)
        gx_ref[...] = dx + dx1_ref[...]
        first = pl.program_id(1) == 0
        _acc(dmod_ref, jnp.concatenate([dshift, dscale], axis=-1), first)
        _acc(dg_ref, dgain, first)

    return pl.pallas_call(
        body, name="in_proj_bwd", grid=(B, S // bt),
        in_specs=[_tok(bt, D), _per_batch(6 * D), _full((1, D)), _full((D, IN_PAD))] + [_tok(bt, w) for w in widths]
        + [_tok(bt, 128), _tok(bt, D)],
        out_specs=[_tok(bt, D), _tok(bt, IN_PAD), _per_batch(2 * D), _per_batch(D)],
        out_shape=[_sds((B, S, D), F32), _sds((B, S, IN_PAD), BF), _sds((B, 1, 2 * D), F32), _sds((B, 1, D), F32)],
        compiler_params=_cparams(dimension_semantics=("parallel", "arbitrary")),
    )(x, mod, norm1_g, w_in, *pieces, dba, dx1)


def _matmul_tn(tag, a, b, bk, bn, bt):
    T, K = a.shape
    N = b.shape[1]
    nt = T // bt

    def body(a_ref, b_ref, o_ref, acc_ref):
        t = pl.program_id(2)

        @pl.when(t == 0)
        def _():
            acc_ref[...] = jnp.zeros_like(acc_ref)
        acc_ref[...] += lax.dot_general(a_ref[...], b_ref[...], (((0,), (0,)), ((), ())), preferred_element_type=F32)

        @pl.when(t == nt - 1)
        def _():
            o_ref[...] = acc_ref[...]

    return pl.pallas_call(
        body, name=f"grad_{tag}", grid=(K // bk, N // bn, nt),
        in_specs=[pl.BlockSpec((bt, bk), lambda i, j, t: (t, i)), pl.BlockSpec((bt, bn), lambda i, j, t: (t, j))],
        out_specs=pl.BlockSpec((bk, bn), lambda i, j, t: (i, j)),
        out_shape=_sds((K, N), F32),
        scratch_shapes=[pltpu.VMEM((bk, bn), F32)],
        compiler_params=_cparams(dimension_semantics=("parallel", "parallel", "arbitrary")),
    )(a, b)


def _rope_table(positions):
    inv_freq = THETA ** (-jnp.arange(0, ROT, 2, dtype=F32) / ROT)
    ang = positions.astype(F32)[..., None] * inv_freq
    cos, sin = jnp.cos(ang), jnp.sin(ang)
    ones = jnp.ones(ang.shape[:-1] + (HD - ROT,), F32)
    cos_full = jnp.concatenate([cos, cos, ones], axis=-1)
    sin_full = jnp.concatenate([-sin, sin, 0.0 * ones], axis=-1)
    return jnp.concatenate([cos_full, sin_full], axis=-1)


def _lane_pad(v, offset, width=128):
    return jnp.zeros((1, width), F32).at[0, offset:offset + v.shape[-1]].set(v.reshape(-1))


def _tile(S, want):
    return min(S, want)


def _local_step(x, mod, positions, tgt, norm1_g, w_in_pad, conv_w, q_norm_g, k_norm_g, sinks, a_log, dt_bias,
                dn_norm_g, w_branch, w_out, norm2_g, w_gu, w_dn):
    B, S, _ = x.shape
    T = B * S
    rope_t = _rope_table(positions)
    alog = _lane_pad(a_log, DN_H)
    dtb = _lane_pad(dt_bias, DN_H)
    conv2 = conv_w.reshape(CONV, CONVW)
    bt = _tile(S, 512)
    bt_mlp = _tile(S, 256)

    q, kv, dn, z, ga, gb, ba, h1 = _in_proj(x, mod, norm1_g, w_in_pad, bt)
    o_attn = _attn_fwd(q, kv, rope_t, q_norm_g, k_norm_g, sinks)
    dqkv, bg = _dn_prep(dn, ba, conv2, alog, dtb, bt)
    dn_u, dn_w, dn_qd, dn_kd, dn_a, dn_cd = _dn_intra_fwd(dqkv, bg)
    o_raw, states = _dn_rec_fwd(dn_u, dn_w, dn_qd, dn_kd, dn_a, dn_cd)
    x1, o_dn, merged = _mix_fwd(o_attn, o_raw, z, ga, gb, x, mod, dn_norm_g, w_branch, w_out, bt)
    dx1, h2, act, dgu, dyy, loss, dmod2, dnorm2 = _mlp(x1, tgt, mod, norm2_g, w_gu, w_dn, bt_mlp)

    def flat(t):
        return t.reshape(T, t.shape[-1])
    tn = functools.partial(_matmul_tn, bt=_tile(T, 512))
    g_w_dn = tn("w_down", flat(act), flat(dyy), bk=FFN, bn=D // 2)
    g_w_gu = tn("w_gate_up", flat(h2), flat(dgu), bk=D, bn=2 * FFN // 4)

    d_oa, d_or, dz, dga, dgb, dya, dyd, dout, dgate1, ddn_g = _mix_bwd(
        o_attn, o_raw, z, ga, gb, x, mod, dn_norm_g, w_branch, w_out, dx1, bt_mlp)
    g_w_out = tn("w_out", flat(merged), flat(dout), bk=D, bn=D)
    g_w_br = jnp.concatenate([tn("w_branch_attn", flat(o_attn), flat(dya), bk=QW, bn=D),
                              tn("w_branch_dn", flat(o_dn), flat(dyd), bk=DNW, bn=D)], axis=0)

    d_rec = _dn_rec_bwd(dn_u, dn_w, dn_qd, dn_kd, dn_a, dn_cd, states, d_or)
    d_dqkv, dbg = _dn_intra_bwd(dqkv, bg, *d_rec)
    d_dn, dba, dconv, dalog, ddtb = _dn_prep_bwd(dn, ba, conv2, alog, dtb, d_dqkv, dbg, bt)
    dq, dkv, dqg, dkg, dsk = _attn_bwd(q, kv, rope_t, q_norm_g, k_norm_g, sinks, d_oa)
    grad_x, dproj, dmod1, dnorm1 = _in_proj_bwd(x, mod, norm1_g, w_in_pad, (dq, dkv, d_dn, dz, dga, dgb), dba, dx1, bt)
    g_w_in = tn("w_in", flat(h1), flat(dproj), bk=D, bn=IN_PAD // 3)

    dmod = jnp.concatenate([dmod1, dgate1, dmod2], axis=-1)
    small = dict(norm1_g=jnp.sum(dnorm1, axis=0), norm2_g=jnp.sum(dnorm2, axis=0), q_norm_g=dqg, k_norm_g=dkg,
                 sinks=dsk, a_log=dalog[:, DN_H:2 * DN_H], dt_bias=ddtb[:, DN_H:2 * DN_H],
                 dn_norm_g=jnp.sum(ddn_g, axis=0), conv_w=dconv)
    return jnp.sum(loss), grad_x, dmod, small, (g_w_in, g_w_br, g_w_out, g_w_gu, g_w_dn)


def _me():
    return lax.axis_index("x"), lax.axis_index("y"), lax.axis_index("c")


def _flip(me, f):
    return (me[0] ^ ((f >> 2) & 1), me[1] ^ ((f >> 1) & 1), me[2] ^ (f & 1))


def _comm_call(name, ins, out_shapes, n_remote, plan):
    n_in = len(ins)
    n_out = len(out_shapes)

    def body(*refs):
        in_refs, out_refs = refs[:n_in], refs[n_in:n_in + n_out]
        send_sems, recv_sems = refs[n_in + n_out:]
        remote = plan(_me(), in_refs, out_refs)
        assert len(remote) == n_remote
        cps = [pltpu.make_async_remote_copy(src_ref=src, dst_ref=dst, send_sem=send_sems.at[i], recv_sem=recv_sems.at[i],
                                            device_id=peer, device_id_type=pl.DeviceIdType.MESH)
               for i, (src, dst, peer) in enumerate(remote)]
        for cp in cps:
            cp.start()
        for cp in cps:
            cp.wait_recv()
        for cp in cps:
            cp.wait_send()

    any_spec = pl.BlockSpec(memory_space=pl.ANY)
    return pl.pallas_call(
        body, name=name, in_specs=[any_spec] * n_in, out_specs=[any_spec] * n_out, out_shape=out_shapes,
        scratch_shapes=[pltpu.SemaphoreType.DMA((n_remote,)), pltpu.SemaphoreType.DMA((n_remote,))],
    )(*ins)


def _by_origin(own, received, index):
    stack = jnp.concatenate([own[None], received], axis=0)
    n = stack.shape[0]
    return jnp.stack([lax.dynamic_index_in_dim(stack, k ^ index, 0, keepdims=False) for k in range(n)])


def _gather_devices(name, arrs, dev):
    def plan(me, in_refs, out_refs):
        return [(a, o.at[f - 1], _flip(me, f)) for a, o in zip(in_refs, out_refs) for f in range(1, N_DEV)]
    outs = [_sds((N_DEV - 1,) + a.shape, a.dtype) for a in arrs]
    got = _comm_call(name, arrs, outs, (N_DEV - 1) * len(arrs), plan)
    return [_by_origin(a, g, dev) for a, g in zip(arrs, got)]


def _gather_chips(name, arrs, chip):
    def plan(me, in_refs, out_refs):
        return [(a, o.at[j], _flip(me, 2 * (j + 1))) for a, o in zip(in_refs, out_refs) for j in range(N_CHIP - 1)]
    outs = [_sds((N_CHIP - 1,) + a.shape, a.dtype) for a in arrs]
    got = _comm_call(name, arrs, outs, (N_CHIP - 1) * len(arrs), plan)
    return [_by_origin(a, g, chip) for a, g in zip(arrs, got)]


def _halves(core, mine, other):
    lo = jnp.where(core == 0, mine, other)
    hi = jnp.where(core == 0, other, mine)
    return jnp.concatenate([lo, hi], axis=-2)


def _gather_weights(shards, chip, core):
    def plan_ici(me, in_refs, out_refs):
        remote = []
        for a, o in zip(in_refs, out_refs):
            half = a.shape[0] // 2
            mine = a.at[pl.ds(me[2] * half, half)]
            remote += [(mine, o.at[j], _flip(me, 2 * (j + 1))) for j in range(N_CHIP - 1)]
        return remote
    mine = _comm_call("weights_ici", shards, [_sds((N_CHIP - 1, a.shape[0] // 2, a.shape[1]), a.dtype) for a in shards],
                      (N_CHIP - 1) * len(shards), plan_ici)

    def plan_d2d(me, in_refs, out_refs):
        return [(g, o, _flip(me, 1)) for g, o in zip(in_refs, out_refs)]
    other = _comm_call("weights_d2d", mine, [_sds(g.shape, g.dtype) for g in mine], len(mine), plan_d2d)
    return [_by_origin(a, _halves(core, g, h), chip) for a, g, h in zip(shards, mine, other)]


def _rows(r):
    for br in (512, 352, 256, 128, 64, 32, 16, 8):
        if r % br == 0:
            return br
    raise ValueError(r)


def _pair_add(tag, g, recv, c):
    n, r, cols = g.shape
    half = r // 2
    br = _rows(half)
    nb = half // br

    def body(c_ref, g_ref, r_ref, o_ref):
        o_ref[...] = (g_ref[...] + r_ref[...]).astype(BF)

    return pl.pallas_call(
        body, name=f"pair_add_{tag}",
        grid_spec=pltpu.PrefetchScalarGridSpec(
            num_scalar_prefetch=1, grid=(n, nb),
            in_specs=[pl.BlockSpec((None, br, cols), lambda k, i, c_ref: (k, c_ref[0] * nb + i, 0)),
                      pl.BlockSpec((None, br, cols), lambda k, i, c_ref: (k, i, 0))],
            out_specs=pl.BlockSpec((None, br, cols), lambda k, i, c_ref: (k, i, 0))),
        out_shape=_sds((n, half, cols), BF),
        compiler_params=_cparams(dimension_semantics=("parallel", "parallel")),
    )(c, g, recv)


def _sum_chips(tag, p, q, chip):
    n, r, cols = q.shape
    br = _rows(r)

    def body(chip_ref, p_ref, q_ref, o_ref):
        acc = p_ref[...].astype(F32)
        for k in range(n):
            acc = acc + q_ref[k].astype(F32)
        o_ref[...] = acc

    return pl.pallas_call(
        body, name=f"sum_chips_{tag}",
        grid_spec=pltpu.PrefetchScalarGridSpec(
            num_scalar_prefetch=1, grid=(r // br,),
            in_specs=[pl.BlockSpec((None, br, cols), lambda i, chip_ref: (chip_ref[0], i, 0)),
                      pl.BlockSpec((n, br, cols), lambda i, chip_ref: (0, i, 0))],
            out_specs=pl.BlockSpec((br, cols), lambda i, chip_ref: (i, 0))),
        out_shape=_sds((r, cols), F32),
        compiler_params=_cparams(dimension_semantics=("parallel",)),
    )(chip, p, q)


def _reduce_grads(tags, grads, chip, core):
    core_arr = core.reshape(1).astype(jnp.int32)
    chip_arr = chip.reshape(1).astype(jnp.int32)

    def plan_pair(me, in_refs, out_refs):
        remote = []
        for g, o in zip(in_refs, out_refs):
            half = g.shape[1] // 2
            remote += [(g.at[k, pl.ds((1 - me[2]) * half, half)], o.at[k], _flip(me, 1)) for k in range(N_CHIP)]
        return remote
    recv = _comm_call("grads_pair", grads, [_sds((N_CHIP, g.shape[1] // 2, g.shape[2]), F32) for g in grads],
                      N_CHIP * len(grads), plan_pair)
    pair = [_pair_add(t, g, r, core_arr) for t, g, r in zip(tags, grads, recv)]

    def plan_chips(me, in_refs, out_refs):
        remote = []
        for p, o in zip(in_refs, out_refs):
            for j in range(N_CHIP - 1):
                peer = _flip(me, 2 * (j + 1))
                remote.append((p.at[2 * peer[0] + peer[1]], o.at[j], peer))
        return remote
    parts = _comm_call("grads_chips", pair, [_sds((N_CHIP - 1,) + p.shape[1:], BF) for p in pair],
                       (N_CHIP - 1) * len(pair), plan_chips)
    mine = [_sum_chips(t, p, q, chip_arr) for t, p, q in zip(tags, pair, parts)]

    def plan_swap(me, in_refs, out_refs):
        return [(h, o, _flip(me, 1)) for h, o in zip(in_refs, out_refs)]
    other = _comm_call("grads_swap", mine, [_sds(h.shape, F32) for h in mine], len(mine), plan_swap)
    return [_halves(core, h, o) for h, o in zip(mine, other)]


def _adamw_math(w, g, m, v):
    m = ADAM_B1 * m + (1.0 - ADAM_B1) * g
    v = ADAM_B2 * v + (1.0 - ADAM_B2) * (g * g)
    m_hat = m / (1.0 - ADAM_B1 ** ADAM_STEP)
    v_hat = v / (1.0 - ADAM_B2 ** ADAM_STEP)
    delta = -ADAM_LR * (m_hat / (jnp.sqrt(v_hat) + ADAM_EPS) + ADAM_WD * w)
    return delta, m, v


def _adamw(name, w, g, m, v):
    r, cols = w.shape
    br = _rows(r)
    if br * cols * 4 > (1 << 20) and br % 16 == 0:
        br //= 2

    def body(w_ref, g_ref, m_ref, v_ref, d_ref, mo_ref, vo_ref):
        d_ref[...], mo_ref[...], vo_ref[...] = _adamw_math(w_ref[...], g_ref[...], m_ref[...], v_ref[...])

    spec = pl.BlockSpec((br, cols), lambda i: (i, 0))
    return pl.pallas_call(
        body, name=f"adamw_{name}", grid=(r // br,), in_specs=[spec] * 4, out_specs=[spec] * 3,
        out_shape=[_sds((r, cols), F32)] * 3,
        compiler_params=_cparams(dimension_semantics=("parallel",)),
    )(w, g, m, v)


def _ada_fwd(c_all, ada_w, ada_b_cols):
    n = c_all.shape[0]

    def body(c_ref, w_ref, b_ref, o_ref):
        o_ref[...] = _mmx(_silu(c_ref[...]), w_ref[...]) + b_ref[...]

    return pl.pallas_call(
        body, name="ada_fwd", out_shape=_sds((n, ada_w.shape[1]), F32), compiler_params=_cparams(),
    )(c_all, ada_w, ada_b_cols)


def _ada_bwd(c_all, dmod_cols, w, m, v):
    n = c_all.shape[0]
    r, cols = w.shape
    br = 128

    def body(c_ref, d_ref, w_ref, m_ref, v_ref, g_ref, dl_ref, mo_ref, vo_ref):
        cond = _silu(c_ref[...])
        g = lax.dot_general(cond, d_ref[...], (((0,), (0,)), ((), ())), precision=lax.Precision.HIGHEST,
                            preferred_element_type=F32)
        g_ref[...] = g
        dl_ref[...], mo_ref[...], vo_ref[...] = _adamw_math(w_ref[...], g, m_ref[...], v_ref[...])

    spec = pl.BlockSpec((br, cols), lambda i: (i, 0))
    return pl.pallas_call(
        body, name="ada_bwd", grid=(r // br,),
        in_specs=[pl.BlockSpec((n, br), lambda i: (0, i)), pl.BlockSpec((n, cols), lambda i: (0, 0)), spec, spec, spec],
        out_specs=[spec] * 4, out_shape=[_sds((r, cols), F32)] * 4,
        compiler_params=_cparams(dimension_semantics=("parallel",)),
    )(c_all, dmod_cols, w, m, v)


def _sum_devices(parts):
    n, r, cols = parts.shape

    def body(p_ref, o_ref):
        acc = p_ref[0]
        for k in range(1, n):
            acc = acc + p_ref[k]
        o_ref[...] = acc

    return pl.pallas_call(body, name="sum_devices", out_shape=_sds((r, cols), F32), compiler_params=_cparams())(parts)


SMALL_ROWS = 16
_SMALL_SLOTS = dict(norm1_g=(0, 0, D), norm2_g=(1, 0, D), q_norm_g=(2, 0, HD), k_norm_g=(2, 128, HD), sinks=(2, 256, HEADS),
                    a_log=(2, 384, DN_H), dt_bias=(2, 512, DN_H), dn_norm_g=(2, 640, DN_D))
_CONV_ROW = 4
_ADA_B_ROW = 8


def _pack_small(vals, conv, ada_b):
    sheet = jnp.zeros((SMALL_ROWS, CONVW), F32)
    for name, (row, col, n) in _SMALL_SLOTS.items():
        sheet = sheet.at[row, col:col + n].set(vals[name].reshape(n))
    sheet = sheet.at[_CONV_ROW:_CONV_ROW + CONV, 0:conv.shape[1]].set(conv)
    return sheet.at[_ADA_B_ROW:_ADA_B_ROW + 4, :].set(ada_b.reshape(4, CONVW))


def _unpack_small(sheet, conv_cols):
    out = {name: sheet[row, col:col + n].reshape(1, n) for name, (row, col, n) in _SMALL_SLOTS.items()}
    out["conv_w"] = sheet[_CONV_ROW:_CONV_ROW + CONV, 0:conv_cols].reshape(1, CONV, 1, conv_cols)
    out["ada_b"] = sheet[_ADA_B_ROW:_ADA_B_ROW + 4, :].reshape(1, 6 * D)
    return out


def _pad_w_in(w):
    return jnp.concatenate([w[:, :C_Z], w[:, C_Z + 2 * DN_H:IN_WIDTH], w[:, C_Z:C_Z + 2 * DN_H],
                            jnp.zeros((w.shape[0], IN_PAD - IN_WIDTH), w.dtype)], axis=1)


def _unpad_w_in(g):
    return jnp.concatenate([g[:, :C_Z], g[:, C_BA:C_BA + 2 * DN_H], g[:, C_Z:C_BA]], axis=1)


def _cols_to_blocks(g):
    r = g.shape[0]
    return g.reshape(r, N_CHIP, g.shape[1] // N_CHIP).transpose(1, 0, 2)


def _blocks_to_cols(f):
    return f.transpose(1, 0, 2).reshape(f.shape[1], N_CHIP * f.shape[2])


def kernel(x, c, positions, ada_w, ada_b, norm1_g, w_in, conv_w, q_norm_g, k_norm_g, sinks, a_log, dt_bias, dn_norm_g, w_branch, w_out, norm2_g, w_gate_up, w_down, loss_target, m_ada_w, m_ada_b, m_norm1_g, m_w_in, m_conv_w, m_q_norm_g, m_k_norm_g, m_sinks, m_a_log, m_dt_bias, m_dn_norm_g, m_w_branch, m_w_out, m_norm2_g, m_w_gate_up, m_w_down, v_ada_w, v_ada_b, v_norm1_g, v_w_in, v_conv_w, v_q_norm_g, v_k_norm_g, v_sinks, v_a_log, v_dt_bias, v_dn_norm_g, v_w_branch, v_w_out, v_norm2_g, v_w_gate_up, v_w_down):
    ix, iy, ic = lax.axis_index("x"), lax.axis_index("y"), lax.axis_index("c")
    dev = 4 * ix + 2 * iy + ic
    chip = 2 * ix + iy
    n_seq = x.shape[0]
    conv_cols = conv_w.shape[-1]

    c_all, conv_all = _gather_devices("gather_cond", [c, conv_w.reshape(CONV, conv_cols)], dev)
    c_all = c_all.reshape(N_DEV * n_seq, D)
    ada_cols = ada_w.shape[-1]
    ada_b_cols = lax.dynamic_slice(ada_b, (0, chip * ada_cols), (1, ada_cols))
    mod_cols = _ada_fwd(c_all, ada_w[0], ada_b_cols)
    (mod_blocks,) = _gather_chips("gather_mod", [mod_cols], chip)
    mod_all = _blocks_to_cols(mod_blocks)
    mod = lax.dynamic_slice(mod_all, (dev * n_seq, 0), (n_seq, 6 * D)).reshape(n_seq, 1, 6 * D)
    conv_full = _blocks_to_cols(conv_all[0::2])

    shards = [w_in[0].astype(BF), w_branch[0].astype(BF), w_out[0].astype(BF), w_gate_up[0].astype(BF), w_down[0].astype(BF)]
    f_in, f_br, f_out, f_gu, f_dn = _gather_weights(shards, chip, ic)
    w_in_pad = _pad_w_in(_blocks_to_cols(f_in))
    w_gu_full = _blocks_to_cols(f_gu)
    w_br_full, w_out_full, w_dn_full = (f.reshape(N_CHIP * f.shape[1], f.shape[2]) for f in (f_br, f_out, f_dn))

    loss, grad_x, dmod, small, (g_in, g_br, g_out, g_gu, g_dn) = _local_step(
        x, mod, positions, loss_target, norm1_g, w_in_pad, conv_full.reshape(CONV, 1, CONVW), q_norm_g, k_norm_g, sinks,
        a_log, dt_bias, dn_norm_g, w_br_full, w_out_full, norm2_g, w_gu_full, w_dn_full)
    loss = lax.psum(loss, ("x", "y", "c"))

    blocks = [_cols_to_blocks(_unpad_w_in(g_in)), g_br.reshape(N_CHIP, -1, D), g_out.reshape(N_CHIP, -1, D),
              _cols_to_blocks(g_gu), g_dn.reshape(N_CHIP, -1, D)]
    r_in, r_br, r_out, r_gu, r_dn = _reduce_grads(("w_in", "w_branch", "w_out", "w_gate_up", "w_down"), blocks, chip, ic)
    big = {}
    for name, w, g, m, v in (("w_in", w_in, r_in, m_w_in, v_w_in), ("w_branch", w_branch, r_br, m_w_branch, v_w_branch),
                             ("w_out", w_out, r_out, m_w_out, v_w_out), ("w_gate_up", w_gate_up, r_gu, m_w_gate_up, v_w_gate_up),
                             ("w_down", w_down, r_dn, m_w_down, v_w_down)):
        big[name] = (g,) + tuple(_adamw(name, w[0], g, m[0], v[0]))

    part = _pack_small(small, small["conv_w"], jnp.sum(dmod, axis=(0, 1)).reshape(1, 6 * D))
    dmod_all, parts = _gather_devices("gather_small", [dmod.reshape(n_seq, 6 * D), part], dev)
    dmod_all = dmod_all.reshape(N_DEV * n_seq, 6 * D)
    g_small = _unpack_small(_sum_devices(parts), CONVW)
    g_conv = lax.dynamic_slice(g_small["conv_w"].reshape(CONV, CONVW), (0, chip * conv_cols), (CONV, conv_cols))
    g_small["conv_w"] = g_conv.reshape(1, CONV, 1, conv_cols)

    given = dict(norm1_g=(norm1_g, m_norm1_g, v_norm1_g), norm2_g=(norm2_g, m_norm2_g, v_norm2_g),
                 q_norm_g=(q_norm_g, m_q_norm_g, v_q_norm_g), k_norm_g=(k_norm_g, m_k_norm_g, v_k_norm_g),
                 sinks=(sinks, m_sinks, v_sinks), a_log=(a_log, m_a_log, v_a_log), dt_bias=(dt_bias, m_dt_bias, v_dt_bias),
                 dn_norm_g=(dn_norm_g, m_dn_norm_g, v_dn_norm_g))
    sheets = [_pack_small({k: t[j] for k, t in given.items()}, cw.reshape(CONV, conv_cols), ab)
              for j, (cw, ab) in enumerate(((conv_w, ada_b), (m_conv_w, m_ada_b), (v_conv_w, v_ada_b)))]
    g_local = _pack_small(g_small, g_conv, g_small["ada_b"])
    upd = [_unpack_small(s, conv_cols) for s in _adamw("small", sheets[0], g_local, sheets[1], sheets[2])]

    dmod_cols = lax.dynamic_slice(dmod_all, (0, chip * ada_cols), (N_DEV * n_seq, ada_cols))
    ada = _ada_bwd(c_all, dmod_cols, ada_w[0], m_ada_w[0], v_ada_w[0])

    names = ["ada_w", "ada_b", "norm1_g", "w_in", "conv_w", "q_norm_g", "k_norm_g", "sinks", "a_log", "dt_bias", "dn_norm_g",
             "w_branch", "w_out", "norm2_g", "w_gate_up", "w_down"]

    def leaf(name, j):
        if name == "ada_w":
            return ada[j][None]
        if name in big:
            return big[name][j][None]
        return g_small[name] if j == 0 else upd[j - 1][name]

    return (loss, grad_x) + tuple(leaf(n, j) for j in range(4) for n in names)
```

```python
import functools

import jax
import jax.numpy as jnp
import numpy as np
from jax import lax
from jax.experimental import pallas as pl
from jax.experimental.pallas import tpu as pltpu

F32 = jnp.float32
BF = jnp.bfloat16

D = 1024
HEADS = 8
KV_HEADS = 2
GROUP = 4
HD = 64
BLK = 128
ROT = 16
THETA = 500000.0
QW = 512
KVW = 128
DN_H = 4
DN_D = 128
CONV = 4
CHUNK = 64
DNW = 512
CONVW = 1536
FFN = 2816
EPS = 1e-6
IN_WIDTH = 4872
IN_PAD = 4992
C_KV = 512
C_DN = 768
C_Z = 2304
C_GA = 2816
C_GB = 3840
C_BA = 4864
NEG = -1e30
N_DEV = 8
N_CHIP = 4

ADAM_LR = 0.001
ADAM_B1 = 0.9
ADAM_B2 = 0.999
ADAM_EPS = 1e-08
ADAM_WD = 0.01
ADAM_STEP = 10

VMEM_LIMIT = 60 * 1024 * 1024


def _cparams(**kw):
    return pltpu.CompilerParams(vmem_limit_bytes=VMEM_LIMIT, **kw)


def _dg(a, b, ca, cb):
    return lax.dot_general(a.astype(BF), b.astype(BF), (((ca,), (cb,)), ((), ())),
                           preferred_element_type=F32)


@jax.custom_vjp
def _mm(a, b):
    return _dg(a, b, 1, 0)


def _mm_fwd(a, b):
    return _dg(a, b, 1, 0), (a, b)


def _mm_bwd(res, dy):
    a, b = res
    return _dg(dy, b, 1, 1).astype(a.dtype), _dg(a, dy, 0, 0).astype(b.dtype)


_mm.defvjp(_mm_fwd, _mm_bwd)


@jax.custom_vjp
def _mm_nt(a, b):
    return _dg(a, b, 1, 1)


def _mm_nt_fwd(a, b):
    return _dg(a, b, 1, 1), (a, b)


def _mm_nt_bwd(res, dy):
    a, b = res
    return _dg(dy, b, 1, 0).astype(a.dtype), _dg(dy, a, 0, 0).astype(b.dtype)


_mm_nt.defvjp(_mm_nt_fwd, _mm_nt_bwd)


@jax.custom_vjp
def _mm_tn(a, b):
    return _dg(a, b, 0, 0)


def _mm_tn_fwd(a, b):
    return _dg(a, b, 0, 0), (a, b)


def _mm_tn_bwd(res, dy):
    a, b = res
    return _dg(b, dy, 1, 1).astype(a.dtype), _dg(a, dy, 1, 0).astype(b.dtype)


_mm_tn.defvjp(_mm_tn_fwd, _mm_tn_bwd)


def _mmx(a, b):
    return jnp.dot(a, b, precision=lax.Precision.HIGHEST, preferred_element_type=F32)


def _mmx_nt(a, b):
    return lax.dot_general(a, b, (((1,), (1,)), ((), ())), precision=lax.Precision.HIGHEST,
                           preferred_element_type=F32)


def _iota(shape, dim):
    return lax.broadcasted_iota(jnp.int32, shape, dim)


def _sigmoid(x):
    return 1.0 / (1.0 + jnp.exp(-x))


def _silu(x):
    return x * _sigmoid(x)


def _softplus(x):
    return jnp.maximum(x, 0.0) + jnp.log(1.0 + jnp.exp(-jnp.abs(x)))


def _rms(x, gain):
    return x * lax.rsqrt(jnp.mean(x * x, axis=-1, keepdims=True) + EPS) * gain


def _norm_mod(x, gain, shift, scale):
    return _rms(x, gain) * (1.0 + scale) + shift


def _rope_perm():
    r = _iota((HD, HD), 0)
    c = _iota((HD, HD), 1)
    half = ROT // 2
    hit = ((c < half) & (r == c + half)) | ((c >= half) & (c < ROT) & (r == c - half))
    return jnp.where(hit, 1.0, 0.0).astype(F32)


def _rope(x, cos, sin, perm):
    return x * cos + _mmx(x, perm) * sin


def _attn_block(qs, kc, kp, vc, vp, qg, kg, sinks, cos_c, sin_c, cos_p, sin_p, has_prev):
    perm = _rope_perm()
    rows = GROUP * BLK
    qi = _iota((rows, 2 * BLK), 0) % BLK + BLK
    kj = _iota((rows, 2 * BLK), 1)
    dist = qi - kj
    valid = (dist >= 0) & (dist < BLK) & ((kj >= BLK) | has_prev)
    grp = _iota((rows, HEADS), 0) // BLK
    col = _iota((rows, HEADS), 1)
    outs = []
    for h in range(KV_HEADS):
        q = jnp.concatenate([_rope(_rms(qs[h * GROUP + g], qg), cos_c, sin_c, perm)
                             for g in range(GROUP)], axis=0)
        k = jnp.concatenate([_rope(_rms(kp[h], kg), cos_p, sin_p, perm),
                             _rope(_rms(kc[h], kg), cos_c, sin_c, perm)], axis=0)
        v = jnp.concatenate([vp[h], vc[h]], axis=0)
        s = _mm_nt(q, k) * (HD ** -0.5)
        s = jnp.where(valid, s, NEG)
        sink = jnp.sum(jnp.where(col == h * GROUP + grp, sinks, 0.0), axis=-1, keepdims=True)
        m = lax.stop_gradient(jnp.maximum(jnp.max(s, axis=-1, keepdims=True), sink))
        p = jnp.exp(s - m)
        probs = p / (jnp.sum(p, axis=-1, keepdims=True) + jnp.exp(sink - m))
        o = _mm(probs, v)
        outs += [o[g * BLK:(g + 1) * BLK] for g in range(GROUP)]
    return outs


def _dn_tail(ys, ba, alog, dtb):
    def l2(t):
        return t * lax.rsqrt(jnp.sum(t * t, axis=-1, keepdims=True) + EPS)
    s = [_silu(y) for y in ys]
    out = [l2(t) for t in s[:2 * DN_H]] + s[2 * DN_H:]
    lane = _iota(ba.shape, 1)
    beta = _sigmoid(ba)
    g = -jnp.exp(alog) * _softplus(ba + dtb)
    bg = jnp.where(lane < DN_H, beta, jnp.where(lane < 2 * DN_H, g, 0.0))
    return out, bg


def _bdg(a, b, ca, cb):
    return lax.dot_general(a.astype(BF), b.astype(BF), (((ca,), (cb,)), ((0,), (0,))), preferred_element_type=F32)


@jax.custom_vjp
def _bmm(a, b):
    return _bdg(a, b, 2, 1)


def _bmm_fwd(a, b):
    return _bdg(a, b, 2, 1), (a, b)


def _bmm_bwd(res, dy):
    a, b = res
    return _bdg(dy, b, 2, 2), _bdg(a, dy, 1, 1)


_bmm.defvjp(_bmm_fwd, _bmm_bwd)


@jax.custom_vjp
def _bmm_nt(a, b):
    return _bdg(a, b, 2, 2)


def _bmm_nt_fwd(a, b):
    return _bdg(a, b, 2, 2), (a, b)


def _bmm_nt_bwd(res, dy):
    a, b = res
    return _bdg(dy, b, 2, 1), _bdg(dy, a, 1, 1)


_bmm_nt.defvjp(_bmm_nt_fwd, _bmm_nt_bwd)


def _bmmx(a, b):
    return lax.dot_general(a, b, (((2,), (1,)), ((0,), (0,))), precision=lax.Precision.HIGHEST,
                           preferred_element_type=F32)


def _dn_intra(q, k, v, bg):
    C = CHUNK
    G = bg.shape[0]
    r = _iota((C, C), 0)
    c = _iota((C, C), 1)
    incl = (r >= c)[None]
    strict = (r > c)[None]
    eye = jnp.where(r == c, 1.0, 0.0).astype(F32)[None]
    tri = jnp.broadcast_to(jnp.where(r >= c, 1.0, 0.0).astype(F32)[None], (G, C, C))
    gc_all = _bmmx(tri, bg)
    lane = _iota((C, DN_D), 1)

    def per_head(x, offset):
        return jnp.concatenate([jnp.sum(jnp.where(lane == offset + h, x[g], 0.0), axis=-1, keepdims=True)[None]
                                for g in range(G) for h in range(DN_H)], axis=0)
    beta = per_head(bg, 0)
    gcol = per_head(gc_all, DN_H)
    grow = jnp.sum(eye * gcol, axis=1, keepdims=True)
    glast = jnp.sum(jnp.where(_iota((1, C, 1), 1) == C - 1, gcol, 0.0), axis=1, keepdims=True)
    decay = jnp.exp(jnp.where(incl, gcol - grow, NEG))
    q = q * (DN_D ** -0.5)
    kb = k * beta
    lmat = jnp.where(strict, _bmm_nt(kb, k) * decay, 0.0)
    pw = -lmat
    tinv = eye + pw
    for _ in range(5):
        pw = _bmmx(pw, pw)
        tinv = tinv + _bmmx(tinv, pw)
    egc = jnp.exp(gcol)
    u = _bmm(tinv, v * beta)
    w = _bmm(tinv, kb * egc)
    a = _bmm_nt(q, k) * decay
    return u, w, q * egc, k * jnp.exp(glast - gcol), a, jnp.exp(glast)


def _dn_rec(state, us, ws, qds, kds, ats, cd):
    lane_row = _iota((1, DN_D), 1)
    new_state, outs = [], []
    for h in range(DN_H):
        cdh = jnp.sum(jnp.where(lane_row == h, cd, 0.0), axis=-1, keepdims=True)
        v_new = us[h] - _mm(ws[h], state[h])
        outs.append(_mm(qds[h], state[h]) + _mm(ats[h], v_new))
        new_state.append(state[h] * cdh + _mm_tn(kds[h], v_new))
    return new_state, outs


def _mix_tile(o_attn, o_raw, zs, ga, gb, x, gate1, dn_g, wb_a, wb_d, w_out, p_ya, p_yd, p_out):
    o_dn = jnp.concatenate([_rms(o_raw[h], dn_g) * _silu(zs[h]) for h in range(DN_H)], axis=-1)
    y_a = _mm(o_attn, wb_a) + p_ya
    y_d = _mm(o_dn, wb_d) + p_yd
    merged = _sigmoid(ga) * y_a + _sigmoid(gb) * y_d
    out = _mm(merged, w_out) + p_out
    return x + gate1 * out, o_dn, merged


def _mlp_tile(x1, gain, shift, scale, gate2, w_gu, w_dn, tgt, p_gu, p_yy):
    h2 = _norm_mod(x1, gain, shift, scale)
    gu = _mm(h2, w_gu) + p_gu
    act = _silu(gu[:, :FFN]) * gu[:, FFN:]
    yy = _mm(act, w_dn) + p_yy
    y = x1 + gate2 * yy
    err = y - tgt
    return 0.5 * jnp.sum(err * err) * (1.0 / D), (h2, act)


def _tok(bt, f):
    return pl.BlockSpec((None, bt, f), lambda b, i: (b, i, 0))


def _full(shape):
    return pl.BlockSpec(shape, lambda b, i: (0,) * len(shape))


def _per_batch(f):
    return pl.BlockSpec((None, 1, f), lambda b, i: (b, 0, 0))


def _sds(shape, dtype):
    return jax.ShapeDtypeStruct(shape, dtype)


def _acc(ref, val, first):
    @pl.when(first)
    def _():
        ref[...] = val

    @pl.when(jnp.logical_not(first))
    def _():
        ref[...] += val


def _in_proj(x, mod, norm1_g, w_in, bt):
    B, S, _ = x.shape

    def body(x_ref, mod_ref, g_ref, w_ref, q_ref, kv_ref, dn_ref, z_ref, ga_ref, gb_ref, ba_ref, h_ref):
        h = _norm_mod(x_ref[...], g_ref[...], mod_ref[:, 0:D], mod_ref[:, D:2 * D]).astype(BF)
        h_ref[...] = h

        def proj(c0, c1):
            return jnp.dot(h, w_ref[:, c0:c1], preferred_element_type=F32)
        q_ref[...] = proj(0, C_KV).astype(BF)
        kv_ref[...] = proj(C_KV, C_DN).astype(BF)
        dn_ref[...] = proj(C_DN, C_Z).astype(BF)
        z_ref[...] = proj(C_Z, C_GA).astype(BF)
        ga_ref[...] = proj(C_GA, C_GB).astype(BF)
        gb_ref[...] = proj(C_GB, C_BA).astype(BF)
        ba_ref[...] = proj(C_BA, IN_PAD)

    widths = (QW, 2 * KVW, CONVW, DNW, D, D)
    return pl.pallas_call(
        body, name="in_proj", grid=(B, S // bt),
        in_specs=[_tok(bt, D), _per_batch(6 * D), _full((1, D)), _full((D, IN_PAD))],
        out_specs=[_tok(bt, w) for w in widths] + [_tok(bt, 128), _tok(bt, D)],
        out_shape=[_sds((B, S, w), BF) for w in widths] + [_sds((B, S, 128), F32), _sds((B, S, D), BF)],
        compiler_params=_cparams(dimension_semantics=("parallel", "parallel")),
    )(x, mod, norm1_g, w_in)


def _prev_blk(bt, f):
    return pl.BlockSpec((None, bt, f), lambda b, i: (b, jnp.maximum(i - 1, 0), 0))


def _attn_load(q_ref, kvc_ref, kvp_ref):
    qs = [q_ref[:, h * HD:(h + 1) * HD].astype(F32) for h in range(HEADS)]
    kc = [kvc_ref[:, h * HD:(h + 1) * HD].astype(F32) for h in range(KV_HEADS)]
    kp = [kvp_ref[:, h * HD:(h + 1) * HD].astype(F32) for h in range(KV_HEADS)]
    vc = [kvc_ref[:, KVW + h * HD:KVW + (h + 1) * HD].astype(F32) for h in range(KV_HEADS)]
    vp = [kvp_ref[:, KVW + h * HD:KVW + (h + 1) * HD].astype(F32) for h in range(KV_HEADS)]
    return qs, kc, kp, vc, vp


def _attn_fwd(q, kv, rope_t, q_norm_g, k_norm_g, sinks):
    B, S, _ = q.shape

    def body(q_ref, kvc_ref, kvp_ref, rc_ref, rp_ref, qg_ref, kg_ref, sk_ref, o_ref):
        qs, kc, kp, vc, vp = _attn_load(q_ref, kvc_ref, kvp_ref)
        outs = _attn_block(qs, kc, kp, vc, vp, qg_ref[...], kg_ref[...], sk_ref[...],
                           rc_ref[:, 0:HD], rc_ref[:, HD:2 * HD], rp_ref[:, 0:HD], rp_ref[:, HD:2 * HD],
                           pl.program_id(1) > 0)
        for h in range(HEADS):
            o_ref[:, h * HD:(h + 1) * HD] = outs[h].astype(BF)

    return pl.pallas_call(
        body, name="attn_fwd", grid=(B, S // BLK),
        in_specs=[_tok(BLK, QW), _tok(BLK, 2 * KVW), _prev_blk(BLK, 2 * KVW), _tok(BLK, 2 * HD), _prev_blk(BLK, 2 * HD),
                  _full((1, HD)), _full((1, HD)), _full((1, HEADS))],
        out_specs=_tok(BLK, QW), out_shape=_sds((B, S, QW), BF),
        compiler_params=_cparams(dimension_semantics=("parallel", "parallel")),
    )(q, kv, kv, rope_t, rope_t, q_norm_g, k_norm_g, sinks)


def _conv_fwd_tile(xe_ref, x_ref, halo_ref, cw_ref, first, bt):
    halo = halo_ref[...].astype(F32)
    xe_ref[0:8, :] = jnp.where(first, 0.0, halo)
    xe_ref[8:bt + 8, :] = x_ref[...].astype(F32)
    y = cw_ref[0:1, :] * xe_ref[5:bt + 5, :]
    for j in range(1, CONV):
        y = y + cw_ref[j:j + 1, :] * xe_ref[5 + j:bt + 5 + j, :]
    return y


def _halo_spec(bt):
    return pl.BlockSpec((None, 8, CONVW), lambda b, i: (b, jnp.maximum(i * (bt // 8) - 1, 0), 0))


def _dn_prep(dn, ba, conv_w, alog, dtb, bt):
    B, S, _ = dn.shape

    def body(x_ref, halo_ref, ba_ref, cw_ref, al_ref, dt_ref, qkv_ref, bg_ref, xe_ref):
        y = _conv_fwd_tile(xe_ref, x_ref, halo_ref, cw_ref, pl.program_id(1) == 0, bt)
        ys = [y[:, j * DN_D:(j + 1) * DN_D] for j in range(3 * DN_H)]
        out, bg = _dn_tail(ys, ba_ref[...], al_ref[...], dt_ref[...])
        for j in range(3 * DN_H):
            qkv_ref[:, j * DN_D:(j + 1) * DN_D] = out[j]
        bg_ref[...] = bg

    return pl.pallas_call(
        body, name="dn_prep", grid=(B, S // bt),
        in_specs=[_tok(bt, CONVW), _halo_spec(bt), _tok(bt, 128), _full((CONV, CONVW)), _full((1, 128)), _full((1, 128))],
        out_specs=[_tok(bt, CONVW), _tok(bt, 128)],
        out_shape=[_sds((B, S, CONVW), F32), _sds((B, S, 128), F32)],
        scratch_shapes=[pltpu.VMEM((bt + 8, CONVW), F32)],
        compiler_params=_cparams(dimension_semantics=("parallel", "arbitrary")),
    )(dn, dn, ba, conv_w, alog, dtb)


def _dn_load(qkv_ref):
    qs = [qkv_ref[:, h * DN_D:(h + 1) * DN_D] for h in range(DN_H)]
    ks = [qkv_ref[:, DNW + h * DN_D:DNW + (h + 1) * DN_D] for h in range(DN_H)]
    vs = [qkv_ref[:, 2 * DNW + h * DN_D:2 * DNW + (h + 1) * DN_D] for h in range(DN_H)]
    return qs, ks, vs


DN_GROUP = 4
AW = DN_H * CHUNK


def _stack_heads(ref, G, offset, width):
    return jnp.stack([ref[g * CHUNK:(g + 1) * CHUNK, offset + h * width:offset + (h + 1) * width]
                      for g in range(G) for h in range(DN_H)])


def _dn_load_stack(qkv_ref, G):
    return tuple(_stack_heads(qkv_ref, G, j * DNW, DN_D) for j in range(3))


def _cd_spec(n):
    return pl.BlockSpec((None, n, 1, DN_D), lambda b, i: (b, i, 0, 0))


def _dn_intra_fwd(qkv, bg):
    B, S, _ = qkv.shape
    nc = S // CHUNK
    G = min(DN_GROUP, nc)
    rows = G * CHUNK

    def body(qkv_ref, bg_ref, u_ref, w_ref, qd_ref, kd_ref, a_ref, cd_ref):
        q, k, v = _dn_load_stack(qkv_ref, G)
        u, w, qd, kd, a, cd = _dn_intra(q, k, v, bg_ref[...].reshape(G, CHUNK, DN_D))
        lane_row = _iota((1, DN_D), 1)
        for g in range(G):
            rows = slice(g * CHUNK, (g + 1) * CHUNK)
            cd_row = jnp.zeros((1, DN_D), F32)
            for h in range(DN_H):
                n = g * DN_H + h
                cols = slice(h * DN_D, (h + 1) * DN_D)
                u_ref[rows, cols] = u[n]
                w_ref[rows, cols] = w[n].astype(BF)
                qd_ref[rows, cols] = qd[n].astype(BF)
                kd_ref[rows, cols] = kd[n].astype(BF)
                a_ref[rows, h * CHUNK:(h + 1) * CHUNK] = a[n].astype(BF)
                cd_row = cd_row + jnp.where(lane_row == h, cd[n], 0.0)
            cd_ref[g] = cd_row

    return pl.pallas_call(
        body, name="dn_intra_fwd", grid=(B, nc // G),
        in_specs=[_tok(rows, CONVW), _tok(rows, 128)],
        out_specs=[_tok(rows, DNW)] * 4 + [_tok(rows, AW), _cd_spec(G)],
        out_shape=[_sds((B, S, DNW), F32)] + [_sds((B, S, DNW), BF)] * 3 + [_sds((B, S, AW), BF), _sds((B, nc, 1, DN_D), F32)],
        compiler_params=_cparams(dimension_semantics=("parallel", "parallel")),
    )(qkv, bg)


def _rec_load(u_ref, w_ref, qd_ref, kd_ref, a_ref):
    def heads(ref, width):
        return [ref[:, h * width:(h + 1) * width].astype(F32) for h in range(DN_H)]
    return heads(u_ref, DN_D), heads(w_ref, DN_D), heads(qd_ref, DN_D), heads(kd_ref, DN_D), heads(a_ref, CHUNK)


def _dn_rec_fwd(u, w, qd, kd, a, cd):
    B, S, _ = u.shape
    nc = S // CHUNK

    def body(u_ref, w_ref, qd_ref, kd_ref, a_ref, cd_ref, o_ref, st_ref, s_ref):
        @pl.when(pl.program_id(1) == 0)
        def _():
            s_ref[...] = jnp.zeros_like(s_ref)
        state = [s_ref[h] for h in range(DN_H)]
        for h in range(DN_H):
            st_ref[h] = state[h]
        new_state, outs = _dn_rec(state, *_rec_load(u_ref, w_ref, qd_ref, kd_ref, a_ref), cd_ref[0])
        for h in range(DN_H):
            s_ref[h] = new_state[h]
            o_ref[:, h * DN_D:(h + 1) * DN_D] = outs[h]

    return pl.pallas_call(
        body, name="dn_rec_fwd", grid=(B, nc),
        in_specs=[_tok(CHUNK, DNW)] * 4 + [_tok(CHUNK, AW), _cd_spec(1)],
        out_specs=[_tok(CHUNK, DNW), pl.BlockSpec((None, None, DN_H, DN_D, DN_D), lambda b, i: (b, i, 0, 0, 0))],
        out_shape=[_sds((B, S, DNW), F32), _sds((B, nc, DN_H, DN_D, DN_D), F32)],
        scratch_shapes=[pltpu.VMEM((DN_H, DN_D, DN_D), F32)],
        compiler_params=_cparams(dimension_semantics=("parallel", "arbitrary")),
    )(u, w, qd, kd, a, cd)


def _mix_load(oa_ref, or_ref, z_ref):
    o_raw = [or_ref[:, h * DN_D:(h + 1) * DN_D] for h in range(DN_H)]
    zs = [z_ref[:, h * DN_D:(h + 1) * DN_D].astype(F32) for h in range(DN_H)]
    return oa_ref[...].astype(F32), o_raw, zs


def _mix_fwd(o_attn, o_raw, z, ga, gb, x, mod, dn_g, w_branch, w_out, bt):
    B, S, _ = x.shape

    def body(oa_ref, or_ref, z_ref, ga_ref, gb_ref, x_ref, mod_ref, dg_ref, wb_ref, wo_ref, x1_ref, od_ref, mg_ref):
        oa, o_r, zs = _mix_load(oa_ref, or_ref, z_ref)
        x1, o_dn, merged = _mix_tile(oa, o_r, zs, ga_ref[...].astype(F32), gb_ref[...].astype(F32), x_ref[...],
                                     mod_ref[:, 2 * D:3 * D], dg_ref[...], wb_ref[0:QW, :], wb_ref[QW:2 * QW, :],
                                     wo_ref[...], 0.0, 0.0, 0.0)
        x1_ref[...] = x1
        od_ref[...] = o_dn.astype(BF)
        mg_ref[...] = merged.astype(BF)

    return pl.pallas_call(
        body, name="mix_fwd", grid=(B, S // bt),
        in_specs=[_tok(bt, QW), _tok(bt, DNW), _tok(bt, DNW), _tok(bt, D), _tok(bt, D), _tok(bt, D), _per_batch(6 * D),
                  _full((1, DN_D)), _full((D, D)), _full((D, D))],
        out_specs=[_tok(bt, D), _tok(bt, DNW), _tok(bt, D)],
        out_shape=[_sds((B, S, D), F32), _sds((B, S, DNW), BF), _sds((B, S, D), BF)],
        compiler_params=_cparams(dimension_semantics=("parallel", "parallel")),
    )(o_attn, o_raw, z, ga, gb, x, mod, dn_g, w_branch, w_out)


def _mlp(x1, tgt, mod, norm2_g, w_gu, w_dn, bt):
    B, S, _ = x1.shape

    def body(x1_ref, t_ref, mod_ref, g_ref, wgu_ref, wdn_ref,
             dx1_ref, h2_ref, act_ref, dgu_ref, dyy_ref, loss_ref, dmod_ref, dg_ref):
        w_gu_v, w_dn_v, t = wgu_ref[...], wdn_ref[...], t_ref[...]

        def f(x1, gain, shift, scale, gate2, p_gu, p_yy):
            return _mlp_tile(x1, gain, shift, scale, gate2, w_gu_v, w_dn_v, t, p_gu, p_yy)
        zero_gu = jnp.zeros((bt, 2 * FFN), F32)
        zero_yy = jnp.zeros((bt, D), F32)
        loss, vjp, (h2, act) = jax.vjp(f, x1_ref[...], g_ref[...], mod_ref[:, 3 * D:4 * D], mod_ref[:, 4 * D:5 * D],
                                       mod_ref[:, 5 * D:6 * D], zero_gu, zero_yy, has_aux=True)
        dx1, dgain, dshift, dscale, dgate2, dgu, dyy = vjp(jnp.ones((), F32))
        dx1_ref[...] = dx1
        h2_ref[...] = h2.astype(BF)
        act_ref[...] = act.astype(BF)
        dgu_ref[...] = dgu.astype(BF)
        dyy_ref[...] = dyy.astype(BF)
        first = pl.program_id(1) == 0
        _acc(loss_ref, jnp.reshape(loss, (1, 1)), first)
        _acc(dmod_ref, jnp.concatenate([dshift, dscale, dgate2], axis=-1), first)
        _acc(dg_ref, dgain, first)

    return pl.pallas_call(
        body, name="mlp", grid=(B, S // bt),
        in_specs=[_tok(bt, D), _tok(bt, D), _per_batch(6 * D), _full((1, D)), _full((D, 2 * FFN)), _full((FFN, D))],
        out_specs=[_tok(bt, D), _tok(bt, D), _tok(bt, FFN), _tok(bt, 2 * FFN), _tok(bt, D),
                   _per_batch(1), _per_batch(3 * D), _per_batch(D)],
        out_shape=[_sds((B, S, D), F32), _sds((B, S, D), BF), _sds((B, S, FFN), BF), _sds((B, S, 2 * FFN), BF),
                   _sds((B, S, D), BF), _sds((B, 1, 1), F32), _sds((B, 1, 3 * D), F32), _sds((B, 1, D), F32)],
        compiler_params=_cparams(dimension_semantics=("parallel", "arbitrary")),
    )(x1, tgt, mod, norm2_g, w_gu, w_dn)


def _mix_bwd(o_attn, o_raw, z, ga, gb, x, mod, dn_g, w_branch, w_out, dx1, bt):
    B, S, _ = x.shape

    def body(oa_ref, or_ref, z_ref, ga_ref, gb_ref, x_ref, mod_ref, dg_ref, wb_ref, wo_ref, dx1_ref,
             doa_ref, dor_ref, dz_ref, dga_ref, dgb_ref, dya_ref, dyd_ref, dout_ref, dgate_ref, ddg_ref):
        oa, o_r, zs = _mix_load(oa_ref, or_ref, z_ref)
        wb_a, wb_d, wo = wb_ref[0:QW, :], wb_ref[QW:2 * QW, :], wo_ref[...]

        def f(oa, o_r, zs, ga, gb, gate1, dn_g, p_ya, p_yd, p_out):
            return _mix_tile(oa, o_r, zs, ga, gb, x_ref[...], gate1, dn_g, wb_a, wb_d, wo, p_ya, p_yd, p_out)[0]
        zero = jnp.zeros((bt, D), F32)
        _, vjp = jax.vjp(f, oa, o_r, zs, ga_ref[...].astype(F32), gb_ref[...].astype(F32), mod_ref[:, 2 * D:3 * D],
                         dg_ref[...], zero, zero, zero)
        doa, dor, dzs, dga, dgb, dgate1, ddn_g, dya, dyd, dout = vjp(dx1_ref[...])
        doa_ref[...] = doa
        for h in range(DN_H):
            dor_ref[:, h * DN_D:(h + 1) * DN_D] = dor[h]
            dz_ref[:, h * DN_D:(h + 1) * DN_D] = dzs[h].astype(BF)
        dga_ref[...] = dga.astype(BF)
        dgb_ref[...] = dgb.astype(BF)
        dya_ref[...] = dya.astype(BF)
        dyd_ref[...] = dyd.astype(BF)
        dout_ref[...] = dout.astype(BF)
        first = pl.program_id(1) == 0
        _acc(dgate_ref, dgate1, first)
        _acc(ddg_ref, ddn_g, first)

    return pl.pallas_call(
        body, name="mix_bwd", grid=(B, S // bt),
        in_specs=[_tok(bt, QW), _tok(bt, DNW), _tok(bt, DNW), _tok(bt, D), _tok(bt, D), _tok(bt, D), _per_batch(6 * D),
                  _full((1, DN_D)), _full((D, D)), _full((D, D)), _tok(bt, D)],
        out_specs=[_tok(bt, QW), _tok(bt, DNW), _tok(bt, DNW), _tok(bt, D), _tok(bt, D), _tok(bt, D), _tok(bt, D), _tok(bt, D),
                   _per_batch(D), _per_batch(DN_D)],
        out_shape=[_sds((B, S, QW), F32), _sds((B, S, DNW), F32), _sds((B, S, DNW), BF), _sds((B, S, D), BF),
                   _sds((B, S, D), BF), _sds((B, S, D), BF), _sds((B, S, D), BF), _sds((B, S, D), BF),
                   _sds((B, 1, D), F32), _sds((B, 1, DN_D), F32)],
        compiler_params=_cparams(dimension_semantics=("parallel", "arbitrary")),
    )(o_attn, o_raw, z, ga, gb, x, mod, dn_g, w_branch, w_out, dx1)


def _dn_rec_bwd(u, w, qd, kd, a, cd, states, d_o):
    B, S, _ = u.shape
    nc = S // CHUNK

    def rev(f):
        return pl.BlockSpec((None, CHUNK, f), lambda b, i: (b, nc - 1 - i, 0))

    cd_rev = pl.BlockSpec((None, 1, 1, DN_D), lambda b, i: (b, nc - 1 - i, 0, 0))

    def body(u_ref, w_ref, qd_ref, kd_ref, a_ref, cd_ref, st_ref, do_ref,
             du_ref, dw_ref, dqd_ref, dkd_ref, da_ref, dcd_ref, ds_ref):
        @pl.when(pl.program_id(1) == 0)
        def _():
            ds_ref[...] = jnp.zeros_like(ds_ref)
        state = [st_ref[h] for h in range(DN_H)]
        _, vjp = jax.vjp(_dn_rec, state, *_rec_load(u_ref, w_ref, qd_ref, kd_ref, a_ref), cd_ref[0])
        d_state = [ds_ref[h] for h in range(DN_H)]
        d_outs = [do_ref[:, h * DN_D:(h + 1) * DN_D] for h in range(DN_H)]
        dst, du, dw, dqd, dkd, da, dcd = vjp((d_state, d_outs))
        for h in range(DN_H):
            cols = slice(h * DN_D, (h + 1) * DN_D)
            ds_ref[h] = dst[h]
            du_ref[:, cols] = du[h]
            dw_ref[:, cols] = dw[h]
            dqd_ref[:, cols] = dqd[h]
            dkd_ref[:, cols] = dkd[h]
            da_ref[:, h * CHUNK:(h + 1) * CHUNK] = da[h]
        dcd_ref[0] = dcd

    return pl.pallas_call(
        body, name="dn_rec_bwd", grid=(B, nc),
        in_specs=[rev(DNW)] * 4 + [rev(AW), cd_rev,
                  pl.BlockSpec((None, None, DN_H, DN_D, DN_D), lambda b, i: (b, nc - 1 - i, 0, 0, 0)), rev(DNW)],
        out_specs=[rev(DNW)] * 4 + [rev(AW), cd_rev],
        out_shape=[_sds((B, S, DNW), F32)] * 4 + [_sds((B, S, AW), F32), _sds((B, nc, 1, DN_D), F32)],
        scratch_shapes=[pltpu.VMEM((DN_H, DN_D, DN_D), F32)],
        compiler_params=_cparams(dimension_semantics=("parallel", "arbitrary")),
    )(u, w, qd, kd, a, cd, states, d_o)


def _dn_intra_bwd(qkv, bg, du, dw, dqd, dkd, da, dcd):
    B, S, _ = qkv.shape
    nc = S // CHUNK
    G = min(DN_GROUP, nc)
    rows = G * CHUNK

    def body(qkv_ref, bg_ref, du_ref, dw_ref, dqd_ref, dkd_ref, da_ref, dcd_ref, dqkv_ref, dbg_ref):
        q, k, v = _dn_load_stack(qkv_ref, G)
        _, vjp = jax.vjp(_dn_intra, q, k, v, bg_ref[...].reshape(G, CHUNK, DN_D))
        lane_row = _iota((1, DN_D), 1)
        dcd = jnp.stack([jnp.sum(jnp.where(lane_row == h, dcd_ref[g], 0.0), axis=-1, keepdims=True)
                         for g in range(G) for h in range(DN_H)])
        dq, dk, dv, dbg = vjp((_stack_heads(du_ref, G, 0, DN_D), _stack_heads(dw_ref, G, 0, DN_D),
                               _stack_heads(dqd_ref, G, 0, DN_D), _stack_heads(dkd_ref, G, 0, DN_D),
                               _stack_heads(da_ref, G, 0, CHUNK), dcd))
        for g in range(G):
            rows = slice(g * CHUNK, (g + 1) * CHUNK)
            for h in range(DN_H):
                n = g * DN_H + h
                dqkv_ref[rows, h * DN_D:(h + 1) * DN_D] = dq[n]
                dqkv_ref[rows, DNW + h * DN_D:DNW + (h + 1) * DN_D] = dk[n]
                dqkv_ref[rows, 2 * DNW + h * DN_D:2 * DNW + (h + 1) * DN_D] = dv[n]
        dbg_ref[...] = dbg.reshape(G * CHUNK, DN_D)

    return pl.pallas_call(
        body, name="dn_intra_bwd", grid=(B, nc // G),
        in_specs=[_tok(rows, CONVW), _tok(rows, 128)] + [_tok(rows, DNW)] * 4 + [_tok(rows, AW), _cd_spec(G)],
        out_specs=[_tok(rows, CONVW), _tok(rows, 128)],
        out_shape=[_sds((B, S, CONVW), F32), _sds((B, S, 128), F32)],
        compiler_params=_cparams(dimension_semantics=("parallel", "parallel")),
    )(qkv, bg, du, dw, dqd, dkd, da, dcd)


def _dn_prep_bwd(dn, ba, conv_w, alog, dtb, dqkv, dbg, bt):
    B, S, _ = dn.shape
    nt = S // bt

    def rev(f):
        return pl.BlockSpec((None, bt, f), lambda b, i: (b, nt - 1 - i, 0))

    halo = pl.BlockSpec((None, 8, CONVW), lambda b, i: (b, jnp.maximum((nt - 1 - i) * (bt // 8) - 1, 0), 0))

    def body(x_ref, halo_ref, ba_ref, cw_ref, al_ref, dt_ref, dqkv_ref, dbg_ref,
             dx_ref, dba_ref, dcw_ref, dal_ref, ddt_ref, xe_ref, dye_ref):
        i = pl.program_id(1)
        y = _conv_fwd_tile(xe_ref, x_ref, halo_ref, cw_ref, i == nt - 1, bt)
        ys = [y[:, j * DN_D:(j + 1) * DN_D] for j in range(3 * DN_H)]
        _, vjp = jax.vjp(_dn_tail, ys, ba_ref[...], al_ref[...], dt_ref[...])
        d_out = [dqkv_ref[:, j * DN_D:(j + 1) * DN_D] for j in range(3 * DN_H)]
        dys, dba, dal, ddt = vjp((d_out, dbg_ref[...]))
        @pl.when(i == 0)
        def _():
            dye_ref[bt:bt + 8, :] = jnp.zeros((8, CONVW), F32)

        @pl.when(i > 0)
        def _():
            dye_ref[bt:bt + 8, :] = dye_ref[0:8, :]
        for j in range(3 * DN_H):
            dye_ref[0:bt, j * DN_D:(j + 1) * DN_D] = dys[j]
        dx = cw_ref[0:1, :] * dye_ref[3:bt + 3, :]
        for j in range(1, CONV):
            dx = dx + cw_ref[j:j + 1, :] * dye_ref[3 - j:bt + 3 - j, :]
        dx_ref[...] = dx.astype(BF)
        dy = dye_ref[0:bt, :]
        dcw = jnp.concatenate([jnp.sum(dy * xe_ref[5 + j:bt + 5 + j, :], axis=0, keepdims=True) for j in range(CONV)], axis=0)
        first = (i == 0) & (pl.program_id(0) == 0)
        dba_ref[...] = dba
        _acc(dcw_ref, dcw, first)
        _acc(dal_ref, dal, first)
        _acc(ddt_ref, ddt, first)

    return pl.pallas_call(
        body, name="dn_prep_bwd", grid=(B, nt),
        in_specs=[rev(CONVW), halo, rev(128), _full((CONV, CONVW)), _full((1, 128)), _full((1, 128)), rev(CONVW), rev(128)],
        out_specs=[rev(CONVW), rev(128), _full((CONV, CONVW)), _full((1, 128)), _full((1, 128))],
        out_shape=[_sds((B, S, CONVW), BF), _sds((B, S, 128), F32), _sds((CONV, CONVW), F32), _sds((1, 128), F32),
                   _sds((1, 128), F32)],
        scratch_shapes=[pltpu.VMEM((bt + 8, CONVW), F32), pltpu.VMEM((bt + 8, CONVW), F32)],
        compiler_params=_cparams(dimension_semantics=("arbitrary", "arbitrary")),
    )(dn, dn, ba, conv_w, alog, dtb, dqkv, dbg)


def _attn_bwd(q, kv, rope_t, q_norm_g, k_norm_g, sinks, d_o):
    B, S, _ = q.shape
    nb = S // BLK

    def cur(f):
        return pl.BlockSpec((None, BLK, f), lambda b, i: (b, jnp.minimum(i, nb - 1), 0))

    def prev(f):
        return pl.BlockSpec((None, BLK, f), lambda b, i: (b, jnp.maximum(jnp.minimum(i, nb - 1) - 1, 0), 0))

    def out_prev(f):
        return pl.BlockSpec((None, BLK, f), lambda b, i: (b, jnp.maximum(i - 1, 0), 0))

    def body(q_ref, kvc_ref, kvp_ref, rc_ref, rp_ref, qg_ref, kg_ref, sk_ref, do_ref,
             dq_ref, dkv_ref, dqg_ref, dkg_ref, dsk_ref, carry_ref):
        n = pl.program_id(1)
        first = (n == 0) & (pl.program_id(0) == 0)

        @pl.when(n < nb)
        def _():
            qs, kc, kp, vc, vp = _attn_load(q_ref, kvc_ref, kvp_ref)
            cos_c, sin_c, cos_p, sin_p = rc_ref[:, 0:HD], rc_ref[:, HD:2 * HD], rp_ref[:, 0:HD], rp_ref[:, HD:2 * HD]

            def f(qs, kc, kp, vc, vp, qg, kg, sk):
                return _attn_block(qs, kc, kp, vc, vp, qg, kg, sk, cos_c, sin_c, cos_p, sin_p, n > 0)
            _, vjp = jax.vjp(f, qs, kc, kp, vc, vp, qg_ref[...], kg_ref[...], sk_ref[...])
            d_outs = [do_ref[:, h * HD:(h + 1) * HD] for h in range(HEADS)]
            dqs, dkc, dkp, dvc, dvp, dqg, dkg, dsk = vjp(d_outs)
            for h in range(HEADS):
                dq_ref[:, h * HD:(h + 1) * HD] = dqs[h].astype(BF)
            for h in range(KV_HEADS):
                ksl = slice(h * HD, (h + 1) * HD)
                vsl = slice(KVW + h * HD, KVW + (h + 1) * HD)
                dkv_ref[:, ksl] = (carry_ref[:, ksl] + dkp[h]).astype(BF)
                dkv_ref[:, vsl] = (carry_ref[:, vsl] + dvp[h]).astype(BF)
                carry_ref[:, ksl] = dkc[h]
                carry_ref[:, vsl] = dvc[h]
            _acc(dqg_ref, dqg, first)
            _acc(dkg_ref, dkg, first)
            _acc(dsk_ref, dsk, first)

        @pl.when(n == nb)
        def _():
            dkv_ref[...] = carry_ref[...].astype(BF)

    return pl.pallas_call(
        body, name="attn_bwd", grid=(B, nb + 1),
        in_specs=[cur(QW), cur(2 * KVW), prev(2 * KVW), cur(2 * HD), prev(2 * HD),
                  _full((1, HD)), _full((1, HD)), _full((1, HEADS)), cur(QW)],
        out_specs=[cur(QW), out_prev(2 * KVW), _full((1, HD)), _full((1, HD)), _full((1, HEADS))],
        out_shape=[_sds((B, S, QW), BF), _sds((B, S, 2 * KVW), BF), _sds((1, HD), F32), _sds((1, HD), F32),
                   _sds((1, HEADS), F32)],
        scratch_shapes=[pltpu.VMEM((BLK, 2 * KVW), F32)],
        compiler_params=_cparams(dimension_semantics=("arbitrary", "arbitrary")),
    )(q, kv, kv, rope_t, rope_t, q_norm_g, k_norm_g, sinks, d_o)


def _in_proj_bwd(x, mod, norm1_g, w_in, pieces, dba, dx1, bt):
    B, S, _ = x.shape
    widths = (QW, 2 * KVW, CONVW, DNW, D, D)

    def body(x_ref, mod_ref, g_ref, w_ref, dq_ref, dkv_ref, ddn_ref, dz_ref, dga_ref, dgb_ref, dba_ref, dx1_ref,
             gx_ref, dp_ref, dmod_ref, dg_ref):
        dp = jnp.concatenate([r[...] for r in (dq_ref, dkv_ref, ddn_ref, dz_ref, dga_ref, dgb_ref)]
                             + [dba_ref[...].astype(BF)], axis=-1)
        dp_ref[...] = dp
        dh = lax.dot_general(dp, w_ref[...], (((1,), (1,)), ((), ())), preferred_element_type=F32)
        _, vjp = jax.vjp(_norm_mod, x_ref[...], g_ref[...], mod_ref[:, 0:D], mod_ref[:, D:2 * D])
        dx, dgain, dshift, dscale = vjp(dh)
        gx_ref[...] = dx + dx1_ref[...]
        first = pl.program_id(1) == 0
        _acc(dmod_ref, jnp.concatenate([dshift, dscale], axis=-1), first)
        _acc(dg_ref, dgain, first)

    return pl.pallas_call(
        body, name="in_proj_bwd", grid=(B, S // bt),
        in_specs=[_tok(bt, D), _per_batch(6 * D), _full((1, D)), _full((D, IN_PAD))] + [_tok(bt, w) for w in widths]
        + [_tok(bt, 128), _tok(bt, D)],
        out_specs=[_tok(bt, D), _tok(bt, IN_PAD), _per_batch(2 * D), _per_batch(D)],
        out_shape=[_sds((B, S, D), F32), _sds((B, S, IN_PAD), BF), _sds((B, 1, 2 * D), F32), _sds((B, 1, D), F32)],
        compiler_params=_cparams(dimension_semantics=("parallel", "arbitrary")),
    )(x, mod, norm1_g, w_in, *pieces, dba, dx1)


def _matmul_tn(tag, a, b, bk, bn, bt):
    T, K = a.shape
    N = b.shape[1]
    nt = T // bt

    def body(a_ref, b_ref, o_ref, acc_ref):
        t = pl.program_id(2)

        @pl.when(t == 0)
        def _():
            acc_ref[...] = jnp.zeros_like(acc_ref)
        acc_ref[...] += lax.dot_general(a_ref[...], b_ref[...], (((0,), (0,)), ((), ())), preferred_element_type=F32)

        @pl.when(t == nt - 1)
        def _():
            o_ref[...] = acc_ref[...]

    return pl.pallas_call(
        body, name=f"grad_{tag}", grid=(K // bk, N // bn, nt),
        in_specs=[pl.BlockSpec((bt, bk), lambda i, j, t: (t, i)), pl.BlockSpec((bt, bn), lambda i, j, t: (t, j))],
        out_specs=pl.BlockSpec((bk, bn), lambda i, j, t: (i, j)),
        out_shape=_sds((K, N), F32),
        scratch_shapes=[pltpu.VMEM((bk, bn), F32)],
        compiler_params=_cparams(dimension_semantics=("parallel", "parallel", "arbitrary")),
    )(a, b)


def _rope_table(positions):
    inv_freq = THETA ** (-jnp.arange(0, ROT, 2, dtype=F32) / ROT)
    ang = positions.astype(F32)[..., None] * inv_freq
    cos, sin = jnp.cos(ang), jnp.sin(ang)
    ones = jnp.ones(ang.shape[:-1] + (HD - ROT,), F32)
    cos_full = jnp.concatenate([cos, cos, ones], axis=-1)
    sin_full = jnp.concatenate([-sin, sin, 0.0 * ones], axis=-1)
    return jnp.concatenate([cos_full, sin_full], axis=-1)


def _lane_pad(v, offset, width=128):
    return jnp.zeros((1, width), F32).at[0, offset:offset + v.shape[-1]].set(v.reshape(-1))


def _tile(S, want):
    return min(S, want)


def _local_step(x, mod, positions, tgt, norm1_g, w_in_pad, conv_w, q_norm_g, k_norm_g, sinks, a_log, dt_bias,
                dn_norm_g, w_branch, w_out, norm2_g, w_gu, w_dn):
    B, S, _ = x.shape
    T = B * S
    rope_t = _rope_table(positions)
    alog = _lane_pad(a_log, DN_H)
    dtb = _lane_pad(dt_bias, DN_H)
    conv2 = conv_w.reshape(CONV, CONVW)
    bt = _tile(S, 512)
    bt_mlp = _tile(S, 256)

    q, kv, dn, z, ga, gb, ba, h1 = _in_proj(x, mod, norm1_g, w_in_pad, bt)
    o_attn = _attn_fwd(q, kv, rope_t, q_norm_g, k_norm_g, sinks)
    dqkv, bg = _dn_prep(dn, ba, conv2, alog, dtb, bt)
    dn_u, dn_w, dn_qd, dn_kd, dn_a, dn_cd = _dn_intra_fwd(dqkv, bg)
    o_raw, states = _dn_rec_fwd(dn_u, dn_w, dn_qd, dn_kd, dn_a, dn_cd)
    x1, o_dn, merged = _mix_fwd(o_attn, o_raw, z, ga, gb, x, mod, dn_norm_g, w_branch, w_out, bt)
    dx1, h2, act, dgu, dyy, loss, dmod2, dnorm2 = _mlp(x1, tgt, mod, norm2_g, w_gu, w_dn, bt_mlp)

    def flat(t):
        return t.reshape(T, t.shape[-1])
    tn = functools.partial(_matmul_tn, bt=_tile(T, 512))
    g_w_dn = tn("w_down", flat(act), flat(dyy), bk=FFN, bn=D // 2)
    g_w_gu = tn("w_gate_up", flat(h2), flat(dgu), bk=D, bn=2 * FFN // 4)

    d_oa, d_or, dz, dga, dgb, dya, dyd, dout, dgate1, ddn_g = _mix_bwd(
        o_attn, o_raw, z, ga, gb, x, mod, dn_norm_g, w_branch, w_out, dx1, bt_mlp)
    g_w_out = tn("w_out", flat(merged), flat(dout), bk=D, bn=D)
    g_w_br = jnp.concatenate([tn("w_branch_attn", flat(o_attn), flat(dya), bk=QW, bn=D),
                              tn("w_branch_dn", flat(o_dn), flat(dyd), bk=DNW, bn=D)], axis=0)

    d_rec = _dn_rec_bwd(dn_u, dn_w, dn_qd, dn_kd, dn_a, dn_cd, states, d_or)
    d_dqkv, dbg = _dn_intra_bwd(dqkv, bg, *d_rec)
    d_dn, dba, dconv, dalog, ddtb = _dn_prep_bwd(dn, ba, conv2, alog, dtb, d_dqkv, dbg, bt)
    dq, dkv, dqg, dkg, dsk = _attn_bwd(q, kv, rope_t, q_norm_g, k_norm_g, sinks, d_oa)
    grad_x, dproj, dmod1, dnorm1 = _in_proj_bwd(x, mod, norm1_g, w_in_pad, (dq, dkv, d_dn, dz, dga, dgb), dba, dx1, bt)
    g_w_in = tn("w_in", flat(h1), flat(dproj), bk=D, bn=IN_PAD // 3)

    dmod = jnp.concatenate([dmod1, dgate1, dmod2], axis=-1)
    small = dict(norm1_g=jnp.sum(dnorm1, axis=0), norm2_g=jnp.sum(dnorm2, axis=0), q_norm_g=dqg, k_norm_g=dkg,
                 sinks=dsk, a_log=dalog[:, DN_H:2 * DN_H], dt_bias=ddtb[:, DN_H:2 * DN_H],
                 dn_norm_g=jnp.sum(ddn_g, axis=0), conv_w=dconv)
    return jnp.sum(loss), grad_x, dmod, small, (g_w_in, g_w_br, g_w_out, g_w_gu, g_w_dn)


def _me():
    return lax.axis_index("x"), lax.axis_index("y"), lax.axis_index("c")


def _flip(me, f):
    return (me[0] ^ ((f >> 2) & 1), me[1] ^ ((f >> 1) & 1), me[2] ^ (f & 1))


def _comm_call(name, ins, out_shapes, n_remote, plan):
    n_in = len(ins)
    n_out = len(out_shapes)

    def body(*refs):
        in_refs, out_refs = refs[:n_in], refs[n_in:n_in + n_out]
        send_sems, recv_sems = refs[n_in + n_out:]
        remote = plan(_me(), in_refs, out_refs)
        assert len(remote) == n_remote
        cps = [pltpu.make_async_remote_copy(src_ref=src, dst_ref=dst, send_sem=send_sems.at[i], recv_sem=recv_sems.at[i],
                                            device_id=peer, device_id_type=pl.DeviceIdType.MESH)
               for i, (src, dst, peer) in enumerate(remote)]
        for cp in cps:
            cp.start()
        for cp in cps:
            cp.wait_recv()
        for cp in cps:
            cp.wait_send()

    any_spec = pl.BlockSpec(memory_space=pl.ANY)
    return pl.pallas_call(
        body, name=name, in_specs=[any_spec] * n_in, out_specs=[any_spec] * n_out, out_shape=out_shapes,
        scratch_shapes=[pltpu.SemaphoreType.DMA((n_remote,)), pltpu.SemaphoreType.DMA((n_remote,))],
    )(*ins)


def _by_origin(own, received, index):
    stack = jnp.concatenate([own[None], received], axis=0)
    n = stack.shape[0]
    return jnp.stack([lax.dynamic_index_in_dim(stack, k ^ index, 0, keepdims=False) for k in range(n)])


def _gather_devices(name, arrs, dev):
    def plan(me, in_refs, out_refs):
        return [(a, o.at[f - 1], _flip(me, f)) for a, o in zip(in_refs, out_refs) for f in range(1, N_DEV)]
    outs = [_sds((N_DEV - 1,) + a.shape, a.dtype) for a in arrs]
    got = _comm_call(name, arrs, outs, (N_DEV - 1) * len(arrs), plan)
    return [_by_origin(a, g, dev) for a, g in zip(arrs, got)]


def _gather_chips(name, arrs, chip):
    def plan(me, in_refs, out_refs):
        return [(a, o.at[j], _flip(me, 2 * (j + 1))) for a, o in zip(in_refs, out_refs) for j in range(N_CHIP - 1)]
    outs = [_sds((N_CHIP - 1,) + a.shape, a.dtype) for a in arrs]
    got = _comm_call(name, arrs, outs, (N_CHIP - 1) * len(arrs), plan)
    return [_by_origin(a, g, chip) for a, g in zip(arrs, got)]


def _halves(core, mine, other):
    lo = jnp.where(core == 0, mine, other)
    hi = jnp.where(core == 0, other, mine)
    return jnp.concatenate([lo, hi], axis=-2)


def _gather_weights(shards, chip, core):
    def plan_ici(me, in_refs, out_refs):
        remote = []
        for a, o in zip(in_refs, out_refs):
            half = a.shape[0] // 2
            mine = a.at[pl.ds(me[2] * half, half)]
            remote += [(mine, o.at[j], _flip(me, 2 * (j + 1))) for j in range(N_CHIP - 1)]
        return remote
    mine = _comm_call("weights_ici", shards, [_sds((N_CHIP - 1, a.shape[0] // 2, a.shape[1]), a.dtype) for a in shards],
                      (N_CHIP - 1) * len(shards), plan_ici)

    def plan_d2d(me, in_refs, out_refs):
        return [(g, o, _flip(me, 1)) for g, o in zip(in_refs, out_refs)]
    other = _comm_call("weights_d2d", mine, [_sds(g.shape, g.dtype) for g in mine], len(mine), plan_d2d)
    return [_by_origin(a, _halves(core, g, h), chip) for a, g, h in zip(shards, mine, other)]


def _rows(r):
    for br in (512, 352, 256, 128, 64, 32, 16, 8):
        if r % br == 0:
            return br
    raise ValueError(r)


def _pair_add(tag, g, recv, c):
    n, r, cols = g.shape
    half = r // 2
    br = _rows(half)
    nb = half // br

    def body(c_ref, g_ref, r_ref, o_ref):
        o_ref[...] = (g_ref[...] + r_ref[...]).astype(BF)

    return pl.pallas_call(
        body, name=f"pair_add_{tag}",
        grid_spec=pltpu.PrefetchScalarGridSpec(
            num_scalar_prefetch=1, grid=(n, nb),
            in_specs=[pl.BlockSpec((None, br, cols), lambda k, i, c_ref: (k, c_ref[0] * nb + i, 0)),
                      pl.BlockSpec((None, br, cols), lambda k, i, c_ref: (k, i, 0))],
            out_specs=pl.BlockSpec((None, br, cols), lambda k, i, c_ref: (k, i, 0))),
        out_shape=_sds((n, half, cols), BF),
        compiler_params=_cparams(dimension_semantics=("parallel", "parallel")),
    )(c, g, recv)


def _sum_chips(tag, p, q, chip):
    n, r, cols = q.shape
    br = _rows(r)

    def body(chip_ref, p_ref, q_ref, o_ref):
        acc = p_ref[...].astype(F32)
        for k in range(n):
            acc = acc + q_ref[k].astype(F32)
        o_ref[...] = acc

    return pl.pallas_call(
        body, name=f"sum_chips_{tag}",
        grid_spec=pltpu.PrefetchScalarGridSpec(
            num_scalar_prefetch=1, grid=(r // br,),
            in_specs=[pl.BlockSpec((None, br, cols), lambda i, chip_ref: (chip_ref[0], i, 0)),
                      pl.BlockSpec((n, br, cols), lambda i, chip_ref: (0, i, 0))],
            out_specs=pl.BlockSpec((br, cols), lambda i, chip_ref: (i, 0))),
        out_shape=_sds((r, cols), F32),
        compiler_params=_cparams(dimension_semantics=("parallel",)),
    )(chip, p, q)


def _reduce_grads(tags, grads, chip, core):
    core_arr = core.reshape(1).astype(jnp.int32)
    chip_arr = chip.reshape(1).astype(jnp.int32)

    def plan_pair(me, in_refs, out_refs):
        remote = []
        for g, o in zip(in_refs, out_refs):
            half = g.shape[1] // 2
            remote += [(g.at[k, pl.ds((1 - me[2]) * half, half)], o.at[k], _flip(me, 1)) for k in range(N_CHIP)]
        return remote
    recv = _comm_call("grads_pair", grads, [_sds((N_CHIP, g.shape[1] // 2, g.shape[2]), F32) for g in grads],
                      N_CHIP * len(grads), plan_pair)
    pair = [_pair_add(t, g, r, core_arr) for t, g, r in zip(tags, grads, recv)]

    def plan_chips(me, in_refs, out_refs):
        remote = []
        for p, o in zip(in_refs, out_refs):
            for j in range(N_CHIP - 1):
                peer = _flip(me, 2 * (j + 1))
                remote.append((p.at[2 * peer[0] + peer[1]], o.at[j], peer))
        return remote
    parts = _comm_call("grads_chips", pair, [_sds((N_CHIP - 1,) + p.shape[1:], BF) for p in pair],
                       (N_CHIP - 1) * len(pair), plan_chips)
    mine = [_sum_chips(t, p, q, chip_arr) for t, p, q in zip(tags, pair, parts)]

    def plan_swap(me, in_refs, out_refs):
        return [(h, o, _flip(me, 1)) for h, o in zip(in_refs, out_refs)]
    other = _comm_call("grads_swap", mine, [_sds(h.shape, F32) for h in mine], len(mine), plan_swap)
    return [_halves(core, h, o) for h, o in zip(mine, other)]


def _adamw_math(w, g, m, v):
    m = ADAM_B1 * m + (1.0 - ADAM_B1) * g
    v = ADAM_B2 * v + (1.0 - ADAM_B2) * (g * g)
    m_hat = m / (1.0 - ADAM_B1 ** ADAM_STEP)
    v_hat = v / (1.0 - ADAM_B2 ** ADAM_STEP)
    delta = -ADAM_LR * (m_hat / (jnp.sqrt(v_hat) + ADAM_EPS) + ADAM_WD * w)
    return delta, m, v


def _adamw(name, w, g, m, v):
    r, cols = w.shape
    br = _rows(r)
    if br * cols * 4 > (1 << 20) and br % 16 == 0:
        br //= 2

    def body(w_ref, g_ref, m_ref, v_ref, d_ref, mo_ref, vo_ref):
        d_ref[...], mo_ref[...], vo_ref[...] = _adamw_math(w_ref[...], g_ref[...], m_ref[...], v_ref[...])

    spec = pl.BlockSpec((br, cols), lambda i: (i, 0))
    return pl.pallas_call(
        body, name=f"adamw_{name}", grid=(r // br,), in_specs=[spec] * 4, out_specs=[spec] * 3,
        out_shape=[_sds((r, cols), F32)] * 3,
        compiler_params=_cparams(dimension_semantics=("parallel",)),
    )(w, g, m, v)


def _ada_fwd(c_all, ada_w, ada_b_cols):
    n = c_all.shape[0]

    def body(c_ref, w_ref, b_ref, o_ref):
        o_ref[...] = _mmx(_silu(c_ref[...]), w_ref[...]) + b_ref[...]

    return pl.pallas_call(
        body, name="ada_fwd", out_shape=_sds((n, ada_w.shape[1]), F32), compiler_params=_cparams(),
    )(c_all, ada_w, ada_b_cols)


def _ada_bwd(c_all, dmod_cols, w, m, v):
    n = c_all.shape[0]
    r, cols = w.shape
    br = 128

    def body(c_ref, d_ref, w_ref, m_ref, v_ref, g_ref, dl_ref, mo_ref, vo_ref):
        cond = _silu(c_ref[...])
        g = lax.dot_general(cond, d_ref[...], (((0,), (0,)), ((), ())), precision=lax.Precision.HIGHEST,
                            preferred_element_type=F32)
        g_ref[...] = g
        dl_ref[...], mo_ref[...], vo_ref[...] = _adamw_math(w_ref[...], g, m_ref[...], v_ref[...])

    spec = pl.BlockSpec((br, cols), lambda i: (i, 0))
    return pl.pallas_call(
        body, name="ada_bwd", grid=(r // br,),
        in_specs=[pl.BlockSpec((n, br), lambda i: (0, i)), pl.BlockSpec((n, cols), lambda i: (0, 0)), spec, spec, spec],
        out_specs=[spec] * 4, out_shape=[_sds((r, cols), F32)] * 4,
        compiler_params=_cparams(dimension_semantics=("parallel",)),
    )(c_all, dmod_cols, w, m, v)


def _sum_devices(parts):
    n, r, cols = parts.shape

    def body(p_ref, o_ref):
        acc = p_ref[0]
        for k in range(1, n):
            acc = acc + p_ref[k]
        o_ref[...] = acc

    return pl.pallas_call(body, name="sum_devices", out_shape=_sds((r, cols), F32), compiler_params=_cparams())(parts)


SMALL_ROWS = 16
_SMALL_SLOTS = dict(norm1_g=(0, 0, D), norm2_g=(1, 0, D), q_norm_g=(2, 0, HD), k_norm_g=(2, 128, HD), sinks=(2, 256, HEADS),
                    a_log=(2, 384, DN_H), dt_bias=(2, 512, DN_H), dn_norm_g=(2, 640, DN_D))
_CONV_ROW = 4
_ADA_B_ROW = 8


def _pack_small(vals, conv, ada_b):
    sheet = jnp.zeros((SMALL_ROWS, CONVW), F32)
    for name, (row, col, n) in _SMALL_SLOTS.items():
        sheet = sheet.at[row, col:col + n].set(vals[name].reshape(n))
    sheet = sheet.at[_CONV_ROW:_CONV_ROW + CONV, 0:conv.shape[1]].set(conv)
    return sheet.at[_ADA_B_ROW:_ADA_B_ROW + 4, :].set(ada_b.reshape(4, CONVW))


def _unpack_small(sheet, conv_cols):
    out = {name: sheet[row, col:col + n].reshape(1, n) for name, (row, col, n) in _SMALL_SLOTS.items()}
    out["conv_w"] = sheet[_CONV_ROW:_CONV_ROW + CONV, 0:conv_cols].reshape(1, CONV, 1, conv_cols)
    out["ada_b"] = sheet[_ADA_B_ROW:_ADA_B_ROW + 4, :].reshape(1, 6 * D)
    return out


def _pad_w_in(w):
    return jnp.concatenate([w[:, :C_Z], w[:, C_Z + 2 * DN_H:IN_WIDTH], w[:, C_Z:C_Z + 2 * DN_H],
                            jnp.zeros((w.shape[0], IN_PAD - IN_WIDTH), w.dtype)], axis=1)


def _unpad_w_in(g):
    return jnp.concatenate([g[:, :C_Z], g[:, C_BA:C_BA + 2 * DN_H], g[:, C_Z:C_BA]], axis=1)


def _cols_to_blocks(g):
    r = g.shape[0]
    return g.reshape(r, N_CHIP, g.shape[1] // N_CHIP).transpose(1, 0, 2)


def _blocks_to_cols(f):
    return f.transpose(1, 0, 2).reshape(f.shape[1], N_CHIP * f.shape[2])


def kernel(x, c, positions, ada_w, ada_b, norm1_g, w_in, conv_w, q_norm_g, k_norm_g, sinks, a_log, dt_bias, dn_norm_g, w_branch, w_out, norm2_g, w_gate_up, w_down, loss_target, m_ada_w, m_ada_b, m_norm1_g, m_w_in, m_conv_w, m_q_norm_g, m_k_norm_g, m_sinks, m_a_log, m_dt_bias, m_dn_norm_g, m_w_branch, m_w_out, m_norm2_g, m_w_gate_up, m_w_down, v_ada_w, v_ada_b, v_norm1_g, v_w_in, v_conv_w, v_q_norm_g, v_k_norm_g, v_sinks, v_a_log, v_dt_bias, v_dn_norm_g, v_w_branch, v_w_out, v_norm2_g, v_w_gate_up, v_w_down):
    ix, iy, ic = lax.axis_index("x"), lax.axis_index("y"), lax.axis_index("c")
    dev = 4 * ix + 2 * iy + ic
    chip = 2 * ix + iy
    n_seq = x.shape[0]
    conv_cols = conv_w.shape[-1]

    c_all, conv_all = _gather_devices("gather_cond", [c, conv_w.reshape(CONV, conv_cols)], dev)
    c_all = c_all.reshape(N_DEV * n_seq, D)
    ada_cols = ada_w.shape[-1]
    ada_b_cols = lax.dynamic_slice(ada_b, (0, chip * ada_cols), (1, ada_cols))
    mod_cols = _ada_fwd(c_all, ada_w[0], ada_b_cols)
    (mod_blocks,) = _gather_chips("gather_mod", [mod_cols], chip)
    mod_all = _blocks_to_cols(mod_blocks)
    mod = lax.dynamic_slice(mod_all, (dev * n_seq, 0), (n_seq, 6 * D)).reshape(n_seq, 1, 6 * D)
    conv_full = _blocks_to_cols(conv_all[0::2])

    shards = [w_in[0].astype(BF), w_branch[0].astype(BF), w_out[0].astype(BF), w_gate_up[0].astype(BF), w_down[0].astype(BF)]
    f_in, f_br, f_out, f_gu, f_dn = _gather_weights(shards, chip, ic)
    w_in_pad = _pad_w_in(_blocks_to_cols(f_in))
    w_gu_full = _blocks_to_cols(f_gu)
    w_br_full, w_out_full, w_dn_full = (f.reshape(N_CHIP * f.shape[1], f.shape[2]) for f in (f_br, f_out, f_dn))

    loss, grad_x, dmod, small, (g_in, g_br, g_out, g_gu, g_dn) = _local_step(
        x, mod, positions, loss_target, norm1_g, w_in_pad, conv_full.reshape(CONV, 1, CONVW), q_norm_g, k_norm_g, sinks,
        a_log, dt_bias, dn_norm_g, w_br_full, w_out_full, norm2_g, w_gu_full, w_dn_full)
    loss = lax.psum(loss, ("x", "y", "c"))

    blocks = [_cols_to_blocks(_unpad_w_in(g_in)), g_br.reshape(N_CHIP, -1, D), g_out.reshape(N_CHIP, -1, D),
              _cols_to_blocks(g_gu), g_dn.reshape(N_CHIP, -1, D)]
    r_in, r_br, r_out, r_gu, r_dn = _reduce_grads(("w_in", "w_branch", "w_out", "w_gate_up", "w_down"), blocks, chip, ic)
    big = {}
    for name, w, g, m, v in (("w_in", w_in, r_in, m_w_in, v_w_in), ("w_branch", w_branch, r_br, m_w_branch, v_w_branch),
                             ("w_out", w_out, r_out, m_w_out, v_w_out), ("w_gate_up", w_gate_up, r_gu, m_w_gate_up, v_w_gate_up),
                             ("w_down", w_down, r_dn, m_w_down, v_w_down)):
        big[name] = (g,) + tuple(_adamw(name, w[0], g, m[0], v[0]))

    part = _pack_small(small, small["conv_w"], jnp.sum(dmod, axis=(0, 1)).reshape(1, 6 * D))
    dmod_all, parts = _gather_devices("gather_small", [dmod.reshape(n_seq, 6 * D), part], dev)
    dmod_all = dmod_all.reshape(N_DEV * n_seq, 6 * D)
    g_small = _unpack_small(_sum_devices(parts), CONVW)
    g_conv = lax.dynamic_slice(g_small["conv_w"].reshape(CONV, CONVW), (0, chip * conv_cols), (CONV, conv_cols))
    g_small["conv_w"] = g_conv.reshape(1, CONV, 1, conv_cols)

    given = dict(norm1_g=(norm1_g, m_norm1_g, v_norm1_g), norm2_g=(norm2_g, m_norm2_g, v_norm2_g),
                 q_norm_g=(q_norm_g, m_q_norm_g, v_q_norm_g), k_norm_g=(k_norm_g, m_k_norm_g, v_k_norm_g),
                 sinks=(sinks, m_sinks, v_sinks), a_log=(a_log, m_a_log, v_a_log), dt_bias=(dt_bias, m_dt_bias, v_dt_bias),
                 dn_norm_g=(dn_norm_g, m_dn_norm_g, v_dn_norm_g))
    sheets = [_pack_small({k: t[j] for k, t in given.items()}, cw.reshape(CONV, conv_cols), ab)
              for j, (cw, ab) in enumerate(((conv_w, ada_b), (m_conv_w, m_ada_b), (v_conv_w, v_ada_b)))]
    g_local = _pack_small(g_small, g_conv, g_small["ada_b"])
    upd = [_unpack_small(s, conv_cols) for s in _adamw("small", sheets[0], g_local, sheets[1], sheets[2])]

    dmod_cols = lax.dynamic_slice(dmod_all, (0, chip * ada_cols), (N_DEV * n_seq, ada_cols))
    ada = _ada_bwd(c_all, dmod_cols, ada_w[0], m_ada_w[0], v_ada_w[0])

    names = ["ada_w", "ada_b", "norm1_g", "w_in", "conv_w", "q_norm_g", "k_norm_g", "sinks", "a_log", "dt_bias", "dn_norm_g",
             "w_branch", "w_out", "norm2_g", "w_gate_up", "w_down"]

    def leaf(name, j):
        if name == "ada_w":
            return ada[j][None]
        if name in big:
            return big[name][j][None]
        return g_small[name] if j == 0 else upd[j - 1][name]

    return (loss, grad_x) + tuple(leaf(n, j) for j in range(4) for n in names)
```

```python
import functools

import jax
import jax.numpy as jnp
import numpy as np
from jax import lax
from jax.experimental import pallas as pl
from jax.experimental.pallas import tpu as pltpu

F32 = jnp.float32
BF = jnp.bfloat16

D = 1024
HEADS = 8
KV_HEADS = 2
GROUP = 4
HD = 64
BLK = 128
ROT = 16
THETA = 500000.0
QW = 512
KVW = 128
DN_H = 4
DN_D = 128
CONV = 4
CHUNK = 64
DNW = 512
CONVW = 1536
FFN = 2816
EPS = 1e-6
IN_WIDTH = 4872
IN_PAD = 4992
C_KV = 512
C_DN = 768
C_Z = 2304
C_GA = 2816
C_GB = 3840
C_BA = 4864
NEG = -1e30
N_DEV = 8
N_CHIP = 4

ADAM_LR = 0.001
ADAM_B1 = 0.9
ADAM_B2 = 0.999
ADAM_EPS = 1e-08
ADAM_WD = 0.01
ADAM_STEP = 10

VMEM_LIMIT = 60 * 1024 * 1024


def _cparams(**kw):
    return pltpu.CompilerParams(vmem_limit_bytes=VMEM_LIMIT, **kw)


def _dg(a, b, ca, cb):
    return lax.dot_general(a.astype(BF), b.astype(BF), (((ca,), (cb,)), ((), ())),
                           preferred_element_type=F32)


@jax.custom_vjp
def _mm(a, b):
    return _dg(a, b, 1, 0)


def _mm_fwd(a, b):
    return _dg(a, b, 1, 0), (a, b)


def _mm_bwd(res, dy):
    a, b = res
    return _dg(dy, b, 1, 1).astype(a.dtype), _dg(a, dy, 0, 0).astype(b.dtype)


_mm.defvjp(_mm_fwd, _mm_bwd)


@jax.custom_vjp
def _mm_nt(a, b):
    return _dg(a, b, 1, 1)


def _mm_nt_fwd(a, b):
    return _dg(a, b, 1, 1), (a, b)


def _mm_nt_bwd(res, dy):
    a, b = res
    return _dg(dy, b, 1, 0).astype(a.dtype), _dg(dy, a, 0, 0).astype(b.dtype)


_mm_nt.defvjp(_mm_nt_fwd, _mm_nt_bwd)


@jax.custom_vjp
def _mm_tn(a, b):
    return _dg(a, b, 0, 0)


def _mm_tn_fwd(a, b):
    return _dg(a, b, 0, 0), (a, b)


def _mm_tn_bwd(res, dy):
    a, b = res
    return _dg(b, dy, 1, 1).astype(a.dtype), _dg(a, dy, 1, 0).astype(b.dtype)


_mm_tn.defvjp(_mm_tn_fwd, _mm_tn_bwd)


def _mmx(a, b):
    return jnp.dot(a, b, precision=lax.Precision.HIGHEST, preferred_element_type=F32)


def _mmx_nt(a, b):
    return lax.dot_general(a, b, (((1,), (1,)), ((), ())), precision=lax.Precision.HIGHEST,
                           preferred_element_type=F32)


def _iota(shape, dim):
    return lax.broadcasted_iota(jnp.int32, shape, dim)


def _sigmoid(x):
    return 1.0 / (1.0 + jnp.exp(-x))


def _silu(x):
    return x * _sigmoid(x)


def _softplus(x):
    return jnp.maximum(x, 0.0) + jnp.log(1.0 + jnp.exp(-jnp.abs(x)))


def _rms(x, gain):
    return x * lax.rsqrt(jnp.mean(x * x, axis=-1, keepdims=True) + EPS) * gain


def _norm_mod(x, gain, shift, scale):
    return _rms(x, gain) * (1.0 + scale) + shift


def _split(a):
    hi = a.astype(BF)
    return hi, (a - hi.astype(F32)).astype(BF)


def _dg3(a, b, ca, cb):
    ah, al = _split(a)
    bh, bl = _split(b)

    def dg(x, y):
        return lax.dot_general(x, y, (((ca,), (cb,)), ((), ())), preferred_element_type=F32)
    return dg(ah, bh) + (dg(ah, bl) + dg(al, bh))


@jax.custom_vjp
def _mm3(a, b):
    return _dg3(a, b, 1, 0)


def _mm3_fwd(a, b):
    return _dg3(a, b, 1, 0), (a, b)


def _mm3_bwd(res, dy):
    a, b = res
    return _dg3(dy, b, 1, 1), _dg3(a, dy, 0, 0)


_mm3.defvjp(_mm3_fwd, _mm3_bwd)


def _qk_prep(slabs, gain, cos, sin):
    r = _iota((2 * HD, 2 * HD), 0)
    c = _iota((2 * HD, 2 * HD), 1)
    seg = jnp.where(r // HD == c // HD, 1.0 / HD, 0.0).astype(F32)
    half = ROT // 2
    cd = c % HD
    pair = jnp.where(((cd < half) & (r == c + half)) | ((cd >= half) & (cd < ROT) & (r == c - half)), 1.0, 0.0).astype(F32)
    out = []
    for x in slabs:
        y = x * lax.rsqrt(_mm3(x * x, seg) + EPS) * gain
        out.append(y * cos + _mm3(y, pair) * sin)
    return out


def _attn_block(qs, kc, kp, vc, vp, sinks, has_prev):
    rows = GROUP * BLK
    qi = _iota((rows, 2 * BLK), 0) % BLK + BLK
    kj = _iota((rows, 2 * BLK), 1)
    dist = qi - kj
    valid = (dist >= 0) & (dist < BLK) & ((kj >= BLK) | has_prev)
    grp = _iota((rows, HEADS), 0) // BLK
    col = _iota((rows, HEADS), 1)
    outs = []
    for h in range(KV_HEADS):
        q = jnp.concatenate([qs[h * GROUP + g] for g in range(GROUP)], axis=0)
        k = jnp.concatenate([kp[h], kc[h]], axis=0)
        v = jnp.concatenate([vp[h], vc[h]], axis=0)
        s = _mm_nt(q, k) * (HD ** -0.5)
        s = jnp.where(valid, s, NEG)
        sink = jnp.sum(jnp.where(col == h * GROUP + grp, sinks, 0.0), axis=-1, keepdims=True)
        m = lax.stop_gradient(jnp.maximum(jnp.max(s, axis=-1, keepdims=True), sink))
        p = jnp.exp(s - m)
        probs = p / (jnp.sum(p, axis=-1, keepdims=True) + jnp.exp(sink - m))
        o = _mm(probs, v)
        outs += [o[g * BLK:(g + 1) * BLK] for g in range(GROUP)]
    return outs


def _dn_tail(ys, ba, alog, dtb):
    def l2(t):
        return t * lax.rsqrt(jnp.sum(t * t, axis=-1, keepdims=True) + EPS)
    s = [_silu(y) for y in ys]
    out = [l2(t) for t in s[:2 * DN_H]] + s[2 * DN_H:]
    lane = _iota(ba.shape, 1)
    beta = _sigmoid(ba)
    g = -jnp.exp(alog) * _softplus(ba + dtb)
    bg = jnp.where(lane < DN_H, beta, jnp.where(lane < 2 * DN_H, g, 0.0))
    return out, bg


def _bdg(a, b, ca, cb):
    return lax.dot_general(a.astype(BF), b.astype(BF), (((ca,), (cb,)), ((0,), (0,))), preferred_element_type=F32)


@jax.custom_vjp
def _bmm(a, b):
    return _bdg(a, b, 2, 1)


def _bmm_fwd(a, b):
    return _bdg(a, b, 2, 1), (a, b)


def _bmm_bwd(res, dy):
    a, b = res
    return _bdg(dy, b, 2, 2), _bdg(a, dy, 1, 1)


_bmm.defvjp(_bmm_fwd, _bmm_bwd)


@jax.custom_vjp
def _bmm_nt(a, b):
    return _bdg(a, b, 2, 2)


def _bmm_nt_fwd(a, b):
    return _bdg(a, b, 2, 2), (a, b)


def _bmm_nt_bwd(res, dy):
    a, b = res
    return _bdg(dy, b, 2, 1), _bdg(dy, a, 1, 1)


_bmm_nt.defvjp(_bmm_nt_fwd, _bmm_nt_bwd)


def _bmmx(a, b):
    return lax.dot_general(a, b, (((2,), (1,)), ((0,), (0,))), precision=lax.Precision.HIGHEST,
                           preferred_element_type=F32)


def _dn_intra(q, k, v, bg):
    C = CHUNK
    G = bg.shape[0]
    r = _iota((C, C), 0)
    c = _iota((C, C), 1)
    incl = (r >= c)[None]
    strict = (r > c)[None]
    eye = jnp.where(r == c, 1.0, 0.0).astype(F32)[None]
    tri = jnp.broadcast_to(jnp.where(r >= c, 1.0, 0.0).astype(F32)[None], (G, C, C))
    gc_all = _bmmx(tri, bg)
    lane = _iota((C, DN_D), 1)

    def per_head(x, offset):
        return jnp.concatenate([jnp.sum(jnp.where(lane == offset + h, x[g], 0.0), axis=-1, keepdims=True)[None]
                                for g in range(G) for h in range(DN_H)], axis=0)
    beta = per_head(bg, 0)
    gcol = per_head(gc_all, DN_H)
    grow = jnp.sum(eye * gcol, axis=1, keepdims=True)
    glast = jnp.sum(jnp.where(_iota((1, C, 1), 1) == C - 1, gcol, 0.0), axis=1, keepdims=True)
    decay = jnp.exp(jnp.where(incl, gcol - grow, NEG))
    q = q * (DN_D ** -0.5)
    kb = k * beta
    lmat = jnp.where(strict, _bmm_nt(kb, k) * decay, 0.0)
    pw = -lmat
    tinv = eye + pw
    for _ in range(5):
        pw = _bmmx(pw, pw)
        tinv = tinv + _bmmx(tinv, pw)
    egc = jnp.exp(gcol)
    u = _bmm(tinv, v * beta)
    w = _bmm(tinv, kb * egc)
    a = _bmm_nt(q, k) * decay
    return u, w, q * egc, k * jnp.exp(glast - gcol), a, jnp.exp(glast)


def _dn_rec(state, us, ws, qds, kds, ats, cd):
    lane_row = _iota((1, DN_D), 1)
    new_state, outs = [], []
    for h in range(DN_H):
        cdh = jnp.sum(jnp.where(lane_row == h, cd, 0.0), axis=-1, keepdims=True)
        v_new = us[h] - _mm(ws[h], state[h])
        outs.append(_mm(qds[h], state[h]) + _mm(ats[h], v_new))
        new_state.append(state[h] * cdh + _mm_tn(kds[h], v_new))
    return new_state, outs


def _mix_tile(o_attn, o_raw, zs, ga, gb, x, gate1, dn_g, wb_a, wb_d, w_out, p_ya, p_yd, p_out):
    o_dn = jnp.concatenate([_rms(o_raw[h], dn_g) * _silu(zs[h]) for h in range(DN_H)], axis=-1)
    y_a = _mm(o_attn, wb_a) + p_ya
    y_d = _mm(o_dn, wb_d) + p_yd
    merged = _sigmoid(ga) * y_a + _sigmoid(gb) * y_d
    out = _mm(merged, w_out) + p_out
    return x + gate1 * out, o_dn, merged


def _mlp_tile(x1, gain, shift, scale, gate2, w_gu, w_dn, tgt, p_gu, p_yy):
    h2 = _norm_mod(x1, gain, shift, scale)
    gu = jnp.concatenate([_mm(h2, w) for w in w_gu], axis=-1) + p_gu
    act = _silu(gu[:, :FFN]) * gu[:, FFN:]
    yy = _mm(act, w_dn) + p_yy
    y = x1 + gate2 * yy
    err = y - tgt
    return 0.5 * jnp.sum(err * err) * (1.0 / D), (h2, act)


def _tok(bt, f):
    return pl.BlockSpec((None, bt, f), lambda b, i: (b, i, 0))


def _full(shape):
    return pl.BlockSpec(shape, lambda b, i: (0,) * len(shape))


def _per_batch(f):
    return pl.BlockSpec((None, 1, f), lambda b, i: (b, 0, 0))


def _sds(shape, dtype):
    return jax.ShapeDtypeStruct(shape, dtype)


def _acc(ref, val, first):
    @pl.when(first)
    def _():
        ref[...] = val

    @pl.when(jnp.logical_not(first))
    def _():
        ref[...] += val


def _in_proj(x, mod, norm1_g, w_in, bt):
    B, S, _ = x.shape

    def body(x_ref, mod_ref, g_ref, w_ref, q_ref, kv_ref, dn_ref, z_ref, ga_ref, gb_ref, ba_ref, h_ref):
        h = _norm_mod(x_ref[...], g_ref[...], mod_ref[:, 0:D], mod_ref[:, D:2 * D]).astype(BF)
        h_ref[...] = h

        def proj(c0, c1):
            return jnp.dot(h, w_ref[:, c0:c1], preferred_element_type=F32)
        q_ref[...] = proj(0, C_KV).astype(BF)
        kv_ref[...] = proj(C_KV, C_DN).astype(BF)
        dn_ref[...] = proj(C_DN, C_Z).astype(BF)
        z_ref[...] = proj(C_Z, C_GA).astype(BF)
        ga_ref[...] = proj(C_GA, C_GB).astype(BF)
        gb_ref[...] = proj(C_GB, C_BA).astype(BF)
        ba_ref[...] = proj(C_BA, IN_PAD)

    widths = (QW, 2 * KVW, CONVW, DNW, D, D)
    return pl.pallas_call(
        body, name="in_proj", grid=(B, S // bt),
        in_specs=[_tok(bt, D), _per_batch(6 * D), _full((1, D)), _full((D, IN_PAD))],
        out_specs=[_tok(bt, w) for w in widths] + [_tok(bt, 128), _tok(bt, D)],
        out_shape=[_sds((B, S, w), BF) for w in widths] + [_sds((B, S, 128), F32), _sds((B, S, D), BF)],
        compiler_params=_cparams(dimension_semantics=("parallel", "parallel")),
    )(x, mod, norm1_g, w_in)


def _prev_blk(bt, f):
    return pl.BlockSpec((None, bt, f), lambda b, i: (b, jnp.maximum(i - 1, 0), 0))


QKV = QW + 2 * KVW


def _qk_slabs(q_ref, kv_ref):
    return ([q_ref[:, j * 2 * HD:(j + 1) * 2 * HD].astype(F32) for j in range(QW // (2 * HD))],
            [kv_ref[:, 0:KVW].astype(F32)])


def _qk_prep_fwd(q, kv, cos, sin, qg, kg, bt):
    B, S, _ = q.shape

    def body(q_ref, kv_ref, cos_ref, sin_ref, qg_ref, kg_ref, o_ref):
        qs, ks = _qk_slabs(q_ref, kv_ref)
        qn = _qk_prep(qs, qg_ref[...], cos_ref[...], sin_ref[...])
        kn = _qk_prep(ks, kg_ref[...], cos_ref[...], sin_ref[...])
        for j, t in enumerate(qn + kn):
            o_ref[:, j * 2 * HD:(j + 1) * 2 * HD] = t.astype(BF)
        o_ref[:, QW + KVW:QKV] = kv_ref[:, KVW:2 * KVW]

    return pl.pallas_call(
        body, name="qk_prep_fwd", grid=(B, S // bt),
        in_specs=[_tok(bt, QW), _tok(bt, 2 * KVW), _tok(bt, 2 * HD), _tok(bt, 2 * HD), _full((1, 2 * HD)), _full((1, 2 * HD))],
        out_specs=_tok(bt, QKV), out_shape=_sds((B, S, QKV), BF),
        compiler_params=_cparams(dimension_semantics=("parallel", "parallel")),
    )(q, kv, cos, sin, qg, kg)


def _qk_prep_bwd(q, kv, cos, sin, qg, kg, dqn, dkvn, bt):
    B, S, _ = q.shape

    def body(q_ref, kv_ref, cos_ref, sin_ref, qg_ref, kg_ref, dqn_ref, dkvn_ref, dq_ref, dkv_ref, dqg_ref, dkg_ref):
        qs, ks = _qk_slabs(q_ref, kv_ref)
        cos, sin = cos_ref[...], sin_ref[...]

        def f(qs, ks, qg, kg):
            return _qk_prep(qs, qg, cos, sin), _qk_prep(ks, kg, cos, sin)
        _, vjp = jax.vjp(f, qs, ks, qg_ref[...], kg_ref[...])
        n_q = len(qs)
        d_q = [dqn_ref[:, j * 2 * HD:(j + 1) * 2 * HD].astype(F32) for j in range(n_q)]
        d_k = [dkvn_ref[:, 0:KVW].astype(F32)]
        dqs, dks, dqg, dkg = vjp((d_q, d_k))
        for j in range(n_q):
            dq_ref[:, j * 2 * HD:(j + 1) * 2 * HD] = dqs[j].astype(BF)
        dkv_ref[:, 0:KVW] = dks[0].astype(BF)
        dkv_ref[:, KVW:2 * KVW] = dkvn_ref[:, KVW:2 * KVW]
        first = (pl.program_id(0) == 0) & (pl.program_id(1) == 0)
        _acc(dqg_ref, dqg, first)
        _acc(dkg_ref, dkg, first)

    return pl.pallas_call(
        body, name="qk_prep_bwd", grid=(B, S // bt),
        in_specs=[_tok(bt, QW), _tok(bt, 2 * KVW), _tok(bt, 2 * HD), _tok(bt, 2 * HD), _full((1, 2 * HD)), _full((1, 2 * HD)),
                  _tok(bt, QW), _tok(bt, 2 * KVW)],
        out_specs=[_tok(bt, QW), _tok(bt, 2 * KVW), _full((1, 2 * HD)), _full((1, 2 * HD))],
        out_shape=[_sds((B, S, QW), BF), _sds((B, S, 2 * KVW), BF), _sds((1, 2 * HD), F32), _sds((1, 2 * HD), F32)],
        compiler_params=_cparams(dimension_semantics=("arbitrary", "arbitrary")),
    )(q, kv, cos, sin, qg, kg, dqn, dkvn)


def _attn_load(qkv_ref, kvp_ref):
    qs = [qkv_ref[:, h * HD:(h + 1) * HD].astype(F32) for h in range(HEADS)]
    kc = [qkv_ref[:, QW + h * HD:QW + (h + 1) * HD].astype(F32) for h in range(KV_HEADS)]
    vc = [qkv_ref[:, QW + KVW + h * HD:QW + KVW + (h + 1) * HD].astype(F32) for h in range(KV_HEADS)]
    kp = [kvp_ref[:, h * HD:(h + 1) * HD].astype(F32) for h in range(KV_HEADS)]
    vp = [kvp_ref[:, KVW + h * HD:KVW + (h + 1) * HD].astype(F32) for h in range(KV_HEADS)]
    return qs, kc, kp, vc, vp


def _kv_prev_spec(index):
    return pl.BlockSpec((None, BLK, 2 * KVW), lambda b, i: (b, index(i), QW // (2 * KVW)))


def _attn_fwd(qkv, sinks):
    B, S, _ = qkv.shape

    def body(qkv_ref, kvp_ref, sk_ref, o_ref):
        qs, kc, kp, vc, vp = _attn_load(qkv_ref, kvp_ref)
        outs = _attn_block(qs, kc, kp, vc, vp, sk_ref[...], pl.program_id(1) > 0)
        for h in range(HEADS):
            o_ref[:, h * HD:(h + 1) * HD] = outs[h].astype(BF)

    return pl.pallas_call(
        body, name="attn_fwd", grid=(B, S // BLK),
        in_specs=[_tok(BLK, QKV), _kv_prev_spec(lambda i: jnp.maximum(i - 1, 0)), _full((1, HEADS))],
        out_specs=_tok(BLK, QW), out_shape=_sds((B, S, QW), BF),
        compiler_params=_cparams(dimension_semantics=("parallel", "parallel")),
    )(qkv, qkv, sinks)


def _conv_fwd_tile(xe_ref, x_ref, halo_ref, cw_ref, first, bt):
    halo = halo_ref[...].astype(F32)
    xe_ref[0:8, :] = jnp.where(first, 0.0, halo)
    xe_ref[8:bt + 8, :] = x_ref[...].astype(F32)
    y = cw_ref[0:1, :] * xe_ref[5:bt + 5, :]
    for j in range(1, CONV):
        y = y + cw_ref[j:j + 1, :] * xe_ref[5 + j:bt + 5 + j, :]
    return y


def _halo_spec(bt):
    return pl.BlockSpec((None, 8, CONVW), lambda b, i: (b, jnp.maximum(i * (bt // 8) - 1, 0), 0))


def _dn_prep(dn, ba, conv_w, alog, dtb, bt):
    B, S, _ = dn.shape

    def body(x_ref, halo_ref, ba_ref, cw_ref, al_ref, dt_ref, qkv_ref, bg_ref, xe_ref):
        y = _conv_fwd_tile(xe_ref, x_ref, halo_ref, cw_ref, pl.program_id(1) == 0, bt)
        ys = [y[:, j * DN_D:(j + 1) * DN_D] for j in range(3 * DN_H)]
        out, bg = _dn_tail(ys, ba_ref[...], al_ref[...], dt_ref[...])
        for j in range(3 * DN_H):
            qkv_ref[:, j * DN_D:(j + 1) * DN_D] = out[j]
        bg_ref[...] = bg

    return pl.pallas_call(
        body, name="dn_prep", grid=(B, S // bt),
        in_specs=[_tok(bt, CONVW), _halo_spec(bt), _tok(bt, 128), _full((CONV, CONVW)), _full((1, 128)), _full((1, 128))],
        out_specs=[_tok(bt, CONVW), _tok(bt, 128)],
        out_shape=[_sds((B, S, CONVW), F32), _sds((B, S, 128), F32)],
        scratch_shapes=[pltpu.VMEM((bt + 8, CONVW), F32)],
        compiler_params=_cparams(dimension_semantics=("parallel", "arbitrary")),
    )(dn, dn, ba, conv_w, alog, dtb)


def _dn_load(qkv_ref):
    qs = [qkv_ref[:, h * DN_D:(h + 1) * DN_D] for h in range(DN_H)]
    ks = [qkv_ref[:, DNW + h * DN_D:DNW + (h + 1) * DN_D] for h in range(DN_H)]
    vs = [qkv_ref[:, 2 * DNW + h * DN_D:2 * DNW + (h + 1) * DN_D] for h in range(DN_H)]
    return qs, ks, vs


DN_GROUP = 4
AW = DN_H * CHUNK


def _stack_heads(ref, G, offset, width):
    return jnp.stack([ref[g * CHUNK:(g + 1) * CHUNK, offset + h * width:offset + (h + 1) * width]
                      for g in range(G) for h in range(DN_H)])


def _dn_load_stack(qkv_ref, G):
    return tuple(_stack_heads(qkv_ref, G, j * DNW, DN_D) for j in range(3))


def _cd_spec(n):
    return pl.BlockSpec((None, n, 1, DN_D), lambda b, i: (b, i, 0, 0))


def _dn_intra_fwd(qkv, bg):
    B, S, _ = qkv.shape
    nc = S // CHUNK
    G = min(DN_GROUP, nc)
    rows = G * CHUNK

    def body(qkv_ref, bg_ref, u_ref, w_ref, qd_ref, kd_ref, a_ref, cd_ref):
        q, k, v = _dn_load_stack(qkv_ref, G)
        u, w, qd, kd, a, cd = _dn_intra(q, k, v, bg_ref[...].reshape(G, CHUNK, DN_D))
        lane_row = _iota((1, DN_D), 1)
        for g in range(G):
            rows = slice(g * CHUNK, (g + 1) * CHUNK)
            cd_row = jnp.zeros((1, DN_D), F32)
            for h in range(DN_H):
                n = g * DN_H + h
                cols = slice(h * DN_D, (h + 1) * DN_D)
                u_ref[rows, cols] = u[n]
                w_ref[rows, cols] = w[n].astype(BF)
                qd_ref[rows, cols] = qd[n].astype(BF)
                kd_ref[rows, cols] = kd[n].astype(BF)
                a_ref[rows, h * CHUNK:(h + 1) * CHUNK] = a[n].astype(BF)
                cd_row = cd_row + jnp.where(lane_row == h, cd[n], 0.0)
            cd_ref[g] = cd_row

    return pl.pallas_call(
        body, name="dn_intra_fwd", grid=(B, nc // G),
        in_specs=[_tok(rows, CONVW), _tok(rows, 128)],
        out_specs=[_tok(rows, DNW)] * 4 + [_tok(rows, AW), _cd_spec(G)],
        out_shape=[_sds((B, S, DNW), F32)] + [_sds((B, S, DNW), BF)] * 3 + [_sds((B, S, AW), BF), _sds((B, nc, 1, DN_D), F32)],
        compiler_params=_cparams(dimension_semantics=("parallel", "parallel")),
    )(qkv, bg)


def _rec_load(u_ref, w_ref, qd_ref, kd_ref, a_ref):
    def heads(ref, width):
        return [ref[:, h * width:(h + 1) * width].astype(F32) for h in range(DN_H)]
    return heads(u_ref, DN_D), heads(w_ref, DN_D), heads(qd_ref, DN_D), heads(kd_ref, DN_D), heads(a_ref, CHUNK)


def _dn_rec_fwd(u, w, qd, kd, a, cd):
    B, S, _ = u.shape
    nc = S // CHUNK

    def body(u_ref, w_ref, qd_ref, kd_ref, a_ref, cd_ref, o_ref, st_ref, s_ref):
        @pl.when(pl.program_id(1) == 0)
        def _():
            s_ref[...] = jnp.zeros_like(s_ref)
        state = [s_ref[h] for h in range(DN_H)]
        for h in range(DN_H):
            st_ref[h] = state[h]
        new_state, outs = _dn_rec(state, *_rec_load(u_ref, w_ref, qd_ref, kd_ref, a_ref), cd_ref[0])
        for h in range(DN_H):
            s_ref[h] = new_state[h]
            o_ref[:, h * DN_D:(h + 1) * DN_D] = outs[h]

    return pl.pallas_call(
        body, name="dn_rec_fwd", grid=(B, nc),
        in_specs=[_tok(CHUNK, DNW)] * 4 + [_tok(CHUNK, AW), _cd_spec(1)],
        out_specs=[_tok(CHUNK, DNW), pl.BlockSpec((None, None, DN_H, DN_D, DN_D), lambda b, i: (b, i, 0, 0, 0))],
        out_shape=[_sds((B, S, DNW), F32), _sds((B, nc, DN_H, DN_D, DN_D), F32)],
        scratch_shapes=[pltpu.VMEM((DN_H, DN_D, DN_D), F32)],
        compiler_params=_cparams(dimension_semantics=("parallel", "arbitrary")),
    )(u, w, qd, kd, a, cd)


def _mix_load(oa_ref, or_ref, z_ref):
    o_raw = [or_ref[:, h * DN_D:(h + 1) * DN_D] for h in range(DN_H)]
    zs = [z_ref[:, h * DN_D:(h + 1) * DN_D].astype(F32) for h in range(DN_H)]
    return oa_ref[...].astype(F32), o_raw, zs


def _mix_fwd(o_attn, o_raw, z, ga, gb, x, mod, dn_g, w_branch, w_out, bt):
    B, S, _ = x.shape

    def body(oa_ref, or_ref, z_ref, ga_ref, gb_ref, x_ref, mod_ref, dg_ref, wb_ref, wo_ref, x1_ref, od_ref, mg_ref):
        oa, o_r, zs = _mix_load(oa_ref, or_ref, z_ref)
        x1, o_dn, merged = _mix_tile(oa, o_r, zs, ga_ref[...].astype(F32), gb_ref[...].astype(F32), x_ref[...],
                                     mod_ref[:, 2 * D:3 * D], dg_ref[...], wb_ref[0:QW, :], wb_ref[QW:2 * QW, :],
                                     wo_ref[...], 0.0, 0.0, 0.0)
        x1_ref[...] = x1
        od_ref[...] = o_dn.astype(BF)
        mg_ref[...] = merged.astype(BF)

    return pl.pallas_call(
        body, name="mix_fwd", grid=(B, S // bt),
        in_specs=[_tok(bt, QW), _tok(bt, DNW), _tok(bt, DNW), _tok(bt, D), _tok(bt, D), _tok(bt, D), _per_batch(6 * D),
                  _full((1, DN_D)), _full((D, D)), _full((D, D))],
        out_specs=[_tok(bt, D), _tok(bt, DNW), _tok(bt, D)],
        out_shape=[_sds((B, S, D), F32), _sds((B, S, DNW), BF), _sds((B, S, D), BF)],
        compiler_params=_cparams(dimension_semantics=("parallel", "parallel")),
    )(o_attn, o_raw, z, ga, gb, x, mod, dn_g, w_branch, w_out)


def _mlp(x1, tgt, mod, norm2_g, w_gu, w_dn, bt):
    B, S, _ = x1.shape

    def body(x1_ref, t_ref, mod_ref, g_ref, wgu_ref, wdn_ref,
             dx1_ref, h2_ref, act_ref, dgu_ref, dyy_ref, loss_ref, dmod_ref, dg_ref):
        w_gu_v, w_dn_v, t = [wgu_ref[k] for k in range(N_CHIP)], wdn_ref[...], t_ref[...]

        def f(x1, gain, shift, scale, gate2, p_gu, p_yy):
            return _mlp_tile(x1, gain, shift, scale, gate2, w_gu_v, w_dn_v, t, p_gu, p_yy)
        zero_gu = jnp.zeros((bt, 2 * FFN), F32)
        zero_yy = jnp.zeros((bt, D), F32)
        loss, vjp, (h2, act) = jax.vjp(f, x1_ref[...], g_ref[...], mod_ref[:, 3 * D:4 * D], mod_ref[:, 4 * D:5 * D],
                                       mod_ref[:, 5 * D:6 * D], zero_gu, zero_yy, has_aux=True)
        dx1, dgain, dshift, dscale, dgate2, dgu, dyy = vjp(jnp.ones((), F32))
        dx1_ref[...] = dx1
        h2_ref[...] = h2.astype(BF)
        act_ref[...] = act.astype(BF)
        dgu_ref[...] = dgu.astype(BF)
        dyy_ref[...] = dyy.astype(BF)
        first = pl.program_id(1) == 0
        _acc(loss_ref, jnp.reshape(loss, (1, 1)), first)
        _acc(dmod_ref, jnp.concatenate([dshift, dscale, dgate2], axis=-1), first)
        _acc(dg_ref, dgain, first)

    return pl.pallas_call(
        body, name="mlp", grid=(B, S // bt),
        in_specs=[_tok(bt, D), _tok(bt, D), _per_batch(6 * D), _full((1, D)), _full((N_CHIP, D, 2 * FFN // N_CHIP)),
                  _full((FFN, D))],
        out_specs=[_tok(bt, D), _tok(bt, D), _tok(bt, FFN), _tok(bt, 2 * FFN), _tok(bt, D),
                   _per_batch(1), _per_batch(3 * D), _per_batch(D)],
        out_shape=[_sds((B, S, D), F32), _sds((B, S, D), BF), _sds((B, S, FFN), BF), _sds((B, S, 2 * FFN), BF),
                   _sds((B, S, D), BF), _sds((B, 1, 1), F32), _sds((B, 1, 3 * D), F32), _sds((B, 1, D), F32)],
        compiler_params=_cparams(dimension_semantics=("parallel", "arbitrary")),
    )(x1, tgt, mod, norm2_g, w_gu, w_dn)


def _mix_bwd(o_attn, o_raw, z, ga, gb, x, mod, dn_g, w_branch, w_out, dx1, bt):
    B, S, _ = x.shape

    def body(oa_ref, or_ref, z_ref, ga_ref, gb_ref, x_ref, mod_ref, dg_ref, wb_ref, wo_ref, dx1_ref,
             doa_ref, dor_ref, dz_ref, dga_ref, dgb_ref, dya_ref, dyd_ref, dout_ref, dgate_ref, ddg_ref):
        oa, o_r, zs = _mix_load(oa_ref, or_ref, z_ref)
        wb_a, wb_d, wo = wb_ref[0:QW, :], wb_ref[QW:2 * QW, :], wo_ref[...]

        def f(oa, o_r, zs, ga, gb, gate1, dn_g, p_ya, p_yd, p_out):
            return _mix_tile(oa, o_r, zs, ga, gb, x_ref[...], gate1, dn_g, wb_a, wb_d, wo, p_ya, p_yd, p_out)[0]
        zero = jnp.zeros((bt, D), F32)
        _, vjp = jax.vjp(f, oa, o_r, zs, ga_ref[...].astype(F32), gb_ref[...].astype(F32), mod_ref[:, 2 * D:3 * D],
                         dg_ref[...], zero, zero, zero)
        doa, dor, dzs, dga, dgb, dgate1, ddn_g, dya, dyd, dout = vjp(dx1_ref[...])
        doa_ref[...] = doa
        for h in range(DN_H):
            dor_ref[:, h * DN_D:(h + 1) * DN_D] = dor[h]
            dz_ref[:, h * DN_D:(h + 1) * DN_D] = dzs[h].astype(BF)
        dga_ref[...] = dga.astype(BF)
        dgb_ref[...] = dgb.astype(BF)
        dya_ref[...] = dya.astype(BF)
        dyd_ref[...] = dyd.astype(BF)
        dout_ref[...] = dout.astype(BF)
        first = pl.program_id(1) == 0
        _acc(dgate_ref, dgate1, first)
        _acc(ddg_ref, ddn_g, first)

    return pl.pallas_call(
        body, name="mix_bwd", grid=(B, S // bt),
        in_specs=[_tok(bt, QW), _tok(bt, DNW), _tok(bt, DNW), _tok(bt, D), _tok(bt, D), _tok(bt, D), _per_batch(6 * D),
                  _full((1, DN_D)), _full((D, D)), _full((D, D)), _tok(bt, D)],
        out_specs=[_tok(bt, QW), _tok(bt, DNW), _tok(bt, DNW), _tok(bt, D), _tok(bt, D), _tok(bt, D), _tok(bt, D), _tok(bt, D),
                   _per_batch(D), _per_batch(DN_D)],
        out_shape=[_sds((B, S, QW), F32), _sds((B, S, DNW), F32), _sds((B, S, DNW), BF), _sds((B, S, D), BF),
                   _sds((B, S, D), BF), _sds((B, S, D), BF), _sds((B, S, D), BF), _sds((B, S, D), BF),
                   _sds((B, 1, D), F32), _sds((B, 1, DN_D), F32)],
        compiler_params=_cparams(dimension_semantics=("parallel", "arbitrary")),
    )(o_attn, o_raw, z, ga, gb, x, mod, dn_g, w_branch, w_out, dx1)


def _dn_rec_bwd(u, w, qd, kd, a, cd, states, d_o):
    B, S, _ = u.shape
    nc = S // CHUNK

    def rev(f):
        return pl.BlockSpec((None, CHUNK, f), lambda b, i: (b, nc - 1 - i, 0))

    cd_rev = pl.BlockSpec((None, 1, 1, DN_D), lambda b, i: (b, nc - 1 - i, 0, 0))

    def body(u_ref, w_ref, qd_ref, kd_ref, a_ref, cd_ref, st_ref, do_ref,
             du_ref, dw_ref, dqd_ref, dkd_ref, da_ref, dcd_ref, ds_ref):
        @pl.when(pl.program_id(1) == 0)
        def _():
            ds_ref[...] = jnp.zeros_like(ds_ref)
        state = [st_ref[h] for h in range(DN_H)]
        _, vjp = jax.vjp(_dn_rec, state, *_rec_load(u_ref, w_ref, qd_ref, kd_ref, a_ref), cd_ref[0])
        d_state = [ds_ref[h] for h in range(DN_H)]
        d_outs = [do_ref[:, h * DN_D:(h + 1) * DN_D] for h in range(DN_H)]
        dst, du, dw, dqd, dkd, da, dcd = vjp((d_state, d_outs))
        for h in range(DN_H):
            cols = slice(h * DN_D, (h + 1) * DN_D)
            ds_ref[h] = dst[h]
            du_ref[:, cols] = du[h]
            dw_ref[:, cols] = dw[h]
            dqd_ref[:, cols] = dqd[h]
            dkd_ref[:, cols] = dkd[h]
            da_ref[:, h * CHUNK:(h + 1) * CHUNK] = da[h]
        dcd_ref[0] = dcd

    return pl.pallas_call(
        body, name="dn_rec_bwd", grid=(B, nc),
        in_specs=[rev(DNW)] * 4 + [rev(AW), cd_rev,
                  pl.BlockSpec((None, None, DN_H, DN_D, DN_D), lambda b, i: (b, nc - 1 - i, 0, 0, 0)), rev(DNW)],
        out_specs=[rev(DNW)] * 4 + [rev(AW), cd_rev],
        out_shape=[_sds((B, S, DNW), F32)] * 4 + [_sds((B, S, AW), F32), _sds((B, nc, 1, DN_D), F32)],
        scratch_shapes=[pltpu.VMEM((DN_H, DN_D, DN_D), F32)],
        compiler_params=_cparams(dimension_semantics=("parallel", "arbitrary")),
    )(u, w, qd, kd, a, cd, states, d_o)


def _dn_intra_bwd(qkv, bg, du, dw, dqd, dkd, da, dcd):
    B, S, _ = qkv.shape
    nc = S // CHUNK
    G = min(DN_GROUP, nc)
    rows = G * CHUNK

    def body(qkv_ref, bg_ref, du_ref, dw_ref, dqd_ref, dkd_ref, da_ref, dcd_ref, dqkv_ref, dbg_ref):
        q, k, v = _dn_load_stack(qkv_ref, G)
        _, vjp = jax.vjp(_dn_intra, q, k, v, bg_ref[...].reshape(G, CHUNK, DN_D))
        lane_row = _iota((1, DN_D), 1)
        dcd = jnp.stack([jnp.sum(jnp.where(lane_row == h, dcd_ref[g], 0.0), axis=-1, keepdims=True)
                         for g in range(G) for h in range(DN_H)])
        dq, dk, dv, dbg = vjp((_stack_heads(du_ref, G, 0, DN_D), _stack_heads(dw_ref, G, 0, DN_D),
                               _stack_heads(dqd_ref, G, 0, DN_D), _stack_heads(dkd_ref, G, 0, DN_D),
                               _stack_heads(da_ref, G, 0, CHUNK), dcd))
        for g in range(G):
            rows = slice(g * CHUNK, (g + 1) * CHUNK)
            for h in range(DN_H):
                n = g * DN_H + h
                dqkv_ref[rows, h * DN_D:(h + 1) * DN_D] = dq[n]
                dqkv_ref[rows, DNW + h * DN_D:DNW + (h + 1) * DN_D] = dk[n]
                dqkv_ref[rows, 2 * DNW + h * DN_D:2 * DNW + (h + 1) * DN_D] = dv[n]
        dbg_ref[...] = dbg.reshape(G * CHUNK, DN_D)

    return pl.pallas_call(
        body, name="dn_intra_bwd", grid=(B, nc // G),
        in_specs=[_tok(rows, CONVW), _tok(rows, 128)] + [_tok(rows, DNW)] * 4 + [_tok(rows, AW), _cd_spec(G)],
        out_specs=[_tok(rows, CONVW), _tok(rows, 128)],
        out_shape=[_sds((B, S, CONVW), F32), _sds((B, S, 128), F32)],
        compiler_params=_cparams(dimension_semantics=("parallel", "parallel")),
    )(qkv, bg, du, dw, dqd, dkd, da, dcd)


def _dn_prep_bwd(dn, ba, conv_w, alog, dtb, dqkv, dbg, bt):
    B, S, _ = dn.shape
    nt = S // bt

    def rev(f):
        return pl.BlockSpec((None, bt, f), lambda b, i: (b, nt - 1 - i, 0))

    halo = pl.BlockSpec((None, 8, CONVW), lambda b, i: (b, jnp.maximum((nt - 1 - i) * (bt // 8) - 1, 0), 0))

    def body(x_ref, halo_ref, ba_ref, cw_ref, al_ref, dt_ref, dqkv_ref, dbg_ref,
             dx_ref, dba_ref, dcw_ref, dal_ref, ddt_ref, xe_ref, dye_ref):
        i = pl.program_id(1)
        y = _conv_fwd_tile(xe_ref, x_ref, halo_ref, cw_ref, i == nt - 1, bt)
        ys = [y[:, j * DN_D:(j + 1) * DN_D] for j in range(3 * DN_H)]
        _, vjp = jax.vjp(_dn_tail, ys, ba_ref[...], al_ref[...], dt_ref[...])
        d_out = [dqkv_ref[:, j * DN_D:(j + 1) * DN_D] for j in range(3 * DN_H)]
        dys, dba, dal, ddt = vjp((d_out, dbg_ref[...]))
        @pl.when(i == 0)
        def _():
            dye_ref[bt:bt + 8, :] = jnp.zeros((8, CONVW), F32)

        @pl.when(i > 0)
        def _():
            dye_ref[bt:bt + 8, :] = dye_ref[0:8, :]
        for j in range(3 * DN_H):
            dye_ref[0:bt, j * DN_D:(j + 1) * DN_D] = dys[j]
        dx = cw_ref[0:1, :] * dye_ref[3:bt + 3, :]
        for j in range(1, CONV):
            dx = dx + cw_ref[j:j + 1, :] * dye_ref[3 - j:bt + 3 - j, :]
        dx_ref[...] = dx.astype(BF)
        dy = dye_ref[0:bt, :]
        dcw = jnp.concatenate([jnp.sum(dy * xe_ref[5 + j:bt + 5 + j, :], axis=0, keepdims=True) for j in range(CONV)], axis=0)
        first = (i == 0) & (pl.program_id(0) == 0)
        dba_ref[...] = dba
        _acc(dcw_ref, dcw, first)
        _acc(dal_ref, dal, first)
        _acc(ddt_ref, ddt, first)

    return pl.pallas_call(
        body, name="dn_prep_bwd", grid=(B, nt),
        in_specs=[rev(CONVW), halo, rev(128), _full((CONV, CONVW)), _full((1, 128)), _full((1, 128)), rev(CONVW), rev(128)],
        out_specs=[rev(CONVW), rev(128), _full((CONV, CONVW)), _full((1, 128)), _full((1, 128))],
        out_shape=[_sds((B, S, CONVW), BF), _sds((B, S, 128), F32), _sds((CONV, CONVW), F32), _sds((1, 128), F32),
                   _sds((1, 128), F32)],
        scratch_shapes=[pltpu.VMEM((bt + 8, CONVW), F32), pltpu.VMEM((bt + 8, CONVW), F32)],
        compiler_params=_cparams(dimension_semantics=("arbitrary", "arbitrary")),
    )(dn, dn, ba, conv_w, alog, dtb, dqkv, dbg)


def _attn_bwd(qkv, sinks, d_o):
    B, S, _ = qkv.shape
    nb = S // BLK

    def cur(f):
        return pl.BlockSpec((None, BLK, f), lambda b, i: (b, jnp.minimum(i, nb - 1), 0))

    def out_prev(f):
        return pl.BlockSpec((None, BLK, f), lambda b, i: (b, jnp.maximum(i - 1, 0), 0))

    def body(qkv_ref, kvp_ref, sk_ref, do_ref, dq_ref, dkv_ref, dsk_ref, carry_ref):
        n = pl.program_id(1)
        first = (n == 0) & (pl.program_id(0) == 0)

        @pl.when(n == 0)
        def _():
            carry_ref[...] = jnp.zeros_like(carry_ref)

        @pl.when(n < nb)
        def _():
            qs, kc, kp, vc, vp = _attn_load(qkv_ref, kvp_ref)

            def f(qs, kc, kp, vc, vp, sk):
                return _attn_block(qs, kc, kp, vc, vp, sk, n > 0)
            _, vjp = jax.vjp(f, qs, kc, kp, vc, vp, sk_ref[...])
            d_outs = [do_ref[:, h * HD:(h + 1) * HD] for h in range(HEADS)]
            dqs, dkc, dkp, dvc, dvp, dsk = vjp(d_outs)
            for h in range(HEADS):
                dq_ref[:, h * HD:(h + 1) * HD] = dqs[h].astype(BF)
            for h in range(KV_HEADS):
                ksl = slice(h * HD, (h + 1) * HD)
                vsl = slice(KVW + h * HD, KVW + (h + 1) * HD)
                dkv_ref[:, ksl] = (carry_ref[:, ksl] + dkp[h]).astype(BF)
                dkv_ref[:, vsl] = (carry_ref[:, vsl] + dvp[h]).astype(BF)
                carry_ref[:, ksl] = dkc[h]
                carry_ref[:, vsl] = dvc[h]
            _acc(dsk_ref, dsk, first)

        @pl.when(n == nb)
        def _():
            dkv_ref[...] = carry_ref[...].astype(BF)

    return pl.pallas_call(
        body, name="attn_bwd", grid=(B, nb + 1),
        in_specs=[cur(QKV), _kv_prev_spec(lambda i: jnp.maximum(jnp.minimum(i, nb - 1) - 1, 0)), _full((1, HEADS)), cur(QW)],
        out_specs=[cur(QW), out_prev(2 * KVW), _full((1, HEADS))],
        out_shape=[_sds((B, S, QW), BF), _sds((B, S, 2 * KVW), BF), _sds((1, HEADS), F32)],
        scratch_shapes=[pltpu.VMEM((BLK, 2 * KVW), F32)],
        compiler_params=_cparams(dimension_semantics=("arbitrary", "arbitrary")),
    )(qkv, qkv, sinks, d_o)


def _in_proj_bwd(x, mod, norm1_g, w_in, pieces, dba, dx1, bt):
    B, S, _ = x.shape
    widths = (QW, 2 * KVW, CONVW, DNW, D, D)

    def body(x_ref, mod_ref, g_ref, w_ref, dq_ref, dkv_ref, ddn_ref, dz_ref, dga_ref, dgb_ref, dba_ref, dx1_ref,
             gx_ref, dp_ref, dmod_ref, dg_ref):
        dp = jnp.concatenate([r[...] for r in (dq_ref, dkv_ref, ddn_ref, dz_ref, dga_ref, dgb_ref)]
                             + [dba_ref[...].astype(BF)], axis=-1)
        dp_ref[...] = dp
        dh = lax.dot_general(dp, w_ref[...], (((1,), (1,)), ((), ())), preferred_element_type=F32)
        _, vjp = jax.vjp(_norm_mod, x_ref[...], g_ref[...], mod_ref[:, 0:D], mod_ref[:, D:2 * D])
        dx, dgain, dshift, dscale = vjp(dh)
        gx_ref[...] = dx + dx1_ref[...]
        first = pl.program_id(1) == 0
        _acc(dmod_ref, jnp.concatenate([dshift, dscale], axis=-1), first)
        _acc(dg_ref, dgain, first)

    return pl.pallas_call(
        body, name="in_proj_bwd", grid=(B, S // bt),
        in_specs=[_tok(bt, D), _per_batch(6 * D), _full((1, D)), _full((D, IN_PAD))] + [_tok(bt, w) for w in widths]
        + [_tok(bt, 128), _tok(bt, D)],
        out_specs=[_tok(bt, D), _tok(bt, IN_PAD), _per_batch(2 * D), _per_batch(D)],
        out_shape=[_sds((B, S, D), F32), _sds((B, S, IN_PAD), BF), _sds((B, 1, 2 * D), F32), _sds((B, 1, D), F32)],
        compiler_params=_cparams(dimension_semantics=("parallel", "arbitrary")),
    )(x, mod, norm1_g, w_in, *pieces, dba, dx1)


def _matmul_tn(tag, a, b, bk, bn, bt, col_blocks=False):
    T, K = a.shape
    N = b.shape[1]
    nt = T // bt
    if col_blocks:
        assert bk == K
        out_spec = pl.BlockSpec((None, bk, bn), lambda i, j, t: (j, 0, 0))
        out_shape = _sds((N // bn, K, bn), F32)
    else:
        out_spec = pl.BlockSpec((bk, bn), lambda i, j, t: (i, j))
        out_shape = _sds((K, N), F32)

    def body(a_ref, b_ref, o_ref, acc_ref):
        t = pl.program_id(2)

        @pl.when(t == 0)
        def _():
            acc_ref[...] = jnp.zeros_like(acc_ref)
        acc_ref[...] += lax.dot_general(a_ref[...], b_ref[...], (((0,), (0,)), ((), ())), preferred_element_type=F32)

        @pl.when(t == nt - 1)
        def _():
            o_ref[...] = acc_ref[...]

    return pl.pallas_call(
        body, name=f"grad_{tag}", grid=(K // bk, N // bn, nt),
        in_specs=[pl.BlockSpec((bt, bk), lambda i, j, t: (t, i)), pl.BlockSpec((bt, bn), lambda i, j, t: (t, j))],
        out_specs=out_spec, out_shape=out_shape,
        scratch_shapes=[pltpu.VMEM((bk, bn), F32)],
        compiler_params=_cparams(dimension_semantics=("parallel", "parallel", "arbitrary")),
    )(a, b)


def _rope_table(positions):
    inv_freq = THETA ** (-jnp.arange(0, ROT, 2, dtype=F32) / ROT)
    ang = positions.astype(F32)[..., None] * inv_freq
    cos, sin = jnp.cos(ang), jnp.sin(ang)
    ones = jnp.ones(ang.shape[:-1] + (HD - ROT,), F32)
    cos_head = [cos, cos, ones]
    sin_head = [-sin, sin, 0.0 * ones]
    return jnp.concatenate(cos_head + cos_head, axis=-1), jnp.concatenate(sin_head + sin_head, axis=-1)


def _lane_pad(v, offset, width=128):
    return jnp.zeros((1, width), F32).at[0, offset:offset + v.shape[-1]].set(v.reshape(-1))


def _tile(S, want):
    return min(S, want)


def _local_step(x, mod, positions, tgt, norm1_g, w_in_pad, conv_w, q_norm_g, k_norm_g, sinks, a_log, dt_bias,
                dn_norm_g, w_branch, w_out, norm2_g, w_gu, w_dn):
    B, S, _ = x.shape
    T = B * S
    cos_t, sin_t = _rope_table(positions)
    qg2 = jnp.concatenate([q_norm_g, q_norm_g], axis=-1)
    kg2 = jnp.concatenate([k_norm_g, k_norm_g], axis=-1)
    alog = _lane_pad(a_log, DN_H)
    dtb = _lane_pad(dt_bias, DN_H)
    conv2 = conv_w.reshape(CONV, CONVW)
    bt = _tile(S, 512)
    bt_mlp = _tile(S, 256)

    q, kv, dn, z, ga, gb, ba, h1 = _in_proj(x, mod, norm1_g, w_in_pad, bt)
    qkv_n = _qk_prep_fwd(q, kv, cos_t, sin_t, qg2, kg2, bt)
    o_attn = _attn_fwd(qkv_n, sinks)
    dqkv, bg = _dn_prep(dn, ba, conv2, alog, dtb, bt)
    dn_u, dn_w, dn_qd, dn_kd, dn_a, dn_cd = _dn_intra_fwd(dqkv, bg)
    o_raw, states = _dn_rec_fwd(dn_u, dn_w, dn_qd, dn_kd, dn_a, dn_cd)
    x1, o_dn, merged = _mix_fwd(o_attn, o_raw, z, ga, gb, x, mod, dn_norm_g, w_branch, w_out, bt)
    dx1, h2, act, dgu, dyy, loss, dmod2, dnorm2 = _mlp(x1, tgt, mod, norm2_g, w_gu, w_dn, bt_mlp)

    def flat(t):
        return t.reshape(T, t.shape[-1])
    tn = functools.partial(_matmul_tn, bt=_tile(T, 512))
    g_w_dn = tn("w_down", flat(act), flat(dyy), bk=FFN, bn=D // 2)
    g_w_gu = tn("w_gate_up", flat(h2), flat(dgu), bk=D, bn=2 * FFN // N_CHIP, col_blocks=True)

    d_oa, d_or, dz, dga, dgb, dya, dyd, dout, dgate1, ddn_g = _mix_bwd(
        o_attn, o_raw, z, ga, gb, x, mod, dn_norm_g, w_branch, w_out, dx1, bt_mlp)
    g_w_out = tn("w_out", flat(merged), flat(dout), bk=D, bn=D)
    g_w_br = jnp.concatenate([tn("w_branch_attn", flat(o_attn), flat(dya), bk=QW, bn=D),
                              tn("w_branch_dn", flat(o_dn), flat(dyd), bk=DNW, bn=D)], axis=0)

    d_rec = _dn_rec_bwd(dn_u, dn_w, dn_qd, dn_kd, dn_a, dn_cd, states, d_or)
    d_dqkv, dbg = _dn_intra_bwd(dqkv, bg, *d_rec)
    d_dn, dba, dconv, dalog, ddtb = _dn_prep_bwd(dn, ba, conv2, alog, dtb, d_dqkv, dbg, bt)
    dqn, dkvn, dsk = _attn_bwd(qkv_n, sinks, d_oa)
    dq, dkv, dqg2, dkg2 = _qk_prep_bwd(q, kv, cos_t, sin_t, qg2, kg2, dqn, dkvn, bt)
    dqg = dqg2[:, :HD] + dqg2[:, HD:]
    dkg = dkg2[:, :HD] + dkg2[:, HD:]
    grad_x, dproj, dmod1, dnorm1 = _in_proj_bwd(x, mod, norm1_g, w_in_pad, (dq, dkv, d_dn, dz, dga, dgb), dba, dx1, bt)
    g_w_in = tn("w_in", flat(h1), flat(dproj), bk=D, bn=IN_PAD // 3)

    dmod = jnp.concatenate([dmod1, dgate1, dmod2], axis=-1)
    small = dict(norm1_g=jnp.sum(dnorm1, axis=0), norm2_g=jnp.sum(dnorm2, axis=0), q_norm_g=dqg, k_norm_g=dkg,
                 sinks=dsk, a_log=dalog[:, DN_H:2 * DN_H], dt_bias=ddtb[:, DN_H:2 * DN_H],
                 dn_norm_g=jnp.sum(ddn_g, axis=0), conv_w=dconv)
    return jnp.sum(loss), grad_x, dmod, small, (g_w_in, g_w_br, g_w_out, g_w_gu, g_w_dn)


def _me():
    return lax.axis_index("x"), lax.axis_index("y"), lax.axis_index("c")


def _flip(me, f):
    return (me[0] ^ ((f >> 2) & 1), me[1] ^ ((f >> 1) & 1), me[2] ^ (f & 1))


def _comm_call(name, ins, out_shapes, n_remote, plan):
    n_in = len(ins)
    n_out = len(out_shapes)

    def body(*refs):
        in_refs, out_refs = refs[:n_in], refs[n_in:n_in + n_out]
        send_sems, recv_sems = refs[n_in + n_out:]
        remote = plan(_me(), in_refs, out_refs)
        assert len(remote) == n_remote
        cps = [pltpu.make_async_remote_copy(src_ref=src, dst_ref=dst, send_sem=send_sems.at[i], recv_sem=recv_sems.at[i],
                                            device_id=peer, device_id_type=pl.DeviceIdType.MESH)
               for i, (src, dst, peer) in enumerate(remote)]
        for cp in cps:
            cp.start()
        for cp in cps:
            cp.wait_recv()
        for cp in cps:
            cp.wait_send()

    any_spec = pl.BlockSpec(memory_space=pl.ANY)
    return pl.pallas_call(
        body, name=name, in_specs=[any_spec] * n_in, out_specs=[any_spec] * n_out, out_shape=out_shapes,
        scratch_shapes=[pltpu.SemaphoreType.DMA((n_remote,)), pltpu.SemaphoreType.DMA((n_remote,))],
    )(*ins)


def _by_origin(own, received, index):
    stack = jnp.concatenate([own[None], received], axis=0)
    n = stack.shape[0]
    return jnp.stack([lax.dynamic_index_in_dim(stack, k ^ index, 0, keepdims=False) for k in range(n)])


def _gather_devices(name, arrs, dev):
    def plan(me, in_refs, out_refs):
        return [(a, o.at[f - 1], _flip(me, f)) for a, o in zip(in_refs, out_refs) for f in range(1, N_DEV)]
    outs = [_sds((N_DEV - 1,) + a.shape, a.dtype) for a in arrs]
    got = _comm_call(name, arrs, outs, (N_DEV - 1) * len(arrs), plan)
    return [_by_origin(a, g, dev) for a, g in zip(arrs, got)]


def _gather_chips(name, arrs, chip):
    def plan(me, in_refs, out_refs):
        return [(a, o.at[j], _flip(me, 2 * (j + 1))) for a, o in zip(in_refs, out_refs) for j in range(N_CHIP - 1)]
    outs = [_sds((N_CHIP - 1,) + a.shape, a.dtype) for a in arrs]
    got = _comm_call(name, arrs, outs, (N_CHIP - 1) * len(arrs), plan)
    return [_by_origin(a, g, chip) for a, g in zip(arrs, got)]


def _halves(core, mine, other):
    lo = jnp.where(core == 0, mine, other)
    hi = jnp.where(core == 0, other, mine)
    return jnp.concatenate([lo, hi], axis=-2)


def _gather_weights(shards, chip, core):
    def plan_ici(me, in_refs, out_refs):
        remote = []
        for a, o in zip(in_refs, out_refs):
            half = a.shape[0] // 2
            mine = a.at[pl.ds(me[2] * half, half)]
            remote += [(mine, o.at[j], _flip(me, 2 * (j + 1))) for j in range(N_CHIP - 1)]
        return remote
    mine = _comm_call("weights_ici", shards, [_sds((N_CHIP - 1, a.shape[0] // 2, a.shape[1]), a.dtype) for a in shards],
                      (N_CHIP - 1) * len(shards), plan_ici)

    def plan_d2d(me, in_refs, out_refs):
        return [(g, o, _flip(me, 1)) for g, o in zip(in_refs, out_refs)]
    other = _comm_call("weights_d2d", mine, [_sds(g.shape, g.dtype) for g in mine], len(mine), plan_d2d)
    return [_by_origin(a, _halves(core, g, h), chip) for a, g, h in zip(shards, mine, other)]


def _rows(r):
    for br in (512, 352, 256, 128, 64, 32, 16, 8):
        if r % br == 0:
            return br
    raise ValueError(r)


def _pair_add(tag, g, recv, c):
    n, r, cols = g.shape
    half = r // 2
    br = _rows(half)
    nb = half // br

    def body(c_ref, g_ref, r_ref, o_ref):
        o_ref[...] = (g_ref[...] + r_ref[...]).astype(BF)

    return pl.pallas_call(
        body, name=f"pair_add_{tag}",
        grid_spec=pltpu.PrefetchScalarGridSpec(
            num_scalar_prefetch=1, grid=(n, nb),
            in_specs=[pl.BlockSpec((None, br, cols), lambda k, i, c_ref: (k, c_ref[0] * nb + i, 0)),
                      pl.BlockSpec((None, br, cols), lambda k, i, c_ref: (k, i, 0))],
            out_specs=pl.BlockSpec((None, br, cols), lambda k, i, c_ref: (k, i, 0))),
        out_shape=_sds((n, half, cols), BF),
        compiler_params=_cparams(dimension_semantics=("parallel", "parallel")),
    )(c, g, recv)


def _sum_chips(tag, p, q, chip):
    n, r, cols = q.shape
    br = _rows(r)

    def body(chip_ref, p_ref, q_ref, o_ref):
        acc = p_ref[...].astype(F32)
        for k in range(n):
            acc = acc + q_ref[k].astype(F32)
        o_ref[...] = acc

    return pl.pallas_call(
        body, name=f"sum_chips_{tag}",
        grid_spec=pltpu.PrefetchScalarGridSpec(
            num_scalar_prefetch=1, grid=(r // br,),
            in_specs=[pl.BlockSpec((None, br, cols), lambda i, chip_ref: (chip_ref[0], i, 0)),
                      pl.BlockSpec((n, br, cols), lambda i, chip_ref: (0, i, 0))],
            out_specs=pl.BlockSpec((br, cols), lambda i, chip_ref: (i, 0))),
        out_shape=_sds((r, cols), F32),
        compiler_params=_cparams(dimension_semantics=("parallel",)),
    )(chip, p, q)


def _reduce_grads(tags, grads, chip, core):
    core_arr = core.reshape(1).astype(jnp.int32)
    chip_arr = chip.reshape(1).astype(jnp.int32)

    def plan_pair(me, in_refs, out_refs):
        remote = []
        for g, o in zip(in_refs, out_refs):
            half = g.shape[1] // 2
            remote += [(g.at[k, pl.ds((1 - me[2]) * half, half)], o.at[k], _flip(me, 1)) for k in range(N_CHIP)]
        return remote
    recv = _comm_call("grads_pair", grads, [_sds((N_CHIP, g.shape[1] // 2, g.shape[2]), F32) for g in grads],
                      N_CHIP * len(grads), plan_pair)
    pair = [_pair_add(t, g, r, core_arr) for t, g, r in zip(tags, grads, recv)]

    def plan_chips(me, in_refs, out_refs):
        remote = []
        for p, o in zip(in_refs, out_refs):
            for j in range(N_CHIP - 1):
                peer = _flip(me, 2 * (j + 1))
                remote.append((p.at[2 * peer[0] + peer[1]], o.at[j], peer))
        return remote
    parts = _comm_call("grads_chips", pair, [_sds((N_CHIP - 1,) + p.shape[1:], BF) for p in pair],
                       (N_CHIP - 1) * len(pair), plan_chips)
    mine = [_sum_chips(t, p, q, chip_arr) for t, p, q in zip(tags, pair, parts)]

    def plan_swap(me, in_refs, out_refs):
        return [(h, o, _flip(me, 1)) for h, o in zip(in_refs, out_refs)]
    other = _comm_call("grads_swap", mine, [_sds(h.shape, F32) for h in mine], len(mine), plan_swap)
    return [_halves(core, h, o) for h, o in zip(mine, other)]


def _adamw_math(w, g, m, v):
    m = ADAM_B1 * m + (1.0 - ADAM_B1) * g
    v = ADAM_B2 * v + (1.0 - ADAM_B2) * (g * g)
    m_hat = m / (1.0 - ADAM_B1 ** ADAM_STEP)
    v_hat = v / (1.0 - ADAM_B2 ** ADAM_STEP)
    delta = -ADAM_LR * (m_hat / (jnp.sqrt(v_hat) + ADAM_EPS) + ADAM_WD * w)
    return delta, m, v


def _adamw(name, w, g, m, v):
    r, cols = w.shape
    br = _rows(r)
    if br * cols * 4 > (1 << 20) and br % 16 == 0:
        br //= 2

    def body(w_ref, g_ref, m_ref, v_ref, d_ref, mo_ref, vo_ref):
        d_ref[...], mo_ref[...], vo_ref[...] = _adamw_math(w_ref[...], g_ref[...], m_ref[...], v_ref[...])

    spec = pl.BlockSpec((br, cols), lambda i: (i, 0))
    return pl.pallas_call(
        body, name=f"adamw_{name}", grid=(r // br,), in_specs=[spec] * 4, out_specs=[spec] * 3,
        out_shape=[_sds((r, cols), F32)] * 3,
        compiler_params=_cparams(dimension_semantics=("parallel",)),
    )(w, g, m, v)


def _ada_fwd(c_all, ada_w, ada_b_cols):
    n = c_all.shape[0]

    def body(c_ref, w_ref, b_ref, o_ref):
        o_ref[...] = _mmx(_silu(c_ref[...]), w_ref[...]) + b_ref[...]

    return pl.pallas_call(
        body, name="ada_fwd", out_shape=_sds((n, ada_w.shape[1]), F32), compiler_params=_cparams(),
    )(c_all, ada_w, ada_b_cols)


def _ada_bwd(c_all, dmod_cols, w, m, v):
    n = c_all.shape[0]
    r, cols = w.shape
    br = 128

    def body(c_ref, d_ref, w_ref, m_ref, v_ref, g_ref, dl_ref, mo_ref, vo_ref):
        cond = _silu(c_ref[...])
        g = lax.dot_general(cond, d_ref[...], (((0,), (0,)), ((), ())), precision=lax.Precision.HIGHEST,
                            preferred_element_type=F32)
        g_ref[...] = g
        dl_ref[...], mo_ref[...], vo_ref[...] = _adamw_math(w_ref[...], g, m_ref[...], v_ref[...])

    spec = pl.BlockSpec((br, cols), lambda i: (i, 0))
    return pl.pallas_call(
        body, name="ada_bwd", grid=(r // br,),
        in_specs=[pl.BlockSpec((n, br), lambda i: (0, i)), pl.BlockSpec((n, cols), lambda i: (0, 0)), spec, spec, spec],
        out_specs=[spec] * 4, out_shape=[_sds((r, cols), F32)] * 4,
        compiler_params=_cparams(dimension_semantics=("parallel",)),
    )(c_all, dmod_cols, w, m, v)


def _sum_devices(parts):
    n, r, cols = parts.shape

    def body(p_ref, o_ref):
        acc = p_ref[0]
        for k in range(1, n):
            acc = acc + p_ref[k]
        o_ref[...] = acc

    return pl.pallas_call(body, name="sum_devices", out_shape=_sds((r, cols), F32), compiler_params=_cparams())(parts)


SMALL_ROWS = 16
_SMALL_SLOTS = dict(norm1_g=(0, 0, D), norm2_g=(1, 0, D), q_norm_g=(2, 0, HD), k_norm_g=(2, 128, HD), sinks=(2, 256, HEADS),
                    a_log=(2, 384, DN_H), dt_bias=(2, 512, DN_H), dn_norm_g=(2, 640, DN_D))
_CONV_ROW = 4
_ADA_B_ROW = 8


def _pack_small(vals, conv, ada_b):
    sheet = jnp.zeros((SMALL_ROWS, CONVW), F32)
    for name, (row, col, n) in _SMALL_SLOTS.items():
        sheet = sheet.at[row, col:col + n].set(vals[name].reshape(n))
    sheet = sheet.at[_CONV_ROW:_CONV_ROW + CONV, 0:conv.shape[1]].set(conv)
    return sheet.at[_ADA_B_ROW:_ADA_B_ROW + 4, :].set(ada_b.reshape(4, CONVW))


def _unpack_small(sheet, conv_cols):
    out = {name: sheet[row, col:col + n].reshape(1, n) for name, (row, col, n) in _SMALL_SLOTS.items()}
    out["conv_w"] = sheet[_CONV_ROW:_CONV_ROW + CONV, 0:conv_cols].reshape(1, CONV, 1, conv_cols)
    out["ada_b"] = sheet[_ADA_B_ROW:_ADA_B_ROW + 4, :].reshape(1, 6 * D)
    return out


def _w_in_segments():
    shard = IN_WIDTH // N_CHIP
    cuts = sorted({0, IN_WIDTH, C_Z, C_Z + 2 * DN_H} | {k * shard for k in range(1, N_CHIP)})
    segs = []
    for a, b in zip(cuts[:-1], cuts[1:]):
        k = a // shard
        pad = a if a < C_Z else (C_BA + a - C_Z if a < C_Z + 2 * DN_H else a - 2 * DN_H)
        segs.append((k, a - k * shard, b - k * shard, pad))
    return segs


def _pad_w_in(f):
    parts = [f[k][:, lo:hi] for k, lo, hi, _ in sorted(_w_in_segments(), key=lambda s: s[3])]
    return jnp.concatenate(parts + [jnp.zeros((f.shape[1], IN_PAD - IN_WIDTH), f.dtype)], axis=1)


def _unpad_w_in(g):
    return jnp.stack([jnp.concatenate([g[:, pad:pad + hi - lo] for kk, lo, hi, pad in _w_in_segments() if kk == k], axis=1)
                      for k in range(N_CHIP)])


def _blocks_to_cols(f):
    return f.transpose(1, 0, 2).reshape(f.shape[1], N_CHIP * f.shape[2])


def kernel(x, c, positions, ada_w, ada_b, norm1_g, w_in, conv_w, q_norm_g, k_norm_g, sinks, a_log, dt_bias, dn_norm_g, w_branch, w_out, norm2_g, w_gate_up, w_down, loss_target, m_ada_w, m_ada_b, m_norm1_g, m_w_in, m_conv_w, m_q_norm_g, m_k_norm_g, m_sinks, m_a_log, m_dt_bias, m_dn_norm_g, m_w_branch, m_w_out, m_norm2_g, m_w_gate_up, m_w_down, v_ada_w, v_ada_b, v_norm1_g, v_w_in, v_conv_w, v_q_norm_g, v_k_norm_g, v_sinks, v_a_log, v_dt_bias, v_dn_norm_g, v_w_branch, v_w_out, v_norm2_g, v_w_gate_up, v_w_down):
    ix, iy, ic = lax.axis_index("x"), lax.axis_index("y"), lax.axis_index("c")
    dev = 4 * ix + 2 * iy + ic
    chip = 2 * ix + iy
    n_seq = x.shape[0]
    conv_cols = conv_w.shape[-1]

    c_all, conv_all = _gather_devices("gather_cond", [c, conv_w.reshape(CONV, conv_cols)], dev)
    c_all = c_all.reshape(N_DEV * n_seq, D)
    ada_cols = ada_w.shape[-1]
    ada_b_cols = lax.dynamic_slice(ada_b, (0, chip * ada_cols), (1, ada_cols))
    mod_cols = _ada_fwd(c_all, ada_w[0], ada_b_cols)
    (mod_blocks,) = _gather_chips("gather_mod", [mod_cols], chip)
    mod_all = _blocks_to_cols(mod_blocks)
    mod = lax.dynamic_slice(mod_all, (dev * n_seq, 0), (n_seq, 6 * D)).reshape(n_seq, 1, 6 * D)
    conv_full = _blocks_to_cols(conv_all[0::2])

    shards = [w_in[0].astype(BF), w_branch[0].astype(BF), w_out[0].astype(BF), w_gate_up[0].astype(BF), w_down[0].astype(BF)]
    f_in, f_br, f_out, f_gu, f_dn = _gather_weights(shards, chip, ic)
    w_in_pad = _pad_w_in(f_in)
    w_br_full, w_out_full, w_dn_full = (f.reshape(N_CHIP * f.shape[1], f.shape[2]) for f in (f_br, f_out, f_dn))

    loss, grad_x, dmod, small, (g_in, g_br, g_out, g_gu, g_dn) = _local_step(
        x, mod, positions, loss_target, norm1_g, w_in_pad, conv_full.reshape(CONV, 1, CONVW), q_norm_g, k_norm_g, sinks,
        a_log, dt_bias, dn_norm_g, w_br_full, w_out_full, norm2_g, f_gu, w_dn_full)
    loss = lax.psum(loss, ("x", "y", "c"))

    blocks = [_unpad_w_in(g_in), g_br.reshape(N_CHIP, -1, D), g_out.reshape(N_CHIP, -1, D), g_gu, g_dn.reshape(N_CHIP, -1, D)]
    r_in, r_br, r_out, r_gu, r_dn = _reduce_grads(("w_in", "w_branch", "w_out", "w_gate_up", "w_down"), blocks, chip, ic)
    big = {}
    for name, w, g, m, v in (("w_in", w_in, r_in, m_w_in, v_w_in), ("w_branch", w_branch, r_br, m_w_branch, v_w_branch),
                             ("w_out", w_out, r_out, m_w_out, v_w_out), ("w_gate_up", w_gate_up, r_gu, m_w_gate_up, v_w_gate_up),
                             ("w_down", w_down, r_dn, m_w_down, v_w_down)):
        big[name] = (g,) + tuple(_adamw(name, w[0], g, m[0], v[0]))

    part = _pack_small(small, small["conv_w"], jnp.sum(dmod, axis=(0, 1)).reshape(1, 6 * D))
    dmod_all, parts = _gather_devices("gather_small", [dmod.reshape(n_seq, 6 * D), part], dev)
    dmod_all = dmod_all.reshape(N_DEV * n_seq, 6 * D)
    g_small = _unpack_small(_sum_devices(parts), CONVW)
    g_conv = lax.dynamic_slice(g_small["conv_w"].reshape(CONV, CONVW), (0, chip * conv_cols), (CONV, conv_cols))
    g_small["conv_w"] = g_conv.reshape(1, CONV, 1, conv_cols)

    given = dict(norm1_g=(norm1_g, m_norm1_g, v_norm1_g), norm2_g=(norm2_g, m_norm2_g, v_norm2_g),
                 q_norm_g=(q_norm_g, m_q_norm_g, v_q_norm_g), k_norm_g=(k_norm_g, m_k_norm_g, v_k_norm_g),
                 sinks=(sinks, m_sinks, v_sinks), a_log=(a_log, m_a_log, v_a_log), dt_bias=(dt_bias, m_dt_bias, v_dt_bias),
                 dn_norm_g=(dn_norm_g, m_dn_norm_g, v_dn_norm_g))
    sheets = [_pack_small({k: t[j] for k, t in given.items()}, cw.reshape(CONV, conv_cols), ab)
              for j, (cw, ab) in enumerate(((conv_w, ada_b), (m_conv_w, m_ada_b), (v_conv_w, v_ada_b)))]
    g_local = _pack_small(g_small, g_conv, g_small["ada_b"])
    upd = [_unpack_small(s, conv_cols) for s in _adamw("small", sheets[0], g_local, sheets[1], sheets[2])]

    dmod_cols = lax.dynamic_slice(dmod_all, (0, chip * ada_cols), (N_DEV * n_seq, ada_cols))
    ada = _ada_bwd(c_all, dmod_cols, ada_w[0], m_ada_w[0], v_ada_w[0])

    names = ["ada_w", "ada_b", "norm1_g", "w_in", "conv_w", "q_norm_g", "k_norm_g", "sinks", "a_log", "dt_bias", "dn_norm_g",
             "w_branch", "w_out", "norm2_g", "w_gate_up", "w_down"]

    def leaf(name, j):
        if name == "ada_w":
            return ada[j][None]
        if name in big:
            return big[name][j][None]
        return g_small[name] if j == 0 else upd[j - 1][name]

    return (loss, grad_x) + tuple(leaf(n, j) for j in range(4) for n in names)
```

```python
import functools

import jax
import jax.numpy as jnp
import numpy as np
from jax import lax
from jax.experimental import pallas as pl
from jax.experimental.pallas import tpu as pltpu

F32 = jnp.float32
BF = jnp.bfloat16

D = 1024
HEADS = 8
KV_HEADS = 2
GROUP = 4
HD = 64
BLK = 128
ROT = 16
THETA = 500000.0
QW = 512
KVW = 128
DN_H = 4
DN_D = 128
CONV = 4
CHUNK = 64
DNW = 512
CONVW = 1536
FFN = 2816
EPS = 1e-6
IN_WIDTH = 4872
IN_PAD = 4992
C_KV = 512
C_DN = 768
C_Z = 2304
C_GA = 2816
C_GB = 3840
C_BA = 4864
NEG = -1e30
N_DEV = 8
N_CHIP = 4

ADAM_LR = 0.001
ADAM_B1 = 0.9
ADAM_B2 = 0.999
ADAM_EPS = 1e-08
ADAM_WD = 0.01
ADAM_STEP = 10

VMEM_LIMIT = 60 * 1024 * 1024


def _cparams(**kw):
    return pltpu.CompilerParams(vmem_limit_bytes=VMEM_LIMIT, **kw)


def _dg(a, b, ca, cb):
    return lax.dot_general(a.astype(BF), b.astype(BF), (((ca,), (cb,)), ((), ())),
                           preferred_element_type=F32)


@jax.custom_vjp
def _mm(a, b):
    return _dg(a, b, 1, 0)


def _mm_fwd(a, b):
    return _dg(a, b, 1, 0), (a, b)


def _mm_bwd(res, dy):
    a, b = res
    return _dg(dy, b, 1, 1).astype(a.dtype), _dg(a, dy, 0, 0).astype(b.dtype)


_mm.defvjp(_mm_fwd, _mm_bwd)


@jax.custom_vjp
def _mm_nt(a, b):
    return _dg(a, b, 1, 1)


def _mm_nt_fwd(a, b):
    return _dg(a, b, 1, 1), (a, b)


def _mm_nt_bwd(res, dy):
    a, b = res
    return _dg(dy, b, 1, 0).astype(a.dtype), _dg(dy, a, 0, 0).astype(b.dtype)


_mm_nt.defvjp(_mm_nt_fwd, _mm_nt_bwd)


@jax.custom_vjp
def _mm_tn(a, b):
    return _dg(a, b, 0, 0)


def _mm_tn_fwd(a, b):
    return _dg(a, b, 0, 0), (a, b)


def _mm_tn_bwd(res, dy):
    a, b = res
    return _dg(b, dy, 1, 1).astype(a.dtype), _dg(a, dy, 1, 0).astype(b.dtype)


_mm_tn.defvjp(_mm_tn_fwd, _mm_tn_bwd)


def _mmx(a, b):
    return jnp.dot(a, b, precision=lax.Precision.HIGHEST, preferred_element_type=F32)


def _mmx_nt(a, b):
    return lax.dot_general(a, b, (((1,), (1,)), ((), ())), precision=lax.Precision.HIGHEST,
                           preferred_element_type=F32)


def _iota(shape, dim):
    return lax.broadcasted_iota(jnp.int32, shape, dim)


def _sigmoid(x):
    return 1.0 / (1.0 + jnp.exp(-x))


def _silu(x):
    return x * _sigmoid(x)


def _softplus(x):
    return jnp.maximum(x, 0.0) + jnp.log(1.0 + jnp.exp(-jnp.abs(x)))


def _rms(x, gain):
    return x * lax.rsqrt(jnp.mean(x * x, axis=-1, keepdims=True) + EPS) * gain


def _norm_mod(x, gain, shift, scale):
    return _rms(x, gain) * (1.0 + scale) + shift


def _split(a):
    hi = a.astype(BF)
    return hi, (a - hi.astype(F32)).astype(BF)


def _dg3(a, b, ca, cb):
    ah, al = _split(a)
    bh, bl = _split(b)

    def dg(x, y):
        return lax.dot_general(x, y, (((ca,), (cb,)), ((), ())), preferred_element_type=F32)
    return dg(ah, bh) + (dg(ah, bl) + dg(al, bh))


@jax.custom_vjp
def _mm3(a, b):
    return _dg3(a, b, 1, 0)


def _mm3_fwd(a, b):
    return _dg3(a, b, 1, 0), (a, b)


def _mm3_bwd(res, dy):
    a, b = res
    return _dg3(dy, b, 1, 1), _dg3(a, dy, 0, 0)


_mm3.defvjp(_mm3_fwd, _mm3_bwd)


def _qk_prep(slabs, gain, cos, sin):
    r = _iota((2 * HD, 2 * HD), 0)
    c = _iota((2 * HD, 2 * HD), 1)
    seg = jnp.where(r // HD == c // HD, 1.0 / HD, 0.0).astype(F32)
    half = ROT // 2
    cd = c % HD
    pair = jnp.where(((cd < half) & (r == c + half)) | ((cd >= half) & (cd < ROT) & (r == c - half)), 1.0, 0.0).astype(F32)
    out = []
    for x in slabs:
        y = x * lax.rsqrt(_mm3(x * x, seg) + EPS) * gain
        out.append(y * cos + _mm3(y, pair) * sin)
    return out


def _attn_block(qs, kc, kp, vc, vp, sinks, has_prev):
    rows = GROUP * BLK
    qi = _iota((rows, 2 * BLK), 0) % BLK + BLK
    kj = _iota((rows, 2 * BLK), 1)
    dist = qi - kj
    valid = (dist >= 0) & (dist < BLK) & ((kj >= BLK) | has_prev)
    grp = _iota((rows, HEADS), 0) // BLK
    col = _iota((rows, HEADS), 1)
    outs = []
    for h in range(KV_HEADS):
        q = jnp.concatenate([qs[h * GROUP + g] for g in range(GROUP)], axis=0)
        k = jnp.concatenate([kp[h], kc[h]], axis=0)
        v = jnp.concatenate([vp[h], vc[h]], axis=0)
        s = _mm_nt(q, k) * (HD ** -0.5)
        s = jnp.where(valid, s, NEG)
        sink = jnp.sum(jnp.where(col == h * GROUP + grp, sinks, 0.0), axis=-1, keepdims=True)
        m = lax.stop_gradient(jnp.maximum(jnp.max(s, axis=-1, keepdims=True), sink))
        p = jnp.exp(s - m)
        probs = p / (jnp.sum(p, axis=-1, keepdims=True) + jnp.exp(sink - m))
        o = _mm(probs, v)
        outs += [o[g * BLK:(g + 1) * BLK] for g in range(GROUP)]
    return outs


def _dn_tail(ys, ba, alog, dtb):
    def l2(t):
        return t * lax.rsqrt(jnp.sum(t * t, axis=-1, keepdims=True) + EPS)
    s = [_silu(y) for y in ys]
    out = [l2(t) for t in s[:2 * DN_H]] + s[2 * DN_H:]
    lane = _iota(ba.shape, 1)
    beta = _sigmoid(ba)
    g = -jnp.exp(alog) * _softplus(ba + dtb)
    bg = jnp.where(lane < DN_H, beta, jnp.where(lane < 2 * DN_H, g, 0.0))
    return out, bg


def _bdg(a, b, ca, cb):
    return lax.dot_general(a.astype(BF), b.astype(BF), (((ca,), (cb,)), ((0,), (0,))), preferred_element_type=F32)


@jax.custom_vjp
def _bmm(a, b):
    return _bdg(a, b, 2, 1)


def _bmm_fwd(a, b):
    return _bdg(a, b, 2, 1), (a, b)


def _bmm_bwd(res, dy):
    a, b = res
    return _bdg(dy, b, 2, 2), _bdg(a, dy, 1, 1)


_bmm.defvjp(_bmm_fwd, _bmm_bwd)


@jax.custom_vjp
def _bmm_nt(a, b):
    return _bdg(a, b, 2, 2)


def _bmm_nt_fwd(a, b):
    return _bdg(a, b, 2, 2), (a, b)


def _bmm_nt_bwd(res, dy):
    a, b = res
    return _bdg(dy, b, 2, 1), _bdg(dy, a, 1, 1)


_bmm_nt.defvjp(_bmm_nt_fwd, _bmm_nt_bwd)


def _bmmx(a, b):
    return lax.dot_general(a, b, (((2,), (1,)), ((0,), (0,))), precision=lax.Precision.HIGHEST,
                           preferred_element_type=F32)


def _neumann_inverse(lmat):
    C = CHUNK
    eye = jnp.where(_iota((C, C), 0) == _iota((C, C), 1), 1.0, 0.0).astype(F32)[None]
    a = -lmat
    tinv = eye + a
    pw = _bmmx(a, a)
    for _ in range(4):
        both = _bmmx(jnp.concatenate([pw, tinv], axis=1), pw)
        pw, tinv = both[:, :C], tinv + both[:, C:]
    return tinv + _bmmx(tinv, pw)


def _inverse_bwd(tinv, d_tinv):
    x = lax.dot_general(d_tinv, tinv, (((2,), (2,)), ((0,), (0,))), precision=lax.Precision.HIGHEST,
                        preferred_element_type=F32)
    return -lax.dot_general(tinv, x, (((1,), (1,)), ((0,), (0,))), precision=lax.Precision.HIGHEST,
                            preferred_element_type=F32)


@jax.custom_vjp
def _tri_inverse(lmat):
    return _neumann_inverse(lmat)


def _tri_inverse_fwd(lmat):
    tinv = _neumann_inverse(lmat)
    return tinv, tinv


def _tri_inverse_bwd(tinv, d_tinv):
    return (_inverse_bwd(tinv, d_tinv),)


_tri_inverse.defvjp(_tri_inverse_fwd, _tri_inverse_bwd)


@jax.custom_vjp
def _tri_inverse_known(lmat, tinv):
    return tinv


def _tri_inverse_known_fwd(lmat, tinv):
    return tinv, tinv


def _tri_inverse_known_bwd(tinv, d_tinv):
    return _inverse_bwd(tinv, d_tinv), jnp.zeros_like(tinv)


_tri_inverse_known.defvjp(_tri_inverse_known_fwd, _tri_inverse_known_bwd)


def _dn_intra(q, k, v, bg, tinv=None):
    C = CHUNK
    G = bg.shape[0]
    r = _iota((C, C), 0)
    c = _iota((C, C), 1)
    incl = (r >= c)[None]
    strict = (r > c)[None]
    eye = jnp.where(r == c, 1.0, 0.0).astype(F32)[None]
    tri = jnp.broadcast_to(jnp.where(r >= c, 1.0, 0.0).astype(F32)[None], (G, C, C))
    gc_all = _bmmx(tri, bg)
    lane = _iota((C, DN_D), 1)

    def per_head(x, offset):
        return jnp.concatenate([jnp.sum(jnp.where(lane == offset + h, x[g], 0.0), axis=-1, keepdims=True)[None]
                                for g in range(G) for h in range(DN_H)], axis=0)
    beta = per_head(bg, 0)
    gcol = per_head(gc_all, DN_H)
    grow = jnp.sum(eye * gcol, axis=1, keepdims=True)
    glast = jnp.sum(jnp.where(_iota((1, C, 1), 1) == C - 1, gcol, 0.0), axis=1, keepdims=True)
    decay = jnp.exp(jnp.where(incl, gcol - grow, NEG))
    q = q * (DN_D ** -0.5)
    kb = k * beta
    lmat = jnp.where(strict, _bmm_nt(kb, k) * decay, 0.0)
    tinv = _tri_inverse(lmat) if tinv is None else _tri_inverse_known(lmat, tinv)
    egc = jnp.exp(gcol)
    u = _bmm(tinv, v * beta)
    w = _bmm(tinv, kb * egc)
    a = _bmm_nt(q, k) * decay
    return u, w, q * egc, k * jnp.exp(glast - gcol), a, jnp.exp(glast), tinv


@jax.custom_vjp
def _bmm_tn(a, b):
    return _bdg(a, b, 1, 1)


def _bmm_tn_fwd(a, b):
    return _bdg(a, b, 1, 1), (a, b)


def _bmm_tn_bwd(res, dy):
    a, b = res
    return _bdg(b, dy, 2, 2), _bdg(a, dy, 2, 1)


_bmm_tn.defvjp(_bmm_tn_fwd, _bmm_tn_bwd)


def _dn_rec(state, u, w, qd, kd, a, cd):
    v_new = u - _bmm(w, state)
    out = _bmm(qd, state) + _bmm(a, v_new)
    return state * cd + _bmm_tn(kd, v_new), out


def _mix_tile(o_attn, o_raw, zs, ga, gb, x, gate1, dn_g, wb_a, wb_d, w_out, p_ya, p_yd, p_out):
    o_dn = jnp.concatenate([_rms(o_raw[h], dn_g) * _silu(zs[h]) for h in range(DN_H)], axis=-1)
    y_a = _mm(o_attn, wb_a) + p_ya
    y_d = _mm(o_dn, wb_d) + p_yd
    merged = _sigmoid(ga) * y_a + _sigmoid(gb) * y_d
    out = _mm(merged, w_out) + p_out
    return x + gate1 * out, o_dn, merged


def _mlp_tile(x1, gain, shift, scale, gate2, w_gu, w_dn, tgt, p_gu, p_yy):
    h2 = _norm_mod(x1, gain, shift, scale)
    gu = jnp.concatenate([_mm(h2, w) for w in w_gu], axis=-1) + p_gu
    act = _silu(gu[:, :FFN]) * gu[:, FFN:]
    yy = _mm(act, w_dn) + p_yy
    y = x1 + gate2 * yy
    err = y - tgt
    return 0.5 * jnp.sum(err * err) * (1.0 / D), (h2, act)


def _tok(bt, f):
    return pl.BlockSpec((None, bt, f), lambda b, i: (b, i, 0))


def _full(shape):
    return pl.BlockSpec(shape, lambda b, i: (0,) * len(shape))


def _per_batch(f):
    return pl.BlockSpec((None, 1, f), lambda b, i: (b, 0, 0))


def _sds(shape, dtype):
    return jax.ShapeDtypeStruct(shape, dtype)


def _acc(ref, val, first):
    @pl.when(first)
    def _():
        ref[...] = val

    @pl.when(jnp.logical_not(first))
    def _():
        ref[...] += val


def _in_proj(x, mod, norm1_g, w_in, bt):
    B, S, _ = x.shape

    def body(x_ref, mod_ref, g_ref, w_ref, q_ref, kv_ref, dn_ref, z_ref, ga_ref, gb_ref, ba_ref, h_ref):
        h = _norm_mod(x_ref[...], g_ref[...], mod_ref[:, 0:D], mod_ref[:, D:2 * D]).astype(BF)
        h_ref[...] = h

        def proj(c0, c1):
            return jnp.dot(h, w_ref[:, c0:c1], preferred_element_type=F32)
        q_ref[...] = proj(0, C_KV).astype(BF)
        kv_ref[...] = proj(C_KV, C_DN).astype(BF)
        dn_ref[...] = proj(C_DN, C_Z).astype(BF)
        z_ref[...] = proj(C_Z, C_GA).astype(BF)
        ga_ref[...] = proj(C_GA, C_GB).astype(BF)
        gb_ref[...] = proj(C_GB, C_BA).astype(BF)
        ba_ref[...] = proj(C_BA, IN_PAD)

    widths = (QW, 2 * KVW, CONVW, DNW, D, D)
    return pl.pallas_call(
        body, name="in_proj", grid=(B, S // bt),
        in_specs=[_tok(bt, D), _per_batch(6 * D), _full((1, D)), _full((D, IN_PAD))],
        out_specs=[_tok(bt, w) for w in widths] + [_tok(bt, 128), _tok(bt, D)],
        out_shape=[_sds((B, S, w), BF) for w in widths] + [_sds((B, S, 128), F32), _sds((B, S, D), BF)],
        compiler_params=_cparams(dimension_semantics=("parallel", "parallel")),
    )(x, mod, norm1_g, w_in)


def _prev_blk(bt, f):
    return pl.BlockSpec((None, bt, f), lambda b, i: (b, jnp.maximum(i - 1, 0), 0))


QKV = QW + 2 * KVW


def _qk_slabs(q_ref, kv_ref):
    return ([q_ref[:, j * 2 * HD:(j + 1) * 2 * HD].astype(F32) for j in range(QW // (2 * HD))],
            [kv_ref[:, 0:KVW].astype(F32)])


def _qk_prep_fwd(q, kv, cos, sin, qg, kg, bt):
    B, S, _ = q.shape

    def body(q_ref, kv_ref, cos_ref, sin_ref, qg_ref, kg_ref, o_ref):
        qs, ks = _qk_slabs(q_ref, kv_ref)
        qn = _qk_prep(qs, qg_ref[...], cos_ref[...], sin_ref[...])
        kn = _qk_prep(ks, kg_ref[...], cos_ref[...], sin_ref[...])
        for j, t in enumerate(qn + kn):
            o_ref[:, j * 2 * HD:(j + 1) * 2 * HD] = t.astype(BF)
        o_ref[:, QW + KVW:QKV] = kv_ref[:, KVW:2 * KVW]

    return pl.pallas_call(
        body, name="qk_prep_fwd", grid=(B, S // bt),
        in_specs=[_tok(bt, QW), _tok(bt, 2 * KVW), _tok(bt, 2 * HD), _tok(bt, 2 * HD), _full((1, 2 * HD)), _full((1, 2 * HD))],
        out_specs=_tok(bt, QKV), out_shape=_sds((B, S, QKV), BF),
        compiler_params=_cparams(dimension_semantics=("parallel", "parallel")),
    )(q, kv, cos, sin, qg, kg)


def _qk_prep_bwd(q, kv, cos, sin, qg, kg, dqn, dkvn, bt):
    B, S, _ = q.shape

    def body(q_ref, kv_ref, cos_ref, sin_ref, qg_ref, kg_ref, dqn_ref, dkvn_ref, dq_ref, dkv_ref, dqg_ref, dkg_ref):
        qs, ks = _qk_slabs(q_ref, kv_ref)
        cos, sin = cos_ref[...], sin_ref[...]

        def f(qs, ks, qg, kg):
            return _qk_prep(qs, qg, cos, sin), _qk_prep(ks, kg, cos, sin)
        _, vjp = jax.vjp(f, qs, ks, qg_ref[...], kg_ref[...])
        n_q = len(qs)
        d_q = [dqn_ref[:, j * 2 * HD:(j + 1) * 2 * HD].astype(F32) for j in range(n_q)]
        d_k = [dkvn_ref[:, 0:KVW].astype(F32)]
        dqs, dks, dqg, dkg = vjp((d_q, d_k))
        for j in range(n_q):
            dq_ref[:, j * 2 * HD:(j + 1) * 2 * HD] = dqs[j].astype(BF)
        dkv_ref[:, 0:KVW] = dks[0].astype(BF)
        dkv_ref[:, KVW:2 * KVW] = dkvn_ref[:, KVW:2 * KVW]
        first = (pl.program_id(0) == 0) & (pl.program_id(1) == 0)
        _acc(dqg_ref, dqg, first)
        _acc(dkg_ref, dkg, first)

    return pl.pallas_call(
        body, name="qk_prep_bwd", grid=(B, S // bt),
        in_specs=[_tok(bt, QW), _tok(bt, 2 * KVW), _tok(bt, 2 * HD), _tok(bt, 2 * HD), _full((1, 2 * HD)), _full((1, 2 * HD)),
                  _tok(bt, QW), _tok(bt, 2 * KVW)],
        out_specs=[_tok(bt, QW), _tok(bt, 2 * KVW), _full((1, 2 * HD)), _full((1, 2 * HD))],
        out_shape=[_sds((B, S, QW), BF), _sds((B, S, 2 * KVW), BF), _sds((1, 2 * HD), F32), _sds((1, 2 * HD), F32)],
        compiler_params=_cparams(dimension_semantics=("arbitrary", "arbitrary")),
    )(q, kv, cos, sin, qg, kg, dqn, dkvn)


def _attn_load(qkv_ref, kvp_ref):
    qs = [qkv_ref[:, h * HD:(h + 1) * HD].astype(F32) for h in range(HEADS)]
    kc = [qkv_ref[:, QW + h * HD:QW + (h + 1) * HD].astype(F32) for h in range(KV_HEADS)]
    vc = [qkv_ref[:, QW + KVW + h * HD:QW + KVW + (h + 1) * HD].astype(F32) for h in range(KV_HEADS)]
    kp = [kvp_ref[:, h * HD:(h + 1) * HD].astype(F32) for h in range(KV_HEADS)]
    vp = [kvp_ref[:, KVW + h * HD:KVW + (h + 1) * HD].astype(F32) for h in range(KV_HEADS)]
    return qs, kc, kp, vc, vp


def _kv_prev_spec(index):
    return pl.BlockSpec((None, BLK, 2 * KVW), lambda b, i: (b, index(i), QW // (2 * KVW)))


def _attn_fwd(qkv, sinks):
    B, S, _ = qkv.shape

    def body(qkv_ref, kvp_ref, sk_ref, o_ref):
        qs, kc, kp, vc, vp = _attn_load(qkv_ref, kvp_ref)
        outs = _attn_block(qs, kc, kp, vc, vp, sk_ref[...], pl.program_id(1) > 0)
        for h in range(HEADS):
            o_ref[:, h * HD:(h + 1) * HD] = outs[h].astype(BF)

    return pl.pallas_call(
        body, name="attn_fwd", grid=(B, S // BLK),
        in_specs=[_tok(BLK, QKV), _kv_prev_spec(lambda i: jnp.maximum(i - 1, 0)), _full((1, HEADS))],
        out_specs=_tok(BLK, QW), out_shape=_sds((B, S, QW), BF),
        compiler_params=_cparams(dimension_semantics=("parallel", "parallel")),
    )(qkv, qkv, sinks)


def _conv_fwd_tile(xe_ref, x_ref, halo_ref, cw_ref, first, bt):
    halo = halo_ref[...].astype(F32)
    xe_ref[0:8, :] = jnp.where(first, 0.0, halo)
    xe_ref[8:bt + 8, :] = x_ref[...].astype(F32)
    y = cw_ref[0:1, :] * xe_ref[5:bt + 5, :]
    for j in range(1, CONV):
        y = y + cw_ref[j:j + 1, :] * xe_ref[5 + j:bt + 5 + j, :]
    return y


def _halo_spec(bt):
    return pl.BlockSpec((None, 8, CONVW), lambda b, i: (b, jnp.maximum(i * (bt // 8) - 1, 0), 0))


def _dn_prep(dn, ba, conv_w, alog, dtb, bt):
    B, S, _ = dn.shape

    def body(x_ref, halo_ref, ba_ref, cw_ref, al_ref, dt_ref, qkv_ref, bg_ref, xe_ref):
        y = _conv_fwd_tile(xe_ref, x_ref, halo_ref, cw_ref, pl.program_id(1) == 0, bt)
        ys = [y[:, j * DN_D:(j + 1) * DN_D] for j in range(3 * DN_H)]
        out, bg = _dn_tail(ys, ba_ref[...], al_ref[...], dt_ref[...])
        for j in range(3 * DN_H):
            qkv_ref[:, j * DN_D:(j + 1) * DN_D] = out[j]
        bg_ref[...] = bg

    return pl.pallas_call(
        body, name="dn_prep", grid=(B, S // bt),
        in_specs=[_tok(bt, CONVW), _halo_spec(bt), _tok(bt, 128), _full((CONV, CONVW)), _full((1, 128)), _full((1, 128))],
        out_specs=[_tok(bt, CONVW), _tok(bt, 128)],
        out_shape=[_sds((B, S, CONVW), F32), _sds((B, S, 128), F32)],
        scratch_shapes=[pltpu.VMEM((bt + 8, CONVW), F32)],
        compiler_params=_cparams(dimension_semantics=("parallel", "arbitrary")),
    )(dn, dn, ba, conv_w, alog, dtb)


def _dn_load(qkv_ref):
    qs = [qkv_ref[:, h * DN_D:(h + 1) * DN_D] for h in range(DN_H)]
    ks = [qkv_ref[:, DNW + h * DN_D:DNW + (h + 1) * DN_D] for h in range(DN_H)]
    vs = [qkv_ref[:, 2 * DNW + h * DN_D:2 * DNW + (h + 1) * DN_D] for h in range(DN_H)]
    return qs, ks, vs


DN_GROUP = 4
AW = DN_H * CHUNK


def _stack_heads(ref, G, offset, width):
    return jnp.stack([ref[g * CHUNK:(g + 1) * CHUNK, offset + h * width:offset + (h + 1) * width]
                      for g in range(G) for h in range(DN_H)])


def _dn_load_stack(qkv_ref, G):
    return tuple(_stack_heads(qkv_ref, G, j * DNW, DN_D) for j in range(3))


def _cd_spec(n):
    return pl.BlockSpec((None, n, 1, DN_D), lambda b, i: (b, i, 0, 0))


def _dn_intra_fwd(qkv, bg):
    B, S, _ = qkv.shape
    nc = S // CHUNK
    G = min(DN_GROUP, nc)
    rows = G * CHUNK

    def body(qkv_ref, bg_ref, u_ref, w_ref, qd_ref, kd_ref, a_ref, cd_ref, t_ref):
        q, k, v = _dn_load_stack(qkv_ref, G)
        u, w, qd, kd, a, cd, tinv = _dn_intra(q, k, v, bg_ref[...].reshape(G, CHUNK, DN_D))
        lane_row = _iota((1, DN_D), 1)
        for g in range(G):
            rows = slice(g * CHUNK, (g + 1) * CHUNK)
            cd_row = jnp.zeros((1, DN_D), F32)
            for h in range(DN_H):
                n = g * DN_H + h
                cols = slice(h * DN_D, (h + 1) * DN_D)
                u_ref[rows, cols] = u[n]
                w_ref[rows, cols] = w[n].astype(BF)
                qd_ref[rows, cols] = qd[n].astype(BF)
                kd_ref[rows, cols] = kd[n].astype(BF)
                a_ref[rows, h * CHUNK:(h + 1) * CHUNK] = a[n].astype(BF)
                t_ref[rows, h * CHUNK:(h + 1) * CHUNK] = tinv[n]
                cd_row = cd_row + jnp.where(lane_row == h, cd[n], 0.0)
            cd_ref[g] = cd_row

    return pl.pallas_call(
        body, name="dn_intra_fwd", grid=(B, nc // G),
        in_specs=[_tok(rows, CONVW), _tok(rows, 128)],
        out_specs=[_tok(rows, DNW)] * 4 + [_tok(rows, AW), _cd_spec(G), _tok(rows, AW)],
        out_shape=[_sds((B, S, DNW), F32)] + [_sds((B, S, DNW), BF)] * 3 + [_sds((B, S, AW), BF), _sds((B, nc, 1, DN_D), F32),
                                                                            _sds((B, S, AW), F32)],
        compiler_params=_cparams(dimension_semantics=("parallel", "parallel")),
    )(qkv, bg)


def _rec_stack(ref, B, width):
    return jnp.stack([ref[b, :, h * width:(h + 1) * width].astype(F32) for b in range(B) for h in range(DN_H)])


def _rec_load(B, u_ref, w_ref, qd_ref, kd_ref, a_ref, cd_ref):
    lane_row = _iota((1, DN_D), 1)
    cd = jnp.stack([jnp.sum(jnp.where(lane_row == h, cd_ref[b, 0], 0.0), axis=-1, keepdims=True)
                    for b in range(B) for h in range(DN_H)])
    return (_rec_stack(u_ref, B, DN_D), _rec_stack(w_ref, B, DN_D), _rec_stack(qd_ref, B, DN_D),
            _rec_stack(kd_ref, B, DN_D), _rec_stack(a_ref, B, CHUNK), cd)


def _rec_store(B, ref, val, width):
    for b in range(B):
        for h in range(DN_H):
            ref[b, :, h * width:(h + 1) * width] = val[b * DN_H + h]


def _rec_specs(B, index):
    def tok(f):
        return pl.BlockSpec((B, CHUNK, f), lambda i: (0, index(i), 0))
    cd = pl.BlockSpec((B, 1, 1, DN_D), lambda i: (0, index(i), 0, 0))
    st = pl.BlockSpec((B, None, DN_H, DN_D, DN_D), lambda i: (0, index(i), 0, 0, 0))
    return tok, cd, st


def _dn_rec_fwd(u, w, qd, kd, a, cd):
    B, S, _ = u.shape
    nc = S // CHUNK
    tok, cd_spec, st_spec = _rec_specs(B, lambda i: i)

    def body(u_ref, w_ref, qd_ref, kd_ref, a_ref, cd_ref, o_ref, st_ref, s_ref):
        @pl.when(pl.program_id(0) == 0)
        def _():
            s_ref[...] = jnp.zeros_like(s_ref)
        state = s_ref[...]
        st_ref[...] = state.reshape(B, DN_H, DN_D, DN_D)
        new_state, out = _dn_rec(state, *_rec_load(B, u_ref, w_ref, qd_ref, kd_ref, a_ref, cd_ref))
        s_ref[...] = new_state
        _rec_store(B, o_ref, out, DN_D)

    return pl.pallas_call(
        body, name="dn_rec_fwd", grid=(nc,),
        in_specs=[tok(DNW)] * 4 + [tok(AW), cd_spec],
        out_specs=[tok(DNW), st_spec],
        out_shape=[_sds((B, S, DNW), F32), _sds((B, nc, DN_H, DN_D, DN_D), F32)],
        scratch_shapes=[pltpu.VMEM((B * DN_H, DN_D, DN_D), F32)],
        compiler_params=_cparams(dimension_semantics=("arbitrary",)),
    )(u, w, qd, kd, a, cd)


def _mix_load(oa_ref, or_ref, z_ref):
    o_raw = [or_ref[:, h * DN_D:(h + 1) * DN_D] for h in range(DN_H)]
    zs = [z_ref[:, h * DN_D:(h + 1) * DN_D].astype(F32) for h in range(DN_H)]
    return oa_ref[...].astype(F32), o_raw, zs


def _mix_fwd(o_attn, o_raw, z, ga, gb, x, mod, dn_g, w_branch, w_out, bt):
    B, S, _ = x.shape

    def body(oa_ref, or_ref, z_ref, ga_ref, gb_ref, x_ref, mod_ref, dg_ref, wb_ref, wo_ref, x1_ref, od_ref, mg_ref):
        oa, o_r, zs = _mix_load(oa_ref, or_ref, z_ref)
        x1, o_dn, merged = _mix_tile(oa, o_r, zs, ga_ref[...].astype(F32), gb_ref[...].astype(F32), x_ref[...],
                                     mod_ref[:, 2 * D:3 * D], dg_ref[...], wb_ref[0:QW, :], wb_ref[QW:2 * QW, :],
                                     wo_ref[...], 0.0, 0.0, 0.0)
        x1_ref[...] = x1
        od_ref[...] = o_dn.astype(BF)
        mg_ref[...] = merged.astype(BF)

    return pl.pallas_call(
        body, name="mix_fwd", grid=(B, S // bt),
        in_specs=[_tok(bt, QW), _tok(bt, DNW), _tok(bt, DNW), _tok(bt, D), _tok(bt, D), _tok(bt, D), _per_batch(6 * D),
                  _full((1, DN_D)), _full((D, D)), _full((D, D))],
        out_specs=[_tok(bt, D), _tok(bt, DNW), _tok(bt, D)],
        out_shape=[_sds((B, S, D), F32), _sds((B, S, DNW), BF), _sds((B, S, D), BF)],
        compiler_params=_cparams(dimension_semantics=("parallel", "parallel")),
    )(o_attn, o_raw, z, ga, gb, x, mod, dn_g, w_branch, w_out)


def _mlp(x1, tgt, mod, norm2_g, w_gu, w_dn, bt):
    B, S, _ = x1.shape

    def body(x1_ref, t_ref, mod_ref, g_ref, wgu_ref, wdn_ref,
             dx1_ref, h2_ref, act_ref, dgu_ref, dyy_ref, loss_ref, dmod_ref, dg_ref):
        w_gu_v, w_dn_v, t = [wgu_ref[k] for k in range(N_CHIP)], wdn_ref[...], t_ref[...]

        def f(x1, gain, shift, scale, gate2, p_gu, p_yy):
            return _mlp_tile(x1, gain, shift, scale, gate2, w_gu_v, w_dn_v, t, p_gu, p_yy)
        zero_gu = jnp.zeros((bt, 2 * FFN), F32)
        zero_yy = jnp.zeros((bt, D), F32)
        loss, vjp, (h2, act) = jax.vjp(f, x1_ref[...], g_ref[...], mod_ref[:, 3 * D:4 * D], mod_ref[:, 4 * D:5 * D],
                                       mod_ref[:, 5 * D:6 * D], zero_gu, zero_yy, has_aux=True)
        dx1, dgain, dshift, dscale, dgate2, dgu, dyy = vjp(jnp.ones((), F32))
        dx1_ref[...] = dx1
        h2_ref[...] = h2.astype(BF)
        act_ref[...] = act.astype(BF)
        dgu_ref[...] = dgu.astype(BF)
        dyy_ref[...] = dyy.astype(BF)
        first = pl.program_id(1) == 0
        _acc(loss_ref, jnp.reshape(loss, (1, 1)), first)
        _acc(dmod_ref, jnp.concatenate([dshift, dscale, dgate2], axis=-1), first)
        _acc(dg_ref, dgain, first)

    return pl.pallas_call(
        body, name="mlp", grid=(B, S // bt),
        in_specs=[_tok(bt, D), _tok(bt, D), _per_batch(6 * D), _full((1, D)), _full((N_CHIP, D, 2 * FFN // N_CHIP)),
                  _full((FFN, D))],
        out_specs=[_tok(bt, D), _tok(bt, D), _tok(bt, FFN), _tok(bt, 2 * FFN), _tok(bt, D),
                   _per_batch(1), _per_batch(3 * D), _per_batch(D)],
        out_shape=[_sds((B, S, D), F32), _sds((B, S, D), BF), _sds((B, S, FFN), BF), _sds((B, S, 2 * FFN), BF),
                   _sds((B, S, D), BF), _sds((B, 1, 1), F32), _sds((B, 1, 3 * D), F32), _sds((B, 1, D), F32)],
        compiler_params=_cparams(dimension_semantics=("parallel", "arbitrary")),
    )(x1, tgt, mod, norm2_g, w_gu, w_dn)


def _mix_bwd(o_attn, o_raw, z, ga, gb, x, mod, dn_g, w_branch, w_out, dx1, bt):
    B, S, _ = x.shape

    def body(oa_ref, or_ref, z_ref, ga_ref, gb_ref, x_ref, mod_ref, dg_ref, wb_ref, wo_ref, dx1_ref,
             doa_ref, dor_ref, dz_ref, dga_ref, dgb_ref, dya_ref, dyd_ref, dout_ref, dgate_ref, ddg_ref):
        oa, o_r, zs = _mix_load(oa_ref, or_ref, z_ref)
        wb_a, wb_d, wo = wb_ref[0:QW, :], wb_ref[QW:2 * QW, :], wo_ref[...]

        def f(oa, o_r, zs, ga, gb, gate1, dn_g, p_ya, p_yd, p_out):
            return _mix_tile(oa, o_r, zs, ga, gb, x_ref[...], gate1, dn_g, wb_a, wb_d, wo, p_ya, p_yd, p_out)[0]
        zero = jnp.zeros((bt, D), F32)
        _, vjp = jax.vjp(f, oa, o_r, zs, ga_ref[...].astype(F32), gb_ref[...].astype(F32), mod_ref[:, 2 * D:3 * D],
                         dg_ref[...], zero, zero, zero)
        doa, dor, dzs, dga, dgb, dgate1, ddn_g, dya, dyd, dout = vjp(dx1_ref[...])
        doa_ref[...] = doa
        for h in range(DN_H):
            dor_ref[:, h * DN_D:(h + 1) * DN_D] = dor[h]
            dz_ref[:, h * DN_D:(h + 1) * DN_D] = dzs[h].astype(BF)
        dga_ref[...] = dga.astype(BF)
        dgb_ref[...] = dgb.astype(BF)
        dya_ref[...] = dya.astype(BF)
        dyd_ref[...] = dyd.astype(BF)
        dout_ref[...] = dout.astype(BF)
        first = pl.program_id(1) == 0
        _acc(dgate_ref, dgate1, first)
        _acc(ddg_ref, ddn_g, first)

    return pl.pallas_call(
        body, name="mix_bwd", grid=(B, S // bt),
        in_specs=[_tok(bt, QW), _tok(bt, DNW), _tok(bt, DNW), _tok(bt, D), _tok(bt, D), _tok(bt, D), _per_batch(6 * D),
                  _full((1, DN_D)), _full((D, D)), _full((D, D)), _tok(bt, D)],
        out_specs=[_tok(bt, QW), _tok(bt, DNW), _tok(bt, DNW), _tok(bt, D), _tok(bt, D), _tok(bt, D), _tok(bt, D), _tok(bt, D),
                   _per_batch(D), _per_batch(DN_D)],
        out_shape=[_sds((B, S, QW), F32), _sds((B, S, DNW), F32), _sds((B, S, DNW), BF), _sds((B, S, D), BF),
                   _sds((B, S, D), BF), _sds((B, S, D), BF), _sds((B, S, D), BF), _sds((B, S, D), BF),
                   _sds((B, 1, D), F32), _sds((B, 1, DN_D), F32)],
        compiler_params=_cparams(dimension_semantics=("parallel", "arbitrary")),
    )(o_attn, o_raw, z, ga, gb, x, mod, dn_g, w_branch, w_out, dx1)


def _dn_rec_bwd(u, w, qd, kd, a, cd, states, d_o):
    B, S, _ = u.shape
    nc = S // CHUNK
    tok, cd_spec, st_spec = _rec_specs(B, lambda i: nc - 1 - i)

    def body(u_ref, w_ref, qd_ref, kd_ref, a_ref, cd_ref, st_ref, do_ref,
             du_ref, dw_ref, dqd_ref, dkd_ref, da_ref, dcd_ref, ds_ref):
        @pl.when(pl.program_id(0) == 0)
        def _():
            ds_ref[...] = jnp.zeros_like(ds_ref)
        state = st_ref[...].reshape(B * DN_H, DN_D, DN_D)
        _, vjp = jax.vjp(_dn_rec, state, *_rec_load(B, u_ref, w_ref, qd_ref, kd_ref, a_ref, cd_ref))
        dst, du, dw, dqd, dkd, da, dcd = vjp((ds_ref[...], _rec_stack(do_ref, B, DN_D)))
        ds_ref[...] = dst
        for ref, val, width in ((du_ref, du, DN_D), (dw_ref, dw, DN_D), (dqd_ref, dqd, DN_D), (dkd_ref, dkd, DN_D),
                                (da_ref, da, CHUNK)):
            _rec_store(B, ref, val, width)
        lane_row = _iota((1, DN_D), 1)
        for b in range(B):
            row = jnp.zeros((1, DN_D), F32)
            for h in range(DN_H):
                row = row + jnp.where(lane_row == h, dcd[b * DN_H + h], 0.0)
            dcd_ref[b, 0] = row

    return pl.pallas_call(
        body, name="dn_rec_bwd", grid=(nc,),
        in_specs=[tok(DNW)] * 4 + [tok(AW), cd_spec, st_spec, tok(DNW)],
        out_specs=[tok(DNW)] * 4 + [tok(AW), cd_spec],
        out_shape=[_sds((B, S, DNW), F32)] * 4 + [_sds((B, S, AW), F32), _sds((B, nc, 1, DN_D), F32)],
        scratch_shapes=[pltpu.VMEM((B * DN_H, DN_D, DN_D), F32)],
        compiler_params=_cparams(dimension_semantics=("arbitrary",)),
    )(u, w, qd, kd, a, cd, states, d_o)


def _dn_intra_bwd(qkv, bg, tinv, du, dw, dqd, dkd, da, dcd):
    B, S, _ = qkv.shape
    nc = S // CHUNK
    G = min(DN_GROUP, nc)
    rows = G * CHUNK

    def body(qkv_ref, bg_ref, t_ref, du_ref, dw_ref, dqd_ref, dkd_ref, da_ref, dcd_ref, dqkv_ref, dbg_ref):
        q, k, v = _dn_load_stack(qkv_ref, G)
        known = _stack_heads(t_ref, G, 0, CHUNK)
        _, vjp = jax.vjp(lambda q, k, v, bg: _dn_intra(q, k, v, bg, known)[:6], q, k, v,
                         bg_ref[...].reshape(G, CHUNK, DN_D))
        lane_row = _iota((1, DN_D), 1)
        dcd = jnp.stack([jnp.sum(jnp.where(lane_row == h, dcd_ref[g], 0.0), axis=-1, keepdims=True)
                         for g in range(G) for h in range(DN_H)])
        dq, dk, dv, dbg = vjp((_stack_heads(du_ref, G, 0, DN_D), _stack_heads(dw_ref, G, 0, DN_D),
                               _stack_heads(dqd_ref, G, 0, DN_D), _stack_heads(dkd_ref, G, 0, DN_D),
                               _stack_heads(da_ref, G, 0, CHUNK), dcd))
        for g in range(G):
            rows = slice(g * CHUNK, (g + 1) * CHUNK)
            for h in range(DN_H):
                n = g * DN_H + h
                dqkv_ref[rows, h * DN_D:(h + 1) * DN_D] = dq[n]
                dqkv_ref[rows, DNW + h * DN_D:DNW + (h + 1) * DN_D] = dk[n]
                dqkv_ref[rows, 2 * DNW + h * DN_D:2 * DNW + (h + 1) * DN_D] = dv[n]
        dbg_ref[...] = dbg.reshape(G * CHUNK, DN_D)

    return pl.pallas_call(
        body, name="dn_intra_bwd", grid=(B, nc // G),
        in_specs=[_tok(rows, CONVW), _tok(rows, 128), _tok(rows, AW)] + [_tok(rows, DNW)] * 4 + [_tok(rows, AW), _cd_spec(G)],
        out_specs=[_tok(rows, CONVW), _tok(rows, 128)],
        out_shape=[_sds((B, S, CONVW), F32), _sds((B, S, 128), F32)],
        compiler_params=_cparams(dimension_semantics=("parallel", "parallel")),
    )(qkv, bg, tinv, du, dw, dqd, dkd, da, dcd)


def _dn_prep_bwd(dn, ba, conv_w, alog, dtb, dqkv, dbg, bt):
    B, S, _ = dn.shape
    nt = S // bt

    def rev(f):
        return pl.BlockSpec((None, bt, f), lambda b, i: (b, nt - 1 - i, 0))

    halo = pl.BlockSpec((None, 8, CONVW), lambda b, i: (b, jnp.maximum((nt - 1 - i) * (bt // 8) - 1, 0), 0))

    def body(x_ref, halo_ref, ba_ref, cw_ref, al_ref, dt_ref, dqkv_ref, dbg_ref,
             dx_ref, dba_ref, dcw_ref, dal_ref, ddt_ref, xe_ref, dye_ref):
        i = pl.program_id(1)
        y = _conv_fwd_tile(xe_ref, x_ref, halo_ref, cw_ref, i == nt - 1, bt)
        ys = [y[:, j * DN_D:(j + 1) * DN_D] for j in range(3 * DN_H)]
        _, vjp = jax.vjp(_dn_tail, ys, ba_ref[...], al_ref[...], dt_ref[...])
        d_out = [dqkv_ref[:, j * DN_D:(j + 1) * DN_D] for j in range(3 * DN_H)]
        dys, dba, dal, ddt = vjp((d_out, dbg_ref[...]))
        @pl.when(i == 0)
        def _():
            dye_ref[bt:bt + 8, :] = jnp.zeros((8, CONVW), F32)

        @pl.when(i > 0)
        def _():
            dye_ref[bt:bt + 8, :] = dye_ref[0:8, :]
        for j in range(3 * DN_H):
            dye_ref[0:bt, j * DN_D:(j + 1) * DN_D] = dys[j]
        dx = cw_ref[0:1, :] * dye_ref[3:bt + 3, :]
        for j in range(1, CONV):
            dx = dx + cw_ref[j:j + 1, :] * dye_ref[3 - j:bt + 3 - j, :]
        dx_ref[...] = dx.astype(BF)
        dy = dye_ref[0:bt, :]
        dcw = jnp.concatenate([jnp.sum(dy * xe_ref[5 + j:bt + 5 + j, :], axis=0, keepdims=True) for j in range(CONV)], axis=0)
        first = (i == 0) & (pl.program_id(0) == 0)
        dba_ref[...] = dba
        _acc(dcw_ref, dcw, first)
        _acc(dal_ref, dal, first)
        _acc(ddt_ref, ddt, first)

    return pl.pallas_call(
        body, name="dn_prep_bwd", grid=(B, nt),
        in_specs=[rev(CONVW), halo, rev(128), _full((CONV, CONVW)), _full((1, 128)), _full((1, 128)), rev(CONVW), rev(128)],
        out_specs=[rev(CONVW), rev(128), _full((CONV, CONVW)), _full((1, 128)), _full((1, 128))],
        out_shape=[_sds((B, S, CONVW), BF), _sds((B, S, 128), F32), _sds((CONV, CONVW), F32), _sds((1, 128), F32),
                   _sds((1, 128), F32)],
        scratch_shapes=[pltpu.VMEM((bt + 8, CONVW), F32), pltpu.VMEM((bt + 8, CONVW), F32)],
        compiler_params=_cparams(dimension_semantics=("arbitrary", "arbitrary")),
    )(dn, dn, ba, conv_w, alog, dtb, dqkv, dbg)


def _attn_bwd(qkv, sinks, d_o):
    B, S, _ = qkv.shape
    nb = S // BLK

    def cur(f):
        return pl.BlockSpec((None, BLK, f), lambda b, i: (b, jnp.minimum(i, nb - 1), 0))

    def out_prev(f):
        return pl.BlockSpec((None, BLK, f), lambda b, i: (b, jnp.maximum(i - 1, 0), 0))

    def body(qkv_ref, kvp_ref, sk_ref, do_ref, dq_ref, dkv_ref, dsk_ref, carry_ref):
        n = pl.program_id(1)
        first = (n == 0) & (pl.program_id(0) == 0)

        @pl.when(n == 0)
        def _():
            carry_ref[...] = jnp.zeros_like(carry_ref)

        @pl.when(n < nb)
        def _():
            qs, kc, kp, vc, vp = _attn_load(qkv_ref, kvp_ref)

            def f(qs, kc, kp, vc, vp, sk):
                return _attn_block(qs, kc, kp, vc, vp, sk, n > 0)
            _, vjp = jax.vjp(f, qs, kc, kp, vc, vp, sk_ref[...])
            d_outs = [do_ref[:, h * HD:(h + 1) * HD] for h in range(HEADS)]
            dqs, dkc, dkp, dvc, dvp, dsk = vjp(d_outs)
            for h in range(HEADS):
                dq_ref[:, h * HD:(h + 1) * HD] = dqs[h].astype(BF)
            for h in range(KV_HEADS):
                ksl = slice(h * HD, (h + 1) * HD)
                vsl = slice(KVW + h * HD, KVW + (h + 1) * HD)
                dkv_ref[:, ksl] = (carry_ref[:, ksl] + dkp[h]).astype(BF)
                dkv_ref[:, vsl] = (carry_ref[:, vsl] + dvp[h]).astype(BF)
                carry_ref[:, ksl] = dkc[h]
                carry_ref[:, vsl] = dvc[h]
            _acc(dsk_ref, dsk, first)

        @pl.when(n == nb)
        def _():
            dkv_ref[...] = carry_ref[...].astype(BF)

    return pl.pallas_call(
        body, name="attn_bwd", grid=(B, nb + 1),
        in_specs=[cur(QKV), _kv_prev_spec(lambda i: jnp.maximum(jnp.minimum(i, nb - 1) - 1, 0)), _full((1, HEADS)), cur(QW)],
        out_specs=[cur(QW), out_prev(2 * KVW), _full((1, HEADS))],
        out_shape=[_sds((B, S, QW), BF), _sds((B, S, 2 * KVW), BF), _sds((1, HEADS), F32)],
        scratch_shapes=[pltpu.VMEM((BLK, 2 * KVW), F32)],
        compiler_params=_cparams(dimension_semantics=("arbitrary", "arbitrary")),
    )(qkv, qkv, sinks, d_o)


def _in_proj_bwd(x, mod, norm1_g, w_in, pieces, dba, dx1, bt):
    B, S, _ = x.shape
    widths = (QW, 2 * KVW, CONVW, DNW, D, D)

    def body(x_ref, mod_ref, g_ref, w_ref, dq_ref, dkv_ref, ddn_ref, dz_ref, dga_ref, dgb_ref, dba_ref, dx1_ref,
             gx_ref, dp_ref, dmod_ref, dg_ref):
        dp = jnp.concatenate([r[...] for r in (dq_ref, dkv_ref, ddn_ref, dz_ref, dga_ref, dgb_ref)]
                             + [dba_ref[...].astype(BF)], axis=-1)
        dp_ref[...] = dp
        dh = lax.dot_general(dp, w_ref[...], (((1,), (1,)), ((), ())), preferred_element_type=F32)
        _, vjp = jax.vjp(_norm_mod, x_ref[...], g_ref[...], mod_ref[:, 0:D], mod_ref[:, D:2 * D])
        dx, dgain, dshift, dscale = vjp(dh)
        gx_ref[...] = dx + dx1_ref[...]
        first = pl.program_id(1) == 0
        _acc(dmod_ref, jnp.concatenate([dshift, dscale], axis=-1), first)
        _acc(dg_ref, dgain, first)

    return pl.pallas_call(
        body, name="in_proj_bwd", grid=(B, S // bt),
        in_specs=[_tok(bt, D), _per_batch(6 * D), _full((1, D)), _full((D, IN_PAD))] + [_tok(bt, w) for w in widths]
        + [_tok(bt, 128), _tok(bt, D)],
        out_specs=[_tok(bt, D), _tok(bt, IN_PAD), _per_batch(2 * D), _per_batch(D)],
        out_shape=[_sds((B, S, D), F32), _sds((B, S, IN_PAD), BF), _sds((B, 1, 2 * D), F32), _sds((B, 1, D), F32)],
        compiler_params=_cparams(dimension_semantics=("parallel", "arbitrary")),
    )(x, mod, norm1_g, w_in, *pieces, dba, dx1)


def _matmul_tn(tag, a, b, bk, bn, bt, col_blocks=False):
    T, K = a.shape
    N = b.shape[1]
    nt = T // bt
    if col_blocks:
        assert bk == K
        out_spec = pl.BlockSpec((None, bk, bn), lambda i, j, t: (j, 0, 0))
        out_shape = _sds((N // bn, K, bn), F32)
    else:
        out_spec = pl.BlockSpec((bk, bn), lambda i, j, t: (i, j))
        out_shape = _sds((K, N), F32)

    def body(a_ref, b_ref, o_ref, acc_ref):
        t = pl.program_id(2)

        @pl.when(t == 0)
        def _():
            acc_ref[...] = jnp.zeros_like(acc_ref)
        acc_ref[...] += lax.dot_general(a_ref[...], b_ref[...], (((0,), (0,)), ((), ())), preferred_element_type=F32)

        @pl.when(t == nt - 1)
        def _():
            o_ref[...] = acc_ref[...]

    return pl.pallas_call(
        body, name=f"grad_{tag}", grid=(K // bk, N // bn, nt),
        in_specs=[pl.BlockSpec((bt, bk), lambda i, j, t: (t, i)), pl.BlockSpec((bt, bn), lambda i, j, t: (t, j))],
        out_specs=out_spec, out_shape=out_shape,
        scratch_shapes=[pltpu.VMEM((bk, bn), F32)],
        compiler_params=_cparams(dimension_semantics=("parallel", "parallel", "arbitrary")),
    )(a, b)


def _rope_table(positions):
    inv_freq = THETA ** (-jnp.arange(0, ROT, 2, dtype=F32) / ROT)
    ang = positions.astype(F32)[..., None] * inv_freq
    cos, sin = jnp.cos(ang), jnp.sin(ang)
    ones = jnp.ones(ang.shape[:-1] + (HD - ROT,), F32)
    cos_head = [cos, cos, ones]
    sin_head = [-sin, sin, 0.0 * ones]
    return jnp.concatenate(cos_head + cos_head, axis=-1), jnp.concatenate(sin_head + sin_head, axis=-1)


def _lane_pad(v, offset, width=128):
    return jnp.zeros((1, width), F32).at[0, offset:offset + v.shape[-1]].set(v.reshape(-1))


def _tile(S, want):
    return min(S, want)


def _local_step(x, mod, positions, tgt, norm1_g, w_in_pad, conv_w, q_norm_g, k_norm_g, sinks, a_log, dt_bias,
                dn_norm_g, w_branch, w_out, norm2_g, w_gu, w_dn):
    B, S, _ = x.shape
    T = B * S
    cos_t, sin_t = _rope_table(positions)
    qg2 = jnp.concatenate([q_norm_g, q_norm_g], axis=-1)
    kg2 = jnp.concatenate([k_norm_g, k_norm_g], axis=-1)
    alog = _lane_pad(a_log, DN_H)
    dtb = _lane_pad(dt_bias, DN_H)
    conv2 = conv_w.reshape(CONV, CONVW)
    bt = _tile(S, 512)
    bt_mlp = _tile(S, 256)

    q, kv, dn, z, ga, gb, ba, h1 = _in_proj(x, mod, norm1_g, w_in_pad, bt)
    qkv_n = _qk_prep_fwd(q, kv, cos_t, sin_t, qg2, kg2, bt)
    o_attn = _attn_fwd(qkv_n, sinks)
    dqkv, bg = _dn_prep(dn, ba, conv2, alog, dtb, bt)
    dn_u, dn_w, dn_qd, dn_kd, dn_a, dn_cd, dn_tinv = _dn_intra_fwd(dqkv, bg)
    o_raw, states = _dn_rec_fwd(dn_u, dn_w, dn_qd, dn_kd, dn_a, dn_cd)
    x1, o_dn, merged = _mix_fwd(o_attn, o_raw, z, ga, gb, x, mod, dn_norm_g, w_branch, w_out, bt)
    dx1, h2, act, dgu, dyy, loss, dmod2, dnorm2 = _mlp(x1, tgt, mod, norm2_g, w_gu, w_dn, bt_mlp)

    def flat(t):
        return t.reshape(T, t.shape[-1])
    tn = functools.partial(_matmul_tn, bt=_tile(T, 512))
    g_w_dn = tn("w_down", flat(act), flat(dyy), bk=FFN, bn=D // 2)
    g_w_gu = tn("w_gate_up", flat(h2), flat(dgu), bk=D, bn=2 * FFN // N_CHIP, col_blocks=True)

    d_oa, d_or, dz, dga, dgb, dya, dyd, dout, dgate1, ddn_g = _mix_bwd(
        o_attn, o_raw, z, ga, gb, x, mod, dn_norm_g, w_branch, w_out, dx1, bt_mlp)
    g_w_out = tn("w_out", flat(merged), flat(dout), bk=D, bn=D)
    g_w_br = jnp.concatenate([tn("w_branch_attn", flat(o_attn), flat(dya), bk=QW, bn=D),
                              tn("w_branch_dn", flat(o_dn), flat(dyd), bk=DNW, bn=D)], axis=0)

    d_rec = _dn_rec_bwd(dn_u, dn_w, dn_qd, dn_kd, dn_a, dn_cd, states, d_or)
    d_dqkv, dbg = _dn_intra_bwd(dqkv, bg, dn_tinv, *d_rec)
    d_dn, dba, dconv, dalog, ddtb = _dn_prep_bwd(dn, ba, conv2, alog, dtb, d_dqkv, dbg, bt)
    dqn, dkvn, dsk = _attn_bwd(qkv_n, sinks, d_oa)
    dq, dkv, dqg2, dkg2 = _qk_prep_bwd(q, kv, cos_t, sin_t, qg2, kg2, dqn, dkvn, bt)
    dqg = dqg2[:, :HD] + dqg2[:, HD:]
    dkg = dkg2[:, :HD] + dkg2[:, HD:]
    grad_x, dproj, dmod1, dnorm1 = _in_proj_bwd(x, mod, norm1_g, w_in_pad, (dq, dkv, d_dn, dz, dga, dgb), dba, dx1, bt)
    g_w_in = tn("w_in", flat(h1), flat(dproj), bk=D, bn=IN_PAD // 3)

    dmod = jnp.concatenate([dmod1, dgate1, dmod2], axis=-1)
    small = dict(norm1_g=jnp.sum(dnorm1, axis=0), norm2_g=jnp.sum(dnorm2, axis=0), q_norm_g=dqg, k_norm_g=dkg,
                 sinks=dsk, a_log=dalog[:, DN_H:2 * DN_H], dt_bias=ddtb[:, DN_H:2 * DN_H],
                 dn_norm_g=jnp.sum(ddn_g, axis=0), conv_w=dconv)
    return jnp.sum(loss), grad_x, dmod, small, (g_w_in, g_w_br, g_w_out, g_w_gu, g_w_dn)


def _me():
    return lax.axis_index("x"), lax.axis_index("y"), lax.axis_index("c")


def _flip(me, f):
    return (me[0] ^ ((f >> 2) & 1), me[1] ^ ((f >> 1) & 1), me[2] ^ (f & 1))


def _comm_call(name, ins, out_shapes, n_remote, plan):
    n_in = len(ins)
    n_out = len(out_shapes)

    def body(*refs):
        in_refs, out_refs = refs[:n_in], refs[n_in:n_in + n_out]
        send_sems, recv_sems = refs[n_in + n_out:]
        remote = plan(_me(), in_refs, out_refs)
        assert len(remote) == n_remote
        cps = [pltpu.make_async_remote_copy(src_ref=src, dst_ref=dst, send_sem=send_sems.at[i], recv_sem=recv_sems.at[i],
                                            device_id=peer, device_id_type=pl.DeviceIdType.MESH)
               for i, (src, dst, peer) in enumerate(remote)]
        for cp in cps:
            cp.start()
        for cp in cps:
            cp.wait_recv()
        for cp in cps:
            cp.wait_send()

    any_spec = pl.BlockSpec(memory_space=pl.ANY)
    return pl.pallas_call(
        body, name=name, in_specs=[any_spec] * n_in, out_specs=[any_spec] * n_out, out_shape=out_shapes,
        scratch_shapes=[pltpu.SemaphoreType.DMA((n_remote,)), pltpu.SemaphoreType.DMA((n_remote,))],
    )(*ins)


def _by_origin(own, received, index):
    stack = jnp.concatenate([own[None], received], axis=0)
    n = stack.shape[0]
    return jnp.stack([lax.dynamic_index_in_dim(stack, k ^ index, 0, keepdims=False) for k in range(n)])


def _gather_devices(name, arrs, dev):
    def plan(me, in_refs, out_refs):
        return [(a, o.at[f - 1], _flip(me, f)) for a, o in zip(in_refs, out_refs) for f in range(1, N_DEV)]
    outs = [_sds((N_DEV - 1,) + a.shape, a.dtype) for a in arrs]
    got = _comm_call(name, arrs, outs, (N_DEV - 1) * len(arrs), plan)
    return [_by_origin(a, g, dev) for a, g in zip(arrs, got)]


def _gather_chips(name, arrs, chip):
    def plan(me, in_refs, out_refs):
        return [(a, o.at[j], _flip(me, 2 * (j + 1))) for a, o in zip(in_refs, out_refs) for j in range(N_CHIP - 1)]
    outs = [_sds((N_CHIP - 1,) + a.shape, a.dtype) for a in arrs]
    got = _comm_call(name, arrs, outs, (N_CHIP - 1) * len(arrs), plan)
    return [_by_origin(a, g, chip) for a, g in zip(arrs, got)]


def _halves(core, mine, other):
    lo = jnp.where(core == 0, mine, other)
    hi = jnp.where(core == 0, other, mine)
    return jnp.concatenate([lo, hi], axis=-2)


def _gather_weights(shards, chip, core):
    def plan_ici(me, in_refs, out_refs):
        remote = []
        for a, o in zip(in_refs, out_refs):
            half = a.shape[0] // 2
            mine = a.at[pl.ds(me[2] * half, half)]
            remote += [(mine, o.at[j], _flip(me, 2 * (j + 1))) for j in range(N_CHIP - 1)]
        return remote
    mine = _comm_call("weights_ici", shards, [_sds((N_CHIP - 1, a.shape[0] // 2, a.shape[1]), a.dtype) for a in shards],
                      (N_CHIP - 1) * len(shards), plan_ici)

    def plan_d2d(me, in_refs, out_refs):
        return [(g, o, _flip(me, 1)) for g, o in zip(in_refs, out_refs)]
    other = _comm_call("weights_d2d", mine, [_sds(g.shape, g.dtype) for g in mine], len(mine), plan_d2d)
    return [_by_origin(a, _halves(core, g, h), chip) for a, g, h in zip(shards, mine, other)]


def _rows(r):
    for br in (512, 352, 256, 128, 64, 32, 16, 8):
        if r % br == 0:
            return br
    raise ValueError(r)


def _pair_add(tag, g, recv, c):
    n, r, cols = g.shape
    half = r // 2
    br = _rows(half)
    nb = half // br

    def body(c_ref, g_ref, r_ref, o_ref):
        o_ref[...] = (g_ref[...] + r_ref[...]).astype(BF)

    return pl.pallas_call(
        body, name=f"pair_add_{tag}",
        grid_spec=pltpu.PrefetchScalarGridSpec(
            num_scalar_prefetch=1, grid=(n, nb),
            in_specs=[pl.BlockSpec((None, br, cols), lambda k, i, c_ref: (k, c_ref[0] * nb + i, 0)),
                      pl.BlockSpec((None, br, cols), lambda k, i, c_ref: (k, i, 0))],
            out_specs=pl.BlockSpec((None, br, cols), lambda k, i, c_ref: (k, i, 0))),
        out_shape=_sds((n, half, cols), BF),
        compiler_params=_cparams(dimension_semantics=("parallel", "parallel")),
    )(c, g, recv)


def _sum_chips(tag, p, q, chip):
    n, r, cols = q.shape
    br = _rows(r)

    def body(chip_ref, p_ref, q_ref, o_ref):
        acc = p_ref[...].astype(F32)
        for k in range(n):
            acc = acc + q_ref[k].astype(F32)
        o_ref[...] = acc

    return pl.pallas_call(
        body, name=f"sum_chips_{tag}",
        grid_spec=pltpu.PrefetchScalarGridSpec(
            num_scalar_prefetch=1, grid=(r // br,),
            in_specs=[pl.BlockSpec((None, br, cols), lambda i, chip_ref: (chip_ref[0], i, 0)),
                      pl.BlockSpec((n, br, cols), lambda i, chip_ref: (0, i, 0))],
            out_specs=pl.BlockSpec((br, cols), lambda i, chip_ref: (i, 0))),
        out_shape=_sds((r, cols), F32),
        compiler_params=_cparams(dimension_semantics=("parallel",)),
    )(chip, p, q)


def _reduce_grads(tags, grads, chip, core):
    core_arr = core.reshape(1).astype(jnp.int32)
    chip_arr = chip.reshape(1).astype(jnp.int32)

    def plan_pair(me, in_refs, out_refs):
        remote = []
        for g, o in zip(in_refs, out_refs):
            half = g.shape[1] // 2
            remote += [(g.at[k, pl.ds((1 - me[2]) * half, half)], o.at[k], _flip(me, 1)) for k in range(N_CHIP)]
        return remote
    recv = _comm_call("grads_pair", grads, [_sds((N_CHIP, g.shape[1] // 2, g.shape[2]), F32) for g in grads],
                      N_CHIP * len(grads), plan_pair)
    pair = [_pair_add(t, g, r, core_arr) for t, g, r in zip(tags, grads, recv)]

    def plan_chips(me, in_refs, out_refs):
        remote = []
        for p, o in zip(in_refs, out_refs):
            for j in range(N_CHIP - 1):
                peer = _flip(me, 2 * (j + 1))
                remote.append((p.at[2 * peer[0] + peer[1]], o.at[j], peer))
        return remote
    parts = _comm_call("grads_chips", pair, [_sds((N_CHIP - 1,) + p.shape[1:], BF) for p in pair],
                       (N_CHIP - 1) * len(pair), plan_chips)
    mine = [_sum_chips(t, p, q, chip_arr) for t, p, q in zip(tags, pair, parts)]

    def plan_swap(me, in_refs, out_refs):
        return [(h, o, _flip(me, 1)) for h, o in zip(in_refs, out_refs)]
    other = _comm_call("grads_swap", mine, [_sds(h.shape, F32) for h in mine], len(mine), plan_swap)
    return [_halves(core, h, o) for h, o in zip(mine, other)]


def _adamw_math(w, g, m, v):
    m = ADAM_B1 * m + (1.0 - ADAM_B1) * g
    v = ADAM_B2 * v + (1.0 - ADAM_B2) * (g * g)
    m_hat = m / (1.0 - ADAM_B1 ** ADAM_STEP)
    v_hat = v / (1.0 - ADAM_B2 ** ADAM_STEP)
    delta = -ADAM_LR * (m_hat / (jnp.sqrt(v_hat) + ADAM_EPS) + ADAM_WD * w)
    return delta, m, v


def _adamw(name, w, g, m, v):
    r, cols = w.shape
    br = _rows(r)
    if br * cols * 4 > (1 << 20) and br % 16 == 0:
        br //= 2

    def body(w_ref, g_ref, m_ref, v_ref, d_ref, mo_ref, vo_ref):
        d_ref[...], mo_ref[...], vo_ref[...] = _adamw_math(w_ref[...], g_ref[...], m_ref[...], v_ref[...])

    spec = pl.BlockSpec((br, cols), lambda i: (i, 0))
    return pl.pallas_call(
        body, name=f"adamw_{name}", grid=(r // br,), in_specs=[spec] * 4, out_specs=[spec] * 3,
        out_shape=[_sds((r, cols), F32)] * 3,
        compiler_params=_cparams(dimension_semantics=("parallel",)),
    )(w, g, m, v)


def _ada_fwd(c_all, ada_w, ada_b_cols):
    n = c_all.shape[0]

    def body(c_ref, w_ref, b_ref, o_ref):
        o_ref[...] = _mmx(_silu(c_ref[...]), w_ref[...]) + b_ref[...]

    return pl.pallas_call(
        body, name="ada_fwd", out_shape=_sds((n, ada_w.shape[1]), F32), compiler_params=_cparams(),
    )(c_all, ada_w, ada_b_cols)


def _ada_bwd(c_all, dmod_cols, w, m, v):
    n = c_all.shape[0]
    r, cols = w.shape
    br = 128

    def body(c_ref, d_ref, w_ref, m_ref, v_ref, g_ref, dl_ref, mo_ref, vo_ref):
        cond = _silu(c_ref[...])
        g = lax.dot_general(cond, d_ref[...], (((0,), (0,)), ((), ())), precision=lax.Precision.HIGHEST,
                            preferred_element_type=F32)
        g_ref[...] = g
        dl_ref[...], mo_ref[...], vo_ref[...] = _adamw_math(w_ref[...], g, m_ref[...], v_ref[...])

    spec = pl.BlockSpec((br, cols), lambda i: (i, 0))
    return pl.pallas_call(
        body, name="ada_bwd", grid=(r // br,),
        in_specs=[pl.BlockSpec((n, br), lambda i: (0, i)), pl.BlockSpec((n, cols), lambda i: (0, 0)), spec, spec, spec],
        out_specs=[spec] * 4, out_shape=[_sds((r, cols), F32)] * 4,
        compiler_params=_cparams(dimension_semantics=("parallel",)),
    )(c_all, dmod_cols, w, m, v)


def _sum_devices(parts):
    n, r, cols = parts.shape

    def body(p_ref, o_ref):
        acc = p_ref[0]
        for k in range(1, n):
            acc = acc + p_ref[k]
        o_ref[...] = acc

    return pl.pallas_call(body, name="sum_devices", out_shape=_sds((r, cols), F32), compiler_params=_cparams())(parts)


SMALL_ROWS = 16
_SMALL_SLOTS = dict(norm1_g=(0, 0, D), norm2_g=(1, 0, D), q_norm_g=(2, 0, HD), k_norm_g=(2, 128, HD), sinks=(2, 256, HEADS),
                    a_log=(2, 384, DN_H), dt_bias=(2, 512, DN_H), dn_norm_g=(2, 640, DN_D))
_CONV_ROW = 4
_ADA_B_ROW = 8


def _pack_small(vals, conv, ada_b):
    sheet = jnp.zeros((SMALL_ROWS, CONVW), F32)
    for name, (row, col, n) in _SMALL_SLOTS.items():
        sheet = sheet.at[row, col:col + n].set(vals[name].reshape(n))
    sheet = sheet.at[_CONV_ROW:_CONV_ROW + CONV, 0:conv.shape[1]].set(conv)
    return sheet.at[_ADA_B_ROW:_ADA_B_ROW + 4, :].set(ada_b.reshape(4, CONVW))


def _unpack_small(sheet, conv_cols):
    out = {name: sheet[row, col:col + n].reshape(1, n) for name, (row, col, n) in _SMALL_SLOTS.items()}
    out["conv_w"] = sheet[_CONV_ROW:_CONV_ROW + CONV, 0:conv_cols].reshape(1, CONV, 1, conv_cols)
    out["ada_b"] = sheet[_ADA_B_ROW:_ADA_B_ROW + 4, :].reshape(1, 6 * D)
    return out


def _w_in_segments():
    shard = IN_WIDTH // N_CHIP
    cuts = sorted({0, IN_WIDTH, C_Z, C_Z + 2 * DN_H} | {k * shard for k in range(1, N_CHIP)})
    segs = []
    for a, b in zip(cuts[:-1], cuts[1:]):
        k = a // shard
        pad = a if a < C_Z else (C_BA + a - C_Z if a < C_Z + 2 * DN_H else a - 2 * DN_H)
        segs.append((k, a - k * shard, b - k * shard, pad))
    return segs


def _pad_w_in(f):
    parts = [f[k][:, lo:hi] for k, lo, hi, _ in sorted(_w_in_segments(), key=lambda s: s[3])]
    return jnp.concatenate(parts + [jnp.zeros((f.shape[1], IN_PAD - IN_WIDTH), f.dtype)], axis=1)


def _unpad_w_in(g):
    return jnp.stack([jnp.concatenate([g[:, pad:pad + hi - lo] for kk, lo, hi, pad in _w_in_segments() if kk == k], axis=1)
                      for k in range(N_CHIP)])


def _blocks_to_cols(f):
    return f.transpose(1, 0, 2).reshape(f.shape[1], N_CHIP * f.shape[2])


def kernel(x, c, positions, ada_w, ada_b, norm1_g, w_in, conv_w, q_norm_g, k_norm_g, sinks, a_log, dt_bias, dn_norm_g, w_branch, w_out, norm2_g, w_gate_up, w_down, loss_target, m_ada_w, m_ada_b, m_norm1_g, m_w_in, m_conv_w, m_q_norm_g, m_k_norm_g, m_sinks, m_a_log, m_dt_bias, m_dn_norm_g, m_w_branch, m_w_out, m_norm2_g, m_w_gate_up, m_w_down, v_ada_w, v_ada_b, v_norm1_g, v_w_in, v_conv_w, v_q_norm_g, v_k_norm_g, v_sinks, v_a_log, v_dt_bias, v_dn_norm_g, v_w_branch, v_w_out, v_norm2_g, v_w_gate_up, v_w_down):
    ix, iy, ic = lax.axis_index("x"), lax.axis_index("y"), lax.axis_index("c")
    dev = 4 * ix + 2 * iy + ic
    chip = 2 * ix + iy
    n_seq = x.shape[0]
    conv_cols = conv_w.shape[-1]

    c_all, conv_all = _gather_devices("gather_cond", [c, conv_w.reshape(CONV, conv_cols)], dev)
    c_all = c_all.reshape(N_DEV * n_seq, D)
    ada_cols = ada_w.shape[-1]
    ada_b_cols = lax.dynamic_slice(ada_b, (0, chip * ada_cols), (1, ada_cols))
    mod_cols = _ada_fwd(c_all, ada_w[0], ada_b_cols)
    (mod_blocks,) = _gather_chips("gather_mod", [mod_cols], chip)
    mod_all = _blocks_to_cols(mod_blocks)
    mod = lax.dynamic_slice(mod_all, (dev * n_seq, 0), (n_seq, 6 * D)).reshape(n_seq, 1, 6 * D)
    conv_full = _blocks_to_cols(conv_all[0::2])

    shards = [w_in[0].astype(BF), w_branch[0].astype(BF), w_out[0].astype(BF), w_gate_up[0].astype(BF), w_down[0].astype(BF)]
    f_in, f_br, f_out, f_gu, f_dn = _gather_weights(shards, chip, ic)
    w_in_pad = _pad_w_in(f_in)
    w_br_full, w_out_full, w_dn_full = (f.reshape(N_CHIP * f.shape[1], f.shape[2]) for f in (f_br, f_out, f_dn))

    loss, grad_x, dmod, small, (g_in, g_br, g_out, g_gu, g_dn) = _local_step(
        x, mod, positions, loss_target, norm1_g, w_in_pad, conv_full.reshape(CONV, 1, CONVW), q_norm_g, k_norm_g, sinks,
        a_log, dt_bias, dn_norm_g, w_br_full, w_out_full, norm2_g, f_gu, w_dn_full)
    loss = lax.psum(loss, ("x", "y", "c"))

    blocks = [_unpad_w_in(g_in), g_br.reshape(N_CHIP, -1, D), g_out.reshape(N_CHIP, -1, D), g_gu, g_dn.reshape(N_CHIP, -1, D)]
    r_in, r_br, r_out, r_gu, r_dn = _reduce_grads(("w_in", "w_branch", "w_out", "w_gate_up", "w_down"), blocks, chip, ic)
    big = {}
    for name, w, g, m, v in (("w_in", w_in, r_in, m_w_in, v_w_in), ("w_branch", w_branch, r_br, m_w_branch, v_w_branch),
                             ("w_out", w_out, r_out, m_w_out, v_w_out), ("w_gate_up", w_gate_up, r_gu, m_w_gate_up, v_w_gate_up),
                             ("w_down", w_down, r_dn, m_w_down, v_w_down)):
        big[name] = (g,) + tuple(_adamw(name, w[0], g, m[0], v[0]))

    part = _pack_small(small, small["conv_w"], jnp.sum(dmod, axis=(0, 1)).reshape(1, 6 * D))
    dmod_all, parts = _gather_devices("gather_small", [dmod.reshape(n_seq, 6 * D), part], dev)
    dmod_all = dmod_all.reshape(N_DEV * n_seq, 6 * D)
    g_small = _unpack_small(_sum_devices(parts), CONVW)
    g_conv = lax.dynamic_slice(g_small["conv_w"].reshape(CONV, CONVW), (0, chip * conv_cols), (CONV, conv_cols))
    g_small["conv_w"] = g_conv.reshape(1, CONV, 1, conv_cols)

    given = dict(norm1_g=(norm1_g, m_norm1_g, v_norm1_g), norm2_g=(norm2_g, m_norm2_g, v_norm2_g),
                 q_norm_g=(q_norm_g, m_q_norm_g, v_q_norm_g), k_norm_g=(k_norm_g, m_k_norm_g, v_k_norm_g),
                 sinks=(sinks, m_sinks, v_sinks), a_log=(a_log, m_a_log, v_a_log), dt_bias=(dt_bias, m_dt_bias, v_dt_bias),
                 dn_norm_g=(dn_norm_g, m_dn_norm_g, v_dn_norm_g))
    sheets = [_pack_small({k: t[j] for k, t in given.items()}, cw.reshape(CONV, conv_cols), ab)
              for j, (cw, ab) in enumerate(((conv_w, ada_b), (m_conv_w, m_ada_b), (v_conv_w, v_ada_b)))]
    g_local = _pack_small(g_small, g_conv, g_small["ada_b"])
    upd = [_unpack_small(s, conv_cols) for s in _adamw("small", sheets[0], g_local, sheets[1], sheets[2])]

    dmod_cols = lax.dynamic_slice(dmod_all, (0, chip * ada_cols), (N_DEV * n_seq, ada_cols))
    ada = _ada_bwd(c_all, dmod_cols, ada_w[0], m_ada_w[0], v_ada_w[0])

    names = ["ada_w", "ada_b", "norm1_g", "w_in", "conv_w", "q_norm_g", "k_norm_g", "sinks", "a_log", "dt_bias", "dn_norm_g",
             "w_branch", "w_out", "norm2_g", "w_gate_up", "w_down"]

    def leaf(name, j):
        if name == "ada_w":
            return ada[j][None]
        if name in big:
            return big[name][j][None]
        return g_small[name] if j == 0 else upd[j - 1][name]

    return (loss, grad_x) + tuple(leaf(n, j) for j in range(4) for n in names)
```

```python
import functools
from typing import Callable, NamedTuple

import jax
import jax.numpy as jnp
import numpy as np
from jax import lax
from jax.experimental import pallas as pl
from jax.experimental.pallas import tpu as pltpu

F32 = jnp.float32
BF = jnp.bfloat16

D = 1024
HEADS = 8
KV_HEADS = 2
GROUP = 4
HD = 64
BLK = 128
ROT = 16
THETA = 500000.0
QW = 512
KVW = 128
DN_H = 4
DN_D = 128
CONV = 4
CHUNK = 64
DNW = 512
CONVW = 1536
FFN = 2816
EPS = 1e-6
IN_WIDTH = 4872
IN_PAD = 4992
C_KV = 512
C_DN = 768
C_Z = 2304
C_GA = 2816
C_GB = 3840
C_BA = 4864
NEG = -1e30
N_DEV = 8
N_CHIP = 4

ADAM_LR = 0.001
ADAM_B1 = 0.9
ADAM_B2 = 0.999
ADAM_EPS = 1e-08
ADAM_WD = 0.01
ADAM_STEP = 10

VMEM_LIMIT = 60 * 1024 * 1024


def _cparams(**kw):
    return pltpu.CompilerParams(vmem_limit_bytes=VMEM_LIMIT, **kw)


def _dg(a, b, ca, cb):
    return lax.dot_general(a.astype(BF), b.astype(BF), (((ca,), (cb,)), ((), ())),
                           preferred_element_type=F32)


@jax.custom_vjp
def _mm(a, b):
    return _dg(a, b, 1, 0)


def _mm_fwd(a, b):
    return _dg(a, b, 1, 0), (a, b)


def _mm_bwd(res, dy):
    a, b = res
    return _dg(dy, b, 1, 1).astype(a.dtype), _dg(a, dy, 0, 0).astype(b.dtype)


_mm.defvjp(_mm_fwd, _mm_bwd)


@jax.custom_vjp
def _mm_nt(a, b):
    return _dg(a, b, 1, 1)


def _mm_nt_fwd(a, b):
    return _dg(a, b, 1, 1), (a, b)


def _mm_nt_bwd(res, dy):
    a, b = res
    return _dg(dy, b, 1, 0).astype(a.dtype), _dg(dy, a, 0, 0).astype(b.dtype)


_mm_nt.defvjp(_mm_nt_fwd, _mm_nt_bwd)


@jax.custom_vjp
def _mm_tn(a, b):
    return _dg(a, b, 0, 0)


def _mm_tn_fwd(a, b):
    return _dg(a, b, 0, 0), (a, b)


def _mm_tn_bwd(res, dy):
    a, b = res
    return _dg(b, dy, 1, 1).astype(a.dtype), _dg(a, dy, 1, 0).astype(b.dtype)


_mm_tn.defvjp(_mm_tn_fwd, _mm_tn_bwd)


def _mmx(a, b):
    return jnp.dot(a, b, precision=lax.Precision.HIGHEST, preferred_element_type=F32)


def _mmx_nt(a, b):
    return lax.dot_general(a, b, (((1,), (1,)), ((), ())), precision=lax.Precision.HIGHEST,
                           preferred_element_type=F32)


def _iota(shape, dim):
    return lax.broadcasted_iota(jnp.int32, shape, dim)


def _sigmoid(x):
    return 1.0 / (1.0 + jnp.exp(-x))


def _silu(x):
    return x * _sigmoid(x)


def _softplus(x):
    return jnp.maximum(x, 0.0) + jnp.log(1.0 + jnp.exp(-jnp.abs(x)))


def _rms(x, gain):
    return x * lax.rsqrt(jnp.mean(x * x, axis=-1, keepdims=True) + EPS) * gain


def _norm_mod(x, gain, shift, scale):
    return _rms(x, gain) * (1.0 + scale) + shift


def _split(a):
    hi = a.astype(BF)
    return hi, (a - hi.astype(F32)).astype(BF)


def _dg3(a, b, ca, cb):
    ah, al = _split(a)
    bh, bl = _split(b)

    def dg(x, y):
        return lax.dot_general(x, y, (((ca,), (cb,)), ((), ())), preferred_element_type=F32)
    return dg(ah, bh) + (dg(ah, bl) + dg(al, bh))


@jax.custom_vjp
def _mm3(a, b):
    return _dg3(a, b, 1, 0)


def _mm3_fwd(a, b):
    return _dg3(a, b, 1, 0), (a, b)


def _mm3_bwd(res, dy):
    a, b = res
    return _dg3(dy, b, 1, 1), _dg3(a, dy, 0, 0)


_mm3.defvjp(_mm3_fwd, _mm3_bwd)


def _qk_prep(slabs, gain, cos, sin):
    r = _iota((2 * HD, 2 * HD), 0)
    c = _iota((2 * HD, 2 * HD), 1)
    seg = jnp.where(r // HD == c // HD, 1.0 / HD, 0.0).astype(F32)
    half = ROT // 2
    cd = c % HD
    pair = jnp.where(((cd < half) & (r == c + half)) | ((cd >= half) & (cd < ROT) & (r == c - half)), 1.0, 0.0).astype(F32)
    out = []
    for x in slabs:
        y = x * lax.rsqrt(_mm3(x * x, seg) + EPS) * gain
        out.append(y * cos + _mm3(y, pair) * sin)
    return out


def _attn_block(qs, kc, kp, vc, vp, sinks, has_prev):
    rows = GROUP * BLK
    qi = _iota((rows, 2 * BLK), 0) % BLK + BLK
    kj = _iota((rows, 2 * BLK), 1)
    dist = qi - kj
    valid = (dist >= 0) & (dist < BLK) & ((kj >= BLK) | has_prev)
    grp = _iota((rows, HEADS), 0) // BLK
    col = _iota((rows, HEADS), 1)
    outs = []
    for h in range(KV_HEADS):
        q = jnp.concatenate([qs[h * GROUP + g] for g in range(GROUP)], axis=0)
        k = jnp.concatenate([kp[h], kc[h]], axis=0)
        v = jnp.concatenate([vp[h], vc[h]], axis=0)
        s = _mm_nt(q, k) * (HD ** -0.5)
        s = jnp.where(valid, s, NEG)
        sink = jnp.sum(jnp.where(col == h * GROUP + grp, sinks, 0.0), axis=-1, keepdims=True)
        m = lax.stop_gradient(jnp.maximum(jnp.max(s, axis=-1, keepdims=True), sink))
        p = jnp.exp(s - m)
        probs = p / (jnp.sum(p, axis=-1, keepdims=True) + jnp.exp(sink - m))
        o = _mm(probs, v)
        outs += [o[g * BLK:(g + 1) * BLK] for g in range(GROUP)]
    return outs


def _dn_tail(ys, ba, alog, dtb):
    def l2(t):
        return t * lax.rsqrt(jnp.sum(t * t, axis=-1, keepdims=True) + EPS)
    s = [_silu(y) for y in ys]
    out = [l2(t) for t in s[:2 * DN_H]] + s[2 * DN_H:]
    lane = _iota(ba.shape, 1)
    beta = _sigmoid(ba)
    g = -jnp.exp(alog) * _softplus(ba + dtb)
    bg = jnp.where(lane < DN_H, beta, jnp.where(lane < 2 * DN_H, g, 0.0))
    return out, bg


def _bdg(a, b, ca, cb):
    return lax.dot_general(a.astype(BF), b.astype(BF), (((ca,), (cb,)), ((0,), (0,))), preferred_element_type=F32)


@jax.custom_vjp
def _bmm(a, b):
    return _bdg(a, b, 2, 1)


def _bmm_fwd(a, b):
    return _bdg(a, b, 2, 1), (a, b)


def _bmm_bwd(res, dy):
    a, b = res
    return _bdg(dy, b, 2, 2), _bdg(a, dy, 1, 1)


_bmm.defvjp(_bmm_fwd, _bmm_bwd)


@jax.custom_vjp
def _bmm_nt(a, b):
    return _bdg(a, b, 2, 2)


def _bmm_nt_fwd(a, b):
    return _bdg(a, b, 2, 2), (a, b)


def _bmm_nt_bwd(res, dy):
    a, b = res
    return _bdg(dy, b, 2, 1), _bdg(dy, a, 1, 1)


_bmm_nt.defvjp(_bmm_nt_fwd, _bmm_nt_bwd)


def _bmmx(a, b):
    return lax.dot_general(a, b, (((2,), (1,)), ((0,), (0,))), precision=lax.Precision.HIGHEST,
                           preferred_element_type=F32)


def _neumann_inverse(lmat):
    C = CHUNK
    eye = jnp.where(_iota((C, C), 0) == _iota((C, C), 1), 1.0, 0.0).astype(F32)[None]
    a = -lmat
    tinv = eye + a
    pw = _bmmx(a, a)
    for _ in range(4):
        both = _bmmx(jnp.concatenate([pw, tinv], axis=1), pw)
        pw, tinv = both[:, :C], tinv + both[:, C:]
    return tinv + _bmmx(tinv, pw)


def _inverse_bwd(tinv, d_tinv):
    x = lax.dot_general(d_tinv, tinv, (((2,), (2,)), ((0,), (0,))), precision=lax.Precision.HIGHEST,
                        preferred_element_type=F32)
    return -lax.dot_general(tinv, x, (((1,), (1,)), ((0,), (0,))), precision=lax.Precision.HIGHEST,
                            preferred_element_type=F32)


@jax.custom_vjp
def _tri_inverse(lmat):
    return _neumann_inverse(lmat)


def _tri_inverse_fwd(lmat):
    tinv = _neumann_inverse(lmat)
    return tinv, tinv


def _tri_inverse_bwd(tinv, d_tinv):
    return (_inverse_bwd(tinv, d_tinv),)


_tri_inverse.defvjp(_tri_inverse_fwd, _tri_inverse_bwd)


@jax.custom_vjp
def _tri_inverse_known(lmat, tinv):
    return tinv


def _tri_inverse_known_fwd(lmat, tinv):
    return tinv, tinv


def _tri_inverse_known_bwd(tinv, d_tinv):
    return _inverse_bwd(tinv, d_tinv), jnp.zeros_like(tinv)


_tri_inverse_known.defvjp(_tri_inverse_known_fwd, _tri_inverse_known_bwd)


def _dn_intra(q, k, v, bg, tinv=None):
    C = CHUNK
    G = bg.shape[0]
    r = _iota((C, C), 0)
    c = _iota((C, C), 1)
    incl = (r >= c)[None]
    strict = (r > c)[None]
    eye = jnp.where(r == c, 1.0, 0.0).astype(F32)[None]
    tri = jnp.broadcast_to(jnp.where(r >= c, 1.0, 0.0).astype(F32)[None], (G, C, C))
    gc_all = _bmmx(tri, bg)
    lane = _iota((C, DN_D), 1)

    def per_head(x, offset):
        return jnp.concatenate([jnp.sum(jnp.where(lane == offset + h, x[g], 0.0), axis=-1, keepdims=True)[None]
                                for g in range(G) for h in range(DN_H)], axis=0)
    beta = per_head(bg, 0)
    gcol = per_head(gc_all, DN_H)
    grow = jnp.sum(eye * gcol, axis=1, keepdims=True)
    glast = jnp.sum(jnp.where(_iota((1, C, 1), 1) == C - 1, gcol, 0.0), axis=1, keepdims=True)
    decay = jnp.exp(jnp.where(incl, gcol - grow, NEG))
    q = q * (DN_D ** -0.5)
    kb = k * beta
    lmat = jnp.where(strict, _bmm_nt(kb, k) * decay, 0.0)
    tinv = _tri_inverse(lmat) if tinv is None else _tri_inverse_known(lmat, tinv)
    egc = jnp.exp(gcol)
    u = _bmm(tinv, v * beta)
    w = _bmm(tinv, kb * egc)
    a = _bmm_nt(q, k) * decay
    return u, w, q * egc, k * jnp.exp(glast - gcol), a, jnp.exp(glast), tinv


@jax.custom_vjp
def _bmm_tn(a, b):
    return _bdg(a, b, 1, 1)


def _bmm_tn_fwd(a, b):
    return _bdg(a, b, 1, 1), (a, b)


def _bmm_tn_bwd(res, dy):
    a, b = res
    return _bdg(b, dy, 2, 2), _bdg(a, dy, 2, 1)


_bmm_tn.defvjp(_bmm_tn_fwd, _bmm_tn_bwd)


def _dn_rec(state, u, w, qd, kd, a, cd):
    v_new = u - _bmm(w, state)
    out = _bmm(qd, state) + _bmm(a, v_new)
    return state * cd + _bmm_tn(kd, v_new), out


def _mix_tile(o_attn, o_raw, zs, ga, gb, x, gate1, dn_g, wb_a, wb_d, w_out, p_ya, p_yd, p_out):
    o_dn = jnp.concatenate([_rms(o_raw[h], dn_g) * _silu(zs[h]) for h in range(DN_H)], axis=-1)
    y_a = _mm(o_attn, wb_a) + p_ya
    y_d = _mm(o_dn, wb_d) + p_yd
    merged = _sigmoid(ga) * y_a + _sigmoid(gb) * y_d
    out = _mm(merged, w_out) + p_out
    return x + gate1 * out, o_dn, merged


def _mlp_tile(x1, gain, shift, scale, gate2, w_gu, w_dn, tgt, p_gu, p_yy):
    h2 = _norm_mod(x1, gain, shift, scale)
    gu = jnp.concatenate([_mm(h2, w) for w in w_gu], axis=-1) + p_gu
    act = _silu(gu[:, :FFN]) * gu[:, FFN:]
    yy = _mm(act, w_dn) + p_yy
    y = x1 + gate2 * yy
    err = y - tgt
    return 0.5 * jnp.sum(err * err) * (1.0 / D), (h2, act)


def _tok(bt, f):
    return pl.BlockSpec((None, bt, f), lambda b, i: (b, i, 0))


def _full(shape):
    return pl.BlockSpec(shape, lambda b, i: (0,) * len(shape))


def _per_batch(f):
    return pl.BlockSpec((None, 1, f), lambda b, i: (b, 0, 0))


def _sds(shape, dtype):
    return jax.ShapeDtypeStruct(shape, dtype)


class _Exchange(NamedTuple):
    ins: tuple
    out_shapes: tuple
    n_remote: int
    plan: Callable


def _exchange_copies(ex, in_refs, out_refs, send_sems, recv_sems):
    remote = ex.plan((lax.axis_index("x"), lax.axis_index("y"), lax.axis_index("c")), in_refs, out_refs)
    assert len(remote) == ex.n_remote
    return [pltpu.make_async_remote_copy(src_ref=src, dst_ref=dst, send_sem=send_sems.at[i], recv_sem=recv_sems.at[i],
                                         device_id=peer, device_id_type=pl.DeviceIdType.MESH)
            for i, (src, dst, peer) in enumerate(remote)]


def _hosted_call(body, name, grid, in_specs, out_specs, out_shape, scratch_shapes, semantics, ins, ex=None):
    if ex is None:
        outs = pl.pallas_call(body, name=name, grid=grid, in_specs=in_specs, out_specs=out_specs, out_shape=out_shape,
                              scratch_shapes=scratch_shapes,
                              compiler_params=_cparams(dimension_semantics=semantics))(*ins)
        return outs, ()
    n_in, n_out, n_scr = len(ins), len(out_shape), len(scratch_shapes)
    c_in, c_out = len(ex.ins), len(ex.out_shapes)
    steps = 1
    for g in grid:
        steps *= g

    def wrapped(*refs):
        a, b, c, d = n_in, n_in + c_in, n_in + c_in + n_out, n_in + c_in + n_out + c_out
        scratch, (send_sems, recv_sems) = refs[d:d + n_scr], refs[d + n_scr:]
        step = 0
        for axis, g in enumerate(grid):
            step = step * g + pl.program_id(axis)

        @pl.when(step == 0)
        def _():
            for cp in _exchange_copies(ex, refs[a:b], refs[c:d], send_sems, recv_sems):
                cp.start()
        body(*refs[:a], *refs[b:c], *scratch)

        @pl.when(step == steps - 1)
        def _():
            cps = _exchange_copies(ex, refs[a:b], refs[c:d], send_sems, recv_sems)
            for cp in cps:
                cp.wait_recv()
            for cp in cps:
                cp.wait_send()

    any_spec = pl.BlockSpec(memory_space=pl.ANY)
    res = pl.pallas_call(
        wrapped, name=name, grid=grid, in_specs=list(in_specs) + [any_spec] * c_in,
        out_specs=list(out_specs) + [any_spec] * c_out, out_shape=list(out_shape) + list(ex.out_shapes),
        scratch_shapes=list(scratch_shapes) + [pltpu.SemaphoreType.DMA((ex.n_remote,)), pltpu.SemaphoreType.DMA((ex.n_remote,))],
        compiler_params=_cparams(dimension_semantics=("arbitrary",) * len(grid)),
    )(*ins, *ex.ins)
    return res[:n_out], res[n_out:]


def _acc(ref, val, first):
    @pl.when(first)
    def _():
        ref[...] = val

    @pl.when(jnp.logical_not(first))
    def _():
        ref[...] += val


def _in_proj(x, mod, norm1_g, w_in, bt):
    B, S, _ = x.shape

    def body(x_ref, mod_ref, g_ref, w_ref, q_ref, kv_ref, dn_ref, z_ref, ga_ref, gb_ref, ba_ref, h_ref):
        h = _norm_mod(x_ref[...], g_ref[...], mod_ref[:, 0:D], mod_ref[:, D:2 * D]).astype(BF)
        h_ref[...] = h

        def proj(c0, c1):
            return jnp.dot(h, w_ref[:, c0:c1], preferred_element_type=F32)
        q_ref[...] = proj(0, C_KV).astype(BF)
        kv_ref[...] = proj(C_KV, C_DN).astype(BF)
        dn_ref[...] = proj(C_DN, C_Z).astype(BF)
        z_ref[...] = proj(C_Z, C_GA).astype(BF)
        ga_ref[...] = proj(C_GA, C_GB).astype(BF)
        gb_ref[...] = proj(C_GB, C_BA).astype(BF)
        ba_ref[...] = proj(C_BA, IN_PAD)

    widths = (QW, 2 * KVW, CONVW, DNW, D, D)
    return pl.pallas_call(
        body, name="in_proj", grid=(B, S // bt),
        in_specs=[_tok(bt, D), _per_batch(6 * D), _full((1, D)), _full((D, IN_PAD))],
        out_specs=[_tok(bt, w) for w in widths] + [_tok(bt, 128), _tok(bt, D)],
        out_shape=[_sds((B, S, w), BF) for w in widths] + [_sds((B, S, 128), F32), _sds((B, S, D), BF)],
        compiler_params=_cparams(dimension_semantics=("parallel", "parallel")),
    )(x, mod, norm1_g, w_in)


def _prev_blk(bt, f):
    return pl.BlockSpec((None, bt, f), lambda b, i: (b, jnp.maximum(i - 1, 0), 0))


QKV = QW + 2 * KVW


def _qk_slabs(q_ref, kv_ref):
    return ([q_ref[:, j * 2 * HD:(j + 1) * 2 * HD].astype(F32) for j in range(QW // (2 * HD))],
            [kv_ref[:, 0:KVW].astype(F32)])


def _qk_prep_fwd(q, kv, cos, sin, qg, kg, bt):
    B, S, _ = q.shape

    def body(q_ref, kv_ref, cos_ref, sin_ref, qg_ref, kg_ref, o_ref):
        qs, ks = _qk_slabs(q_ref, kv_ref)
        qn = _qk_prep(qs, qg_ref[...], cos_ref[...], sin_ref[...])
        kn = _qk_prep(ks, kg_ref[...], cos_ref[...], sin_ref[...])
        for j, t in enumerate(qn + kn):
            o_ref[:, j * 2 * HD:(j + 1) * 2 * HD] = t.astype(BF)
        o_ref[:, QW + KVW:QKV] = kv_ref[:, KVW:2 * KVW]

    return pl.pallas_call(
        body, name="qk_prep_fwd", grid=(B, S // bt),
        in_specs=[_tok(bt, QW), _tok(bt, 2 * KVW), _tok(bt, 2 * HD), _tok(bt, 2 * HD), _full((1, 2 * HD)), _full((1, 2 * HD))],
        out_specs=_tok(bt, QKV), out_shape=_sds((B, S, QKV), BF),
        compiler_params=_cparams(dimension_semantics=("parallel", "parallel")),
    )(q, kv, cos, sin, qg, kg)


def _qk_prep_bwd(q, kv, cos, sin, qg, kg, dqn, dkvn, bt):
    B, S, _ = q.shape

    def body(q_ref, kv_ref, cos_ref, sin_ref, qg_ref, kg_ref, dqn_ref, dkvn_ref, dq_ref, dkv_ref, dqg_ref, dkg_ref):
        qs, ks = _qk_slabs(q_ref, kv_ref)
        cos, sin = cos_ref[...], sin_ref[...]

        def f(qs, ks, qg, kg):
            return _qk_prep(qs, qg, cos, sin), _qk_prep(ks, kg, cos, sin)
        _, vjp = jax.vjp(f, qs, ks, qg_ref[...], kg_ref[...])
        n_q = len(qs)
        d_q = [dqn_ref[:, j * 2 * HD:(j + 1) * 2 * HD].astype(F32) for j in range(n_q)]
        d_k = [dkvn_ref[:, 0:KVW].astype(F32)]
        dqs, dks, dqg, dkg = vjp((d_q, d_k))
        for j in range(n_q):
            dq_ref[:, j * 2 * HD:(j + 1) * 2 * HD] = dqs[j].astype(BF)
        dkv_ref[:, 0:KVW] = dks[0].astype(BF)
        dkv_ref[:, KVW:2 * KVW] = dkvn_ref[:, KVW:2 * KVW]
        first = (pl.program_id(0) == 0) & (pl.program_id(1) == 0)
        _acc(dqg_ref, dqg, first)
        _acc(dkg_ref, dkg, first)

    return pl.pallas_call(
        body, name="qk_prep_bwd", grid=(B, S // bt),
        in_specs=[_tok(bt, QW), _tok(bt, 2 * KVW), _tok(bt, 2 * HD), _tok(bt, 2 * HD), _full((1, 2 * HD)), _full((1, 2 * HD)),
                  _tok(bt, QW), _tok(bt, 2 * KVW)],
        out_specs=[_tok(bt, QW), _tok(bt, 2 * KVW), _full((1, 2 * HD)), _full((1, 2 * HD))],
        out_shape=[_sds((B, S, QW), BF), _sds((B, S, 2 * KVW), BF), _sds((1, 2 * HD), F32), _sds((1, 2 * HD), F32)],
        compiler_params=_cparams(dimension_semantics=("arbitrary", "arbitrary")),
    )(q, kv, cos, sin, qg, kg, dqn, dkvn)


def _attn_load(qkv_ref, kvp_ref):
    qs = [qkv_ref[:, h * HD:(h + 1) * HD].astype(F32) for h in range(HEADS)]
    kc = [qkv_ref[:, QW + h * HD:QW + (h + 1) * HD].astype(F32) for h in range(KV_HEADS)]
    vc = [qkv_ref[:, QW + KVW + h * HD:QW + KVW + (h + 1) * HD].astype(F32) for h in range(KV_HEADS)]
    kp = [kvp_ref[:, h * HD:(h + 1) * HD].astype(F32) for h in range(KV_HEADS)]
    vp = [kvp_ref[:, KVW + h * HD:KVW + (h + 1) * HD].astype(F32) for h in range(KV_HEADS)]
    return qs, kc, kp, vc, vp


def _kv_prev_spec(index):
    return pl.BlockSpec((None, BLK, 2 * KVW), lambda b, i: (b, index(i), QW // (2 * KVW)))


def _attn_fwd(qkv, sinks):
    B, S, _ = qkv.shape

    def body(qkv_ref, kvp_ref, sk_ref, o_ref):
        qs, kc, kp, vc, vp = _attn_load(qkv_ref, kvp_ref)
        outs = _attn_block(qs, kc, kp, vc, vp, sk_ref[...], pl.program_id(1) > 0)
        for h in range(HEADS):
            o_ref[:, h * HD:(h + 1) * HD] = outs[h].astype(BF)

    return pl.pallas_call(
        body, name="attn_fwd", grid=(B, S // BLK),
        in_specs=[_tok(BLK, QKV), _kv_prev_spec(lambda i: jnp.maximum(i - 1, 0)), _full((1, HEADS))],
        out_specs=_tok(BLK, QW), out_shape=_sds((B, S, QW), BF),
        compiler_params=_cparams(dimension_semantics=("parallel", "parallel")),
    )(qkv, qkv, sinks)


def _conv_fwd_tile(xe_ref, x_ref, halo_ref, cw_ref, first, bt):
    halo = halo_ref[...].astype(F32)
    xe_ref[0:8, :] = jnp.where(first, 0.0, halo)
    xe_ref[8:bt + 8, :] = x_ref[...].astype(F32)
    y = cw_ref[0:1, :] * xe_ref[5:bt + 5, :]
    for j in range(1, CONV):
        y = y + cw_ref[j:j + 1, :] * xe_ref[5 + j:bt + 5 + j, :]
    return y


def _halo_spec(bt):
    return pl.BlockSpec((None, 8, CONVW), lambda b, i: (b, jnp.maximum(i * (bt // 8) - 1, 0), 0))


def _dn_prep(dn, ba, conv_w, alog, dtb, bt):
    B, S, _ = dn.shape

    def body(x_ref, halo_ref, ba_ref, cw_ref, al_ref, dt_ref, qkv_ref, bg_ref, xe_ref):
        y = _conv_fwd_tile(xe_ref, x_ref, halo_ref, cw_ref, pl.program_id(1) == 0, bt)
        ys = [y[:, j * DN_D:(j + 1) * DN_D] for j in range(3 * DN_H)]
        out, bg = _dn_tail(ys, ba_ref[...], al_ref[...], dt_ref[...])
        for j in range(3 * DN_H):
            qkv_ref[:, j * DN_D:(j + 1) * DN_D] = out[j]
        bg_ref[...] = bg

    return pl.pallas_call(
        body, name="dn_prep", grid=(B, S // bt),
        in_specs=[_tok(bt, CONVW), _halo_spec(bt), _tok(bt, 128), _full((CONV, CONVW)), _full((1, 128)), _full((1, 128))],
        out_specs=[_tok(bt, CONVW), _tok(bt, 128)],
        out_shape=[_sds((B, S, CONVW), F32), _sds((B, S, 128), F32)],
        scratch_shapes=[pltpu.VMEM((bt + 8, CONVW), F32)],
        compiler_params=_cparams(dimension_semantics=("parallel", "arbitrary")),
    )(dn, dn, ba, conv_w, alog, dtb)


def _dn_load(qkv_ref):
    qs = [qkv_ref[:, h * DN_D:(h + 1) * DN_D] for h in range(DN_H)]
    ks = [qkv_ref[:, DNW + h * DN_D:DNW + (h + 1) * DN_D] for h in range(DN_H)]
    vs = [qkv_ref[:, 2 * DNW + h * DN_D:2 * DNW + (h + 1) * DN_D] for h in range(DN_H)]
    return qs, ks, vs


DN_GROUP = 4
AW = DN_H * CHUNK


def _stack_heads(ref, G, offset, width):
    return jnp.stack([ref[g * CHUNK:(g + 1) * CHUNK, offset + h * width:offset + (h + 1) * width]
                      for g in range(G) for h in range(DN_H)])


def _dn_load_stack(qkv_ref, G):
    return tuple(_stack_heads(qkv_ref, G, j * DNW, DN_D) for j in range(3))


def _cd_spec(n):
    return pl.BlockSpec((None, n, 1, DN_D), lambda b, i: (b, i, 0, 0))


def _dn_intra_fwd(qkv, bg, ex=None):
    B, S, _ = qkv.shape
    nc = S // CHUNK
    G = min(DN_GROUP, nc)
    rows = G * CHUNK

    def body(qkv_ref, bg_ref, u_ref, w_ref, qd_ref, kd_ref, a_ref, cd_ref, t_ref):
        q, k, v = _dn_load_stack(qkv_ref, G)
        u, w, qd, kd, a, cd, tinv = _dn_intra(q, k, v, bg_ref[...].reshape(G, CHUNK, DN_D))
        lane_row = _iota((1, DN_D), 1)
        for g in range(G):
            rows = slice(g * CHUNK, (g + 1) * CHUNK)
            cd_row = jnp.zeros((1, DN_D), F32)
            for h in range(DN_H):
                n = g * DN_H + h
                cols = slice(h * DN_D, (h + 1) * DN_D)
                u_ref[rows, cols] = u[n]
                w_ref[rows, cols] = w[n].astype(BF)
                qd_ref[rows, cols] = qd[n].astype(BF)
                kd_ref[rows, cols] = kd[n].astype(BF)
                a_ref[rows, h * CHUNK:(h + 1) * CHUNK] = a[n].astype(BF)
                t_ref[rows, h * CHUNK:(h + 1) * CHUNK] = tinv[n]
                cd_row = cd_row + jnp.where(lane_row == h, cd[n], 0.0)
            cd_ref[g] = cd_row

    return _hosted_call(
        body, "dn_intra_fwd", (B, nc // G),
        in_specs=[_tok(rows, CONVW), _tok(rows, 128)],
        out_specs=[_tok(rows, DNW)] * 4 + [_tok(rows, AW), _cd_spec(G), _tok(rows, AW)],
        out_shape=[_sds((B, S, DNW), F32)] + [_sds((B, S, DNW), BF)] * 3 + [_sds((B, S, AW), BF), _sds((B, nc, 1, DN_D), F32),
                                                                            _sds((B, S, AW), F32)],
        scratch_shapes=[], semantics=("parallel", "parallel"), ins=(qkv, bg), ex=ex)


def _rec_stack(ref, B, width):
    return jnp.stack([ref[b, :, h * width:(h + 1) * width].astype(F32) for b in range(B) for h in range(DN_H)])


def _rec_load(B, u_ref, w_ref, qd_ref, kd_ref, a_ref, cd_ref):
    lane_row = _iota((1, DN_D), 1)
    cd = jnp.stack([jnp.sum(jnp.where(lane_row == h, cd_ref[b, 0], 0.0), axis=-1, keepdims=True)
                    for b in range(B) for h in range(DN_H)])
    return (_rec_stack(u_ref, B, DN_D), _rec_stack(w_ref, B, DN_D), _rec_stack(qd_ref, B, DN_D),
            _rec_stack(kd_ref, B, DN_D), _rec_stack(a_ref, B, CHUNK), cd)


def _rec_store(B, ref, val, width):
    for b in range(B):
        for h in range(DN_H):
            ref[b, :, h * width:(h + 1) * width] = val[b * DN_H + h]


def _rec_specs(B, index):
    def tok(f):
        return pl.BlockSpec((B, CHUNK, f), lambda i: (0, index(i), 0))
    cd = pl.BlockSpec((B, 1, 1, DN_D), lambda i: (0, index(i), 0, 0))
    st = pl.BlockSpec((B, None, DN_H, DN_D, DN_D), lambda i: (0, index(i), 0, 0, 0))
    return tok, cd, st


def _dn_rec_fwd(u, w, qd, kd, a, cd, ex=None):
    B, S, _ = u.shape
    nc = S // CHUNK
    tok, cd_spec, st_spec = _rec_specs(B, lambda i: i)

    def body(u_ref, w_ref, qd_ref, kd_ref, a_ref, cd_ref, o_ref, st_ref, s_ref):
        @pl.when(pl.program_id(0) == 0)
        def _():
            s_ref[...] = jnp.zeros_like(s_ref)
        state = s_ref[...]
        st_ref[...] = state.reshape(B, DN_H, DN_D, DN_D)
        new_state, out = _dn_rec(state, *_rec_load(B, u_ref, w_ref, qd_ref, kd_ref, a_ref, cd_ref))
        s_ref[...] = new_state
        _rec_store(B, o_ref, out, DN_D)

    return _hosted_call(
        body, "dn_rec_fwd", (nc,),
        in_specs=[tok(DNW)] * 4 + [tok(AW), cd_spec],
        out_specs=[tok(DNW), st_spec],
        out_shape=[_sds((B, S, DNW), F32), _sds((B, nc, DN_H, DN_D, DN_D), F32)],
        scratch_shapes=[pltpu.VMEM((B * DN_H, DN_D, DN_D), F32)],
        semantics=("arbitrary",), ins=(u, w, qd, kd, a, cd), ex=ex)


def _mix_load(oa_ref, or_ref, z_ref):
    o_raw = [or_ref[:, h * DN_D:(h + 1) * DN_D] for h in range(DN_H)]
    zs = [z_ref[:, h * DN_D:(h + 1) * DN_D].astype(F32) for h in range(DN_H)]
    return oa_ref[...].astype(F32), o_raw, zs


def _mix_fwd(o_attn, o_raw, z, ga, gb, x, mod, dn_g, w_branch, w_out, bt):
    B, S, _ = x.shape

    def body(oa_ref, or_ref, z_ref, ga_ref, gb_ref, x_ref, mod_ref, dg_ref, wb_ref, wo_ref, x1_ref, od_ref, mg_ref):
        oa, o_r, zs = _mix_load(oa_ref, or_ref, z_ref)
        x1, o_dn, merged = _mix_tile(oa, o_r, zs, ga_ref[...].astype(F32), gb_ref[...].astype(F32), x_ref[...],
                                     mod_ref[:, 2 * D:3 * D], dg_ref[...], wb_ref[0:QW, :], wb_ref[QW:2 * QW, :],
                                     wo_ref[...], 0.0, 0.0, 0.0)
        x1_ref[...] = x1
        od_ref[...] = o_dn.astype(BF)
        mg_ref[...] = merged.astype(BF)

    return pl.pallas_call(
        body, name="mix_fwd", grid=(B, S // bt),
        in_specs=[_tok(bt, QW), _tok(bt, DNW), _tok(bt, DNW), _tok(bt, D), _tok(bt, D), _tok(bt, D), _per_batch(6 * D),
                  _full((1, DN_D)), _full((D, D)), _full((D, D))],
        out_specs=[_tok(bt, D), _tok(bt, DNW), _tok(bt, D)],
        out_shape=[_sds((B, S, D), F32), _sds((B, S, DNW), BF), _sds((B, S, D), BF)],
        compiler_params=_cparams(dimension_semantics=("parallel", "parallel")),
    )(o_attn, o_raw, z, ga, gb, x, mod, dn_g, w_branch, w_out)


def _mlp(x1, tgt, mod, norm2_g, w_gu, w_dn, bt):
    B, S, _ = x1.shape

    def body(x1_ref, t_ref, mod_ref, g_ref, wgu_ref, wdn_ref,
             dx1_ref, h2_ref, act_ref, dgu_ref, dyy_ref, loss_ref, dmod_ref, dg_ref):
        w_gu_v, w_dn_v, t = [wgu_ref[k] for k in range(N_CHIP)], wdn_ref[...], t_ref[...]

        def f(x1, gain, shift, scale, gate2, p_gu, p_yy):
            return _mlp_tile(x1, gain, shift, scale, gate2, w_gu_v, w_dn_v, t, p_gu, p_yy)
        zero_gu = jnp.zeros((bt, 2 * FFN), F32)
        zero_yy = jnp.zeros((bt, D), F32)
        loss, vjp, (h2, act) = jax.vjp(f, x1_ref[...], g_ref[...], mod_ref[:, 3 * D:4 * D], mod_ref[:, 4 * D:5 * D],
                                       mod_ref[:, 5 * D:6 * D], zero_gu, zero_yy, has_aux=True)
        dx1, dgain, dshift, dscale, dgate2, dgu, dyy = vjp(jnp.ones((), F32))
        dx1_ref[...] = dx1
        h2_ref[...] = h2.astype(BF)
        act_ref[...] = act.astype(BF)
        dgu_ref[...] = dgu.astype(BF)
        dyy_ref[...] = dyy.astype(BF)
        first = pl.program_id(1) == 0
        _acc(loss_ref, jnp.reshape(loss, (1, 1)), first)
        _acc(dmod_ref, jnp.concatenate([dshift, dscale, dgate2], axis=-1), first)
        _acc(dg_ref, dgain, first)

    return pl.pallas_call(
        body, name="mlp", grid=(B, S // bt),
        in_specs=[_tok(bt, D), _tok(bt, D), _per_batch(6 * D), _full((1, D)), _full((N_CHIP, D, 2 * FFN // N_CHIP)),
                  _full((FFN, D))],
        out_specs=[_tok(bt, D), _tok(bt, D), _tok(bt, FFN), _tok(bt, 2 * FFN), _tok(bt, D),
                   _per_batch(1), _per_batch(3 * D), _per_batch(D)],
        out_shape=[_sds((B, S, D), F32), _sds((B, S, D), BF), _sds((B, S, FFN), BF), _sds((B, S, 2 * FFN), BF),
                   _sds((B, S, D), BF), _sds((B, 1, 1), F32), _sds((B, 1, 3 * D), F32), _sds((B, 1, D), F32)],
        compiler_params=_cparams(dimension_semantics=("parallel", "arbitrary")),
    )(x1, tgt, mod, norm2_g, w_gu, w_dn)


def _mix_bwd(o_attn, o_raw, z, ga, gb, x, mod, dn_g, w_branch, w_out, dx1, bt, ex=None):
    B, S, _ = x.shape

    def body(oa_ref, or_ref, z_ref, ga_ref, gb_ref, x_ref, mod_ref, dg_ref, wb_ref, wo_ref, dx1_ref,
             doa_ref, dor_ref, dz_ref, dga_ref, dgb_ref, dya_ref, dyd_ref, dout_ref, dgate_ref, ddg_ref):
        oa, o_r, zs = _mix_load(oa_ref, or_ref, z_ref)
        wb_a, wb_d, wo = wb_ref[0:QW, :], wb_ref[QW:2 * QW, :], wo_ref[...]

        def f(oa, o_r, zs, ga, gb, gate1, dn_g, p_ya, p_yd, p_out):
            return _mix_tile(oa, o_r, zs, ga, gb, x_ref[...], gate1, dn_g, wb_a, wb_d, wo, p_ya, p_yd, p_out)[0]
        zero = jnp.zeros((bt, D), F32)
        _, vjp = jax.vjp(f, oa, o_r, zs, ga_ref[...].astype(F32), gb_ref[...].astype(F32), mod_ref[:, 2 * D:3 * D],
                         dg_ref[...], zero, zero, zero)
        doa, dor, dzs, dga, dgb, dgate1, ddn_g, dya, dyd, dout = vjp(dx1_ref[...])
        doa_ref[...] = doa
        for h in range(DN_H):
            dor_ref[:, h * DN_D:(h + 1) * DN_D] = dor[h]
            dz_ref[:, h * DN_D:(h + 1) * DN_D] = dzs[h].astype(BF)
        dga_ref[...] = dga.astype(BF)
        dgb_ref[...] = dgb.astype(BF)
        dya_ref[...] = dya.astype(BF)
        dyd_ref[...] = dyd.astype(BF)
        dout_ref[...] = dout.astype(BF)
        first = pl.program_id(1) == 0
        _acc(dgate_ref, dgate1, first)
        _acc(ddg_ref, ddn_g, first)

    return _hosted_call(
        body, "mix_bwd", (B, S // bt),
        in_specs=[_tok(bt, QW), _tok(bt, DNW), _tok(bt, DNW), _tok(bt, D), _tok(bt, D), _tok(bt, D), _per_batch(6 * D),
                  _full((1, DN_D)), _full((D, D)), _full((D, D)), _tok(bt, D)],
        out_specs=[_tok(bt, QW), _tok(bt, DNW), _tok(bt, DNW), _tok(bt, D), _tok(bt, D), _tok(bt, D), _tok(bt, D), _tok(bt, D),
                   _per_batch(D), _per_batch(DN_D)],
        out_shape=[_sds((B, S, QW), F32), _sds((B, S, DNW), F32), _sds((B, S, DNW), BF), _sds((B, S, D), BF),
                   _sds((B, S, D), BF), _sds((B, S, D), BF), _sds((B, S, D), BF), _sds((B, S, D), BF),
                   _sds((B, 1, D), F32), _sds((B, 1, DN_D), F32)],
        scratch_shapes=[], semantics=("parallel", "arbitrary"),
        ins=(o_attn, o_raw, z, ga, gb, x, mod, dn_g, w_branch, w_out, dx1), ex=ex)


def _dn_rec_bwd(u, w, qd, kd, a, cd, states, d_o, ex=None):
    B, S, _ = u.shape
    nc = S // CHUNK
    tok, cd_spec, st_spec = _rec_specs(B, lambda i: nc - 1 - i)

    def body(u_ref, w_ref, qd_ref, kd_ref, a_ref, cd_ref, st_ref, do_ref,
             du_ref, dw_ref, dqd_ref, dkd_ref, da_ref, dcd_ref, ds_ref):
        @pl.when(pl.program_id(0) == 0)
        def _():
            ds_ref[...] = jnp.zeros_like(ds_ref)
        state = st_ref[...].reshape(B * DN_H, DN_D, DN_D)
        _, vjp = jax.vjp(_dn_rec, state, *_rec_load(B, u_ref, w_ref, qd_ref, kd_ref, a_ref, cd_ref))
        dst, du, dw, dqd, dkd, da, dcd = vjp((ds_ref[...], _rec_stack(do_ref, B, DN_D)))
        ds_ref[...] = dst
        for ref, val, width in ((du_ref, du, DN_D), (dw_ref, dw, DN_D), (dqd_ref, dqd, DN_D), (dkd_ref, dkd, DN_D),
                                (da_ref, da, CHUNK)):
            _rec_store(B, ref, val, width)
        lane_row = _iota((1, DN_D), 1)
        for b in range(B):
            row = jnp.zeros((1, DN_D), F32)
            for h in range(DN_H):
                row = row + jnp.where(lane_row == h, dcd[b * DN_H + h], 0.0)
            dcd_ref[b, 0] = row

    return _hosted_call(
        body, "dn_rec_bwd", (nc,),
        in_specs=[tok(DNW)] * 4 + [tok(AW), cd_spec, st_spec, tok(DNW)],
        out_specs=[tok(DNW)] * 4 + [tok(AW), cd_spec],
        out_shape=[_sds((B, S, DNW), F32)] * 4 + [_sds((B, S, AW), F32), _sds((B, nc, 1, DN_D), F32)],
        scratch_shapes=[pltpu.VMEM((B * DN_H, DN_D, DN_D), F32)],
        semantics=("arbitrary",), ins=(u, w, qd, kd, a, cd, states, d_o), ex=ex)


def _dn_intra_bwd(qkv, bg, tinv, du, dw, dqd, dkd, da, dcd, ex=None):
    B, S, _ = qkv.shape
    nc = S // CHUNK
    G = min(DN_GROUP, nc)
    rows = G * CHUNK

    def body(qkv_ref, bg_ref, t_ref, du_ref, dw_ref, dqd_ref, dkd_ref, da_ref, dcd_ref, dqkv_ref, dbg_ref):
        q, k, v = _dn_load_stack(qkv_ref, G)
        known = _stack_heads(t_ref, G, 0, CHUNK)
        _, vjp = jax.vjp(lambda q, k, v, bg: _dn_intra(q, k, v, bg, known)[:6], q, k, v,
                         bg_ref[...].reshape(G, CHUNK, DN_D))
        lane_row = _iota((1, DN_D), 1)
        dcd = jnp.stack([jnp.sum(jnp.where(lane_row == h, dcd_ref[g], 0.0), axis=-1, keepdims=True)
                         for g in range(G) for h in range(DN_H)])
        dq, dk, dv, dbg = vjp((_stack_heads(du_ref, G, 0, DN_D), _stack_heads(dw_ref, G, 0, DN_D),
                               _stack_heads(dqd_ref, G, 0, DN_D), _stack_heads(dkd_ref, G, 0, DN_D),
                               _stack_heads(da_ref, G, 0, CHUNK), dcd))
        for g in range(G):
            rows = slice(g * CHUNK, (g + 1) * CHUNK)
            for h in range(DN_H):
                n = g * DN_H + h
                dqkv_ref[rows, h * DN_D:(h + 1) * DN_D] = dq[n]
                dqkv_ref[rows, DNW + h * DN_D:DNW + (h + 1) * DN_D] = dk[n]
                dqkv_ref[rows, 2 * DNW + h * DN_D:2 * DNW + (h + 1) * DN_D] = dv[n]
        dbg_ref[...] = dbg.reshape(G * CHUNK, DN_D)

    return _hosted_call(
        body, "dn_intra_bwd", (B, nc // G),
        in_specs=[_tok(rows, CONVW), _tok(rows, 128), _tok(rows, AW)] + [_tok(rows, DNW)] * 4 + [_tok(rows, AW), _cd_spec(G)],
        out_specs=[_tok(rows, CONVW), _tok(rows, 128)],
        out_shape=[_sds((B, S, CONVW), F32), _sds((B, S, 128), F32)],
        scratch_shapes=[], semantics=("parallel", "parallel"), ins=(qkv, bg, tinv, du, dw, dqd, dkd, da, dcd), ex=ex)


def _dn_prep_bwd(dn, ba, conv_w, alog, dtb, dqkv, dbg, bt):
    B, S, _ = dn.shape
    nt = S // bt

    def rev(f):
        return pl.BlockSpec((None, bt, f), lambda b, i: (b, nt - 1 - i, 0))

    halo = pl.BlockSpec((None, 8, CONVW), lambda b, i: (b, jnp.maximum((nt - 1 - i) * (bt // 8) - 1, 0), 0))

    def body(x_ref, halo_ref, ba_ref, cw_ref, al_ref, dt_ref, dqkv_ref, dbg_ref,
             dx_ref, dba_ref, dcw_ref, dal_ref, ddt_ref, xe_ref, dye_ref):
        i = pl.program_id(1)
        y = _conv_fwd_tile(xe_ref, x_ref, halo_ref, cw_ref, i == nt - 1, bt)
        ys = [y[:, j * DN_D:(j + 1) * DN_D] for j in range(3 * DN_H)]
        _, vjp = jax.vjp(_dn_tail, ys, ba_ref[...], al_ref[...], dt_ref[...])
        d_out = [dqkv_ref[:, j * DN_D:(j + 1) * DN_D] for j in range(3 * DN_H)]
        dys, dba, dal, ddt = vjp((d_out, dbg_ref[...]))
        @pl.when(i == 0)
        def _():
            dye_ref[bt:bt + 8, :] = jnp.zeros((8, CONVW), F32)

        @pl.when(i > 0)
        def _():
            dye_ref[bt:bt + 8, :] = dye_ref[0:8, :]
        for j in range(3 * DN_H):
            dye_ref[0:bt, j * DN_D:(j + 1) * DN_D] = dys[j]
        dx = cw_ref[0:1, :] * dye_ref[3:bt + 3, :]
        for j in range(1, CONV):
            dx = dx + cw_ref[j:j + 1, :] * dye_ref[3 - j:bt + 3 - j, :]
        dx_ref[...] = dx.astype(BF)
        dy = dye_ref[0:bt, :]
        dcw = jnp.concatenate([jnp.sum(dy * xe_ref[5 + j:bt + 5 + j, :], axis=0, keepdims=True) for j in range(CONV)], axis=0)
        first = (i == 0) & (pl.program_id(0) == 0)
        dba_ref[...] = dba
        _acc(dcw_ref, dcw, first)
        _acc(dal_ref, dal, first)
        _acc(ddt_ref, ddt, first)

    return pl.pallas_call(
        body, name="dn_prep_bwd", grid=(B, nt),
        in_specs=[rev(CONVW), halo, rev(128), _full((CONV, CONVW)), _full((1, 128)), _full((1, 128)), rev(CONVW), rev(128)],
        out_specs=[rev(CONVW), rev(128), _full((CONV, CONVW)), _full((1, 128)), _full((1, 128))],
        out_shape=[_sds((B, S, CONVW), BF), _sds((B, S, 128), F32), _sds((CONV, CONVW), F32), _sds((1, 128), F32),
                   _sds((1, 128), F32)],
        scratch_shapes=[pltpu.VMEM((bt + 8, CONVW), F32), pltpu.VMEM((bt + 8, CONVW), F32)],
        compiler_params=_cparams(dimension_semantics=("arbitrary", "arbitrary")),
    )(dn, dn, ba, conv_w, alog, dtb, dqkv, dbg)


def _attn_bwd(qkv, sinks, d_o):
    B, S, _ = qkv.shape
    nb = S // BLK

    def cur(f):
        return pl.BlockSpec((None, BLK, f), lambda b, i: (b, jnp.minimum(i, nb - 1), 0))

    def out_prev(f):
        return pl.BlockSpec((None, BLK, f), lambda b, i: (b, jnp.maximum(i - 1, 0), 0))

    def body(qkv_ref, kvp_ref, sk_ref, do_ref, dq_ref, dkv_ref, dsk_ref, carry_ref):
        n = pl.program_id(1)
        first = (n == 0) & (pl.program_id(0) == 0)

        @pl.when(n == 0)
        def _():
            carry_ref[...] = jnp.zeros_like(carry_ref)

        @pl.when(n < nb)
        def _():
            qs, kc, kp, vc, vp = _attn_load(qkv_ref, kvp_ref)

            def f(qs, kc, kp, vc, vp, sk):
                return _attn_block(qs, kc, kp, vc, vp, sk, n > 0)
            _, vjp = jax.vjp(f, qs, kc, kp, vc, vp, sk_ref[...])
            d_outs = [do_ref[:, h * HD:(h + 1) * HD] for h in range(HEADS)]
            dqs, dkc, dkp, dvc, dvp, dsk = vjp(d_outs)
            for h in range(HEADS):
                dq_ref[:, h * HD:(h + 1) * HD] = dqs[h].astype(BF)
            for h in range(KV_HEADS):
                ksl = slice(h * HD, (h + 1) * HD)
                vsl = slice(KVW + h * HD, KVW + (h + 1) * HD)
                dkv_ref[:, ksl] = (carry_ref[:, ksl] + dkp[h]).astype(BF)
                dkv_ref[:, vsl] = (carry_ref[:, vsl] + dvp[h]).astype(BF)
                carry_ref[:, ksl] = dkc[h]
                carry_ref[:, vsl] = dvc[h]
            _acc(dsk_ref, dsk, first)

        @pl.when(n == nb)
        def _():
            dkv_ref[...] = carry_ref[...].astype(BF)

    return pl.pallas_call(
        body, name="attn_bwd", grid=(B, nb + 1),
        in_specs=[cur(QKV), _kv_prev_spec(lambda i: jnp.maximum(jnp.minimum(i, nb - 1) - 1, 0)), _full((1, HEADS)), cur(QW)],
        out_specs=[cur(QW), out_prev(2 * KVW), _full((1, HEADS))],
        out_shape=[_sds((B, S, QW), BF), _sds((B, S, 2 * KVW), BF), _sds((1, HEADS), F32)],
        scratch_shapes=[pltpu.VMEM((BLK, 2 * KVW), F32)],
        compiler_params=_cparams(dimension_semantics=("arbitrary", "arbitrary")),
    )(qkv, qkv, sinks, d_o)


def _in_proj_bwd(x, mod, norm1_g, w_in, pieces, dba, dx1, bt):
    B, S, _ = x.shape
    widths = (QW, 2 * KVW, CONVW, DNW, D, D)

    def body(x_ref, mod_ref, g_ref, w_ref, dq_ref, dkv_ref, ddn_ref, dz_ref, dga_ref, dgb_ref, dba_ref, dx1_ref,
             gx_ref, dp_ref, dmod_ref, dg_ref):
        dp = jnp.concatenate([r[...] for r in (dq_ref, dkv_ref, ddn_ref, dz_ref, dga_ref, dgb_ref)]
                             + [dba_ref[...].astype(BF)], axis=-1)
        dp_ref[...] = dp
        dh = lax.dot_general(dp, w_ref[...], (((1,), (1,)), ((), ())), preferred_element_type=F32)
        _, vjp = jax.vjp(_norm_mod, x_ref[...], g_ref[...], mod_ref[:, 0:D], mod_ref[:, D:2 * D])
        dx, dgain, dshift, dscale = vjp(dh)
        gx_ref[...] = dx + dx1_ref[...]
        first = pl.program_id(1) == 0
        _acc(dmod_ref, jnp.concatenate([dshift, dscale], axis=-1), first)
        _acc(dg_ref, dgain, first)

    return pl.pallas_call(
        body, name="in_proj_bwd", grid=(B, S // bt),
        in_specs=[_tok(bt, D), _per_batch(6 * D), _full((1, D)), _full((D, IN_PAD))] + [_tok(bt, w) for w in widths]
        + [_tok(bt, 128), _tok(bt, D)],
        out_specs=[_tok(bt, D), _tok(bt, IN_PAD), _per_batch(2 * D), _per_batch(D)],
        out_shape=[_sds((B, S, D), F32), _sds((B, S, IN_PAD), BF), _sds((B, 1, 2 * D), F32), _sds((B, 1, D), F32)],
        compiler_params=_cparams(dimension_semantics=("parallel", "arbitrary")),
    )(x, mod, norm1_g, w_in, *pieces, dba, dx1)


def _matmul_tn(tag, a, b, bk, bn, bt, col_blocks=False):
    T, K = a.shape
    N = b.shape[1]
    nt = T // bt
    if col_blocks:
        assert bk == K
        out_spec = pl.BlockSpec((None, bk, bn), lambda i, j, t: (j, 0, 0))
        out_shape = _sds((N // bn, K, bn), F32)
    else:
        out_spec = pl.BlockSpec((bk, bn), lambda i, j, t: (i, j))
        out_shape = _sds((K, N), F32)

    def body(a_ref, b_ref, o_ref, acc_ref):
        t = pl.program_id(2)

        @pl.when(t == 0)
        def _():
            acc_ref[...] = jnp.zeros_like(acc_ref)
        acc_ref[...] += lax.dot_general(a_ref[...], b_ref[...], (((0,), (0,)), ((), ())), preferred_element_type=F32)

        @pl.when(t == nt - 1)
        def _():
            o_ref[...] = acc_ref[...]

    return pl.pallas_call(
        body, name=f"grad_{tag}", grid=(K // bk, N // bn, nt),
        in_specs=[pl.BlockSpec((bt, bk), lambda i, j, t: (t, i)), pl.BlockSpec((bt, bn), lambda i, j, t: (t, j))],
        out_specs=out_spec, out_shape=out_shape,
        scratch_shapes=[pltpu.VMEM((bk, bn), F32)],
        compiler_params=_cparams(dimension_semantics=("parallel", "parallel", "arbitrary")),
    )(a, b)


def _rope_table(positions):
    inv_freq = THETA ** (-jnp.arange(0, ROT, 2, dtype=F32) / ROT)
    rest = jnp.zeros((HD - ROT,), F32)
    freq = jnp.concatenate([inv_freq, inv_freq, rest] * 2)
    sign = jnp.concatenate([-jnp.ones_like(inv_freq), jnp.ones_like(inv_freq), rest] * 2)
    ang = positions.astype(F32)[..., None] * freq
    return jnp.cos(ang), jnp.sin(ang) * sign


def _lane_pad(v, offset, width=128):
    return jnp.zeros((1, width), F32).at[0, offset:offset + v.shape[-1]].set(v.reshape(-1))


def _tile(S, want):
    return min(S, want)


class _Hosted:
    def __init__(self, call):
        self.call = call
        self.outs = None

    def __call__(self, ex):
        self.outs, landed = self.call(ex)
        return landed


def _local_step(x, mod, positions, tgt, norm1_g, w_in_pad, conv_w, q_norm_g, k_norm_g, sinks, a_log, dt_bias,
                dn_norm_g, w_branch, w_out, norm2_g, w_gu, w_dn, dist=None):
    B, S, _ = x.shape
    T = B * S
    cos_t, sin_t = _rope_table(positions)
    qg2 = jnp.concatenate([q_norm_g, q_norm_g], axis=-1)
    kg2 = jnp.concatenate([k_norm_g, k_norm_g], axis=-1)
    alog = _lane_pad(a_log, DN_H)
    dtb = _lane_pad(dt_bias, DN_H)
    conv2 = conv_w.reshape(CONV, CONVW)
    bt = _tile(S, 512)
    bt_mlp = _tile(S, 256)

    q, kv, dn, z, ga, gb, ba, h1 = _in_proj(x, mod, norm1_g, w_in_pad, bt)
    qkv_n = _qk_prep_fwd(q, kv, cos_t, sin_t, qg2, kg2, bt)
    o_attn = _attn_fwd(qkv_n, sinks)
    dqkv, bg = _dn_prep(dn, ba, conv2, alog, dtb, bt)
    intra = _Hosted(lambda ex: _dn_intra_fwd(dqkv, bg, ex))
    rec = _Hosted(lambda ex: _dn_rec_fwd(*intra.outs[:6], ex))
    if dist is None:
        intra(None)
        rec(None)
    else:
        f_br, f_out, w_gu, f_dn = _gather_weights("late", [w_branch, w_out, w_gu, w_dn], *dist, hosts=[intra, rec])
        w_branch, w_out, w_dn = (f.reshape(N_CHIP * f.shape[1], f.shape[2]) for f in (f_br, f_out, f_dn))
    dn_u, dn_w, dn_qd, dn_kd, dn_a, dn_cd, dn_tinv = intra.outs
    o_raw, states = rec.outs
    x1, o_dn, merged = _mix_fwd(o_attn, o_raw, z, ga, gb, x, mod, dn_norm_g, w_branch, w_out, bt)
    dx1, h2, act, dgu, dyy, loss, dmod2, dnorm2 = _mlp(x1, tgt, mod, norm2_g, w_gu, w_dn, bt_mlp)

    def flat(t):
        return t.reshape(T, t.shape[-1])
    tn = functools.partial(_matmul_tn, bt=_tile(T, 512))
    g_w_dn = tn("w_down", flat(act), flat(dyy), bk=FFN, bn=D // 2)
    g_w_gu = tn("w_gate_up", flat(h2), flat(dgu), bk=D, bn=2 * FFN // N_CHIP, col_blocks=True)

    mix_b = _Hosted(lambda ex: _mix_bwd(o_attn, o_raw, z, ga, gb, x, mod, dn_norm_g, w_branch, w_out, dx1, bt_mlp, ex))
    rec_b = _Hosted(lambda ex: _dn_rec_bwd(dn_u, dn_w, dn_qd, dn_kd, dn_a, dn_cd, states, mix_b.outs[1], ex))
    intra_b = _Hosted(lambda ex: _dn_intra_bwd(dqkv, bg, dn_tinv, *rec_b.outs, ex))
    if dist is None:
        for host in (mix_b, rec_b, intra_b):
            host(None)
    else:
        g_w_gu, g_w_dn = _reduce_grads(("w_gate_up", "w_down"), [g_w_gu, g_w_dn.reshape(N_CHIP, -1, D)], *dist,
                                       hosts=[mix_b, rec_b, intra_b])
    d_oa, _, dz, dga, dgb, dya, dyd, dout, dgate1, ddn_g = mix_b.outs
    d_dqkv, dbg = intra_b.outs
    g_w_out = tn("w_out", flat(merged), flat(dout), bk=D, bn=D)
    g_w_br = jnp.concatenate([tn("w_branch_attn", flat(o_attn), flat(dya), bk=QW, bn=D),
                              tn("w_branch_dn", flat(o_dn), flat(dyd), bk=DNW, bn=D)], axis=0)
    d_dn, dba, dconv, dalog, ddtb = _dn_prep_bwd(dn, ba, conv2, alog, dtb, d_dqkv, dbg, bt)
    dqn, dkvn, dsk = _attn_bwd(qkv_n, sinks, d_oa)
    dq, dkv, dqg2, dkg2 = _qk_prep_bwd(q, kv, cos_t, sin_t, qg2, kg2, dqn, dkvn, bt)
    dqg = dqg2[:, :HD] + dqg2[:, HD:]
    dkg = dkg2[:, :HD] + dkg2[:, HD:]
    grad_x, dproj, dmod1, dnorm1 = _in_proj_bwd(x, mod, norm1_g, w_in_pad, (dq, dkv, d_dn, dz, dga, dgb), dba, dx1, bt)
    g_w_in = tn("w_in", flat(h1), flat(dproj), bk=D, bn=IN_PAD // 3)

    dmod = jnp.concatenate([dmod1, dgate1, dmod2], axis=-1)
    small = dict(norm1_g=jnp.sum(dnorm1, axis=0), norm2_g=jnp.sum(dnorm2, axis=0), q_norm_g=dqg, k_norm_g=dkg,
                 sinks=dsk, a_log=dalog[:, DN_H:2 * DN_H], dt_bias=ddtb[:, DN_H:2 * DN_H],
                 dn_norm_g=jnp.sum(ddn_g, axis=0), conv_w=dconv)
    return jnp.sum(loss), grad_x, dmod, small, (g_w_in, g_w_br, g_w_out, g_w_gu, g_w_dn)


def _flip(me, f):
    return (me[0] ^ ((f >> 2) & 1), me[1] ^ ((f >> 1) & 1), me[2] ^ (f & 1))


def _comm_call(name, ex):
    n_in, n_out = len(ex.ins), len(ex.out_shapes)

    def body(*refs):
        cps = _exchange_copies(ex, refs[:n_in], refs[n_in:n_in + n_out], *refs[n_in + n_out:])
        for cp in cps:
            cp.start()
        for cp in cps:
            cp.wait_recv()
        for cp in cps:
            cp.wait_send()

    any_spec = pl.BlockSpec(memory_space=pl.ANY)
    return pl.pallas_call(
        body, name=name, in_specs=[any_spec] * n_in, out_specs=[any_spec] * n_out, out_shape=list(ex.out_shapes),
        scratch_shapes=[pltpu.SemaphoreType.DMA((ex.n_remote,)), pltpu.SemaphoreType.DMA((ex.n_remote,))],
    )(*ex.ins)


def _by_origin(own, received, index):
    stack = jnp.concatenate([own[None], received], axis=0)
    n = stack.shape[0]
    return jnp.stack([lax.dynamic_index_in_dim(stack, k ^ index, 0, keepdims=False) for k in range(n)])


def _gather_devices(name, arrs, dev):
    def plan(me, in_refs, out_refs):
        return [(a, o.at[f - 1], _flip(me, f)) for a, o in zip(in_refs, out_refs) for f in range(1, N_DEV)]
    outs = tuple(_sds((N_DEV - 1,) + a.shape, a.dtype) for a in arrs)
    got = _comm_call(name, _Exchange(tuple(arrs), outs, (N_DEV - 1) * len(arrs), plan))
    return [_by_origin(a, g, dev) for a, g in zip(arrs, got)]


def _gather_chips(name, arrs, chip):
    def plan(me, in_refs, out_refs):
        return [(a, o.at[j], _flip(me, 2 * (j + 1))) for a, o in zip(in_refs, out_refs) for j in range(N_CHIP - 1)]
    outs = tuple(_sds((N_CHIP - 1,) + a.shape, a.dtype) for a in arrs)
    got = _comm_call(name, _Exchange(tuple(arrs), outs, (N_CHIP - 1) * len(arrs), plan))
    return [_by_origin(a, g, chip) for a, g in zip(arrs, got)]


def _halves(core, mine, other):
    lo = jnp.where(core == 0, mine, other)
    hi = jnp.where(core == 0, other, mine)
    return jnp.concatenate([lo, hi], axis=-2)


def _swap_cores_ex(arrs):
    def plan(me, in_refs, out_refs):
        return [(g, o, _flip(me, 1)) for g, o in zip(in_refs, out_refs)]
    return _Exchange(tuple(arrs), tuple(_sds(g.shape, g.dtype) for g in arrs), len(arrs), plan)


def _gather_weights(tag, shards, chip, core, hosts=None):
    def plan_ici(me, in_refs, out_refs):
        remote = []
        for a, o in zip(in_refs, out_refs):
            half = a.shape[0] // 2
            mine = a.at[pl.ds(me[2] * half, half)]
            remote += [(mine, o.at[j], _flip(me, 2 * (j + 1))) for j in range(N_CHIP - 1)]
        return remote
    run = hosts or [functools.partial(_comm_call, f"weights_ici_{tag}"), functools.partial(_comm_call, f"weights_d2d_{tag}")]
    mine = run[0](_Exchange(tuple(shards), tuple(_sds((N_CHIP - 1, a.shape[0] // 2, a.shape[1]), a.dtype) for a in shards),
                            (N_CHIP - 1) * len(shards), plan_ici))
    other = run[1](_swap_cores_ex(mine))
    return [_by_origin(a, _halves(core, g, h), chip) for a, g, h in zip(shards, mine, other)]


def _rows(r):
    for br in (512, 352, 256, 128, 64, 32, 16, 8):
        if r % br == 0:
            return br
    raise ValueError(r)


def _pair_add(tag, g, recv, c):
    n, r, cols = g.shape
    half = r // 2
    br = _rows(half)
    nb = half // br

    def body(c_ref, g_ref, r_ref, o_ref):
        o_ref[...] = (g_ref[...] + r_ref[...]).astype(BF)

    return pl.pallas_call(
        body, name=f"pair_add_{tag}",
        grid_spec=pltpu.PrefetchScalarGridSpec(
            num_scalar_prefetch=1, grid=(n, nb),
            in_specs=[pl.BlockSpec((None, br, cols), lambda k, i, c_ref: (k, c_ref[0] * nb + i, 0)),
                      pl.BlockSpec((None, br, cols), lambda k, i, c_ref: (k, i, 0))],
            out_specs=pl.BlockSpec((None, br, cols), lambda k, i, c_ref: (k, i, 0))),
        out_shape=_sds((n, half, cols), BF),
        compiler_params=_cparams(dimension_semantics=("parallel", "parallel")),
    )(c, g, recv)


def _sum_chips(tag, p, q, chip):
    n, r, cols = q.shape
    br = _rows(r)

    def body(chip_ref, p_ref, q_ref, o_ref):
        acc = p_ref[...].astype(F32)
        for k in range(n):
            acc = acc + q_ref[k].astype(F32)
        o_ref[...] = acc

    return pl.pallas_call(
        body, name=f"sum_chips_{tag}",
        grid_spec=pltpu.PrefetchScalarGridSpec(
            num_scalar_prefetch=1, grid=(r // br,),
            in_specs=[pl.BlockSpec((None, br, cols), lambda i, chip_ref: (chip_ref[0], i, 0)),
                      pl.BlockSpec((n, br, cols), lambda i, chip_ref: (0, i, 0))],
            out_specs=pl.BlockSpec((br, cols), lambda i, chip_ref: (i, 0))),
        out_shape=_sds((r, cols), F32),
        compiler_params=_cparams(dimension_semantics=("parallel",)),
    )(chip, p, q)


def _reduce_grads(tags, grads, chip, core, hosts=None):
    core_arr = core.reshape(1).astype(jnp.int32)
    chip_arr = chip.reshape(1).astype(jnp.int32)
    name = "_".join(tags)
    run = hosts or [functools.partial(_comm_call, f"grads_{stage}_{name}") for stage in ("pair", "chips", "swap")]

    def plan_pair(me, in_refs, out_refs):
        remote = []
        for g, o in zip(in_refs, out_refs):
            half = g.shape[1] // 2
            remote += [(g.at[k, pl.ds((1 - me[2]) * half, half)], o.at[k], _flip(me, 1)) for k in range(N_CHIP)]
        return remote
    recv = run[0](_Exchange(tuple(grads), tuple(_sds((N_CHIP, g.shape[1] // 2, g.shape[2]), F32) for g in grads),
                            N_CHIP * len(grads), plan_pair))
    pair = [_pair_add(t, g, r, core_arr) for t, g, r in zip(tags, grads, recv)]

    def plan_chips(me, in_refs, out_refs):
        remote = []
        for p, o in zip(in_refs, out_refs):
            for j in range(N_CHIP - 1):
                peer = _flip(me, 2 * (j + 1))
                remote.append((p.at[2 * peer[0] + peer[1]], o.at[j], peer))
        return remote
    parts = run[1](_Exchange(tuple(pair), tuple(_sds((N_CHIP - 1,) + p.shape[1:], BF) for p in pair),
                             (N_CHIP - 1) * len(pair), plan_chips))
    mine = [_sum_chips(t, p, q, chip_arr) for t, p, q in zip(tags, pair, parts)]
    other = run[2](_swap_cores_ex(mine))
    return [_halves(core, h, o) for h, o in zip(mine, other)]


def _adamw_math(w, g, m, v):
    m = ADAM_B1 * m + (1.0 - ADAM_B1) * g
    v = ADAM_B2 * v + (1.0 - ADAM_B2) * (g * g)
    m_hat = m / (1.0 - ADAM_B1 ** ADAM_STEP)
    v_hat = v / (1.0 - ADAM_B2 ** ADAM_STEP)
    delta = -ADAM_LR * (m_hat / (jnp.sqrt(v_hat) + ADAM_EPS) + ADAM_WD * w)
    return delta, m, v


def _adamw(name, w, g, m, v):
    r, cols = w.shape
    br = _rows(r)
    if br * cols * 4 > (1 << 20) and br % 16 == 0:
        br //= 2

    def body(w_ref, g_ref, m_ref, v_ref, d_ref, mo_ref, vo_ref):
        d_ref[...], mo_ref[...], vo_ref[...] = _adamw_math(w_ref[...], g_ref[...], m_ref[...], v_ref[...])

    spec = pl.BlockSpec((br, cols), lambda i: (i, 0))
    return pl.pallas_call(
        body, name=f"adamw_{name}", grid=(r // br,), in_specs=[spec] * 4, out_specs=[spec] * 3,
        out_shape=[_sds((r, cols), F32)] * 3,
        compiler_params=_cparams(dimension_semantics=("parallel",)),
    )(w, g, m, v)


def _ada_fwd(c_all, ada_w, ada_b_cols):
    n = c_all.shape[0]

    def body(c_ref, w_ref, b_ref, o_ref):
        o_ref[...] = _mmx(_silu(c_ref[...]), w_ref[...]) + b_ref[...]

    return pl.pallas_call(
        body, name="ada_fwd", out_shape=_sds((n, ada_w.shape[1]), F32), compiler_params=_cparams(),
    )(c_all, ada_w, ada_b_cols)


def _ada_bwd(c_all, dmod_cols, w, m, v):
    n = c_all.shape[0]
    r, cols = w.shape
    br = 128

    def body(c_ref, d_ref, w_ref, m_ref, v_ref, g_ref, dl_ref, mo_ref, vo_ref):
        cond = _silu(c_ref[...])
        g = lax.dot_general(cond, d_ref[...], (((0,), (0,)), ((), ())), precision=lax.Precision.HIGHEST,
                            preferred_element_type=F32)
        g_ref[...] = g
        dl_ref[...], mo_ref[...], vo_ref[...] = _adamw_math(w_ref[...], g, m_ref[...], v_ref[...])

    spec = pl.BlockSpec((br, cols), lambda i: (i, 0))
    return pl.pallas_call(
        body, name="ada_bwd", grid=(r // br,),
        in_specs=[pl.BlockSpec((n, br), lambda i: (0, i)), pl.BlockSpec((n, cols), lambda i: (0, 0)), spec, spec, spec],
        out_specs=[spec] * 4, out_shape=[_sds((r, cols), F32)] * 4,
        compiler_params=_cparams(dimension_semantics=("parallel",)),
    )(c_all, dmod_cols, w, m, v)


def _sum_devices(parts):
    n, r, cols = parts.shape

    def body(p_ref, o_ref):
        acc = p_ref[0]
        for k in range(1, n):
            acc = acc + p_ref[k]
        o_ref[...] = acc

    return pl.pallas_call(body, name="sum_devices", out_shape=_sds((r, cols), F32), compiler_params=_cparams())(parts)


SMALL_ROWS = 16
_SMALL_SLOTS = dict(norm1_g=(0, 0, D), norm2_g=(1, 0, D), q_norm_g=(2, 0, HD), k_norm_g=(2, 128, HD), sinks=(2, 256, HEADS),
                    a_log=(2, 384, DN_H), dt_bias=(2, 512, DN_H), dn_norm_g=(2, 640, DN_D))
_CONV_ROW = 4
_ADA_B_ROW = 8


def _pack_small(vals, conv, ada_b):
    def row(pieces):
        out, at = [], 0
        for col, val in pieces:
            out += [jnp.zeros((1, col - at), F32), val.reshape(1, -1)]
            at = col + val.size
        return jnp.concatenate(out + [jnp.zeros((1, CONVW - at), F32)], axis=1)
    rows = {}
    for name, (r, col, n) in _SMALL_SLOTS.items():
        rows.setdefault(r, []).append((col, vals[name]))
    blank = jnp.zeros((1, CONVW), F32)
    top = [row(sorted(rows[r], key=lambda p: p[0])) if r in rows else blank for r in range(_CONV_ROW)]
    conv_rows = jnp.concatenate([conv, jnp.zeros((CONV, CONVW - conv.shape[1]), F32)], axis=1)
    tail = jnp.zeros((SMALL_ROWS - _ADA_B_ROW - 4, CONVW), F32)
    return jnp.concatenate(top + [conv_rows, ada_b.reshape(4, CONVW), tail], axis=0)


def _unpack_small(sheet, conv_cols):
    out = {name: sheet[row, col:col + n].reshape(1, n) for name, (row, col, n) in _SMALL_SLOTS.items()}
    out["conv_w"] = sheet[_CONV_ROW:_CONV_ROW + CONV, 0:conv_cols].reshape(1, CONV, 1, conv_cols)
    out["ada_b"] = sheet[_ADA_B_ROW:_ADA_B_ROW + 4, :].reshape(1, 6 * D)
    return out


def _w_in_segments():
    shard = IN_WIDTH // N_CHIP
    cuts = sorted({0, IN_WIDTH, C_Z, C_Z + 2 * DN_H} | {k * shard for k in range(1, N_CHIP)})
    segs = []
    for a, b in zip(cuts[:-1], cuts[1:]):
        k = a // shard
        pad = a if a < C_Z else (C_BA + a - C_Z if a < C_Z + 2 * DN_H else a - 2 * DN_H)
        segs.append((k, a - k * shard, b - k * shard, pad))
    return segs


def _pad_w_in(f):
    parts = [f[k][:, lo:hi] for k, lo, hi, _ in sorted(_w_in_segments(), key=lambda s: s[3])]
    return jnp.concatenate(parts + [jnp.zeros((f.shape[1], IN_PAD - IN_WIDTH), f.dtype)], axis=1)


def _unpad_w_in(g):
    return jnp.stack([jnp.concatenate([g[:, pad:pad + hi - lo] for kk, lo, hi, pad in _w_in_segments() if kk == k], axis=1)
                      for k in range(N_CHIP)])


def _blocks_to_cols(f):
    return f.transpose(1, 0, 2).reshape(f.shape[1], N_CHIP * f.shape[2])


def kernel(x, c, positions, ada_w, ada_b, norm1_g, w_in, conv_w, q_norm_g, k_norm_g, sinks, a_log, dt_bias, dn_norm_g, w_branch, w_out, norm2_g, w_gate_up, w_down, loss_target, m_ada_w, m_ada_b, m_norm1_g, m_w_in, m_conv_w, m_q_norm_g, m_k_norm_g, m_sinks, m_a_log, m_dt_bias, m_dn_norm_g, m_w_branch, m_w_out, m_norm2_g, m_w_gate_up, m_w_down, v_ada_w, v_ada_b, v_norm1_g, v_w_in, v_conv_w, v_q_norm_g, v_k_norm_g, v_sinks, v_a_log, v_dt_bias, v_dn_norm_g, v_w_branch, v_w_out, v_norm2_g, v_w_gate_up, v_w_down):
    ix, iy, ic = lax.axis_index("x"), lax.axis_index("y"), lax.axis_index("c")
    dev = 4 * ix + 2 * iy + ic
    chip = 2 * ix + iy
    n_seq = x.shape[0]
    conv_cols = conv_w.shape[-1]

    c_all, conv_all = _gather_devices("gather_cond", [c, conv_w.reshape(CONV, conv_cols)], dev)
    c_all = c_all.reshape(N_DEV * n_seq, D)
    ada_cols = ada_w.shape[-1]
    ada_b_cols = lax.dynamic_slice(ada_b, (0, chip * ada_cols), (1, ada_cols))
    mod_cols = _ada_fwd(c_all, ada_w[0], ada_b_cols)
    (mod_blocks,) = _gather_chips("gather_mod", [mod_cols], chip)
    mod_all = _blocks_to_cols(mod_blocks)
    mod = lax.dynamic_slice(mod_all, (dev * n_seq, 0), (n_seq, 6 * D)).reshape(n_seq, 1, 6 * D)
    conv_full = _blocks_to_cols(conv_all[0::2])

    (f_in,) = _gather_weights("w_in", [w_in[0].astype(BF)], chip, ic)
    w_in_pad = _pad_w_in(f_in)

    loss, grad_x, dmod, small, (g_in, g_br, g_out, r_gu, r_dn) = _local_step(
        x, mod, positions, loss_target, norm1_g, w_in_pad, conv_full.reshape(CONV, 1, CONVW), q_norm_g, k_norm_g, sinks,
        a_log, dt_bias, dn_norm_g, w_branch[0].astype(BF), w_out[0].astype(BF), norm2_g, w_gate_up[0].astype(BF),
        w_down[0].astype(BF), dist=(chip, ic))
    loss = lax.psum(loss, ("x", "y", "c"))

    blocks = [_unpad_w_in(g_in), g_br.reshape(N_CHIP, -1, D), g_out.reshape(N_CHIP, -1, D)]
    r_in, r_br, r_out = _reduce_grads(("w_in", "w_branch", "w_out"), blocks, chip, ic)
    big = {}
    for name, w, g, m, v in (("w_in", w_in, r_in, m_w_in, v_w_in), ("w_branch", w_branch, r_br, m_w_branch, v_w_branch),
                             ("w_out", w_out, r_out, m_w_out, v_w_out), ("w_gate_up", w_gate_up, r_gu, m_w_gate_up, v_w_gate_up),
                             ("w_down", w_down, r_dn, m_w_down, v_w_down)):
        big[name] = (g,) + tuple(_adamw(name, w[0], g, m[0], v[0]))

    part = _pack_small(small, small["conv_w"], jnp.sum(dmod, axis=(0, 1)).reshape(1, 6 * D))
    dmod_all, parts = _gather_devices("gather_small", [dmod.reshape(n_seq, 6 * D), part], dev)
    dmod_all = dmod_all.reshape(N_DEV * n_seq, 6 * D)
    g_small = _unpack_small(_sum_devices(parts), CONVW)
    g_conv = lax.dynamic_slice(g_small["conv_w"].reshape(CONV, CONVW), (0, chip * conv_cols), (CONV, conv_cols))
    g_small["conv_w"] = g_conv.reshape(1, CONV, 1, conv_cols)

    given = dict(norm1_g=(norm1_g, m_norm1_g, v_norm1_g), norm2_g=(norm2_g, m_norm2_g, v_norm2_g),
                 q_norm_g=(q_norm_g, m_q_norm_g, v_q_norm_g), k_norm_g=(k_norm_g, m_k_norm_g, v_k_norm_g),
                 sinks=(sinks, m_sinks, v_sinks), a_log=(a_log, m_a_log, v_a_log), dt_bias=(dt_bias, m_dt_bias, v_dt_bias),
                 dn_norm_g=(dn_norm_g, m_dn_norm_g, v_dn_norm_g))
    sheets = [_pack_small({k: t[j] for k, t in given.items()}, cw.reshape(CONV, conv_cols), ab)
              for j, (cw, ab) in enumerate(((conv_w, ada_b), (m_conv_w, m_ada_b), (v_conv_w, v_ada_b)))]
    g_local = _pack_small(g_small, g_conv, g_small["ada_b"])
    upd = [_unpack_small(s, conv_cols) for s in _adamw("small", sheets[0], g_local, sheets[1], sheets[2])]

    dmod_cols = lax.dynamic_slice(dmod_all, (0, chip * ada_cols), (N_DEV * n_seq, ada_cols))
    ada = _ada_bwd(c_all, dmod_cols, ada_w[0], m_ada_w[0], v_ada_w[0])

    names = ["ada_w", "ada_b", "norm1_g", "w_in", "conv_w", "q_norm_g", "k_norm_g", "sinks", "a_log", "dt_bias", "dn_norm_g",
             "w_branch", "w_out", "norm2_g", "w_gate_up", "w_down"]

    def leaf(name, j):
        if name == "ada_w":
            return ada[j][None]
        if name in big:
            return big[name][j][None]
        return g_small[name] if j == 0 else upd[j - 1][name]

    return (loss, grad_x) + tuple(leaf(n, j) for j in range(4) for n in names)
```

```python
import functools
from typing import Callable, NamedTuple

import jax
import jax.numpy as jnp
import numpy as np
from jax import lax
from jax.experimental import pallas as pl
from jax.experimental.pallas import tpu as pltpu

F32 = jnp.float32
BF = jnp.bfloat16

D = 1024
HEADS = 8
KV_HEADS = 2
GROUP = 4
HD = 64
BLK = 128
ROT = 16
THETA = 500000.0
QW = 512
KVW = 128
DN_H = 4
DN_D = 128
CONV = 4
CHUNK = 64
DNW = 512
CONVW = 1536
FFN = 2816
EPS = 1e-6
IN_WIDTH = 4872
IN_PAD = 4992
C_KV = 512
C_DN = 768
C_Z = 2304
C_GA = 2816
C_GB = 3840
C_BA = 4864
NEG = -1e30
N_DEV = 8
N_CHIP = 4

ADAM_LR = 0.001
ADAM_B1 = 0.9
ADAM_B2 = 0.999
ADAM_EPS = 1e-08
ADAM_WD = 0.01
ADAM_STEP = 10

VMEM_LIMIT = 60 * 1024 * 1024


def _cparams(**kw):
    return pltpu.CompilerParams(vmem_limit_bytes=VMEM_LIMIT, **kw)


def _dg(a, b, ca, cb):
    return lax.dot_general(a.astype(BF), b.astype(BF), (((ca,), (cb,)), ((), ())),
                           preferred_element_type=F32)


@jax.custom_vjp
def _mm(a, b):
    return _dg(a, b, 1, 0)


def _mm_fwd(a, b):
    return _dg(a, b, 1, 0), (a, b)


def _mm_bwd(res, dy):
    a, b = res
    return _dg(dy, b, 1, 1).astype(a.dtype), _dg(a, dy, 0, 0).astype(b.dtype)


_mm.defvjp(_mm_fwd, _mm_bwd)


@jax.custom_vjp
def _mm_nt(a, b):
    return _dg(a, b, 1, 1)


def _mm_nt_fwd(a, b):
    return _dg(a, b, 1, 1), (a, b)


def _mm_nt_bwd(res, dy):
    a, b = res
    return _dg(dy, b, 1, 0).astype(a.dtype), _dg(dy, a, 0, 0).astype(b.dtype)


_mm_nt.defvjp(_mm_nt_fwd, _mm_nt_bwd)


@jax.custom_vjp
def _mm_tn(a, b):
    return _dg(a, b, 0, 0)


def _mm_tn_fwd(a, b):
    return _dg(a, b, 0, 0), (a, b)


def _mm_tn_bwd(res, dy):
    a, b = res
    return _dg(b, dy, 1, 1).astype(a.dtype), _dg(a, dy, 1, 0).astype(b.dtype)


_mm_tn.defvjp(_mm_tn_fwd, _mm_tn_bwd)


def _mmx(a, b):
    return jnp.dot(a, b, precision=lax.Precision.HIGHEST, preferred_element_type=F32)


def _mmx_nt(a, b):
    return lax.dot_general(a, b, (((1,), (1,)), ((), ())), precision=lax.Precision.HIGHEST,
                           preferred_element_type=F32)


def _iota(shape, dim):
    return lax.broadcasted_iota(jnp.int32, shape, dim)


def _sigmoid(x):
    return 1.0 / (1.0 + jnp.exp(-x))


def _silu(x):
    return x * _sigmoid(x)


def _softplus(x):
    return jnp.maximum(x, 0.0) + jnp.log(1.0 + jnp.exp(-jnp.abs(x)))


def _rms(x, gain):
    return x * lax.rsqrt(jnp.mean(x * x, axis=-1, keepdims=True) + EPS) * gain


def _norm_mod(x, gain, shift, scale):
    return _rms(x, gain) * (1.0 + scale) + shift


def _split(a):
    hi = a.astype(BF)
    return hi, (a - hi.astype(F32)).astype(BF)


def _dg3(a, b, ca, cb):
    ah, al = _split(a)
    bh, bl = _split(b)

    def dg(x, y):
        return lax.dot_general(x, y, (((ca,), (cb,)), ((), ())), preferred_element_type=F32)
    return dg(ah, bh) + (dg(ah, bl) + dg(al, bh))


@jax.custom_vjp
def _mm3(a, b):
    return _dg3(a, b, 1, 0)


def _mm3_fwd(a, b):
    return _dg3(a, b, 1, 0), (a, b)


def _mm3_bwd(res, dy):
    a, b = res
    return _dg3(dy, b, 1, 1), _dg3(a, dy, 0, 0)


_mm3.defvjp(_mm3_fwd, _mm3_bwd)


def _qk_prep(slabs, gain, cos, sin):
    r = _iota((2 * HD, 2 * HD), 0)
    c = _iota((2 * HD, 2 * HD), 1)
    seg = jnp.where(r // HD == c // HD, 1.0 / HD, 0.0).astype(F32)
    half = ROT // 2
    cd = c % HD
    pair = jnp.where(((cd < half) & (r == c + half)) | ((cd >= half) & (cd < ROT) & (r == c - half)), 1.0, 0.0).astype(F32)
    out = []
    for x in slabs:
        y = x * lax.rsqrt(_mm3(x * x, seg) + EPS) * gain
        out.append(y * cos + _mm3(y, pair) * sin)
    return out


def _attn_block(qs, kc, kp, vc, vp, sinks, has_prev):
    rows = GROUP * BLK
    qi = _iota((rows, 2 * BLK), 0) % BLK + BLK
    kj = _iota((rows, 2 * BLK), 1)
    dist = qi - kj
    valid = (dist >= 0) & (dist < BLK) & ((kj >= BLK) | has_prev)
    grp = _iota((rows, HEADS), 0) // BLK
    col = _iota((rows, HEADS), 1)

    def per_kv_head(f):
        return jnp.concatenate([f(h)[None] for h in range(KV_HEADS)], axis=0)
    q = per_kv_head(lambda h: jnp.concatenate([qs[h * GROUP + g] for g in range(GROUP)], axis=0))
    k = per_kv_head(lambda h: jnp.concatenate([kp[h], kc[h]], axis=0))
    v = per_kv_head(lambda h: jnp.concatenate([vp[h], vc[h]], axis=0))
    sink = per_kv_head(lambda h: jnp.sum(jnp.where(col == h * GROUP + grp, sinks, 0.0), axis=-1, keepdims=True))
    s = jnp.where(valid[None], _bmm_nt(q, k) * (HD ** -0.5), NEG)
    m = lax.stop_gradient(jnp.maximum(jnp.max(s, axis=-1, keepdims=True), sink))
    p = jnp.exp(s - m)
    probs = p / (jnp.sum(p, axis=-1, keepdims=True) + jnp.exp(sink - m))
    o = _bmm(probs, v)
    return [o[h, g * BLK:(g + 1) * BLK] for h in range(KV_HEADS) for g in range(GROUP)]


def _dn_tail(ys, ba, alog, dtb):
    def l2(t):
        return t * lax.rsqrt(jnp.sum(t * t, axis=-1, keepdims=True) + EPS)
    s = [_silu(y) for y in ys]
    out = [l2(t) for t in s[:2 * DN_H]] + s[2 * DN_H:]
    lane = _iota(ba.shape, 1)
    beta = _sigmoid(ba)
    g = -jnp.exp(alog) * _softplus(ba + dtb)
    bg = jnp.where(lane < DN_H, beta, jnp.where(lane < 2 * DN_H, g, 0.0))
    return out, bg


def _bdg(a, b, ca, cb):
    return lax.dot_general(a.astype(BF), b.astype(BF), (((ca,), (cb,)), ((0,), (0,))), preferred_element_type=F32)


@jax.custom_vjp
def _bmm(a, b):
    return _bdg(a, b, 2, 1)


def _bmm_fwd(a, b):
    return _bdg(a, b, 2, 1), (a, b)


def _bmm_bwd(res, dy):
    a, b = res
    return _bdg(dy, b, 2, 2), _bdg(a, dy, 1, 1)


_bmm.defvjp(_bmm_fwd, _bmm_bwd)


@jax.custom_vjp
def _bmm_nt(a, b):
    return _bdg(a, b, 2, 2)


def _bmm_nt_fwd(a, b):
    return _bdg(a, b, 2, 2), (a, b)


def _bmm_nt_bwd(res, dy):
    a, b = res
    return _bdg(dy, b, 2, 1), _bdg(dy, a, 1, 1)


_bmm_nt.defvjp(_bmm_nt_fwd, _bmm_nt_bwd)


def _bmmx(a, b):
    return lax.dot_general(a, b, (((2,), (1,)), ((0,), (0,))), precision=lax.Precision.HIGHEST,
                           preferred_element_type=F32)


def _neumann_inverse(lmat):
    C = CHUNK
    eye = jnp.where(_iota((C, C), 0) == _iota((C, C), 1), 1.0, 0.0).astype(F32)[None]
    a = -lmat
    tinv = eye + a
    pw = _bmmx(a, a)
    for _ in range(4):
        both = _bmmx(jnp.concatenate([pw, tinv], axis=1), pw)
        pw, tinv = both[:, :C], tinv + both[:, C:]
    return tinv + _bmmx(tinv, pw)


def _inverse_bwd(tinv, d_tinv):
    x = lax.dot_general(d_tinv, tinv, (((2,), (2,)), ((0,), (0,))), precision=lax.Precision.HIGHEST,
                        preferred_element_type=F32)
    return -lax.dot_general(tinv, x, (((1,), (1,)), ((0,), (0,))), precision=lax.Precision.HIGHEST,
                            preferred_element_type=F32)


@jax.custom_vjp
def _tri_inverse(lmat):
    return _neumann_inverse(lmat)


def _tri_inverse_fwd(lmat):
    tinv = _neumann_inverse(lmat)
    return tinv, tinv


def _tri_inverse_bwd(tinv, d_tinv):
    return (_inverse_bwd(tinv, d_tinv),)


_tri_inverse.defvjp(_tri_inverse_fwd, _tri_inverse_bwd)


@jax.custom_vjp
def _tri_inverse_known(lmat, tinv):
    return tinv


def _tri_inverse_known_fwd(lmat, tinv):
    return tinv, tinv


def _tri_inverse_known_bwd(tinv, d_tinv):
    return _inverse_bwd(tinv, d_tinv), jnp.zeros_like(tinv)


_tri_inverse_known.defvjp(_tri_inverse_known_fwd, _tri_inverse_known_bwd)


def _dn_intra(q, k, v, bg, tinv=None):
    C = CHUNK
    G = bg.shape[0]
    r = _iota((C, C), 0)
    c = _iota((C, C), 1)
    incl = (r >= c)[None]
    strict = (r > c)[None]
    eye = jnp.where(r == c, 1.0, 0.0).astype(F32)[None]
    tri = jnp.broadcast_to(jnp.where(r >= c, 1.0, 0.0).astype(F32)[None], (G, C, C))
    gc_all = _bmmx(tri, bg)
    lane = _iota((C, DN_D), 1)

    def per_head(x, offset):
        return jnp.concatenate([jnp.sum(jnp.where(lane == offset + h, x[g], 0.0), axis=-1, keepdims=True)[None]
                                for g in range(G) for h in range(DN_H)], axis=0)
    beta = per_head(bg, 0)
    gcol = per_head(gc_all, DN_H)
    grow = jnp.sum(eye * gcol, axis=1, keepdims=True)
    glast = jnp.sum(jnp.where(_iota((1, C, 1), 1) == C - 1, gcol, 0.0), axis=1, keepdims=True)
    decay = jnp.exp(jnp.where(incl, gcol - grow, NEG))
    q = q * (DN_D ** -0.5)
    kb = k * beta
    lmat = jnp.where(strict, _bmm_nt(kb, k) * decay, 0.0)
    tinv = _tri_inverse(lmat) if tinv is None else _tri_inverse_known(lmat, tinv)
    egc = jnp.exp(gcol)
    u = _bmm(tinv, v * beta)
    w = _bmm(tinv, kb * egc)
    a = _bmm_nt(q, k) * decay
    return u, w, q * egc, k * jnp.exp(glast - gcol), a, jnp.exp(glast), tinv


@jax.custom_vjp
def _bmm_tn(a, b):
    return _bdg(a, b, 1, 1)


def _bmm_tn_fwd(a, b):
    return _bdg(a, b, 1, 1), (a, b)


def _bmm_tn_bwd(res, dy):
    a, b = res
    return _bdg(b, dy, 2, 2), _bdg(a, dy, 2, 1)


_bmm_tn.defvjp(_bmm_tn_fwd, _bmm_tn_bwd)


def _dn_rec(state, u, w, qd, kd, a, cd):
    v_new = u - _bmm(w, state)
    out = _bmm(qd, state) + _bmm(a, v_new)
    return state * cd + _bmm_tn(kd, v_new), out


def _mix_tile(o_attn, o_raw, zs, ga, gb, x, gate1, dn_g, wb_a, wb_d, w_out, p_ya, p_yd, p_out):
    o_dn = jnp.concatenate([_rms(o_raw[h], dn_g) * _silu(zs[h]) for h in range(DN_H)], axis=-1)
    y_a = _mm(o_attn, wb_a) + p_ya
    y_d = _mm(o_dn, wb_d) + p_yd
    merged = _sigmoid(ga) * y_a + _sigmoid(gb) * y_d
    out = _mm(merged, w_out) + p_out
    return x + gate1 * out, o_dn, merged


def _mlp_tile(x1, gain, shift, scale, gate2, w_gu, w_dn, tgt, p_gu, p_yy):
    h2 = _norm_mod(x1, gain, shift, scale)
    gu = jnp.concatenate([_mm(h2, w) for w in w_gu], axis=-1) + p_gu
    act = _silu(gu[:, :FFN]) * gu[:, FFN:]
    yy = _mm(act, w_dn) + p_yy
    y = x1 + gate2 * yy
    err = y - tgt
    return 0.5 * jnp.sum(err * err) * (1.0 / D), (h2, act)


def _tok(bt, f):
    return pl.BlockSpec((None, bt, f), lambda b, i: (b, i, 0))


def _full(shape):
    return pl.BlockSpec(shape, lambda b, i: (0,) * len(shape))


def _resident(shape):
    return pl.BlockSpec(shape, lambda b, i: (0,) * len(shape), pipeline_mode=pl.Buffered(1))


def _per_batch(f):
    return pl.BlockSpec((None, 1, f), lambda b, i: (b, 0, 0))


def _sds(shape, dtype):
    return jax.ShapeDtypeStruct(shape, dtype)


class _Exchange(NamedTuple):
    ins: tuple
    out_shapes: tuple
    n_remote: int
    plan: Callable


def _exchange_copies(ex, in_refs, out_refs, send_sems, recv_sems):
    remote = ex.plan((lax.axis_index("x"), lax.axis_index("y"), lax.axis_index("c")), in_refs, out_refs)
    assert len(remote) == ex.n_remote
    return [pltpu.make_async_remote_copy(src_ref=src, dst_ref=dst, send_sem=send_sems.at[i], recv_sem=recv_sems.at[i],
                                         device_id=peer, device_id_type=pl.DeviceIdType.MESH)
            for i, (src, dst, peer) in enumerate(remote)]


def _hosted_call(body, name, grid, in_specs, out_specs, out_shape, scratch_shapes, semantics, ins, ex=None):
    if ex is None:
        outs = pl.pallas_call(body, name=name, grid=grid, in_specs=in_specs, out_specs=out_specs, out_shape=out_shape,
                              scratch_shapes=scratch_shapes,
                              compiler_params=_cparams(dimension_semantics=semantics))(*ins)
        return outs, ()
    n_in, n_out, n_scr = len(ins), len(out_shape), len(scratch_shapes)
    c_in, c_out = len(ex.ins), len(ex.out_shapes)
    steps = 1
    for g in grid:
        steps *= g

    def wrapped(*refs):
        a, b, c, d = n_in, n_in + c_in, n_in + c_in + n_out, n_in + c_in + n_out + c_out
        scratch, (send_sems, recv_sems) = refs[d:d + n_scr], refs[d + n_scr:]
        step = 0
        for axis, g in enumerate(grid):
            step = step * g + pl.program_id(axis)

        @pl.when(step == 0)
        def _():
            for cp in _exchange_copies(ex, refs[a:b], refs[c:d], send_sems, recv_sems):
                cp.start()
        body(*refs[:a], *refs[b:c], *scratch)

        @pl.when(step == steps - 1)
        def _():
            cps = _exchange_copies(ex, refs[a:b], refs[c:d], send_sems, recv_sems)
            for cp in cps:
                cp.wait_recv()
            for cp in cps:
                cp.wait_send()

    any_spec = pl.BlockSpec(memory_space=pl.ANY)
    res = pl.pallas_call(
        wrapped, name=name, grid=grid, in_specs=list(in_specs) + [any_spec] * c_in,
        out_specs=list(out_specs) + [any_spec] * c_out, out_shape=list(out_shape) + list(ex.out_shapes),
        scratch_shapes=list(scratch_shapes) + [pltpu.SemaphoreType.DMA((ex.n_remote,)), pltpu.SemaphoreType.DMA((ex.n_remote,))],
        compiler_params=_cparams(dimension_semantics=("arbitrary",) * len(grid)),
    )(*ins, *ex.ins)
    return res[:n_out], res[n_out:]


def _acc(ref, val, first):
    @pl.when(first)
    def _():
        ref[...] = val

    @pl.when(jnp.logical_not(first))
    def _():
        ref[...] += val


def _in_proj(x, mod, norm1_g, w_in, bt):
    B, S, _ = x.shape

    def body(x_ref, mod_ref, g_ref, w_ref, q_ref, kv_ref, dn_ref, z_ref, ga_ref, gb_ref, ba_ref, h_ref):
        h = _norm_mod(x_ref[...], g_ref[...], mod_ref[:, 0:D], mod_ref[:, D:2 * D]).astype(BF)
        h_ref[...] = h

        def proj(c0, c1):
            return jnp.dot(h, w_ref[:, c0:c1], preferred_element_type=F32)
        q_ref[...] = proj(0, C_KV).astype(BF)
        kv_ref[...] = proj(C_KV, C_DN).astype(BF)
        dn_ref[...] = proj(C_DN, C_Z).astype(BF)
        z_ref[...] = proj(C_Z, C_GA).astype(BF)
        ga_ref[...] = proj(C_GA, C_GB).astype(BF)
        gb_ref[...] = proj(C_GB, C_BA).astype(BF)
        ba_ref[...] = proj(C_BA, IN_PAD)

    widths = (QW, 2 * KVW, CONVW, DNW, D, D)
    return pl.pallas_call(
        body, name="in_proj", grid=(B, S // bt),
        in_specs=[_tok(bt, D), _per_batch(6 * D), _full((1, D)), _full((D, IN_PAD))],
        out_specs=[_tok(bt, w) for w in widths] + [_tok(bt, 128), _tok(bt, D)],
        out_shape=[_sds((B, S, w), BF) for w in widths] + [_sds((B, S, 128), F32), _sds((B, S, D), BF)],
        compiler_params=_cparams(dimension_semantics=("parallel", "parallel")),
    )(x, mod, norm1_g, w_in)


def _prev_blk(bt, f):
    return pl.BlockSpec((None, bt, f), lambda b, i: (b, jnp.maximum(i - 1, 0), 0))


QKV = QW + 2 * KVW


def _qk_slabs(q_ref, kv_ref):
    return ([q_ref[:, j * 2 * HD:(j + 1) * 2 * HD].astype(F32) for j in range(QW // (2 * HD))],
            [kv_ref[:, 0:KVW].astype(F32)])


def _qk_prep_fwd(q, kv, cos, sin, qg, kg, bt):
    B, S, _ = q.shape

    def body(q_ref, kv_ref, cos_ref, sin_ref, qg_ref, kg_ref, o_ref):
        qs, ks = _qk_slabs(q_ref, kv_ref)
        qn = _qk_prep(qs, qg_ref[...], cos_ref[...], sin_ref[...])
        kn = _qk_prep(ks, kg_ref[...], cos_ref[...], sin_ref[...])
        for j, t in enumerate(qn + kn):
            o_ref[:, j * 2 * HD:(j + 1) * 2 * HD] = t.astype(BF)
        o_ref[:, QW + KVW:QKV] = kv_ref[:, KVW:2 * KVW]

    return pl.pallas_call(
        body, name="qk_prep_fwd", grid=(B, S // bt),
        in_specs=[_tok(bt, QW), _tok(bt, 2 * KVW), _tok(bt, 2 * HD), _tok(bt, 2 * HD), _full((1, 2 * HD)), _full((1, 2 * HD))],
        out_specs=_tok(bt, QKV), out_shape=_sds((B, S, QKV), BF),
        compiler_params=_cparams(dimension_semantics=("parallel", "parallel")),
    )(q, kv, cos, sin, qg, kg)


def _qk_prep_bwd(q, kv, cos, sin, qg, kg, dqn, dkvn, bt):
    B, S, _ = q.shape

    def body(q_ref, kv_ref, cos_ref, sin_ref, qg_ref, kg_ref, dqn_ref, dkvn_ref, dq_ref, dkv_ref, dqg_ref, dkg_ref):
        qs, ks = _qk_slabs(q_ref, kv_ref)
        cos, sin = cos_ref[...], sin_ref[...]

        def f(qs, ks, qg, kg):
            return _qk_prep(qs, qg, cos, sin), _qk_prep(ks, kg, cos, sin)
        _, vjp = jax.vjp(f, qs, ks, qg_ref[...], kg_ref[...])
        n_q = len(qs)
        d_q = [dqn_ref[:, j * 2 * HD:(j + 1) * 2 * HD].astype(F32) for j in range(n_q)]
        d_k = [dkvn_ref[:, 0:KVW].astype(F32)]
        dqs, dks, dqg, dkg = vjp((d_q, d_k))
        for j in range(n_q):
            dq_ref[:, j * 2 * HD:(j + 1) * 2 * HD] = dqs[j].astype(BF)
        dkv_ref[:, 0:KVW] = dks[0].astype(BF)
        dkv_ref[:, KVW:2 * KVW] = dkvn_ref[:, KVW:2 * KVW]
        first = (pl.program_id(0) == 0) & (pl.program_id(1) == 0)
        _acc(dqg_ref, dqg, first)
        _acc(dkg_ref, dkg, first)

    return pl.pallas_call(
        body, name="qk_prep_bwd", grid=(B, S // bt),
        in_specs=[_tok(bt, QW), _tok(bt, 2 * KVW), _tok(bt, 2 * HD), _tok(bt, 2 * HD), _full((1, 2 * HD)), _full((1, 2 * HD)),
                  _tok(bt, QW), _tok(bt, 2 * KVW)],
        out_specs=[_tok(bt, QW), _tok(bt, 2 * KVW), _full((1, 2 * HD)), _full((1, 2 * HD))],
        out_shape=[_sds((B, S, QW), BF), _sds((B, S, 2 * KVW), BF), _sds((1, 2 * HD), F32), _sds((1, 2 * HD), F32)],
        compiler_params=_cparams(dimension_semantics=("arbitrary", "arbitrary")),
    )(q, kv, cos, sin, qg, kg, dqn, dkvn)


def _attn_load(qkv_ref, kvp_ref):
    qs = [qkv_ref[:, h * HD:(h + 1) * HD].astype(F32) for h in range(HEADS)]
    kc = [qkv_ref[:, QW + h * HD:QW + (h + 1) * HD].astype(F32) for h in range(KV_HEADS)]
    vc = [qkv_ref[:, QW + KVW + h * HD:QW + KVW + (h + 1) * HD].astype(F32) for h in range(KV_HEADS)]
    kp = [kvp_ref[:, h * HD:(h + 1) * HD].astype(F32) for h in range(KV_HEADS)]
    vp = [kvp_ref[:, KVW + h * HD:KVW + (h + 1) * HD].astype(F32) for h in range(KV_HEADS)]
    return qs, kc, kp, vc, vp


def _kv_prev_spec(index):
    return pl.BlockSpec((None, BLK, 2 * KVW), lambda b, i: (b, index(i), QW // (2 * KVW)))


def _attn_fwd(qkv, sinks):
    B, S, _ = qkv.shape

    def body(qkv_ref, kvp_ref, sk_ref, o_ref):
        qs, kc, kp, vc, vp = _attn_load(qkv_ref, kvp_ref)
        outs = _attn_block(qs, kc, kp, vc, vp, sk_ref[...], pl.program_id(1) > 0)
        for h in range(HEADS):
            o_ref[:, h * HD:(h + 1) * HD] = outs[h].astype(BF)

    return pl.pallas_call(
        body, name="attn_fwd", grid=(B, S // BLK),
        in_specs=[_tok(BLK, QKV), _kv_prev_spec(lambda i: jnp.maximum(i - 1, 0)), _full((1, HEADS))],
        out_specs=_tok(BLK, QW), out_shape=_sds((B, S, QW), BF),
        compiler_params=_cparams(dimension_semantics=("parallel", "parallel")),
    )(qkv, qkv, sinks)


def _conv_fwd_tile(xe_ref, x_ref, halo_ref, cw_ref, first, bt):
    halo = halo_ref[...].astype(F32)
    xe_ref[0:8, :] = jnp.where(first, 0.0, halo)
    xe_ref[8:bt + 8, :] = x_ref[...].astype(F32)
    y = cw_ref[0:1, :] * xe_ref[5:bt + 5, :]
    for j in range(1, CONV):
        y = y + cw_ref[j:j + 1, :] * xe_ref[5 + j:bt + 5 + j, :]
    return y


def _halo_spec(bt):
    return pl.BlockSpec((None, 8, CONVW), lambda b, i: (b, jnp.maximum(i * (bt // 8) - 1, 0), 0))


def _dn_prep(dn, ba, conv_w, alog, dtb, bt):
    B, S, _ = dn.shape

    def body(x_ref, halo_ref, ba_ref, cw_ref, al_ref, dt_ref, qkv_ref, bg_ref, y_ref, xe_ref):
        y = _conv_fwd_tile(xe_ref, x_ref, halo_ref, cw_ref, pl.program_id(1) == 0, bt)
        y_ref[...] = y.astype(BF)
        ys = [y[:, j * DN_D:(j + 1) * DN_D] for j in range(3 * DN_H)]
        out, bg = _dn_tail(ys, ba_ref[...], al_ref[...], dt_ref[...])
        for j in range(3 * DN_H):
            qkv_ref[:, j * DN_D:(j + 1) * DN_D] = out[j]
        bg_ref[...] = bg

    return pl.pallas_call(
        body, name="dn_prep", grid=(B, S // bt),
        in_specs=[_tok(bt, CONVW), _halo_spec(bt), _tok(bt, 128), _full((CONV, CONVW)), _full((1, 128)), _full((1, 128))],
        out_specs=[_tok(bt, CONVW), _tok(bt, 128), _tok(bt, CONVW)],
        out_shape=[_sds((B, S, CONVW), F32), _sds((B, S, 128), F32), _sds((B, S, CONVW), BF)],
        scratch_shapes=[pltpu.VMEM((bt + 8, CONVW), F32)],
        compiler_params=_cparams(dimension_semantics=("parallel", "arbitrary")),
    )(dn, dn, ba, conv_w, alog, dtb)


def _dn_load(qkv_ref):
    qs = [qkv_ref[:, h * DN_D:(h + 1) * DN_D] for h in range(DN_H)]
    ks = [qkv_ref[:, DNW + h * DN_D:DNW + (h + 1) * DN_D] for h in range(DN_H)]
    vs = [qkv_ref[:, 2 * DNW + h * DN_D:2 * DNW + (h + 1) * DN_D] for h in range(DN_H)]
    return qs, ks, vs


DN_GROUP = 4
AW = DN_H * CHUNK


def _stack_heads(ref, G, offset, width):
    return jnp.stack([ref[g * CHUNK:(g + 1) * CHUNK, offset + h * width:offset + (h + 1) * width]
                      for g in range(G) for h in range(DN_H)])


def _dn_load_stack(qkv_ref, G):
    return tuple(_stack_heads(qkv_ref, G, j * DNW, DN_D) for j in range(3))


def _cd_spec(n):
    return pl.BlockSpec((None, n, 1, DN_D), lambda b, i: (b, i, 0, 0))


def _dn_intra_fwd(qkv, bg, ex=None):
    B, S, _ = qkv.shape
    nc = S // CHUNK
    G = min(DN_GROUP, nc)
    rows = G * CHUNK

    def body(qkv_ref, bg_ref, u_ref, w_ref, qd_ref, kd_ref, a_ref, cd_ref, t_ref):
        q, k, v = _dn_load_stack(qkv_ref, G)
        u, w, qd, kd, a, cd, tinv = _dn_intra(q, k, v, bg_ref[...].reshape(G, CHUNK, DN_D))
        lane_row = _iota((1, DN_D), 1)
        for g in range(G):
            rows = slice(g * CHUNK, (g + 1) * CHUNK)
            cd_row = jnp.zeros((1, DN_D), F32)
            for h in range(DN_H):
                n = g * DN_H + h
                cols = slice(h * DN_D, (h + 1) * DN_D)
                u_ref[rows, cols] = u[n]
                w_ref[rows, cols] = w[n].astype(BF)
                qd_ref[rows, cols] = qd[n].astype(BF)
                kd_ref[rows, cols] = kd[n].astype(BF)
                a_ref[rows, h * CHUNK:(h + 1) * CHUNK] = a[n].astype(BF)
                t_ref[rows, h * CHUNK:(h + 1) * CHUNK] = tinv[n]
                cd_row = cd_row + jnp.where(lane_row == h, cd[n], 0.0)
            cd_ref[g] = cd_row

    return _hosted_call(
        body, "dn_intra_fwd", (B, nc // G),
        in_specs=[_tok(rows, CONVW), _tok(rows, 128)],
        out_specs=[_tok(rows, DNW)] * 4 + [_tok(rows, AW), _cd_spec(G), _tok(rows, AW)],
        out_shape=[_sds((B, S, DNW), F32)] + [_sds((B, S, DNW), BF)] * 3 + [_sds((B, S, AW), BF), _sds((B, nc, 1, DN_D), F32),
                                                                            _sds((B, S, AW), F32)],
        scratch_shapes=[], semantics=("parallel", "parallel"), ins=(qkv, bg), ex=ex)


def _rec_stack(ref, B, width):
    return jnp.stack([ref[b, :, h * width:(h + 1) * width].astype(F32) for b in range(B) for h in range(DN_H)])


def _rec_load(B, u_ref, w_ref, qd_ref, kd_ref, a_ref, cd_ref):
    lane_row = _iota((1, DN_D), 1)
    cd = jnp.stack([jnp.sum(jnp.where(lane_row == h, cd_ref[b, 0], 0.0), axis=-1, keepdims=True)
                    for b in range(B) for h in range(DN_H)])
    return (_rec_stack(u_ref, B, DN_D), _rec_stack(w_ref, B, DN_D), _rec_stack(qd_ref, B, DN_D),
            _rec_stack(kd_ref, B, DN_D), _rec_stack(a_ref, B, CHUNK), cd)


def _rec_store(B, ref, val, width):
    for b in range(B):
        for h in range(DN_H):
            ref[b, :, h * width:(h + 1) * width] = val[b * DN_H + h]


def _rec_specs(B, index):
    def tok(f):
        return pl.BlockSpec((B, CHUNK, f), lambda i: (0, index(i), 0))
    cd = pl.BlockSpec((B, 1, 1, DN_D), lambda i: (0, index(i), 0, 0))
    st = pl.BlockSpec((B, None, DN_H, DN_D, DN_D), lambda i: (0, index(i), 0, 0, 0))
    return tok, cd, st


def _dn_rec_fwd(u, w, qd, kd, a, cd, ex=None):
    B, S, _ = u.shape
    nc = S // CHUNK
    tok, cd_spec, st_spec = _rec_specs(B, lambda i: i)

    def body(u_ref, w_ref, qd_ref, kd_ref, a_ref, cd_ref, o_ref, st_ref, s_ref):
        @pl.when(pl.program_id(0) == 0)
        def _():
            s_ref[...] = jnp.zeros_like(s_ref)
        state = s_ref[...]
        st_ref[...] = state.reshape(B, DN_H, DN_D, DN_D)
        new_state, out = _dn_rec(state, *_rec_load(B, u_ref, w_ref, qd_ref, kd_ref, a_ref, cd_ref))
        s_ref[...] = new_state
        _rec_store(B, o_ref, out, DN_D)

    return _hosted_call(
        body, "dn_rec_fwd", (nc,),
        in_specs=[tok(DNW)] * 4 + [tok(AW), cd_spec],
        out_specs=[tok(DNW), st_spec],
        out_shape=[_sds((B, S, DNW), F32), _sds((B, nc, DN_H, DN_D, DN_D), F32)],
        scratch_shapes=[pltpu.VMEM((B * DN_H, DN_D, DN_D), F32)],
        semantics=("arbitrary",), ins=(u, w, qd, kd, a, cd), ex=ex)


def _mix_load(oa_ref, or_ref, z_ref):
    o_raw = [or_ref[:, h * DN_D:(h + 1) * DN_D] for h in range(DN_H)]
    zs = [z_ref[:, h * DN_D:(h + 1) * DN_D].astype(F32) for h in range(DN_H)]
    return oa_ref[...].astype(F32), o_raw, zs


def _mix_fwd(o_attn, o_raw, z, ga, gb, x, mod, dn_g, w_branch, w_out, bt):
    B, S, _ = x.shape

    def body(oa_ref, or_ref, z_ref, ga_ref, gb_ref, x_ref, mod_ref, dg_ref, wb_ref, wo_ref, x1_ref, od_ref, mg_ref):
        oa, o_r, zs = _mix_load(oa_ref, or_ref, z_ref)
        x1, o_dn, merged = _mix_tile(oa, o_r, zs, ga_ref[...].astype(F32), gb_ref[...].astype(F32), x_ref[...],
                                     mod_ref[:, 2 * D:3 * D], dg_ref[...], wb_ref[0:QW, :], wb_ref[QW:2 * QW, :],
                                     wo_ref[...], 0.0, 0.0, 0.0)
        x1_ref[...] = x1
        od_ref[...] = o_dn.astype(BF)
        mg_ref[...] = merged.astype(BF)

    return pl.pallas_call(
        body, name="mix_fwd", grid=(B, S // bt),
        in_specs=[_tok(bt, QW), _tok(bt, DNW), _tok(bt, DNW), _tok(bt, D), _tok(bt, D), _tok(bt, D), _per_batch(6 * D),
                  _full((1, DN_D)), _full((D, D)), _full((D, D))],
        out_specs=[_tok(bt, D), _tok(bt, DNW), _tok(bt, D)],
        out_shape=[_sds((B, S, D), F32), _sds((B, S, DNW), BF), _sds((B, S, D), BF)],
        compiler_params=_cparams(dimension_semantics=("parallel", "parallel")),
    )(o_attn, o_raw, z, ga, gb, x, mod, dn_g, w_branch, w_out)


def _mlp(x1, tgt, mod, norm2_g, w_gu, w_dn, bt):
    B, S, _ = x1.shape

    def body(x1_ref, t_ref, mod_ref, g_ref, wgu_ref, wdn_ref,
             dx1_ref, h2_ref, act_ref, dgu_ref, dyy_ref, loss_ref, dmod_ref, dg_ref):
        w_gu_v, w_dn_v, t = [wgu_ref[k] for k in range(N_CHIP)], wdn_ref[...], t_ref[...]

        def f(x1, gain, shift, scale, gate2, p_gu, p_yy):
            return _mlp_tile(x1, gain, shift, scale, gate2, w_gu_v, w_dn_v, t, p_gu, p_yy)
        zero_gu = jnp.zeros((bt, 2 * FFN), F32)
        zero_yy = jnp.zeros((bt, D), F32)
        loss, vjp, (h2, act) = jax.vjp(f, x1_ref[...], g_ref[...], mod_ref[:, 3 * D:4 * D], mod_ref[:, 4 * D:5 * D],
                                       mod_ref[:, 5 * D:6 * D], zero_gu, zero_yy, has_aux=True)
        dx1, dgain, dshift, dscale, dgate2, dgu, dyy = vjp(jnp.ones((), F32))
        dx1_ref[...] = dx1
        h2_ref[...] = h2.astype(BF)
        act_ref[...] = act.astype(BF)
        dgu_ref[...] = dgu.astype(BF)
        dyy_ref[...] = dyy.astype(BF)
        first = pl.program_id(1) == 0
        _acc(loss_ref, jnp.reshape(loss, (1, 1)), first)
        _acc(dmod_ref, jnp.concatenate([dshift, dscale, dgate2], axis=-1), first)
        _acc(dg_ref, dgain, first)

    return pl.pallas_call(
        body, name="mlp", grid=(B, S // bt),
        in_specs=[_tok(bt, D), _tok(bt, D), _per_batch(6 * D), _full((1, D)), _resident((N_CHIP, D, 2 * FFN // N_CHIP)),
                  _resident((FFN, D))],
        out_specs=[_tok(bt, D), _tok(bt, D), _tok(bt, FFN), _tok(bt, 2 * FFN), _tok(bt, D),
                   _per_batch(1), _per_batch(3 * D), _per_batch(D)],
        out_shape=[_sds((B, S, D), F32), _sds((B, S, D), BF), _sds((B, S, FFN), BF), _sds((B, S, 2 * FFN), BF),
                   _sds((B, S, D), BF), _sds((B, 1, 1), F32), _sds((B, 1, 3 * D), F32), _sds((B, 1, D), F32)],
        compiler_params=_cparams(dimension_semantics=("parallel", "arbitrary")),
    )(x1, tgt, mod, norm2_g, w_gu, w_dn)


def _mix_bwd(o_attn, o_raw, z, ga, gb, x, mod, dn_g, w_branch, w_out, dx1, bt, ex=None):
    B, S, _ = x.shape

    def body(oa_ref, or_ref, z_ref, ga_ref, gb_ref, x_ref, mod_ref, dg_ref, wb_ref, wo_ref, dx1_ref,
             doa_ref, dor_ref, dz_ref, dga_ref, dgb_ref, dya_ref, dyd_ref, dout_ref, dgate_ref, ddg_ref):
        oa, o_r, zs = _mix_load(oa_ref, or_ref, z_ref)
        wb_a, wb_d, wo = wb_ref[0:QW, :], wb_ref[QW:2 * QW, :], wo_ref[...]

        def f(oa, o_r, zs, ga, gb, gate1, dn_g, p_ya, p_yd, p_out):
            return _mix_tile(oa, o_r, zs, ga, gb, x_ref[...], gate1, dn_g, wb_a, wb_d, wo, p_ya, p_yd, p_out)[0]
        zero = jnp.zeros((bt, D), F32)
        _, vjp = jax.vjp(f, oa, o_r, zs, ga_ref[...].astype(F32), gb_ref[...].astype(F32), mod_ref[:, 2 * D:3 * D],
                         dg_ref[...], zero, zero, zero)
        doa, dor, dzs, dga, dgb, dgate1, ddn_g, dya, dyd, dout = vjp(dx1_ref[...])
        doa_ref[...] = doa
        for h in range(DN_H):
            dor_ref[:, h * DN_D:(h + 1) * DN_D] = dor[h]
            dz_ref[:, h * DN_D:(h + 1) * DN_D] = dzs[h].astype(BF)
        dga_ref[...] = dga.astype(BF)
        dgb_ref[...] = dgb.astype(BF)
        dya_ref[...] = dya.astype(BF)
        dyd_ref[...] = dyd.astype(BF)
        dout_ref[...] = dout.astype(BF)
        first = pl.program_id(1) == 0
        _acc(dgate_ref, dgate1, first)
        _acc(ddg_ref, ddn_g, first)

    return _hosted_call(
        body, "mix_bwd", (B, S // bt),
        in_specs=[_tok(bt, QW), _tok(bt, DNW), _tok(bt, DNW), _tok(bt, D), _tok(bt, D), _tok(bt, D), _per_batch(6 * D),
                  _full((1, DN_D)), _full((D, D)), _full((D, D)), _tok(bt, D)],
        out_specs=[_tok(bt, QW), _tok(bt, DNW), _tok(bt, DNW), _tok(bt, D), _tok(bt, D), _tok(bt, D), _tok(bt, D), _tok(bt, D),
                   _per_batch(D), _per_batch(DN_D)],
        out_shape=[_sds((B, S, QW), F32), _sds((B, S, DNW), F32), _sds((B, S, DNW), BF), _sds((B, S, D), BF),
                   _sds((B, S, D), BF), _sds((B, S, D), BF), _sds((B, S, D), BF), _sds((B, S, D), BF),
                   _sds((B, 1, D), F32), _sds((B, 1, DN_D), F32)],
        scratch_shapes=[], semantics=("parallel", "arbitrary"),
        ins=(o_attn, o_raw, z, ga, gb, x, mod, dn_g, w_branch, w_out, dx1), ex=ex)


def _dn_rec_bwd(u, w, qd, kd, a, cd, states, d_o, ex=None):
    B, S, _ = u.shape
    nc = S // CHUNK
    tok, cd_spec, st_spec = _rec_specs(B, lambda i: nc - 1 - i)

    def body(u_ref, w_ref, qd_ref, kd_ref, a_ref, cd_ref, st_ref, do_ref,
             du_ref, dw_ref, dqd_ref, dkd_ref, da_ref, dcd_ref, ds_ref):
        @pl.when(pl.program_id(0) == 0)
        def _():
            ds_ref[...] = jnp.zeros_like(ds_ref)
        state = st_ref[...].reshape(B * DN_H, DN_D, DN_D)
        _, vjp = jax.vjp(_dn_rec, state, *_rec_load(B, u_ref, w_ref, qd_ref, kd_ref, a_ref, cd_ref))
        dst, du, dw, dqd, dkd, da, dcd = vjp((ds_ref[...], _rec_stack(do_ref, B, DN_D)))
        ds_ref[...] = dst
        for ref, val, width in ((du_ref, du, DN_D), (dw_ref, dw, DN_D), (dqd_ref, dqd, DN_D), (dkd_ref, dkd, DN_D),
                                (da_ref, da, CHUNK)):
            _rec_store(B, ref, val, width)
        lane_row = _iota((1, DN_D), 1)
        for b in range(B):
            row = jnp.zeros((1, DN_D), F32)
            for h in range(DN_H):
                row = row + jnp.where(lane_row == h, dcd[b * DN_H + h], 0.0)
            dcd_ref[b, 0] = row

    return _hosted_call(
        body, "dn_rec_bwd", (nc,),
        in_specs=[tok(DNW)] * 4 + [tok(AW), cd_spec, st_spec, tok(DNW)],
        out_specs=[tok(DNW)] * 4 + [tok(AW), cd_spec],
        out_shape=[_sds((B, S, DNW), F32)] * 4 + [_sds((B, S, AW), F32), _sds((B, nc, 1, DN_D), F32)],
        scratch_shapes=[pltpu.VMEM((B * DN_H, DN_D, DN_D), F32)],
        semantics=("arbitrary",), ins=(u, w, qd, kd, a, cd, states, d_o), ex=ex)


def _dn_intra_bwd(qkv, bg, tinv, du, dw, dqd, dkd, da, dcd, ex=None):
    B, S, _ = qkv.shape
    nc = S // CHUNK
    G = min(DN_GROUP, nc)
    rows = G * CHUNK

    def body(qkv_ref, bg_ref, t_ref, du_ref, dw_ref, dqd_ref, dkd_ref, da_ref, dcd_ref, dqkv_ref, dbg_ref):
        q, k, v = _dn_load_stack(qkv_ref, G)
        known = _stack_heads(t_ref, G, 0, CHUNK)
        _, vjp = jax.vjp(lambda q, k, v, bg: _dn_intra(q, k, v, bg, known)[:6], q, k, v,
                         bg_ref[...].reshape(G, CHUNK, DN_D))
        lane_row = _iota((1, DN_D), 1)
        dcd = jnp.stack([jnp.sum(jnp.where(lane_row == h, dcd_ref[g], 0.0), axis=-1, keepdims=True)
                         for g in range(G) for h in range(DN_H)])
        dq, dk, dv, dbg = vjp((_stack_heads(du_ref, G, 0, DN_D), _stack_heads(dw_ref, G, 0, DN_D),
                               _stack_heads(dqd_ref, G, 0, DN_D), _stack_heads(dkd_ref, G, 0, DN_D),
                               _stack_heads(da_ref, G, 0, CHUNK), dcd))
        for g in range(G):
            rows = slice(g * CHUNK, (g + 1) * CHUNK)
            for h in range(DN_H):
                n = g * DN_H + h
                dqkv_ref[rows, h * DN_D:(h + 1) * DN_D] = dq[n]
                dqkv_ref[rows, DNW + h * DN_D:DNW + (h + 1) * DN_D] = dk[n]
                dqkv_ref[rows, 2 * DNW + h * DN_D:2 * DNW + (h + 1) * DN_D] = dv[n]
        dbg_ref[...] = dbg.reshape(G * CHUNK, DN_D)

    return _hosted_call(
        body, "dn_intra_bwd", (B, nc // G),
        in_specs=[_tok(rows, CONVW), _tok(rows, 128), _tok(rows, AW)] + [_tok(rows, DNW)] * 4 + [_tok(rows, AW), _cd_spec(G)],
        out_specs=[_tok(rows, CONVW), _tok(rows, 128)],
        out_shape=[_sds((B, S, CONVW), F32), _sds((B, S, 128), F32)],
        scratch_shapes=[], semantics=("parallel", "parallel"), ins=(qkv, bg, tinv, du, dw, dqd, dkd, da, dcd), ex=ex)


def _dn_prep_bwd(dn, y, ba, conv_w, alog, dtb, dqkv, dbg, bt):
    B, S, _ = dn.shape
    nt = S // bt

    def rev(f):
        return pl.BlockSpec((None, bt, f), lambda b, i: (b, nt - 1 - i, 0))

    def body(x_ref, y_ref, ba_ref, cw_ref, al_ref, dt_ref, dqkv_ref, dbg_ref,
             dx_ref, dba_ref, dcw_ref, dal_ref, ddt_ref, dye_ref):
        i = pl.program_id(1)
        ys = [y_ref[:, j * DN_D:(j + 1) * DN_D].astype(F32) for j in range(3 * DN_H)]
        _, vjp = jax.vjp(_dn_tail, ys, ba_ref[...], al_ref[...], dt_ref[...])
        d_out = [dqkv_ref[:, j * DN_D:(j + 1) * DN_D] for j in range(3 * DN_H)]
        dys, dba, dal, ddt = vjp((d_out, dbg_ref[...]))
        @pl.when(i == 0)
        def _():
            dye_ref[bt:bt + 8, :] = jnp.zeros((8, CONVW), F32)

        @pl.when(i > 0)
        def _():
            dye_ref[bt:bt + 8, :] = dye_ref[0:8, :]
        for j in range(3 * DN_H):
            dye_ref[0:bt, j * DN_D:(j + 1) * DN_D] = dys[j]
        shifted = [dye_ref[3 - j:bt + 3 - j, :] for j in range(CONV)]
        dx = cw_ref[0:1, :] * shifted[0]
        for j in range(1, CONV):
            dx = dx + cw_ref[j:j + 1, :] * shifted[j]
        dx_ref[...] = dx.astype(BF)
        x = x_ref[...].astype(F32)
        dcw = jnp.concatenate([jnp.sum(shifted[j] * x, axis=0, keepdims=True) for j in range(CONV)], axis=0)
        first = (i == 0) & (pl.program_id(0) == 0)
        dba_ref[...] = dba
        _acc(dcw_ref, dcw, first)
        _acc(dal_ref, dal, first)
        _acc(ddt_ref, ddt, first)

    return pl.pallas_call(
        body, name="dn_prep_bwd", grid=(B, nt),
        in_specs=[rev(CONVW), rev(CONVW), rev(128), _full((CONV, CONVW)), _full((1, 128)), _full((1, 128)), rev(CONVW), rev(128)],
        out_specs=[rev(CONVW), rev(128), _full((CONV, CONVW)), _full((1, 128)), _full((1, 128))],
        out_shape=[_sds((B, S, CONVW), BF), _sds((B, S, 128), F32), _sds((CONV, CONVW), F32), _sds((1, 128), F32),
                   _sds((1, 128), F32)],
        scratch_shapes=[pltpu.VMEM((bt + 8, CONVW), F32)],
        compiler_params=_cparams(dimension_semantics=("arbitrary", "arbitrary")),
    )(dn, y, ba, conv_w, alog, dtb, dqkv, dbg)


def _attn_bwd(qkv, sinks, d_o):
    B, S, _ = qkv.shape
    nb = S // BLK

    def cur(f):
        return pl.BlockSpec((None, BLK, f), lambda b, i: (b, jnp.minimum(i, nb - 1), 0))

    def out_prev(f):
        return pl.BlockSpec((None, BLK, f), lambda b, i: (b, jnp.maximum(i - 1, 0), 0))

    def body(qkv_ref, kvp_ref, sk_ref, do_ref, dq_ref, dkv_ref, dsk_ref, carry_ref):
        n = pl.program_id(1)
        first = (n == 0) & (pl.program_id(0) == 0)

        @pl.when(n == 0)
        def _():
            carry_ref[...] = jnp.zeros_like(carry_ref)

        @pl.when(n < nb)
        def _():
            qs, kc, kp, vc, vp = _attn_load(qkv_ref, kvp_ref)

            def f(qs, kc, kp, vc, vp, sk):
                return _attn_block(qs, kc, kp, vc, vp, sk, n > 0)
            _, vjp = jax.vjp(f, qs, kc, kp, vc, vp, sk_ref[...])
            d_outs = [do_ref[:, h * HD:(h + 1) * HD] for h in range(HEADS)]
            dqs, dkc, dkp, dvc, dvp, dsk = vjp(d_outs)
            for h in range(HEADS):
                dq_ref[:, h * HD:(h + 1) * HD] = dqs[h].astype(BF)
            for h in range(KV_HEADS):
                ksl = slice(h * HD, (h + 1) * HD)
                vsl = slice(KVW + h * HD, KVW + (h + 1) * HD)
                dkv_ref[:, ksl] = (carry_ref[:, ksl] + dkp[h]).astype(BF)
                dkv_ref[:, vsl] = (carry_ref[:, vsl] + dvp[h]).astype(BF)
                carry_ref[:, ksl] = dkc[h]
                carry_ref[:, vsl] = dvc[h]
            _acc(dsk_ref, dsk, first)

        @pl.when(n == nb)
        def _():
            dkv_ref[...] = carry_ref[...].astype(BF)

    return pl.pallas_call(
        body, name="attn_bwd", grid=(B, nb + 1),
        in_specs=[cur(QKV), _kv_prev_spec(lambda i: jnp.maximum(jnp.minimum(i, nb - 1) - 1, 0)), _full((1, HEADS)), cur(QW)],
        out_specs=[cur(QW), out_prev(2 * KVW), _full((1, HEADS))],
        out_shape=[_sds((B, S, QW), BF), _sds((B, S, 2 * KVW), BF), _sds((1, HEADS), F32)],
        scratch_shapes=[pltpu.VMEM((BLK, 2 * KVW), F32)],
        compiler_params=_cparams(dimension_semantics=("arbitrary", "arbitrary")),
    )(qkv, qkv, sinks, d_o)


def _in_proj_bwd(x, mod, norm1_g, w_in, pieces, dba, dx1, bt):
    B, S, _ = x.shape
    widths = (QW, 2 * KVW, CONVW, DNW, D, D)

    def body(x_ref, mod_ref, g_ref, w_ref, dq_ref, dkv_ref, ddn_ref, dz_ref, dga_ref, dgb_ref, dba_ref, dx1_ref,
             gx_ref, dp_ref, dmod_ref, dg_ref):
        dp = jnp.concatenate([r[...] for r in (dq_ref, dkv_ref, ddn_ref, dz_ref, dga_ref, dgb_ref)]
                             + [dba_ref[...].astype(BF)], axis=-1)
        dp_ref[...] = dp
        dh = lax.dot_general(dp, w_ref[...], (((1,), (1,)), ((), ())), preferred_element_type=F32)
        _, vjp = jax.vjp(_norm_mod, x_ref[...], g_ref[...], mod_ref[:, 0:D], mod_ref[:, D:2 * D])
        dx, dgain, dshift, dscale = vjp(dh)
        gx_ref[...] = dx + dx1_ref[...]
        first = pl.program_id(1) == 0
        _acc(dmod_ref, jnp.concatenate([dshift, dscale], axis=-1), first)
        _acc(dg_ref, dgain, first)

    return pl.pallas_call(
        body, name="in_proj_bwd", grid=(B, S // bt),
        in_specs=[_tok(bt, D), _per_batch(6 * D), _full((1, D)), _full((D, IN_PAD))] + [_tok(bt, w) for w in widths]
        + [_tok(bt, 128), _tok(bt, D)],
        out_specs=[_tok(bt, D), _tok(bt, IN_PAD), _per_batch(2 * D), _per_batch(D)],
        out_shape=[_sds((B, S, D), F32), _sds((B, S, IN_PAD), BF), _sds((B, 1, 2 * D), F32), _sds((B, 1, D), F32)],
        compiler_params=_cparams(dimension_semantics=("parallel", "arbitrary")),
    )(x, mod, norm1_g, w_in, *pieces, dba, dx1)


def _matmul_tn(tag, a, b, bk, bn, bt, col_blocks=False):
    T, K = a.shape
    N = b.shape[1]
    nt = T // bt
    if col_blocks:
        assert bk == K
        out_spec = pl.BlockSpec((None, bk, bn), lambda i, j, t: (j, 0, 0))
        out_shape = _sds((N // bn, K, bn), F32)
    else:
        out_spec = pl.BlockSpec((bk, bn), lambda i, j, t: (i, j))
        out_shape = _sds((K, N), F32)

    def body(a_ref, b_ref, o_ref, acc_ref):
        t = pl.program_id(2)

        @pl.when(t == 0)
        def _():
            acc_ref[...] = jnp.zeros_like(acc_ref)
        acc_ref[...] += lax.dot_general(a_ref[...], b_ref[...], (((0,), (0,)), ((), ())), preferred_element_type=F32)

        @pl.when(t == nt - 1)
        def _():
            o_ref[...] = acc_ref[...]

    return pl.pallas_call(
        body, name=f"grad_{tag}", grid=(K // bk, N // bn, nt),
        in_specs=[pl.BlockSpec((bt, bk), lambda i, j, t: (t, i)), pl.BlockSpec((bt, bn), lambda i, j, t: (t, j))],
        out_specs=out_spec, out_shape=out_shape,
        scratch_shapes=[pltpu.VMEM((bk, bn), F32)],
        compiler_params=_cparams(dimension_semantics=("parallel", "parallel", "arbitrary")),
    )(a, b)


def _rope_table(positions):
    inv_freq = THETA ** (-jnp.arange(0, ROT, 2, dtype=F32) / ROT)
    rest = jnp.zeros((HD - ROT,), F32)
    freq = jnp.concatenate([inv_freq, inv_freq, rest] * 2)
    sign = jnp.concatenate([-jnp.ones_like(inv_freq), jnp.ones_like(inv_freq), rest] * 2)
    ang = positions.astype(F32)[..., None] * freq
    return jnp.cos(ang), jnp.sin(ang) * sign


def _lane_pad(v, offset, width=128):
    return jnp.zeros((1, width), F32).at[0, offset:offset + v.shape[-1]].set(v.reshape(-1))


def _tile(S, want):
    return min(S, want)


class _Hosted:
    def __init__(self, call):
        self.call = call
        self.outs = None

    def __call__(self, ex):
        self.outs, landed = self.call(ex)
        return landed


def _local_step(x, mod, positions, tgt, norm1_g, w_in_pad, conv_w, q_norm_g, k_norm_g, sinks, a_log, dt_bias,
                dn_norm_g, w_branch, w_out, norm2_g, w_gu, w_dn, dist=None):
    B, S, _ = x.shape
    T = B * S
    cos_t, sin_t = _rope_table(positions)
    qg2 = jnp.concatenate([q_norm_g, q_norm_g], axis=-1)
    kg2 = jnp.concatenate([k_norm_g, k_norm_g], axis=-1)
    alog = _lane_pad(a_log, DN_H)
    dtb = _lane_pad(dt_bias, DN_H)
    conv2 = conv_w.reshape(CONV, CONVW)
    bt = _tile(S, 512)
    bt_mlp = _tile(S, 256)

    q, kv, dn, z, ga, gb, ba, h1 = _in_proj(x, mod, norm1_g, w_in_pad, bt)
    qkv_n = _qk_prep_fwd(q, kv, cos_t, sin_t, qg2, kg2, bt)
    o_attn = _attn_fwd(qkv_n, sinks)
    dqkv, bg, dn_y = _dn_prep(dn, ba, conv2, alog, dtb, bt)
    intra = _Hosted(lambda ex: _dn_intra_fwd(dqkv, bg, ex))
    rec = _Hosted(lambda ex: _dn_rec_fwd(*intra.outs[:6], ex))
    if dist is None:
        intra(None)
        rec(None)
    else:
        f_br, f_out, w_gu, f_dn = _gather_weights("late", [w_branch, w_out, w_gu, w_dn], *dist, hosts=[intra, rec])
        w_branch, w_out, w_dn = (f.reshape(N_CHIP * f.shape[1], f.shape[2]) for f in (f_br, f_out, f_dn))
    dn_u, dn_w, dn_qd, dn_kd, dn_a, dn_cd, dn_tinv = intra.outs
    o_raw, states = rec.outs
    x1, o_dn, merged = _mix_fwd(o_attn, o_raw, z, ga, gb, x, mod, dn_norm_g, w_branch, w_out, bt)
    dx1, h2, act, dgu, dyy, loss, dmod2, dnorm2 = _mlp(x1, tgt, mod, norm2_g, w_gu, w_dn, bt_mlp)

    def flat(t):
        return t.reshape(T, t.shape[-1])
    tn = functools.partial(_matmul_tn, bt=_tile(T, 1024))
    g_w_dn = tn("w_down", flat(act), flat(dyy), bk=FFN, bn=D // 2)
    g_w_gu = tn("w_gate_up", flat(h2), flat(dgu), bk=D, bn=2 * FFN // N_CHIP, col_blocks=True)

    mix_b = _Hosted(lambda ex: _mix_bwd(o_attn, o_raw, z, ga, gb, x, mod, dn_norm_g, w_branch, w_out, dx1, bt_mlp, ex))
    rec_b = _Hosted(lambda ex: _dn_rec_bwd(dn_u, dn_w, dn_qd, dn_kd, dn_a, dn_cd, states, mix_b.outs[1], ex))
    intra_b = _Hosted(lambda ex: _dn_intra_bwd(dqkv, bg, dn_tinv, *rec_b.outs, ex))
    if dist is None:
        for host in (mix_b, rec_b, intra_b):
            host(None)
    else:
        g_w_gu, g_w_dn = _reduce_grads(("w_gate_up", "w_down"), [g_w_gu, g_w_dn.reshape(N_CHIP, -1, D)], *dist,
                                       hosts=[mix_b, rec_b, intra_b])
    d_oa, _, dz, dga, dgb, dya, dyd, dout, dgate1, ddn_g = mix_b.outs
    d_dqkv, dbg = intra_b.outs
    g_w_out = tn("w_out", flat(merged), flat(dout), bk=D, bn=D)
    g_w_br = jnp.concatenate([tn("w_branch_attn", flat(o_attn), flat(dya), bk=QW, bn=D),
                              tn("w_branch_dn", flat(o_dn), flat(dyd), bk=DNW, bn=D)], axis=0)
    d_dn, dba, dconv, dalog, ddtb = _dn_prep_bwd(dn, dn_y, ba, conv2, alog, dtb, d_dqkv, dbg, bt)
    dqn, dkvn, dsk = _attn_bwd(qkv_n, sinks, d_oa)
    dq, dkv, dqg2, dkg2 = _qk_prep_bwd(q, kv, cos_t, sin_t, qg2, kg2, dqn, dkvn, bt)
    dqg = dqg2[:, :HD] + dqg2[:, HD:]
    dkg = dkg2[:, :HD] + dkg2[:, HD:]
    grad_x, dproj, dmod1, dnorm1 = _in_proj_bwd(x, mod, norm1_g, w_in_pad, (dq, dkv, d_dn, dz, dga, dgb), dba, dx1, bt)
    g_w_in = tn("w_in", flat(h1), flat(dproj), bk=D, bn=IN_PAD // 3)

    dmod = jnp.concatenate([dmod1, dgate1, dmod2], axis=-1)
    small = dict(norm1_g=jnp.sum(dnorm1, axis=0), norm2_g=jnp.sum(dnorm2, axis=0), q_norm_g=dqg, k_norm_g=dkg,
                 sinks=dsk, a_log=dalog[:, DN_H:2 * DN_H], dt_bias=ddtb[:, DN_H:2 * DN_H],
                 dn_norm_g=jnp.sum(ddn_g, axis=0), conv_w=dconv)
    return jnp.sum(loss), grad_x, dmod, small, (g_w_in, g_w_br, g_w_out, g_w_gu, g_w_dn)


def _flip(me, f):
    return (me[0] ^ ((f >> 2) & 1), me[1] ^ ((f >> 1) & 1), me[2] ^ (f & 1))


def _comm_call(name, ex):
    n_in, n_out = len(ex.ins), len(ex.out_shapes)

    def body(*refs):
        cps = _exchange_copies(ex, refs[:n_in], refs[n_in:n_in + n_out], *refs[n_in + n_out:])
        for cp in cps:
            cp.start()
        for cp in cps:
            cp.wait_recv()
        for cp in cps:
            cp.wait_send()

    any_spec = pl.BlockSpec(memory_space=pl.ANY)
    return pl.pallas_call(
        body, name=name, in_specs=[any_spec] * n_in, out_specs=[any_spec] * n_out, out_shape=list(ex.out_shapes),
        scratch_shapes=[pltpu.SemaphoreType.DMA((ex.n_remote,)), pltpu.SemaphoreType.DMA((ex.n_remote,))],
    )(*ex.ins)


def _by_origin(own, received, index):
    stack = jnp.concatenate([own[None], received], axis=0)
    n = stack.shape[0]
    return jnp.stack([lax.dynamic_index_in_dim(stack, k ^ index, 0, keepdims=False) for k in range(n)])


def _gather_devices(name, arrs, dev):
    def plan(me, in_refs, out_refs):
        return [(a, o.at[f - 1], _flip(me, f)) for a, o in zip(in_refs, out_refs) for f in range(1, N_DEV)]
    outs = tuple(_sds((N_DEV - 1,) + a.shape, a.dtype) for a in arrs)
    got = _comm_call(name, _Exchange(tuple(arrs), outs, (N_DEV - 1) * len(arrs), plan))
    return [_by_origin(a, g, dev) for a, g in zip(arrs, got)]


def _gather_chips(name, arrs, chip):
    def plan(me, in_refs, out_refs):
        return [(a, o.at[j], _flip(me, 2 * (j + 1))) for a, o in zip(in_refs, out_refs) for j in range(N_CHIP - 1)]
    outs = tuple(_sds((N_CHIP - 1,) + a.shape, a.dtype) for a in arrs)
    got = _comm_call(name, _Exchange(tuple(arrs), outs, (N_CHIP - 1) * len(arrs), plan))
    return [_by_origin(a, g, chip) for a, g in zip(arrs, got)]


def _halves(core, mine, other):
    lo = jnp.where(core == 0, mine, other)
    hi = jnp.where(core == 0, other, mine)
    return jnp.concatenate([lo, hi], axis=-2)


def _swap_cores_ex(arrs):
    def plan(me, in_refs, out_refs):
        return [(g, o, _flip(me, 1)) for g, o in zip(in_refs, out_refs)]
    return _Exchange(tuple(arrs), tuple(_sds(g.shape, g.dtype) for g in arrs), len(arrs), plan)


def _gather_weights(tag, shards, chip, core, hosts=None):
    def plan_ici(me, in_refs, out_refs):
        remote = []
        for a, o in zip(in_refs, out_refs):
            half = a.shape[0] // 2
            mine = a.at[pl.ds(me[2] * half, half)]
            remote += [(mine, o.at[j], _flip(me, 2 * (j + 1))) for j in range(N_CHIP - 1)]
        return remote
    run = hosts or [functools.partial(_comm_call, f"weights_ici_{tag}"), functools.partial(_comm_call, f"weights_d2d_{tag}")]
    mine = run[0](_Exchange(tuple(shards), tuple(_sds((N_CHIP - 1, a.shape[0] // 2, a.shape[1]), a.dtype) for a in shards),
                            (N_CHIP - 1) * len(shards), plan_ici))
    other = run[1](_swap_cores_ex(mine))
    return [_by_origin(a, _halves(core, g, h), chip) for a, g, h in zip(shards, mine, other)]


def _rows(r):
    for br in (512, 352, 256, 128, 64, 32, 16, 8):
        if r % br == 0:
            return br
    raise ValueError(r)


def _pair_add(tag, g, recv, c):
    n, r, cols = g.shape
    half = r // 2
    br = _rows(half)
    nb = half // br

    def body(c_ref, g_ref, r_ref, o_ref):
        o_ref[...] = (g_ref[...] + r_ref[...]).astype(BF)

    return pl.pallas_call(
        body, name=f"pair_add_{tag}",
        grid_spec=pltpu.PrefetchScalarGridSpec(
            num_scalar_prefetch=1, grid=(n, nb),
            in_specs=[pl.BlockSpec((None, br, cols), lambda k, i, c_ref: (k, c_ref[0] * nb + i, 0)),
                      pl.BlockSpec((None, br, cols), lambda k, i, c_ref: (k, i, 0))],
            out_specs=pl.BlockSpec((None, br, cols), lambda k, i, c_ref: (k, i, 0))),
        out_shape=_sds((n, half, cols), BF),
        compiler_params=_cparams(dimension_semantics=("parallel", "parallel")),
    )(c, g, recv)


def _sum_chips(tag, p, q, chip):
    n, r, cols = q.shape
    br = _rows(r)

    def body(chip_ref, p_ref, q_ref, o_ref):
        acc = p_ref[...].astype(F32)
        for k in range(n):
            acc = acc + q_ref[k].astype(F32)
        o_ref[...] = acc

    return pl.pallas_call(
        body, name=f"sum_chips_{tag}",
        grid_spec=pltpu.PrefetchScalarGridSpec(
            num_scalar_prefetch=1, grid=(r // br,),
            in_specs=[pl.BlockSpec((None, br, cols), lambda i, chip_ref: (chip_ref[0], i, 0)),
                      pl.BlockSpec((n, br, cols), lambda i, chip_ref: (0, i, 0))],
            out_specs=pl.BlockSpec((br, cols), lambda i, chip_ref: (i, 0))),
        out_shape=_sds((r, cols), F32),
        compiler_params=_cparams(dimension_semantics=("parallel",)),
    )(chip, p, q)


def _reduce_grads(tags, grads, chip, core, hosts=None):
    core_arr = core.reshape(1).astype(jnp.int32)
    chip_arr = chip.reshape(1).astype(jnp.int32)
    name = "_".join(tags)
    run = hosts or [functools.partial(_comm_call, f"grads_{stage}_{name}") for stage in ("pair", "chips", "swap")]

    def plan_pair(me, in_refs, out_refs):
        remote = []
        for g, o in zip(in_refs, out_refs):
            half = g.shape[1] // 2
            remote += [(g.at[k, pl.ds((1 - me[2]) * half, half)], o.at[k], _flip(me, 1)) for k in range(N_CHIP)]
        return remote
    recv = run[0](_Exchange(tuple(grads), tuple(_sds((N_CHIP, g.shape[1] // 2, g.shape[2]), F32) for g in grads),
                            N_CHIP * len(grads), plan_pair))
    pair = [_pair_add(t, g, r, core_arr) for t, g, r in zip(tags, grads, recv)]

    def plan_chips(me, in_refs, out_refs):
        remote = []
        for p, o in zip(in_refs, out_refs):
            for j in range(N_CHIP - 1):
                peer = _flip(me, 2 * (j + 1))
                remote.append((p.at[2 * peer[0] + peer[1]], o.at[j], peer))
        return remote
    parts = run[1](_Exchange(tuple(pair), tuple(_sds((N_CHIP - 1,) + p.shape[1:], BF) for p in pair),
                             (N_CHIP - 1) * len(pair), plan_chips))
    mine = [_sum_chips(t, p, q, chip_arr) for t, p, q in zip(tags, pair, parts)]
    other = run[2](_swap_cores_ex(mine))
    return [_halves(core, h, o) for h, o in zip(mine, other)]


def _adamw_math(w, g, m, v):
    m = ADAM_B1 * m + (1.0 - ADAM_B1) * g
    v = ADAM_B2 * v + (1.0 - ADAM_B2) * (g * g)
    m_hat = m / (1.0 - ADAM_B1 ** ADAM_STEP)
    v_hat = v / (1.0 - ADAM_B2 ** ADAM_STEP)
    delta = -ADAM_LR * (m_hat / (jnp.sqrt(v_hat) + ADAM_EPS) + ADAM_WD * w)
    return delta, m, v


def _adamw(name, w, g, m, v):
    r, cols = w.shape
    br = _rows(r)
    if br * cols * 4 > (1 << 20) and br % 16 == 0:
        br //= 2

    def body(w_ref, g_ref, m_ref, v_ref, d_ref, mo_ref, vo_ref):
        d_ref[...], mo_ref[...], vo_ref[...] = _adamw_math(w_ref[...], g_ref[...], m_ref[...], v_ref[...])

    spec = pl.BlockSpec((br, cols), lambda i: (i, 0))
    return pl.pallas_call(
        body, name=f"adamw_{name}", grid=(r // br,), in_specs=[spec] * 4, out_specs=[spec] * 3,
        out_shape=[_sds((r, cols), F32)] * 3,
        compiler_params=_cparams(dimension_semantics=("parallel",)),
    )(w, g, m, v)


def _ada_fwd(c_all, ada_w, ada_b_cols):
    n = c_all.shape[0]

    def body(c_ref, w_ref, b_ref, o_ref):
        o_ref[...] = _mmx(_silu(c_ref[...]), w_ref[...]) + b_ref[...]

    return pl.pallas_call(
        body, name="ada_fwd", out_shape=_sds((n, ada_w.shape[1]), F32), compiler_params=_cparams(),
    )(c_all, ada_w, ada_b_cols)


def _ada_bwd(c_all, dmod_cols, w, m, v):
    n = c_all.shape[0]
    r, cols = w.shape
    br = 128

    def body(c_ref, d_ref, w_ref, m_ref, v_ref, g_ref, dl_ref, mo_ref, vo_ref):
        cond = _silu(c_ref[...])
        g = lax.dot_general(cond, d_ref[...], (((0,), (0,)), ((), ())), precision=lax.Precision.HIGHEST,
                            preferred_element_type=F32)
        g_ref[...] = g
        dl_ref[...], mo_ref[...], vo_ref[...] = _adamw_math(w_ref[...], g, m_ref[...], v_ref[...])

    spec = pl.BlockSpec((br, cols), lambda i: (i, 0))
    return pl.pallas_call(
        body, name="ada_bwd", grid=(r // br,),
        in_specs=[pl.BlockSpec((n, br), lambda i: (0, i)), pl.BlockSpec((n, cols), lambda i: (0, 0)), spec, spec, spec],
        out_specs=[spec] * 4, out_shape=[_sds((r, cols), F32)] * 4,
        compiler_params=_cparams(dimension_semantics=("parallel",)),
    )(c_all, dmod_cols, w, m, v)


def _sum_devices(parts):
    n, r, cols = parts.shape

    def body(p_ref, o_ref):
        acc = p_ref[0]
        for k in range(1, n):
            acc = acc + p_ref[k]
        o_ref[...] = acc

    return pl.pallas_call(body, name="sum_devices", out_shape=_sds((r, cols), F32), compiler_params=_cparams())(parts)


SMALL_ROWS = 16
_SMALL_SLOTS = dict(norm1_g=(0, 0, D), norm2_g=(1, 0, D), q_norm_g=(2, 0, HD), k_norm_g=(2, 128, HD), sinks=(2, 256, HEADS),
                    a_log=(2, 384, DN_H), dt_bias=(2, 512, DN_H), dn_norm_g=(2, 640, DN_D))
_CONV_ROW = 4
_ADA_B_ROW = 8


def _pack_small(vals, conv, ada_b):
    def row(pieces):
        out, at = [], 0
        for col, val in pieces:
            out += [jnp.zeros((1, col - at), F32), val.reshape(1, -1)]
            at = col + val.size
        return jnp.concatenate(out + [jnp.zeros((1, CONVW - at), F32)], axis=1)
    rows = {}
    for name, (r, col, n) in _SMALL_SLOTS.items():
        rows.setdefault(r, []).append((col, vals[name]))
    blank = jnp.zeros((1, CONVW), F32)
    top = [row(sorted(rows[r], key=lambda p: p[0])) if r in rows else blank for r in range(_CONV_ROW)]
    conv_rows = jnp.concatenate([conv, jnp.zeros((CONV, CONVW - conv.shape[1]), F32)], axis=1)
    tail = jnp.zeros((SMALL_ROWS - _ADA_B_ROW - 4, CONVW), F32)
    return jnp.concatenate(top + [conv_rows, ada_b.reshape(4, CONVW), tail], axis=0)


def _unpack_small(sheet, conv_cols):
    out = {name: sheet[row, col:col + n].reshape(1, n) for name, (row, col, n) in _SMALL_SLOTS.items()}
    out["conv_w"] = sheet[_CONV_ROW:_CONV_ROW + CONV, 0:conv_cols].reshape(1, CONV, 1, conv_cols)
    out["ada_b"] = sheet[_ADA_B_ROW:_ADA_B_ROW + 4, :].reshape(1, 6 * D)
    return out


def _w_in_segments():
    shard = IN_WIDTH // N_CHIP
    cuts = sorted({0, IN_WIDTH, C_Z, C_Z + 2 * DN_H} | {k * shard for k in range(1, N_CHIP)})
    segs = []
    for a, b in zip(cuts[:-1], cuts[1:]):
        k = a // shard
        pad = a if a < C_Z else (C_BA + a - C_Z if a < C_Z + 2 * DN_H else a - 2 * DN_H)
        segs.append((k, a - k * shard, b - k * shard, pad))
    return segs


def _pad_w_in(f):
    parts = [f[k][:, lo:hi] for k, lo, hi, _ in sorted(_w_in_segments(), key=lambda s: s[3])]
    return jnp.concatenate(parts + [jnp.zeros((f.shape[1], IN_PAD - IN_WIDTH), f.dtype)], axis=1)


def _unpad_w_in(g):
    return jnp.stack([jnp.concatenate([g[:, pad:pad + hi - lo] for kk, lo, hi, pad in _w_in_segments() if kk == k], axis=1)
                      for k in range(N_CHIP)])


def _blocks_to_cols(f):
    return f.transpose(1, 0, 2).reshape(f.shape[1], N_CHIP * f.shape[2])


def kernel(x, c, positions, ada_w, ada_b, norm1_g, w_in, conv_w, q_norm_g, k_norm_g, sinks, a_log, dt_bias, dn_norm_g, w_branch, w_out, norm2_g, w_gate_up, w_down, loss_target, m_ada_w, m_ada_b, m_norm1_g, m_w_in, m_conv_w, m_q_norm_g, m_k_norm_g, m_sinks, m_a_log, m_dt_bias, m_dn_norm_g, m_w_branch, m_w_out, m_norm2_g, m_w_gate_up, m_w_down, v_ada_w, v_ada_b, v_norm1_g, v_w_in, v_conv_w, v_q_norm_g, v_k_norm_g, v_sinks, v_a_log, v_dt_bias, v_dn_norm_g, v_w_branch, v_w_out, v_norm2_g, v_w_gate_up, v_w_down):
    ix, iy, ic = lax.axis_index("x"), lax.axis_index("y"), lax.axis_index("c")
    dev = 4 * ix + 2 * iy + ic
    chip = 2 * ix + iy
    n_seq = x.shape[0]
    conv_cols = conv_w.shape[-1]

    c_all, conv_all = _gather_devices("gather_cond", [c, conv_w.reshape(CONV, conv_cols)], dev)
    c_all = c_all.reshape(N_DEV * n_seq, D)
    ada_cols = ada_w.shape[-1]
    ada_b_cols = lax.dynamic_slice(ada_b, (0, chip * ada_cols), (1, ada_cols))
    mod_cols = _ada_fwd(c_all, ada_w[0], ada_b_cols)
    (mod_blocks,) = _gather_chips("gather_mod", [mod_cols], chip)
    mod_all = _blocks_to_cols(mod_blocks)
    mod = lax.dynamic_slice(mod_all, (dev * n_seq, 0), (n_seq, 6 * D)).reshape(n_seq, 1, 6 * D)
    conv_full = _blocks_to_cols(conv_all[0::2])

    (f_in,) = _gather_weights("w_in", [w_in[0].astype(BF)], chip, ic)
    w_in_pad = _pad_w_in(f_in)

    loss, grad_x, dmod, small, (g_in, g_br, g_out, r_gu, r_dn) = _local_step(
        x, mod, positions, loss_target, norm1_g, w_in_pad, conv_full.reshape(CONV, 1, CONVW), q_norm_g, k_norm_g, sinks,
        a_log, dt_bias, dn_norm_g, w_branch[0].astype(BF), w_out[0].astype(BF), norm2_g, w_gate_up[0].astype(BF),
        w_down[0].astype(BF), dist=(chip, ic))
    loss = lax.psum(loss, ("x", "y", "c"))

    blocks = [_unpad_w_in(g_in), g_br.reshape(N_CHIP, -1, D), g_out.reshape(N_CHIP, -1, D)]
    r_in, r_br, r_out = _reduce_grads(("w_in", "w_branch", "w_out"), blocks, chip, ic)
    big = {}
    for name, w, g, m, v in (("w_in", w_in, r_in, m_w_in, v_w_in), ("w_branch", w_branch, r_br, m_w_branch, v_w_branch),
                             ("w_out", w_out, r_out, m_w_out, v_w_out), ("w_gate_up", w_gate_up, r_gu, m_w_gate_up, v_w_gate_up),
                             ("w_down", w_down, r_dn, m_w_down, v_w_down)):
        big[name] = (g,) + tuple(_adamw(name, w[0], g, m[0], v[0]))

    part = _pack_small(small, small["conv_w"], jnp.sum(dmod, axis=(0, 1)).reshape(1, 6 * D))
    dmod_all, parts = _gather_devices("gather_small", [dmod.reshape(n_seq, 6 * D), part], dev)
    dmod_all = dmod_all.reshape(N_DEV * n_seq, 6 * D)
    g_small = _unpack_small(_sum_devices(parts), CONVW)
    g_conv = lax.dynamic_slice(g_small["conv_w"].reshape(CONV, CONVW), (0, chip * conv_cols), (CONV, conv_cols))
    g_small["conv_w"] = g_conv.reshape(1, CONV, 1, conv_cols)

    given = dict(norm1_g=(norm1_g, m_norm1_g, v_norm1_g), norm2_g=(norm2_g, m_norm2_g, v_norm2_g),
                 q_norm_g=(q_norm_g, m_q_norm_g, v_q_norm_g), k_norm_g=(k_norm_g, m_k_norm_g, v_k_norm_g),
                 sinks=(sinks, m_sinks, v_sinks), a_log=(a_log, m_a_log, v_a_log), dt_bias=(dt_bias, m_dt_bias, v_dt_bias),
                 dn_norm_g=(dn_norm_g, m_dn_norm_g, v_dn_norm_g))
    sheets = [_pack_small({k: t[j] for k, t in given.items()}, cw.reshape(CONV, conv_cols), ab)
              for j, (cw, ab) in enumerate(((conv_w, ada_b), (m_conv_w, m_ada_b), (v_conv_w, v_ada_b)))]
    g_local = _pack_small(g_small, g_conv, g_small["ada_b"])
    upd = [_unpack_small(s, conv_cols) for s in _adamw("small", sheets[0], g_local, sheets[1], sheets[2])]

    dmod_cols = lax.dynamic_slice(dmod_all, (0, chip * ada_cols), (N_DEV * n_seq, ada_cols))
    ada = _ada_bwd(c_all, dmod_cols, ada_w[0], m_ada_w[0], v_ada_w[0])

    names = ["ada_w", "ada_b", "norm1_g", "w_in", "conv_w", "q_norm_g", "k_norm_g", "sinks", "a_log", "dt_bias", "dn_norm_g",
             "w_branch", "w_out", "norm2_g", "w_gate_up", "w_down"]

    def leaf(name, j):
        if name == "ada_w":
            return ada[j][None]
        if name in big:
            return big[name][j][None]
        return g_small[name] if j == 0 else upd[j - 1][name]

    return (loss, grad_x) + tuple(leaf(n, j) for j in range(4) for n in names)
```

```python
import functools
from typing import Callable, NamedTuple

import jax
import jax.numpy as jnp
import numpy as np
from jax import lax
from jax.experimental import pallas as pl
from jax.experimental.pallas import tpu as pltpu

F32 = jnp.float32
BF = jnp.bfloat16

D = 1024
HEADS = 8
KV_HEADS = 2
GROUP = 4
HD = 64
BLK = 128
ROT = 16
THETA = 500000.0
QW = 512
KVW = 128
DN_H = 4
DN_D = 128
CONV = 4
CHUNK = 64
DNW = 512
CONVW = 1536
FFN = 2816
EPS = 1e-6
IN_WIDTH = 4872
IN_PAD = 4992
C_KV = 512
C_DN = 768
C_Z = 2304
C_GA = 2816
C_GB = 3840
C_BA = 4864
NEG = -1e30
N_DEV = 8
N_CHIP = 4

ADAM_LR = 0.001
ADAM_B1 = 0.9
ADAM_B2 = 0.999
ADAM_EPS = 1e-08
ADAM_WD = 0.01
ADAM_STEP = 10

VMEM_LIMIT = 60 * 1024 * 1024


def _cparams(**kw):
    return pltpu.CompilerParams(vmem_limit_bytes=VMEM_LIMIT, **kw)


def _dg(a, b, ca, cb):
    return lax.dot_general(a.astype(BF), b.astype(BF), (((ca,), (cb,)), ((), ())),
                           preferred_element_type=F32)


@jax.custom_vjp
def _mm(a, b):
    return _dg(a, b, 1, 0)


def _mm_fwd(a, b):
    return _dg(a, b, 1, 0), (a, b)


def _mm_bwd(res, dy):
    a, b = res
    return _dg(dy, b, 1, 1).astype(a.dtype), _dg(a, dy, 0, 0).astype(b.dtype)


_mm.defvjp(_mm_fwd, _mm_bwd)


@jax.custom_vjp
def _mm_nt(a, b):
    return _dg(a, b, 1, 1)


def _mm_nt_fwd(a, b):
    return _dg(a, b, 1, 1), (a, b)


def _mm_nt_bwd(res, dy):
    a, b = res
    return _dg(dy, b, 1, 0).astype(a.dtype), _dg(dy, a, 0, 0).astype(b.dtype)


_mm_nt.defvjp(_mm_nt_fwd, _mm_nt_bwd)


@jax.custom_vjp
def _mm_tn(a, b):
    return _dg(a, b, 0, 0)


def _mm_tn_fwd(a, b):
    return _dg(a, b, 0, 0), (a, b)


def _mm_tn_bwd(res, dy):
    a, b = res
    return _dg(b, dy, 1, 1).astype(a.dtype), _dg(a, dy, 1, 0).astype(b.dtype)


_mm_tn.defvjp(_mm_tn_fwd, _mm_tn_bwd)


def _mmx(a, b):
    return jnp.dot(a, b, precision=lax.Precision.HIGHEST, preferred_element_type=F32)


def _mmx_nt(a, b):
    return lax.dot_general(a, b, (((1,), (1,)), ((), ())), precision=lax.Precision.HIGHEST,
                           preferred_element_type=F32)


def _iota(shape, dim):
    return lax.broadcasted_iota(jnp.int32, shape, dim)


def _sigmoid(x):
    return 1.0 / (1.0 + jnp.exp(-x))


def _silu(x):
    return x * _sigmoid(x)


def _softplus(x):
    return jnp.maximum(x, 0.0) + jnp.log(1.0 + jnp.exp(-jnp.abs(x)))


def _rms(x, gain):
    return x * lax.rsqrt(jnp.mean(x * x, axis=-1, keepdims=True) + EPS) * gain


def _norm_mod(x, gain, shift, scale):
    return _rms(x, gain) * (1.0 + scale) + shift


def _split(a):
    hi = a.astype(BF)
    return hi, (a - hi.astype(F32)).astype(BF)


def _dg3(a, b, ca, cb):
    ah, al = _split(a)
    bh, bl = _split(b)

    def dg(x, y):
        return lax.dot_general(x, y, (((ca,), (cb,)), ((), ())), preferred_element_type=F32)
    return dg(ah, bh) + (dg(ah, bl) + dg(al, bh))


@jax.custom_vjp
def _mm3(a, b):
    return _dg3(a, b, 1, 0)


def _mm3_fwd(a, b):
    return _dg3(a, b, 1, 0), (a, b)


def _mm3_bwd(res, dy):
    a, b = res
    return _dg3(dy, b, 1, 1), _dg3(a, dy, 0, 0)


_mm3.defvjp(_mm3_fwd, _mm3_bwd)


def _qk_prep(slabs, gain, cos, sin):
    r = _iota((2 * HD, 2 * HD), 0)
    c = _iota((2 * HD, 2 * HD), 1)
    seg = jnp.where(r // HD == c // HD, 1.0 / HD, 0.0).astype(F32)
    half = ROT // 2
    cd = c % HD
    pair = jnp.where(((cd < half) & (r == c + half)) | ((cd >= half) & (cd < ROT) & (r == c - half)), 1.0, 0.0).astype(F32)
    out = []
    for x in slabs:
        y = x * lax.rsqrt(_mm3(x * x, seg) + EPS) * gain
        out.append(y * cos + _mm3(y, pair) * sin)
    return out


def _attn_block(qs, kc, kp, vc, vp, sinks, has_prev):
    rows = GROUP * BLK
    qi = _iota((rows, 2 * BLK), 0) % BLK + BLK
    kj = _iota((rows, 2 * BLK), 1)
    dist = qi - kj
    valid = (dist >= 0) & (dist < BLK) & ((kj >= BLK) | has_prev)
    grp = _iota((rows, HEADS), 0) // BLK
    col = _iota((rows, HEADS), 1)

    outs = []
    for h in range(KV_HEADS):
        q = jnp.concatenate([qs[h * GROUP + g] for g in range(GROUP)], axis=0)
        k = jnp.concatenate([kp[h], kc[h]], axis=0)
        v = jnp.concatenate([vp[h], vc[h]], axis=0)
        s = _mm_nt(q, k) * (HD ** -0.5)
        s = jnp.where(valid, s, NEG)
        sink = jnp.sum(jnp.where(col == h * GROUP + grp, sinks, 0.0), axis=-1, keepdims=True)
        m = lax.stop_gradient(jnp.maximum(jnp.max(s, axis=-1, keepdims=True), sink))
        p = jnp.exp(s - m)
        probs = p / (jnp.sum(p, axis=-1, keepdims=True) + jnp.exp(sink - m))
        o = _mm(probs, v)
        outs += [o[g * BLK:(g + 1) * BLK] for g in range(GROUP)]
    return outs


def _dn_tail(ys, ba, alog, dtb):
    def l2(t):
        return t * lax.rsqrt(jnp.sum(t * t, axis=-1, keepdims=True) + EPS)
    s = [_silu(y) for y in ys]
    out = [l2(t) for t in s[:2 * DN_H]] + s[2 * DN_H:]
    lane = _iota(ba.shape, 1)
    beta = _sigmoid(ba)
    g = -jnp.exp(alog) * _softplus(ba + dtb)
    bg = jnp.where(lane < DN_H, beta, jnp.where(lane < 2 * DN_H, g, 0.0))
    return out, bg


def _bdg(a, b, ca, cb):
    return lax.dot_general(a.astype(BF), b.astype(BF), (((ca,), (cb,)), ((0,), (0,))), preferred_element_type=F32)


@jax.custom_vjp
def _bmm(a, b):
    return _bdg(a, b, 2, 1)


def _bmm_fwd(a, b):
    return _bdg(a, b, 2, 1), (a, b)


def _bmm_bwd(res, dy):
    a, b = res
    return _bdg(dy, b, 2, 2), _bdg(a, dy, 1, 1)


_bmm.defvjp(_bmm_fwd, _bmm_bwd)


@jax.custom_vjp
def _bmm_nt(a, b):
    return _bdg(a, b, 2, 2)


def _bmm_nt_fwd(a, b):
    return _bdg(a, b, 2, 2), (a, b)


def _bmm_nt_bwd(res, dy):
    a, b = res
    return _bdg(dy, b, 2, 1), _bdg(dy, a, 1, 1)


_bmm_nt.defvjp(_bmm_nt_fwd, _bmm_nt_bwd)


def _bmmx(a, b):
    return lax.dot_general(a, b, (((2,), (1,)), ((0,), (0,))), precision=lax.Precision.HIGHEST,
                           preferred_element_type=F32)


def _neumann_inverse(lmat):
    C = CHUNK
    eye = jnp.where(_iota((C, C), 0) == _iota((C, C), 1), 1.0, 0.0).astype(F32)[None]
    a = -lmat
    tinv = eye + a
    pw = _bmmx(a, a)
    for _ in range(4):
        both = _bmmx(jnp.concatenate([pw, tinv], axis=1), pw)
        pw, tinv = both[:, :C], tinv + both[:, C:]
    return tinv + _bmmx(tinv, pw)


def _inverse_bwd(tinv, d_tinv):
    x = lax.dot_general(d_tinv, tinv, (((2,), (2,)), ((0,), (0,))), precision=lax.Precision.HIGHEST,
                        preferred_element_type=F32)
    return -lax.dot_general(tinv, x, (((1,), (1,)), ((0,), (0,))), precision=lax.Precision.HIGHEST,
                            preferred_element_type=F32)


@jax.custom_vjp
def _tri_inverse(lmat):
    return _neumann_inverse(lmat)


def _tri_inverse_fwd(lmat):
    tinv = _neumann_inverse(lmat)
    return tinv, tinv


def _tri_inverse_bwd(tinv, d_tinv):
    return (_inverse_bwd(tinv, d_tinv),)


_tri_inverse.defvjp(_tri_inverse_fwd, _tri_inverse_bwd)


@jax.custom_vjp
def _tri_inverse_known(lmat, tinv):
    return tinv


def _tri_inverse_known_fwd(lmat, tinv):
    return tinv, tinv


def _tri_inverse_known_bwd(tinv, d_tinv):
    return _inverse_bwd(tinv, d_tinv), jnp.zeros_like(tinv)


_tri_inverse_known.defvjp(_tri_inverse_known_fwd, _tri_inverse_known_bwd)


def _dn_intra(q, k, v, bg, tinv=None):
    C = CHUNK
    G = bg.shape[0]
    r = _iota((C, C), 0)
    c = _iota((C, C), 1)
    incl = (r >= c)[None]
    strict = (r > c)[None]
    eye = jnp.where(r == c, 1.0, 0.0).astype(F32)[None]
    tri = jnp.broadcast_to(jnp.where(r >= c, 1.0, 0.0).astype(F32)[None], (G, C, C))
    gc_all = _bmmx(tri, bg)
    lane = _iota((C, DN_D), 1)

    def per_head(x, offset):
        return jnp.concatenate([jnp.sum(jnp.where(lane == offset + h, x[g], 0.0), axis=-1, keepdims=True)[None]
                                for g in range(G) for h in range(DN_H)], axis=0)
    beta = per_head(bg, 0)
    gcol = per_head(gc_all, DN_H)
    grow = jnp.sum(eye * gcol, axis=1, keepdims=True)
    glast = jnp.sum(jnp.where(_iota((1, C, 1), 1) == C - 1, gcol, 0.0), axis=1, keepdims=True)
    decay = jnp.exp(jnp.where(incl, gcol - grow, NEG))
    q = q * (DN_D ** -0.5)
    kb = k * beta
    lmat = jnp.where(strict, _bmm_nt(kb, k) * decay, 0.0)
    tinv = _tri_inverse(lmat) if tinv is None else _tri_inverse_known(lmat, tinv)
    egc = jnp.exp(gcol)
    u = _bmm(tinv, v * beta)
    w = _bmm(tinv, kb * egc)
    a = _bmm_nt(q, k) * decay
    return u, w, q * egc, k * jnp.exp(glast - gcol), a, jnp.exp(glast), tinv


@jax.custom_vjp
def _bmm_tn(a, b):
    return _bdg(a, b, 1, 1)


def _bmm_tn_fwd(a, b):
    return _bdg(a, b, 1, 1), (a, b)


def _bmm_tn_bwd(res, dy):
    a, b = res
    return _bdg(b, dy, 2, 2), _bdg(a, dy, 2, 1)


_bmm_tn.defvjp(_bmm_tn_fwd, _bmm_tn_bwd)


def _dn_rec(state, u, w, qd, kd, a, cd):
    v_new = u - _bmm(w, state)
    out = _bmm(qd, state) + _bmm(a, v_new)
    return state * cd + _bmm_tn(kd, v_new), out


def _mix_tile(o_attn, o_raw, zs, ga, gb, x, gate1, dn_g, wb_a, wb_d, w_out, p_ya, p_yd, p_out):
    o_dn = jnp.concatenate([_rms(o_raw[h], dn_g) * _silu(zs[h]) for h in range(DN_H)], axis=-1)
    y_a = _mm(o_attn, wb_a) + p_ya
    y_d = _mm(o_dn, wb_d) + p_yd
    merged = _sigmoid(ga) * y_a + _sigmoid(gb) * y_d
    out = _mm(merged, w_out) + p_out
    return x + gate1 * out, o_dn, merged


def _mlp_tile(x1, gain, shift, scale, gate2, w_gu, w_dn, tgt, p_gu, p_yy):
    h2 = _norm_mod(x1, gain, shift, scale)
    gu = jnp.concatenate([_mm(h2, w) for w in w_gu], axis=-1) + p_gu
    act = _silu(gu[:, :FFN]) * gu[:, FFN:]
    yy = _mm(act, w_dn) + p_yy
    y = x1 + gate2 * yy
    err = y - tgt
    return 0.5 * jnp.sum(err * err) * (1.0 / D), (h2, act)


def _tok(bt, f):
    return pl.BlockSpec((None, bt, f), lambda b, i: (b, i, 0))


def _full(shape):
    return pl.BlockSpec(shape, lambda b, i: (0,) * len(shape))


def _resident(shape):
    return pl.BlockSpec(shape, lambda b, i: (0,) * len(shape), pipeline_mode=pl.Buffered(1))


def _per_batch(f):
    return pl.BlockSpec((None, 1, f), lambda b, i: (b, 0, 0))


def _sds(shape, dtype):
    return jax.ShapeDtypeStruct(shape, dtype)


class _Exchange(NamedTuple):
    ins: tuple
    out_shapes: tuple
    n_remote: int
    plan: Callable
    n_forward: int = 0
    forward: Callable = None


def _remote_copies(remote, send_sems, recv_sems):
    return [pltpu.make_async_remote_copy(src_ref=src, dst_ref=dst, send_sem=send_sems.at[i], recv_sem=recv_sems.at[i],
                                         device_id=peer, device_id_type=pl.DeviceIdType.MESH)
            for i, (src, dst, peer) in enumerate(remote)]


def _exchange_copies(ex, in_refs, out_refs, send_sems, recv_sems):
    remote = ex.plan((lax.axis_index("x"), lax.axis_index("y"), lax.axis_index("c")), in_refs, out_refs)
    assert len(remote) == ex.n_remote
    return _remote_copies(remote, send_sems, recv_sems)


def _forward_copies(ex, out_refs, send_sems, recv_sems):
    remote = ex.forward((lax.axis_index("x"), lax.axis_index("y"), lax.axis_index("c")), out_refs)
    assert len(remote) == ex.n_forward
    return _remote_copies(remote, send_sems, recv_sems)


def _exchange_sems(ex):
    sems = [pltpu.SemaphoreType.DMA((ex.n_remote,)), pltpu.SemaphoreType.DMA((ex.n_remote,))]
    if ex.n_forward:
        sems += [pltpu.SemaphoreType.DMA((ex.n_forward,)), pltpu.SemaphoreType.DMA((ex.n_forward,))]
    return sems


def _hosted_call(body, name, grid, in_specs, out_specs, out_shape, scratch_shapes, semantics, ins, ex=None):
    if ex is None:
        outs = pl.pallas_call(body, name=name, grid=grid, in_specs=in_specs, out_specs=out_specs, out_shape=out_shape,
                              scratch_shapes=scratch_shapes,
                              compiler_params=_cparams(dimension_semantics=semantics))(*ins)
        return outs, ()
    n_in, n_out, n_scr = len(ins), len(out_shape), len(scratch_shapes)
    c_in, c_out = len(ex.ins), len(ex.out_shapes)
    steps = 1
    for g in grid:
        steps *= g

    def wrapped(*refs):
        a, b, c, d = n_in, n_in + c_in, n_in + c_in + n_out, n_in + c_in + n_out + c_out
        scratch, sems = refs[d:d + n_scr], refs[d + n_scr:]
        step = 0
        for axis, g in enumerate(grid):
            step = step * g + pl.program_id(axis)

        def first_phase():
            return _exchange_copies(ex, refs[a:b], refs[c:d], sems[0], sems[1])

        @pl.when(step == 0)
        def _():
            for cp in first_phase():
                cp.start()
        body(*refs[:a], *refs[b:c], *scratch)

        if ex.n_forward:
            @pl.when(step == (3 * steps) // 4)
            def _():
                for cp in first_phase():
                    cp.wait_recv()
                for cp in _forward_copies(ex, refs[c:d], sems[2], sems[3]):
                    cp.start()

        @pl.when(step == steps - 1)
        def _():
            cps = first_phase()
            if ex.n_forward:
                fwd = _forward_copies(ex, refs[c:d], sems[2], sems[3])
                for cp in fwd:
                    cp.wait_recv()
                for cp in cps + fwd:
                    cp.wait_send()
            else:
                for cp in cps:
                    cp.wait_recv()
                for cp in cps:
                    cp.wait_send()

    any_spec = pl.BlockSpec(memory_space=pl.ANY)
    res = pl.pallas_call(
        wrapped, name=name, grid=grid, in_specs=list(in_specs) + [any_spec] * c_in,
        out_specs=list(out_specs) + [any_spec] * c_out, out_shape=list(out_shape) + list(ex.out_shapes),
        scratch_shapes=list(scratch_shapes) + _exchange_sems(ex),
        compiler_params=_cparams(dimension_semantics=("arbitrary",) * len(grid)),
    )(*ins, *ex.ins)
    return res[:n_out], res[n_out:]


def _acc(ref, val, first):
    @pl.when(first)
    def _():
        ref[...] = val

    @pl.when(jnp.logical_not(first))
    def _():
        ref[...] += val


def _in_proj(x, mod, norm1_g, w_in, bt):
    B, S, _ = x.shape

    def body(x_ref, mod_ref, g_ref, w_ref, q_ref, kv_ref, dn_ref, z_ref, ga_ref, gb_ref, ba_ref, h_ref):
        h = _norm_mod(x_ref[...], g_ref[...], mod_ref[:, 0:D], mod_ref[:, D:2 * D]).astype(BF)
        h_ref[...] = h

        def proj(c0, c1):
            return jnp.dot(h, w_ref[:, c0:c1], preferred_element_type=F32)
        q_ref[...] = proj(0, C_KV).astype(BF)
        kv_ref[...] = proj(C_KV, C_DN).astype(BF)
        dn_ref[...] = proj(C_DN, C_Z).astype(BF)
        z_ref[...] = proj(C_Z, C_GA).astype(BF)
        ga_ref[...] = proj(C_GA, C_GB).astype(BF)
        gb_ref[...] = proj(C_GB, C_BA).astype(BF)
        ba_ref[...] = proj(C_BA, IN_PAD)

    widths = (QW, 2 * KVW, CONVW, DNW, D, D)
    return pl.pallas_call(
        body, name="in_proj", grid=(B, S // bt),
        in_specs=[_tok(bt, D), _per_batch(6 * D), _full((1, D)), _full((D, IN_PAD))],
        out_specs=[_tok(bt, w) for w in widths] + [_tok(bt, 128), _tok(bt, D)],
        out_shape=[_sds((B, S, w), BF) for w in widths] + [_sds((B, S, 128), F32), _sds((B, S, D), BF)],
        compiler_params=_cparams(dimension_semantics=("parallel", "parallel")),
    )(x, mod, norm1_g, w_in)


def _prev_blk(bt, f):
    return pl.BlockSpec((None, bt, f), lambda b, i: (b, jnp.maximum(i - 1, 0), 0))


QKV = QW + 2 * KVW


def _qk_slabs(q_ref, kv_ref):
    return ([q_ref[:, j * 2 * HD:(j + 1) * 2 * HD].astype(F32) for j in range(QW // (2 * HD))],
            [kv_ref[:, 0:KVW].astype(F32)])


def _qk_prep_fwd(q, kv, cos, sin, qg, kg, bt):
    B, S, _ = q.shape

    def body(q_ref, kv_ref, cos_ref, sin_ref, qg_ref, kg_ref, o_ref):
        qs, ks = _qk_slabs(q_ref, kv_ref)
        qn = _qk_prep(qs, qg_ref[...], cos_ref[...], sin_ref[...])
        kn = _qk_prep(ks, kg_ref[...], cos_ref[...], sin_ref[...])
        for j, t in enumerate(qn + kn):
            o_ref[:, j * 2 * HD:(j + 1) * 2 * HD] = t.astype(BF)
        o_ref[:, QW + KVW:QKV] = kv_ref[:, KVW:2 * KVW]

    return pl.pallas_call(
        body, name="qk_prep_fwd", grid=(B, S // bt),
        in_specs=[_tok(bt, QW), _tok(bt, 2 * KVW), _tok(bt, 2 * HD), _tok(bt, 2 * HD), _full((1, 2 * HD)), _full((1, 2 * HD))],
        out_specs=_tok(bt, QKV), out_shape=_sds((B, S, QKV), BF),
        compiler_params=_cparams(dimension_semantics=("parallel", "parallel")),
    )(q, kv, cos, sin, qg, kg)


def _qk_prep_bwd(q, kv, cos, sin, qg, kg, dqn, dkvn, bt):
    B, S, _ = q.shape

    def body(q_ref, kv_ref, cos_ref, sin_ref, qg_ref, kg_ref, dqn_ref, dkvn_ref, dq_ref, dkv_ref, dqg_ref, dkg_ref):
        qs, ks = _qk_slabs(q_ref, kv_ref)
        cos, sin = cos_ref[...], sin_ref[...]

        def f(qs, ks, qg, kg):
            return _qk_prep(qs, qg, cos, sin), _qk_prep(ks, kg, cos, sin)
        _, vjp = jax.vjp(f, qs, ks, qg_ref[...], kg_ref[...])
        n_q = len(qs)
        d_q = [dqn_ref[:, j * 2 * HD:(j + 1) * 2 * HD].astype(F32) for j in range(n_q)]
        d_k = [dkvn_ref[:, 0:KVW].astype(F32)]
        dqs, dks, dqg, dkg = vjp((d_q, d_k))
        for j in range(n_q):
            dq_ref[:, j * 2 * HD:(j + 1) * 2 * HD] = dqs[j].astype(BF)
        dkv_ref[:, 0:KVW] = dks[0].astype(BF)
        dkv_ref[:, KVW:2 * KVW] = dkvn_ref[:, KVW:2 * KVW]
        first = (pl.program_id(0) == 0) & (pl.program_id(1) == 0)
        _acc(dqg_ref, dqg, first)
        _acc(dkg_ref, dkg, first)

    return pl.pallas_call(
        body, name="qk_prep_bwd", grid=(B, S // bt),
        in_specs=[_tok(bt, QW), _tok(bt, 2 * KVW), _tok(bt, 2 * HD), _tok(bt, 2 * HD), _full((1, 2 * HD)), _full((1, 2 * HD)),
                  _tok(bt, QW), _tok(bt, 2 * KVW)],
        out_specs=[_tok(bt, QW), _tok(bt, 2 * KVW), _full((1, 2 * HD)), _full((1, 2 * HD))],
        out_shape=[_sds((B, S, QW), BF), _sds((B, S, 2 * KVW), BF), _sds((1, 2 * HD), F32), _sds((1, 2 * HD), F32)],
        compiler_params=_cparams(dimension_semantics=("arbitrary", "arbitrary")),
    )(q, kv, cos, sin, qg, kg, dqn, dkvn)


def _attn_load(qkv_ref, kvp_ref):
    qs = [qkv_ref[:, h * HD:(h + 1) * HD].astype(F32) for h in range(HEADS)]
    kc = [qkv_ref[:, QW + h * HD:QW + (h + 1) * HD].astype(F32) for h in range(KV_HEADS)]
    vc = [qkv_ref[:, QW + KVW + h * HD:QW + KVW + (h + 1) * HD].astype(F32) for h in range(KV_HEADS)]
    kp = [kvp_ref[:, h * HD:(h + 1) * HD].astype(F32) for h in range(KV_HEADS)]
    vp = [kvp_ref[:, KVW + h * HD:KVW + (h + 1) * HD].astype(F32) for h in range(KV_HEADS)]
    return qs, kc, kp, vc, vp


def _kv_prev_spec(index):
    return pl.BlockSpec((None, BLK, 2 * KVW), lambda b, i: (b, index(i), QW // (2 * KVW)))


def _attn_fwd(qkv, sinks):
    B, S, _ = qkv.shape

    def body(qkv_ref, kvp_ref, sk_ref, o_ref):
        qs, kc, kp, vc, vp = _attn_load(qkv_ref, kvp_ref)
        outs = _attn_block(qs, kc, kp, vc, vp, sk_ref[...], pl.program_id(1) > 0)
        for h in range(HEADS):
            o_ref[:, h * HD:(h + 1) * HD] = outs[h].astype(BF)

    return pl.pallas_call(
        body, name="attn_fwd", grid=(B, S // BLK),
        in_specs=[_tok(BLK, QKV), _kv_prev_spec(lambda i: jnp.maximum(i - 1, 0)), _full((1, HEADS))],
        out_specs=_tok(BLK, QW), out_shape=_sds((B, S, QW), BF),
        compiler_params=_cparams(dimension_semantics=("parallel", "parallel")),
    )(qkv, qkv, sinks)


def _conv_fwd_tile(xe_ref, x_ref, halo_ref, cw_ref, first, bt):
    halo = halo_ref[...].astype(F32)
    xe_ref[0:8, :] = jnp.where(first, 0.0, halo)
    xe_ref[8:bt + 8, :] = x_ref[...].astype(F32)
    y = cw_ref[0:1, :] * xe_ref[5:bt + 5, :]
    for j in range(1, CONV):
        y = y + cw_ref[j:j + 1, :] * xe_ref[5 + j:bt + 5 + j, :]
    return y


def _halo_spec(bt):
    return pl.BlockSpec((None, 8, CONVW), lambda b, i: (b, jnp.maximum(i * (bt // 8) - 1, 0), 0))


def _dn_prep(dn, ba, conv_w, alog, dtb, bt):
    B, S, _ = dn.shape

    def body(x_ref, halo_ref, ba_ref, cw_ref, al_ref, dt_ref, qkv_ref, bg_ref, y_ref, xe_ref):
        y = _conv_fwd_tile(xe_ref, x_ref, halo_ref, cw_ref, pl.program_id(1) == 0, bt)
        y_ref[...] = y.astype(BF)
        ys = [y[:, j * DN_D:(j + 1) * DN_D] for j in range(3 * DN_H)]
        out, bg = _dn_tail(ys, ba_ref[...], al_ref[...], dt_ref[...])
        for j in range(3 * DN_H):
            qkv_ref[:, j * DN_D:(j + 1) * DN_D] = out[j]
        bg_ref[...] = bg

    return pl.pallas_call(
        body, name="dn_prep", grid=(B, S // bt),
        in_specs=[_tok(bt, CONVW), _halo_spec(bt), _tok(bt, 128), _full((CONV, CONVW)), _full((1, 128)), _full((1, 128))],
        out_specs=[_tok(bt, CONVW), _tok(bt, 128), _tok(bt, CONVW)],
        out_shape=[_sds((B, S, CONVW), F32), _sds((B, S, 128), F32), _sds((B, S, CONVW), BF)],
        scratch_shapes=[pltpu.VMEM((bt + 8, CONVW), F32)],
        compiler_params=_cparams(dimension_semantics=("parallel", "arbitrary")),
    )(dn, dn, ba, conv_w, alog, dtb)


def _dn_load(qkv_ref):
    qs = [qkv_ref[:, h * DN_D:(h + 1) * DN_D] for h in range(DN_H)]
    ks = [qkv_ref[:, DNW + h * DN_D:DNW + (h + 1) * DN_D] for h in range(DN_H)]
    vs = [qkv_ref[:, 2 * DNW + h * DN_D:2 * DNW + (h + 1) * DN_D] for h in range(DN_H)]
    return qs, ks, vs


DN_GROUP = 4
AW = DN_H * CHUNK


def _stack_heads(ref, G, offset, width):
    return jnp.stack([ref[g * CHUNK:(g + 1) * CHUNK, offset + h * width:offset + (h + 1) * width]
                      for g in range(G) for h in range(DN_H)])


def _dn_load_stack(qkv_ref, G):
    return tuple(_stack_heads(qkv_ref, G, j * DNW, DN_D) for j in range(3))


def _cd_spec(n):
    return pl.BlockSpec((None, n, 1, DN_D), lambda b, i: (b, i, 0, 0))


def _dn_intra_fwd(qkv, bg, ex=None):
    B, S, _ = qkv.shape
    nc = S // CHUNK
    G = min(DN_GROUP, nc)
    rows = G * CHUNK

    def body(qkv_ref, bg_ref, u_ref, w_ref, qd_ref, kd_ref, a_ref, cd_ref, t_ref):
        q, k, v = _dn_load_stack(qkv_ref, G)
        u, w, qd, kd, a, cd, tinv = _dn_intra(q, k, v, bg_ref[...].reshape(G, CHUNK, DN_D))
        lane_row = _iota((1, DN_D), 1)
        for g in range(G):
            rows = slice(g * CHUNK, (g + 1) * CHUNK)
            cd_row = jnp.zeros((1, DN_D), F32)
            for h in range(DN_H):
                n = g * DN_H + h
                cols = slice(h * DN_D, (h + 1) * DN_D)
                u_ref[rows, cols] = u[n]
                w_ref[rows, cols] = w[n].astype(BF)
                qd_ref[rows, cols] = qd[n].astype(BF)
                kd_ref[rows, cols] = kd[n].astype(BF)
                a_ref[rows, h * CHUNK:(h + 1) * CHUNK] = a[n].astype(BF)
                t_ref[rows, h * CHUNK:(h + 1) * CHUNK] = tinv[n]
                cd_row = cd_row + jnp.where(lane_row == h, cd[n], 0.0)
            cd_ref[g] = cd_row

    return _hosted_call(
        body, "dn_intra_fwd", (B, nc // G),
        in_specs=[_tok(rows, CONVW), _tok(rows, 128)],
        out_specs=[_tok(rows, DNW)] * 4 + [_tok(rows, AW), _cd_spec(G), _tok(rows, AW)],
        out_shape=[_sds((B, S, DNW), F32)] + [_sds((B, S, DNW), BF)] * 3 + [_sds((B, S, AW), BF), _sds((B, nc, 1, DN_D), F32),
                                                                            _sds((B, S, AW), F32)],
        scratch_shapes=[], semantics=("parallel", "parallel"), ins=(qkv, bg), ex=ex)


def _rec_stack(ref, B, width):
    return jnp.stack([ref[b, :, h * width:(h + 1) * width].astype(F32) for b in range(B) for h in range(DN_H)])


def _rec_load(B, u_ref, w_ref, qd_ref, kd_ref, a_ref, cd_ref):
    lane_row = _iota((1, DN_D), 1)
    cd = jnp.stack([jnp.sum(jnp.where(lane_row == h, cd_ref[b, 0], 0.0), axis=-1, keepdims=True)
                    for b in range(B) for h in range(DN_H)])
    return (_rec_stack(u_ref, B, DN_D), _rec_stack(w_ref, B, DN_D), _rec_stack(qd_ref, B, DN_D),
            _rec_stack(kd_ref, B, DN_D), _rec_stack(a_ref, B, CHUNK), cd)


def _rec_store(B, ref, val, width):
    for b in range(B):
        for h in range(DN_H):
            ref[b, :, h * width:(h + 1) * width] = val[b * DN_H + h]


def _rec_specs(B, index):
    def tok(f):
        return pl.BlockSpec((B, CHUNK, f), lambda i: (0, index(i), 0))
    cd = pl.BlockSpec((B, 1, 1, DN_D), lambda i: (0, index(i), 0, 0))
    st = pl.BlockSpec((B, None, DN_H, DN_D, DN_D), lambda i: (0, index(i), 0, 0, 0))
    return tok, cd, st


def _dn_rec_fwd(u, w, qd, kd, a, cd, ex=None):
    B, S, _ = u.shape
    nc = S // CHUNK
    tok, cd_spec, st_spec = _rec_specs(B, lambda i: i)

    def body(u_ref, w_ref, qd_ref, kd_ref, a_ref, cd_ref, o_ref, st_ref, s_ref):
        @pl.when(pl.program_id(0) == 0)
        def _():
            s_ref[...] = jnp.zeros_like(s_ref)
        state = s_ref[...]
        st_ref[...] = state.reshape(B, DN_H, DN_D, DN_D)
        new_state, out = _dn_rec(state, *_rec_load(B, u_ref, w_ref, qd_ref, kd_ref, a_ref, cd_ref))
        s_ref[...] = new_state
        _rec_store(B, o_ref, out, DN_D)

    return _hosted_call(
        body, "dn_rec_fwd", (nc,),
        in_specs=[tok(DNW)] * 4 + [tok(AW), cd_spec],
        out_specs=[tok(DNW), st_spec],
        out_shape=[_sds((B, S, DNW), F32), _sds((B, nc, DN_H, DN_D, DN_D), F32)],
        scratch_shapes=[pltpu.VMEM((B * DN_H, DN_D, DN_D), F32)],
        semantics=("arbitrary",), ins=(u, w, qd, kd, a, cd), ex=ex)


def _mix_load(oa_ref, or_ref, z_ref):
    o_raw = [or_ref[:, h * DN_D:(h + 1) * DN_D] for h in range(DN_H)]
    zs = [z_ref[:, h * DN_D:(h + 1) * DN_D].astype(F32) for h in range(DN_H)]
    return oa_ref[...].astype(F32), o_raw, zs


def _mix_fwd(o_attn, o_raw, z, ga, gb, x, mod, dn_g, w_branch, w_out, bt):
    B, S, _ = x.shape

    def body(oa_ref, or_ref, z_ref, ga_ref, gb_ref, x_ref, mod_ref, dg_ref, wb_ref, wo_ref, x1_ref, od_ref, mg_ref):
        oa, o_r, zs = _mix_load(oa_ref, or_ref, z_ref)
        x1, o_dn, merged = _mix_tile(oa, o_r, zs, ga_ref[...].astype(F32), gb_ref[...].astype(F32), x_ref[...],
                                     mod_ref[:, 2 * D:3 * D], dg_ref[...], wb_ref[0:QW, :], wb_ref[QW:2 * QW, :],
                                     wo_ref[...], 0.0, 0.0, 0.0)
        x1_ref[...] = x1
        od_ref[...] = o_dn.astype(BF)
        mg_ref[...] = merged.astype(BF)

    return pl.pallas_call(
        body, name="mix_fwd", grid=(B, S // bt),
        in_specs=[_tok(bt, QW), _tok(bt, DNW), _tok(bt, DNW), _tok(bt, D), _tok(bt, D), _tok(bt, D), _per_batch(6 * D),
                  _full((1, DN_D)), _full((D, D)), _full((D, D))],
        out_specs=[_tok(bt, D), _tok(bt, DNW), _tok(bt, D)],
        out_shape=[_sds((B, S, D), F32), _sds((B, S, DNW), BF), _sds((B, S, D), BF)],
        compiler_params=_cparams(dimension_semantics=("parallel", "parallel")),
    )(o_attn, o_raw, z, ga, gb, x, mod, dn_g, w_branch, w_out)


def _mlp(x1, tgt, mod, norm2_g, w_gu, w_dn, bt):
    B, S, _ = x1.shape

    def body(x1_ref, t_ref, mod_ref, g_ref, wgu_ref, wdn_ref,
             dx1_ref, h2_ref, act_ref, dgu_ref, dyy_ref, loss_ref, dmod_ref, dg_ref):
        w_gu_v, w_dn_v, t = [wgu_ref[k] for k in range(N_CHIP)], wdn_ref[...], t_ref[...]

        def f(x1, gain, shift, scale, gate2, p_gu, p_yy):
            return _mlp_tile(x1, gain, shift, scale, gate2, w_gu_v, w_dn_v, t, p_gu, p_yy)
        zero_gu = jnp.zeros((bt, 2 * FFN), F32)
        zero_yy = jnp.zeros((bt, D), F32)
        loss, vjp, (h2, act) = jax.vjp(f, x1_ref[...], g_ref[...], mod_ref[:, 3 * D:4 * D], mod_ref[:, 4 * D:5 * D],
                                       mod_ref[:, 5 * D:6 * D], zero_gu, zero_yy, has_aux=True)
        dx1, dgain, dshift, dscale, dgate2, dgu, dyy = vjp(jnp.ones((), F32))
        dx1_ref[...] = dx1
        h2_ref[...] = h2.astype(BF)
        act_ref[...] = act.astype(BF)
        dgu_ref[...] = dgu.astype(BF)
        dyy_ref[...] = dyy.astype(BF)
        first = pl.program_id(1) == 0
        _acc(loss_ref, jnp.reshape(loss, (1, 1)), first)
        _acc(dmod_ref, jnp.concatenate([dshift, dscale, dgate2], axis=-1), first)
        _acc(dg_ref, dgain, first)

    return pl.pallas_call(
        body, name="mlp", grid=(B, S // bt),
        in_specs=[_tok(bt, D), _tok(bt, D), _per_batch(6 * D), _full((1, D)), _resident((N_CHIP, D, 2 * FFN // N_CHIP)),
                  _resident((FFN, D))],
        out_specs=[_tok(bt, D), _tok(bt, D), _tok(bt, FFN), _tok(bt, 2 * FFN), _tok(bt, D),
                   _per_batch(1), _per_batch(3 * D), _per_batch(D)],
        out_shape=[_sds((B, S, D), F32), _sds((B, S, D), BF), _sds((B, S, FFN), BF), _sds((B, S, 2 * FFN), BF),
                   _sds((B, S, D), BF), _sds((B, 1, 1), F32), _sds((B, 1, 3 * D), F32), _sds((B, 1, D), F32)],
        compiler_params=_cparams(dimension_semantics=("parallel", "arbitrary")),
    )(x1, tgt, mod, norm2_g, w_gu, w_dn)


def _mix_bwd(o_attn, o_raw, z, ga, gb, x, mod, dn_g, w_branch, w_out, dx1, bt, ex=None):
    B, S, _ = x.shape

    def body(oa_ref, or_ref, z_ref, ga_ref, gb_ref, x_ref, mod_ref, dg_ref, wb_ref, wo_ref, dx1_ref,
             doa_ref, dor_ref, dz_ref, dga_ref, dgb_ref, dya_ref, dyd_ref, dout_ref, dgate_ref, ddg_ref):
        oa, o_r, zs = _mix_load(oa_ref, or_ref, z_ref)
        wb_a, wb_d, wo = wb_ref[0:QW, :], wb_ref[QW:2 * QW, :], wo_ref[...]

        def f(oa, o_r, zs, ga, gb, gate1, dn_g, p_ya, p_yd, p_out):
            return _mix_tile(oa, o_r, zs, ga, gb, x_ref[...], gate1, dn_g, wb_a, wb_d, wo, p_ya, p_yd, p_out)[0]
        zero = jnp.zeros((bt, D), F32)
        _, vjp = jax.vjp(f, oa, o_r, zs, ga_ref[...].astype(F32), gb_ref[...].astype(F32), mod_ref[:, 2 * D:3 * D],
                         dg_ref[...], zero, zero, zero)
        doa, dor, dzs, dga, dgb, dgate1, ddn_g, dya, dyd, dout = vjp(dx1_ref[...])
        doa_ref[...] = doa
        for h in range(DN_H):
            dor_ref[:, h * DN_D:(h + 1) * DN_D] = dor[h]
            dz_ref[:, h * DN_D:(h + 1) * DN_D] = dzs[h].astype(BF)
        dga_ref[...] = dga.astype(BF)
        dgb_ref[...] = dgb.astype(BF)
        dya_ref[...] = dya.astype(BF)
        dyd_ref[...] = dyd.astype(BF)
        dout_ref[...] = dout.astype(BF)
        first = pl.program_id(1) == 0
        _acc(dgate_ref, dgate1, first)
        _acc(ddg_ref, ddn_g, first)

    return _hosted_call(
        body, "mix_bwd", (B, S // bt),
        in_specs=[_tok(bt, QW), _tok(bt, DNW), _tok(bt, DNW), _tok(bt, D), _tok(bt, D), _tok(bt, D), _per_batch(6 * D),
                  _full((1, DN_D)), _full((D, D)), _full((D, D)), _tok(bt, D)],
        out_specs=[_tok(bt, QW), _tok(bt, DNW), _tok(bt, DNW), _tok(bt, D), _tok(bt, D), _tok(bt, D), _tok(bt, D), _tok(bt, D),
                   _per_batch(D), _per_batch(DN_D)],
        out_shape=[_sds((B, S, QW), F32), _sds((B, S, DNW), F32), _sds((B, S, DNW), BF), _sds((B, S, D), BF),
                   _sds((B, S, D), BF), _sds((B, S, D), BF), _sds((B, S, D), BF), _sds((B, S, D), BF),
                   _sds((B, 1, D), F32), _sds((B, 1, DN_D), F32)],
        scratch_shapes=[], semantics=("parallel", "arbitrary"),
        ins=(o_attn, o_raw, z, ga, gb, x, mod, dn_g, w_branch, w_out, dx1), ex=ex)


def _dn_rec_bwd(u, w, qd, kd, a, cd, states, d_o, ex=None):
    B, S, _ = u.shape
    nc = S // CHUNK
    tok, cd_spec, st_spec = _rec_specs(B, lambda i: nc - 1 - i)

    def body(u_ref, w_ref, qd_ref, kd_ref, a_ref, cd_ref, st_ref, do_ref,
             du_ref, dw_ref, dqd_ref, dkd_ref, da_ref, dcd_ref, ds_ref):
        @pl.when(pl.program_id(0) == 0)
        def _():
            ds_ref[...] = jnp.zeros_like(ds_ref)
        state = st_ref[...].reshape(B * DN_H, DN_D, DN_D)
        _, vjp = jax.vjp(_dn_rec, state, *_rec_load(B, u_ref, w_ref, qd_ref, kd_ref, a_ref, cd_ref))
        dst, du, dw, dqd, dkd, da, dcd = vjp((ds_ref[...], _rec_stack(do_ref, B, DN_D)))
        ds_ref[...] = dst
        for ref, val, width in ((du_ref, du, DN_D), (dw_ref, dw, DN_D), (dqd_ref, dqd, DN_D), (dkd_ref, dkd, DN_D),
                                (da_ref, da, CHUNK)):
            _rec_store(B, ref, val, width)
        lane_row = _iota((1, DN_D), 1)
        for b in range(B):
            row = jnp.zeros((1, DN_D), F32)
            for h in range(DN_H):
                row = row + jnp.where(lane_row == h, dcd[b * DN_H + h], 0.0)
            dcd_ref[b, 0] = row

    return _hosted_call(
        body, "dn_rec_bwd", (nc,),
        in_specs=[tok(DNW)] * 4 + [tok(AW), cd_spec, st_spec, tok(DNW)],
        out_specs=[tok(DNW)] * 4 + [tok(AW), cd_spec],
        out_shape=[_sds((B, S, DNW), F32)] * 4 + [_sds((B, S, AW), F32), _sds((B, nc, 1, DN_D), F32)],
        scratch_shapes=[pltpu.VMEM((B * DN_H, DN_D, DN_D), F32)],
        semantics=("arbitrary",), ins=(u, w, qd, kd, a, cd, states, d_o), ex=ex)


def _dn_intra_bwd(qkv, bg, tinv, du, dw, dqd, dkd, da, dcd, ex=None):
    B, S, _ = qkv.shape
    nc = S // CHUNK
    G = min(DN_GROUP, nc)
    rows = G * CHUNK

    def body(qkv_ref, bg_ref, t_ref, du_ref, dw_ref, dqd_ref, dkd_ref, da_ref, dcd_ref, dqkv_ref, dbg_ref):
        q, k, v = _dn_load_stack(qkv_ref, G)
        known = _stack_heads(t_ref, G, 0, CHUNK)
        _, vjp = jax.vjp(lambda q, k, v, bg: _dn_intra(q, k, v, bg, known)[:6], q, k, v,
                         bg_ref[...].reshape(G, CHUNK, DN_D))
        lane_row = _iota((1, DN_D), 1)
        dcd = jnp.stack([jnp.sum(jnp.where(lane_row == h, dcd_ref[g], 0.0), axis=-1, keepdims=True)
                         for g in range(G) for h in range(DN_H)])
        dq, dk, dv, dbg = vjp((_stack_heads(du_ref, G, 0, DN_D), _stack_heads(dw_ref, G, 0, DN_D),
                               _stack_heads(dqd_ref, G, 0, DN_D), _stack_heads(dkd_ref, G, 0, DN_D),
                               _stack_heads(da_ref, G, 0, CHUNK), dcd))
        for g in range(G):
            rows = slice(g * CHUNK, (g + 1) * CHUNK)
            for h in range(DN_H):
                n = g * DN_H + h
                dqkv_ref[rows, h * DN_D:(h + 1) * DN_D] = dq[n]
                dqkv_ref[rows, DNW + h * DN_D:DNW + (h + 1) * DN_D] = dk[n]
                dqkv_ref[rows, 2 * DNW + h * DN_D:2 * DNW + (h + 1) * DN_D] = dv[n]
        dbg_ref[...] = dbg.reshape(G * CHUNK, DN_D)

    return _hosted_call(
        body, "dn_intra_bwd", (B, nc // G),
        in_specs=[_tok(rows, CONVW), _tok(rows, 128), _tok(rows, AW)] + [_tok(rows, DNW)] * 4 + [_tok(rows, AW), _cd_spec(G)],
        out_specs=[_tok(rows, CONVW), _tok(rows, 128)],
        out_shape=[_sds((B, S, CONVW), F32), _sds((B, S, 128), F32)],
        scratch_shapes=[], semantics=("parallel", "parallel"), ins=(qkv, bg, tinv, du, dw, dqd, dkd, da, dcd), ex=ex)


def _dn_prep_bwd(dn, y, ba, conv_w, alog, dtb, dqkv, dbg, bt):
    B, S, _ = dn.shape
    nt = S // bt

    def rev(f):
        return pl.BlockSpec((None, bt, f), lambda b, i: (b, nt - 1 - i, 0))

    def body(x_ref, y_ref, ba_ref, cw_ref, al_ref, dt_ref, dqkv_ref, dbg_ref,
             dx_ref, dba_ref, dcw_ref, dal_ref, ddt_ref, dye_ref):
        i = pl.program_id(1)
        ys = [y_ref[:, j * DN_D:(j + 1) * DN_D].astype(F32) for j in range(3 * DN_H)]
        _, vjp = jax.vjp(_dn_tail, ys, ba_ref[...], al_ref[...], dt_ref[...])
        d_out = [dqkv_ref[:, j * DN_D:(j + 1) * DN_D] for j in range(3 * DN_H)]
        dys, dba, dal, ddt = vjp((d_out, dbg_ref[...]))
        @pl.when(i == 0)
        def _():
            dye_ref[bt:bt + 8, :] = jnp.zeros((8, CONVW), F32)

        @pl.when(i > 0)
        def _():
            dye_ref[bt:bt + 8, :] = dye_ref[0:8, :]
        for j in range(3 * DN_H):
            dye_ref[0:bt, j * DN_D:(j + 1) * DN_D] = dys[j]
        shifted = [dye_ref[3 - j:bt + 3 - j, :] for j in range(CONV)]
        dx = cw_ref[0:1, :] * shifted[0]
        for j in range(1, CONV):
            dx = dx + cw_ref[j:j + 1, :] * shifted[j]
        dx_ref[...] = dx.astype(BF)
        x = x_ref[...].astype(F32)
        dcw = jnp.concatenate([jnp.sum(shifted[j] * x, axis=0, keepdims=True) for j in range(CONV)], axis=0)
        first = (i == 0) & (pl.program_id(0) == 0)
        dba_ref[...] = dba
        _acc(dcw_ref, dcw, first)
        _acc(dal_ref, dal, first)
        _acc(ddt_ref, ddt, first)

    return pl.pallas_call(
        body, name="dn_prep_bwd", grid=(B, nt),
        in_specs=[rev(CONVW), rev(CONVW), rev(128), _full((CONV, CONVW)), _full((1, 128)), _full((1, 128)), rev(CONVW), rev(128)],
        out_specs=[rev(CONVW), rev(128), _full((CONV, CONVW)), _full((1, 128)), _full((1, 128))],
        out_shape=[_sds((B, S, CONVW), BF), _sds((B, S, 128), F32), _sds((CONV, CONVW), F32), _sds((1, 128), F32),
                   _sds((1, 128), F32)],
        scratch_shapes=[pltpu.VMEM((bt + 8, CONVW), F32)],
        compiler_params=_cparams(dimension_semantics=("arbitrary", "arbitrary")),
    )(dn, y, ba, conv_w, alog, dtb, dqkv, dbg)


def _attn_bwd(qkv, sinks, d_o):
    B, S, _ = qkv.shape
    nb = S // BLK

    def cur(f):
        return pl.BlockSpec((None, BLK, f), lambda b, i: (b, jnp.minimum(i, nb - 1), 0))

    def out_prev(f):
        return pl.BlockSpec((None, BLK, f), lambda b, i: (b, jnp.maximum(i - 1, 0), 0))

    def body(qkv_ref, kvp_ref, sk_ref, do_ref, dq_ref, dkv_ref, dsk_ref, carry_ref):
        n = pl.program_id(1)
        first = (n == 0) & (pl.program_id(0) == 0)

        @pl.when(n == 0)
        def _():
            carry_ref[...] = jnp.zeros_like(carry_ref)

        @pl.when(n < nb)
        def _():
            qs, kc, kp, vc, vp = _attn_load(qkv_ref, kvp_ref)

            def f(qs, kc, kp, vc, vp, sk):
                return _attn_block(qs, kc, kp, vc, vp, sk, n > 0)
            _, vjp = jax.vjp(f, qs, kc, kp, vc, vp, sk_ref[...])
            d_outs = [do_ref[:, h * HD:(h + 1) * HD] for h in range(HEADS)]
            dqs, dkc, dkp, dvc, dvp, dsk = vjp(d_outs)
            for h in range(HEADS):
                dq_ref[:, h * HD:(h + 1) * HD] = dqs[h].astype(BF)
            for h in range(KV_HEADS):
                ksl = slice(h * HD, (h + 1) * HD)
                vsl = slice(KVW + h * HD, KVW + (h + 1) * HD)
                dkv_ref[:, ksl] = (carry_ref[:, ksl] + dkp[h]).astype(BF)
                dkv_ref[:, vsl] = (carry_ref[:, vsl] + dvp[h]).astype(BF)
                carry_ref[:, ksl] = dkc[h]
                carry_ref[:, vsl] = dvc[h]
            _acc(dsk_ref, dsk, first)

        @pl.when(n == nb)
        def _():
            dkv_ref[...] = carry_ref[...].astype(BF)

    return pl.pallas_call(
        body, name="attn_bwd", grid=(B, nb + 1),
        in_specs=[cur(QKV), _kv_prev_spec(lambda i: jnp.maximum(jnp.minimum(i, nb - 1) - 1, 0)), _full((1, HEADS)), cur(QW)],
        out_specs=[cur(QW), out_prev(2 * KVW), _full((1, HEADS))],
        out_shape=[_sds((B, S, QW), BF), _sds((B, S, 2 * KVW), BF), _sds((1, HEADS), F32)],
        scratch_shapes=[pltpu.VMEM((BLK, 2 * KVW), F32)],
        compiler_params=_cparams(dimension_semantics=("arbitrary", "arbitrary")),
    )(qkv, qkv, sinks, d_o)


def _in_proj_bwd(x, mod, norm1_g, w_in, pieces, dba, dx1, bt):
    B, S, _ = x.shape
    widths = (QW, 2 * KVW, CONVW, DNW, D, D)

    def body(x_ref, mod_ref, g_ref, w_ref, dq_ref, dkv_ref, ddn_ref, dz_ref, dga_ref, dgb_ref, dba_ref, dx1_ref,
             gx_ref, dp_ref, dmod_ref, dg_ref):
        dp = jnp.concatenate([r[...] for r in (dq_ref, dkv_ref, ddn_ref, dz_ref, dga_ref, dgb_ref)]
                             + [dba_ref[...].astype(BF)], axis=-1)
        dp_ref[...] = dp
        dh = lax.dot_general(dp, w_ref[...], (((1,), (1,)), ((), ())), preferred_element_type=F32)
        _, vjp = jax.vjp(_norm_mod, x_ref[...], g_ref[...], mod_ref[:, 0:D], mod_ref[:, D:2 * D])
        dx, dgain, dshift, dscale = vjp(dh)
        gx_ref[...] = dx + dx1_ref[...]
        first = pl.program_id(1) == 0
        _acc(dmod_ref, jnp.concatenate([dshift, dscale], axis=-1), first)
        _acc(dg_ref, dgain, first)

    return pl.pallas_call(
        body, name="in_proj_bwd", grid=(B, S // bt),
        in_specs=[_tok(bt, D), _per_batch(6 * D), _full((1, D)), _full((D, IN_PAD))] + [_tok(bt, w) for w in widths]
        + [_tok(bt, 128), _tok(bt, D)],
        out_specs=[_tok(bt, D), _tok(bt, IN_PAD), _per_batch(2 * D), _per_batch(D)],
        out_shape=[_sds((B, S, D), F32), _sds((B, S, IN_PAD), BF), _sds((B, 1, 2 * D), F32), _sds((B, 1, D), F32)],
        compiler_params=_cparams(dimension_semantics=("parallel", "arbitrary")),
    )(x, mod, norm1_g, w_in, *pieces, dba, dx1)


def _matmul_tn(tag, a, b, bk, bn, bt, col_blocks=False):
    T, K = a.shape
    N = b.shape[1]
    nt = T // bt
    if col_blocks:
        assert bk == K
        out_spec = pl.BlockSpec((None, bk, bn), lambda i, j, t: (j, 0, 0))
        out_shape = _sds((N // bn, K, bn), F32)
    else:
        out_spec = pl.BlockSpec((bk, bn), lambda i, j, t: (i, j))
        out_shape = _sds((K, N), F32)

    def body(a_ref, b_ref, o_ref, acc_ref):
        t = pl.program_id(2)

        @pl.when(t == 0)
        def _():
            acc_ref[...] = jnp.zeros_like(acc_ref)
        acc_ref[...] += lax.dot_general(a_ref[...], b_ref[...], (((0,), (0,)), ((), ())), preferred_element_type=F32)

        @pl.when(t == nt - 1)
        def _():
            o_ref[...] = acc_ref[...]

    return pl.pallas_call(
        body, name=f"grad_{tag}", grid=(K // bk, N // bn, nt),
        in_specs=[pl.BlockSpec((bt, bk), lambda i, j, t: (t, i)), pl.BlockSpec((bt, bn), lambda i, j, t: (t, j))],
        out_specs=out_spec, out_shape=out_shape,
        scratch_shapes=[pltpu.VMEM((bk, bn), F32)],
        compiler_params=_cparams(dimension_semantics=("parallel", "parallel", "arbitrary")),
    )(a, b)


def _rope_table(positions):
    inv_freq = THETA ** (-jnp.arange(0, ROT, 2, dtype=F32) / ROT)
    rest = jnp.zeros((HD - ROT,), F32)
    freq = jnp.concatenate([inv_freq, inv_freq, rest] * 2)
    sign = jnp.concatenate([-jnp.ones_like(inv_freq), jnp.ones_like(inv_freq), rest] * 2)
    ang = positions.astype(F32)[..., None] * freq
    return jnp.cos(ang), jnp.sin(ang) * sign


def _lane_pad(v, offset, width=128):
    return jnp.zeros((1, width), F32).at[0, offset:offset + v.shape[-1]].set(v.reshape(-1))


def _tile(S, want):
    return min(S, want)


class _Hosted:
    def __init__(self, call):
        self.call = call
        self.outs = None

    def __call__(self, ex):
        self.outs, landed = self.call(ex)
        return landed


def _local_step(x, mod, positions, tgt, norm1_g, w_in_pad, conv_w, q_norm_g, k_norm_g, sinks, a_log, dt_bias,
                dn_norm_g, w_branch, w_out, norm2_g, w_gu, w_dn, dist=None):
    B, S, _ = x.shape
    T = B * S
    cos_t, sin_t = _rope_table(positions)
    qg2 = jnp.concatenate([q_norm_g, q_norm_g], axis=-1)
    kg2 = jnp.concatenate([k_norm_g, k_norm_g], axis=-1)
    alog = _lane_pad(a_log, DN_H)
    dtb = _lane_pad(dt_bias, DN_H)
    conv2 = conv_w.reshape(CONV, CONVW)
    bt = _tile(S, 512)
    bt_mlp = _tile(S, 256)

    q, kv, dn, z, ga, gb, ba, h1 = _in_proj(x, mod, norm1_g, w_in_pad, bt)
    qkv_n = _qk_prep_fwd(q, kv, cos_t, sin_t, qg2, kg2, bt)
    o_attn = _attn_fwd(qkv_n, sinks)
    dqkv, bg, dn_y = _dn_prep(dn, ba, conv2, alog, dtb, bt)
    intra = _Hosted(lambda ex: _dn_intra_fwd(dqkv, bg, ex))
    if dist is None:
        intra(None)
    else:
        f_br, f_out, w_gu, f_dn = _gather_weights("late", [w_branch, w_out, w_gu, w_dn], dist[0], host=intra)
        w_branch, w_out, w_dn = (f.reshape(N_CHIP * f.shape[1], f.shape[2]) for f in (f_br, f_out, f_dn))
    dn_u, dn_w, dn_qd, dn_kd, dn_a, dn_cd, dn_tinv = intra.outs
    (o_raw, states), _ = _dn_rec_fwd(dn_u, dn_w, dn_qd, dn_kd, dn_a, dn_cd)
    x1, o_dn, merged = _mix_fwd(o_attn, o_raw, z, ga, gb, x, mod, dn_norm_g, w_branch, w_out, bt)
    dx1, h2, act, dgu, dyy, loss, dmod2, dnorm2 = _mlp(x1, tgt, mod, norm2_g, w_gu, w_dn, bt_mlp)

    def flat(t):
        return t.reshape(T, t.shape[-1])
    tn = functools.partial(_matmul_tn, bt=_tile(T, 1024))
    g_w_dn = tn("w_down", flat(act), flat(dyy), bk=FFN, bn=D // 2)
    g_w_gu = tn("w_gate_up", flat(h2), flat(dgu), bk=D, bn=2 * FFN // N_CHIP, col_blocks=True)

    mix_b = _Hosted(lambda ex: _mix_bwd(o_attn, o_raw, z, ga, gb, x, mod, dn_norm_g, w_branch, w_out, dx1, bt_mlp, ex))
    rec_b = _Hosted(lambda ex: _dn_rec_bwd(dn_u, dn_w, dn_qd, dn_kd, dn_a, dn_cd, states, mix_b.outs[1], ex))
    intra_b = _Hosted(lambda ex: _dn_intra_bwd(dqkv, bg, dn_tinv, *rec_b.outs, ex))
    if dist is None:
        for host in (mix_b, rec_b, intra_b):
            host(None)
    else:
        g_w_gu, g_w_dn = _reduce_grads(("w_gate_up", "w_down"), [g_w_gu, g_w_dn.reshape(N_CHIP, -1, D)], *dist,
                                       hosts=[mix_b, rec_b, intra_b])
    d_oa, _, dz, dga, dgb, dya, dyd, dout, dgate1, ddn_g = mix_b.outs
    d_dqkv, dbg = intra_b.outs
    g_w_out = tn("w_out", flat(merged), flat(dout), bk=D, bn=D)
    g_w_br = jnp.concatenate([tn("w_branch_attn", flat(o_attn), flat(dya), bk=QW, bn=D),
                              tn("w_branch_dn", flat(o_dn), flat(dyd), bk=DNW, bn=D)], axis=0)
    d_dn, dba, dconv, dalog, ddtb = _dn_prep_bwd(dn, dn_y, ba, conv2, alog, dtb, d_dqkv, dbg, bt)
    dqn, dkvn, dsk = _attn_bwd(qkv_n, sinks, d_oa)
    dq, dkv, dqg2, dkg2 = _qk_prep_bwd(q, kv, cos_t, sin_t, qg2, kg2, dqn, dkvn, bt)
    dqg = dqg2[:, :HD] + dqg2[:, HD:]
    dkg = dkg2[:, :HD] + dkg2[:, HD:]
    grad_x, dproj, dmod1, dnorm1 = _in_proj_bwd(x, mod, norm1_g, w_in_pad, (dq, dkv, d_dn, dz, dga, dgb), dba, dx1, bt)
    g_w_in = tn("w_in", flat(h1), flat(dproj), bk=D, bn=IN_PAD // 3)

    dmod = jnp.concatenate([dmod1, dgate1, dmod2], axis=-1)
    small = dict(norm1_g=jnp.sum(dnorm1, axis=0), norm2_g=jnp.sum(dnorm2, axis=0), q_norm_g=dqg, k_norm_g=dkg,
                 sinks=dsk, a_log=dalog[:, DN_H:2 * DN_H], dt_bias=ddtb[:, DN_H:2 * DN_H],
                 dn_norm_g=jnp.sum(ddn_g, axis=0), conv_w=dconv)
    return jnp.sum(loss), grad_x, dmod, small, (g_w_in, g_w_br, g_w_out, g_w_gu, g_w_dn)


def _flip(me, f):
    return (me[0] ^ ((f >> 2) & 1), me[1] ^ ((f >> 1) & 1), me[2] ^ (f & 1))


def _comm_call(name, ex):
    n_in, n_out = len(ex.ins), len(ex.out_shapes)

    def body(*refs):
        out_refs, sems = refs[n_in:n_in + n_out], refs[n_in + n_out:]
        cps = _exchange_copies(ex, refs[:n_in], out_refs, sems[0], sems[1])
        for cp in cps:
            cp.start()
        for cp in cps:
            cp.wait_recv()
        if ex.n_forward:
            fwd = _forward_copies(ex, out_refs, sems[2], sems[3])
            for cp in fwd:
                cp.start()
            for cp in fwd:
                cp.wait_recv()
            cps = cps + fwd
        for cp in cps:
            cp.wait_send()

    any_spec = pl.BlockSpec(memory_space=pl.ANY)
    return pl.pallas_call(
        body, name=name, in_specs=[any_spec] * n_in, out_specs=[any_spec] * n_out, out_shape=list(ex.out_shapes),
        scratch_shapes=_exchange_sems(ex),
    )(*ex.ins)


def _by_origin(own, received, index):
    stack = jnp.concatenate([own[None], received], axis=0)
    n = stack.shape[0]
    return jnp.stack([lax.dynamic_index_in_dim(stack, k ^ index, 0, keepdims=False) for k in range(n)])


def _gather_devices(name, arrs, dev):
    def plan(me, in_refs, out_refs):
        return [(a, o.at[f - 1], _flip(me, f)) for a, o in zip(in_refs, out_refs) for f in range(1, N_DEV)]
    outs = tuple(_sds((N_DEV - 1,) + a.shape, a.dtype) for a in arrs)
    got = _comm_call(name, _Exchange(tuple(arrs), outs, (N_DEV - 1) * len(arrs), plan))
    return [_by_origin(a, g, dev) for a, g in zip(arrs, got)]


def _gather_chips(name, arrs, chip):
    def plan(me, in_refs, out_refs):
        return [(a, o.at[j], _flip(me, 2 * (j + 1))) for a, o in zip(in_refs, out_refs) for j in range(N_CHIP - 1)]
    outs = tuple(_sds((N_CHIP - 1,) + a.shape, a.dtype) for a in arrs)
    got = _comm_call(name, _Exchange(tuple(arrs), outs, (N_CHIP - 1) * len(arrs), plan))
    return [_by_origin(a, g, chip) for a, g in zip(arrs, got)]


def _halves(core, mine, other):
    lo = jnp.where(core == 0, mine, other)
    hi = jnp.where(core == 0, other, mine)
    return jnp.concatenate([lo, hi], axis=-2)


def _swap_cores_ex(arrs):
    def plan(me, in_refs, out_refs):
        return [(g, o, _flip(me, 1)) for g, o in zip(in_refs, out_refs)]
    return _Exchange(tuple(arrs), tuple(_sds(g.shape, g.dtype) for g in arrs), len(arrs), plan)


def _gather_weights(tag, shards, chip, host=None):
    def plan(me, in_refs, out_refs):
        chip_me = 2 * me[0] + me[1]
        remote = []
        for a, o in zip(in_refs, out_refs):
            half = a.shape[0] // 2
            mine = a.at[pl.ds(me[2] * half, half)]
            remote += [(mine, o.at[chip_me, me[2]], _flip(me, 2 * (j + 1))) for j in range(N_CHIP - 1)]
        return remote

    def forward(me, out_refs):
        chip_me = 2 * me[0] + me[1]
        return [(o.at[chip_me ^ (j + 1), me[2]], o.at[chip_me ^ (j + 1), me[2]], _flip(me, 1))
                for o in out_refs for j in range(N_CHIP - 1)]
    run = host or functools.partial(_comm_call, f"weights_{tag}")
    n = (N_CHIP - 1) * len(shards)
    landed = run(_Exchange(tuple(shards), tuple(_sds((N_CHIP, 2, a.shape[0] // 2, a.shape[1]), a.dtype) for a in shards),
                           n, plan, n, forward))
    return [lax.dynamic_update_slice(f.reshape((N_CHIP,) + a.shape), a[None], (chip, 0, 0)) for a, f in zip(shards, landed)]


def _rows(r):
    for br in (512, 352, 256, 128, 64, 32, 16, 8):
        if r % br == 0:
            return br
    raise ValueError(r)


def _pair_add(tag, g, recv, c):
    n, r, cols = g.shape
    half = r // 2
    br = _rows(half)
    nb = half // br

    def body(c_ref, g_ref, r_ref, o_ref):
        o_ref[...] = (g_ref[...] + r_ref[...]).astype(BF)

    return pl.pallas_call(
        body, name=f"pair_add_{tag}",
        grid_spec=pltpu.PrefetchScalarGridSpec(
            num_scalar_prefetch=1, grid=(n, nb),
            in_specs=[pl.BlockSpec((None, br, cols), lambda k, i, c_ref: (k, c_ref[0] * nb + i, 0)),
                      pl.BlockSpec((None, br, cols), lambda k, i, c_ref: (k, i, 0))],
            out_specs=pl.BlockSpec((None, br, cols), lambda k, i, c_ref: (k, i, 0))),
        out_shape=_sds((n, half, cols), BF),
        compiler_params=_cparams(dimension_semantics=("parallel", "parallel")),
    )(c, g, recv)


def _sum_chips(tag, p, q, chip):
    n, r, cols = q.shape
    br = _rows(r)

    def body(chip_ref, p_ref, q_ref, o_ref):
        acc = p_ref[...].astype(F32)
        for k in range(n):
            acc = acc + q_ref[k].astype(F32)
        o_ref[...] = acc

    return pl.pallas_call(
        body, name=f"sum_chips_{tag}",
        grid_spec=pltpu.PrefetchScalarGridSpec(
            num_scalar_prefetch=1, grid=(r // br,),
            in_specs=[pl.BlockSpec((None, br, cols), lambda i, chip_ref: (chip_ref[0], i, 0)),
                      pl.BlockSpec((n, br, cols), lambda i, chip_ref: (0, i, 0))],
            out_specs=pl.BlockSpec((br, cols), lambda i, chip_ref: (i, 0))),
        out_shape=_sds((r, cols), F32),
        compiler_params=_cparams(dimension_semantics=("parallel",)),
    )(chip, p, q)


def _reduce_grads(tags, grads, chip, core, hosts=None):
    core_arr = core.reshape(1).astype(jnp.int32)
    chip_arr = chip.reshape(1).astype(jnp.int32)
    name = "_".join(tags)
    run = hosts or [functools.partial(_comm_call, f"grads_{stage}_{name}") for stage in ("pair", "chips", "swap")]

    def plan_pair(me, in_refs, out_refs):
        remote = []
        for g, o in zip(in_refs, out_refs):
            half = g.shape[1] // 2
            remote += [(g.at[k, pl.ds((1 - me[2]) * half, half)], o.at[k], _flip(me, 1)) for k in range(N_CHIP)]
        return remote
    recv = run[0](_Exchange(tuple(grads), tuple(_sds((N_CHIP, g.shape[1] // 2, g.shape[2]), F32) for g in grads),
                            N_CHIP * len(grads), plan_pair))
    pair = [_pair_add(t, g, r, core_arr) for t, g, r in zip(tags, grads, recv)]

    def plan_chips(me, in_refs, out_refs):
        remote = []
        for p, o in zip(in_refs, out_refs):
            for j in range(N_CHIP - 1):
                peer = _flip(me, 2 * (j + 1))
                remote.append((p.at[2 * peer[0] + peer[1]], o.at[j], peer))
        return remote
    parts = run[1](_Exchange(tuple(pair), tuple(_sds((N_CHIP - 1,) + p.shape[1:], BF) for p in pair),
                             (N_CHIP - 1) * len(pair), plan_chips))
    mine = [_sum_chips(t, p, q, chip_arr) for t, p, q in zip(tags, pair, parts)]
    other = run[2](_swap_cores_ex(mine))
    return [_halves(core, h, o) for h, o in zip(mine, other)]


def _adamw_math(w, g, m, v):
    m = ADAM_B1 * m + (1.0 - ADAM_B1) * g
    v = ADAM_B2 * v + (1.0 - ADAM_B2) * (g * g)
    m_hat = m / (1.0 - ADAM_B1 ** ADAM_STEP)
    v_hat = v / (1.0 - ADAM_B2 ** ADAM_STEP)
    delta = -ADAM_LR * (m_hat / (jnp.sqrt(v_hat) + ADAM_EPS) + ADAM_WD * w)
    return delta, m, v


def _adamw(name, w, g, m, v):
    r, cols = w.shape
    br = _rows(r)
    if br * cols * 4 > (1 << 20) and br % 16 == 0:
        br //= 2

    def body(w_ref, g_ref, m_ref, v_ref, d_ref, mo_ref, vo_ref):
        d_ref[...], mo_ref[...], vo_ref[...] = _adamw_math(w_ref[...], g_ref[...], m_ref[...], v_ref[...])

    spec = pl.BlockSpec((br, cols), lambda i: (i, 0))
    return pl.pallas_call(
        body, name=f"adamw_{name}", grid=(r // br,), in_specs=[spec] * 4, out_specs=[spec] * 3,
        out_shape=[_sds((r, cols), F32)] * 3,
        compiler_params=_cparams(dimension_semantics=("parallel",)),
    )(w, g, m, v)


def _ada_fwd(c_all, ada_w, ada_b_cols):
    n = c_all.shape[0]

    def body(c_ref, w_ref, b_ref, o_ref):
        o_ref[...] = _mmx(_silu(c_ref[...]), w_ref[...]) + b_ref[...]

    return pl.pallas_call(
        body, name="ada_fwd", out_shape=_sds((n, ada_w.shape[1]), F32), compiler_params=_cparams(),
    )(c_all, ada_w, ada_b_cols)


def _ada_bwd(c_all, dmod_cols, w, m, v):
    n = c_all.shape[0]
    r, cols = w.shape
    br = 128

    def body(c_ref, d_ref, w_ref, m_ref, v_ref, g_ref, dl_ref, mo_ref, vo_ref):
        cond = _silu(c_ref[...])
        g = lax.dot_general(cond, d_ref[...], (((0,), (0,)), ((), ())), precision=lax.Precision.HIGHEST,
                            preferred_element_type=F32)
        g_ref[...] = g
        dl_ref[...], mo_ref[...], vo_ref[...] = _adamw_math(w_ref[...], g, m_ref[...], v_ref[...])

    spec = pl.BlockSpec((br, cols), lambda i: (i, 0))
    return pl.pallas_call(
        body, name="ada_bwd", grid=(r // br,),
        in_specs=[pl.BlockSpec((n, br), lambda i: (0, i)), pl.BlockSpec((n, cols), lambda i: (0, 0)), spec, spec, spec],
        out_specs=[spec] * 4, out_shape=[_sds((r, cols), F32)] * 4,
        compiler_params=_cparams(dimension_semantics=("parallel",)),
    )(c_all, dmod_cols, w, m, v)


def _sum_devices(parts):
    n, r, cols = parts.shape

    def body(p_ref, o_ref):
        acc = p_ref[0]
        for k in range(1, n):
            acc = acc + p_ref[k]
        o_ref[...] = acc

    return pl.pallas_call(body, name="sum_devices", out_shape=_sds((r, cols), F32), compiler_params=_cparams())(parts)


SMALL_ROWS = 16
_SMALL_SLOTS = dict(norm1_g=(0, 0, D), norm2_g=(1, 0, D), q_norm_g=(2, 0, HD), k_norm_g=(2, 128, HD), sinks=(2, 256, HEADS),
                    a_log=(2, 384, DN_H), dt_bias=(2, 512, DN_H), dn_norm_g=(2, 640, DN_D))
_CONV_ROW = 4
_ADA_B_ROW = 8


def _pack_small(vals, conv, ada_b):
    def row(pieces):
        out, at = [], 0
        for col, val in pieces:
            out += [jnp.zeros((1, col - at), F32), val.reshape(1, -1)]
            at = col + val.size
        return jnp.concatenate(out + [jnp.zeros((1, CONVW - at), F32)], axis=1)
    rows = {}
    for name, (r, col, n) in _SMALL_SLOTS.items():
        rows.setdefault(r, []).append((col, vals[name]))
    blank = jnp.zeros((1, CONVW), F32)
    top = [row(sorted(rows[r], key=lambda p: p[0])) if r in rows else blank for r in range(_CONV_ROW)]
    conv_rows = jnp.concatenate([conv, jnp.zeros((CONV, CONVW - conv.shape[1]), F32)], axis=1)
    tail = jnp.zeros((SMALL_ROWS - _ADA_B_ROW - 4, CONVW), F32)
    return jnp.concatenate(top + [conv_rows, ada_b.reshape(4, CONVW), tail], axis=0)


def _unpack_small(sheet, conv_cols):
    out = {name: sheet[row, col:col + n].reshape(1, n) for name, (row, col, n) in _SMALL_SLOTS.items()}
    out["conv_w"] = sheet[_CONV_ROW:_CONV_ROW + CONV, 0:conv_cols].reshape(1, CONV, 1, conv_cols)
    out["ada_b"] = sheet[_ADA_B_ROW:_ADA_B_ROW + 4, :].reshape(1, 6 * D)
    return out


def _w_in_segments():
    shard = IN_WIDTH // N_CHIP
    cuts = sorted({0, IN_WIDTH, C_Z, C_Z + 2 * DN_H} | {k * shard for k in range(1, N_CHIP)})
    segs = []
    for a, b in zip(cuts[:-1], cuts[1:]):
        k = a // shard
        pad = a if a < C_Z else (C_BA + a - C_Z if a < C_Z + 2 * DN_H else a - 2 * DN_H)
        segs.append((k, a - k * shard, b - k * shard, pad))
    return segs


def _pad_w_in(f):
    parts = [f[k][:, lo:hi] for k, lo, hi, _ in sorted(_w_in_segments(), key=lambda s: s[3])]
    return jnp.concatenate(parts + [jnp.zeros((f.shape[1], IN_PAD - IN_WIDTH), f.dtype)], axis=1)


def _unpad_w_in(g):
    return jnp.stack([jnp.concatenate([g[:, pad:pad + hi - lo] for kk, lo, hi, pad in _w_in_segments() if kk == k], axis=1)
                      for k in range(N_CHIP)])


def _blocks_to_cols(f):
    return f.transpose(1, 0, 2).reshape(f.shape[1], N_CHIP * f.shape[2])


def kernel(x, c, positions, ada_w, ada_b, norm1_g, w_in, conv_w, q_norm_g, k_norm_g, sinks, a_log, dt_bias, dn_norm_g, w_branch, w_out, norm2_g, w_gate_up, w_down, loss_target, m_ada_w, m_ada_b, m_norm1_g, m_w_in, m_conv_w, m_q_norm_g, m_k_norm_g, m_sinks, m_a_log, m_dt_bias, m_dn_norm_g, m_w_branch, m_w_out, m_norm2_g, m_w_gate_up, m_w_down, v_ada_w, v_ada_b, v_norm1_g, v_w_in, v_conv_w, v_q_norm_g, v_k_norm_g, v_sinks, v_a_log, v_dt_bias, v_dn_norm_g, v_w_branch, v_w_out, v_norm2_g, v_w_gate_up, v_w_down):
    ix, iy, ic = lax.axis_index("x"), lax.axis_index("y"), lax.axis_index("c")
    dev = 4 * ix + 2 * iy + ic
    chip = 2 * ix + iy
    n_seq = x.shape[0]
    conv_cols = conv_w.shape[-1]

    c_all, conv_all = _gather_devices("gather_cond", [c, conv_w.reshape(CONV, conv_cols)], dev)
    c_all = c_all.reshape(N_DEV * n_seq, D)
    ada_cols = ada_w.shape[-1]
    ada_b_cols = lax.dynamic_slice(ada_b, (0, chip * ada_cols), (1, ada_cols))
    mod_cols = _ada_fwd(c_all, ada_w[0], ada_b_cols)
    (mod_blocks,) = _gather_chips("gather_mod", [mod_cols], chip)
    mod_all = _blocks_to_cols(mod_blocks)
    mod = lax.dynamic_slice(mod_all, (dev * n_seq, 0), (n_seq, 6 * D)).reshape(n_seq, 1, 6 * D)
    conv_full = _blocks_to_cols(conv_all[0::2])

    (f_in,) = _gather_weights("w_in", [w_in[0].astype(BF)], chip)
    w_in_pad = _pad_w_in(f_in)

    loss, grad_x, dmod, small, (g_in, g_br, g_out, r_gu, r_dn) = _local_step(
        x, mod, positions, loss_target, norm1_g, w_in_pad, conv_full.reshape(CONV, 1, CONVW), q_norm_g, k_norm_g, sinks,
        a_log, dt_bias, dn_norm_g, w_branch[0].astype(BF), w_out[0].astype(BF), norm2_g, w_gate_up[0].astype(BF),
        w_down[0].astype(BF), dist=(chip, ic))
    loss = lax.psum(loss, ("x", "y", "c"))

    blocks = [_unpad_w_in(g_in), g_br.reshape(N_CHIP, -1, D), g_out.reshape(N_CHIP, -1, D)]
    r_in, r_br, r_out = _reduce_grads(("w_in", "w_branch", "w_out"), blocks, chip, ic)
    big = {}
    for name, w, g, m, v in (("w_in", w_in, r_in, m_w_in, v_w_in), ("w_branch", w_branch, r_br, m_w_branch, v_w_branch),
                             ("w_out", w_out, r_out, m_w_out, v_w_out), ("w_gate_up", w_gate_up, r_gu, m_w_gate_up, v_w_gate_up),
                             ("w_down", w_down, r_dn, m_w_down, v_w_down)):
        big[name] = (g,) + tuple(_adamw(name, w[0], g, m[0], v[0]))

    part = _pack_small(small, small["conv_w"], jnp.sum(dmod, axis=(0, 1)).reshape(1, 6 * D))
    dmod_all, parts = _gather_devices("gather_small", [dmod.reshape(n_seq, 6 * D), part], dev)
    dmod_all = dmod_all.reshape(N_DEV * n_seq, 6 * D)
    g_small = _unpack_small(_sum_devices(parts), CONVW)
    g_conv = lax.dynamic_slice(g_small["conv_w"].reshape(CONV, CONVW), (0, chip * conv_cols), (CONV, conv_cols))
    g_small["conv_w"] = g_conv.reshape(1, CONV, 1, conv_cols)

    given = dict(norm1_g=(norm1_g, m_norm1_g, v_norm1_g), norm2_g=(norm2_g, m_norm2_g, v_norm2_g),
                 q_norm_g=(q_norm_g, m_q_norm_g, v_q_norm_g), k_norm_g=(k_norm_g, m_k_norm_g, v_k_norm_g),
                 sinks=(sinks, m_sinks, v_sinks), a_log=(a_log, m_a_log, v_a_log), dt_bias=(dt_bias, m_dt_bias, v_dt_bias),
                 dn_norm_g=(dn_norm_g, m_dn_norm_g, v_dn_norm_g))
    sheets = [_pack_small({k: t[j] for k, t in given.items()}, cw.reshape(CONV, conv_cols), ab)
              for j, (cw, ab) in enumerate(((conv_w, ada_b), (m_conv_w, m_ada_b), (v_conv_w, v_ada_b)))]
    g_local = _pack_small(g_small, g_conv, g_small["ada_b"])
    upd = [_unpack_small(s, conv_cols) for s in _adamw("small", sheets[0], g_local, sheets[1], sheets[2])]

    dmod_cols = lax.dynamic_slice(dmod_all, (0, chip * ada_cols), (N_DEV * n_seq, ada_cols))
    ada = _ada_bwd(c_all, dmod_cols, ada_w[0], m_ada_w[0], v_ada_w[0])

    names = ["ada_w", "ada_b", "norm1_g", "w_in", "conv_w", "q_norm_g", "k_norm_g", "sinks", "a_log", "dt_bias", "dn_norm_g",
             "w_branch", "w_out", "norm2_g", "w_gate_up", "w_down"]

    def leaf(name, j):
        if name == "ada_w":
            return ada[j][None]
        if name in big:
            return big[name][j][None]
        return g_small[name] if j == 0 else upd[j - 1][name]

    return (loss, grad_x) + tuple(leaf(n, j) for j in range(4) for n in names)
```

```python
import functools
from typing import Callable, NamedTuple

import jax
import jax.numpy as jnp
import numpy as np
from jax import lax
from jax.experimental import pallas as pl
from jax.experimental.pallas import tpu as pltpu

F32 = jnp.float32
BF = jnp.bfloat16

D = 1024
HEADS = 8
KV_HEADS = 2
GROUP = 4
HD = 64
BLK = 128
ROT = 16
THETA = 500000.0
QW = 512
KVW = 128
DN_H = 4
DN_D = 128
CONV = 4
CHUNK = 64
DNW = 512
CONVW = 1536
FFN = 2816
EPS = 1e-6
IN_WIDTH = 4872
IN_PAD = 4992
C_KV = 512
C_DN = 768
C_Z = 2304
C_GA = 2816
C_GB = 3840
C_BA = 4864
NEG = -1e30
N_DEV = 8
N_CHIP = 4

ADAM_LR = 0.001
ADAM_B1 = 0.9
ADAM_B2 = 0.999
ADAM_EPS = 1e-08
ADAM_WD = 0.01
ADAM_STEP = 10

VMEM_LIMIT = 60 * 1024 * 1024


def _cparams(**kw):
    return pltpu.CompilerParams(vmem_limit_bytes=VMEM_LIMIT, **kw)


def _dg(a, b, ca, cb):
    return lax.dot_general(a.astype(BF), b.astype(BF), (((ca,), (cb,)), ((), ())),
                           preferred_element_type=F32)


@jax.custom_vjp
def _mm(a, b):
    return _dg(a, b, 1, 0)


def _mm_fwd(a, b):
    return _dg(a, b, 1, 0), (a, b)


def _mm_bwd(res, dy):
    a, b = res
    return _dg(dy, b, 1, 1).astype(a.dtype), _dg(a, dy, 0, 0).astype(b.dtype)


_mm.defvjp(_mm_fwd, _mm_bwd)


@jax.custom_vjp
def _mm_nt(a, b):
    return _dg(a, b, 1, 1)


def _mm_nt_fwd(a, b):
    return _dg(a, b, 1, 1), (a, b)


def _mm_nt_bwd(res, dy):
    a, b = res
    return _dg(dy, b, 1, 0).astype(a.dtype), _dg(dy, a, 0, 0).astype(b.dtype)


_mm_nt.defvjp(_mm_nt_fwd, _mm_nt_bwd)


@jax.custom_vjp
def _mm_tn(a, b):
    return _dg(a, b, 0, 0)


def _mm_tn_fwd(a, b):
    return _dg(a, b, 0, 0), (a, b)


def _mm_tn_bwd(res, dy):
    a, b = res
    return _dg(b, dy, 1, 1).astype(a.dtype), _dg(a, dy, 1, 0).astype(b.dtype)


_mm_tn.defvjp(_mm_tn_fwd, _mm_tn_bwd)


def _mmx(a, b):
    return jnp.dot(a, b, precision=lax.Precision.HIGHEST, preferred_element_type=F32)


def _mmx_nt(a, b):
    return lax.dot_general(a, b, (((1,), (1,)), ((), ())), precision=lax.Precision.HIGHEST,
                           preferred_element_type=F32)


def _iota(shape, dim):
    return lax.broadcasted_iota(jnp.int32, shape, dim)


def _sigmoid(x):
    return 1.0 / (1.0 + jnp.exp(-x))


def _silu(x):
    return x * _sigmoid(x)


def _softplus(x):
    return jnp.maximum(x, 0.0) + jnp.log(1.0 + jnp.exp(-jnp.abs(x)))


def _rms(x, gain):
    return x * lax.rsqrt(jnp.mean(x * x, axis=-1, keepdims=True) + EPS) * gain


def _norm_mod(x, gain, shift, scale):
    return _rms(x, gain) * (1.0 + scale) + shift


def _split(a):
    hi = a.astype(BF)
    return hi, (a - hi.astype(F32)).astype(BF)


def _dg3(a, b, ca, cb):
    ah, al = _split(a)
    bh, bl = _split(b)

    def dg(x, y):
        return lax.dot_general(x, y, (((ca,), (cb,)), ((), ())), preferred_element_type=F32)
    return dg(ah, bh) + (dg(ah, bl) + dg(al, bh))


@jax.custom_vjp
def _mm3(a, b):
    return _dg3(a, b, 1, 0)


def _mm3_fwd(a, b):
    return _dg3(a, b, 1, 0), (a, b)


def _mm3_bwd(res, dy):
    a, b = res
    return _dg3(dy, b, 1, 1), _dg3(a, dy, 0, 0)


_mm3.defvjp(_mm3_fwd, _mm3_bwd)


def _qk_prep(slabs, gain, cos, sin):
    r = _iota((2 * HD, 2 * HD), 0)
    c = _iota((2 * HD, 2 * HD), 1)
    seg = jnp.where(r // HD == c // HD, 1.0 / HD, 0.0).astype(F32)
    half = ROT // 2
    cd = c % HD
    pair = jnp.where(((cd < half) & (r == c + half)) | ((cd >= half) & (cd < ROT) & (r == c - half)), 1.0, 0.0).astype(F32)
    out = []
    for x in slabs:
        y = x * lax.rsqrt(_mm3(x * x, seg) + EPS) * gain
        out.append(y * cos + _mm3(y, pair) * sin)
    return out


def _attn_block(qs, kc, kp, vc, vp, sinks, has_prev):
    rows = GROUP * BLK
    qi = _iota((rows, 2 * BLK), 0) % BLK + BLK
    kj = _iota((rows, 2 * BLK), 1)
    dist = qi - kj
    valid = (dist >= 0) & (dist < BLK) & ((kj >= BLK) | has_prev)
    grp = _iota((rows, HEADS), 0) // BLK
    col = _iota((rows, HEADS), 1)

    outs = []
    for h in range(KV_HEADS):
        q = jnp.concatenate([qs[h * GROUP + g] for g in range(GROUP)], axis=0)
        k = jnp.concatenate([kp[h], kc[h]], axis=0)
        v = jnp.concatenate([vp[h], vc[h]], axis=0)
        s = _mm_nt(q, k) * (HD ** -0.5)
        s = jnp.where(valid, s, NEG)
        sink = jnp.sum(jnp.where(col == h * GROUP + grp, sinks, 0.0), axis=-1, keepdims=True)
        m = lax.stop_gradient(jnp.maximum(jnp.max(s, axis=-1, keepdims=True), sink))
        p = jnp.exp(s - m)
        probs = p / (jnp.sum(p, axis=-1, keepdims=True) + jnp.exp(sink - m))
        o = _mm(probs, v)
        outs += [o[g * BLK:(g + 1) * BLK] for g in range(GROUP)]
    return outs


def _dn_tail(ys, ba, alog, dtb):
    def l2(t):
        return t * lax.rsqrt(jnp.sum(t * t, axis=-1, keepdims=True) + EPS)
    s = [_silu(y) for y in ys]
    out = [l2(t) for t in s[:2 * DN_H]] + s[2 * DN_H:]
    lane = _iota(ba.shape, 1)
    beta = _sigmoid(ba)
    g = -jnp.exp(alog) * _softplus(ba + dtb)
    bg = jnp.where(lane < DN_H, beta, jnp.where(lane < 2 * DN_H, g, 0.0))
    return out, bg


def _bdg(a, b, ca, cb):
    return lax.dot_general(a.astype(BF), b.astype(BF), (((ca,), (cb,)), ((0,), (0,))), preferred_element_type=F32)


@jax.custom_vjp
def _bmm(a, b):
    return _bdg(a, b, 2, 1)


def _bmm_fwd(a, b):
    return _bdg(a, b, 2, 1), (a, b)


def _bmm_bwd(res, dy):
    a, b = res
    return _bdg(dy, b, 2, 2), _bdg(a, dy, 1, 1)


_bmm.defvjp(_bmm_fwd, _bmm_bwd)


@jax.custom_vjp
def _bmm_nt(a, b):
    return _bdg(a, b, 2, 2)


def _bmm_nt_fwd(a, b):
    return _bdg(a, b, 2, 2), (a, b)


def _bmm_nt_bwd(res, dy):
    a, b = res
    return _bdg(dy, b, 2, 1), _bdg(dy, a, 1, 1)


_bmm_nt.defvjp(_bmm_nt_fwd, _bmm_nt_bwd)


def _bmmx(a, b):
    return lax.dot_general(a, b, (((2,), (1,)), ((0,), (0,))), precision=lax.Precision.HIGHEST,
                           preferred_element_type=F32)


def _neumann_inverse(lmat):
    C = CHUNK
    eye = jnp.where(_iota((C, C), 0) == _iota((C, C), 1), 1.0, 0.0).astype(F32)[None]
    a = -lmat
    tinv = eye + a
    pw = _bmmx(a, a)
    for _ in range(4):
        both = _bmmx(jnp.concatenate([pw, tinv], axis=1), pw)
        pw, tinv = both[:, :C], tinv + both[:, C:]
    return tinv + _bmmx(tinv, pw)


def _inverse_bwd(tinv, d_tinv):
    x = lax.dot_general(d_tinv, tinv, (((2,), (2,)), ((0,), (0,))), precision=lax.Precision.HIGHEST,
                        preferred_element_type=F32)
    return -lax.dot_general(tinv, x, (((1,), (1,)), ((0,), (0,))), precision=lax.Precision.HIGHEST,
                            preferred_element_type=F32)


@jax.custom_vjp
def _tri_inverse(lmat):
    return _neumann_inverse(lmat)


def _tri_inverse_fwd(lmat):
    tinv = _neumann_inverse(lmat)
    return tinv, tinv


def _tri_inverse_bwd(tinv, d_tinv):
    return (_inverse_bwd(tinv, d_tinv),)


_tri_inverse.defvjp(_tri_inverse_fwd, _tri_inverse_bwd)


@jax.custom_vjp
def _tri_inverse_known(lmat, tinv):
    return tinv


def _tri_inverse_known_fwd(lmat, tinv):
    return tinv, tinv


def _tri_inverse_known_bwd(tinv, d_tinv):
    return _inverse_bwd(tinv, d_tinv), jnp.zeros_like(tinv)


_tri_inverse_known.defvjp(_tri_inverse_known_fwd, _tri_inverse_known_bwd)


def _dn_intra(q, k, v, bg, tinv=None):
    C = CHUNK
    G = bg.shape[0]
    r = _iota((C, C), 0)
    c = _iota((C, C), 1)
    incl = (r >= c)[None]
    strict = (r > c)[None]
    eye = jnp.where(r == c, 1.0, 0.0).astype(F32)[None]
    tri = jnp.broadcast_to(jnp.where(r >= c, 1.0, 0.0).astype(F32)[None], (G, C, C))
    gc_all = _bmmx(tri, bg)
    lane = _iota((C, DN_D), 1)

    def per_head(x, offset):
        return jnp.concatenate([jnp.sum(jnp.where(lane == offset + h, x[g], 0.0), axis=-1, keepdims=True)[None]
                                for g in range(G) for h in range(DN_H)], axis=0)
    beta = per_head(bg, 0)
    gcol = per_head(gc_all, DN_H)
    grow = jnp.sum(eye * gcol, axis=1, keepdims=True)
    glast = jnp.sum(jnp.where(_iota((1, C, 1), 1) == C - 1, gcol, 0.0), axis=1, keepdims=True)
    decay = jnp.exp(jnp.where(incl, gcol - grow, NEG))
    q = q * (DN_D ** -0.5)
    kb = k * beta
    lmat = jnp.where(strict, _bmm_nt(kb, k) * decay, 0.0)
    tinv = _tri_inverse(lmat) if tinv is None else _tri_inverse_known(lmat, tinv)
    egc = jnp.exp(gcol)
    u = _bmm(tinv, v * beta)
    w = _bmm(tinv, kb * egc)
    a = _bmm_nt(q, k) * decay
    return u, w, q * egc, k * jnp.exp(glast - gcol), a, jnp.exp(glast), tinv


@jax.custom_vjp
def _bmm_tn(a, b):
    return _bdg(a, b, 1, 1)


def _bmm_tn_fwd(a, b):
    return _bdg(a, b, 1, 1), (a, b)


def _bmm_tn_bwd(res, dy):
    a, b = res
    return _bdg(b, dy, 2, 2), _bdg(a, dy, 2, 1)


_bmm_tn.defvjp(_bmm_tn_fwd, _bmm_tn_bwd)


def _dn_rec(state, u, w, qd, kd, a, cd):
    v_new = u - _bmm(w, state)
    out = _bmm(qd, state) + _bmm(a, v_new)
    return state * cd + _bmm_tn(kd, v_new), out


def _mix_tile(o_attn, o_raw, zs, ga, gb, x, gate1, dn_g, wb_a, wb_d, w_out, p_ya, p_yd, p_out):
    o_dn = jnp.concatenate([_rms(o_raw[h], dn_g) * _silu(zs[h]) for h in range(DN_H)], axis=-1)
    y_a = _mm(o_attn, wb_a) + p_ya
    y_d = _mm(o_dn, wb_d) + p_yd
    merged = _sigmoid(ga) * y_a + _sigmoid(gb) * y_d
    out = _mm(merged, w_out) + p_out
    return x + gate1 * out, o_dn, merged


def _mlp_tile(x1, gain, shift, scale, gate2, w_gu, w_dn, tgt, p_gu, p_yy):
    h2 = _norm_mod(x1, gain, shift, scale)
    gu = jnp.concatenate([_mm(h2, w) for w in w_gu], axis=-1) + p_gu
    act = _silu(gu[:, :FFN]) * gu[:, FFN:]
    yy = _mm(act, w_dn) + p_yy
    y = x1 + gate2 * yy
    err = y - tgt
    return 0.5 * jnp.sum(err * err) * (1.0 / D), (h2, act)


def _tok(bt, f):
    return pl.BlockSpec((None, bt, f), lambda b, i: (b, i, 0))


def _full(shape):
    return pl.BlockSpec(shape, lambda b, i: (0,) * len(shape))


def _resident(shape):
    return pl.BlockSpec(shape, lambda b, i: (0,) * len(shape), pipeline_mode=pl.Buffered(1))


def _per_batch(f):
    return pl.BlockSpec((None, 1, f), lambda b, i: (b, 0, 0))


def _sds(shape, dtype):
    return jax.ShapeDtypeStruct(shape, dtype)


class _Exchange(NamedTuple):
    ins: tuple
    out_shapes: tuple
    n_remote: int
    plan: Callable
    n_forward: int = 0
    forward: Callable = None


def _remote_copies(remote, send_sems, recv_sems):
    return [pltpu.make_async_remote_copy(src_ref=src, dst_ref=dst, send_sem=send_sems.at[i], recv_sem=recv_sems.at[i],
                                         device_id=peer, device_id_type=pl.DeviceIdType.MESH)
            for i, (src, dst, peer) in enumerate(remote)]


def _exchange_copies(ex, in_refs, out_refs, send_sems, recv_sems):
    remote = ex.plan((lax.axis_index("x"), lax.axis_index("y"), lax.axis_index("c")), in_refs, out_refs)
    assert len(remote) == ex.n_remote
    return _remote_copies(remote, send_sems, recv_sems)


def _forward_copies(ex, out_refs, send_sems, recv_sems):
    remote = ex.forward((lax.axis_index("x"), lax.axis_index("y"), lax.axis_index("c")), out_refs)
    assert len(remote) == ex.n_forward
    return _remote_copies(remote, send_sems, recv_sems)


def _exchange_sems(ex):
    sems = [pltpu.SemaphoreType.DMA((ex.n_remote,)), pltpu.SemaphoreType.DMA((ex.n_remote,))]
    if ex.n_forward:
        sems += [pltpu.SemaphoreType.DMA((ex.n_forward,)), pltpu.SemaphoreType.DMA((ex.n_forward,))]
    return sems


def _hosted_call(body, name, grid, in_specs, out_specs, out_shape, scratch_shapes, semantics, ins, ex=None):
    if ex is None:
        outs = pl.pallas_call(body, name=name, grid=grid, in_specs=in_specs, out_specs=out_specs, out_shape=out_shape,
                              scratch_shapes=scratch_shapes,
                              compiler_params=_cparams(dimension_semantics=semantics))(*ins)
        return outs, ()
    n_in, n_out, n_scr = len(ins), len(out_shape), len(scratch_shapes)
    c_in, c_out = len(ex.ins), len(ex.out_shapes)
    steps = 1
    for g in grid:
        steps *= g

    def wrapped(*refs):
        a, b, c, d = n_in, n_in + c_in, n_in + c_in + n_out, n_in + c_in + n_out + c_out
        scratch, sems = refs[d:d + n_scr], refs[d + n_scr:]
        step = 0
        for axis, g in enumerate(grid):
            step = step * g + pl.program_id(axis)

        def first_phase():
            return _exchange_copies(ex, refs[a:b], refs[c:d], sems[0], sems[1])

        @pl.when(step == 0)
        def _():
            for cp in first_phase():
                cp.start()
        body(*refs[:a], *refs[b:c], *scratch)

        if ex.n_forward:
            @pl.when(step == (3 * steps) // 4)
            def _():
                for cp in first_phase():
                    cp.wait_recv()
                for cp in _forward_copies(ex, refs[c:d], sems[2], sems[3]):
                    cp.start()

        @pl.when(step == steps - 1)
        def _():
            cps = first_phase()
            if ex.n_forward:
                fwd = _forward_copies(ex, refs[c:d], sems[2], sems[3])
                for cp in fwd:
                    cp.wait_recv()
                for cp in cps + fwd:
                    cp.wait_send()
            else:
                for cp in cps:
                    cp.wait_recv()
                for cp in cps:
                    cp.wait_send()

    any_spec = pl.BlockSpec(memory_space=pl.ANY)
    res = pl.pallas_call(
        wrapped, name=name, grid=grid, in_specs=list(in_specs) + [any_spec] * c_in,
        out_specs=list(out_specs) + [any_spec] * c_out, out_shape=list(out_shape) + list(ex.out_shapes),
        scratch_shapes=list(scratch_shapes) + _exchange_sems(ex),
        compiler_params=_cparams(dimension_semantics=("arbitrary",) * len(grid)),
    )(*ins, *ex.ins)
    return res[:n_out], res[n_out:]


def _acc(ref, val, first):
    @pl.when(first)
    def _():
        ref[...] = val

    @pl.when(jnp.logical_not(first))
    def _():
        ref[...] += val


def _in_proj(x, mod, norm1_g, w_in, bt):
    B, S, _ = x.shape

    def body(x_ref, mod_ref, g_ref, w_ref, q_ref, kv_ref, dn_ref, z_ref, ga_ref, gb_ref, ba_ref, h_ref):
        h = _norm_mod(x_ref[...], g_ref[...], mod_ref[:, 0:D], mod_ref[:, D:2 * D]).astype(BF)
        h_ref[...] = h

        def proj(c0, c1):
            return jnp.dot(h, w_ref[:, c0:c1], preferred_element_type=F32)
        q_ref[...] = proj(0, C_KV).astype(BF)
        kv_ref[...] = proj(C_KV, C_DN).astype(BF)
        dn_ref[...] = proj(C_DN, C_Z).astype(BF)
        z_ref[...] = proj(C_Z, C_GA).astype(BF)
        ga_ref[...] = proj(C_GA, C_GB).astype(BF)
        gb_ref[...] = proj(C_GB, C_BA).astype(BF)
        ba_ref[...] = proj(C_BA, IN_PAD)

    widths = (QW, 2 * KVW, CONVW, DNW, D, D)
    return pl.pallas_call(
        body, name="in_proj", grid=(B, S // bt),
        in_specs=[_tok(bt, D), _per_batch(6 * D), _full((1, D)), _resident((D, IN_PAD))],
        out_specs=[_tok(bt, w) for w in widths] + [_tok(bt, 128), _tok(bt, D)],
        out_shape=[_sds((B, S, w), BF) for w in widths] + [_sds((B, S, 128), F32), _sds((B, S, D), BF)],
        compiler_params=_cparams(dimension_semantics=("parallel", "parallel")),
    )(x, mod, norm1_g, w_in)


def _prev_blk(bt, f):
    return pl.BlockSpec((None, bt, f), lambda b, i: (b, jnp.maximum(i - 1, 0), 0))


QKV = QW + 2 * KVW


def _qk_slabs(q_ref, kv_ref):
    return ([q_ref[:, j * 2 * HD:(j + 1) * 2 * HD].astype(F32) for j in range(QW // (2 * HD))],
            [kv_ref[:, 0:KVW].astype(F32)])


def _qk_prep_fwd(q, kv, cos, sin, qg, kg, bt):
    B, S, _ = q.shape

    def body(q_ref, kv_ref, cos_ref, sin_ref, qg_ref, kg_ref, o_ref):
        qs, ks = _qk_slabs(q_ref, kv_ref)
        qn = _qk_prep(qs, qg_ref[...], cos_ref[...], sin_ref[...])
        kn = _qk_prep(ks, kg_ref[...], cos_ref[...], sin_ref[...])
        for j, t in enumerate(qn + kn):
            o_ref[:, j * 2 * HD:(j + 1) * 2 * HD] = t.astype(BF)
        o_ref[:, QW + KVW:QKV] = kv_ref[:, KVW:2 * KVW]

    return pl.pallas_call(
        body, name="qk_prep_fwd", grid=(B, S // bt),
        in_specs=[_tok(bt, QW), _tok(bt, 2 * KVW), _tok(bt, 2 * HD), _tok(bt, 2 * HD), _full((1, 2 * HD)), _full((1, 2 * HD))],
        out_specs=_tok(bt, QKV), out_shape=_sds((B, S, QKV), BF),
        compiler_params=_cparams(dimension_semantics=("parallel", "parallel")),
    )(q, kv, cos, sin, qg, kg)


def _qk_prep_bwd(q, kv, cos, sin, qg, kg, dqn, dkvn, bt, ex=None):
    B, S, _ = q.shape

    def body(q_ref, kv_ref, cos_ref, sin_ref, qg_ref, kg_ref, dqn_ref, dkvn_ref, dq_ref, dkv_ref, dqg_ref, dkg_ref):
        qs, ks = _qk_slabs(q_ref, kv_ref)
        cos, sin = cos_ref[...], sin_ref[...]

        def f(qs, ks, qg, kg):
            return _qk_prep(qs, qg, cos, sin), _qk_prep(ks, kg, cos, sin)
        _, vjp = jax.vjp(f, qs, ks, qg_ref[...], kg_ref[...])
        n_q = len(qs)
        d_q = [dqn_ref[:, j * 2 * HD:(j + 1) * 2 * HD].astype(F32) for j in range(n_q)]
        d_k = [dkvn_ref[:, 0:KVW].astype(F32)]
        dqs, dks, dqg, dkg = vjp((d_q, d_k))
        for j in range(n_q):
            dq_ref[:, j * 2 * HD:(j + 1) * 2 * HD] = dqs[j].astype(BF)
        dkv_ref[:, 0:KVW] = dks[0].astype(BF)
        dkv_ref[:, KVW:2 * KVW] = dkvn_ref[:, KVW:2 * KVW]
        first = (pl.program_id(0) == 0) & (pl.program_id(1) == 0)
        _acc(dqg_ref, dqg, first)
        _acc(dkg_ref, dkg, first)

    return _hosted_call(
        body, "qk_prep_bwd", (B, S // bt),
        in_specs=[_tok(bt, QW), _tok(bt, 2 * KVW), _tok(bt, 2 * HD), _tok(bt, 2 * HD), _full((1, 2 * HD)), _full((1, 2 * HD)),
                  _tok(bt, QW), _tok(bt, 2 * KVW)],
        out_specs=[_tok(bt, QW), _tok(bt, 2 * KVW), _full((1, 2 * HD)), _full((1, 2 * HD))],
        out_shape=[_sds((B, S, QW), BF), _sds((B, S, 2 * KVW), BF), _sds((1, 2 * HD), F32), _sds((1, 2 * HD), F32)],
        scratch_shapes=[], semantics=("arbitrary", "arbitrary"), ins=(q, kv, cos, sin, qg, kg, dqn, dkvn), ex=ex)


def _attn_load(qkv_ref, kvp_ref):
    qs = [qkv_ref[:, h * HD:(h + 1) * HD].astype(F32) for h in range(HEADS)]
    kc = [qkv_ref[:, QW + h * HD:QW + (h + 1) * HD].astype(F32) for h in range(KV_HEADS)]
    vc = [qkv_ref[:, QW + KVW + h * HD:QW + KVW + (h + 1) * HD].astype(F32) for h in range(KV_HEADS)]
    kp = [kvp_ref[:, h * HD:(h + 1) * HD].astype(F32) for h in range(KV_HEADS)]
    vp = [kvp_ref[:, KVW + h * HD:KVW + (h + 1) * HD].astype(F32) for h in range(KV_HEADS)]
    return qs, kc, kp, vc, vp


def _kv_prev_spec(index):
    return pl.BlockSpec((None, BLK, 2 * KVW), lambda b, i: (b, index(i), QW // (2 * KVW)))


def _attn_fwd(qkv, sinks):
    B, S, _ = qkv.shape

    def body(qkv_ref, kvp_ref, sk_ref, o_ref):
        qs, kc, kp, vc, vp = _attn_load(qkv_ref, kvp_ref)
        outs = _attn_block(qs, kc, kp, vc, vp, sk_ref[...], pl.program_id(1) > 0)
        for h in range(HEADS):
            o_ref[:, h * HD:(h + 1) * HD] = outs[h].astype(BF)

    return pl.pallas_call(
        body, name="attn_fwd", grid=(B, S // BLK),
        in_specs=[_tok(BLK, QKV), _kv_prev_spec(lambda i: jnp.maximum(i - 1, 0)), _full((1, HEADS))],
        out_specs=_tok(BLK, QW), out_shape=_sds((B, S, QW), BF),
        compiler_params=_cparams(dimension_semantics=("parallel", "parallel")),
    )(qkv, qkv, sinks)


def _conv_fwd_tile(xe_ref, x_ref, halo_ref, cw_ref, first, bt):
    halo = halo_ref[...].astype(F32)
    xe_ref[0:8, :] = jnp.where(first, 0.0, halo)
    xe_ref[8:bt + 8, :] = x_ref[...].astype(F32)
    y = cw_ref[0:1, :] * xe_ref[5:bt + 5, :]
    for j in range(1, CONV):
        y = y + cw_ref[j:j + 1, :] * xe_ref[5 + j:bt + 5 + j, :]
    return y


def _halo_spec(bt):
    return pl.BlockSpec((None, 8, CONVW), lambda b, i: (b, jnp.maximum(i * (bt // 8) - 1, 0), 0))


def _dn_prep(dn, ba, conv_w, alog, dtb, bt):
    B, S, _ = dn.shape

    def body(x_ref, halo_ref, ba_ref, cw_ref, al_ref, dt_ref, qkv_ref, bg_ref, y_ref, xe_ref):
        y = _conv_fwd_tile(xe_ref, x_ref, halo_ref, cw_ref, pl.program_id(1) == 0, bt)
        y_ref[...] = y.astype(BF)
        ys = [y[:, j * DN_D:(j + 1) * DN_D] for j in range(3 * DN_H)]
        out, bg = _dn_tail(ys, ba_ref[...], al_ref[...], dt_ref[...])
        for j in range(3 * DN_H):
            qkv_ref[:, j * DN_D:(j + 1) * DN_D] = out[j]
        bg_ref[...] = bg

    return pl.pallas_call(
        body, name="dn_prep", grid=(B, S // bt),
        in_specs=[_tok(bt, CONVW), _halo_spec(bt), _tok(bt, 128), _full((CONV, CONVW)), _full((1, 128)), _full((1, 128))],
        out_specs=[_tok(bt, CONVW), _tok(bt, 128), _tok(bt, CONVW)],
        out_shape=[_sds((B, S, CONVW), F32), _sds((B, S, 128), F32), _sds((B, S, CONVW), BF)],
        scratch_shapes=[pltpu.VMEM((bt + 8, CONVW), F32)],
        compiler_params=_cparams(dimension_semantics=("parallel", "arbitrary")),
    )(dn, dn, ba, conv_w, alog, dtb)


def _dn_load(qkv_ref):
    qs = [qkv_ref[:, h * DN_D:(h + 1) * DN_D] for h in range(DN_H)]
    ks = [qkv_ref[:, DNW + h * DN_D:DNW + (h + 1) * DN_D] for h in range(DN_H)]
    vs = [qkv_ref[:, 2 * DNW + h * DN_D:2 * DNW + (h + 1) * DN_D] for h in range(DN_H)]
    return qs, ks, vs


DN_GROUP = 4
AW = DN_H * CHUNK


def _stack_heads(ref, G, offset, width):
    return jnp.stack([ref[g * CHUNK:(g + 1) * CHUNK, offset + h * width:offset + (h + 1) * width]
                      for g in range(G) for h in range(DN_H)])


def _dn_load_stack(qkv_ref, G):
    return tuple(_stack_heads(qkv_ref, G, j * DNW, DN_D) for j in range(3))


def _cd_spec(n):
    return pl.BlockSpec((None, n, 1, DN_D), lambda b, i: (b, i, 0, 0))


def _dn_intra_fwd(qkv, bg, ex=None):
    B, S, _ = qkv.shape
    nc = S // CHUNK
    G = min(DN_GROUP, nc)
    rows = G * CHUNK

    def body(qkv_ref, bg_ref, u_ref, w_ref, qd_ref, kd_ref, a_ref, cd_ref, t_ref):
        q, k, v = _dn_load_stack(qkv_ref, G)
        u, w, qd, kd, a, cd, tinv = _dn_intra(q, k, v, bg_ref[...].reshape(G, CHUNK, DN_D))
        lane_row = _iota((1, DN_D), 1)
        for g in range(G):
            rows = slice(g * CHUNK, (g + 1) * CHUNK)
            cd_row = jnp.zeros((1, DN_D), F32)
            for h in range(DN_H):
                n = g * DN_H + h
                cols = slice(h * DN_D, (h + 1) * DN_D)
                u_ref[rows, cols] = u[n]
                w_ref[rows, cols] = w[n].astype(BF)
                qd_ref[rows, cols] = qd[n].astype(BF)
                kd_ref[rows, cols] = kd[n].astype(BF)
                a_ref[rows, h * CHUNK:(h + 1) * CHUNK] = a[n].astype(BF)
                t_ref[rows, h * CHUNK:(h + 1) * CHUNK] = tinv[n]
                cd_row = cd_row + jnp.where(lane_row == h, cd[n], 0.0)
            cd_ref[g] = cd_row

    return _hosted_call(
        body, "dn_intra_fwd", (B, nc // G),
        in_specs=[_tok(rows, CONVW), _tok(rows, 128)],
        out_specs=[_tok(rows, DNW)] * 4 + [_tok(rows, AW), _cd_spec(G), _tok(rows, AW)],
        out_shape=[_sds((B, S, DNW), F32)] + [_sds((B, S, DNW), BF)] * 3 + [_sds((B, S, AW), BF), _sds((B, nc, 1, DN_D), F32),
                                                                            _sds((B, S, AW), F32)],
        scratch_shapes=[], semantics=("parallel", "parallel"), ins=(qkv, bg), ex=ex)


def _rec_stack(ref, B, width):
    return jnp.stack([ref[b, :, h * width:(h + 1) * width].astype(F32) for b in range(B) for h in range(DN_H)])


def _rec_load(B, u_ref, w_ref, qd_ref, kd_ref, a_ref, cd_ref):
    lane_row = _iota((1, DN_D), 1)
    cd = jnp.stack([jnp.sum(jnp.where(lane_row == h, cd_ref[b, 0], 0.0), axis=-1, keepdims=True)
                    for b in range(B) for h in range(DN_H)])
    return (_rec_stack(u_ref, B, DN_D), _rec_stack(w_ref, B, DN_D), _rec_stack(qd_ref, B, DN_D),
            _rec_stack(kd_ref, B, DN_D), _rec_stack(a_ref, B, CHUNK), cd)


def _rec_store(B, ref, val, width):
    for b in range(B):
        for h in range(DN_H):
            ref[b, :, h * width:(h + 1) * width] = val[b * DN_H + h]


def _rec_specs(B, index):
    def tok(f):
        return pl.BlockSpec((B, CHUNK, f), lambda i: (0, index(i), 0))
    cd = pl.BlockSpec((B, 1, 1, DN_D), lambda i: (0, index(i), 0, 0))
    st = pl.BlockSpec((B, None, DN_H, DN_D, DN_D), lambda i: (0, index(i), 0, 0, 0))
    return tok, cd, st


def _dn_rec_fwd(u, w, qd, kd, a, cd, ex=None):
    B, S, _ = u.shape
    nc = S // CHUNK
    tok, cd_spec, st_spec = _rec_specs(B, lambda i: i)

    def body(u_ref, w_ref, qd_ref, kd_ref, a_ref, cd_ref, o_ref, st_ref, s_ref):
        @pl.when(pl.program_id(0) == 0)
        def _():
            s_ref[...] = jnp.zeros_like(s_ref)
        state = s_ref[...]
        st_ref[...] = state.reshape(B, DN_H, DN_D, DN_D)
        new_state, out = _dn_rec(state, *_rec_load(B, u_ref, w_ref, qd_ref, kd_ref, a_ref, cd_ref))
        s_ref[...] = new_state
        _rec_store(B, o_ref, out, DN_D)

    return _hosted_call(
        body, "dn_rec_fwd", (nc,),
        in_specs=[tok(DNW)] * 4 + [tok(AW), cd_spec],
        out_specs=[tok(DNW), st_spec],
        out_shape=[_sds((B, S, DNW), F32), _sds((B, nc, DN_H, DN_D, DN_D), F32)],
        scratch_shapes=[pltpu.VMEM((B * DN_H, DN_D, DN_D), F32)],
        semantics=("arbitrary",), ins=(u, w, qd, kd, a, cd), ex=ex)


def _mix_load(oa_ref, or_ref, z_ref):
    o_raw = [or_ref[:, h * DN_D:(h + 1) * DN_D] for h in range(DN_H)]
    zs = [z_ref[:, h * DN_D:(h + 1) * DN_D].astype(F32) for h in range(DN_H)]
    return oa_ref[...].astype(F32), o_raw, zs


def _mix_fwd(o_attn, o_raw, z, ga, gb, x, mod, dn_g, w_branch, w_out, bt):
    B, S, _ = x.shape

    def body(oa_ref, or_ref, z_ref, ga_ref, gb_ref, x_ref, mod_ref, dg_ref, wb_ref, wo_ref, x1_ref, od_ref, mg_ref):
        oa, o_r, zs = _mix_load(oa_ref, or_ref, z_ref)
        x1, o_dn, merged = _mix_tile(oa, o_r, zs, ga_ref[...].astype(F32), gb_ref[...].astype(F32), x_ref[...],
                                     mod_ref[:, 2 * D:3 * D], dg_ref[...], wb_ref[0:QW, :], wb_ref[QW:2 * QW, :],
                                     wo_ref[...], 0.0, 0.0, 0.0)
        x1_ref[...] = x1
        od_ref[...] = o_dn.astype(BF)
        mg_ref[...] = merged.astype(BF)

    return pl.pallas_call(
        body, name="mix_fwd", grid=(B, S // bt),
        in_specs=[_tok(bt, QW), _tok(bt, DNW), _tok(bt, DNW), _tok(bt, D), _tok(bt, D), _tok(bt, D), _per_batch(6 * D),
                  _full((1, DN_D)), _resident((D, D)), _resident((D, D))],
        out_specs=[_tok(bt, D), _tok(bt, DNW), _tok(bt, D)],
        out_shape=[_sds((B, S, D), F32), _sds((B, S, DNW), BF), _sds((B, S, D), BF)],
        compiler_params=_cparams(dimension_semantics=("parallel", "parallel")),
    )(o_attn, o_raw, z, ga, gb, x, mod, dn_g, w_branch, w_out)


def _mlp(x1, tgt, mod, norm2_g, w_gu, w_dn, bt):
    B, S, _ = x1.shape

    def body(x1_ref, t_ref, mod_ref, g_ref, wgu_ref, wdn_ref,
             dx1_ref, h2_ref, act_ref, dgu_ref, dyy_ref, loss_ref, dmod_ref, dg_ref):
        w_gu_v, w_dn_v, t = [wgu_ref[k] for k in range(N_CHIP)], wdn_ref[...], t_ref[...]

        def f(x1, gain, shift, scale, gate2, p_gu, p_yy):
            return _mlp_tile(x1, gain, shift, scale, gate2, w_gu_v, w_dn_v, t, p_gu, p_yy)
        zero_gu = jnp.zeros((bt, 2 * FFN), F32)
        zero_yy = jnp.zeros((bt, D), F32)
        loss, vjp, (h2, act) = jax.vjp(f, x1_ref[...], g_ref[...], mod_ref[:, 3 * D:4 * D], mod_ref[:, 4 * D:5 * D],
                                       mod_ref[:, 5 * D:6 * D], zero_gu, zero_yy, has_aux=True)
        dx1, dgain, dshift, dscale, dgate2, dgu, dyy = vjp(jnp.ones((), F32))
        dx1_ref[...] = dx1
        h2_ref[...] = h2.astype(BF)
        act_ref[...] = act.astype(BF)
        dgu_ref[...] = dgu.astype(BF)
        dyy_ref[...] = dyy.astype(BF)
        first = pl.program_id(1) == 0
        _acc(loss_ref, jnp.reshape(loss, (1, 1)), first)
        _acc(dmod_ref, jnp.concatenate([dshift, dscale, dgate2], axis=-1), first)
        _acc(dg_ref, dgain, first)

    return pl.pallas_call(
        body, name="mlp", grid=(B, S // bt),
        in_specs=[_tok(bt, D), _tok(bt, D), _per_batch(6 * D), _full((1, D)), _resident((N_CHIP, D, 2 * FFN // N_CHIP)),
                  _resident((FFN, D))],
        out_specs=[_tok(bt, D), _tok(bt, D), _tok(bt, FFN), _tok(bt, 2 * FFN), _tok(bt, D),
                   _per_batch(1), _per_batch(3 * D), _per_batch(D)],
        out_shape=[_sds((B, S, D), F32), _sds((B, S, D), BF), _sds((B, S, FFN), BF), _sds((B, S, 2 * FFN), BF),
                   _sds((B, S, D), BF), _sds((B, 1, 1), F32), _sds((B, 1, 3 * D), F32), _sds((B, 1, D), F32)],
        compiler_params=_cparams(dimension_semantics=("parallel", "arbitrary")),
    )(x1, tgt, mod, norm2_g, w_gu, w_dn)


def _mix_bwd(o_attn, o_raw, z, ga, gb, x, mod, dn_g, w_branch, w_out, dx1, bt, ex=None):
    B, S, _ = x.shape

    def body(oa_ref, or_ref, z_ref, ga_ref, gb_ref, x_ref, mod_ref, dg_ref, wb_ref, wo_ref, dx1_ref,
             doa_ref, dor_ref, dz_ref, dga_ref, dgb_ref, dya_ref, dyd_ref, dout_ref, dgate_ref, ddg_ref):
        oa, o_r, zs = _mix_load(oa_ref, or_ref, z_ref)
        wb_a, wb_d, wo = wb_ref[0:QW, :], wb_ref[QW:2 * QW, :], wo_ref[...]

        def f(oa, o_r, zs, ga, gb, gate1, dn_g, p_ya, p_yd, p_out):
            return _mix_tile(oa, o_r, zs, ga, gb, x_ref[...], gate1, dn_g, wb_a, wb_d, wo, p_ya, p_yd, p_out)[0]
        zero = jnp.zeros((bt, D), F32)
        _, vjp = jax.vjp(f, oa, o_r, zs, ga_ref[...].astype(F32), gb_ref[...].astype(F32), mod_ref[:, 2 * D:3 * D],
                         dg_ref[...], zero, zero, zero)
        doa, dor, dzs, dga, dgb, dgate1, ddn_g, dya, dyd, dout = vjp(dx1_ref[...])
        doa_ref[...] = doa
        for h in range(DN_H):
            dor_ref[:, h * DN_D:(h + 1) * DN_D] = dor[h]
            dz_ref[:, h * DN_D:(h + 1) * DN_D] = dzs[h].astype(BF)
        dga_ref[...] = dga.astype(BF)
        dgb_ref[...] = dgb.astype(BF)
        dya_ref[...] = dya.astype(BF)
        dyd_ref[...] = dyd.astype(BF)
        dout_ref[...] = dout.astype(BF)
        first = pl.program_id(1) == 0
        _acc(dgate_ref, dgate1, first)
        _acc(ddg_ref, ddn_g, first)

    return _hosted_call(
        body, "mix_bwd", (B, S // bt),
        in_specs=[_tok(bt, QW), _tok(bt, DNW), _tok(bt, DNW), _tok(bt, D), _tok(bt, D), _tok(bt, D), _per_batch(6 * D),
                  _full((1, DN_D)), _resident((D, D)), _resident((D, D)), _tok(bt, D)],
        out_specs=[_tok(bt, QW), _tok(bt, DNW), _tok(bt, DNW), _tok(bt, D), _tok(bt, D), _tok(bt, D), _tok(bt, D), _tok(bt, D),
                   _per_batch(D), _per_batch(DN_D)],
        out_shape=[_sds((B, S, QW), F32), _sds((B, S, DNW), F32), _sds((B, S, DNW), BF), _sds((B, S, D), BF),
                   _sds((B, S, D), BF), _sds((B, S, D), BF), _sds((B, S, D), BF), _sds((B, S, D), BF),
                   _sds((B, 1, D), F32), _sds((B, 1, DN_D), F32)],
        scratch_shapes=[], semantics=("parallel", "arbitrary"),
        ins=(o_attn, o_raw, z, ga, gb, x, mod, dn_g, w_branch, w_out, dx1), ex=ex)


def _dn_rec_bwd(u, w, qd, kd, a, cd, states, d_o, ex=None):
    B, S, _ = u.shape
    nc = S // CHUNK
    tok, cd_spec, st_spec = _rec_specs(B, lambda i: nc - 1 - i)

    def body(u_ref, w_ref, qd_ref, kd_ref, a_ref, cd_ref, st_ref, do_ref,
             du_ref, dw_ref, dqd_ref, dkd_ref, da_ref, dcd_ref, ds_ref):
        @pl.when(pl.program_id(0) == 0)
        def _():
            ds_ref[...] = jnp.zeros_like(ds_ref)
        state = st_ref[...].reshape(B * DN_H, DN_D, DN_D)
        _, vjp = jax.vjp(_dn_rec, state, *_rec_load(B, u_ref, w_ref, qd_ref, kd_ref, a_ref, cd_ref))
        dst, du, dw, dqd, dkd, da, dcd = vjp((ds_ref[...], _rec_stack(do_ref, B, DN_D)))
        ds_ref[...] = dst
        for ref, val, width in ((du_ref, du, DN_D), (dw_ref, dw, DN_D), (dqd_ref, dqd, DN_D), (dkd_ref, dkd, DN_D),
                                (da_ref, da, CHUNK)):
            _rec_store(B, ref, val, width)
        lane_row = _iota((1, DN_D), 1)
        for b in range(B):
            row = jnp.zeros((1, DN_D), F32)
            for h in range(DN_H):
                row = row + jnp.where(lane_row == h, dcd[b * DN_H + h], 0.0)
            dcd_ref[b, 0] = row

    return _hosted_call(
        body, "dn_rec_bwd", (nc,),
        in_specs=[tok(DNW)] * 4 + [tok(AW), cd_spec, st_spec, tok(DNW)],
        out_specs=[tok(DNW)] * 4 + [tok(AW), cd_spec],
        out_shape=[_sds((B, S, DNW), F32)] * 4 + [_sds((B, S, AW), F32), _sds((B, nc, 1, DN_D), F32)],
        scratch_shapes=[pltpu.VMEM((B * DN_H, DN_D, DN_D), F32)],
        semantics=("arbitrary",), ins=(u, w, qd, kd, a, cd, states, d_o), ex=ex)


def _dn_intra_bwd(qkv, bg, tinv, du, dw, dqd, dkd, da, dcd, ex=None):
    B, S, _ = qkv.shape
    nc = S // CHUNK
    G = min(DN_GROUP, nc)
    rows = G * CHUNK

    def body(qkv_ref, bg_ref, t_ref, du_ref, dw_ref, dqd_ref, dkd_ref, da_ref, dcd_ref, dqkv_ref, dbg_ref):
        q, k, v = _dn_load_stack(qkv_ref, G)
        known = _stack_heads(t_ref, G, 0, CHUNK)
        _, vjp = jax.vjp(lambda q, k, v, bg: _dn_intra(q, k, v, bg, known)[:6], q, k, v,
                         bg_ref[...].reshape(G, CHUNK, DN_D))
        lane_row = _iota((1, DN_D), 1)
        dcd = jnp.stack([jnp.sum(jnp.where(lane_row == h, dcd_ref[g], 0.0), axis=-1, keepdims=True)
                         for g in range(G) for h in range(DN_H)])
        dq, dk, dv, dbg = vjp((_stack_heads(du_ref, G, 0, DN_D), _stack_heads(dw_ref, G, 0, DN_D),
                               _stack_heads(dqd_ref, G, 0, DN_D), _stack_heads(dkd_ref, G, 0, DN_D),
                               _stack_heads(da_ref, G, 0, CHUNK), dcd))
        for g in range(G):
            rows = slice(g * CHUNK, (g + 1) * CHUNK)
            for h in range(DN_H):
                n = g * DN_H + h
                dqkv_ref[rows, h * DN_D:(h + 1) * DN_D] = dq[n]
                dqkv_ref[rows, DNW + h * DN_D:DNW + (h + 1) * DN_D] = dk[n]
                dqkv_ref[rows, 2 * DNW + h * DN_D:2 * DNW + (h + 1) * DN_D] = dv[n]
        dbg_ref[...] = dbg.reshape(G * CHUNK, DN_D)

    return _hosted_call(
        body, "dn_intra_bwd", (B, nc // G),
        in_specs=[_tok(rows, CONVW), _tok(rows, 128), _tok(rows, AW)] + [_tok(rows, DNW)] * 4 + [_tok(rows, AW), _cd_spec(G)],
        out_specs=[_tok(rows, CONVW), _tok(rows, 128)],
        out_shape=[_sds((B, S, CONVW), F32), _sds((B, S, 128), F32)],
        scratch_shapes=[], semantics=("parallel", "parallel"), ins=(qkv, bg, tinv, du, dw, dqd, dkd, da, dcd), ex=ex)


def _dn_prep_bwd(dn, y, ba, conv_w, alog, dtb, dqkv, dbg, bt, ex=None):
    B, S, _ = dn.shape
    nt = S // bt

    def rev(f):
        return pl.BlockSpec((None, bt, f), lambda b, i: (b, nt - 1 - i, 0))

    def body(x_ref, y_ref, ba_ref, cw_ref, al_ref, dt_ref, dqkv_ref, dbg_ref,
             dx_ref, dba_ref, dcw_ref, dal_ref, ddt_ref, dye_ref):
        i = pl.program_id(1)
        ys = [y_ref[:, j * DN_D:(j + 1) * DN_D].astype(F32) for j in range(3 * DN_H)]
        _, vjp = jax.vjp(_dn_tail, ys, ba_ref[...], al_ref[...], dt_ref[...])
        d_out = [dqkv_ref[:, j * DN_D:(j + 1) * DN_D] for j in range(3 * DN_H)]
        dys, dba, dal, ddt = vjp((d_out, dbg_ref[...]))
        @pl.when(i == 0)
        def _():
            dye_ref[bt:bt + 8, :] = jnp.zeros((8, CONVW), F32)

        @pl.when(i > 0)
        def _():
            dye_ref[bt:bt + 8, :] = dye_ref[0:8, :]
        for j in range(3 * DN_H):
            dye_ref[0:bt, j * DN_D:(j + 1) * DN_D] = dys[j]
        shifted = [dye_ref[3 - j:bt + 3 - j, :] for j in range(CONV)]
        dx = cw_ref[0:1, :] * shifted[0]
        for j in range(1, CONV):
            dx = dx + cw_ref[j:j + 1, :] * shifted[j]
        dx_ref[...] = dx.astype(BF)
        x = x_ref[...].astype(F32)
        dcw = jnp.concatenate([jnp.sum(shifted[j] * x, axis=0, keepdims=True) for j in range(CONV)], axis=0)
        first = (i == 0) & (pl.program_id(0) == 0)
        dba_ref[...] = dba
        _acc(dcw_ref, dcw, first)
        _acc(dal_ref, dal, first)
        _acc(ddt_ref, ddt, first)

    return _hosted_call(
        body, "dn_prep_bwd", (B, nt),
        in_specs=[rev(CONVW), rev(CONVW), rev(128), _full((CONV, CONVW)), _full((1, 128)), _full((1, 128)), rev(CONVW), rev(128)],
        out_specs=[rev(CONVW), rev(128), _full((CONV, CONVW)), _full((1, 128)), _full((1, 128))],
        out_shape=[_sds((B, S, CONVW), BF), _sds((B, S, 128), F32), _sds((CONV, CONVW), F32), _sds((1, 128), F32),
                   _sds((1, 128), F32)],
        scratch_shapes=[pltpu.VMEM((bt + 8, CONVW), F32)],
        semantics=("arbitrary", "arbitrary"), ins=(dn, y, ba, conv_w, alog, dtb, dqkv, dbg), ex=ex)


def _attn_bwd(qkv, sinks, d_o, ex=None):
    B, S, _ = qkv.shape
    nb = S // BLK

    def cur(f):
        return pl.BlockSpec((None, BLK, f), lambda b, i: (b, jnp.minimum(i, nb - 1), 0))

    def out_prev(f):
        return pl.BlockSpec((None, BLK, f), lambda b, i: (b, jnp.maximum(i - 1, 0), 0))

    def body(qkv_ref, kvp_ref, sk_ref, do_ref, dq_ref, dkv_ref, dsk_ref, carry_ref):
        n = pl.program_id(1)
        first = (n == 0) & (pl.program_id(0) == 0)

        @pl.when(n == 0)
        def _():
            carry_ref[...] = jnp.zeros_like(carry_ref)

        @pl.when(n < nb)
        def _():
            qs, kc, kp, vc, vp = _attn_load(qkv_ref, kvp_ref)

            def f(qs, kc, kp, vc, vp, sk):
                return _attn_block(qs, kc, kp, vc, vp, sk, n > 0)
            _, vjp = jax.vjp(f, qs, kc, kp, vc, vp, sk_ref[...])
            d_outs = [do_ref[:, h * HD:(h + 1) * HD] for h in range(HEADS)]
            dqs, dkc, dkp, dvc, dvp, dsk = vjp(d_outs)
            for h in range(HEADS):
                dq_ref[:, h * HD:(h + 1) * HD] = dqs[h].astype(BF)
            for h in range(KV_HEADS):
                ksl = slice(h * HD, (h + 1) * HD)
                vsl = slice(KVW + h * HD, KVW + (h + 1) * HD)
                dkv_ref[:, ksl] = (carry_ref[:, ksl] + dkp[h]).astype(BF)
                dkv_ref[:, vsl] = (carry_ref[:, vsl] + dvp[h]).astype(BF)
                carry_ref[:, ksl] = dkc[h]
                carry_ref[:, vsl] = dvc[h]
            _acc(dsk_ref, dsk, first)

        @pl.when(n == nb)
        def _():
            dkv_ref[...] = carry_ref[...].astype(BF)

    return _hosted_call(
        body, "attn_bwd", (B, nb + 1),
        in_specs=[cur(QKV), _kv_prev_spec(lambda i: jnp.maximum(jnp.minimum(i, nb - 1) - 1, 0)), _full((1, HEADS)), cur(QW)],
        out_specs=[cur(QW), out_prev(2 * KVW), _full((1, HEADS))],
        out_shape=[_sds((B, S, QW), BF), _sds((B, S, 2 * KVW), BF), _sds((1, HEADS), F32)],
        scratch_shapes=[pltpu.VMEM((BLK, 2 * KVW), F32)],
        semantics=("arbitrary", "arbitrary"), ins=(qkv, qkv, sinks, d_o), ex=ex)


def _in_proj_bwd(x, mod, norm1_g, w_in, pieces, dba, dx1, bt):
    B, S, _ = x.shape
    widths = (QW, 2 * KVW, CONVW, DNW, D, D)

    def body(x_ref, mod_ref, g_ref, w_ref, dq_ref, dkv_ref, ddn_ref, dz_ref, dga_ref, dgb_ref, dba_ref, dx1_ref,
             gx_ref, dp_ref, dmod_ref, dg_ref):
        dp = jnp.concatenate([r[...] for r in (dq_ref, dkv_ref, ddn_ref, dz_ref, dga_ref, dgb_ref)]
                             + [dba_ref[...].astype(BF)], axis=-1)
        dp_ref[...] = dp
        dh = lax.dot_general(dp, w_ref[...], (((1,), (1,)), ((), ())), preferred_element_type=F32)
        _, vjp = jax.vjp(_norm_mod, x_ref[...], g_ref[...], mod_ref[:, 0:D], mod_ref[:, D:2 * D])
        dx, dgain, dshift, dscale = vjp(dh)
        gx_ref[...] = dx + dx1_ref[...]
        first = pl.program_id(1) == 0
        _acc(dmod_ref, jnp.concatenate([dshift, dscale], axis=-1), first)
        _acc(dg_ref, dgain, first)

    return pl.pallas_call(
        body, name="in_proj_bwd", grid=(B, S // bt),
        in_specs=[_tok(bt, D), _per_batch(6 * D), _full((1, D)), _resident((D, IN_PAD))] + [_tok(bt, w) for w in widths]
        + [_tok(bt, 128), _tok(bt, D)],
        out_specs=[_tok(bt, D), _tok(bt, IN_PAD), _per_batch(2 * D), _per_batch(D)],
        out_shape=[_sds((B, S, D), F32), _sds((B, S, IN_PAD), BF), _sds((B, 1, 2 * D), F32), _sds((B, 1, D), F32)],
        compiler_params=_cparams(dimension_semantics=("parallel", "arbitrary")),
    )(x, mod, norm1_g, w_in, *pieces, dba, dx1)


def _matmul_tn(tag, a, b, bk, bn, bt, col_blocks=False):
    T, K = a.shape
    N = b.shape[1]
    nt = T // bt
    if col_blocks:
        assert bk == K
        out_spec = pl.BlockSpec((None, bk, bn), lambda i, j, t: (j, 0, 0))
        out_shape = _sds((N // bn, K, bn), F32)
    else:
        out_spec = pl.BlockSpec((bk, bn), lambda i, j, t: (i, j))
        out_shape = _sds((K, N), F32)

    def body(a_ref, b_ref, o_ref, acc_ref):
        t = pl.program_id(2)

        @pl.when(t == 0)
        def _():
            acc_ref[...] = jnp.zeros_like(acc_ref)
        acc_ref[...] += lax.dot_general(a_ref[...], b_ref[...], (((0,), (0,)), ((), ())), preferred_element_type=F32)

        @pl.when(t == nt - 1)
        def _():
            o_ref[...] = acc_ref[...]

    return pl.pallas_call(
        body, name=f"grad_{tag}", grid=(K // bk, N // bn, nt),
        in_specs=[pl.BlockSpec((bt, bk), lambda i, j, t: (t, i)), pl.BlockSpec((bt, bn), lambda i, j, t: (t, j))],
        out_specs=out_spec, out_shape=out_shape,
        scratch_shapes=[pltpu.VMEM((bk, bn), F32)],
        compiler_params=_cparams(dimension_semantics=("parallel", "parallel", "arbitrary")),
    )(a, b)


def _rope_table(positions):
    inv_freq = THETA ** (-jnp.arange(0, ROT, 2, dtype=F32) / ROT)
    rest = jnp.zeros((HD - ROT,), F32)
    freq = jnp.concatenate([inv_freq, inv_freq, rest] * 2)
    sign = jnp.concatenate([-jnp.ones_like(inv_freq), jnp.ones_like(inv_freq), rest] * 2)
    ang = positions.astype(F32)[..., None] * freq
    return jnp.cos(ang), jnp.sin(ang) * sign


def _lane_pad(v, offset, width=128):
    return jnp.zeros((1, width), F32).at[0, offset:offset + v.shape[-1]].set(v.reshape(-1))


def _tile(S, want):
    return min(S, want)


class _Hosted:
    def __init__(self, call):
        self.call = call
        self.outs = None

    def __call__(self, ex):
        self.outs, landed = self.call(ex)
        return landed


def _local_step(x, mod, positions, tgt, norm1_g, w_in_pad, conv_w, q_norm_g, k_norm_g, sinks, a_log, dt_bias,
                dn_norm_g, w_branch, w_out, norm2_g, w_gu, w_dn, dist=None):
    B, S, _ = x.shape
    T = B * S
    cos_t, sin_t = _rope_table(positions)
    qg2 = jnp.concatenate([q_norm_g, q_norm_g], axis=-1)
    kg2 = jnp.concatenate([k_norm_g, k_norm_g], axis=-1)
    alog = _lane_pad(a_log, DN_H)
    dtb = _lane_pad(dt_bias, DN_H)
    conv2 = conv_w.reshape(CONV, CONVW)
    bt = _tile(S, 512)
    bt_mlp = _tile(S, 256)

    q, kv, dn, z, ga, gb, ba, h1 = _in_proj(x, mod, norm1_g, w_in_pad, bt)
    qkv_n = _qk_prep_fwd(q, kv, cos_t, sin_t, qg2, kg2, bt)
    o_attn = _attn_fwd(qkv_n, sinks)
    dqkv, bg, dn_y = _dn_prep(dn, ba, conv2, alog, dtb, bt)
    intra = _Hosted(lambda ex: _dn_intra_fwd(dqkv, bg, ex))
    if dist is None:
        intra(None)
    else:
        f_br, f_out, w_gu, f_dn = _gather_weights("late", [w_branch, w_out, w_gu, w_dn], dist[0], host=intra)
        w_branch, w_out, w_dn = (f.reshape(N_CHIP * f.shape[1], f.shape[2]) for f in (f_br, f_out, f_dn))
    dn_u, dn_w, dn_qd, dn_kd, dn_a, dn_cd, dn_tinv = intra.outs
    (o_raw, states), _ = _dn_rec_fwd(dn_u, dn_w, dn_qd, dn_kd, dn_a, dn_cd)
    x1, o_dn, merged = _mix_fwd(o_attn, o_raw, z, ga, gb, x, mod, dn_norm_g, w_branch, w_out, bt)
    dx1, h2, act, dgu, dyy, loss, dmod2, dnorm2 = _mlp(x1, tgt, mod, norm2_g, w_gu, w_dn, bt_mlp)

    def flat(t):
        return t.reshape(T, t.shape[-1])
    tn = functools.partial(_matmul_tn, bt=_tile(T, 1024))
    g_w_dn = tn("w_down", flat(act), flat(dyy), bk=FFN, bn=D // 2)
    g_w_gu = tn("w_gate_up", flat(h2), flat(dgu), bk=D, bn=2 * FFN // N_CHIP, col_blocks=True)

    mix_b = _Hosted(lambda ex: _mix_bwd(o_attn, o_raw, z, ga, gb, x, mod, dn_norm_g, w_branch, w_out, dx1, bt_mlp, ex))
    rec_b = _Hosted(lambda ex: _dn_rec_bwd(dn_u, dn_w, dn_qd, dn_kd, dn_a, dn_cd, states, mix_b.outs[1], ex))
    intra_b = _Hosted(lambda ex: _dn_intra_bwd(dqkv, bg, dn_tinv, *rec_b.outs, ex))
    if dist is None:
        for host in (mix_b, rec_b, intra_b):
            host(None)
    else:
        g_w_gu, g_w_dn = _reduce_grads(("w_gate_up", "w_down"), [g_w_gu, g_w_dn.reshape(N_CHIP, -1, D)], *dist,
                                       hosts=[mix_b, rec_b, intra_b])
    d_oa, _, dz, dga, dgb, dya, dyd, dout, dgate1, ddn_g = mix_b.outs
    d_dqkv, dbg = intra_b.outs
    g_w_out = tn("w_out", flat(merged), flat(dout), bk=D, bn=D)
    g_w_br = jnp.concatenate([tn("w_branch_attn", flat(o_attn), flat(dya), bk=QW, bn=D),
                              tn("w_branch_dn", flat(o_dn), flat(dyd), bk=DNW, bn=D)], axis=0)
    prep_b = _Hosted(lambda ex: _dn_prep_bwd(dn, dn_y, ba, conv2, alog, dtb, d_dqkv, dbg, bt, ex))
    attn_b = _Hosted(lambda ex: _attn_bwd(qkv_n, sinks, d_oa, ex))
    qk_b = _Hosted(lambda ex: _qk_prep_bwd(q, kv, cos_t, sin_t, qg2, kg2, *attn_b.outs[:2], bt, ex))
    if dist is None:
        for host in (prep_b, attn_b, qk_b):
            host(None)
    else:
        g_w_br, g_w_out = _reduce_grads(("w_branch", "w_out"), [g_w_br.reshape(N_CHIP, -1, D), g_w_out.reshape(N_CHIP, -1, D)],
                                        *dist, hosts=[prep_b, attn_b, qk_b])
    d_dn, dba, dconv, dalog, ddtb = prep_b.outs
    dsk = attn_b.outs[2]
    dq, dkv, dqg2, dkg2 = qk_b.outs
    dqg = dqg2[:, :HD] + dqg2[:, HD:]
    dkg = dkg2[:, :HD] + dkg2[:, HD:]
    grad_x, dproj, dmod1, dnorm1 = _in_proj_bwd(x, mod, norm1_g, w_in_pad, (dq, dkv, d_dn, dz, dga, dgb), dba, dx1, bt)
    g_w_in = tn("w_in", flat(h1), flat(dproj), bk=D, bn=IN_PAD // 3)

    dmod = jnp.concatenate([dmod1, dgate1, dmod2], axis=-1)
    small = dict(norm1_g=jnp.sum(dnorm1, axis=0), norm2_g=jnp.sum(dnorm2, axis=0), q_norm_g=dqg, k_norm_g=dkg,
                 sinks=dsk, a_log=dalog[:, DN_H:2 * DN_H], dt_bias=ddtb[:, DN_H:2 * DN_H],
                 dn_norm_g=jnp.sum(ddn_g, axis=0), conv_w=dconv)
    return jnp.sum(loss), grad_x, dmod, small, (g_w_in, g_w_br, g_w_out, g_w_gu, g_w_dn)


def _flip(me, f):
    return (me[0] ^ ((f >> 2) & 1), me[1] ^ ((f >> 1) & 1), me[2] ^ (f & 1))


def _comm_call(name, ex):
    n_in, n_out = len(ex.ins), len(ex.out_shapes)

    def body(*refs):
        out_refs, sems = refs[n_in:n_in + n_out], refs[n_in + n_out:]
        cps = _exchange_copies(ex, refs[:n_in], out_refs, sems[0], sems[1])
        for cp in cps:
            cp.start()
        for cp in cps:
            cp.wait_recv()
        if ex.n_forward:
            fwd = _forward_copies(ex, out_refs, sems[2], sems[3])
            for cp in fwd:
                cp.start()
            for cp in fwd:
                cp.wait_recv()
            cps = cps + fwd
        for cp in cps:
            cp.wait_send()

    any_spec = pl.BlockSpec(memory_space=pl.ANY)
    return pl.pallas_call(
        body, name=name, in_specs=[any_spec] * n_in, out_specs=[any_spec] * n_out, out_shape=list(ex.out_shapes),
        scratch_shapes=_exchange_sems(ex),
    )(*ex.ins)


def _by_origin(own, received, index):
    stack = jnp.concatenate([own[None], received], axis=0)
    n = stack.shape[0]
    return jnp.stack([lax.dynamic_index_in_dim(stack, k ^ index, 0, keepdims=False) for k in range(n)])


def _gather_devices(name, arrs, dev):
    def plan(me, in_refs, out_refs):
        return [(a, o.at[f - 1], _flip(me, f)) for a, o in zip(in_refs, out_refs) for f in range(1, N_DEV)]
    outs = tuple(_sds((N_DEV - 1,) + a.shape, a.dtype) for a in arrs)
    got = _comm_call(name, _Exchange(tuple(arrs), outs, (N_DEV - 1) * len(arrs), plan))
    return [_by_origin(a, g, dev) for a, g in zip(arrs, got)]


def _gather_chips(name, arrs, chip):
    def plan(me, in_refs, out_refs):
        return [(a, o.at[j], _flip(me, 2 * (j + 1))) for a, o in zip(in_refs, out_refs) for j in range(N_CHIP - 1)]
    outs = tuple(_sds((N_CHIP - 1,) + a.shape, a.dtype) for a in arrs)
    got = _comm_call(name, _Exchange(tuple(arrs), outs, (N_CHIP - 1) * len(arrs), plan))
    return [_by_origin(a, g, chip) for a, g in zip(arrs, got)]


def _halves(core, mine, other):
    lo = jnp.where(core == 0, mine, other)
    hi = jnp.where(core == 0, other, mine)
    return jnp.concatenate([lo, hi], axis=-2)


def _swap_cores_ex(arrs):
    def plan(me, in_refs, out_refs):
        return [(g, o, _flip(me, 1)) for g, o in zip(in_refs, out_refs)]
    return _Exchange(tuple(arrs), tuple(_sds(g.shape, g.dtype) for g in arrs), len(arrs), plan)


def _gather_weights(tag, shards, chip, host=None):
    def plan(me, in_refs, out_refs):
        chip_me = 2 * me[0] + me[1]
        remote = []
        for a, o in zip(in_refs, out_refs):
            half = a.shape[0] // 2
            mine = a.at[pl.ds(me[2] * half, half)]
            remote += [(mine, o.at[chip_me, me[2]], _flip(me, 2 * (j + 1))) for j in range(N_CHIP - 1)]
        return remote

    def forward(me, out_refs):
        chip_me = 2 * me[0] + me[1]
        return [(o.at[chip_me ^ (j + 1), me[2]], o.at[chip_me ^ (j + 1), me[2]], _flip(me, 1))
                for o in out_refs for j in range(N_CHIP - 1)]
    run = host or functools.partial(_comm_call, f"weights_{tag}")
    n = (N_CHIP - 1) * len(shards)
    landed = run(_Exchange(tuple(shards), tuple(_sds((N_CHIP, 2, a.shape[0] // 2, a.shape[1]), a.dtype) for a in shards),
                           n, plan, n, forward))
    return [lax.dynamic_update_slice(f.reshape((N_CHIP,) + a.shape), a[None], (chip, 0, 0)) for a, f in zip(shards, landed)]


def _rows(r):
    for br in (512, 352, 256, 128, 64, 32, 16, 8):
        if r % br == 0:
            return br
    raise ValueError(r)


def _pair_add(tag, g, recv, c):
    n, r, cols = g.shape
    half = r // 2
    br = _rows(half)
    nb = half // br

    def body(c_ref, g_ref, r_ref, o_ref):
        o_ref[...] = (g_ref[...] + r_ref[...]).astype(BF)

    return pl.pallas_call(
        body, name=f"pair_add_{tag}",
        grid_spec=pltpu.PrefetchScalarGridSpec(
            num_scalar_prefetch=1, grid=(n, nb),
            in_specs=[pl.BlockSpec((None, br, cols), lambda k, i, c_ref: (k, c_ref[0] * nb + i, 0)),
                      pl.BlockSpec((None, br, cols), lambda k, i, c_ref: (k, i, 0))],
            out_specs=pl.BlockSpec((None, br, cols), lambda k, i, c_ref: (k, i, 0))),
        out_shape=_sds((n, half, cols), BF),
        compiler_params=_cparams(dimension_semantics=("parallel", "parallel")),
    )(c, g, recv)


def _sum_chips(tag, p, q, chip):
    n, r, cols = q.shape
    br = _rows(r)

    def body(chip_ref, p_ref, q_ref, o_ref):
        acc = p_ref[...].astype(F32)
        for k in range(n):
            acc = acc + q_ref[k].astype(F32)
        o_ref[...] = acc

    return pl.pallas_call(
        body, name=f"sum_chips_{tag}",
        grid_spec=pltpu.PrefetchScalarGridSpec(
            num_scalar_prefetch=1, grid=(r // br,),
            in_specs=[pl.BlockSpec((None, br, cols), lambda i, chip_ref: (chip_ref[0], i, 0)),
                      pl.BlockSpec((n, br, cols), lambda i, chip_ref: (0, i, 0))],
            out_specs=pl.BlockSpec((br, cols), lambda i, chip_ref: (i, 0))),
        out_shape=_sds((r, cols), F32),
        compiler_params=_cparams(dimension_semantics=("parallel",)),
    )(chip, p, q)


def _reduce_grads(tags, grads, chip, core, hosts=None):
    core_arr = core.reshape(1).astype(jnp.int32)
    chip_arr = chip.reshape(1).astype(jnp.int32)
    name = "_".join(tags)
    run = hosts or [functools.partial(_comm_call, f"grads_{stage}_{name}") for stage in ("pair", "chips", "swap")]

    def plan_pair(me, in_refs, out_refs):
        remote = []
        for g, o in zip(in_refs, out_refs):
            half = g.shape[1] // 2
            remote += [(g.at[k, pl.ds((1 - me[2]) * half, half)], o.at[k], _flip(me, 1)) for k in range(N_CHIP)]
        return remote
    recv = run[0](_Exchange(tuple(grads), tuple(_sds((N_CHIP, g.shape[1] // 2, g.shape[2]), F32) for g in grads),
                            N_CHIP * len(grads), plan_pair))
    pair = [_pair_add(t, g, r, core_arr) for t, g, r in zip(tags, grads, recv)]

    def plan_chips(me, in_refs, out_refs):
        remote = []
        for p, o in zip(in_refs, out_refs):
            for j in range(N_CHIP - 1):
                peer = _flip(me, 2 * (j + 1))
                remote.append((p.at[2 * peer[0] + peer[1]], o.at[j], peer))
        return remote
    parts = run[1](_Exchange(tuple(pair), tuple(_sds((N_CHIP - 1,) + p.shape[1:], BF) for p in pair),
                             (N_CHIP - 1) * len(pair), plan_chips))
    mine = [_sum_chips(t, p, q, chip_arr) for t, p, q in zip(tags, pair, parts)]
    other = run[2](_swap_cores_ex(mine))
    return [_halves(core, h, o) for h, o in zip(mine, other)]


def _adamw_math(w, g, m, v):
    m = ADAM_B1 * m + (1.0 - ADAM_B1) * g
    v = ADAM_B2 * v + (1.0 - ADAM_B2) * (g * g)
    m_hat = m / (1.0 - ADAM_B1 ** ADAM_STEP)
    v_hat = v / (1.0 - ADAM_B2 ** ADAM_STEP)
    delta = -ADAM_LR * (m_hat / (jnp.sqrt(v_hat) + ADAM_EPS) + ADAM_WD * w)
    return delta, m, v


def _adamw(name, w, g, m, v):
    r, cols = w.shape
    br = _rows(r)
    if br * cols * 4 > (1 << 20) and br % 16 == 0:
        br //= 2

    def body(w_ref, g_ref, m_ref, v_ref, d_ref, mo_ref, vo_ref):
        d_ref[...], mo_ref[...], vo_ref[...] = _adamw_math(w_ref[...], g_ref[...], m_ref[...], v_ref[...])

    spec = pl.BlockSpec((br, cols), lambda i: (i, 0))
    return pl.pallas_call(
        body, name=f"adamw_{name}", grid=(r // br,), in_specs=[spec] * 4, out_specs=[spec] * 3,
        out_shape=[_sds((r, cols), F32)] * 3,
        compiler_params=_cparams(dimension_semantics=("parallel",)),
    )(w, g, m, v)


def _ada_fwd(c_all, ada_w, ada_b_cols):
    n = c_all.shape[0]

    def body(c_ref, w_ref, b_ref, o_ref):
        o_ref[...] = _mmx(_silu(c_ref[...]), w_ref[...]) + b_ref[...]

    return pl.pallas_call(
        body, name="ada_fwd", out_shape=_sds((n, ada_w.shape[1]), F32), compiler_params=_cparams(),
    )(c_all, ada_w, ada_b_cols)


def _ada_bwd(c_all, dmod_cols, w, m, v):
    n = c_all.shape[0]
    r, cols = w.shape
    br = 128

    def body(c_ref, d_ref, w_ref, m_ref, v_ref, g_ref, dl_ref, mo_ref, vo_ref):
        cond = _silu(c_ref[...])
        g = lax.dot_general(cond, d_ref[...], (((0,), (0,)), ((), ())), precision=lax.Precision.HIGHEST,
                            preferred_element_type=F32)
        g_ref[...] = g
        dl_ref[...], mo_ref[...], vo_ref[...] = _adamw_math(w_ref[...], g, m_ref[...], v_ref[...])

    spec = pl.BlockSpec((br, cols), lambda i: (i, 0))
    return pl.pallas_call(
        body, name="ada_bwd", grid=(r // br,),
        in_specs=[pl.BlockSpec((n, br), lambda i: (0, i)), pl.BlockSpec((n, cols), lambda i: (0, 0)), spec, spec, spec],
        out_specs=[spec] * 4, out_shape=[_sds((r, cols), F32)] * 4,
        compiler_params=_cparams(dimension_semantics=("parallel",)),
    )(c_all, dmod_cols, w, m, v)


def _sum_devices(parts):
    n, r, cols = parts.shape

    def body(p_ref, o_ref):
        acc = p_ref[0]
        for k in range(1, n):
            acc = acc + p_ref[k]
        o_ref[...] = acc

    return pl.pallas_call(body, name="sum_devices", out_shape=_sds((r, cols), F32), compiler_params=_cparams())(parts)


SMALL_ROWS = 16
_SMALL_SLOTS = dict(norm1_g=(0, 0, D), norm2_g=(1, 0, D), q_norm_g=(2, 0, HD), k_norm_g=(2, 128, HD), sinks=(2, 256, HEADS),
                    a_log=(2, 384, DN_H), dt_bias=(2, 512, DN_H), dn_norm_g=(2, 640, DN_D))
_CONV_ROW = 4
_ADA_B_ROW = 8


def _pack_small(vals, conv, ada_b):
    def row(pieces):
        out, at = [], 0
        for col, val in pieces:
            out += [jnp.zeros((1, col - at), F32), val.reshape(1, -1)]
            at = col + val.size
        return jnp.concatenate(out + [jnp.zeros((1, CONVW - at), F32)], axis=1)
    rows = {}
    for name, (r, col, n) in _SMALL_SLOTS.items():
        rows.setdefault(r, []).append((col, vals[name]))
    blank = jnp.zeros((1, CONVW), F32)
    top = [row(sorted(rows[r], key=lambda p: p[0])) if r in rows else blank for r in range(_CONV_ROW)]
    conv_rows = jnp.concatenate([conv, jnp.zeros((CONV, CONVW - conv.shape[1]), F32)], axis=1)
    tail = jnp.zeros((SMALL_ROWS - _ADA_B_ROW - 4, CONVW), F32)
    return jnp.concatenate(top + [conv_rows, ada_b.reshape(4, CONVW), tail], axis=0)


def _unpack_small(sheet, conv_cols):
    out = {name: sheet[row, col:col + n].reshape(1, n) for name, (row, col, n) in _SMALL_SLOTS.items()}
    out["conv_w"] = sheet[_CONV_ROW:_CONV_ROW + CONV, 0:conv_cols].reshape(1, CONV, 1, conv_cols)
    out["ada_b"] = sheet[_ADA_B_ROW:_ADA_B_ROW + 4, :].reshape(1, 6 * D)
    return out


def _w_in_segments():
    shard = IN_WIDTH // N_CHIP
    cuts = sorted({0, IN_WIDTH, C_Z, C_Z + 2 * DN_H} | {k * shard for k in range(1, N_CHIP)})
    segs = []
    for a, b in zip(cuts[:-1], cuts[1:]):
        k = a // shard
        pad = a if a < C_Z else (C_BA + a - C_Z if a < C_Z + 2 * DN_H else a - 2 * DN_H)
        segs.append((k, a - k * shard, b - k * shard, pad))
    return segs


def _pad_w_in(f):
    parts = [f[k][:, lo:hi] for k, lo, hi, _ in sorted(_w_in_segments(), key=lambda s: s[3])]
    return jnp.concatenate(parts + [jnp.zeros((f.shape[1], IN_PAD - IN_WIDTH), f.dtype)], axis=1)


def _unpad_w_in(g):
    return jnp.stack([jnp.concatenate([g[:, pad:pad + hi - lo] for kk, lo, hi, pad in _w_in_segments() if kk == k], axis=1)
                      for k in range(N_CHIP)])


def _blocks_to_cols(f):
    return f.transpose(1, 0, 2).reshape(f.shape[1], N_CHIP * f.shape[2])


def kernel(x, c, positions, ada_w, ada_b, norm1_g, w_in, conv_w, q_norm_g, k_norm_g, sinks, a_log, dt_bias, dn_norm_g, w_branch, w_out, norm2_g, w_gate_up, w_down, loss_target, m_ada_w, m_ada_b, m_norm1_g, m_w_in, m_conv_w, m_q_norm_g, m_k_norm_g, m_sinks, m_a_log, m_dt_bias, m_dn_norm_g, m_w_branch, m_w_out, m_norm2_g, m_w_gate_up, m_w_down, v_ada_w, v_ada_b, v_norm1_g, v_w_in, v_conv_w, v_q_norm_g, v_k_norm_g, v_sinks, v_a_log, v_dt_bias, v_dn_norm_g, v_w_branch, v_w_out, v_norm2_g, v_w_gate_up, v_w_down):
    ix, iy, ic = lax.axis_index("x"), lax.axis_index("y"), lax.axis_index("c")
    dev = 4 * ix + 2 * iy + ic
    chip = 2 * ix + iy
    n_seq = x.shape[0]
    conv_cols = conv_w.shape[-1]

    c_all, conv_all = _gather_devices("gather_cond", [c, conv_w.reshape(CONV, conv_cols)], dev)
    c_all = c_all.reshape(N_DEV * n_seq, D)
    ada_cols = ada_w.shape[-1]
    ada_b_cols = lax.dynamic_slice(ada_b, (0, chip * ada_cols), (1, ada_cols))
    mod_cols = _ada_fwd(c_all, ada_w[0], ada_b_cols)
    (mod_blocks,) = _gather_chips("gather_mod", [mod_cols], chip)
    mod_all = _blocks_to_cols(mod_blocks)
    mod = lax.dynamic_slice(mod_all, (dev * n_seq, 0), (n_seq, 6 * D)).reshape(n_seq, 1, 6 * D)
    conv_full = _blocks_to_cols(conv_all[0::2])

    (f_in,) = _gather_weights("w_in", [w_in[0].astype(BF)], chip)
    w_in_pad = _pad_w_in(f_in)

    loss, grad_x, dmod, small, (g_in, r_br, r_out, r_gu, r_dn) = _local_step(
        x, mod, positions, loss_target, norm1_g, w_in_pad, conv_full.reshape(CONV, 1, CONVW), q_norm_g, k_norm_g, sinks,
        a_log, dt_bias, dn_norm_g, w_branch[0].astype(BF), w_out[0].astype(BF), norm2_g, w_gate_up[0].astype(BF),
        w_down[0].astype(BF), dist=(chip, ic))
    loss = lax.psum(loss, ("x", "y", "c"))

    (r_in,) = _reduce_grads(("w_in",), [_unpad_w_in(g_in)], chip, ic)
    big = {}
    for name, w, g, m, v in (("w_in", w_in, r_in, m_w_in, v_w_in), ("w_branch", w_branch, r_br, m_w_branch, v_w_branch),
                             ("w_out", w_out, r_out, m_w_out, v_w_out), ("w_gate_up", w_gate_up, r_gu, m_w_gate_up, v_w_gate_up),
                             ("w_down", w_down, r_dn, m_w_down, v_w_down)):
        big[name] = (g,) + tuple(_adamw(name, w[0], g, m[0], v[0]))

    part = _pack_small(small, small["conv_w"], jnp.sum(dmod, axis=(0, 1)).reshape(1, 6 * D))
    dmod_all, parts = _gather_devices("gather_small", [dmod.reshape(n_seq, 6 * D), part], dev)
    dmod_all = dmod_all.reshape(N_DEV * n_seq, 6 * D)
    g_small = _unpack_small(_sum_devices(parts), CONVW)
    g_conv = lax.dynamic_slice(g_small["conv_w"].reshape(CONV, CONVW), (0, chip * conv_cols), (CONV, conv_cols))
    g_small["conv_w"] = g_conv.reshape(1, CONV, 1, conv_cols)

    given = dict(norm1_g=(norm1_g, m_norm1_g, v_norm1_g), norm2_g=(norm2_g, m_norm2_g, v_norm2_g),
                 q_norm_g=(q_norm_g, m_q_norm_g, v_q_norm_g), k_norm_g=(k_norm_g, m_k_norm_g, v_k_norm_g),
                 sinks=(sinks, m_sinks, v_sinks), a_log=(a_log, m_a_log, v_a_log), dt_bias=(dt_bias, m_dt_bias, v_dt_bias),
                 dn_norm_g=(dn_norm_g, m_dn_norm_g, v_dn_norm_g))
    sheets = [_pack_small({k: t[j] for k, t in given.items()}, cw.reshape(CONV, conv_cols), ab)
              for j, (cw, ab) in enumerate(((conv_w, ada_b), (m_conv_w, m_ada_b), (v_conv_w, v_ada_b)))]
    g_local = _pack_small(g_small, g_conv, g_small["ada_b"])
    upd = [_unpack_small(s, conv_cols) for s in _adamw("small", sheets[0], g_local, sheets[1], sheets[2])]

    dmod_cols = lax.dynamic_slice(dmod_all, (0, chip * ada_cols), (N_DEV * n_seq, ada_cols))
    ada = _ada_bwd(c_all, dmod_cols, ada_w[0], m_ada_w[0], v_ada_w[0])

    names = ["ada_w", "ada_b", "norm1_g", "w_in", "conv_w", "q_norm_g", "k_norm_g", "sinks", "a_log", "dt_bias", "dn_norm_g",
             "w_branch", "w_out", "norm2_g", "w_gate_up", "w_down"]

    def leaf(name, j):
        if name == "ada_w":
            return ada[j][None]
        if name in big:
            return big[name][j][None]
        return g_small[name] if j == 0 else upd[j - 1][name]

    return (loss, grad_x) + tuple(leaf(n, j) for j in range(4) for n in names)
```

```python
import functools
from typing import Callable, NamedTuple

import jax
import jax.numpy as jnp
import numpy as np
from jax import lax
from jax.experimental import pallas as pl
from jax.experimental.pallas import tpu as pltpu

F32 = jnp.float32
BF = jnp.bfloat16

D = 1024
HEADS = 8
KV_HEADS = 2
GROUP = 4
HD = 64
BLK = 128
ROT = 16
THETA = 500000.0
QW = 512
KVW = 128
DN_H = 4
DN_D = 128
CONV = 4
CHUNK = 64
DNW = 512
CONVW = 1536
FFN = 2816
EPS = 1e-6
IN_WIDTH = 4872
IN_PAD = 4992
C_KV = 512
C_DN = 768
C_Z = 2304
C_GA = 2816
C_GB = 3840
C_BA = 4864
NEG = -1e30
N_DEV = 8
N_CHIP = 4

ADAM_LR = 0.001
ADAM_B1 = 0.9
ADAM_B2 = 0.999
ADAM_EPS = 1e-08
ADAM_WD = 0.01
ADAM_STEP = 10

VMEM_LIMIT = 60 * 1024 * 1024


def _cparams(**kw):
    return pltpu.CompilerParams(vmem_limit_bytes=VMEM_LIMIT, **kw)


def _dg(a, b, ca, cb):
    return lax.dot_general(a.astype(BF), b.astype(BF), (((ca,), (cb,)), ((), ())),
                           preferred_element_type=F32)


@jax.custom_vjp
def _mm(a, b):
    return _dg(a, b, 1, 0)


def _mm_fwd(a, b):
    return _dg(a, b, 1, 0), (a, b)


def _mm_bwd(res, dy):
    a, b = res
    return _dg(dy, b, 1, 1).astype(a.dtype), _dg(a, dy, 0, 0).astype(b.dtype)


_mm.defvjp(_mm_fwd, _mm_bwd)


@jax.custom_vjp
def _mm_nt(a, b):
    return _dg(a, b, 1, 1)


def _mm_nt_fwd(a, b):
    return _dg(a, b, 1, 1), (a, b)


def _mm_nt_bwd(res, dy):
    a, b = res
    return _dg(dy, b, 1, 0).astype(a.dtype), _dg(dy, a, 0, 0).astype(b.dtype)


_mm_nt.defvjp(_mm_nt_fwd, _mm_nt_bwd)


@jax.custom_vjp
def _mm_tn(a, b):
    return _dg(a, b, 0, 0)


def _mm_tn_fwd(a, b):
    return _dg(a, b, 0, 0), (a, b)


def _mm_tn_bwd(res, dy):
    a, b = res
    return _dg(b, dy, 1, 1).astype(a.dtype), _dg(a, dy, 1, 0).astype(b.dtype)


_mm_tn.defvjp(_mm_tn_fwd, _mm_tn_bwd)


def _mmx(a, b):
    return jnp.dot(a, b, precision=lax.Precision.HIGHEST, preferred_element_type=F32)


def _mmx_nt(a, b):
    return lax.dot_general(a, b, (((1,), (1,)), ((), ())), precision=lax.Precision.HIGHEST,
                           preferred_element_type=F32)


def _iota(shape, dim):
    return lax.broadcasted_iota(jnp.int32, shape, dim)


def _sigmoid(x):
    return 1.0 / (1.0 + jnp.exp(-x))


def _silu(x):
    return x * _sigmoid(x)


def _softplus(x):
    return jnp.maximum(x, 0.0) + jnp.log(1.0 + jnp.exp(-jnp.abs(x)))


def _rms(x, gain):
    return x * lax.rsqrt(jnp.mean(x * x, axis=-1, keepdims=True) + EPS) * gain


def _norm_mod(x, gain, shift, scale):
    return _rms(x, gain) * (1.0 + scale) + shift


def _split(a):
    hi = a.astype(BF)
    return hi, (a - hi.astype(F32)).astype(BF)


def _dg3(a, b, ca, cb):
    ah, al = _split(a)
    bh, bl = _split(b)

    def dg(x, y):
        return lax.dot_general(x, y, (((ca,), (cb,)), ((), ())), preferred_element_type=F32)
    return dg(ah, bh) + (dg(ah, bl) + dg(al, bh))


@jax.custom_vjp
def _mm3(a, b):
    return _dg3(a, b, 1, 0)


def _mm3_fwd(a, b):
    return _dg3(a, b, 1, 0), (a, b)


def _mm3_bwd(res, dy):
    a, b = res
    return _dg3(dy, b, 1, 1), _dg3(a, dy, 0, 0)


_mm3.defvjp(_mm3_fwd, _mm3_bwd)


def _qk_prep(slabs, gain, cos, sin):
    r = _iota((2 * HD, 2 * HD), 0)
    c = _iota((2 * HD, 2 * HD), 1)
    seg = jnp.where(r // HD == c // HD, 1.0 / HD, 0.0).astype(F32)
    half = ROT // 2
    cd = c % HD
    pair = jnp.where(((cd < half) & (r == c + half)) | ((cd >= half) & (cd < ROT) & (r == c - half)), 1.0, 0.0).astype(F32)
    out = []
    for x in slabs:
        y = x * lax.rsqrt(_mm3(x * x, seg) + EPS) * gain
        out.append(y * cos + _mm3(y, pair) * sin)
    return out


def _attn_block(qs, kc, kp, vc, vp, sinks, has_prev):
    rows = GROUP * BLK
    qi = _iota((rows, 2 * BLK), 0) % BLK + BLK
    kj = _iota((rows, 2 * BLK), 1)
    dist = qi - kj
    valid = (dist >= 0) & (dist < BLK) & ((kj >= BLK) | has_prev)
    grp = _iota((rows, HEADS), 0) // BLK
    col = _iota((rows, HEADS), 1)

    outs = []
    for h in range(KV_HEADS):
        q = jnp.concatenate([qs[h * GROUP + g] for g in range(GROUP)], axis=0)
        k = jnp.concatenate([kp[h], kc[h]], axis=0)
        v = jnp.concatenate([vp[h], vc[h]], axis=0)
        s = _mm_nt(q, k) * (HD ** -0.5)
        s = jnp.where(valid, s, NEG)
        sink = jnp.sum(jnp.where(col == h * GROUP + grp, sinks, 0.0), axis=-1, keepdims=True)
        m = lax.stop_gradient(jnp.maximum(jnp.max(s, axis=-1, keepdims=True), sink))
        p = jnp.exp(s - m)
        probs = p / (jnp.sum(p, axis=-1, keepdims=True) + jnp.exp(sink - m))
        o = _mm(probs, v)
        outs += [o[g * BLK:(g + 1) * BLK] for g in range(GROUP)]
    return outs


def _dn_tail(ys, ba, alog, dtb):
    def l2(t):
        return t * lax.rsqrt(jnp.sum(t * t, axis=-1, keepdims=True) + EPS)
    s = [_silu(y) for y in ys]
    out = [l2(t) for t in s[:2 * DN_H]] + s[2 * DN_H:]
    lane = _iota(ba.shape, 1)
    beta = _sigmoid(ba)
    g = -jnp.exp(alog) * _softplus(ba + dtb)
    bg = jnp.where(lane < DN_H, beta, jnp.where(lane < 2 * DN_H, g, 0.0))
    return out, bg


def _bdg(a, b, ca, cb):
    return lax.dot_general(a.astype(BF), b.astype(BF), (((ca,), (cb,)), ((0,), (0,))), preferred_element_type=F32)


@jax.custom_vjp
def _bmm(a, b):
    return _bdg(a, b, 2, 1)


def _bmm_fwd(a, b):
    return _bdg(a, b, 2, 1), (a, b)


def _bmm_bwd(res, dy):
    a, b = res
    return _bdg(dy, b, 2, 2), _bdg(a, dy, 1, 1)


_bmm.defvjp(_bmm_fwd, _bmm_bwd)


@jax.custom_vjp
def _bmm_nt(a, b):
    return _bdg(a, b, 2, 2)


def _bmm_nt_fwd(a, b):
    return _bdg(a, b, 2, 2), (a, b)


def _bmm_nt_bwd(res, dy):
    a, b = res
    return _bdg(dy, b, 2, 1), _bdg(dy, a, 1, 1)


_bmm_nt.defvjp(_bmm_nt_fwd, _bmm_nt_bwd)


def _bmmx(a, b):
    return lax.dot_general(a, b, (((2,), (1,)), ((0,), (0,))), precision=lax.Precision.HIGHEST,
                           preferred_element_type=F32)


def _neumann_inverse(lmat):
    C = CHUNK
    eye = jnp.where(_iota((C, C), 0) == _iota((C, C), 1), 1.0, 0.0).astype(F32)[None]
    a = -lmat
    tinv = eye + a
    pw = _bmmx(a, a)
    for _ in range(4):
        both = _bmmx(jnp.concatenate([pw, tinv], axis=1), pw)
        pw, tinv = both[:, :C], tinv + both[:, C:]
    return tinv + _bmmx(tinv, pw)


def _inverse_bwd(tinv, d_tinv):
    x = lax.dot_general(d_tinv, tinv, (((2,), (2,)), ((0,), (0,))), precision=lax.Precision.HIGHEST,
                        preferred_element_type=F32)
    return -lax.dot_general(tinv, x, (((1,), (1,)), ((0,), (0,))), precision=lax.Precision.HIGHEST,
                            preferred_element_type=F32)


@jax.custom_vjp
def _tri_inverse(lmat):
    return _neumann_inverse(lmat)


def _tri_inverse_fwd(lmat):
    tinv = _neumann_inverse(lmat)
    return tinv, tinv


def _tri_inverse_bwd(tinv, d_tinv):
    return (_inverse_bwd(tinv, d_tinv),)


_tri_inverse.defvjp(_tri_inverse_fwd, _tri_inverse_bwd)


@jax.custom_vjp
def _tri_inverse_known(lmat, tinv):
    return tinv


def _tri_inverse_known_fwd(lmat, tinv):
    return tinv, tinv


def _tri_inverse_known_bwd(tinv, d_tinv):
    return _inverse_bwd(tinv, d_tinv), jnp.zeros_like(tinv)


_tri_inverse_known.defvjp(_tri_inverse_known_fwd, _tri_inverse_known_bwd)


def _dn_intra(q, k, v, bg, tinv=None):
    C = CHUNK
    G = bg.shape[0]
    r = _iota((C, C), 0)
    c = _iota((C, C), 1)
    incl = (r >= c)[None]
    strict = (r > c)[None]
    eye = jnp.where(r == c, 1.0, 0.0).astype(F32)[None]
    tri = jnp.broadcast_to(jnp.where(r >= c, 1.0, 0.0).astype(F32)[None], (G, C, C))
    gc_all = _bmmx(tri, bg)
    lane = _iota((C, DN_D), 1)

    def per_head(x, offset):
        return jnp.concatenate([jnp.sum(jnp.where(lane == offset + h, x[g], 0.0), axis=-1, keepdims=True)[None]
                                for g in range(G) for h in range(DN_H)], axis=0)
    beta = per_head(bg, 0)
    gcol = per_head(gc_all, DN_H)
    grow = jnp.sum(eye * gcol, axis=1, keepdims=True)
    glast = jnp.sum(jnp.where(_iota((1, C, 1), 1) == C - 1, gcol, 0.0), axis=1, keepdims=True)
    decay = jnp.exp(jnp.where(incl, gcol - grow, NEG))
    q = q * (DN_D ** -0.5)
    kb = k * beta
    lmat = jnp.where(strict, _bmm_nt(kb, k) * decay, 0.0)
    tinv = _tri_inverse(lmat) if tinv is None else _tri_inverse_known(lmat, tinv)
    egc = jnp.exp(gcol)
    u = _bmm(tinv, v * beta)
    w = _bmm(tinv, kb * egc)
    a = _bmm_nt(q, k) * decay
    return u, w, q * egc, k * jnp.exp(glast - gcol), a, jnp.exp(glast), tinv


@jax.custom_vjp
def _bmm_tn(a, b):
    return _bdg(a, b, 1, 1)


def _bmm_tn_fwd(a, b):
    return _bdg(a, b, 1, 1), (a, b)


def _bmm_tn_bwd(res, dy):
    a, b = res
    return _bdg(b, dy, 2, 2), _bdg(a, dy, 2, 1)


_bmm_tn.defvjp(_bmm_tn_fwd, _bmm_tn_bwd)


def _dn_rec(state, u, w, qd, kd, a, cd):
    v_new = u - _bmm(w, state)
    out = _bmm(qd, state) + _bmm(a, v_new)
    return state * cd + _bmm_tn(kd, v_new), out


def _mix_tile(o_attn, o_raw, zs, ga, gb, x, gate1, dn_g, wb_a, wb_d, w_out, p_ya, p_yd, p_out):
    o_dn = jnp.concatenate([_rms(o_raw[h], dn_g) * _silu(zs[h]) for h in range(DN_H)], axis=-1)
    y_a = _mm(o_attn, wb_a) + p_ya
    y_d = _mm(o_dn, wb_d) + p_yd
    merged = _sigmoid(ga) * y_a + _sigmoid(gb) * y_d
    out = _mm(merged, w_out) + p_out
    return x + gate1 * out, o_dn, merged


def _mlp_tile(x1, gain, shift, scale, gate2, w_gu, w_dn, tgt, p_gu, p_yy):
    h2 = _norm_mod(x1, gain, shift, scale)
    gu = jnp.concatenate([_mm(h2, w) for w in w_gu], axis=-1) + p_gu
    act = _silu(gu[:, :FFN]) * gu[:, FFN:]
    yy = _mm(act, w_dn) + p_yy
    y = x1 + gate2 * yy
    err = y - tgt
    return 0.5 * jnp.sum(err * err) * (1.0 / D), (h2, act)


def _tok(bt, f):
    return pl.BlockSpec((None, bt, f), lambda b, i: (b, i, 0))


def _full(shape):
    return pl.BlockSpec(shape, lambda b, i: (0,) * len(shape))


def _resident(shape):
    return pl.BlockSpec(shape, lambda b, i: (0,) * len(shape), pipeline_mode=pl.Buffered(1))


def _per_batch(f):
    return pl.BlockSpec((None, 1, f), lambda b, i: (b, 0, 0))


def _sds(shape, dtype):
    return jax.ShapeDtypeStruct(shape, dtype)


class _Exchange(NamedTuple):
    ins: tuple
    out_shapes: tuple
    n_remote: int
    plan: Callable
    n_forward: int = 0
    forward: Callable = None


def _remote_copies(remote, send_sems, recv_sems):
    return [pltpu.make_async_remote_copy(src_ref=src, dst_ref=dst, send_sem=send_sems.at[i], recv_sem=recv_sems.at[i],
                                         device_id=peer, device_id_type=pl.DeviceIdType.MESH)
            for i, (src, dst, peer) in enumerate(remote)]


def _exchange_copies(ex, in_refs, out_refs, send_sems, recv_sems):
    remote = ex.plan((lax.axis_index("x"), lax.axis_index("y"), lax.axis_index("c")), in_refs, out_refs)
    assert len(remote) == ex.n_remote
    return _remote_copies(remote, send_sems, recv_sems)


def _forward_copies(ex, out_refs, send_sems, recv_sems):
    remote = ex.forward((lax.axis_index("x"), lax.axis_index("y"), lax.axis_index("c")), out_refs)
    assert len(remote) == ex.n_forward
    return _remote_copies(remote, send_sems, recv_sems)


def _exchange_sems(ex):
    sems = [pltpu.SemaphoreType.DMA((ex.n_remote,)), pltpu.SemaphoreType.DMA((ex.n_remote,))]
    if ex.n_forward:
        sems += [pltpu.SemaphoreType.DMA((ex.n_forward,)), pltpu.SemaphoreType.DMA((ex.n_forward,))]
    return sems


def _hosted_call(body, name, grid, in_specs, out_specs, out_shape, scratch_shapes, semantics, ins, ex=None):
    if ex is None:
        outs = pl.pallas_call(body, name=name, grid=grid, in_specs=in_specs, out_specs=out_specs, out_shape=out_shape,
                              scratch_shapes=scratch_shapes,
                              compiler_params=_cparams(dimension_semantics=semantics))(*ins)
        return outs, ()
    n_in, n_out, n_scr = len(ins), len(out_shape), len(scratch_shapes)
    c_in, c_out = len(ex.ins), len(ex.out_shapes)
    steps = 1
    for g in grid:
        steps *= g

    def wrapped(*refs):
        a, b, c, d = n_in, n_in + c_in, n_in + c_in + n_out, n_in + c_in + n_out + c_out
        scratch, sems = refs[d:d + n_scr], refs[d + n_scr:]
        step = 0
        for axis, g in enumerate(grid):
            step = step * g + pl.program_id(axis)

        def first_phase():
            return _exchange_copies(ex, refs[a:b], refs[c:d], sems[0], sems[1])

        @pl.when(step == 0)
        def _():
            for cp in first_phase():
                cp.start()
        body(*refs[:a], *refs[b:c], *scratch)

        if ex.n_forward:
            @pl.when(step == (3 * steps) // 4)
            def _():
                for cp in first_phase():
                    cp.wait_recv()
                for cp in _forward_copies(ex, refs[c:d], sems[2], sems[3]):
                    cp.start()

        @pl.when(step == steps - 1)
        def _():
            cps = first_phase()
            if ex.n_forward:
                fwd = _forward_copies(ex, refs[c:d], sems[2], sems[3])
                for cp in fwd:
                    cp.wait_recv()
                for cp in cps + fwd:
                    cp.wait_send()
            else:
                for cp in cps:
                    cp.wait_recv()
                for cp in cps:
                    cp.wait_send()

    any_spec = pl.BlockSpec(memory_space=pl.ANY)
    res = pl.pallas_call(
        wrapped, name=name, grid=grid, in_specs=list(in_specs) + [any_spec] * c_in,
        out_specs=list(out_specs) + [any_spec] * c_out, out_shape=list(out_shape) + list(ex.out_shapes),
        scratch_shapes=list(scratch_shapes) + _exchange_sems(ex),
        compiler_params=_cparams(dimension_semantics=("arbitrary",) * len(grid)),
    )(*ins, *ex.ins)
    return res[:n_out], res[n_out:]


def _acc(ref, val, first):
    @pl.when(first)
    def _():
        ref[...] = val

    @pl.when(jnp.logical_not(first))
    def _():
        ref[...] += val


def _in_proj(x, mod, norm1_g, w_in, bt):
    B, S, _ = x.shape

    def body(x_ref, mod_ref, g_ref, w_ref, q_ref, kv_ref, dn_ref, z_ref, ga_ref, gb_ref, ba_ref, h_ref):
        h = _norm_mod(x_ref[...], g_ref[...], mod_ref[:, 0:D], mod_ref[:, D:2 * D]).astype(BF)
        h_ref[...] = h

        def proj(c0, c1):
            return jnp.dot(h, w_ref[:, c0:c1], preferred_element_type=F32)
        q_ref[...] = proj(0, C_KV).astype(BF)
        kv_ref[...] = proj(C_KV, C_DN).astype(BF)
        dn_ref[...] = proj(C_DN, C_Z).astype(BF)
        z_ref[...] = proj(C_Z, C_GA).astype(BF)
        ga_ref[...] = proj(C_GA, C_GB).astype(BF)
        gb_ref[...] = proj(C_GB, C_BA).astype(BF)
        ba_ref[...] = proj(C_BA, IN_PAD)

    widths = (QW, 2 * KVW, CONVW, DNW, D, D)
    return pl.pallas_call(
        body, name="in_proj", grid=(B, S // bt),
        in_specs=[_tok(bt, D), _per_batch(6 * D), _full((1, D)), _resident((D, IN_PAD))],
        out_specs=[_tok(bt, w) for w in widths] + [_tok(bt, 128), _tok(bt, D)],
        out_shape=[_sds((B, S, w), BF) for w in widths] + [_sds((B, S, 128), F32), _sds((B, S, D), BF)],
        compiler_params=_cparams(dimension_semantics=("parallel", "parallel")),
    )(x, mod, norm1_g, w_in)


def _prev_blk(bt, f):
    return pl.BlockSpec((None, bt, f), lambda b, i: (b, jnp.maximum(i - 1, 0), 0))


QKV = QW + 2 * KVW


def _qk_slabs(q_ref, kv_ref):
    return ([q_ref[:, j * 2 * HD:(j + 1) * 2 * HD].astype(F32) for j in range(QW // (2 * HD))],
            [kv_ref[:, 0:KVW].astype(F32)])


def _qk_prep_fwd(q, kv, cos, sin, qg, kg, bt):
    B, S, _ = q.shape

    def body(q_ref, kv_ref, cos_ref, sin_ref, qg_ref, kg_ref, o_ref):
        qs, ks = _qk_slabs(q_ref, kv_ref)
        qn = _qk_prep(qs, qg_ref[...], cos_ref[...], sin_ref[...])
        kn = _qk_prep(ks, kg_ref[...], cos_ref[...], sin_ref[...])
        for j, t in enumerate(qn + kn):
            o_ref[:, j * 2 * HD:(j + 1) * 2 * HD] = t.astype(BF)
        o_ref[:, QW + KVW:QKV] = kv_ref[:, KVW:2 * KVW]

    return pl.pallas_call(
        body, name="qk_prep_fwd", grid=(B, S // bt),
        in_specs=[_tok(bt, QW), _tok(bt, 2 * KVW), _tok(bt, 2 * HD), _tok(bt, 2 * HD), _full((1, 2 * HD)), _full((1, 2 * HD))],
        out_specs=_tok(bt, QKV), out_shape=_sds((B, S, QKV), BF),
        compiler_params=_cparams(dimension_semantics=("parallel", "parallel")),
    )(q, kv, cos, sin, qg, kg)


def _qk_prep_bwd(q, kv, cos, sin, qg, kg, dqn, dkvn, bt, ex=None):
    B, S, _ = q.shape

    def body(q_ref, kv_ref, cos_ref, sin_ref, qg_ref, kg_ref, dqn_ref, dkvn_ref, dq_ref, dkv_ref, dqg_ref, dkg_ref):
        qs, ks = _qk_slabs(q_ref, kv_ref)
        cos, sin = cos_ref[...], sin_ref[...]

        def f(qs, ks, qg, kg):
            return _qk_prep(qs, qg, cos, sin), _qk_prep(ks, kg, cos, sin)
        _, vjp = jax.vjp(f, qs, ks, qg_ref[...], kg_ref[...])
        n_q = len(qs)
        d_q = [dqn_ref[:, j * 2 * HD:(j + 1) * 2 * HD].astype(F32) for j in range(n_q)]
        d_k = [dkvn_ref[:, 0:KVW].astype(F32)]
        dqs, dks, dqg, dkg = vjp((d_q, d_k))
        for j in range(n_q):
            dq_ref[:, j * 2 * HD:(j + 1) * 2 * HD] = dqs[j].astype(BF)
        dkv_ref[:, 0:KVW] = dks[0].astype(BF)
        dkv_ref[:, KVW:2 * KVW] = dkvn_ref[:, KVW:2 * KVW]
        first = (pl.program_id(0) == 0) & (pl.program_id(1) == 0)
        _acc(dqg_ref, dqg, first)
        _acc(dkg_ref, dkg, first)

    return _hosted_call(
        body, "qk_prep_bwd", (B, S // bt),
        in_specs=[_tok(bt, QW), _tok(bt, 2 * KVW), _tok(bt, 2 * HD), _tok(bt, 2 * HD), _full((1, 2 * HD)), _full((1, 2 * HD)),
                  _tok(bt, QW), _tok(bt, 2 * KVW)],
        out_specs=[_tok(bt, QW), _tok(bt, 2 * KVW), _full((1, 2 * HD)), _full((1, 2 * HD))],
        out_shape=[_sds((B, S, QW), BF), _sds((B, S, 2 * KVW), BF), _sds((1, 2 * HD), F32), _sds((1, 2 * HD), F32)],
        scratch_shapes=[], semantics=("arbitrary", "arbitrary"), ins=(q, kv, cos, sin, qg, kg, dqn, dkvn), ex=ex)


def _attn_load(qkv_ref, kvp_ref):
    qs = [qkv_ref[:, h * HD:(h + 1) * HD].astype(F32) for h in range(HEADS)]
    kc = [qkv_ref[:, QW + h * HD:QW + (h + 1) * HD].astype(F32) for h in range(KV_HEADS)]
    vc = [qkv_ref[:, QW + KVW + h * HD:QW + KVW + (h + 1) * HD].astype(F32) for h in range(KV_HEADS)]
    kp = [kvp_ref[:, h * HD:(h + 1) * HD].astype(F32) for h in range(KV_HEADS)]
    vp = [kvp_ref[:, KVW + h * HD:KVW + (h + 1) * HD].astype(F32) for h in range(KV_HEADS)]
    return qs, kc, kp, vc, vp


def _kv_prev_spec(index):
    return pl.BlockSpec((None, BLK, 2 * KVW), lambda b, i: (b, index(i), QW // (2 * KVW)))


def _attn_fwd(qkv, sinks):
    B, S, _ = qkv.shape

    def body(qkv_ref, kvp_ref, sk_ref, o_ref):
        qs, kc, kp, vc, vp = _attn_load(qkv_ref, kvp_ref)
        outs = _attn_block(qs, kc, kp, vc, vp, sk_ref[...], pl.program_id(1) > 0)
        for h in range(HEADS):
            o_ref[:, h * HD:(h + 1) * HD] = outs[h].astype(BF)

    return pl.pallas_call(
        body, name="attn_fwd", grid=(B, S // BLK),
        in_specs=[_tok(BLK, QKV), _kv_prev_spec(lambda i: jnp.maximum(i - 1, 0)), _full((1, HEADS))],
        out_specs=_tok(BLK, QW), out_shape=_sds((B, S, QW), BF),
        compiler_params=_cparams(dimension_semantics=("parallel", "parallel")),
    )(qkv, qkv, sinks)


def _conv_fwd_tile(xe_ref, x_ref, halo_ref, cw_ref, first, bt):
    halo = halo_ref[...].astype(F32)
    xe_ref[0:8, :] = jnp.where(first, 0.0, halo)
    xe_ref[8:bt + 8, :] = x_ref[...].astype(F32)
    y = cw_ref[0:1, :] * xe_ref[5:bt + 5, :]
    for j in range(1, CONV):
        y = y + cw_ref[j:j + 1, :] * xe_ref[5 + j:bt + 5 + j, :]
    return y


def _halo_spec(bt):
    return pl.BlockSpec((None, 8, CONVW), lambda b, i: (b, jnp.maximum(i * (bt // 8) - 1, 0), 0))


def _dn_prep(dn, ba, conv_w, alog, dtb, bt):
    B, S, _ = dn.shape

    def body(x_ref, halo_ref, ba_ref, cw_ref, al_ref, dt_ref, qkv_ref, bg_ref, y_ref, xe_ref):
        y = _conv_fwd_tile(xe_ref, x_ref, halo_ref, cw_ref, pl.program_id(1) == 0, bt)
        y_ref[...] = y.astype(BF)
        ys = [y[:, j * DN_D:(j + 1) * DN_D] for j in range(3 * DN_H)]
        out, bg = _dn_tail(ys, ba_ref[...], al_ref[...], dt_ref[...])
        for j in range(3 * DN_H):
            qkv_ref[:, j * DN_D:(j + 1) * DN_D] = out[j]
        bg_ref[...] = bg

    return pl.pallas_call(
        body, name="dn_prep", grid=(B, S // bt),
        in_specs=[_tok(bt, CONVW), _halo_spec(bt), _tok(bt, 128), _full((CONV, CONVW)), _full((1, 128)), _full((1, 128))],
        out_specs=[_tok(bt, CONVW), _tok(bt, 128), _tok(bt, CONVW)],
        out_shape=[_sds((B, S, CONVW), F32), _sds((B, S, 128), F32), _sds((B, S, CONVW), BF)],
        scratch_shapes=[pltpu.VMEM((bt + 8, CONVW), F32)],
        compiler_params=_cparams(dimension_semantics=("parallel", "arbitrary")),
    )(dn, dn, ba, conv_w, alog, dtb)


def _dn_load(qkv_ref):
    qs = [qkv_ref[:, h * DN_D:(h + 1) * DN_D] for h in range(DN_H)]
    ks = [qkv_ref[:, DNW + h * DN_D:DNW + (h + 1) * DN_D] for h in range(DN_H)]
    vs = [qkv_ref[:, 2 * DNW + h * DN_D:2 * DNW + (h + 1) * DN_D] for h in range(DN_H)]
    return qs, ks, vs


DN_GROUP = 4
AW = DN_H * CHUNK


def _stack_heads(ref, G, offset, width):
    return jnp.stack([ref[g * CHUNK:(g + 1) * CHUNK, offset + h * width:offset + (h + 1) * width]
                      for g in range(G) for h in range(DN_H)])


def _dn_load_stack(qkv_ref, G):
    return tuple(_stack_heads(qkv_ref, G, j * DNW, DN_D) for j in range(3))


def _cd_spec(n):
    return pl.BlockSpec((None, n, 1, DN_D), lambda b, i: (b, i, 0, 0))


def _dn_intra_fwd(qkv, bg, ex=None):
    B, S, _ = qkv.shape
    nc = S // CHUNK
    G = min(DN_GROUP, nc)
    rows = G * CHUNK

    def body(qkv_ref, bg_ref, u_ref, w_ref, qd_ref, kd_ref, a_ref, cd_ref, t_ref):
        q, k, v = _dn_load_stack(qkv_ref, G)
        u, w, qd, kd, a, cd, tinv = _dn_intra(q, k, v, bg_ref[...].reshape(G, CHUNK, DN_D))
        lane_row = _iota((1, DN_D), 1)
        for g in range(G):
            rows = slice(g * CHUNK, (g + 1) * CHUNK)
            cd_row = jnp.zeros((1, DN_D), F32)
            for h in range(DN_H):
                n = g * DN_H + h
                cols = slice(h * DN_D, (h + 1) * DN_D)
                u_ref[rows, cols] = u[n]
                w_ref[rows, cols] = w[n].astype(BF)
                qd_ref[rows, cols] = qd[n].astype(BF)
                kd_ref[rows, cols] = kd[n].astype(BF)
                a_ref[rows, h * CHUNK:(h + 1) * CHUNK] = a[n].astype(BF)
                t_ref[rows, h * CHUNK:(h + 1) * CHUNK] = tinv[n]
                cd_row = cd_row + jnp.where(lane_row == h, cd[n], 0.0)
            cd_ref[g] = cd_row

    return _hosted_call(
        body, "dn_intra_fwd", (B, nc // G),
        in_specs=[_tok(rows, CONVW), _tok(rows, 128)],
        out_specs=[_tok(rows, DNW)] * 4 + [_tok(rows, AW), _cd_spec(G), _tok(rows, AW)],
        out_shape=[_sds((B, S, DNW), F32)] + [_sds((B, S, DNW), BF)] * 3 + [_sds((B, S, AW), BF), _sds((B, nc, 1, DN_D), F32),
                                                                            _sds((B, S, AW), F32)],
        scratch_shapes=[], semantics=("parallel", "parallel"), ins=(qkv, bg), ex=ex)


REC_GROUP = 2


def _rec_stack(ref, B, width, c):
    rows = slice(c * CHUNK, (c + 1) * CHUNK)
    return jnp.stack([ref[b, rows, h * width:(h + 1) * width].astype(F32) for b in range(B) for h in range(DN_H)])


def _rec_load(B, u_ref, w_ref, qd_ref, kd_ref, a_ref, cd_ref, c):
    lane_row = _iota((1, DN_D), 1)
    cd = jnp.stack([jnp.sum(jnp.where(lane_row == h, cd_ref[b, c], 0.0), axis=-1, keepdims=True)
                    for b in range(B) for h in range(DN_H)])
    return (_rec_stack(u_ref, B, DN_D, c), _rec_stack(w_ref, B, DN_D, c), _rec_stack(qd_ref, B, DN_D, c),
            _rec_stack(kd_ref, B, DN_D, c), _rec_stack(a_ref, B, CHUNK, c), cd)


def _rec_store(B, ref, val, width, c):
    for b in range(B):
        for h in range(DN_H):
            ref[b, c * CHUNK:(c + 1) * CHUNK, h * width:(h + 1) * width] = val[b * DN_H + h]


def _rec_specs(B, R, index):
    def tok(f):
        return pl.BlockSpec((B, R * CHUNK, f), lambda i: (0, index(i), 0))
    cd = pl.BlockSpec((B, R, 1, DN_D), lambda i: (0, index(i), 0, 0))
    st = pl.BlockSpec((B, R, DN_H, DN_D, DN_D), lambda i: (0, index(i), 0, 0, 0))
    return tok, cd, st


def _dn_rec_fwd(u, w, qd, kd, a, cd, ex=None):
    B, S, _ = u.shape
    nc = S // CHUNK
    R = REC_GROUP if nc % REC_GROUP == 0 else 1
    tok, cd_spec, st_spec = _rec_specs(B, R, lambda i: i)

    def body(u_ref, w_ref, qd_ref, kd_ref, a_ref, cd_ref, o_ref, st_ref, s_ref):
        @pl.when(pl.program_id(0) == 0)
        def _():
            s_ref[...] = jnp.zeros_like(s_ref)
        state = s_ref[...]
        for c in range(R):
            st_ref[:, c] = state.reshape(B, DN_H, DN_D, DN_D)
            state, out = _dn_rec(state, *_rec_load(B, u_ref, w_ref, qd_ref, kd_ref, a_ref, cd_ref, c))
            _rec_store(B, o_ref, out, DN_D, c)
        s_ref[...] = state

    return _hosted_call(
        body, "dn_rec_fwd", (nc // R,),
        in_specs=[tok(DNW)] * 4 + [tok(AW), cd_spec],
        out_specs=[tok(DNW), st_spec],
        out_shape=[_sds((B, S, DNW), F32), _sds((B, nc, DN_H, DN_D, DN_D), F32)],
        scratch_shapes=[pltpu.VMEM((B * DN_H, DN_D, DN_D), F32)],
        semantics=("arbitrary",), ins=(u, w, qd, kd, a, cd), ex=ex)


def _mix_load(oa_ref, or_ref, z_ref):
    o_raw = [or_ref[:, h * DN_D:(h + 1) * DN_D] for h in range(DN_H)]
    zs = [z_ref[:, h * DN_D:(h + 1) * DN_D].astype(F32) for h in range(DN_H)]
    return oa_ref[...].astype(F32), o_raw, zs


def _mix_fwd(o_attn, o_raw, z, ga, gb, x, mod, dn_g, w_branch, w_out, bt):
    B, S, _ = x.shape

    def body(oa_ref, or_ref, z_ref, ga_ref, gb_ref, x_ref, mod_ref, dg_ref, wb_ref, wo_ref, x1_ref, od_ref, mg_ref):
        oa, o_r, zs = _mix_load(oa_ref, or_ref, z_ref)
        x1, o_dn, merged = _mix_tile(oa, o_r, zs, ga_ref[...].astype(F32), gb_ref[...].astype(F32), x_ref[...],
                                     mod_ref[:, 2 * D:3 * D], dg_ref[...], wb_ref[0:QW, :], wb_ref[QW:2 * QW, :],
                                     wo_ref[...], 0.0, 0.0, 0.0)
        x1_ref[...] = x1
        od_ref[...] = o_dn.astype(BF)
        mg_ref[...] = merged.astype(BF)

    return pl.pallas_call(
        body, name="mix_fwd", grid=(B, S // bt),
        in_specs=[_tok(bt, QW), _tok(bt, DNW), _tok(bt, DNW), _tok(bt, D), _tok(bt, D), _tok(bt, D), _per_batch(6 * D),
                  _full((1, DN_D)), _resident((D, D)), _resident((D, D))],
        out_specs=[_tok(bt, D), _tok(bt, DNW), _tok(bt, D)],
        out_shape=[_sds((B, S, D), F32), _sds((B, S, DNW), BF), _sds((B, S, D), BF)],
        compiler_params=_cparams(dimension_semantics=("parallel", "parallel")),
    )(o_attn, o_raw, z, ga, gb, x, mod, dn_g, w_branch, w_out)


def _mlp(x1, tgt, mod, norm2_g, w_gu, w_dn, bt):
    B, S, _ = x1.shape

    def body(x1_ref, t_ref, mod_ref, g_ref, wgu_ref, wdn_ref,
             dx1_ref, h2_ref, act_ref, dgu_ref, dyy_ref, loss_ref, dmod_ref, dg_ref):
        w_gu_v, w_dn_v, t = [wgu_ref[k] for k in range(N_CHIP)], wdn_ref[...], t_ref[...]

        def f(x1, gain, shift, scale, gate2, p_gu, p_yy):
            return _mlp_tile(x1, gain, shift, scale, gate2, w_gu_v, w_dn_v, t, p_gu, p_yy)
        zero_gu = jnp.zeros((bt, 2 * FFN), F32)
        zero_yy = jnp.zeros((bt, D), F32)
        loss, vjp, (h2, act) = jax.vjp(f, x1_ref[...], g_ref[...], mod_ref[:, 3 * D:4 * D], mod_ref[:, 4 * D:5 * D],
                                       mod_ref[:, 5 * D:6 * D], zero_gu, zero_yy, has_aux=True)
        dx1, dgain, dshift, dscale, dgate2, dgu, dyy = vjp(jnp.ones((), F32))
        dx1_ref[...] = dx1
        h2_ref[...] = h2.astype(BF)
        act_ref[...] = act.astype(BF)
        dgu_ref[...] = dgu.astype(BF)
        dyy_ref[...] = dyy.astype(BF)
        first = pl.program_id(1) == 0
        _acc(loss_ref, jnp.reshape(loss, (1, 1)), first)
        _acc(dmod_ref, jnp.concatenate([dshift, dscale, dgate2], axis=-1), first)
        _acc(dg_ref, dgain, first)

    return pl.pallas_call(
        body, name="mlp", grid=(B, S // bt),
        in_specs=[_tok(bt, D), _tok(bt, D), _per_batch(6 * D), _full((1, D)), _resident((N_CHIP, D, 2 * FFN // N_CHIP)),
                  _resident((FFN, D))],
        out_specs=[_tok(bt, D), _tok(bt, D), _tok(bt, FFN), _tok(bt, 2 * FFN), _tok(bt, D),
                   _per_batch(1), _per_batch(3 * D), _per_batch(D)],
        out_shape=[_sds((B, S, D), F32), _sds((B, S, D), BF), _sds((B, S, FFN), BF), _sds((B, S, 2 * FFN), BF),
                   _sds((B, S, D), BF), _sds((B, 1, 1), F32), _sds((B, 1, 3 * D), F32), _sds((B, 1, D), F32)],
        compiler_params=_cparams(dimension_semantics=("parallel", "arbitrary")),
    )(x1, tgt, mod, norm2_g, w_gu, w_dn)


def _mix_bwd(o_attn, o_raw, z, ga, gb, x, mod, dn_g, w_branch, w_out, dx1, bt, ex=None):
    B, S, _ = x.shape

    def body(oa_ref, or_ref, z_ref, ga_ref, gb_ref, x_ref, mod_ref, dg_ref, wb_ref, wo_ref, dx1_ref,
             doa_ref, dor_ref, dz_ref, dga_ref, dgb_ref, dya_ref, dyd_ref, dout_ref, dgate_ref, ddg_ref):
        oa, o_r, zs = _mix_load(oa_ref, or_ref, z_ref)
        wb_a, wb_d, wo = wb_ref[0:QW, :], wb_ref[QW:2 * QW, :], wo_ref[...]

        def f(oa, o_r, zs, ga, gb, gate1, dn_g, p_ya, p_yd, p_out):
            return _mix_tile(oa, o_r, zs, ga, gb, x_ref[...], gate1, dn_g, wb_a, wb_d, wo, p_ya, p_yd, p_out)[0]
        zero = jnp.zeros((bt, D), F32)
        _, vjp = jax.vjp(f, oa, o_r, zs, ga_ref[...].astype(F32), gb_ref[...].astype(F32), mod_ref[:, 2 * D:3 * D],
                         dg_ref[...], zero, zero, zero)
        doa, dor, dzs, dga, dgb, dgate1, ddn_g, dya, dyd, dout = vjp(dx1_ref[...])
        doa_ref[...] = doa
        for h in range(DN_H):
            dor_ref[:, h * DN_D:(h + 1) * DN_D] = dor[h]
            dz_ref[:, h * DN_D:(h + 1) * DN_D] = dzs[h].astype(BF)
        dga_ref[...] = dga.astype(BF)
        dgb_ref[...] = dgb.astype(BF)
        dya_ref[...] = dya.astype(BF)
        dyd_ref[...] = dyd.astype(BF)
        dout_ref[...] = dout.astype(BF)
        first = pl.program_id(1) == 0
        _acc(dgate_ref, dgate1, first)
        _acc(ddg_ref, ddn_g, first)

    return _hosted_call(
        body, "mix_bwd", (B, S // bt),
        in_specs=[_tok(bt, QW), _tok(bt, DNW), _tok(bt, DNW), _tok(bt, D), _tok(bt, D), _tok(bt, D), _per_batch(6 * D),
                  _full((1, DN_D)), _resident((D, D)), _resident((D, D)), _tok(bt, D)],
        out_specs=[_tok(bt, QW), _tok(bt, DNW), _tok(bt, DNW), _tok(bt, D), _tok(bt, D), _tok(bt, D), _tok(bt, D), _tok(bt, D),
                   _per_batch(D), _per_batch(DN_D)],
        out_shape=[_sds((B, S, QW), F32), _sds((B, S, DNW), F32), _sds((B, S, DNW), BF), _sds((B, S, D), BF),
                   _sds((B, S, D), BF), _sds((B, S, D), BF), _sds((B, S, D), BF), _sds((B, S, D), BF),
                   _sds((B, 1, D), F32), _sds((B, 1, DN_D), F32)],
        scratch_shapes=[], semantics=("parallel", "arbitrary"),
        ins=(o_attn, o_raw, z, ga, gb, x, mod, dn_g, w_branch, w_out, dx1), ex=ex)


def _dn_rec_bwd(u, w, qd, kd, a, cd, states, d_o, ex=None):
    B, S, _ = u.shape
    nc = S // CHUNK
    R = REC_GROUP if nc % REC_GROUP == 0 else 1
    tok, cd_spec, st_spec = _rec_specs(B, R, lambda i: nc // R - 1 - i)

    def body(u_ref, w_ref, qd_ref, kd_ref, a_ref, cd_ref, st_ref, do_ref,
             du_ref, dw_ref, dqd_ref, dkd_ref, da_ref, dcd_ref, ds_ref):
        @pl.when(pl.program_id(0) == 0)
        def _():
            ds_ref[...] = jnp.zeros_like(ds_ref)
        lane_row = _iota((1, DN_D), 1)
        d_state = ds_ref[...]
        for c in reversed(range(R)):
            state = st_ref[:, c].reshape(B * DN_H, DN_D, DN_D)
            _, vjp = jax.vjp(_dn_rec, state, *_rec_load(B, u_ref, w_ref, qd_ref, kd_ref, a_ref, cd_ref, c))
            d_state, du, dw, dqd, dkd, da, dcd = vjp((d_state, _rec_stack(do_ref, B, DN_D, c)))
            for ref, val, width in ((du_ref, du, DN_D), (dw_ref, dw, DN_D), (dqd_ref, dqd, DN_D), (dkd_ref, dkd, DN_D),
                                    (da_ref, da, CHUNK)):
                _rec_store(B, ref, val, width, c)
            for b in range(B):
                row = jnp.zeros((1, DN_D), F32)
                for h in range(DN_H):
                    row = row + jnp.where(lane_row == h, dcd[b * DN_H + h], 0.0)
                dcd_ref[b, c] = row
        ds_ref[...] = d_state

    return _hosted_call(
        body, "dn_rec_bwd", (nc // R,),
        in_specs=[tok(DNW)] * 4 + [tok(AW), cd_spec, st_spec, tok(DNW)],
        out_specs=[tok(DNW)] * 4 + [tok(AW), cd_spec],
        out_shape=[_sds((B, S, DNW), F32)] * 4 + [_sds((B, S, AW), F32), _sds((B, nc, 1, DN_D), F32)],
        scratch_shapes=[pltpu.VMEM((B * DN_H, DN_D, DN_D), F32)],
        semantics=("arbitrary",), ins=(u, w, qd, kd, a, cd, states, d_o), ex=ex)


def _dn_intra_bwd(qkv, bg, tinv, du, dw, dqd, dkd, da, dcd, ex=None):
    B, S, _ = qkv.shape
    nc = S // CHUNK
    G = min(DN_GROUP, nc)
    rows = G * CHUNK

    def body(qkv_ref, bg_ref, t_ref, du_ref, dw_ref, dqd_ref, dkd_ref, da_ref, dcd_ref, dqkv_ref, dbg_ref):
        q, k, v = _dn_load_stack(qkv_ref, G)
        known = _stack_heads(t_ref, G, 0, CHUNK)
        _, vjp = jax.vjp(lambda q, k, v, bg: _dn_intra(q, k, v, bg, known)[:6], q, k, v,
                         bg_ref[...].reshape(G, CHUNK, DN_D))
        lane_row = _iota((1, DN_D), 1)
        dcd = jnp.stack([jnp.sum(jnp.where(lane_row == h, dcd_ref[g], 0.0), axis=-1, keepdims=True)
                         for g in range(G) for h in range(DN_H)])
        dq, dk, dv, dbg = vjp((_stack_heads(du_ref, G, 0, DN_D), _stack_heads(dw_ref, G, 0, DN_D),
                               _stack_heads(dqd_ref, G, 0, DN_D), _stack_heads(dkd_ref, G, 0, DN_D),
                               _stack_heads(da_ref, G, 0, CHUNK), dcd))
        for g in range(G):
            rows = slice(g * CHUNK, (g + 1) * CHUNK)
            for h in range(DN_H):
                n = g * DN_H + h
                dqkv_ref[rows, h * DN_D:(h + 1) * DN_D] = dq[n]
                dqkv_ref[rows, DNW + h * DN_D:DNW + (h + 1) * DN_D] = dk[n]
                dqkv_ref[rows, 2 * DNW + h * DN_D:2 * DNW + (h + 1) * DN_D] = dv[n]
        dbg_ref[...] = dbg.reshape(G * CHUNK, DN_D)

    return _hosted_call(
        body, "dn_intra_bwd", (B, nc // G),
        in_specs=[_tok(rows, CONVW), _tok(rows, 128), _tok(rows, AW)] + [_tok(rows, DNW)] * 4 + [_tok(rows, AW), _cd_spec(G)],
        out_specs=[_tok(rows, CONVW), _tok(rows, 128)],
        out_shape=[_sds((B, S, CONVW), F32), _sds((B, S, 128), F32)],
        scratch_shapes=[], semantics=("parallel", "parallel"), ins=(qkv, bg, tinv, du, dw, dqd, dkd, da, dcd), ex=ex)


def _dn_prep_bwd(dn, y, ba, conv_w, alog, dtb, dqkv, dbg, bt, ex=None):
    B, S, _ = dn.shape
    nt = S // bt

    def rev(f):
        return pl.BlockSpec((None, bt, f), lambda b, i: (b, nt - 1 - i, 0))

    def body(x_ref, y_ref, ba_ref, cw_ref, al_ref, dt_ref, dqkv_ref, dbg_ref,
             dx_ref, dba_ref, dcw_ref, dal_ref, ddt_ref, dye_ref):
        i = pl.program_id(1)
        ys = [y_ref[:, j * DN_D:(j + 1) * DN_D].astype(F32) for j in range(3 * DN_H)]
        _, vjp = jax.vjp(_dn_tail, ys, ba_ref[...], al_ref[...], dt_ref[...])
        d_out = [dqkv_ref[:, j * DN_D:(j + 1) * DN_D] for j in range(3 * DN_H)]
        dys, dba, dal, ddt = vjp((d_out, dbg_ref[...]))
        @pl.when(i == 0)
        def _():
            dye_ref[bt:bt + 8, :] = jnp.zeros((8, CONVW), F32)

        @pl.when(i > 0)
        def _():
            dye_ref[bt:bt + 8, :] = dye_ref[0:8, :]
        for j in range(3 * DN_H):
            dye_ref[0:bt, j * DN_D:(j + 1) * DN_D] = dys[j]
        shifted = [dye_ref[3 - j:bt + 3 - j, :] for j in range(CONV)]
        dx = cw_ref[0:1, :] * shifted[0]
        for j in range(1, CONV):
            dx = dx + cw_ref[j:j + 1, :] * shifted[j]
        dx_ref[...] = dx.astype(BF)
        x = x_ref[...].astype(F32)
        dcw = jnp.concatenate([jnp.sum(shifted[j] * x, axis=0, keepdims=True) for j in range(CONV)], axis=0)
        first = (i == 0) & (pl.program_id(0) == 0)
        dba_ref[...] = dba
        _acc(dcw_ref, dcw, first)
        _acc(dal_ref, dal, first)
        _acc(ddt_ref, ddt, first)

    return _hosted_call(
        body, "dn_prep_bwd", (B, nt),
        in_specs=[rev(CONVW), rev(CONVW), rev(128), _full((CONV, CONVW)), _full((1, 128)), _full((1, 128)), rev(CONVW), rev(128)],
        out_specs=[rev(CONVW), rev(128), _full((CONV, CONVW)), _full((1, 128)), _full((1, 128))],
        out_shape=[_sds((B, S, CONVW), BF), _sds((B, S, 128), F32), _sds((CONV, CONVW), F32), _sds((1, 128), F32),
                   _sds((1, 128), F32)],
        scratch_shapes=[pltpu.VMEM((bt + 8, CONVW), F32)],
        semantics=("arbitrary", "arbitrary"), ins=(dn, y, ba, conv_w, alog, dtb, dqkv, dbg), ex=ex)


def _attn_bwd(qkv, sinks, d_o, ex=None):
    B, S, _ = qkv.shape
    nb = S // BLK

    def cur(f):
        return pl.BlockSpec((None, BLK, f), lambda b, i: (b, jnp.minimum(i, nb - 1), 0))

    def out_prev(f):
        return pl.BlockSpec((None, BLK, f), lambda b, i: (b, jnp.maximum(i - 1, 0), 0))

    def body(qkv_ref, kvp_ref, sk_ref, do_ref, dq_ref, dkv_ref, dsk_ref, carry_ref):
        n = pl.program_id(1)
        first = (n == 0) & (pl.program_id(0) == 0)

        @pl.when(n == 0)
        def _():
            carry_ref[...] = jnp.zeros_like(carry_ref)

        @pl.when(n < nb)
        def _():
            qs, kc, kp, vc, vp = _attn_load(qkv_ref, kvp_ref)

            def f(qs, kc, kp, vc, vp, sk):
                return _attn_block(qs, kc, kp, vc, vp, sk, n > 0)
            _, vjp = jax.vjp(f, qs, kc, kp, vc, vp, sk_ref[...])
            d_outs = [do_ref[:, h * HD:(h + 1) * HD] for h in range(HEADS)]
            dqs, dkc, dkp, dvc, dvp, dsk = vjp(d_outs)
            for h in range(HEADS):
                dq_ref[:, h * HD:(h + 1) * HD] = dqs[h].astype(BF)
            for h in range(KV_HEADS):
                ksl = slice(h * HD, (h + 1) * HD)
                vsl = slice(KVW + h * HD, KVW + (h + 1) * HD)
                dkv_ref[:, ksl] = (carry_ref[:, ksl] + dkp[h]).astype(BF)
                dkv_ref[:, vsl] = (carry_ref[:, vsl] + dvp[h]).astype(BF)
                carry_ref[:, ksl] = dkc[h]
                carry_ref[:, vsl] = dvc[h]
            _acc(dsk_ref, dsk, first)

        @pl.when(n == nb)
        def _():
            dkv_ref[...] = carry_ref[...].astype(BF)

    return _hosted_call(
        body, "attn_bwd", (B, nb + 1),
        in_specs=[cur(QKV), _kv_prev_spec(lambda i: jnp.maximum(jnp.minimum(i, nb - 1) - 1, 0)), _full((1, HEADS)), cur(QW)],
        out_specs=[cur(QW), out_prev(2 * KVW), _full((1, HEADS))],
        out_shape=[_sds((B, S, QW), BF), _sds((B, S, 2 * KVW), BF), _sds((1, HEADS), F32)],
        scratch_shapes=[pltpu.VMEM((BLK, 2 * KVW), F32)],
        semantics=("arbitrary", "arbitrary"), ins=(qkv, qkv, sinks, d_o), ex=ex)


def _in_proj_bwd(x, mod, norm1_g, w_in, pieces, dba, dx1, bt):
    B, S, _ = x.shape
    widths = (QW, 2 * KVW, CONVW, DNW, D, D)

    def body(x_ref, mod_ref, g_ref, w_ref, dq_ref, dkv_ref, ddn_ref, dz_ref, dga_ref, dgb_ref, dba_ref, dx1_ref,
             gx_ref, dp_ref, dmod_ref, dg_ref):
        dp = jnp.concatenate([r[...] for r in (dq_ref, dkv_ref, ddn_ref, dz_ref, dga_ref, dgb_ref)]
                             + [dba_ref[...].astype(BF)], axis=-1)
        dp_ref[...] = dp
        dh = lax.dot_general(dp, w_ref[...], (((1,), (1,)), ((), ())), preferred_element_type=F32)
        _, vjp = jax.vjp(_norm_mod, x_ref[...], g_ref[...], mod_ref[:, 0:D], mod_ref[:, D:2 * D])
        dx, dgain, dshift, dscale = vjp(dh)
        gx_ref[...] = dx + dx1_ref[...]
        first = pl.program_id(1) == 0
        _acc(dmod_ref, jnp.concatenate([dshift, dscale], axis=-1), first)
        _acc(dg_ref, dgain, first)

    return pl.pallas_call(
        body, name="in_proj_bwd", grid=(B, S // bt),
        in_specs=[_tok(bt, D), _per_batch(6 * D), _full((1, D)), _resident((D, IN_PAD))] + [_tok(bt, w) for w in widths]
        + [_tok(bt, 128), _tok(bt, D)],
        out_specs=[_tok(bt, D), _tok(bt, IN_PAD), _per_batch(2 * D), _per_batch(D)],
        out_shape=[_sds((B, S, D), F32), _sds((B, S, IN_PAD), BF), _sds((B, 1, 2 * D), F32), _sds((B, 1, D), F32)],
        compiler_params=_cparams(dimension_semantics=("parallel", "arbitrary")),
    )(x, mod, norm1_g, w_in, *pieces, dba, dx1)


def _matmul_tn(tag, a, b, bk, bn, bt, col_blocks=False, ex=None):
    T, K = a.shape
    N = b.shape[1]
    nt = T // bt
    if col_blocks:
        assert bk == K
        out_spec = pl.BlockSpec((None, bk, bn), lambda i, j, t: (j, 0, 0))
        out_shape = _sds((N // bn, K, bn), F32)
    else:
        out_spec = pl.BlockSpec((bk, bn), lambda i, j, t: (i, j))
        out_shape = _sds((K, N), F32)

    def body(a_ref, b_ref, o_ref, acc_ref):
        t = pl.program_id(2)

        @pl.when(t == 0)
        def _():
            acc_ref[...] = jnp.zeros_like(acc_ref)
        acc_ref[...] += lax.dot_general(a_ref[...], b_ref[...], (((0,), (0,)), ((), ())), preferred_element_type=F32)

        @pl.when(t == nt - 1)
        def _():
            o_ref[...] = acc_ref[...]

    (out,), landed = _hosted_call(
        body, f"grad_{tag}", (K // bk, N // bn, nt),
        in_specs=[pl.BlockSpec((bt, bk), lambda i, j, t: (t, i)), pl.BlockSpec((bt, bn), lambda i, j, t: (t, j))],
        out_specs=[out_spec], out_shape=[out_shape],
        scratch_shapes=[pltpu.VMEM((bk, bn), F32)],
        semantics=("parallel", "parallel", "arbitrary"), ins=(a, b), ex=ex)
    return out if ex is None else (out, landed)


def _rope_table(positions):
    inv_freq = THETA ** (-jnp.arange(0, ROT, 2, dtype=F32) / ROT)
    rest = jnp.zeros((HD - ROT,), F32)
    freq = jnp.concatenate([inv_freq, inv_freq, rest] * 2)
    sign = jnp.concatenate([-jnp.ones_like(inv_freq), jnp.ones_like(inv_freq), rest] * 2)
    ang = positions.astype(F32)[..., None] * freq
    return jnp.cos(ang), jnp.sin(ang) * sign


def _lane_pad(v, offset, width=128):
    return jnp.zeros((1, width), F32).at[0, offset:offset + v.shape[-1]].set(v.reshape(-1))


def _tile(S, want):
    return min(S, want)


class _Hosted:
    def __init__(self, call):
        self.call = call
        self.outs = None

    def __call__(self, ex):
        self.outs, landed = self.call(ex)
        return landed


def _local_step(x, mod, positions, tgt, norm1_g, w_in_pad, conv_w, q_norm_g, k_norm_g, sinks, a_log, dt_bias,
                dn_norm_g, w_branch, w_out, norm2_g, w_gu, w_dn, dist=None):
    B, S, _ = x.shape
    T = B * S
    cos_t, sin_t = _rope_table(positions)
    qg2 = jnp.concatenate([q_norm_g, q_norm_g], axis=-1)
    kg2 = jnp.concatenate([k_norm_g, k_norm_g], axis=-1)
    alog = _lane_pad(a_log, DN_H)
    dtb = _lane_pad(dt_bias, DN_H)
    conv2 = conv_w.reshape(CONV, CONVW)
    bt = _tile(S, 512)
    bt_mlp = _tile(S, 256)

    q, kv, dn, z, ga, gb, ba, h1 = _in_proj(x, mod, norm1_g, w_in_pad, bt)
    qkv_n = _qk_prep_fwd(q, kv, cos_t, sin_t, qg2, kg2, bt)
    o_attn = _attn_fwd(qkv_n, sinks)
    dqkv, bg, dn_y = _dn_prep(dn, ba, conv2, alog, dtb, bt)
    intra = _Hosted(lambda ex: _dn_intra_fwd(dqkv, bg, ex))
    if dist is None:
        intra(None)
    else:
        f_br, f_out, w_gu, f_dn = _gather_weights("late", [w_branch, w_out, w_gu, w_dn], dist[0], host=intra)
        w_branch, w_out, w_dn = (f.reshape(N_CHIP * f.shape[1], f.shape[2]) for f in (f_br, f_out, f_dn))
    dn_u, dn_w, dn_qd, dn_kd, dn_a, dn_cd, dn_tinv = intra.outs
    (o_raw, states), _ = _dn_rec_fwd(dn_u, dn_w, dn_qd, dn_kd, dn_a, dn_cd)
    x1, o_dn, merged = _mix_fwd(o_attn, o_raw, z, ga, gb, x, mod, dn_norm_g, w_branch, w_out, bt)
    dx1, h2, act, dgu, dyy, loss, dmod2, dnorm2 = _mlp(x1, tgt, mod, norm2_g, w_gu, w_dn, bt_mlp)

    def flat(t):
        return t.reshape(T, t.shape[-1])
    tn = functools.partial(_matmul_tn, bt=_tile(T, 1024))
    g_w_dn = tn("w_down", flat(act), flat(dyy), bk=FFN, bn=D // 2)
    g_w_gu = tn("w_gate_up", flat(h2), flat(dgu), bk=D, bn=2 * FFN // N_CHIP, col_blocks=True)

    mix_b = _Hosted(lambda ex: _mix_bwd(o_attn, o_raw, z, ga, gb, x, mod, dn_norm_g, w_branch, w_out, dx1, bt_mlp, ex))
    rec_b = _Hosted(lambda ex: _dn_rec_bwd(dn_u, dn_w, dn_qd, dn_kd, dn_a, dn_cd, states, mix_b.outs[1], ex))
    intra_b = _Hosted(lambda ex: _dn_intra_bwd(dqkv, bg, dn_tinv, *rec_b.outs, ex))
    if dist is None:
        for host in (mix_b, rec_b, intra_b):
            host(None)
    else:
        g_w_gu, g_w_dn = _reduce_grads(("w_gate_up", "w_down"), [g_w_gu, g_w_dn.reshape(N_CHIP, -1, D)], *dist,
                                       hosts=[mix_b, rec_b, intra_b])
    d_oa, _, dz, dga, dgb, dya, dyd, dout, dgate1, ddn_g = mix_b.outs
    d_dqkv, dbg = intra_b.outs
    g_w_out = tn("w_out", flat(merged), flat(dout), bk=D, bn=D)
    g_w_br = jnp.concatenate([tn("w_branch_attn", flat(o_attn), flat(dya), bk=QW, bn=D),
                              tn("w_branch_dn", flat(o_dn), flat(dyd), bk=DNW, bn=D)], axis=0)
    prep_b = _Hosted(lambda ex: _dn_prep_bwd(dn, dn_y, ba, conv2, alog, dtb, d_dqkv, dbg, bt, ex))
    attn_b = _Hosted(lambda ex: _attn_bwd(qkv_n, sinks, d_oa, ex))
    qk_b = _Hosted(lambda ex: _qk_prep_bwd(q, kv, cos_t, sin_t, qg2, kg2, *attn_b.outs[:2], bt, ex))
    if dist is None:
        for host in (prep_b, attn_b, qk_b):
            host(None)
    else:
        g_w_br, g_w_out = _reduce_grads(("w_branch", "w_out"), [g_w_br.reshape(N_CHIP, -1, D), g_w_out.reshape(N_CHIP, -1, D)],
                                        *dist, hosts=[prep_b, attn_b, qk_b])
    d_dn, dba, dconv, dalog, ddtb = prep_b.outs
    dsk = attn_b.outs[2]
    dq, dkv, dqg2, dkg2 = qk_b.outs
    dqg = dqg2[:, :HD] + dqg2[:, HD:]
    dkg = dkg2[:, :HD] + dkg2[:, HD:]
    grad_x, dproj, dmod1, dnorm1 = _in_proj_bwd(x, mod, norm1_g, w_in_pad, (dq, dkv, d_dn, dz, dga, dgb), dba, dx1, bt)
    g_w_in = _Hosted(lambda ex: (tn("w_in", flat(h1), flat(dproj), bk=D, bn=IN_PAD // 3), ()) if ex is None
                     else tn("w_in", flat(h1), flat(dproj), bk=D, bn=IN_PAD // 3, ex=ex))
    if dist is None:
        g_w_in(None)
        g_w_in = g_w_in.outs

    dmod = jnp.concatenate([dmod1, dgate1, dmod2], axis=-1)
    small = dict(norm1_g=jnp.sum(dnorm1, axis=0), norm2_g=jnp.sum(dnorm2, axis=0), q_norm_g=dqg, k_norm_g=dkg,
                 sinks=dsk, a_log=dalog[:, DN_H:2 * DN_H], dt_bias=ddtb[:, DN_H:2 * DN_H],
                 dn_norm_g=jnp.sum(ddn_g, axis=0), conv_w=dconv)
    return jnp.sum(loss), grad_x, dmod, small, (g_w_in, g_w_br, g_w_out, g_w_gu, g_w_dn)


def _flip(me, f):
    return (me[0] ^ ((f >> 2) & 1), me[1] ^ ((f >> 1) & 1), me[2] ^ (f & 1))


def _comm_call(name, ex):
    n_in, n_out = len(ex.ins), len(ex.out_shapes)

    def body(*refs):
        out_refs, sems = refs[n_in:n_in + n_out], refs[n_in + n_out:]
        cps = _exchange_copies(ex, refs[:n_in], out_refs, sems[0], sems[1])
        for cp in cps:
            cp.start()
        for cp in cps:
            cp.wait_recv()
        if ex.n_forward:
            fwd = _forward_copies(ex, out_refs, sems[2], sems[3])
            for cp in fwd:
                cp.start()
            for cp in fwd:
                cp.wait_recv()
            cps = cps + fwd
        for cp in cps:
            cp.wait_send()

    any_spec = pl.BlockSpec(memory_space=pl.ANY)
    return pl.pallas_call(
        body, name=name, in_specs=[any_spec] * n_in, out_specs=[any_spec] * n_out, out_shape=list(ex.out_shapes),
        scratch_shapes=_exchange_sems(ex),
    )(*ex.ins)


def _by_origin(own, received, index):
    stack = jnp.concatenate([own[None], received], axis=0)
    n = stack.shape[0]
    return jnp.stack([lax.dynamic_index_in_dim(stack, k ^ index, 0, keepdims=False) for k in range(n)])


def _gather_devices(name, arrs, dev, host=None):
    def plan(me, in_refs, out_refs):
        return [(a, o.at[f - 1], _flip(me, f)) for a, o in zip(in_refs, out_refs) for f in range(1, N_DEV)]
    outs = tuple(_sds((N_DEV - 1,) + a.shape, a.dtype) for a in arrs)
    got = (host or functools.partial(_comm_call, name))(_Exchange(tuple(arrs), outs, (N_DEV - 1) * len(arrs), plan))
    return [_by_origin(a, g, dev) for a, g in zip(arrs, got)]


def _gather_chips(name, arrs, chip):
    def plan(me, in_refs, out_refs):
        return [(a, o.at[j], _flip(me, 2 * (j + 1))) for a, o in zip(in_refs, out_refs) for j in range(N_CHIP - 1)]
    outs = tuple(_sds((N_CHIP - 1,) + a.shape, a.dtype) for a in arrs)
    got = _comm_call(name, _Exchange(tuple(arrs), outs, (N_CHIP - 1) * len(arrs), plan))
    return [_by_origin(a, g, chip) for a, g in zip(arrs, got)]


def _halves(core, mine, other):
    lo = jnp.where(core == 0, mine, other)
    hi = jnp.where(core == 0, other, mine)
    return jnp.concatenate([lo, hi], axis=-2)


def _swap_cores_ex(arrs):
    def plan(me, in_refs, out_refs):
        return [(g, o, _flip(me, 1)) for g, o in zip(in_refs, out_refs)]
    return _Exchange(tuple(arrs), tuple(_sds(g.shape, g.dtype) for g in arrs), len(arrs), plan)


def _gather_weights(tag, shards, chip, host=None):
    def plan(me, in_refs, out_refs):
        chip_me = 2 * me[0] + me[1]
        remote = []
        for a, o in zip(in_refs, out_refs):
            half = a.shape[0] // 2
            mine = a.at[pl.ds(me[2] * half, half)]
            remote += [(mine, o.at[chip_me, me[2]], _flip(me, 2 * (j + 1))) for j in range(N_CHIP - 1)]
        return remote

    def forward(me, out_refs):
        chip_me = 2 * me[0] + me[1]
        return [(o.at[chip_me ^ (j + 1), me[2]], o.at[chip_me ^ (j + 1), me[2]], _flip(me, 1))
                for o in out_refs for j in range(N_CHIP - 1)]
    run = host or functools.partial(_comm_call, f"weights_{tag}")
    n = (N_CHIP - 1) * len(shards)
    landed = run(_Exchange(tuple(shards), tuple(_sds((N_CHIP, 2, a.shape[0] // 2, a.shape[1]), a.dtype) for a in shards),
                           n, plan, n, forward))
    return [lax.dynamic_update_slice(f.reshape((N_CHIP,) + a.shape), a[None], (chip, 0, 0)) for a, f in zip(shards, landed)]


def _rows(r):
    for br in (512, 352, 256, 128, 64, 32, 16, 8):
        if r % br == 0:
            return br
    raise ValueError(r)


def _pair_add(tag, g, recv, c):
    n, r, cols = g.shape
    half = r // 2
    br = _rows(half)
    nb = half // br

    def body(c_ref, g_ref, r_ref, o_ref):
        o_ref[...] = (g_ref[...] + r_ref[...]).astype(BF)

    return pl.pallas_call(
        body, name=f"pair_add_{tag}",
        grid_spec=pltpu.PrefetchScalarGridSpec(
            num_scalar_prefetch=1, grid=(n, nb),
            in_specs=[pl.BlockSpec((None, br, cols), lambda k, i, c_ref: (k, c_ref[0] * nb + i, 0)),
                      pl.BlockSpec((None, br, cols), lambda k, i, c_ref: (k, i, 0))],
            out_specs=pl.BlockSpec((None, br, cols), lambda k, i, c_ref: (k, i, 0))),
        out_shape=_sds((n, half, cols), BF),
        compiler_params=_cparams(dimension_semantics=("parallel", "parallel")),
    )(c, g, recv)


def _sum_chips(tag, p, q, chip):
    n, r, cols = q.shape
    br = _rows(r)

    def body(chip_ref, p_ref, q_ref, o_ref):
        acc = p_ref[...].astype(F32)
        for k in range(n):
            acc = acc + q_ref[k].astype(F32)
        o_ref[...] = acc

    return pl.pallas_call(
        body, name=f"sum_chips_{tag}",
        grid_spec=pltpu.PrefetchScalarGridSpec(
            num_scalar_prefetch=1, grid=(r // br,),
            in_specs=[pl.BlockSpec((None, br, cols), lambda i, chip_ref: (chip_ref[0], i, 0)),
                      pl.BlockSpec((n, br, cols), lambda i, chip_ref: (0, i, 0))],
            out_specs=pl.BlockSpec((br, cols), lambda i, chip_ref: (i, 0))),
        out_shape=_sds((r, cols), F32),
        compiler_params=_cparams(dimension_semantics=("parallel",)),
    )(chip, p, q)


def _reduce_grads(tags, grads, chip, core, hosts=None):
    core_arr = core.reshape(1).astype(jnp.int32)
    chip_arr = chip.reshape(1).astype(jnp.int32)
    name = "_".join(tags)
    run = hosts or [functools.partial(_comm_call, f"grads_{stage}_{name}") for stage in ("pair", "chips", "swap")]

    def plan_pair(me, in_refs, out_refs):
        remote = []
        for g, o in zip(in_refs, out_refs):
            half = g.shape[1] // 2
            remote += [(g.at[k, pl.ds((1 - me[2]) * half, half)], o.at[k], _flip(me, 1)) for k in range(N_CHIP)]
        return remote
    recv = run[0](_Exchange(tuple(grads), tuple(_sds((N_CHIP, g.shape[1] // 2, g.shape[2]), F32) for g in grads),
                            N_CHIP * len(grads), plan_pair))
    pair = [_pair_add(t, g, r, core_arr) for t, g, r in zip(tags, grads, recv)]

    def plan_chips(me, in_refs, out_refs):
        remote = []
        for p, o in zip(in_refs, out_refs):
            for j in range(N_CHIP - 1):
                peer = _flip(me, 2 * (j + 1))
                remote.append((p.at[2 * peer[0] + peer[1]], o.at[j], peer))
        return remote
    parts = run[1](_Exchange(tuple(pair), tuple(_sds((N_CHIP - 1,) + p.shape[1:], BF) for p in pair),
                             (N_CHIP - 1) * len(pair), plan_chips))
    mine = [_sum_chips(t, p, q, chip_arr) for t, p, q in zip(tags, pair, parts)]
    other = run[2](_swap_cores_ex(mine))
    return [_halves(core, h, o) for h, o in zip(mine, other)]


def _adamw_math(w, g, m, v):
    m = ADAM_B1 * m + (1.0 - ADAM_B1) * g
    v = ADAM_B2 * v + (1.0 - ADAM_B2) * (g * g)
    m_hat = m / (1.0 - ADAM_B1 ** ADAM_STEP)
    v_hat = v / (1.0 - ADAM_B2 ** ADAM_STEP)
    delta = -ADAM_LR * (m_hat / (jnp.sqrt(v_hat) + ADAM_EPS) + ADAM_WD * w)
    return delta, m, v


def _adamw(name, w, g, m, v, ex=None):
    r, cols = w.shape
    br = _rows(r)
    if br * cols * 4 > (1 << 20) and br % 16 == 0:
        br //= 2

    def body(w_ref, g_ref, m_ref, v_ref, d_ref, mo_ref, vo_ref):
        d_ref[...], mo_ref[...], vo_ref[...] = _adamw_math(w_ref[...], g_ref[...], m_ref[...], v_ref[...])

    spec = pl.BlockSpec((br, cols), lambda i: (i, 0))
    outs, landed = _hosted_call(
        body, f"adamw_{name}", (r // br,), in_specs=[spec] * 4, out_specs=[spec] * 3,
        out_shape=[_sds((r, cols), F32)] * 3, scratch_shapes=[], semantics=("parallel",), ins=(w, g, m, v), ex=ex)
    return outs if ex is None else (outs, landed)


def _ada_fwd(c_all, ada_w, ada_b_cols):
    n = c_all.shape[0]

    def body(c_ref, w_ref, b_ref, o_ref):
        o_ref[...] = _mmx(_silu(c_ref[...]), w_ref[...]) + b_ref[...]

    return pl.pallas_call(
        body, name="ada_fwd", out_shape=_sds((n, ada_w.shape[1]), F32), compiler_params=_cparams(),
    )(c_all, ada_w, ada_b_cols)


def _ada_bwd(c_all, dmod_cols, w, m, v, ex=None):
    n = c_all.shape[0]
    r, cols = w.shape
    br = 128

    def body(c_ref, d_ref, w_ref, m_ref, v_ref, g_ref, dl_ref, mo_ref, vo_ref):
        cond = _silu(c_ref[...])
        g = lax.dot_general(cond, d_ref[...], (((0,), (0,)), ((), ())), precision=lax.Precision.HIGHEST,
                            preferred_element_type=F32)
        g_ref[...] = g
        dl_ref[...], mo_ref[...], vo_ref[...] = _adamw_math(w_ref[...], g, m_ref[...], v_ref[...])

    spec = pl.BlockSpec((br, cols), lambda i: (i, 0))
    outs, landed = _hosted_call(
        body, "ada_bwd", (r // br,),
        in_specs=[pl.BlockSpec((n, br), lambda i: (0, i)), pl.BlockSpec((n, cols), lambda i: (0, 0)), spec, spec, spec],
        out_specs=[spec] * 4, out_shape=[_sds((r, cols), F32)] * 4, scratch_shapes=[], semantics=("parallel",),
        ins=(c_all, dmod_cols, w, m, v), ex=ex)
    return outs if ex is None else (outs, landed)


def _sum_devices(parts):
    n, r, cols = parts.shape

    def body(p_ref, o_ref):
        acc = p_ref[0]
        for k in range(1, n):
            acc = acc + p_ref[k]
        o_ref[...] = acc

    return pl.pallas_call(body, name="sum_devices", out_shape=_sds((r, cols), F32), compiler_params=_cparams())(parts)


SMALL_ROWS = 16
_SMALL_SLOTS = dict(norm1_g=(0, 0, D), norm2_g=(1, 0, D), q_norm_g=(2, 0, HD), k_norm_g=(2, 128, HD), sinks=(2, 256, HEADS),
                    a_log=(2, 384, DN_H), dt_bias=(2, 512, DN_H), dn_norm_g=(2, 640, DN_D))
_CONV_ROW = 4
_ADA_B_ROW = 8


def _pack_small(vals, conv, ada_b):
    def row(pieces):
        out, at = [], 0
        for col, val in pieces:
            out += [jnp.zeros((1, col - at), F32), val.reshape(1, -1)]
            at = col + val.size
        return jnp.concatenate(out + [jnp.zeros((1, CONVW - at), F32)], axis=1)
    rows = {}
    for name, (r, col, n) in _SMALL_SLOTS.items():
        rows.setdefault(r, []).append((col, vals[name]))
    blank = jnp.zeros((1, CONVW), F32)
    top = [row(sorted(rows[r], key=lambda p: p[0])) if r in rows else blank for r in range(_CONV_ROW)]
    conv_rows = jnp.concatenate([conv, jnp.zeros((CONV, CONVW - conv.shape[1]), F32)], axis=1)
    tail = jnp.zeros((SMALL_ROWS - _ADA_B_ROW - 4, CONVW), F32)
    return jnp.concatenate(top + [conv_rows, ada_b.reshape(4, CONVW), tail], axis=0)


def _unpack_small(sheet, conv_cols):
    out = {name: sheet[row, col:col + n].reshape(1, n) for name, (row, col, n) in _SMALL_SLOTS.items()}
    out["conv_w"] = sheet[_CONV_ROW:_CONV_ROW + CONV, 0:conv_cols].reshape(1, CONV, 1, conv_cols)
    out["ada_b"] = sheet[_ADA_B_ROW:_ADA_B_ROW + 4, :].reshape(1, 6 * D)
    return out


def _w_in_segments():
    shard = IN_WIDTH // N_CHIP
    cuts = sorted({0, IN_WIDTH, C_Z, C_Z + 2 * DN_H} | {k * shard for k in range(1, N_CHIP)})
    segs = []
    for a, b in zip(cuts[:-1], cuts[1:]):
        k = a // shard
        pad = a if a < C_Z else (C_BA + a - C_Z if a < C_Z + 2 * DN_H else a - 2 * DN_H)
        segs.append((k, a - k * shard, b - k * shard, pad))
    return segs


def _pad_w_in(f):
    parts = [f[k][:, lo:hi] for k, lo, hi, _ in sorted(_w_in_segments(), key=lambda s: s[3])]
    return jnp.concatenate(parts + [jnp.zeros((f.shape[1], IN_PAD - IN_WIDTH), f.dtype)], axis=1)


def _unpad_w_in(g):
    return jnp.stack([jnp.concatenate([g[:, pad:pad + hi - lo] for kk, lo, hi, pad in _w_in_segments() if kk == k], axis=1)
                      for k in range(N_CHIP)])


def _blocks_to_cols(f):
    return f.transpose(1, 0, 2).reshape(f.shape[1], N_CHIP * f.shape[2])


def kernel(x, c, positions, ada_w, ada_b, norm1_g, w_in, conv_w, q_norm_g, k_norm_g, sinks, a_log, dt_bias, dn_norm_g, w_branch, w_out, norm2_g, w_gate_up, w_down, loss_target, m_ada_w, m_ada_b, m_norm1_g, m_w_in, m_conv_w, m_q_norm_g, m_k_norm_g, m_sinks, m_a_log, m_dt_bias, m_dn_norm_g, m_w_branch, m_w_out, m_norm2_g, m_w_gate_up, m_w_down, v_ada_w, v_ada_b, v_norm1_g, v_w_in, v_conv_w, v_q_norm_g, v_k_norm_g, v_sinks, v_a_log, v_dt_bias, v_dn_norm_g, v_w_branch, v_w_out, v_norm2_g, v_w_gate_up, v_w_down):
    ix, iy, ic = lax.axis_index("x"), lax.axis_index("y"), lax.axis_index("c")
    dev = 4 * ix + 2 * iy + ic
    chip = 2 * ix + iy
    n_seq = x.shape[0]
    conv_cols = conv_w.shape[-1]

    c_all, conv_all = _gather_devices("gather_cond", [c, conv_w.reshape(CONV, conv_cols)], dev)
    c_all = c_all.reshape(N_DEV * n_seq, D)
    ada_cols = ada_w.shape[-1]
    ada_b_cols = lax.dynamic_slice(ada_b, (0, chip * ada_cols), (1, ada_cols))
    mod_cols = _ada_fwd(c_all, ada_w[0], ada_b_cols)
    (mod_blocks,) = _gather_chips("gather_mod", [mod_cols], chip)
    mod_all = _blocks_to_cols(mod_blocks)
    mod = lax.dynamic_slice(mod_all, (dev * n_seq, 0), (n_seq, 6 * D)).reshape(n_seq, 1, 6 * D)
    conv_full = _blocks_to_cols(conv_all[0::2])

    (f_in,) = _gather_weights("w_in", [w_in[0].astype(BF)], chip)
    w_in_pad = _pad_w_in(f_in)

    loss, grad_x, dmod, small, (w_in_grad, r_br, r_out, r_gu, r_dn) = _local_step(
        x, mod, positions, loss_target, norm1_g, w_in_pad, conv_full.reshape(CONV, 1, CONVW), q_norm_g, k_norm_g, sinks,
        a_log, dt_bias, dn_norm_g, w_branch[0].astype(BF), w_out[0].astype(BF), norm2_g, w_gate_up[0].astype(BF),
        w_down[0].astype(BF), dist=(chip, ic))
    loss = lax.psum(loss, ("x", "y", "c"))

    part = _pack_small(small, small["conv_w"], jnp.sum(dmod, axis=(0, 1)).reshape(1, 6 * D))
    dmod_all, parts = _gather_devices("gather_small", [dmod.reshape(n_seq, 6 * D), part], dev, host=w_in_grad)
    dmod_all = dmod_all.reshape(N_DEV * n_seq, 6 * D)
    dmod_cols = lax.dynamic_slice(dmod_all, (0, chip * ada_cols), (N_DEV * n_seq, ada_cols))

    up_gu = _Hosted(lambda ex: _adamw("w_gate_up", w_gate_up[0], r_gu, m_w_gate_up[0], v_w_gate_up[0], ex))
    up_ada = _Hosted(lambda ex: _ada_bwd(c_all, dmod_cols, ada_w[0], m_ada_w[0], v_ada_w[0], ex))
    up_dn = _Hosted(lambda ex: _adamw("w_down", w_down[0], r_dn, m_w_down[0], v_w_down[0], ex))
    (r_in,) = _reduce_grads(("w_in",), [_unpad_w_in(w_in_grad.outs)], chip, ic, hosts=[up_gu, up_ada, up_dn])
    ada = up_ada.outs
    big = {"w_gate_up": (r_gu,) + tuple(up_gu.outs), "w_down": (r_dn,) + tuple(up_dn.outs)}
    for name, w, g, m, v in (("w_in", w_in, r_in, m_w_in, v_w_in), ("w_branch", w_branch, r_br, m_w_branch, v_w_branch),
                             ("w_out", w_out, r_out, m_w_out, v_w_out)):
        big[name] = (g,) + tuple(_adamw(name, w[0], g, m[0], v[0]))
    g_small = _unpack_small(_sum_devices(parts), CONVW)
    g_conv = lax.dynamic_slice(g_small["conv_w"].reshape(CONV, CONVW), (0, chip * conv_cols), (CONV, conv_cols))
    g_small["conv_w"] = g_conv.reshape(1, CONV, 1, conv_cols)

    given = dict(norm1_g=(norm1_g, m_norm1_g, v_norm1_g), norm2_g=(norm2_g, m_norm2_g, v_norm2_g),
                 q_norm_g=(q_norm_g, m_q_norm_g, v_q_norm_g), k_norm_g=(k_norm_g, m_k_norm_g, v_k_norm_g),
                 sinks=(sinks, m_sinks, v_sinks), a_log=(a_log, m_a_log, v_a_log), dt_bias=(dt_bias, m_dt_bias, v_dt_bias),
                 dn_norm_g=(dn_norm_g, m_dn_norm_g, v_dn_norm_g))
    sheets = [_pack_small({k: t[j] for k, t in given.items()}, cw.reshape(CONV, conv_cols), ab)
              for j, (cw, ab) in enumerate(((conv_w, ada_b), (m_conv_w, m_ada_b), (v_conv_w, v_ada_b)))]
    g_local = _pack_small(g_small, g_conv, g_small["ada_b"])
    upd = [_unpack_small(s, conv_cols) for s in _adamw("small", sheets[0], g_local, sheets[1], sheets[2])]

    names = ["ada_w", "ada_b", "norm1_g", "w_in", "conv_w", "q_norm_g", "k_norm_g", "sinks", "a_log", "dt_bias", "dn_norm_g",
             "w_branch", "w_out", "norm2_g", "w_gate_up", "w_down"]

    def leaf(name, j):
        if name == "ada_w":
            return ada[j][None]
        if name in big:
            return big[name][j][None]
        return g_small[name] if j == 0 else upd[j - 1][name]

    return (loss, grad_x) + tuple(leaf(n, j) for j in range(4) for n in names)
```

```python
import functools
from typing import Callable, NamedTuple

import jax
import jax.numpy as jnp
import numpy as np
from jax import lax
from jax.experimental import pallas as pl
from jax.experimental.pallas import tpu as pltpu

F32 = jnp.float32
BF = jnp.bfloat16

D = 1024
HEADS = 8
KV_HEADS = 2
GROUP = 4
HD = 64
BLK = 128
ROT = 16
THETA = 500000.0
QW = 512
KVW = 128
DN_H = 4
DN_D = 128
CONV = 4
CHUNK = 64
DNW = 512
CONVW = 1536
FFN = 2816
EPS = 1e-6
IN_WIDTH = 4872
IN_PAD = 4992
C_KV = 512
C_DN = 768
C_Z = 2304
C_GA = 2816
C_GB = 3840
C_BA = 4864
NEG = -1e30
N_DEV = 8
N_CHIP = 4

ADAM_LR = 0.001
ADAM_B1 = 0.9
ADAM_B2 = 0.999
ADAM_EPS = 1e-08
ADAM_WD = 0.01
ADAM_STEP = 10

VMEM_LIMIT = 60 * 1024 * 1024


def _cparams(**kw):
    return pltpu.CompilerParams(vmem_limit_bytes=VMEM_LIMIT, **kw)


def _dg(a, b, ca, cb):
    return lax.dot_general(a.astype(BF), b.astype(BF), (((ca,), (cb,)), ((), ())),
                           preferred_element_type=F32)


@jax.custom_vjp
def _mm(a, b):
    return _dg(a, b, 1, 0)


def _mm_fwd(a, b):
    return _dg(a, b, 1, 0), (a, b)


def _mm_bwd(res, dy):
    a, b = res
    return _dg(dy, b, 1, 1).astype(a.dtype), _dg(a, dy, 0, 0).astype(b.dtype)


_mm.defvjp(_mm_fwd, _mm_bwd)


@jax.custom_vjp
def _mm_nt(a, b):
    return _dg(a, b, 1, 1)


def _mm_nt_fwd(a, b):
    return _dg(a, b, 1, 1), (a, b)


def _mm_nt_bwd(res, dy):
    a, b = res
    return _dg(dy, b, 1, 0).astype(a.dtype), _dg(dy, a, 0, 0).astype(b.dtype)


_mm_nt.defvjp(_mm_nt_fwd, _mm_nt_bwd)


@jax.custom_vjp
def _mm_tn(a, b):
    return _dg(a, b, 0, 0)


def _mm_tn_fwd(a, b):
    return _dg(a, b, 0, 0), (a, b)


def _mm_tn_bwd(res, dy):
    a, b = res
    return _dg(b, dy, 1, 1).astype(a.dtype), _dg(a, dy, 1, 0).astype(b.dtype)


_mm_tn.defvjp(_mm_tn_fwd, _mm_tn_bwd)


def _mmx(a, b):
    return jnp.dot(a, b, precision=lax.Precision.HIGHEST, preferred_element_type=F32)


def _mmx_nt(a, b):
    return lax.dot_general(a, b, (((1,), (1,)), ((), ())), precision=lax.Precision.HIGHEST,
                           preferred_element_type=F32)


def _iota(shape, dim):
    return lax.broadcasted_iota(jnp.int32, shape, dim)


def _sigmoid(x):
    return 1.0 / (1.0 + jnp.exp(-x))


def _silu(x):
    return x * _sigmoid(x)


def _softplus(x):
    return jnp.maximum(x, 0.0) + jnp.log(1.0 + jnp.exp(-jnp.abs(x)))


def _rms(x, gain):
    return x * lax.rsqrt(jnp.mean(x * x, axis=-1, keepdims=True) + EPS) * gain


def _norm_mod(x, gain, shift, scale):
    return _rms(x, gain) * (1.0 + scale) + shift


def _split(a):
    hi = a.astype(BF)
    return hi, (a - hi.astype(F32)).astype(BF)


def _dg3(a, b, ca, cb):
    ah, al = _split(a)
    bh, bl = _split(b)

    def dg(x, y):
        return lax.dot_general(x, y, (((ca,), (cb,)), ((), ())), preferred_element_type=F32)
    return dg(ah, bh) + (dg(ah, bl) + dg(al, bh))


@jax.custom_vjp
def _mm3(a, b):
    return _dg3(a, b, 1, 0)


def _mm3_fwd(a, b):
    return _dg3(a, b, 1, 0), (a, b)


def _mm3_bwd(res, dy):
    a, b = res
    return _dg3(dy, b, 1, 1), _dg3(a, dy, 0, 0)


_mm3.defvjp(_mm3_fwd, _mm3_bwd)


def _qk_prep(slabs, gain, cos, sin):
    r = _iota((2 * HD, 2 * HD), 0)
    c = _iota((2 * HD, 2 * HD), 1)
    seg = jnp.where(r // HD == c // HD, 1.0 / HD, 0.0).astype(F32)
    half = ROT // 2
    cd = c % HD
    pair = jnp.where(((cd < half) & (r == c + half)) | ((cd >= half) & (cd < ROT) & (r == c - half)), 1.0, 0.0).astype(F32)
    out = []
    for x in slabs:
        y = x * lax.rsqrt(_mm3(x * x, seg) + EPS) * gain
        out.append(y * cos + _mm3(y, pair) * sin)
    return out


def _attn_block(qs, kc, kp, vc, vp, sinks, has_prev):
    rows = GROUP * BLK
    qi = _iota((rows, 2 * BLK), 0) % BLK + BLK
    kj = _iota((rows, 2 * BLK), 1)
    dist = qi - kj
    valid = (dist >= 0) & (dist < BLK) & ((kj >= BLK) | has_prev)
    grp = _iota((rows, HEADS), 0) // BLK
    col = _iota((rows, HEADS), 1)

    outs = []
    for h in range(KV_HEADS):
        q = jnp.concatenate([qs[h * GROUP + g] for g in range(GROUP)], axis=0)
        k = jnp.concatenate([kp[h], kc[h]], axis=0)
        v = jnp.concatenate([vp[h], vc[h]], axis=0)
        s = _mm_nt(q, k) * (HD ** -0.5)
        s = jnp.where(valid, s, NEG)
        sink = jnp.sum(jnp.where(col == h * GROUP + grp, sinks, 0.0), axis=-1, keepdims=True)
        m = lax.stop_gradient(jnp.maximum(jnp.max(s, axis=-1, keepdims=True), sink))
        p = jnp.exp(s - m)
        probs = p / (jnp.sum(p, axis=-1, keepdims=True) + jnp.exp(sink - m))
        o = _mm(probs, v)
        outs += [o[g * BLK:(g + 1) * BLK] for g in range(GROUP)]
    return outs


def _dn_tail(ys, ba, alog, dtb):
    return [_dn_act(y, j < 2 * DN_H) for j, y in enumerate(ys)], _dn_gates(ba, alog, dtb)


def _dn_act(y, normalize):
    s = _silu(y)
    return s * lax.rsqrt(jnp.sum(s * s, axis=-1, keepdims=True) + EPS) if normalize else s


def _dn_gates(ba, alog, dtb):
    lane = _iota(ba.shape, 1)
    beta = _sigmoid(ba)
    g = -jnp.exp(alog) * _softplus(ba + dtb)
    return jnp.where(lane < DN_H, beta, jnp.where(lane < 2 * DN_H, g, 0.0))


def _bdg(a, b, ca, cb):
    return lax.dot_general(a.astype(BF), b.astype(BF), (((ca,), (cb,)), ((0,), (0,))), preferred_element_type=F32)


@jax.custom_vjp
def _bmm(a, b):
    return _bdg(a, b, 2, 1)


def _bmm_fwd(a, b):
    return _bdg(a, b, 2, 1), (a, b)


def _bmm_bwd(res, dy):
    a, b = res
    return _bdg(dy, b, 2, 2), _bdg(a, dy, 1, 1)


_bmm.defvjp(_bmm_fwd, _bmm_bwd)


@jax.custom_vjp
def _bmm_nt(a, b):
    return _bdg(a, b, 2, 2)


def _bmm_nt_fwd(a, b):
    return _bdg(a, b, 2, 2), (a, b)


def _bmm_nt_bwd(res, dy):
    a, b = res
    return _bdg(dy, b, 2, 1), _bdg(dy, a, 1, 1)


_bmm_nt.defvjp(_bmm_nt_fwd, _bmm_nt_bwd)


def _bmmx(a, b):
    return lax.dot_general(a, b, (((2,), (1,)), ((0,), (0,))), precision=lax.Precision.HIGHEST,
                           preferred_element_type=F32)


def _neumann_inverse(lmat):
    C = CHUNK
    eye = jnp.where(_iota((C, C), 0) == _iota((C, C), 1), 1.0, 0.0).astype(F32)[None]
    a = -lmat
    tinv = eye + a
    pw = _bmmx(a, a)
    for _ in range(4):
        both = _bmmx(jnp.concatenate([pw, tinv], axis=1), pw)
        pw, tinv = both[:, :C], tinv + both[:, C:]
    return tinv + _bmmx(tinv, pw)


def _inverse_bwd(tinv, d_tinv):
    x = lax.dot_general(d_tinv, tinv, (((2,), (2,)), ((0,), (0,))), precision=lax.Precision.HIGHEST,
                        preferred_element_type=F32)
    return -lax.dot_general(tinv, x, (((1,), (1,)), ((0,), (0,))), precision=lax.Precision.HIGHEST,
                            preferred_element_type=F32)


@jax.custom_vjp
def _tri_inverse(lmat):
    return _neumann_inverse(lmat)


def _tri_inverse_fwd(lmat):
    tinv = _neumann_inverse(lmat)
    return tinv, tinv


def _tri_inverse_bwd(tinv, d_tinv):
    return (_inverse_bwd(tinv, d_tinv),)


_tri_inverse.defvjp(_tri_inverse_fwd, _tri_inverse_bwd)


@jax.custom_vjp
def _tri_inverse_known(lmat, tinv):
    return tinv


def _tri_inverse_known_fwd(lmat, tinv):
    return tinv, tinv


def _tri_inverse_known_bwd(tinv, d_tinv):
    return _inverse_bwd(tinv, d_tinv), jnp.zeros_like(tinv)


_tri_inverse_known.defvjp(_tri_inverse_known_fwd, _tri_inverse_known_bwd)


def _dn_intra(q, k, v, bg, tinv=None):
    C = CHUNK
    G = bg.shape[0]
    r = _iota((C, C), 0)
    c = _iota((C, C), 1)
    incl = (r >= c)[None]
    strict = (r > c)[None]
    eye = jnp.where(r == c, 1.0, 0.0).astype(F32)[None]
    tri = jnp.broadcast_to(jnp.where(r >= c, 1.0, 0.0).astype(F32)[None], (G, C, C))
    gc_all = _bmmx(tri, bg)
    lane = _iota((C, DN_D), 1)

    def per_head(x, offset):
        return jnp.concatenate([jnp.sum(jnp.where(lane == offset + h, x[g], 0.0), axis=-1, keepdims=True)[None]
                                for g in range(G) for h in range(DN_H)], axis=0)
    beta = per_head(bg, 0)
    gcol = per_head(gc_all, DN_H)
    grow = jnp.sum(eye * gcol, axis=1, keepdims=True)
    glast = jnp.sum(jnp.where(_iota((1, C, 1), 1) == C - 1, gcol, 0.0), axis=1, keepdims=True)
    decay = jnp.exp(jnp.where(incl, gcol - grow, NEG))
    q = q * (DN_D ** -0.5)
    kb = k * beta
    lmat = jnp.where(strict, _bmm_nt(kb, k) * decay, 0.0)
    tinv = _tri_inverse(lmat) if tinv is None else _tri_inverse_known(lmat, tinv)
    egc = jnp.exp(gcol)
    u = _bmm(tinv, v * beta)
    w = _bmm(tinv, kb * egc)
    a = _bmm_nt(q, k) * decay
    return u, w, q * egc, k * jnp.exp(glast - gcol), a, jnp.exp(glast), tinv


@jax.custom_vjp
def _bmm_tn(a, b):
    return _bdg(a, b, 1, 1)


def _bmm_tn_fwd(a, b):
    return _bdg(a, b, 1, 1), (a, b)


def _bmm_tn_bwd(res, dy):
    a, b = res
    return _bdg(b, dy, 2, 2), _bdg(a, dy, 2, 1)


_bmm_tn.defvjp(_bmm_tn_fwd, _bmm_tn_bwd)


def _dn_rec(state, u, w, qd, kd, a, cd):
    v_new = u - _bmm(w, state)
    out = _bmm(qd, state) + _bmm(a, v_new)
    return state * cd + _bmm_tn(kd, v_new), out


def _mix_tile(o_attn, o_raw, zs, ga, gb, x, gate1, dn_g, wb_a, wb_d, w_out, p_ya, p_yd, p_out):
    o_dn = jnp.concatenate([_rms(o_raw[h], dn_g) * _silu(zs[h]) for h in range(DN_H)], axis=-1)
    y_a = _mm(o_attn, wb_a) + p_ya
    y_d = _mm(o_dn, wb_d) + p_yd
    merged = _sigmoid(ga) * y_a + _sigmoid(gb) * y_d
    out = _mm(merged, w_out) + p_out
    return x + gate1 * out, o_dn, merged


def _mlp_tile(x1, gain, shift, scale, gate2, w_gu, w_dn, tgt, p_gu, p_yy):
    h2 = _norm_mod(x1, gain, shift, scale)
    gu = jnp.concatenate([_mm(h2, w) for w in w_gu], axis=-1) + p_gu
    act = _silu(gu[:, :FFN]) * gu[:, FFN:]
    yy = _mm(act, w_dn) + p_yy
    y = x1 + gate2 * yy
    err = y - tgt
    return 0.5 * jnp.sum(err * err) * (1.0 / D), (h2, act)


def _tok(bt, f):
    return pl.BlockSpec((None, bt, f), lambda b, i: (b, i, 0))


def _full(shape):
    return pl.BlockSpec(shape, lambda b, i: (0,) * len(shape))


def _resident(shape):
    return pl.BlockSpec(shape, lambda b, i: (0,) * len(shape), pipeline_mode=pl.Buffered(1))


def _per_batch(f):
    return pl.BlockSpec((None, 1, f), lambda b, i: (b, 0, 0))


def _sds(shape, dtype):
    return jax.ShapeDtypeStruct(shape, dtype)


class _Exchange(NamedTuple):
    ins: tuple
    out_shapes: tuple
    n_remote: int
    plan: Callable
    n_forward: int = 0
    forward: Callable = None


def _remote_copies(remote, send_sems, recv_sems):
    return [pltpu.make_async_remote_copy(src_ref=src, dst_ref=dst, send_sem=send_sems.at[i], recv_sem=recv_sems.at[i],
                                         device_id=peer, device_id_type=pl.DeviceIdType.MESH)
            for i, (src, dst, peer) in enumerate(remote)]


def _exchange_copies(ex, in_refs, out_refs, send_sems, recv_sems):
    remote = ex.plan((lax.axis_index("x"), lax.axis_index("y"), lax.axis_index("c")), in_refs, out_refs)
    assert len(remote) == ex.n_remote
    return _remote_copies(remote, send_sems, recv_sems)


def _forward_copies(ex, out_refs, send_sems, recv_sems):
    remote = ex.forward((lax.axis_index("x"), lax.axis_index("y"), lax.axis_index("c")), out_refs)
    assert len(remote) == ex.n_forward
    return _remote_copies(remote, send_sems, recv_sems)


def _exchange_sems(ex):
    sems = [pltpu.SemaphoreType.DMA((ex.n_remote,)), pltpu.SemaphoreType.DMA((ex.n_remote,))]
    if ex.n_forward:
        sems += [pltpu.SemaphoreType.DMA((ex.n_forward,)), pltpu.SemaphoreType.DMA((ex.n_forward,))]
    return sems


def _hosted_call(body, name, grid, in_specs, out_specs, out_shape, scratch_shapes, semantics, ins, ex=None):
    if ex is None:
        outs = pl.pallas_call(body, name=name, grid=grid, in_specs=in_specs, out_specs=out_specs, out_shape=out_shape,
                              scratch_shapes=scratch_shapes,
                              compiler_params=_cparams(dimension_semantics=semantics))(*ins)
        return outs, ()
    n_in, n_out, n_scr = len(ins), len(out_shape), len(scratch_shapes)
    c_in, c_out = len(ex.ins), len(ex.out_shapes)
    steps = 1
    for g in grid:
        steps *= g

    def wrapped(*refs):
        a, b, c, d = n_in, n_in + c_in, n_in + c_in + n_out, n_in + c_in + n_out + c_out
        scratch, sems = refs[d:d + n_scr], refs[d + n_scr:]
        step = 0
        for axis, g in enumerate(grid):
            step = step * g + pl.program_id(axis)

        def first_phase():
            return _exchange_copies(ex, refs[a:b], refs[c:d], sems[0], sems[1])

        @pl.when(step == 0)
        def _():
            for cp in first_phase():
                cp.start()
        body(*refs[:a], *refs[b:c], *scratch)

        if ex.n_forward:
            @pl.when(step == (3 * steps) // 4)
            def _():
                for cp in first_phase():
                    cp.wait_recv()
                for cp in _forward_copies(ex, refs[c:d], sems[2], sems[3]):
                    cp.start()

        @pl.when(step == steps - 1)
        def _():
            cps = first_phase()
            if ex.n_forward:
                fwd = _forward_copies(ex, refs[c:d], sems[2], sems[3])
                for cp in fwd:
                    cp.wait_recv()
                for cp in cps + fwd:
                    cp.wait_send()
            else:
                for cp in cps:
                    cp.wait_recv()
                for cp in cps:
                    cp.wait_send()

    any_spec = pl.BlockSpec(memory_space=pl.ANY)
    res = pl.pallas_call(
        wrapped, name=name, grid=grid, in_specs=list(in_specs) + [any_spec] * c_in,
        out_specs=list(out_specs) + [any_spec] * c_out, out_shape=list(out_shape) + list(ex.out_shapes),
        scratch_shapes=list(scratch_shapes) + _exchange_sems(ex),
        compiler_params=_cparams(dimension_semantics=("arbitrary",) * len(grid)),
    )(*ins, *ex.ins)
    return res[:n_out], res[n_out:]


def _acc(ref, val, first):
    @pl.when(first)
    def _():
        ref[...] = val

    @pl.when(jnp.logical_not(first))
    def _():
        ref[...] += val


def _in_proj(x, mod, norm1_g, w_in, bt):
    B, S, _ = x.shape

    def body(x_ref, mod_ref, g_ref, w_ref, q_ref, kv_ref, dn_ref, z_ref, ga_ref, gb_ref, ba_ref, h_ref):
        h = _norm_mod(x_ref[...], g_ref[...], mod_ref[:, 0:D], mod_ref[:, D:2 * D]).astype(BF)
        h_ref[...] = h

        def proj(c0, c1):
            return jnp.dot(h, w_ref[:, c0:c1], preferred_element_type=F32)
        q_ref[...] = proj(0, C_KV).astype(BF)
        kv_ref[...] = proj(C_KV, C_DN).astype(BF)
        dn_ref[...] = proj(C_DN, C_Z).astype(BF)
        z_ref[...] = proj(C_Z, C_GA).astype(BF)
        ga_ref[...] = proj(C_GA, C_GB).astype(BF)
        gb_ref[...] = proj(C_GB, C_BA).astype(BF)
        ba_ref[...] = proj(C_BA, IN_PAD)

    widths = (QW, 2 * KVW, CONVW, DNW, D, D)
    return pl.pallas_call(
        body, name="in_proj", grid=(B, S // bt),
        in_specs=[_tok(bt, D), _per_batch(6 * D), _full((1, D)), _resident((D, IN_PAD))],
        out_specs=[_tok(bt, w) for w in widths] + [_tok(bt, 128), _tok(bt, D)],
        out_shape=[_sds((B, S, w), BF) for w in widths] + [_sds((B, S, 128), F32), _sds((B, S, D), BF)],
        compiler_params=_cparams(dimension_semantics=("parallel", "parallel")),
    )(x, mod, norm1_g, w_in)


def _prev_blk(bt, f):
    return pl.BlockSpec((None, bt, f), lambda b, i: (b, jnp.maximum(i - 1, 0), 0))


QKV = QW + 2 * KVW


def _qk_slabs(q_ref, kv_ref):
    return ([q_ref[:, j * 2 * HD:(j + 1) * 2 * HD].astype(F32) for j in range(QW // (2 * HD))],
            [kv_ref[:, 0:KVW].astype(F32)])


def _qk_prep_fwd(q, kv, cos, sin, qg, kg, bt):
    B, S, _ = q.shape

    def body(q_ref, kv_ref, cos_ref, sin_ref, qg_ref, kg_ref, o_ref):
        qs, ks = _qk_slabs(q_ref, kv_ref)
        qn = _qk_prep(qs, qg_ref[...], cos_ref[...], sin_ref[...])
        kn = _qk_prep(ks, kg_ref[...], cos_ref[...], sin_ref[...])
        for j, t in enumerate(qn + kn):
            o_ref[:, j * 2 * HD:(j + 1) * 2 * HD] = t.astype(BF)
        o_ref[:, QW + KVW:QKV] = kv_ref[:, KVW:2 * KVW]

    return pl.pallas_call(
        body, name="qk_prep_fwd", grid=(B, S // bt),
        in_specs=[_tok(bt, QW), _tok(bt, 2 * KVW), _tok(bt, 2 * HD), _tok(bt, 2 * HD), _full((1, 2 * HD)), _full((1, 2 * HD))],
        out_specs=_tok(bt, QKV), out_shape=_sds((B, S, QKV), BF),
        compiler_params=_cparams(dimension_semantics=("parallel", "parallel")),
    )(q, kv, cos, sin, qg, kg)


def _qk_prep_bwd(q, kv, cos, sin, qg, kg, dqn, dkvn, bt, ex=None):
    B, S, _ = q.shape

    def body(q_ref, kv_ref, cos_ref, sin_ref, qg_ref, kg_ref, dqn_ref, dkvn_ref, dq_ref, dkv_ref, dqg_ref, dkg_ref):
        qs, ks = _qk_slabs(q_ref, kv_ref)
        cos, sin = cos_ref[...], sin_ref[...]

        def f(qs, ks, qg, kg):
            return _qk_prep(qs, qg, cos, sin), _qk_prep(ks, kg, cos, sin)
        _, vjp = jax.vjp(f, qs, ks, qg_ref[...], kg_ref[...])
        n_q = len(qs)
        d_q = [dqn_ref[:, j * 2 * HD:(j + 1) * 2 * HD].astype(F32) for j in range(n_q)]
        d_k = [dkvn_ref[:, 0:KVW].astype(F32)]
        dqs, dks, dqg, dkg = vjp((d_q, d_k))
        for j in range(n_q):
            dq_ref[:, j * 2 * HD:(j + 1) * 2 * HD] = dqs[j].astype(BF)
        dkv_ref[:, 0:KVW] = dks[0].astype(BF)
        dkv_ref[:, KVW:2 * KVW] = dkvn_ref[:, KVW:2 * KVW]
        first = (pl.program_id(0) == 0) & (pl.program_id(1) == 0)
        _acc(dqg_ref, dqg, first)
        _acc(dkg_ref, dkg, first)

    return _hosted_call(
        body, "qk_prep_bwd", (B, S // bt),
        in_specs=[_tok(bt, QW), _tok(bt, 2 * KVW), _tok(bt, 2 * HD), _tok(bt, 2 * HD), _full((1, 2 * HD)), _full((1, 2 * HD)),
                  _tok(bt, QW), _tok(bt, 2 * KVW)],
        out_specs=[_tok(bt, QW), _tok(bt, 2 * KVW), _full((1, 2 * HD)), _full((1, 2 * HD))],
        out_shape=[_sds((B, S, QW), BF), _sds((B, S, 2 * KVW), BF), _sds((1, 2 * HD), F32), _sds((1, 2 * HD), F32)],
        scratch_shapes=[], semantics=("arbitrary", "arbitrary"), ins=(q, kv, cos, sin, qg, kg, dqn, dkvn), ex=ex)


def _attn_load(qkv_ref, kvp_ref):
    qs = [qkv_ref[:, h * HD:(h + 1) * HD].astype(F32) for h in range(HEADS)]
    kc = [qkv_ref[:, QW + h * HD:QW + (h + 1) * HD].astype(F32) for h in range(KV_HEADS)]
    vc = [qkv_ref[:, QW + KVW + h * HD:QW + KVW + (h + 1) * HD].astype(F32) for h in range(KV_HEADS)]
    kp = [kvp_ref[:, h * HD:(h + 1) * HD].astype(F32) for h in range(KV_HEADS)]
    vp = [kvp_ref[:, KVW + h * HD:KVW + (h + 1) * HD].astype(F32) for h in range(KV_HEADS)]
    return qs, kc, kp, vc, vp


def _kv_prev_spec(index):
    return pl.BlockSpec((None, BLK, 2 * KVW), lambda b, i: (b, index(i), QW // (2 * KVW)))


def _attn_fwd(qkv, sinks):
    B, S, _ = qkv.shape

    def body(qkv_ref, kvp_ref, sk_ref, o_ref):
        qs, kc, kp, vc, vp = _attn_load(qkv_ref, kvp_ref)
        outs = _attn_block(qs, kc, kp, vc, vp, sk_ref[...], pl.program_id(1) > 0)
        for h in range(HEADS):
            o_ref[:, h * HD:(h + 1) * HD] = outs[h].astype(BF)

    return pl.pallas_call(
        body, name="attn_fwd", grid=(B, S // BLK),
        in_specs=[_tok(BLK, QKV), _kv_prev_spec(lambda i: jnp.maximum(i - 1, 0)), _full((1, HEADS))],
        out_specs=_tok(BLK, QW), out_shape=_sds((B, S, QW), BF),
        compiler_params=_cparams(dimension_semantics=("parallel", "parallel")),
    )(qkv, qkv, sinks)


def _conv_fwd_tile(xe_ref, x_ref, halo_ref, cw_ref, first, bt):
    halo = halo_ref[...].astype(F32)
    xe_ref[0:8, :] = jnp.where(first, 0.0, halo)
    xe_ref[8:bt + 8, :] = x_ref[...].astype(F32)
    y = cw_ref[0:1, :] * xe_ref[5:bt + 5, :]
    for j in range(1, CONV):
        y = y + cw_ref[j:j + 1, :] * xe_ref[5 + j:bt + 5 + j, :]
    return y


def _halo_spec(bt):
    return pl.BlockSpec((None, 8, CONVW), lambda b, i: (b, jnp.maximum(i * (bt // 8) - 1, 0), 0))


def _dn_prep(dn, ba, conv_w, alog, dtb, bt):
    B, S, _ = dn.shape

    def body(x_ref, halo_ref, ba_ref, cw_ref, al_ref, dt_ref, qkv_ref, bg_ref, y_ref, xe_ref):
        y = _conv_fwd_tile(xe_ref, x_ref, halo_ref, cw_ref, pl.program_id(1) == 0, bt)
        y_ref[...] = y.astype(BF)
        ys = [y[:, j * DN_D:(j + 1) * DN_D] for j in range(3 * DN_H)]
        out, bg = _dn_tail(ys, ba_ref[...], al_ref[...], dt_ref[...])
        for j in range(3 * DN_H):
            qkv_ref[:, j * DN_D:(j + 1) * DN_D] = out[j]
        bg_ref[...] = bg

    return pl.pallas_call(
        body, name="dn_prep", grid=(B, S // bt),
        in_specs=[_tok(bt, CONVW), _halo_spec(bt), _tok(bt, 128), _full((CONV, CONVW)), _full((1, 128)), _full((1, 128))],
        out_specs=[_tok(bt, CONVW), _tok(bt, 128), _tok(bt, CONVW)],
        out_shape=[_sds((B, S, CONVW), F32), _sds((B, S, 128), F32), _sds((B, S, CONVW), BF)],
        scratch_shapes=[pltpu.VMEM((bt + 8, CONVW), F32)],
        compiler_params=_cparams(dimension_semantics=("parallel", "arbitrary")),
    )(dn, dn, ba, conv_w, alog, dtb)


def _dn_load(qkv_ref):
    qs = [qkv_ref[:, h * DN_D:(h + 1) * DN_D] for h in range(DN_H)]
    ks = [qkv_ref[:, DNW + h * DN_D:DNW + (h + 1) * DN_D] for h in range(DN_H)]
    vs = [qkv_ref[:, 2 * DNW + h * DN_D:2 * DNW + (h + 1) * DN_D] for h in range(DN_H)]
    return qs, ks, vs


DN_GROUP = 4
AW = DN_H * CHUNK


def _stack_heads(ref, G, offset, width):
    return jnp.stack([ref[g * CHUNK:(g + 1) * CHUNK, offset + h * width:offset + (h + 1) * width]
                      for g in range(G) for h in range(DN_H)])


def _dn_load_stack(qkv_ref, G):
    return tuple(_stack_heads(qkv_ref, G, j * DNW, DN_D) for j in range(3))


def _cd_spec(n):
    return pl.BlockSpec((None, n, 1, DN_D), lambda b, i: (b, i, 0, 0))


def _dn_intra_fwd(qkv, bg, ex=None):
    B, S, _ = qkv.shape
    nc = S // CHUNK
    G = min(DN_GROUP, nc)
    rows = G * CHUNK

    def body(qkv_ref, bg_ref, u_ref, w_ref, qd_ref, kd_ref, a_ref, cd_ref, t_ref):
        q, k, v = _dn_load_stack(qkv_ref, G)
        u, w, qd, kd, a, cd, tinv = _dn_intra(q, k, v, bg_ref[...].reshape(G, CHUNK, DN_D))
        lane_row = _iota((1, DN_D), 1)
        for g in range(G):
            rows = slice(g * CHUNK, (g + 1) * CHUNK)
            cd_row = jnp.zeros((1, DN_D), F32)
            for h in range(DN_H):
                n = g * DN_H + h
                cols = slice(h * DN_D, (h + 1) * DN_D)
                u_ref[rows, cols] = u[n]
                w_ref[rows, cols] = w[n].astype(BF)
                qd_ref[rows, cols] = qd[n].astype(BF)
                kd_ref[rows, cols] = kd[n].astype(BF)
                a_ref[rows, h * CHUNK:(h + 1) * CHUNK] = a[n].astype(BF)
                t_ref[rows, h * CHUNK:(h + 1) * CHUNK] = tinv[n]
                cd_row = cd_row + jnp.where(lane_row == h, cd[n], 0.0)
            cd_ref[g] = cd_row

    return _hosted_call(
        body, "dn_intra_fwd", (B, nc // G),
        in_specs=[_tok(rows, CONVW), _tok(rows, 128)],
        out_specs=[_tok(rows, DNW)] * 4 + [_tok(rows, AW), _cd_spec(G), _tok(rows, AW)],
        out_shape=[_sds((B, S, DNW), F32)] + [_sds((B, S, DNW), BF)] * 3 + [_sds((B, S, AW), BF), _sds((B, nc, 1, DN_D), F32),
                                                                            _sds((B, S, AW), F32)],
        scratch_shapes=[], semantics=("parallel", "parallel"), ins=(qkv, bg), ex=ex)


REC_GROUP = 2


def _rec_stack(ref, B, width, c):
    rows = slice(c * CHUNK, (c + 1) * CHUNK)
    return jnp.stack([ref[b, rows, h * width:(h + 1) * width].astype(F32) for b in range(B) for h in range(DN_H)])


def _rec_load(B, u_ref, w_ref, qd_ref, kd_ref, a_ref, cd_ref, c):
    lane_row = _iota((1, DN_D), 1)
    cd = jnp.stack([jnp.sum(jnp.where(lane_row == h, cd_ref[b, c], 0.0), axis=-1, keepdims=True)
                    for b in range(B) for h in range(DN_H)])
    return (_rec_stack(u_ref, B, DN_D, c), _rec_stack(w_ref, B, DN_D, c), _rec_stack(qd_ref, B, DN_D, c),
            _rec_stack(kd_ref, B, DN_D, c), _rec_stack(a_ref, B, CHUNK, c), cd)


def _rec_store(B, ref, val, width, c):
    for b in range(B):
        for h in range(DN_H):
            ref[b, c * CHUNK:(c + 1) * CHUNK, h * width:(h + 1) * width] = val[b * DN_H + h]


def _rec_specs(B, R, index):
    def tok(f):
        return pl.BlockSpec((B, R * CHUNK, f), lambda i: (0, index(i), 0))
    cd = pl.BlockSpec((B, R, 1, DN_D), lambda i: (0, index(i), 0, 0))
    st = pl.BlockSpec((B, R, DN_H, DN_D, DN_D), lambda i: (0, index(i), 0, 0, 0))
    return tok, cd, st


def _dn_rec_fwd(u, w, qd, kd, a, cd, ex=None):
    B, S, _ = u.shape
    nc = S // CHUNK
    R = REC_GROUP if nc % REC_GROUP == 0 else 1
    tok, cd_spec, st_spec = _rec_specs(B, R, lambda i: i)

    def body(u_ref, w_ref, qd_ref, kd_ref, a_ref, cd_ref, o_ref, st_ref, s_ref):
        @pl.when(pl.program_id(0) == 0)
        def _():
            s_ref[...] = jnp.zeros_like(s_ref)
        state = s_ref[...]
        for c in range(R):
            st_ref[:, c] = state.reshape(B, DN_H, DN_D, DN_D)
            state, out = _dn_rec(state, *_rec_load(B, u_ref, w_ref, qd_ref, kd_ref, a_ref, cd_ref, c))
            _rec_store(B, o_ref, out, DN_D, c)
        s_ref[...] = state

    return _hosted_call(
        body, "dn_rec_fwd", (nc // R,),
        in_specs=[tok(DNW)] * 4 + [tok(AW), cd_spec],
        out_specs=[tok(DNW), st_spec],
        out_shape=[_sds((B, S, DNW), F32), _sds((B, nc, DN_H, DN_D, DN_D), F32)],
        scratch_shapes=[pltpu.VMEM((B * DN_H, DN_D, DN_D), F32)],
        semantics=("arbitrary",), ins=(u, w, qd, kd, a, cd), ex=ex)


def _mix_load(oa_ref, or_ref, z_ref):
    o_raw = [or_ref[:, h * DN_D:(h + 1) * DN_D] for h in range(DN_H)]
    zs = [z_ref[:, h * DN_D:(h + 1) * DN_D].astype(F32) for h in range(DN_H)]
    return oa_ref[...].astype(F32), o_raw, zs


def _mix_fwd(o_attn, o_raw, z, ga, gb, x, mod, dn_g, w_branch, w_out, bt):
    B, S, _ = x.shape

    def body(oa_ref, or_ref, z_ref, ga_ref, gb_ref, x_ref, mod_ref, dg_ref, wb_ref, wo_ref, x1_ref, od_ref, mg_ref):
        oa, o_r, zs = _mix_load(oa_ref, or_ref, z_ref)
        x1, o_dn, merged = _mix_tile(oa, o_r, zs, ga_ref[...].astype(F32), gb_ref[...].astype(F32), x_ref[...],
                                     mod_ref[:, 2 * D:3 * D], dg_ref[...], wb_ref[0:QW, :], wb_ref[QW:2 * QW, :],
                                     wo_ref[...], 0.0, 0.0, 0.0)
        x1_ref[...] = x1
        od_ref[...] = o_dn.astype(BF)
        mg_ref[...] = merged.astype(BF)

    return pl.pallas_call(
        body, name="mix_fwd", grid=(B, S // bt),
        in_specs=[_tok(bt, QW), _tok(bt, DNW), _tok(bt, DNW), _tok(bt, D), _tok(bt, D), _tok(bt, D), _per_batch(6 * D),
                  _full((1, DN_D)), _resident((D, D)), _resident((D, D))],
        out_specs=[_tok(bt, D), _tok(bt, DNW), _tok(bt, D)],
        out_shape=[_sds((B, S, D), F32), _sds((B, S, DNW), BF), _sds((B, S, D), BF)],
        compiler_params=_cparams(dimension_semantics=("parallel", "parallel")),
    )(o_attn, o_raw, z, ga, gb, x, mod, dn_g, w_branch, w_out)


def _mlp(x1, tgt, mod, norm2_g, w_gu, w_dn, bt):
    B, S, _ = x1.shape

    def body(x1_ref, t_ref, mod_ref, g_ref, wgu_ref, wdn_ref,
             dx1_ref, h2_ref, act_ref, dgu_ref, dyy_ref, loss_ref, dmod_ref, dg_ref):
        w_gu_v, w_dn_v, t = [wgu_ref[k] for k in range(N_CHIP)], wdn_ref[...], t_ref[...]

        def f(x1, gain, shift, scale, gate2, p_gu, p_yy):
            return _mlp_tile(x1, gain, shift, scale, gate2, w_gu_v, w_dn_v, t, p_gu, p_yy)
        zero_gu = jnp.zeros((bt, 2 * FFN), F32)
        zero_yy = jnp.zeros((bt, D), F32)
        loss, vjp, (h2, act) = jax.vjp(f, x1_ref[...], g_ref[...], mod_ref[:, 3 * D:4 * D], mod_ref[:, 4 * D:5 * D],
                                       mod_ref[:, 5 * D:6 * D], zero_gu, zero_yy, has_aux=True)
        dx1, dgain, dshift, dscale, dgate2, dgu, dyy = vjp(jnp.ones((), F32))
        dx1_ref[...] = dx1
        h2_ref[...] = h2.astype(BF)
        act_ref[...] = act.astype(BF)
        dgu_ref[...] = dgu.astype(BF)
        dyy_ref[...] = dyy.astype(BF)
        first = pl.program_id(1) == 0
        _acc(loss_ref, jnp.reshape(loss, (1, 1)), first)
        _acc(dmod_ref, jnp.concatenate([dshift, dscale, dgate2], axis=-1), first)
        _acc(dg_ref, dgain, first)

    return pl.pallas_call(
        body, name="mlp", grid=(B, S // bt),
        in_specs=[_tok(bt, D), _tok(bt, D), _per_batch(6 * D), _full((1, D)), _resident((N_CHIP, D, 2 * FFN // N_CHIP)),
                  _resident((FFN, D))],
        out_specs=[_tok(bt, D), _tok(bt, D), _tok(bt, FFN), _tok(bt, 2 * FFN), _tok(bt, D),
                   _per_batch(1), _per_batch(3 * D), _per_batch(D)],
        out_shape=[_sds((B, S, D), F32), _sds((B, S, D), BF), _sds((B, S, FFN), BF), _sds((B, S, 2 * FFN), BF),
                   _sds((B, S, D), BF), _sds((B, 1, 1), F32), _sds((B, 1, 3 * D), F32), _sds((B, 1, D), F32)],
        compiler_params=_cparams(dimension_semantics=("parallel", "arbitrary")),
    )(x1, tgt, mod, norm2_g, w_gu, w_dn)


def _mix_bwd(o_attn, o_raw, z, ga, gb, x, mod, dn_g, w_branch, w_out, dx1, bt, ex=None):
    B, S, _ = x.shape

    def body(oa_ref, or_ref, z_ref, ga_ref, gb_ref, x_ref, mod_ref, dg_ref, wb_ref, wo_ref, dx1_ref,
             doa_ref, dor_ref, dz_ref, dga_ref, dgb_ref, dya_ref, dyd_ref, dout_ref, dgate_ref, ddg_ref):
        oa, o_r, zs = _mix_load(oa_ref, or_ref, z_ref)
        wb_a, wb_d, wo = wb_ref[0:QW, :], wb_ref[QW:2 * QW, :], wo_ref[...]

        def f(oa, o_r, zs, ga, gb, gate1, dn_g, p_ya, p_yd, p_out):
            return _mix_tile(oa, o_r, zs, ga, gb, x_ref[...], gate1, dn_g, wb_a, wb_d, wo, p_ya, p_yd, p_out)[0]
        zero = jnp.zeros((bt, D), F32)
        _, vjp = jax.vjp(f, oa, o_r, zs, ga_ref[...].astype(F32), gb_ref[...].astype(F32), mod_ref[:, 2 * D:3 * D],
                         dg_ref[...], zero, zero, zero)
        doa, dor, dzs, dga, dgb, dgate1, ddn_g, dya, dyd, dout = vjp(dx1_ref[...])
        doa_ref[...] = doa
        for h in range(DN_H):
            dor_ref[:, h * DN_D:(h + 1) * DN_D] = dor[h]
            dz_ref[:, h * DN_D:(h + 1) * DN_D] = dzs[h].astype(BF)
        dga_ref[...] = dga.astype(BF)
        dgb_ref[...] = dgb.astype(BF)
        dya_ref[...] = dya.astype(BF)
        dyd_ref[...] = dyd.astype(BF)
        dout_ref[...] = dout.astype(BF)
        first = pl.program_id(1) == 0
        _acc(dgate_ref, dgate1, first)
        _acc(ddg_ref, ddn_g, first)

    return _hosted_call(
        body, "mix_bwd", (B, S // bt),
        in_specs=[_tok(bt, QW), _tok(bt, DNW), _tok(bt, DNW), _tok(bt, D), _tok(bt, D), _tok(bt, D), _per_batch(6 * D),
                  _full((1, DN_D)), _resident((D, D)), _resident((D, D)), _tok(bt, D)],
        out_specs=[_tok(bt, QW), _tok(bt, DNW), _tok(bt, DNW), _tok(bt, D), _tok(bt, D), _tok(bt, D), _tok(bt, D), _tok(bt, D),
                   _per_batch(D), _per_batch(DN_D)],
        out_shape=[_sds((B, S, QW), F32), _sds((B, S, DNW), F32), _sds((B, S, DNW), BF), _sds((B, S, D), BF),
                   _sds((B, S, D), BF), _sds((B, S, D), BF), _sds((B, S, D), BF), _sds((B, S, D), BF),
                   _sds((B, 1, D), F32), _sds((B, 1, DN_D), F32)],
        scratch_shapes=[], semantics=("parallel", "arbitrary"),
        ins=(o_attn, o_raw, z, ga, gb, x, mod, dn_g, w_branch, w_out, dx1), ex=ex)


def _dn_rec_bwd(u, w, qd, kd, a, cd, states, d_o, ex=None):
    B, S, _ = u.shape
    nc = S // CHUNK
    R = REC_GROUP if nc % REC_GROUP == 0 else 1
    tok, cd_spec, st_spec = _rec_specs(B, R, lambda i: nc // R - 1 - i)

    def body(u_ref, w_ref, qd_ref, kd_ref, a_ref, cd_ref, st_ref, do_ref,
             du_ref, dw_ref, dqd_ref, dkd_ref, da_ref, dcd_ref, ds_ref):
        @pl.when(pl.program_id(0) == 0)
        def _():
            ds_ref[...] = jnp.zeros_like(ds_ref)
        lane_row = _iota((1, DN_D), 1)
        d_state = ds_ref[...]
        for c in reversed(range(R)):
            state = st_ref[:, c].reshape(B * DN_H, DN_D, DN_D)
            _, vjp = jax.vjp(_dn_rec, state, *_rec_load(B, u_ref, w_ref, qd_ref, kd_ref, a_ref, cd_ref, c))
            d_state, du, dw, dqd, dkd, da, dcd = vjp((d_state, _rec_stack(do_ref, B, DN_D, c)))
            for ref, val, width in ((du_ref, du, DN_D), (dw_ref, dw, DN_D), (dqd_ref, dqd, DN_D), (dkd_ref, dkd, DN_D),
                                    (da_ref, da, CHUNK)):
                _rec_store(B, ref, val, width, c)
            for b in range(B):
                row = jnp.zeros((1, DN_D), F32)
                for h in range(DN_H):
                    row = row + jnp.where(lane_row == h, dcd[b * DN_H + h], 0.0)
                dcd_ref[b, c] = row
        ds_ref[...] = d_state

    return _hosted_call(
        body, "dn_rec_bwd", (nc // R,),
        in_specs=[tok(DNW)] * 4 + [tok(AW), cd_spec, st_spec, tok(DNW)],
        out_specs=[tok(DNW)] * 4 + [tok(AW), cd_spec],
        out_shape=[_sds((B, S, DNW), F32)] * 4 + [_sds((B, S, AW), F32), _sds((B, nc, 1, DN_D), F32)],
        scratch_shapes=[pltpu.VMEM((B * DN_H, DN_D, DN_D), F32)],
        semantics=("arbitrary",), ins=(u, w, qd, kd, a, cd, states, d_o), ex=ex)


def _dn_intra_bwd(qkv, bg, tinv, du, dw, dqd, dkd, da, dcd, ex=None):
    B, S, _ = qkv.shape
    nc = S // CHUNK
    G = min(DN_GROUP, nc)
    rows = G * CHUNK

    def body(qkv_ref, bg_ref, t_ref, du_ref, dw_ref, dqd_ref, dkd_ref, da_ref, dcd_ref, dqkv_ref, dbg_ref):
        q, k, v = _dn_load_stack(qkv_ref, G)
        known = _stack_heads(t_ref, G, 0, CHUNK)
        _, vjp = jax.vjp(lambda q, k, v, bg: _dn_intra(q, k, v, bg, known)[:6], q, k, v,
                         bg_ref[...].reshape(G, CHUNK, DN_D))
        lane_row = _iota((1, DN_D), 1)
        dcd = jnp.stack([jnp.sum(jnp.where(lane_row == h, dcd_ref[g], 0.0), axis=-1, keepdims=True)
                         for g in range(G) for h in range(DN_H)])
        dq, dk, dv, dbg = vjp((_stack_heads(du_ref, G, 0, DN_D), _stack_heads(dw_ref, G, 0, DN_D),
                               _stack_heads(dqd_ref, G, 0, DN_D), _stack_heads(dkd_ref, G, 0, DN_D),
                               _stack_heads(da_ref, G, 0, CHUNK), dcd))
        for g in range(G):
            rows = slice(g * CHUNK, (g + 1) * CHUNK)
            for h in range(DN_H):
                n = g * DN_H + h
                dqkv_ref[rows, h * DN_D:(h + 1) * DN_D] = dq[n]
                dqkv_ref[rows, DNW + h * DN_D:DNW + (h + 1) * DN_D] = dk[n]
                dqkv_ref[rows, 2 * DNW + h * DN_D:2 * DNW + (h + 1) * DN_D] = dv[n]
        dbg_ref[...] = dbg.reshape(G * CHUNK, DN_D)

    return _hosted_call(
        body, "dn_intra_bwd", (B, nc // G),
        in_specs=[_tok(rows, CONVW), _tok(rows, 128), _tok(rows, AW)] + [_tok(rows, DNW)] * 4 + [_tok(rows, AW), _cd_spec(G)],
        out_specs=[_tok(rows, CONVW), _tok(rows, 128)],
        out_shape=[_sds((B, S, CONVW), F32), _sds((B, S, 128), F32)],
        scratch_shapes=[], semantics=("parallel", "parallel"), ins=(qkv, bg, tinv, du, dw, dqd, dkd, da, dcd), ex=ex)


def _dn_prep_bwd(dn, y, ba, conv_w, alog, dtb, dqkv, dbg, bt, ex=None):
    B, S, _ = dn.shape
    nt = S // bt
    strip_rows = min(bt, 64)

    def rev(f):
        return pl.BlockSpec((None, bt, f), lambda b, i: (b, nt - 1 - i, 0))

    def body(x_ref, y_ref, ba_ref, cw_ref, al_ref, dt_ref, dqkv_ref, dbg_ref,
             dx_ref, dba_ref, dcw_ref, dal_ref, ddt_ref, dye_ref):
        i = pl.program_id(1)
        @pl.when(i == 0)
        def _():
            dye_ref[bt:bt + 8, :] = jnp.zeros((8, CONVW), F32)

        @pl.when(i > 0)
        def _():
            dye_ref[bt:bt + 8, :] = dye_ref[0:8, :]

        n_strips = bt // strip_rows

        def strip(k, carry):
            dal, ddt, dcw = carry
            r0 = pl.multiple_of((n_strips - 1 - k) * strip_rows, strip_rows)
            rows = pl.ds(r0, strip_rows)
            dcw_slabs = []
            for j in range(3 * DN_H):
                cols = slice(j * DN_D, (j + 1) * DN_D)
                _, vjp = jax.vjp(functools.partial(_dn_act, normalize=j < 2 * DN_H), y_ref[rows, cols].astype(F32))
                (dye_ref[rows, cols],) = vjp(dqkv_ref[rows, cols])
                window = dye_ref[pl.ds(r0, strip_rows + 8), cols]
                shifted = [window[3 - t:strip_rows + 3 - t] for t in range(CONV)]
                dx = cw_ref[0:1, cols] * shifted[0]
                for t in range(1, CONV):
                    dx = dx + cw_ref[t:t + 1, cols] * shifted[t]
                dx_ref[rows, cols] = dx.astype(BF)
                x = x_ref[rows, cols].astype(F32)
                dcw_slabs.append(jnp.concatenate([jnp.sum(shifted[t] * x, axis=0, keepdims=True) for t in range(CONV)], axis=0))
            _, vjp = jax.vjp(_dn_gates, ba_ref[rows, :], al_ref[...], dt_ref[...])
            dba_ref[rows, :], da, dd = vjp(dbg_ref[rows, :])
            return dal + da, ddt + dd, dcw + jnp.concatenate(dcw_slabs, axis=1)
        zero = jnp.zeros((1, DN_D), F32)
        dal, ddt, dcw = lax.fori_loop(0, n_strips, strip, (zero, zero, jnp.zeros((CONV, CONVW), F32)))
        first = (i == 0) & (pl.program_id(0) == 0)
        _acc(dcw_ref, dcw, first)
        _acc(dal_ref, dal, first)
        _acc(ddt_ref, ddt, first)

    return _hosted_call(
        body, "dn_prep_bwd", (B, nt),
        in_specs=[rev(CONVW), rev(CONVW), rev(128), _full((CONV, CONVW)), _full((1, 128)), _full((1, 128)), rev(CONVW), rev(128)],
        out_specs=[rev(CONVW), rev(128), _full((CONV, CONVW)), _full((1, 128)), _full((1, 128))],
        out_shape=[_sds((B, S, CONVW), BF), _sds((B, S, 128), F32), _sds((CONV, CONVW), F32), _sds((1, 128), F32),
                   _sds((1, 128), F32)],
        scratch_shapes=[pltpu.VMEM((bt + 8, CONVW), F32)],
        semantics=("arbitrary", "arbitrary"), ins=(dn, y, ba, conv_w, alog, dtb, dqkv, dbg), ex=ex)


def _attn_bwd(qkv, sinks, d_o, ex=None):
    B, S, _ = qkv.shape
    nb = S // BLK

    def cur(f):
        return pl.BlockSpec((None, BLK, f), lambda b, i: (b, jnp.minimum(i, nb - 1), 0))

    def out_prev(f):
        return pl.BlockSpec((None, BLK, f), lambda b, i: (b, jnp.maximum(i - 1, 0), 0))

    def body(qkv_ref, kvp_ref, sk_ref, do_ref, dq_ref, dkv_ref, dsk_ref, carry_ref):
        n = pl.program_id(1)
        first = (n == 0) & (pl.program_id(0) == 0)

        @pl.when(n == 0)
        def _():
            carry_ref[...] = jnp.zeros_like(carry_ref)

        @pl.when(n < nb)
        def _():
            qs, kc, kp, vc, vp = _attn_load(qkv_ref, kvp_ref)

            def f(qs, kc, kp, vc, vp, sk):
                return _attn_block(qs, kc, kp, vc, vp, sk, n > 0)
            _, vjp = jax.vjp(f, qs, kc, kp, vc, vp, sk_ref[...])
            d_outs = [do_ref[:, h * HD:(h + 1) * HD] for h in range(HEADS)]
            dqs, dkc, dkp, dvc, dvp, dsk = vjp(d_outs)
            for h in range(HEADS):
                dq_ref[:, h * HD:(h + 1) * HD] = dqs[h].astype(BF)
            for h in range(KV_HEADS):
                ksl = slice(h * HD, (h + 1) * HD)
                vsl = slice(KVW + h * HD, KVW + (h + 1) * HD)
                dkv_ref[:, ksl] = (carry_ref[:, ksl] + dkp[h]).astype(BF)
                dkv_ref[:, vsl] = (carry_ref[:, vsl] + dvp[h]).astype(BF)
                carry_ref[:, ksl] = dkc[h]
                carry_ref[:, vsl] = dvc[h]
            _acc(dsk_ref, dsk, first)

        @pl.when(n == nb)
        def _():
            dkv_ref[...] = carry_ref[...].astype(BF)

    return _hosted_call(
        body, "attn_bwd", (B, nb + 1),
        in_specs=[cur(QKV), _kv_prev_spec(lambda i: jnp.maximum(jnp.minimum(i, nb - 1) - 1, 0)), _full((1, HEADS)), cur(QW)],
        out_specs=[cur(QW), out_prev(2 * KVW), _full((1, HEADS))],
        out_shape=[_sds((B, S, QW), BF), _sds((B, S, 2 * KVW), BF), _sds((1, HEADS), F32)],
        scratch_shapes=[pltpu.VMEM((BLK, 2 * KVW), F32)],
        semantics=("arbitrary", "arbitrary"), ins=(qkv, qkv, sinks, d_o), ex=ex)


def _in_proj_bwd(x, mod, norm1_g, w_in, pieces, dba, dx1, bt):
    B, S, _ = x.shape
    widths = (QW, 2 * KVW, CONVW, DNW, D, D)

    def body(x_ref, mod_ref, g_ref, w_ref, dq_ref, dkv_ref, ddn_ref, dz_ref, dga_ref, dgb_ref, dba_ref, dx1_ref,
             gx_ref, dp_ref, dmod_ref, dg_ref):
        dp = jnp.concatenate([r[...] for r in (dq_ref, dkv_ref, ddn_ref, dz_ref, dga_ref, dgb_ref)]
                             + [dba_ref[...].astype(BF)], axis=-1)
        dp_ref[...] = dp
        dh = lax.dot_general(dp, w_ref[...], (((1,), (1,)), ((), ())), preferred_element_type=F32)
        _, vjp = jax.vjp(_norm_mod, x_ref[...], g_ref[...], mod_ref[:, 0:D], mod_ref[:, D:2 * D])
        dx, dgain, dshift, dscale = vjp(dh)
        gx_ref[...] = dx + dx1_ref[...]
        first = pl.program_id(1) == 0
        _acc(dmod_ref, jnp.concatenate([dshift, dscale], axis=-1), first)
        _acc(dg_ref, dgain, first)

    return pl.pallas_call(
        body, name="in_proj_bwd", grid=(B, S // bt),
        in_specs=[_tok(bt, D), _per_batch(6 * D), _full((1, D)), _resident((D, IN_PAD))] + [_tok(bt, w) for w in widths]
        + [_tok(bt, 128), _tok(bt, D)],
        out_specs=[_tok(bt, D), _tok(bt, IN_PAD), _per_batch(2 * D), _per_batch(D)],
        out_shape=[_sds((B, S, D), F32), _sds((B, S, IN_PAD), BF), _sds((B, 1, 2 * D), F32), _sds((B, 1, D), F32)],
        compiler_params=_cparams(dimension_semantics=("parallel", "arbitrary")),
    )(x, mod, norm1_g, w_in, *pieces, dba, dx1)


def _matmul_tn(tag, a, b, bk, bn, bt, col_blocks=False, ex=None):
    T, K = a.shape
    N = b.shape[1]
    nt = T // bt
    if col_blocks:
        assert bk == K
        out_spec = pl.BlockSpec((None, bk, bn), lambda i, j, t: (j, 0, 0))
        out_shape = _sds((N // bn, K, bn), F32)
    else:
        out_spec = pl.BlockSpec((bk, bn), lambda i, j, t: (i, j))
        out_shape = _sds((K, N), F32)

    def body(a_ref, b_ref, o_ref, acc_ref):
        t = pl.program_id(2)

        @pl.when(t == 0)
        def _():
            acc_ref[...] = jnp.zeros_like(acc_ref)
        acc_ref[...] += lax.dot_general(a_ref[...], b_ref[...], (((0,), (0,)), ((), ())), preferred_element_type=F32)

        @pl.when(t == nt - 1)
        def _():
            o_ref[...] = acc_ref[...]

    (out,), landed = _hosted_call(
        body, f"grad_{tag}", (K // bk, N // bn, nt),
        in_specs=[pl.BlockSpec((bt, bk), lambda i, j, t: (t, i)), pl.BlockSpec((bt, bn), lambda i, j, t: (t, j))],
        out_specs=[out_spec], out_shape=[out_shape],
        scratch_shapes=[pltpu.VMEM((bk, bn), F32)],
        semantics=("parallel", "parallel", "arbitrary"), ins=(a, b), ex=ex)
    return out if ex is None else (out, landed)


def _rope_table(positions):
    inv_freq = THETA ** (-jnp.arange(0, ROT, 2, dtype=F32) / ROT)
    rest = jnp.zeros((HD - ROT,), F32)
    freq = jnp.concatenate([inv_freq, inv_freq, rest] * 2)
    sign = jnp.concatenate([-jnp.ones_like(inv_freq), jnp.ones_like(inv_freq), rest] * 2)
    ang = positions.astype(F32)[..., None] * freq
    return jnp.cos(ang), jnp.sin(ang) * sign


def _lane_pad(v, offset, width=128):
    return jnp.zeros((1, width), F32).at[0, offset:offset + v.shape[-1]].set(v.reshape(-1))


def _tile(S, want):
    return min(S, want)


class _Hosted:
    def __init__(self, call):
        self.call = call
        self.outs = None

    def __call__(self, ex):
        self.outs, landed = self.call(ex)
        return landed


def _local_step(x, mod, positions, tgt, norm1_g, w_in_pad, conv_w, q_norm_g, k_norm_g, sinks, a_log, dt_bias,
                dn_norm_g, w_branch, w_out, norm2_g, w_gu, w_dn, dist=None):
    B, S, _ = x.shape
    T = B * S
    cos_t, sin_t = _rope_table(positions)
    qg2 = jnp.concatenate([q_norm_g, q_norm_g], axis=-1)
    kg2 = jnp.concatenate([k_norm_g, k_norm_g], axis=-1)
    alog = _lane_pad(a_log, DN_H)
    dtb = _lane_pad(dt_bias, DN_H)
    conv2 = conv_w.reshape(CONV, CONVW)
    bt = _tile(S, 512)
    bt_mlp = _tile(S, 256)

    q, kv, dn, z, ga, gb, ba, h1 = _in_proj(x, mod, norm1_g, w_in_pad, bt)
    qkv_n = _qk_prep_fwd(q, kv, cos_t, sin_t, qg2, kg2, bt)
    o_attn = _attn_fwd(qkv_n, sinks)
    dqkv, bg, dn_y = _dn_prep(dn, ba, conv2, alog, dtb, bt)
    intra = _Hosted(lambda ex: _dn_intra_fwd(dqkv, bg, ex))
    if dist is None:
        intra(None)
    else:
        f_br, f_out, w_gu, f_dn = _gather_weights("late", [w_branch, w_out, w_gu, w_dn], dist[0], host=intra)
        w_branch, w_out, w_dn = (f.reshape(N_CHIP * f.shape[1], f.shape[2]) for f in (f_br, f_out, f_dn))
    dn_u, dn_w, dn_qd, dn_kd, dn_a, dn_cd, dn_tinv = intra.outs
    (o_raw, states), _ = _dn_rec_fwd(dn_u, dn_w, dn_qd, dn_kd, dn_a, dn_cd)
    x1, o_dn, merged = _mix_fwd(o_attn, o_raw, z, ga, gb, x, mod, dn_norm_g, w_branch, w_out, bt)
    dx1, h2, act, dgu, dyy, loss, dmod2, dnorm2 = _mlp(x1, tgt, mod, norm2_g, w_gu, w_dn, bt_mlp)

    def flat(t):
        return t.reshape(T, t.shape[-1])
    tn = functools.partial(_matmul_tn, bt=_tile(T, 1024))
    g_w_dn = tn("w_down", flat(act), flat(dyy), bk=FFN, bn=D // 2)
    g_w_gu = tn("w_gate_up", flat(h2), flat(dgu), bk=D, bn=2 * FFN // N_CHIP, col_blocks=True)

    mix_b = _Hosted(lambda ex: _mix_bwd(o_attn, o_raw, z, ga, gb, x, mod, dn_norm_g, w_branch, w_out, dx1, bt_mlp, ex))
    rec_b = _Hosted(lambda ex: _dn_rec_bwd(dn_u, dn_w, dn_qd, dn_kd, dn_a, dn_cd, states, mix_b.outs[1], ex))
    intra_b = _Hosted(lambda ex: _dn_intra_bwd(dqkv, bg, dn_tinv, *rec_b.outs, ex))
    if dist is None:
        for host in (mix_b, rec_b, intra_b):
            host(None)
    else:
        g_w_gu, g_w_dn = _reduce_grads(("w_gate_up", "w_down"), [g_w_gu, g_w_dn.reshape(N_CHIP, -1, D)], *dist,
                                       hosts=[mix_b, rec_b, intra_b])
    d_oa, _, dz, dga, dgb, dya, dyd, dout, dgate1, ddn_g = mix_b.outs
    d_dqkv, dbg = intra_b.outs
    g_w_out = tn("w_out", flat(merged), flat(dout), bk=D, bn=D)
    g_w_br = jnp.concatenate([tn("w_branch_attn", flat(o_attn), flat(dya), bk=QW, bn=D),
                              tn("w_branch_dn", flat(o_dn), flat(dyd), bk=DNW, bn=D)], axis=0)
    prep_b = _Hosted(lambda ex: _dn_prep_bwd(dn, dn_y, ba, conv2, alog, dtb, d_dqkv, dbg, bt, ex))
    attn_b = _Hosted(lambda ex: _attn_bwd(qkv_n, sinks, d_oa, ex))
    qk_b = _Hosted(lambda ex: _qk_prep_bwd(q, kv, cos_t, sin_t, qg2, kg2, *attn_b.outs[:2], bt, ex))
    if dist is None:
        for host in (prep_b, attn_b, qk_b):
            host(None)
    else:
        g_w_br, g_w_out = _reduce_grads(("w_branch", "w_out"), [g_w_br.reshape(N_CHIP, -1, D), g_w_out.reshape(N_CHIP, -1, D)],
                                        *dist, hosts=[prep_b, attn_b, qk_b])
    d_dn, dba, dconv, dalog, ddtb = prep_b.outs
    dsk = attn_b.outs[2]
    dq, dkv, dqg2, dkg2 = qk_b.outs
    dqg = dqg2[:, :HD] + dqg2[:, HD:]
    dkg = dkg2[:, :HD] + dkg2[:, HD:]
    grad_x, dproj, dmod1, dnorm1 = _in_proj_bwd(x, mod, norm1_g, w_in_pad, (dq, dkv, d_dn, dz, dga, dgb), dba, dx1, bt)
    g_w_in = _Hosted(lambda ex: (tn("w_in", flat(h1), flat(dproj), bk=D, bn=IN_PAD // 3), ()) if ex is None
                     else tn("w_in", flat(h1), flat(dproj), bk=D, bn=IN_PAD // 3, ex=ex))
    if dist is None:
        g_w_in(None)
        g_w_in = g_w_in.outs

    dmod = jnp.concatenate([dmod1, dgate1, dmod2], axis=-1)
    small = dict(norm1_g=jnp.sum(dnorm1, axis=0), norm2_g=jnp.sum(dnorm2, axis=0), q_norm_g=dqg, k_norm_g=dkg,
                 sinks=dsk, a_log=dalog[:, DN_H:2 * DN_H], dt_bias=ddtb[:, DN_H:2 * DN_H],
                 dn_norm_g=jnp.sum(ddn_g, axis=0), conv_w=dconv)
    return jnp.sum(loss), grad_x, dmod, small, (g_w_in, g_w_br, g_w_out, g_w_gu, g_w_dn)


def _flip(me, f):
    return (me[0] ^ ((f >> 2) & 1), me[1] ^ ((f >> 1) & 1), me[2] ^ (f & 1))


def _comm_call(name, ex):
    n_in, n_out = len(ex.ins), len(ex.out_shapes)

    def body(*refs):
        out_refs, sems = refs[n_in:n_in + n_out], refs[n_in + n_out:]
        cps = _exchange_copies(ex, refs[:n_in], out_refs, sems[0], sems[1])
        for cp in cps:
            cp.start()
        for cp in cps:
            cp.wait_recv()
        if ex.n_forward:
            fwd = _forward_copies(ex, out_refs, sems[2], sems[3])
            for cp in fwd:
                cp.start()
            for cp in fwd:
                cp.wait_recv()
            cps = cps + fwd
        for cp in cps:
            cp.wait_send()

    any_spec = pl.BlockSpec(memory_space=pl.ANY)
    return pl.pallas_call(
        body, name=name, in_specs=[any_spec] * n_in, out_specs=[any_spec] * n_out, out_shape=list(ex.out_shapes),
        scratch_shapes=_exchange_sems(ex),
    )(*ex.ins)


def _by_origin(own, received, index):
    stack = jnp.concatenate([own[None], received], axis=0)
    n = stack.shape[0]
    return jnp.stack([lax.dynamic_index_in_dim(stack, k ^ index, 0, keepdims=False) for k in range(n)])


def _gather_devices(name, arrs, dev, host=None):
    def plan(me, in_refs, out_refs):
        return [(a, o.at[f - 1], _flip(me, f)) for a, o in zip(in_refs, out_refs) for f in range(1, N_DEV)]
    outs = tuple(_sds((N_DEV - 1,) + a.shape, a.dtype) for a in arrs)
    got = (host or functools.partial(_comm_call, name))(_Exchange(tuple(arrs), outs, (N_DEV - 1) * len(arrs), plan))
    return [_by_origin(a, g, dev) for a, g in zip(arrs, got)]


def _gather_chips(name, arrs, chip):
    def plan(me, in_refs, out_refs):
        return [(a, o.at[j], _flip(me, 2 * (j + 1))) for a, o in zip(in_refs, out_refs) for j in range(N_CHIP - 1)]
    outs = tuple(_sds((N_CHIP - 1,) + a.shape, a.dtype) for a in arrs)
    got = _comm_call(name, _Exchange(tuple(arrs), outs, (N_CHIP - 1) * len(arrs), plan))
    return [_by_origin(a, g, chip) for a, g in zip(arrs, got)]


def _halves(core, mine, other):
    lo = jnp.where(core == 0, mine, other)
    hi = jnp.where(core == 0, other, mine)
    return jnp.concatenate([lo, hi], axis=-2)


def _swap_cores_ex(arrs):
    def plan(me, in_refs, out_refs):
        return [(g, o, _flip(me, 1)) for g, o in zip(in_refs, out_refs)]
    return _Exchange(tuple(arrs), tuple(_sds(g.shape, g.dtype) for g in arrs), len(arrs), plan)


def _gather_weights(tag, shards, chip, host=None):
    def plan(me, in_refs, out_refs):
        chip_me = 2 * me[0] + me[1]
        remote = []
        for a, o in zip(in_refs, out_refs):
            half = a.shape[0] // 2
            mine = a.at[pl.ds(me[2] * half, half)]
            remote += [(mine, o.at[chip_me, me[2]], _flip(me, 2 * (j + 1))) for j in range(N_CHIP - 1)]
        return remote

    def forward(me, out_refs):
        chip_me = 2 * me[0] + me[1]
        return [(o.at[chip_me ^ (j + 1), me[2]], o.at[chip_me ^ (j + 1), me[2]], _flip(me, 1))
                for o in out_refs for j in range(N_CHIP - 1)]
    run = host or functools.partial(_comm_call, f"weights_{tag}")
    n = (N_CHIP - 1) * len(shards)
    landed = run(_Exchange(tuple(shards), tuple(_sds((N_CHIP, 2, a.shape[0] // 2, a.shape[1]), a.dtype) for a in shards),
                           n, plan, n, forward))
    return [lax.dynamic_update_slice(f.reshape((N_CHIP,) + a.shape), a[None], (chip, 0, 0)) for a, f in zip(shards, landed)]


def _rows(r):
    for br in (512, 352, 256, 128, 64, 32, 16, 8):
        if r % br == 0:
            return br
    raise ValueError(r)


def _pair_add(tag, g, recv, c):
    n, r, cols = g.shape
    half = r // 2
    br = _rows(half)
    nb = half // br

    def body(c_ref, g_ref, r_ref, o_ref):
        o_ref[...] = (g_ref[...] + r_ref[...]).astype(BF)

    return pl.pallas_call(
        body, name=f"pair_add_{tag}",
        grid_spec=pltpu.PrefetchScalarGridSpec(
            num_scalar_prefetch=1, grid=(n, nb),
            in_specs=[pl.BlockSpec((None, br, cols), lambda k, i, c_ref: (k, c_ref[0] * nb + i, 0)),
                      pl.BlockSpec((None, br, cols), lambda k, i, c_ref: (k, i, 0))],
            out_specs=pl.BlockSpec((None, br, cols), lambda k, i, c_ref: (k, i, 0))),
        out_shape=_sds((n, half, cols), BF),
        compiler_params=_cparams(dimension_semantics=("parallel", "parallel")),
    )(c, g, recv)


def _sum_chips(tag, p, q, chip):
    n, r, cols = q.shape
    br = _rows(r)

    def body(chip_ref, p_ref, q_ref, o_ref):
        acc = p_ref[...].astype(F32)
        for k in range(n):
            acc = acc + q_ref[k].astype(F32)
        o_ref[...] = acc

    return pl.pallas_call(
        body, name=f"sum_chips_{tag}",
        grid_spec=pltpu.PrefetchScalarGridSpec(
            num_scalar_prefetch=1, grid=(r // br,),
            in_specs=[pl.BlockSpec((None, br, cols), lambda i, chip_ref: (chip_ref[0], i, 0)),
                      pl.BlockSpec((n, br, cols), lambda i, chip_ref: (0, i, 0))],
            out_specs=pl.BlockSpec((br, cols), lambda i, chip_ref: (i, 0))),
        out_shape=_sds((r, cols), F32),
        compiler_params=_cparams(dimension_semantics=("parallel",)),
    )(chip, p, q)


def _reduce_grads(tags, grads, chip, core, hosts=None):
    core_arr = core.reshape(1).astype(jnp.int32)
    chip_arr = chip.reshape(1).astype(jnp.int32)
    name = "_".join(tags)
    run = hosts or [functools.partial(_comm_call, f"grads_{stage}_{name}") for stage in ("pair", "chips", "swap")]

    def plan_pair(me, in_refs, out_refs):
        remote = []
        for g, o in zip(in_refs, out_refs):
            half = g.shape[1] // 2
            remote += [(g.at[k, pl.ds((1 - me[2]) * half, half)], o.at[k], _flip(me, 1)) for k in range(N_CHIP)]
        return remote
    recv = run[0](_Exchange(tuple(grads), tuple(_sds((N_CHIP, g.shape[1] // 2, g.shape[2]), F32) for g in grads),
                            N_CHIP * len(grads), plan_pair))
    pair = [_pair_add(t, g, r, core_arr) for t, g, r in zip(tags, grads, recv)]

    def plan_chips(me, in_refs, out_refs):
        remote = []
        for p, o in zip(in_refs, out_refs):
            for j in range(N_CHIP - 1):
                peer = _flip(me, 2 * (j + 1))
                remote.append((p.at[2 * peer[0] + peer[1]], o.at[j], peer))
        return remote
    parts = run[1](_Exchange(tuple(pair), tuple(_sds((N_CHIP - 1,) + p.shape[1:], BF) for p in pair),
                             (N_CHIP - 1) * len(pair), plan_chips))
    mine = [_sum_chips(t, p, q, chip_arr) for t, p, q in zip(tags, pair, parts)]
    other = run[2](_swap_cores_ex(mine))
    return [_halves(core, h, o) for h, o in zip(mine, other)]


def _adamw_math(w, g, m, v):
    m = ADAM_B1 * m + (1.0 - ADAM_B1) * g
    v = ADAM_B2 * v + (1.0 - ADAM_B2) * (g * g)
    m_hat = m / (1.0 - ADAM_B1 ** ADAM_STEP)
    v_hat = v / (1.0 - ADAM_B2 ** ADAM_STEP)
    delta = -ADAM_LR * (m_hat / (jnp.sqrt(v_hat) + ADAM_EPS) + ADAM_WD * w)
    return delta, m, v


def _adamw(name, w, g, m, v, ex=None):
    r, cols = w.shape
    br = _rows(r)
    if br * cols * 4 > (1 << 20) and br % 16 == 0:
        br //= 2

    def body(w_ref, g_ref, m_ref, v_ref, d_ref, mo_ref, vo_ref):
        d_ref[...], mo_ref[...], vo_ref[...] = _adamw_math(w_ref[...], g_ref[...], m_ref[...], v_ref[...])

    spec = pl.BlockSpec((br, cols), lambda i: (i, 0))
    outs, landed = _hosted_call(
        body, f"adamw_{name}", (r // br,), in_specs=[spec] * 4, out_specs=[spec] * 3,
        out_shape=[_sds((r, cols), F32)] * 3, scratch_shapes=[], semantics=("parallel",), ins=(w, g, m, v), ex=ex)
    return outs if ex is None else (outs, landed)


def _ada_fwd(c_all, ada_w, ada_b_cols):
    n = c_all.shape[0]

    def body(c_ref, w_ref, b_ref, o_ref):
        o_ref[...] = _mmx(_silu(c_ref[...]), w_ref[...]) + b_ref[...]

    return pl.pallas_call(
        body, name="ada_fwd", out_shape=_sds((n, ada_w.shape[1]), F32), compiler_params=_cparams(),
    )(c_all, ada_w, ada_b_cols)


def _ada_bwd(c_all, dmod_cols, w, m, v, ex=None):
    n = c_all.shape[0]
    r, cols = w.shape
    br = 128

    def body(c_ref, d_ref, w_ref, m_ref, v_ref, g_ref, dl_ref, mo_ref, vo_ref):
        cond = _silu(c_ref[...])
        g = lax.dot_general(cond, d_ref[...], (((0,), (0,)), ((), ())), precision=lax.Precision.HIGHEST,
                            preferred_element_type=F32)
        g_ref[...] = g
        dl_ref[...], mo_ref[...], vo_ref[...] = _adamw_math(w_ref[...], g, m_ref[...], v_ref[...])

    spec = pl.BlockSpec((br, cols), lambda i: (i, 0))
    outs, landed = _hosted_call(
        body, "ada_bwd", (r // br,),
        in_specs=[pl.BlockSpec((n, br), lambda i: (0, i)), pl.BlockSpec((n, cols), lambda i: (0, 0)), spec, spec, spec],
        out_specs=[spec] * 4, out_shape=[_sds((r, cols), F32)] * 4, scratch_shapes=[], semantics=("parallel",),
        ins=(c_all, dmod_cols, w, m, v), ex=ex)
    return outs if ex is None else (outs, landed)


def _sum_devices(parts):
    n, r, cols = parts.shape

    def body(p_ref, o_ref):
        acc = p_ref[0]
        for k in range(1, n):
            acc = acc + p_ref[k]
        o_ref[...] = acc

    return pl.pallas_call(body, name="sum_devices", out_shape=_sds((r, cols), F32), compiler_params=_cparams())(parts)


SMALL_ROWS = 16
_SMALL_SLOTS = dict(norm1_g=(0, 0, D), norm2_g=(1, 0, D), q_norm_g=(2, 0, HD), k_norm_g=(2, 128, HD), sinks=(2, 256, HEADS),
                    a_log=(2, 384, DN_H), dt_bias=(2, 512, DN_H), dn_norm_g=(2, 640, DN_D))
_CONV_ROW = 4
_ADA_B_ROW = 8


def _pack_small(vals, conv, ada_b):
    def row(pieces):
        out, at = [], 0
        for col, val in pieces:
            out += [jnp.zeros((1, col - at), F32), val.reshape(1, -1)]
            at = col + val.size
        return jnp.concatenate(out + [jnp.zeros((1, CONVW - at), F32)], axis=1)
    rows = {}
    for name, (r, col, n) in _SMALL_SLOTS.items():
        rows.setdefault(r, []).append((col, vals[name]))
    blank = jnp.zeros((1, CONVW), F32)
    top = [row(sorted(rows[r], key=lambda p: p[0])) if r in rows else blank for r in range(_CONV_ROW)]
    conv_rows = jnp.concatenate([conv, jnp.zeros((CONV, CONVW - conv.shape[1]), F32)], axis=1)
    tail = jnp.zeros((SMALL_ROWS - _ADA_B_ROW - 4, CONVW), F32)
    return jnp.concatenate(top + [conv_rows, ada_b.reshape(4, CONVW), tail], axis=0)


def _unpack_small(sheet, conv_cols):
    out = {name: sheet[row, col:col + n].reshape(1, n) for name, (row, col, n) in _SMALL_SLOTS.items()}
    out["conv_w"] = sheet[_CONV_ROW:_CONV_ROW + CONV, 0:conv_cols].reshape(1, CONV, 1, conv_cols)
    out["ada_b"] = sheet[_ADA_B_ROW:_ADA_B_ROW + 4, :].reshape(1, 6 * D)
    return out


def _w_in_segments():
    shard = IN_WIDTH // N_CHIP
    cuts = sorted({0, IN_WIDTH, C_Z, C_Z + 2 * DN_H} | {k * shard for k in range(1, N_CHIP)})
    segs = []
    for a, b in zip(cuts[:-1], cuts[1:]):
        k = a // shard
        pad = a if a < C_Z else (C_BA + a - C_Z if a < C_Z + 2 * DN_H else a - 2 * DN_H)
        segs.append((k, a - k * shard, b - k * shard, pad))
    return segs


def _pad_w_in(f):
    parts = [f[k][:, lo:hi] for k, lo, hi, _ in sorted(_w_in_segments(), key=lambda s: s[3])]
    return jnp.concatenate(parts + [jnp.zeros((f.shape[1], IN_PAD - IN_WIDTH), f.dtype)], axis=1)


def _unpad_w_in(g):
    return jnp.stack([jnp.concatenate([g[:, pad:pad + hi - lo] for kk, lo, hi, pad in _w_in_segments() if kk == k], axis=1)
                      for k in range(N_CHIP)])


def _blocks_to_cols(f):
    return f.transpose(1, 0, 2).reshape(f.shape[1], N_CHIP * f.shape[2])


def kernel(x, c, positions, ada_w, ada_b, norm1_g, w_in, conv_w, q_norm_g, k_norm_g, sinks, a_log, dt_bias, dn_norm_g, w_branch, w_out, norm2_g, w_gate_up, w_down, loss_target, m_ada_w, m_ada_b, m_norm1_g, m_w_in, m_conv_w, m_q_norm_g, m_k_norm_g, m_sinks, m_a_log, m_dt_bias, m_dn_norm_g, m_w_branch, m_w_out, m_norm2_g, m_w_gate_up, m_w_down, v_ada_w, v_ada_b, v_norm1_g, v_w_in, v_conv_w, v_q_norm_g, v_k_norm_g, v_sinks, v_a_log, v_dt_bias, v_dn_norm_g, v_w_branch, v_w_out, v_norm2_g, v_w_gate_up, v_w_down):
    ix, iy, ic = lax.axis_index("x"), lax.axis_index("y"), lax.axis_index("c")
    dev = 4 * ix + 2 * iy + ic
    chip = 2 * ix + iy
    n_seq = x.shape[0]
    conv_cols = conv_w.shape[-1]

    c_all, conv_all = _gather_devices("gather_cond", [c, conv_w.reshape(CONV, conv_cols)], dev)
    c_all = c_all.reshape(N_DEV * n_seq, D)
    ada_cols = ada_w.shape[-1]
    ada_b_cols = lax.dynamic_slice(ada_b, (0, chip * ada_cols), (1, ada_cols))
    mod_cols = _ada_fwd(c_all, ada_w[0], ada_b_cols)
    (mod_blocks,) = _gather_chips("gather_mod", [mod_cols], chip)
    mod_all = _blocks_to_cols(mod_blocks)
    mod = lax.dynamic_slice(mod_all, (dev * n_seq, 0), (n_seq, 6 * D)).reshape(n_seq, 1, 6 * D)
    conv_full = _blocks_to_cols(conv_all[0::2])

    (f_in,) = _gather_weights("w_in", [w_in[0].astype(BF)], chip)
    w_in_pad = _pad_w_in(f_in)

    loss, grad_x, dmod, small, (w_in_grad, r_br, r_out, r_gu, r_dn) = _local_step(
        x, mod, positions, loss_target, norm1_g, w_in_pad, conv_full.reshape(CONV, 1, CONVW), q_norm_g, k_norm_g, sinks,
        a_log, dt_bias, dn_norm_g, w_branch[0].astype(BF), w_out[0].astype(BF), norm2_g, w_gate_up[0].astype(BF),
        w_down[0].astype(BF), dist=(chip, ic))
    loss = lax.psum(loss, ("x", "y", "c"))

    part = _pack_small(small, small["conv_w"], jnp.sum(dmod, axis=(0, 1)).reshape(1, 6 * D))
    dmod_all, parts = _gather_devices("gather_small", [dmod.reshape(n_seq, 6 * D), part], dev, host=w_in_grad)
    dmod_all = dmod_all.reshape(N_DEV * n_seq, 6 * D)
    dmod_cols = lax.dynamic_slice(dmod_all, (0, chip * ada_cols), (N_DEV * n_seq, ada_cols))

    up_gu = _Hosted(lambda ex: _adamw("w_gate_up", w_gate_up[0], r_gu, m_w_gate_up[0], v_w_gate_up[0], ex))
    up_ada = _Hosted(lambda ex: _ada_bwd(c_all, dmod_cols, ada_w[0], m_ada_w[0], v_ada_w[0], ex))
    up_dn = _Hosted(lambda ex: _adamw("w_down", w_down[0], r_dn, m_w_down[0], v_w_down[0], ex))
    (r_in,) = _reduce_grads(("w_in",), [_unpad_w_in(w_in_grad.outs)], chip, ic, hosts=[up_gu, up_ada, up_dn])
    ada = up_ada.outs
    big = {"w_gate_up": (r_gu,) + tuple(up_gu.outs), "w_down": (r_dn,) + tuple(up_dn.outs)}
    for name, w, g, m, v in (("w_in", w_in, r_in, m_w_in, v_w_in), ("w_branch", w_branch, r_br, m_w_branch, v_w_branch),
                             ("w_out", w_out, r_out, m_w_out, v_w_out)):
        big[name] = (g,) + tuple(_adamw(name, w[0], g, m[0], v[0]))
    g_small = _unpack_small(_sum_devices(parts), CONVW)
    g_conv = lax.dynamic_slice(g_small["conv_w"].reshape(CONV, CONVW), (0, chip * conv_cols), (CONV, conv_cols))
    g_small["conv_w"] = g_conv.reshape(1, CONV, 1, conv_cols)

    given = dict(norm1_g=(norm1_g, m_norm1_g, v_norm1_g), norm2_g=(norm2_g, m_norm2_g, v_norm2_g),
                 q_norm_g=(q_norm_g, m_q_norm_g, v_q_norm_g), k_norm_g=(k_norm_g, m_k_norm_g, v_k_norm_g),
                 sinks=(sinks, m_sinks, v_sinks), a_log=(a_log, m_a_log, v_a_log), dt_bias=(dt_bias, m_dt_bias, v_dt_bias),
                 dn_norm_g=(dn_norm_g, m_dn_norm_g, v_dn_norm_g))
    sheets = [_pack_small({k: t[j] for k, t in given.items()}, cw.reshape(CONV, conv_cols), ab)
              for j, (cw, ab) in enumerate(((conv_w, ada_b), (m_conv_w, m_ada_b), (v_conv_w, v_ada_b)))]
    g_local = _pack_small(g_small, g_conv, g_small["ada_b"])
    upd = [_unpack_small(s, conv_cols) for s in _adamw("small", sheets[0], g_local, sheets[1], sheets[2])]

    names = ["ada_w", "ada_b", "norm1_g", "w_in", "conv_w", "q_norm_g", "k_norm_g", "sinks", "a_log", "dt_bias", "dn_norm_g",
             "w_branch", "w_out", "norm2_g", "w_gate_up", "w_down"]

    def leaf(name, j):
        if name == "ada_w":
            return ada[j][None]
        if name in big:
            return big[name][j][None]
        return g_small[name] if j == 0 else upd[j - 1][name]

    return (loss, grad_x) + tuple(leaf(n, j) for j in range(4) for n in names)
```

```python
import functools
from typing import Callable, NamedTuple

import jax
import jax.numpy as jnp
import numpy as np
from jax import lax
from jax.experimental import pallas as pl
from jax.experimental.pallas import tpu as pltpu

F32 = jnp.float32
BF = jnp.bfloat16

D = 1024
HEADS = 8
KV_HEADS = 2
GROUP = 4
HD = 64
BLK = 128
ROT = 16
THETA = 500000.0
QW = 512
KVW = 128
DN_H = 4
DN_D = 128
CONV = 4
CHUNK = 64
DNW = 512
CONVW = 1536
FFN = 2816
EPS = 1e-6
IN_WIDTH = 4872
IN_PAD = 4992
C_KV = 512
C_DN = 768
C_Z = 2304
C_GA = 2816
C_GB = 3840
C_BA = 4864
NEG = -1e30
N_DEV = 8
N_CHIP = 4

ADAM_LR = 0.001
ADAM_B1 = 0.9
ADAM_B2 = 0.999
ADAM_EPS = 1e-08
ADAM_WD = 0.01
ADAM_STEP = 10

VMEM_LIMIT = 60 * 1024 * 1024


def _cparams(**kw):
    return pltpu.CompilerParams(vmem_limit_bytes=VMEM_LIMIT, **kw)


def _dg(a, b, ca, cb):
    return lax.dot_general(a.astype(BF), b.astype(BF), (((ca,), (cb,)), ((), ())),
                           preferred_element_type=F32)


@jax.custom_vjp
def _mm(a, b):
    return _dg(a, b, 1, 0)


def _mm_fwd(a, b):
    return _dg(a, b, 1, 0), (a, b)


def _mm_bwd(res, dy):
    a, b = res
    return _dg(dy, b, 1, 1).astype(a.dtype), _dg(a, dy, 0, 0).astype(b.dtype)


_mm.defvjp(_mm_fwd, _mm_bwd)


@jax.custom_vjp
def _mm_nt(a, b):
    return _dg(a, b, 1, 1)


def _mm_nt_fwd(a, b):
    return _dg(a, b, 1, 1), (a, b)


def _mm_nt_bwd(res, dy):
    a, b = res
    return _dg(dy, b, 1, 0).astype(a.dtype), _dg(dy, a, 0, 0).astype(b.dtype)


_mm_nt.defvjp(_mm_nt_fwd, _mm_nt_bwd)


@jax.custom_vjp
def _mm_tn(a, b):
    return _dg(a, b, 0, 0)


def _mm_tn_fwd(a, b):
    return _dg(a, b, 0, 0), (a, b)


def _mm_tn_bwd(res, dy):
    a, b = res
    return _dg(b, dy, 1, 1).astype(a.dtype), _dg(a, dy, 1, 0).astype(b.dtype)


_mm_tn.defvjp(_mm_tn_fwd, _mm_tn_bwd)


def _mmx(a, b):
    return jnp.dot(a, b, precision=lax.Precision.HIGHEST, preferred_element_type=F32)


def _mmx_nt(a, b):
    return lax.dot_general(a, b, (((1,), (1,)), ((), ())), precision=lax.Precision.HIGHEST,
                           preferred_element_type=F32)


def _iota(shape, dim):
    return lax.broadcasted_iota(jnp.int32, shape, dim)


def _sigmoid(x):
    return lax.logistic(x)


def _silu(x):
    return x * _sigmoid(x)


def _softplus(x):
    return jnp.maximum(x, 0.0) + jnp.log(1.0 + jnp.exp(-jnp.abs(x)))


def _rms(x, gain):
    return x * lax.rsqrt(jnp.mean(x * x, axis=-1, keepdims=True) + EPS) * gain


def _norm_mod(x, gain, shift, scale):
    return _rms(x, gain) * (1.0 + scale) + shift


def _split(a):
    hi = a.astype(BF)
    return hi, (a - hi.astype(F32)).astype(BF)


def _dg3(a, b, ca, cb):
    ah, al = _split(a)
    bh, bl = _split(b)

    def dg(x, y):
        return lax.dot_general(x, y, (((ca,), (cb,)), ((), ())), preferred_element_type=F32)
    return dg(ah, bh) + (dg(ah, bl) + dg(al, bh))


@jax.custom_vjp
def _mm3(a, b):
    return _dg3(a, b, 1, 0)


def _mm3_fwd(a, b):
    return _dg3(a, b, 1, 0), (a, b)


def _mm3_bwd(res, dy):
    a, b = res
    return _dg3(dy, b, 1, 1), _dg3(a, dy, 0, 0)


_mm3.defvjp(_mm3_fwd, _mm3_bwd)


def _qk_prep(slabs, gain, cos, sin):
    r = _iota((2 * HD, 2 * HD), 0)
    c = _iota((2 * HD, 2 * HD), 1)
    seg = jnp.where(r // HD == c // HD, 1.0 / HD, 0.0).astype(F32)
    half = ROT // 2
    cd = c % HD
    pair = jnp.where(((cd < half) & (r == c + half)) | ((cd >= half) & (cd < ROT) & (r == c - half)), 1.0, 0.0).astype(F32)
    out = []
    for x in slabs:
        y = x * lax.rsqrt(_mm3(x * x, seg) + EPS) * gain
        out.append(y * cos + _mm3(y, pair) * sin)
    return out


def _attn_block(qs, kc, kp, vc, vp, sinks, has_prev):
    rows = GROUP * BLK
    qi = _iota((rows, 2 * BLK), 0) % BLK + BLK
    kj = _iota((rows, 2 * BLK), 1)
    dist = qi - kj
    valid = (dist >= 0) & (dist < BLK) & ((kj >= BLK) | has_prev)
    grp = _iota((rows, HEADS), 0) // BLK
    col = _iota((rows, HEADS), 1)

    outs = []
    for h in range(KV_HEADS):
        q = jnp.concatenate([qs[h * GROUP + g] for g in range(GROUP)], axis=0)
        k = jnp.concatenate([kp[h], kc[h]], axis=0)
        v = jnp.concatenate([vp[h], vc[h]], axis=0)
        s = _mm_nt(q, k) * (HD ** -0.5)
        s = jnp.where(valid, s, NEG)
        sink = jnp.sum(jnp.where(col == h * GROUP + grp, sinks, 0.0), axis=-1, keepdims=True)
        m = lax.stop_gradient(jnp.maximum(jnp.max(s, axis=-1, keepdims=True), sink))
        p = jnp.exp(s - m)
        probs = p / (jnp.sum(p, axis=-1, keepdims=True) + jnp.exp(sink - m))
        o = _mm(probs, v)
        outs += [o[g * BLK:(g + 1) * BLK] for g in range(GROUP)]
    return outs


def _dn_act(y, normalize):
    s = _silu(y)
    return s * lax.rsqrt(jnp.sum(s * s, axis=-1, keepdims=True) + EPS) if normalize else s


def _dn_gates(ba, alog, dtb):
    lane = _iota(ba.shape, 1)
    beta = _sigmoid(ba)
    g = -jnp.exp(alog) * _softplus(ba + dtb)
    return jnp.where(lane < DN_H, beta, jnp.where(lane < 2 * DN_H, g, 0.0))


def _bdg(a, b, ca, cb):
    return lax.dot_general(a.astype(BF), b.astype(BF), (((ca,), (cb,)), ((0,), (0,))), preferred_element_type=F32)


@jax.custom_vjp
def _bmm(a, b):
    return _bdg(a, b, 2, 1)


def _bmm_fwd(a, b):
    return _bdg(a, b, 2, 1), (a, b)


def _bmm_bwd(res, dy):
    a, b = res
    return _bdg(dy, b, 2, 2), _bdg(a, dy, 1, 1)


_bmm.defvjp(_bmm_fwd, _bmm_bwd)


@jax.custom_vjp
def _bmm_nt(a, b):
    return _bdg(a, b, 2, 2)


def _bmm_nt_fwd(a, b):
    return _bdg(a, b, 2, 2), (a, b)


def _bmm_nt_bwd(res, dy):
    a, b = res
    return _bdg(dy, b, 2, 1), _bdg(dy, a, 1, 1)


_bmm_nt.defvjp(_bmm_nt_fwd, _bmm_nt_bwd)


def _bmmx(a, b):
    return lax.dot_general(a, b, (((2,), (1,)), ((0,), (0,))), precision=lax.Precision.HIGHEST,
                           preferred_element_type=F32)


def _neumann_inverse(lmat):
    C = CHUNK
    eye = jnp.where(_iota((C, C), 0) == _iota((C, C), 1), 1.0, 0.0).astype(F32)[None]
    a = -lmat
    tinv = eye + a
    pw = _bmmx(a, a)
    for _ in range(4):
        both = _bmmx(jnp.concatenate([pw, tinv], axis=1), pw)
        pw, tinv = both[:, :C], tinv + both[:, C:]
    return tinv + _bmmx(tinv, pw)


def _inverse_bwd(tinv, d_tinv):
    x = lax.dot_general(d_tinv, tinv, (((2,), (2,)), ((0,), (0,))), precision=lax.Precision.HIGHEST,
                        preferred_element_type=F32)
    return -lax.dot_general(tinv, x, (((1,), (1,)), ((0,), (0,))), precision=lax.Precision.HIGHEST,
                            preferred_element_type=F32)


@jax.custom_vjp
def _tri_inverse(lmat):
    return _neumann_inverse(lmat)


def _tri_inverse_fwd(lmat):
    tinv = _neumann_inverse(lmat)
    return tinv, tinv


def _tri_inverse_bwd(tinv, d_tinv):
    return (_inverse_bwd(tinv, d_tinv),)


_tri_inverse.defvjp(_tri_inverse_fwd, _tri_inverse_bwd)


@jax.custom_vjp
def _tri_inverse_known(lmat, tinv):
    return tinv


def _tri_inverse_known_fwd(lmat, tinv):
    return tinv, tinv


def _tri_inverse_known_bwd(tinv, d_tinv):
    return _inverse_bwd(tinv, d_tinv), jnp.zeros_like(tinv)


_tri_inverse_known.defvjp(_tri_inverse_known_fwd, _tri_inverse_known_bwd)


def _dn_intra(q, k, v, bg, tinv=None):
    C = CHUNK
    G = bg.shape[0]
    r = _iota((C, C), 0)
    c = _iota((C, C), 1)
    incl = (r >= c)[None]
    strict = (r > c)[None]
    eye = jnp.where(r == c, 1.0, 0.0).astype(F32)[None]
    tri = jnp.broadcast_to(jnp.where(r >= c, 1.0, 0.0).astype(F32)[None], (G, C, C))
    gc_all = _bmmx(tri, bg)
    lane = _iota((C, DN_D), 1)

    def per_head(x, offset):
        return jnp.concatenate([jnp.sum(jnp.where(lane == offset + h, x[g], 0.0), axis=-1, keepdims=True)[None]
                                for g in range(G) for h in range(DN_H)], axis=0)
    beta = per_head(bg, 0)
    gcol = per_head(gc_all, DN_H)
    grow = jnp.sum(eye * gcol, axis=1, keepdims=True)
    glast = jnp.sum(jnp.where(_iota((1, C, 1), 1) == C - 1, gcol, 0.0), axis=1, keepdims=True)
    decay = jnp.exp(jnp.where(incl, gcol - grow, NEG))
    q = q * (DN_D ** -0.5)
    kb = k * beta
    lmat = jnp.where(strict, _bmm_nt(kb, k) * decay, 0.0)
    tinv = _tri_inverse(lmat) if tinv is None else _tri_inverse_known(lmat, tinv)
    egc = jnp.exp(gcol)
    u = _bmm(tinv, v * beta)
    w = _bmm(tinv, kb * egc)
    a = _bmm_nt(q, k) * decay
    return u, w, q * egc, k * jnp.exp(glast - gcol), a, jnp.exp(glast), tinv


@jax.custom_vjp
def _bmm_tn(a, b):
    return _bdg(a, b, 1, 1)


def _bmm_tn_fwd(a, b):
    return _bdg(a, b, 1, 1), (a, b)


def _bmm_tn_bwd(res, dy):
    a, b = res
    return _bdg(b, dy, 2, 2), _bdg(a, dy, 2, 1)


_bmm_tn.defvjp(_bmm_tn_fwd, _bmm_tn_bwd)


def _dn_rec(state, u, w, qd, kd, a, cd):
    v_new = u - _bmm(w, state)
    out = _bmm(qd, state) + _bmm(a, v_new)
    return state * cd + _bmm_tn(kd, v_new), out


def _mix_tile(o_attn, o_raw, zs, ga, gb, x, gate1, dn_g, wb_a, wb_d, w_out, p_ya, p_yd, p_out):
    o_dn = jnp.concatenate([_rms(o_raw[h], dn_g) * _silu(zs[h]) for h in range(DN_H)], axis=-1)
    y_a = _mm(o_attn, wb_a) + p_ya
    y_d = _mm(o_dn, wb_d) + p_yd
    merged = _sigmoid(ga) * y_a + _sigmoid(gb) * y_d
    out = _mm(merged, w_out) + p_out
    return x + gate1 * out, o_dn, merged


def _mlp_tile(x1, gain, shift, scale, gate2, w_gu, w_dn, tgt, p_gu, p_yy):
    h2 = _norm_mod(x1, gain, shift, scale)
    gu = jnp.concatenate([_mm(h2, w) for w in w_gu], axis=-1) + p_gu
    act = _silu(gu[:, :FFN]) * gu[:, FFN:]
    yy = _mm(act, w_dn) + p_yy
    y = x1 + gate2 * yy
    err = y - tgt
    return 0.5 * jnp.sum(err * err) * (1.0 / D), (h2, act)


def _tok(bt, f):
    return pl.BlockSpec((None, bt, f), lambda b, i: (b, i, 0))


def _full(shape):
    return pl.BlockSpec(shape, lambda b, i: (0,) * len(shape))


def _resident(shape):
    return pl.BlockSpec(shape, lambda b, i: (0,) * len(shape), pipeline_mode=pl.Buffered(1))


def _per_batch(f):
    return pl.BlockSpec((None, 1, f), lambda b, i: (b, 0, 0))


def _sds(shape, dtype):
    return jax.ShapeDtypeStruct(shape, dtype)


class _Exchange(NamedTuple):
    ins: tuple
    out_shapes: tuple
    n_remote: int
    plan: Callable
    n_forward: int = 0
    forward: Callable = None


def _remote_copies(remote, send_sems, recv_sems):
    return [pltpu.make_async_remote_copy(src_ref=src, dst_ref=dst, send_sem=send_sems.at[i], recv_sem=recv_sems.at[i],
                                         device_id=peer, device_id_type=pl.DeviceIdType.MESH)
            for i, (src, dst, peer) in enumerate(remote)]


def _exchange_copies(ex, in_refs, out_refs, send_sems, recv_sems):
    remote = ex.plan((lax.axis_index("x"), lax.axis_index("y"), lax.axis_index("c")), in_refs, out_refs)
    assert len(remote) == ex.n_remote
    return _remote_copies(remote, send_sems, recv_sems)


def _forward_copies(ex, out_refs, send_sems, recv_sems):
    remote = ex.forward((lax.axis_index("x"), lax.axis_index("y"), lax.axis_index("c")), out_refs)
    assert len(remote) == ex.n_forward
    return _remote_copies(remote, send_sems, recv_sems)


def _exchange_sems(ex):
    sems = [pltpu.SemaphoreType.DMA((ex.n_remote,)), pltpu.SemaphoreType.DMA((ex.n_remote,))]
    if ex.n_forward:
        sems += [pltpu.SemaphoreType.DMA((ex.n_forward,)), pltpu.SemaphoreType.DMA((ex.n_forward,))]
    return sems


def _hosted_call(body, name, grid, in_specs, out_specs, out_shape, scratch_shapes, semantics, ins, ex=None):
    if ex is None:
        outs = pl.pallas_call(body, name=name, grid=grid, in_specs=in_specs, out_specs=out_specs, out_shape=out_shape,
                              scratch_shapes=scratch_shapes,
                              compiler_params=_cparams(dimension_semantics=semantics))(*ins)
        return outs, ()
    n_in, n_out, n_scr = len(ins), len(out_shape), len(scratch_shapes)
    c_in, c_out = len(ex.ins), len(ex.out_shapes)
    steps = 1
    for g in grid:
        steps *= g

    def wrapped(*refs):
        a, b, c, d = n_in, n_in + c_in, n_in + c_in + n_out, n_in + c_in + n_out + c_out
        scratch, sems = refs[d:d + n_scr], refs[d + n_scr:]
        step = 0
        for axis, g in enumerate(grid):
            step = step * g + pl.program_id(axis)

        def first_phase():
            return _exchange_copies(ex, refs[a:b], refs[c:d], sems[0], sems[1])

        @pl.when(step == 0)
        def _():
            for cp in first_phase():
                cp.start()
        body(*refs[:a], *refs[b:c], *scratch)

        if ex.n_forward:
            @pl.when(step == (3 * steps) // 4)
            def _():
                for cp in first_phase():
                    cp.wait_recv()
                for cp in _forward_copies(ex, refs[c:d], sems[2], sems[3]):
                    cp.start()

        @pl.when(step == steps - 1)
        def _():
            cps = first_phase()
            if ex.n_forward:
                fwd = _forward_copies(ex, refs[c:d], sems[2], sems[3])
                for cp in fwd:
                    cp.wait_recv()
                for cp in cps + fwd:
                    cp.wait_send()
            else:
                for cp in cps:
                    cp.wait_recv()
                for cp in cps:
                    cp.wait_send()

    any_spec = pl.BlockSpec(memory_space=pl.ANY)
    res = pl.pallas_call(
        wrapped, name=name, grid=grid, in_specs=list(in_specs) + [any_spec] * c_in,
        out_specs=list(out_specs) + [any_spec] * c_out, out_shape=list(out_shape) + list(ex.out_shapes),
        scratch_shapes=list(scratch_shapes) + _exchange_sems(ex),
        compiler_params=_cparams(dimension_semantics=("arbitrary",) * len(grid)),
    )(*ins, *ex.ins)
    return res[:n_out], res[n_out:]


def _acc(ref, val, first):
    @pl.when(first)
    def _():
        ref[...] = val

    @pl.when(jnp.logical_not(first))
    def _():
        ref[...] += val


def _in_proj(x, mod, norm1_g, w_in, bt):
    B, S, _ = x.shape

    def body(x_ref, mod_ref, g_ref, w_ref, q_ref, kv_ref, dn_ref, z_ref, ga_ref, gb_ref, ba_ref, h_ref):
        h = _norm_mod(x_ref[...], g_ref[...], mod_ref[:, 0:D], mod_ref[:, D:2 * D]).astype(BF)
        h_ref[...] = h

        def proj(c0, c1):
            return jnp.dot(h, w_ref[:, c0:c1], preferred_element_type=F32)
        q_ref[...] = proj(0, C_KV).astype(BF)
        kv_ref[...] = proj(C_KV, C_DN).astype(BF)
        dn_ref[...] = proj(C_DN, C_Z).astype(BF)
        z_ref[...] = proj(C_Z, C_GA).astype(BF)
        ga_ref[...] = proj(C_GA, C_GB).astype(BF)
        gb_ref[...] = proj(C_GB, C_BA).astype(BF)
        ba_ref[...] = proj(C_BA, IN_PAD)

    widths = (QW, 2 * KVW, CONVW, DNW, D, D)
    return pl.pallas_call(
        body, name="in_proj", grid=(B, S // bt),
        in_specs=[_tok(bt, D), _per_batch(6 * D), _full((1, D)), _resident((D, IN_PAD))],
        out_specs=[_tok(bt, w) for w in widths] + [_tok(bt, 128), _tok(bt, D)],
        out_shape=[_sds((B, S, w), BF) for w in widths] + [_sds((B, S, 128), F32), _sds((B, S, D), BF)],
        compiler_params=_cparams(dimension_semantics=("parallel", "parallel")),
    )(x, mod, norm1_g, w_in)


def _prev_blk(bt, f):
    return pl.BlockSpec((None, bt, f), lambda b, i: (b, jnp.maximum(i - 1, 0), 0))


QKV = QW + 2 * KVW


def _qk_slabs(q_ref, kv_ref):
    return ([q_ref[:, j * 2 * HD:(j + 1) * 2 * HD].astype(F32) for j in range(QW // (2 * HD))],
            [kv_ref[:, 0:KVW].astype(F32)])


def _qk_prep_fwd(q, kv, cos, sin, qg, kg, bt):
    B, S, _ = q.shape

    def body(q_ref, kv_ref, cos_ref, sin_ref, qg_ref, kg_ref, o_ref):
        qs, ks = _qk_slabs(q_ref, kv_ref)
        qn = _qk_prep(qs, qg_ref[...], cos_ref[...], sin_ref[...])
        kn = _qk_prep(ks, kg_ref[...], cos_ref[...], sin_ref[...])
        for j, t in enumerate(qn + kn):
            o_ref[:, j * 2 * HD:(j + 1) * 2 * HD] = t.astype(BF)
        o_ref[:, QW + KVW:QKV] = kv_ref[:, KVW:2 * KVW]

    return pl.pallas_call(
        body, name="qk_prep_fwd", grid=(B, S // bt),
        in_specs=[_tok(bt, QW), _tok(bt, 2 * KVW), _tok(bt, 2 * HD), _tok(bt, 2 * HD), _full((1, 2 * HD)), _full((1, 2 * HD))],
        out_specs=_tok(bt, QKV), out_shape=_sds((B, S, QKV), BF),
        compiler_params=_cparams(dimension_semantics=("parallel", "parallel")),
    )(q, kv, cos, sin, qg, kg)


def _qk_prep_bwd(q, kv, cos, sin, qg, kg, dqn, dkvn, bt, ex=None):
    B, S, _ = q.shape

    def body(q_ref, kv_ref, cos_ref, sin_ref, qg_ref, kg_ref, dqn_ref, dkvn_ref, dq_ref, dkv_ref, dqg_ref, dkg_ref):
        qs, ks = _qk_slabs(q_ref, kv_ref)
        cos, sin = cos_ref[...], sin_ref[...]

        def f(qs, ks, qg, kg):
            return _qk_prep(qs, qg, cos, sin), _qk_prep(ks, kg, cos, sin)
        _, vjp = jax.vjp(f, qs, ks, qg_ref[...], kg_ref[...])
        n_q = len(qs)
        d_q = [dqn_ref[:, j * 2 * HD:(j + 1) * 2 * HD].astype(F32) for j in range(n_q)]
        d_k = [dkvn_ref[:, 0:KVW].astype(F32)]
        dqs, dks, dqg, dkg = vjp((d_q, d_k))
        for j in range(n_q):
            dq_ref[:, j * 2 * HD:(j + 1) * 2 * HD] = dqs[j].astype(BF)
        dkv_ref[:, 0:KVW] = dks[0].astype(BF)
        dkv_ref[:, KVW:2 * KVW] = dkvn_ref[:, KVW:2 * KVW]
        first = (pl.program_id(0) == 0) & (pl.program_id(1) == 0)
        _acc(dqg_ref, dqg, first)
        _acc(dkg_ref, dkg, first)

    return _hosted_call(
        body, "qk_prep_bwd", (B, S // bt),
        in_specs=[_tok(bt, QW), _tok(bt, 2 * KVW), _tok(bt, 2 * HD), _tok(bt, 2 * HD), _full((1, 2 * HD)), _full((1, 2 * HD)),
                  _tok(bt, QW), _tok(bt, 2 * KVW)],
        out_specs=[_tok(bt, QW), _tok(bt, 2 * KVW), _full((1, 2 * HD)), _full((1, 2 * HD))],
        out_shape=[_sds((B, S, QW), BF), _sds((B, S, 2 * KVW), BF), _sds((1, 2 * HD), F32), _sds((1, 2 * HD), F32)],
        scratch_shapes=[], semantics=("arbitrary", "arbitrary"), ins=(q, kv, cos, sin, qg, kg, dqn, dkvn), ex=ex)


def _attn_load(qkv_ref, kvp_ref):
    qs = [qkv_ref[:, h * HD:(h + 1) * HD].astype(F32) for h in range(HEADS)]
    kc = [qkv_ref[:, QW + h * HD:QW + (h + 1) * HD].astype(F32) for h in range(KV_HEADS)]
    vc = [qkv_ref[:, QW + KVW + h * HD:QW + KVW + (h + 1) * HD].astype(F32) for h in range(KV_HEADS)]
    kp = [kvp_ref[:, h * HD:(h + 1) * HD].astype(F32) for h in range(KV_HEADS)]
    vp = [kvp_ref[:, KVW + h * HD:KVW + (h + 1) * HD].astype(F32) for h in range(KV_HEADS)]
    return qs, kc, kp, vc, vp


def _kv_prev_spec(index):
    return pl.BlockSpec((None, BLK, 2 * KVW), lambda b, i: (b, index(i), QW // (2 * KVW)))


def _attn_fwd(qkv, sinks):
    B, S, _ = qkv.shape

    def body(qkv_ref, kvp_ref, sk_ref, o_ref):
        qs, kc, kp, vc, vp = _attn_load(qkv_ref, kvp_ref)
        outs = _attn_block(qs, kc, kp, vc, vp, sk_ref[...], pl.program_id(1) > 0)
        for h in range(HEADS):
            o_ref[:, h * HD:(h + 1) * HD] = outs[h].astype(BF)

    return pl.pallas_call(
        body, name="attn_fwd", grid=(B, S // BLK),
        in_specs=[_tok(BLK, QKV), _kv_prev_spec(lambda i: jnp.maximum(i - 1, 0)), _full((1, HEADS))],
        out_specs=_tok(BLK, QW), out_shape=_sds((B, S, QW), BF),
        compiler_params=_cparams(dimension_semantics=("parallel", "parallel")),
    )(qkv, qkv, sinks)


def _halo_spec(bt):
    return pl.BlockSpec((None, 8, CONVW), lambda b, i: (b, jnp.maximum(i * (bt // 8) - 1, 0), 0))


def _dn_prep(dn, ba, conv_w, alog, dtb, bt):
    B, S, _ = dn.shape

    strip_rows = min(bt, 64)

    def body(x_ref, halo_ref, ba_ref, cw_ref, al_ref, dt_ref, qkv_ref, bg_ref, y_ref, xe_ref):
        xe_ref[0:8, :] = jnp.where(pl.program_id(1) == 0, 0.0, halo_ref[...].astype(F32))
        xe_ref[8:bt + 8, :] = x_ref[...].astype(F32)

        def strip(k, carry):
            r0 = pl.multiple_of(k * strip_rows, strip_rows)
            rows = pl.ds(r0, strip_rows)
            for j in range(3 * DN_H):
                cols = slice(j * DN_D, (j + 1) * DN_D)
                window = xe_ref[pl.ds(r0, strip_rows + 8), cols]
                y = cw_ref[0:1, cols] * window[5:strip_rows + 5]
                for t in range(1, CONV):
                    y = y + cw_ref[t:t + 1, cols] * window[5 + t:strip_rows + 5 + t]
                y_ref[rows, cols] = y.astype(BF)
                qkv_ref[rows, cols] = _dn_act(y, j < 2 * DN_H)
            bg_ref[rows, :] = _dn_gates(ba_ref[rows, :], al_ref[...], dt_ref[...])
            return carry
        lax.fori_loop(0, bt // strip_rows, strip, 0)

    return pl.pallas_call(
        body, name="dn_prep", grid=(B, S // bt),
        in_specs=[_tok(bt, CONVW), _halo_spec(bt), _tok(bt, 128), _full((CONV, CONVW)), _full((1, 128)), _full((1, 128))],
        out_specs=[_tok(bt, CONVW), _tok(bt, 128), _tok(bt, CONVW)],
        out_shape=[_sds((B, S, CONVW), F32), _sds((B, S, 128), F32), _sds((B, S, CONVW), BF)],
        scratch_shapes=[pltpu.VMEM((bt + 8, CONVW), F32)],
        compiler_params=_cparams(dimension_semantics=("parallel", "arbitrary")),
    )(dn, dn, ba, conv_w, alog, dtb)


def _dn_load(qkv_ref):
    qs = [qkv_ref[:, h * DN_D:(h + 1) * DN_D] for h in range(DN_H)]
    ks = [qkv_ref[:, DNW + h * DN_D:DNW + (h + 1) * DN_D] for h in range(DN_H)]
    vs = [qkv_ref[:, 2 * DNW + h * DN_D:2 * DNW + (h + 1) * DN_D] for h in range(DN_H)]
    return qs, ks, vs


DN_GROUP = 4
AW = DN_H * CHUNK


def _stack_heads(ref, G, offset, width):
    return jnp.stack([ref[g * CHUNK:(g + 1) * CHUNK, offset + h * width:offset + (h + 1) * width]
                      for g in range(G) for h in range(DN_H)])


def _dn_load_stack(qkv_ref, G):
    return tuple(_stack_heads(qkv_ref, G, j * DNW, DN_D) for j in range(3))


def _cd_spec(n):
    return pl.BlockSpec((None, n, 1, DN_D), lambda b, i: (b, i, 0, 0))


def _dn_intra_fwd(qkv, bg, ex=None):
    B, S, _ = qkv.shape
    nc = S // CHUNK
    G = min(DN_GROUP, nc)
    rows = G * CHUNK

    def body(qkv_ref, bg_ref, u_ref, w_ref, qd_ref, kd_ref, a_ref, cd_ref, t_ref):
        q, k, v = _dn_load_stack(qkv_ref, G)
        u, w, qd, kd, a, cd, tinv = _dn_intra(q, k, v, bg_ref[...].reshape(G, CHUNK, DN_D))
        lane_row = _iota((1, DN_D), 1)
        for g in range(G):
            rows = slice(g * CHUNK, (g + 1) * CHUNK)
            cd_row = jnp.zeros((1, DN_D), F32)
            for h in range(DN_H):
                n = g * DN_H + h
                cols = slice(h * DN_D, (h + 1) * DN_D)
                u_ref[rows, cols] = u[n]
                w_ref[rows, cols] = w[n].astype(BF)
                qd_ref[rows, cols] = qd[n].astype(BF)
                kd_ref[rows, cols] = kd[n].astype(BF)
                a_ref[rows, h * CHUNK:(h + 1) * CHUNK] = a[n].astype(BF)
                t_ref[rows, h * CHUNK:(h + 1) * CHUNK] = tinv[n]
                cd_row = cd_row + jnp.where(lane_row == h, cd[n], 0.0)
            cd_ref[g] = cd_row

    return _hosted_call(
        body, "dn_intra_fwd", (B, nc // G),
        in_specs=[_tok(rows, CONVW), _tok(rows, 128)],
        out_specs=[_tok(rows, DNW)] * 4 + [_tok(rows, AW), _cd_spec(G), _tok(rows, AW)],
        out_shape=[_sds((B, S, DNW), F32)] + [_sds((B, S, DNW), BF)] * 3 + [_sds((B, S, AW), BF), _sds((B, nc, 1, DN_D), F32),
                                                                            _sds((B, S, AW), F32)],
        scratch_shapes=[], semantics=("parallel", "parallel"), ins=(qkv, bg), ex=ex)


REC_GROUP = 4


def _rec_stack(ref, B, width, c):
    rows = slice(c * CHUNK, (c + 1) * CHUNK)
    return jnp.stack([ref[b, rows, h * width:(h + 1) * width].astype(F32) for b in range(B) for h in range(DN_H)])


def _rec_load(B, u_ref, w_ref, qd_ref, kd_ref, a_ref, cd_ref, c):
    lane_row = _iota((1, DN_D), 1)
    cd = jnp.stack([jnp.sum(jnp.where(lane_row == h, cd_ref[b, c], 0.0), axis=-1, keepdims=True)
                    for b in range(B) for h in range(DN_H)])
    return (_rec_stack(u_ref, B, DN_D, c), _rec_stack(w_ref, B, DN_D, c), _rec_stack(qd_ref, B, DN_D, c),
            _rec_stack(kd_ref, B, DN_D, c), _rec_stack(a_ref, B, CHUNK, c), cd)


def _rec_store(B, ref, val, width, c):
    for b in range(B):
        for h in range(DN_H):
            ref[b, c * CHUNK:(c + 1) * CHUNK, h * width:(h + 1) * width] = val[b * DN_H + h]


def _rec_specs(B, R, index):
    def tok(f):
        return pl.BlockSpec((B, R * CHUNK, f), lambda i: (0, index(i), 0))
    cd = pl.BlockSpec((B, R, 1, DN_D), lambda i: (0, index(i), 0, 0))
    st = pl.BlockSpec((B, R, DN_H, DN_D, DN_D), lambda i: (0, index(i), 0, 0, 0))
    return tok, cd, st


def _dn_rec_fwd(u, w, qd, kd, a, cd, ex=None):
    B, S, _ = u.shape
    nc = S // CHUNK
    R = REC_GROUP if nc % REC_GROUP == 0 else 1
    tok, cd_spec, st_spec = _rec_specs(B, R, lambda i: i)

    def body(u_ref, w_ref, qd_ref, kd_ref, a_ref, cd_ref, o_ref, st_ref, s_ref):
        @pl.when(pl.program_id(0) == 0)
        def _():
            s_ref[...] = jnp.zeros_like(s_ref)
        state = s_ref[...]
        for c in range(R):
            st_ref[:, c] = state.reshape(B, DN_H, DN_D, DN_D)
            state, out = _dn_rec(state, *_rec_load(B, u_ref, w_ref, qd_ref, kd_ref, a_ref, cd_ref, c))
            _rec_store(B, o_ref, out, DN_D, c)
        s_ref[...] = state

    return _hosted_call(
        body, "dn_rec_fwd", (nc // R,),
        in_specs=[tok(DNW)] * 4 + [tok(AW), cd_spec],
        out_specs=[tok(DNW), st_spec],
        out_shape=[_sds((B, S, DNW), F32), _sds((B, nc, DN_H, DN_D, DN_D), F32)],
        scratch_shapes=[pltpu.VMEM((B * DN_H, DN_D, DN_D), F32)],
        semantics=("arbitrary",), ins=(u, w, qd, kd, a, cd), ex=ex)


def _mix_load(oa_ref, or_ref, z_ref):
    o_raw = [or_ref[:, h * DN_D:(h + 1) * DN_D] for h in range(DN_H)]
    zs = [z_ref[:, h * DN_D:(h + 1) * DN_D].astype(F32) for h in range(DN_H)]
    return oa_ref[...].astype(F32), o_raw, zs


def _mix_fwd(o_attn, o_raw, z, ga, gb, x, mod, dn_g, w_branch, w_out, bt):
    B, S, _ = x.shape

    def body(oa_ref, or_ref, z_ref, ga_ref, gb_ref, x_ref, mod_ref, dg_ref, wb_ref, wo_ref, x1_ref, od_ref, mg_ref):
        oa, o_r, zs = _mix_load(oa_ref, or_ref, z_ref)
        x1, o_dn, merged = _mix_tile(oa, o_r, zs, ga_ref[...].astype(F32), gb_ref[...].astype(F32), x_ref[...],
                                     mod_ref[:, 2 * D:3 * D], dg_ref[...], wb_ref[0:QW, :], wb_ref[QW:2 * QW, :],
                                     wo_ref[...], 0.0, 0.0, 0.0)
        x1_ref[...] = x1
        od_ref[...] = o_dn.astype(BF)
        mg_ref[...] = merged.astype(BF)

    return pl.pallas_call(
        body, name="mix_fwd", grid=(B, S // bt),
        in_specs=[_tok(bt, QW), _tok(bt, DNW), _tok(bt, DNW), _tok(bt, D), _tok(bt, D), _tok(bt, D), _per_batch(6 * D),
                  _full((1, DN_D)), _resident((D, D)), _resident((D, D))],
        out_specs=[_tok(bt, D), _tok(bt, DNW), _tok(bt, D)],
        out_shape=[_sds((B, S, D), F32), _sds((B, S, DNW), BF), _sds((B, S, D), BF)],
        compiler_params=_cparams(dimension_semantics=("parallel", "parallel")),
    )(o_attn, o_raw, z, ga, gb, x, mod, dn_g, w_branch, w_out)


def _mlp(x1, tgt, mod, norm2_g, w_gu, w_dn, bt):
    B, S, _ = x1.shape

    def body(x1_ref, t_ref, mod_ref, g_ref, wgu_ref, wdn_ref,
             dx1_ref, h2_ref, act_ref, dgu_ref, dyy_ref, loss_ref, dmod_ref, dg_ref):
        w_gu_v, w_dn_v, t = [wgu_ref[k] for k in range(N_CHIP)], wdn_ref[...], t_ref[...]

        def f(x1, gain, shift, scale, gate2, p_gu, p_yy):
            return _mlp_tile(x1, gain, shift, scale, gate2, w_gu_v, w_dn_v, t, p_gu, p_yy)
        zero_gu = jnp.zeros((bt, 2 * FFN), F32)
        zero_yy = jnp.zeros((bt, D), F32)
        loss, vjp, (h2, act) = jax.vjp(f, x1_ref[...], g_ref[...], mod_ref[:, 3 * D:4 * D], mod_ref[:, 4 * D:5 * D],
                                       mod_ref[:, 5 * D:6 * D], zero_gu, zero_yy, has_aux=True)
        dx1, dgain, dshift, dscale, dgate2, dgu, dyy = vjp(jnp.ones((), F32))
        dx1_ref[...] = dx1
        h2_ref[...] = h2.astype(BF)
        act_ref[...] = act.astype(BF)
        dgu_ref[...] = dgu.astype(BF)
        dyy_ref[...] = dyy.astype(BF)
        first = pl.program_id(1) == 0
        _acc(loss_ref, jnp.reshape(loss, (1, 1)), first)
        _acc(dmod_ref, jnp.concatenate([dshift, dscale, dgate2], axis=-1), first)
        _acc(dg_ref, dgain, first)

    return pl.pallas_call(
        body, name="mlp", grid=(B, S // bt),
        in_specs=[_tok(bt, D), _tok(bt, D), _per_batch(6 * D), _full((1, D)), _resident((N_CHIP, D, 2 * FFN // N_CHIP)),
                  _resident((FFN, D))],
        out_specs=[_tok(bt, D), _tok(bt, D), _tok(bt, FFN), _tok(bt, 2 * FFN), _tok(bt, D),
                   _per_batch(1), _per_batch(3 * D), _per_batch(D)],
        out_shape=[_sds((B, S, D), F32), _sds((B, S, D), BF), _sds((B, S, FFN), BF), _sds((B, S, 2 * FFN), BF),
                   _sds((B, S, D), BF), _sds((B, 1, 1), F32), _sds((B, 1, 3 * D), F32), _sds((B, 1, D), F32)],
        compiler_params=_cparams(dimension_semantics=("parallel", "arbitrary")),
    )(x1, tgt, mod, norm2_g, w_gu, w_dn)


def _mix_bwd(o_attn, o_raw, z, ga, gb, x, mod, dn_g, w_branch, w_out, dx1, bt, ex=None):
    B, S, _ = x.shape

    def body(oa_ref, or_ref, z_ref, ga_ref, gb_ref, x_ref, mod_ref, dg_ref, wb_ref, wo_ref, dx1_ref,
             doa_ref, dor_ref, dz_ref, dga_ref, dgb_ref, dya_ref, dyd_ref, dout_ref, dgate_ref, ddg_ref):
        oa, o_r, zs = _mix_load(oa_ref, or_ref, z_ref)
        wb_a, wb_d, wo = wb_ref[0:QW, :], wb_ref[QW:2 * QW, :], wo_ref[...]

        def f(oa, o_r, zs, ga, gb, gate1, dn_g, p_ya, p_yd, p_out):
            return _mix_tile(oa, o_r, zs, ga, gb, x_ref[...], gate1, dn_g, wb_a, wb_d, wo, p_ya, p_yd, p_out)[0]
        zero = jnp.zeros((bt, D), F32)
        _, vjp = jax.vjp(f, oa, o_r, zs, ga_ref[...].astype(F32), gb_ref[...].astype(F32), mod_ref[:, 2 * D:3 * D],
                         dg_ref[...], zero, zero, zero)
        doa, dor, dzs, dga, dgb, dgate1, ddn_g, dya, dyd, dout = vjp(dx1_ref[...])
        doa_ref[...] = doa
        for h in range(DN_H):
            dor_ref[:, h * DN_D:(h + 1) * DN_D] = dor[h]
            dz_ref[:, h * DN_D:(h + 1) * DN_D] = dzs[h].astype(BF)
        dga_ref[...] = dga.astype(BF)
        dgb_ref[...] = dgb.astype(BF)
        dya_ref[...] = dya.astype(BF)
        dyd_ref[...] = dyd.astype(BF)
        dout_ref[...] = dout.astype(BF)
        first = pl.program_id(1) == 0
        _acc(dgate_ref, dgate1, first)
        _acc(ddg_ref, ddn_g, first)

    return _hosted_call(
        body, "mix_bwd", (B, S // bt),
        in_specs=[_tok(bt, QW), _tok(bt, DNW), _tok(bt, DNW), _tok(bt, D), _tok(bt, D), _tok(bt, D), _per_batch(6 * D),
                  _full((1, DN_D)), _resident((D, D)), _resident((D, D)), _tok(bt, D)],
        out_specs=[_tok(bt, QW), _tok(bt, DNW), _tok(bt, DNW), _tok(bt, D), _tok(bt, D), _tok(bt, D), _tok(bt, D), _tok(bt, D),
                   _per_batch(D), _per_batch(DN_D)],
        out_shape=[_sds((B, S, QW), F32), _sds((B, S, DNW), F32), _sds((B, S, DNW), BF), _sds((B, S, D), BF),
                   _sds((B, S, D), BF), _sds((B, S, D), BF), _sds((B, S, D), BF), _sds((B, S, D), BF),
                   _sds((B, 1, D), F32), _sds((B, 1, DN_D), F32)],
        scratch_shapes=[], semantics=("parallel", "arbitrary"),
        ins=(o_attn, o_raw, z, ga, gb, x, mod, dn_g, w_branch, w_out, dx1), ex=ex)


def _dn_rec_bwd(u, w, qd, kd, a, cd, states, d_o, ex=None):
    B, S, _ = u.shape
    nc = S // CHUNK
    R = REC_GROUP if nc % REC_GROUP == 0 else 1
    tok, cd_spec, st_spec = _rec_specs(B, R, lambda i: nc // R - 1 - i)

    def body(u_ref, w_ref, qd_ref, kd_ref, a_ref, cd_ref, st_ref, do_ref,
             du_ref, dw_ref, dqd_ref, dkd_ref, da_ref, dcd_ref, ds_ref):
        @pl.when(pl.program_id(0) == 0)
        def _():
            ds_ref[...] = jnp.zeros_like(ds_ref)
        lane_row = _iota((1, DN_D), 1)
        d_state = ds_ref[...]
        for c in reversed(range(R)):
            state = st_ref[:, c].reshape(B * DN_H, DN_D, DN_D)
            _, vjp = jax.vjp(_dn_rec, state, *_rec_load(B, u_ref, w_ref, qd_ref, kd_ref, a_ref, cd_ref, c))
            d_state, du, dw, dqd, dkd, da, dcd = vjp((d_state, _rec_stack(do_ref, B, DN_D, c)))
            for ref, val, width in ((du_ref, du, DN_D), (dw_ref, dw, DN_D), (dqd_ref, dqd, DN_D), (dkd_ref, dkd, DN_D),
                                    (da_ref, da, CHUNK)):
                _rec_store(B, ref, val, width, c)
            for b in range(B):
                row = jnp.zeros((1, DN_D), F32)
                for h in range(DN_H):
                    row = row + jnp.where(lane_row == h, dcd[b * DN_H + h], 0.0)
                dcd_ref[b, c] = row
        ds_ref[...] = d_state

    return _hosted_call(
        body, "dn_rec_bwd", (nc // R,),
        in_specs=[tok(DNW)] * 4 + [tok(AW), cd_spec, st_spec, tok(DNW)],
        out_specs=[tok(DNW)] * 4 + [tok(AW), cd_spec],
        out_shape=[_sds((B, S, DNW), F32)] * 4 + [_sds((B, S, AW), F32), _sds((B, nc, 1, DN_D), F32)],
        scratch_shapes=[pltpu.VMEM((B * DN_H, DN_D, DN_D), F32)],
        semantics=("arbitrary",), ins=(u, w, qd, kd, a, cd, states, d_o), ex=ex)


def _dn_intra_bwd(qkv, bg, tinv, du, dw, dqd, dkd, da, dcd, ex=None):
    B, S, _ = qkv.shape
    nc = S // CHUNK
    G = min(DN_GROUP, nc)
    rows = G * CHUNK

    def body(qkv_ref, bg_ref, t_ref, du_ref, dw_ref, dqd_ref, dkd_ref, da_ref, dcd_ref, dqkv_ref, dbg_ref):
        q, k, v = _dn_load_stack(qkv_ref, G)
        known = _stack_heads(t_ref, G, 0, CHUNK)
        _, vjp = jax.vjp(lambda q, k, v, bg: _dn_intra(q, k, v, bg, known)[:6], q, k, v,
                         bg_ref[...].reshape(G, CHUNK, DN_D))
        lane_row = _iota((1, DN_D), 1)
        dcd = jnp.stack([jnp.sum(jnp.where(lane_row == h, dcd_ref[g], 0.0), axis=-1, keepdims=True)
                         for g in range(G) for h in range(DN_H)])
        dq, dk, dv, dbg = vjp((_stack_heads(du_ref, G, 0, DN_D), _stack_heads(dw_ref, G, 0, DN_D),
                               _stack_heads(dqd_ref, G, 0, DN_D), _stack_heads(dkd_ref, G, 0, DN_D),
                               _stack_heads(da_ref, G, 0, CHUNK), dcd))
        for g in range(G):
            rows = slice(g * CHUNK, (g + 1) * CHUNK)
            for h in range(DN_H):
                n = g * DN_H + h
                dqkv_ref[rows, h * DN_D:(h + 1) * DN_D] = dq[n]
                dqkv_ref[rows, DNW + h * DN_D:DNW + (h + 1) * DN_D] = dk[n]
                dqkv_ref[rows, 2 * DNW + h * DN_D:2 * DNW + (h + 1) * DN_D] = dv[n]
        dbg_ref[...] = dbg.reshape(G * CHUNK, DN_D)

    return _hosted_call(
        body, "dn_intra_bwd", (B, nc // G),
        in_specs=[_tok(rows, CONVW), _tok(rows, 128), _tok(rows, AW)] + [_tok(rows, DNW)] * 4 + [_tok(rows, AW), _cd_spec(G)],
        out_specs=[_tok(rows, CONVW), _tok(rows, 128)],
        out_shape=[_sds((B, S, CONVW), F32), _sds((B, S, 128), F32)],
        scratch_shapes=[], semantics=("parallel", "parallel"), ins=(qkv, bg, tinv, du, dw, dqd, dkd, da, dcd), ex=ex)


def _dn_prep_bwd(dn, y, ba, conv_w, alog, dtb, dqkv, dbg, bt, ex=None):
    B, S, _ = dn.shape
    nt = S // bt
    strip_rows = min(bt, 64)

    def rev(f):
        return pl.BlockSpec((None, bt, f), lambda b, i: (b, nt - 1 - i, 0))

    def body(x_ref, y_ref, ba_ref, cw_ref, al_ref, dt_ref, dqkv_ref, dbg_ref,
             dx_ref, dba_ref, dcw_ref, dal_ref, ddt_ref, dye_ref):
        i = pl.program_id(1)
        @pl.when(i == 0)
        def _():
            dye_ref[bt:bt + 8, :] = jnp.zeros((8, CONVW), F32)

        @pl.when(i > 0)
        def _():
            dye_ref[bt:bt + 8, :] = dye_ref[0:8, :]

        n_strips = bt // strip_rows

        def strip(k, carry):
            dal, ddt, dcw = carry
            r0 = pl.multiple_of((n_strips - 1 - k) * strip_rows, strip_rows)
            rows = pl.ds(r0, strip_rows)
            dcw_slabs = []
            for j in range(3 * DN_H):
                cols = slice(j * DN_D, (j + 1) * DN_D)
                _, vjp = jax.vjp(functools.partial(_dn_act, normalize=j < 2 * DN_H), y_ref[rows, cols].astype(F32))
                (dye_ref[rows, cols],) = vjp(dqkv_ref[rows, cols])
                window = dye_ref[pl.ds(r0, strip_rows + 8), cols]
                shifted = [window[3 - t:strip_rows + 3 - t] for t in range(CONV)]
                dx = cw_ref[0:1, cols] * shifted[0]
                for t in range(1, CONV):
                    dx = dx + cw_ref[t:t + 1, cols] * shifted[t]
                dx_ref[rows, cols] = dx.astype(BF)
                x = x_ref[rows, cols].astype(F32)
                dcw_slabs.append(jnp.concatenate([jnp.sum(shifted[t] * x, axis=0, keepdims=True) for t in range(CONV)], axis=0))
            _, vjp = jax.vjp(_dn_gates, ba_ref[rows, :], al_ref[...], dt_ref[...])
            dba_ref[rows, :], da, dd = vjp(dbg_ref[rows, :])
            return dal + da, ddt + dd, dcw + jnp.concatenate(dcw_slabs, axis=1)
        zero = jnp.zeros((1, DN_D), F32)
        dal, ddt, dcw = lax.fori_loop(0, n_strips, strip, (zero, zero, jnp.zeros((CONV, CONVW), F32)))
        first = (i == 0) & (pl.program_id(0) == 0)
        _acc(dcw_ref, dcw, first)
        _acc(dal_ref, dal, first)
        _acc(ddt_ref, ddt, first)

    return _hosted_call(
        body, "dn_prep_bwd", (B, nt),
        in_specs=[rev(CONVW), rev(CONVW), rev(128), _full((CONV, CONVW)), _full((1, 128)), _full((1, 128)), rev(CONVW), rev(128)],
        out_specs=[rev(CONVW), rev(128), _full((CONV, CONVW)), _full((1, 128)), _full((1, 128))],
        out_shape=[_sds((B, S, CONVW), BF), _sds((B, S, 128), F32), _sds((CONV, CONVW), F32), _sds((1, 128), F32),
                   _sds((1, 128), F32)],
        scratch_shapes=[pltpu.VMEM((bt + 8, CONVW), F32)],
        semantics=("arbitrary", "arbitrary"), ins=(dn, y, ba, conv_w, alog, dtb, dqkv, dbg), ex=ex)


def _attn_bwd(qkv, sinks, d_o, ex=None):
    B, S, _ = qkv.shape
    nb = S // BLK

    def cur(f):
        return pl.BlockSpec((None, BLK, f), lambda b, i: (b, jnp.minimum(i, nb - 1), 0))

    def out_prev(f):
        return pl.BlockSpec((None, BLK, f), lambda b, i: (b, jnp.maximum(i - 1, 0), 0))

    def body(qkv_ref, kvp_ref, sk_ref, do_ref, dq_ref, dkv_ref, dsk_ref, carry_ref):
        n = pl.program_id(1)
        first = (n == 0) & (pl.program_id(0) == 0)

        @pl.when(n == 0)
        def _():
            carry_ref[...] = jnp.zeros_like(carry_ref)

        @pl.when(n < nb)
        def _():
            qs, kc, kp, vc, vp = _attn_load(qkv_ref, kvp_ref)

            def f(qs, kc, kp, vc, vp, sk):
                return _attn_block(qs, kc, kp, vc, vp, sk, n > 0)
            _, vjp = jax.vjp(f, qs, kc, kp, vc, vp, sk_ref[...])
            d_outs = [do_ref[:, h * HD:(h + 1) * HD] for h in range(HEADS)]
            dqs, dkc, dkp, dvc, dvp, dsk = vjp(d_outs)
            for h in range(HEADS):
                dq_ref[:, h * HD:(h + 1) * HD] = dqs[h].astype(BF)
            for h in range(KV_HEADS):
                ksl = slice(h * HD, (h + 1) * HD)
                vsl = slice(KVW + h * HD, KVW + (h + 1) * HD)
                dkv_ref[:, ksl] = (carry_ref[:, ksl] + dkp[h]).astype(BF)
                dkv_ref[:, vsl] = (carry_ref[:, vsl] + dvp[h]).astype(BF)
                carry_ref[:, ksl] = dkc[h]
                carry_ref[:, vsl] = dvc[h]
            _acc(dsk_ref, dsk, first)

        @pl.when(n == nb)
        def _():
            dkv_ref[...] = carry_ref[...].astype(BF)

    return _hosted_call(
        body, "attn_bwd", (B, nb + 1),
        in_specs=[cur(QKV), _kv_prev_spec(lambda i: jnp.maximum(jnp.minimum(i, nb - 1) - 1, 0)), _full((1, HEADS)), cur(QW)],
        out_specs=[cur(QW), out_prev(2 * KVW), _full((1, HEADS))],
        out_shape=[_sds((B, S, QW), BF), _sds((B, S, 2 * KVW), BF), _sds((1, HEADS), F32)],
        scratch_shapes=[pltpu.VMEM((BLK, 2 * KVW), F32)],
        semantics=("arbitrary", "arbitrary"), ins=(qkv, qkv, sinks, d_o), ex=ex)


def _in_proj_bwd(x, mod, norm1_g, w_in, pieces, dba, dx1, bt):
    B, S, _ = x.shape
    widths = (QW, 2 * KVW, CONVW, DNW, D, D)

    def body(x_ref, mod_ref, g_ref, w_ref, dq_ref, dkv_ref, ddn_ref, dz_ref, dga_ref, dgb_ref, dba_ref, dx1_ref,
             gx_ref, dp_ref, dmod_ref, dg_ref):
        dp = jnp.concatenate([r[...] for r in (dq_ref, dkv_ref, ddn_ref, dz_ref, dga_ref, dgb_ref)]
                             + [dba_ref[...].astype(BF)], axis=-1)
        dp_ref[...] = dp
        dh = lax.dot_general(dp, w_ref[...], (((1,), (1,)), ((), ())), preferred_element_type=F32)
        _, vjp = jax.vjp(_norm_mod, x_ref[...], g_ref[...], mod_ref[:, 0:D], mod_ref[:, D:2 * D])
        dx, dgain, dshift, dscale = vjp(dh)
        gx_ref[...] = dx + dx1_ref[...]
        first = pl.program_id(1) == 0
        _acc(dmod_ref, jnp.concatenate([dshift, dscale], axis=-1), first)
        _acc(dg_ref, dgain, first)

    return pl.pallas_call(
        body, name="in_proj_bwd", grid=(B, S // bt),
        in_specs=[_tok(bt, D), _per_batch(6 * D), _full((1, D)), _resident((D, IN_PAD))] + [_tok(bt, w) for w in widths]
        + [_tok(bt, 128), _tok(bt, D)],
        out_specs=[_tok(bt, D), _tok(bt, IN_PAD), _per_batch(2 * D), _per_batch(D)],
        out_shape=[_sds((B, S, D), F32), _sds((B, S, IN_PAD), BF), _sds((B, 1, 2 * D), F32), _sds((B, 1, D), F32)],
        compiler_params=_cparams(dimension_semantics=("parallel", "arbitrary")),
    )(x, mod, norm1_g, w_in, *pieces, dba, dx1)


def _matmul_tn(tag, a, b, bk, bn, bt, col_blocks=False, ex=None):
    T, K = a.shape
    N = b.shape[1]
    nt = T // bt
    if col_blocks:
        assert bk == K
        out_spec = pl.BlockSpec((None, bk, bn), lambda i, j, t: (j, 0, 0))
        out_shape = _sds((N // bn, K, bn), F32)
    else:
        out_spec = pl.BlockSpec((bk, bn), lambda i, j, t: (i, j))
        out_shape = _sds((K, N), F32)

    def body(a_ref, b_ref, o_ref, acc_ref):
        t = pl.program_id(2)

        @pl.when(t == 0)
        def _():
            acc_ref[...] = jnp.zeros_like(acc_ref)
        acc_ref[...] += lax.dot_general(a_ref[...], b_ref[...], (((0,), (0,)), ((), ())), preferred_element_type=F32)

        @pl.when(t == nt - 1)
        def _():
            o_ref[...] = acc_ref[...]

    (out,), landed = _hosted_call(
        body, f"grad_{tag}", (K // bk, N // bn, nt),
        in_specs=[pl.BlockSpec((bt, bk), lambda i, j, t: (t, i)), pl.BlockSpec((bt, bn), lambda i, j, t: (t, j))],
        out_specs=[out_spec], out_shape=[out_shape],
        scratch_shapes=[pltpu.VMEM((bk, bn), F32)],
        semantics=("parallel", "parallel", "arbitrary"), ins=(a, b), ex=ex)
    return out if ex is None else (out, landed)


def _rope_table(positions):
    inv_freq = THETA ** (-jnp.arange(0, ROT, 2, dtype=F32) / ROT)
    rest = jnp.zeros((HD - ROT,), F32)
    freq = jnp.concatenate([inv_freq, inv_freq, rest] * 2)
    sign = jnp.concatenate([-jnp.ones_like(inv_freq), jnp.ones_like(inv_freq), rest] * 2)
    ang = positions.astype(F32)[..., None] * freq
    return jnp.cos(ang), jnp.sin(ang) * sign


def _lane_pad(v, offset, width=128):
    return jnp.zeros((1, width), F32).at[0, offset:offset + v.shape[-1]].set(v.reshape(-1))


def _tile(S, want):
    return min(S, want)


class _Hosted:
    def __init__(self, call):
        self.call = call
        self.outs = None

    def __call__(self, ex):
        self.outs, landed = self.call(ex)
        return landed


def _local_step(x, mod, positions, tgt, norm1_g, w_in_pad, conv_w, q_norm_g, k_norm_g, sinks, a_log, dt_bias,
                dn_norm_g, w_branch, w_out, norm2_g, w_gu, w_dn, dist=None):
    B, S, _ = x.shape
    T = B * S
    cos_t, sin_t = _rope_table(positions)
    qg2 = jnp.concatenate([q_norm_g, q_norm_g], axis=-1)
    kg2 = jnp.concatenate([k_norm_g, k_norm_g], axis=-1)
    alog = _lane_pad(a_log, DN_H)
    dtb = _lane_pad(dt_bias, DN_H)
    conv2 = conv_w.reshape(CONV, CONVW)
    bt = _tile(S, 512)
    bt_mlp = _tile(S, 256)

    q, kv, dn, z, ga, gb, ba, h1 = _in_proj(x, mod, norm1_g, w_in_pad, bt)
    qkv_n = _qk_prep_fwd(q, kv, cos_t, sin_t, qg2, kg2, bt)
    o_attn = _attn_fwd(qkv_n, sinks)
    dqkv, bg, dn_y = _dn_prep(dn, ba, conv2, alog, dtb, bt)
    intra = _Hosted(lambda ex: _dn_intra_fwd(dqkv, bg, ex))
    if dist is None:
        intra(None)
    else:
        f_br, f_out, w_gu, f_dn = _gather_weights("late", [w_branch, w_out, w_gu, w_dn], dist[0], host=intra)
        w_branch, w_out, w_dn = (f.reshape(N_CHIP * f.shape[1], f.shape[2]) for f in (f_br, f_out, f_dn))
    dn_u, dn_w, dn_qd, dn_kd, dn_a, dn_cd, dn_tinv = intra.outs
    (o_raw, states), _ = _dn_rec_fwd(dn_u, dn_w, dn_qd, dn_kd, dn_a, dn_cd)
    x1, o_dn, merged = _mix_fwd(o_attn, o_raw, z, ga, gb, x, mod, dn_norm_g, w_branch, w_out, bt)
    dx1, h2, act, dgu, dyy, loss, dmod2, dnorm2 = _mlp(x1, tgt, mod, norm2_g, w_gu, w_dn, bt_mlp)

    def flat(t):
        return t.reshape(T, t.shape[-1])
    tn = functools.partial(_matmul_tn, bt=_tile(T, 1024))
    g_w_dn = tn("w_down", flat(act), flat(dyy), bk=FFN, bn=D // 2)
    g_w_gu = tn("w_gate_up", flat(h2), flat(dgu), bk=D, bn=2 * FFN // N_CHIP, col_blocks=True)

    mix_b = _Hosted(lambda ex: _mix_bwd(o_attn, o_raw, z, ga, gb, x, mod, dn_norm_g, w_branch, w_out, dx1, bt_mlp, ex))
    rec_b = _Hosted(lambda ex: _dn_rec_bwd(dn_u, dn_w, dn_qd, dn_kd, dn_a, dn_cd, states, mix_b.outs[1], ex))
    intra_b = _Hosted(lambda ex: _dn_intra_bwd(dqkv, bg, dn_tinv, *rec_b.outs, ex))
    if dist is None:
        for host in (mix_b, rec_b, intra_b):
            host(None)
    else:
        g_w_gu, g_w_dn = _reduce_grads(("w_gate_up", "w_down"), [g_w_gu, g_w_dn.reshape(N_CHIP, -1, D)], *dist,
                                       hosts=[mix_b, rec_b, intra_b])
    d_oa, _, dz, dga, dgb, dya, dyd, dout, dgate1, ddn_g = mix_b.outs
    d_dqkv, dbg = intra_b.outs
    g_w_out = tn("w_out", flat(merged), flat(dout), bk=D, bn=D)
    g_w_br = jnp.concatenate([tn("w_branch_attn", flat(o_attn), flat(dya), bk=QW, bn=D),
                              tn("w_branch_dn", flat(o_dn), flat(dyd), bk=DNW, bn=D)], axis=0)
    prep_b = _Hosted(lambda ex: _dn_prep_bwd(dn, dn_y, ba, conv2, alog, dtb, d_dqkv, dbg, bt, ex))
    attn_b = _Hosted(lambda ex: _attn_bwd(qkv_n, sinks, d_oa, ex))
    qk_b = _Hosted(lambda ex: _qk_prep_bwd(q, kv, cos_t, sin_t, qg2, kg2, *attn_b.outs[:2], bt, ex))
    if dist is None:
        for host in (prep_b, attn_b, qk_b):
            host(None)
    else:
        g_w_br, g_w_out = _reduce_grads(("w_branch", "w_out"), [g_w_br.reshape(N_CHIP, -1, D), g_w_out.reshape(N_CHIP, -1, D)],
                                        *dist, hosts=[prep_b, attn_b, qk_b])
    d_dn, dba, dconv, dalog, ddtb = prep_b.outs
    dsk = attn_b.outs[2]
    dq, dkv, dqg2, dkg2 = qk_b.outs
    dqg = dqg2[:, :HD] + dqg2[:, HD:]
    dkg = dkg2[:, :HD] + dkg2[:, HD:]
    grad_x, dproj, dmod1, dnorm1 = _in_proj_bwd(x, mod, norm1_g, w_in_pad, (dq, dkv, d_dn, dz, dga, dgb), dba, dx1, bt)
    g_w_in = _Hosted(lambda ex: (tn("w_in", flat(h1), flat(dproj), bk=D, bn=IN_PAD // 3), ()) if ex is None
                     else tn("w_in", flat(h1), flat(dproj), bk=D, bn=IN_PAD // 3, ex=ex))
    if dist is None:
        g_w_in(None)
        g_w_in = g_w_in.outs

    dmod = jnp.concatenate([dmod1, dgate1, dmod2], axis=-1)
    small = dict(norm1_g=jnp.sum(dnorm1, axis=0), norm2_g=jnp.sum(dnorm2, axis=0), q_norm_g=dqg, k_norm_g=dkg,
                 sinks=dsk, a_log=dalog[:, DN_H:2 * DN_H], dt_bias=ddtb[:, DN_H:2 * DN_H],
                 dn_norm_g=jnp.sum(ddn_g, axis=0), conv_w=dconv)
    return jnp.sum(loss), grad_x, dmod, small, (g_w_in, g_w_br, g_w_out, g_w_gu, g_w_dn)


def _flip(me, f):
    return (me[0] ^ ((f >> 2) & 1), me[1] ^ ((f >> 1) & 1), me[2] ^ (f & 1))


def _comm_call(name, ex):
    n_in, n_out = len(ex.ins), len(ex.out_shapes)

    def body(*refs):
        out_refs, sems = refs[n_in:n_in + n_out], refs[n_in + n_out:]
        cps = _exchange_copies(ex, refs[:n_in], out_refs, sems[0], sems[1])
        for cp in cps:
            cp.start()
        for cp in cps:
            cp.wait_recv()
        if ex.n_forward:
            fwd = _forward_copies(ex, out_refs, sems[2], sems[3])
            for cp in fwd:
                cp.start()
            for cp in fwd:
                cp.wait_recv()
            cps = cps + fwd
        for cp in cps:
            cp.wait_send()

    any_spec = pl.BlockSpec(memory_space=pl.ANY)
    return pl.pallas_call(
        body, name=name, in_specs=[any_spec] * n_in, out_specs=[any_spec] * n_out, out_shape=list(ex.out_shapes),
        scratch_shapes=_exchange_sems(ex),
    )(*ex.ins)


def _by_origin(own, received, index):
    stack = jnp.concatenate([own[None], received], axis=0)
    n = stack.shape[0]
    return jnp.stack([lax.dynamic_index_in_dim(stack, k ^ index, 0, keepdims=False) for k in range(n)])


def _gather_devices(name, arrs, dev, host=None):
    def plan(me, in_refs, out_refs):
        return [(a, o.at[f - 1], _flip(me, f)) for a, o in zip(in_refs, out_refs) for f in range(1, N_DEV)]
    outs = tuple(_sds((N_DEV - 1,) + a.shape, a.dtype) for a in arrs)
    got = (host or functools.partial(_comm_call, name))(_Exchange(tuple(arrs), outs, (N_DEV - 1) * len(arrs), plan))
    return [_by_origin(a, g, dev) for a, g in zip(arrs, got)]


def _gather_chips(name, arrs, chip):
    def plan(me, in_refs, out_refs):
        return [(a, o.at[j], _flip(me, 2 * (j + 1))) for a, o in zip(in_refs, out_refs) for j in range(N_CHIP - 1)]
    outs = tuple(_sds((N_CHIP - 1,) + a.shape, a.dtype) for a in arrs)
    got = _comm_call(name, _Exchange(tuple(arrs), outs, (N_CHIP - 1) * len(arrs), plan))
    return [_by_origin(a, g, chip) for a, g in zip(arrs, got)]


def _halves(core, mine, other):
    lo = jnp.where(core == 0, mine, other)
    hi = jnp.where(core == 0, other, mine)
    return jnp.concatenate([lo, hi], axis=-2)


def _swap_cores_ex(arrs):
    def plan(me, in_refs, out_refs):
        return [(g, o, _flip(me, 1)) for g, o in zip(in_refs, out_refs)]
    return _Exchange(tuple(arrs), tuple(_sds(g.shape, g.dtype) for g in arrs), len(arrs), plan)


def _gather_weights(tag, shards, chip, host=None):
    def plan(me, in_refs, out_refs):
        chip_me = 2 * me[0] + me[1]
        remote = []
        for a, o in zip(in_refs, out_refs):
            half = a.shape[0] // 2
            mine = a.at[pl.ds(me[2] * half, half)]
            remote += [(mine, o.at[chip_me, me[2]], _flip(me, 2 * (j + 1))) for j in range(N_CHIP - 1)]
        return remote

    def forward(me, out_refs):
        chip_me = 2 * me[0] + me[1]
        return [(o.at[chip_me ^ (j + 1), me[2]], o.at[chip_me ^ (j + 1), me[2]], _flip(me, 1))
                for o in out_refs for j in range(N_CHIP - 1)]
    run = host or functools.partial(_comm_call, f"weights_{tag}")
    n = (N_CHIP - 1) * len(shards)
    landed = run(_Exchange(tuple(shards), tuple(_sds((N_CHIP, 2, a.shape[0] // 2, a.shape[1]), a.dtype) for a in shards),
                           n, plan, n, forward))
    return [lax.dynamic_update_slice(f.reshape((N_CHIP,) + a.shape), a[None], (chip, 0, 0)) for a, f in zip(shards, landed)]


def _rows(r):
    for br in (512, 352, 256, 128, 64, 32, 16, 8):
        if r % br == 0:
            return br
    raise ValueError(r)


def _pair_add(tag, g, recv, c):
    n, r, cols = g.shape
    half = r // 2
    br = _rows(half)
    nb = half // br

    def body(c_ref, g_ref, r_ref, o_ref):
        o_ref[...] = (g_ref[...] + r_ref[...]).astype(BF)

    return pl.pallas_call(
        body, name=f"pair_add_{tag}",
        grid_spec=pltpu.PrefetchScalarGridSpec(
            num_scalar_prefetch=1, grid=(n, nb),
            in_specs=[pl.BlockSpec((None, br, cols), lambda k, i, c_ref: (k, c_ref[0] * nb + i, 0)),
                      pl.BlockSpec((None, br, cols), lambda k, i, c_ref: (k, i, 0))],
            out_specs=pl.BlockSpec((None, br, cols), lambda k, i, c_ref: (k, i, 0))),
        out_shape=_sds((n, half, cols), BF),
        compiler_params=_cparams(dimension_semantics=("parallel", "parallel")),
    )(c, g, recv)


def _sum_chips(tag, p, q, chip):
    n, r, cols = q.shape
    br = _rows(r)

    def body(chip_ref, p_ref, q_ref, o_ref):
        acc = p_ref[...].astype(F32)
        for k in range(n):
            acc = acc + q_ref[k].astype(F32)
        o_ref[...] = acc

    return pl.pallas_call(
        body, name=f"sum_chips_{tag}",
        grid_spec=pltpu.PrefetchScalarGridSpec(
            num_scalar_prefetch=1, grid=(r // br,),
            in_specs=[pl.BlockSpec((None, br, cols), lambda i, chip_ref: (chip_ref[0], i, 0)),
                      pl.BlockSpec((n, br, cols), lambda i, chip_ref: (0, i, 0))],
            out_specs=pl.BlockSpec((br, cols), lambda i, chip_ref: (i, 0))),
        out_shape=_sds((r, cols), F32),
        compiler_params=_cparams(dimension_semantics=("parallel",)),
    )(chip, p, q)


def _reduce_grads(tags, grads, chip, core, hosts=None):
    core_arr = core.reshape(1).astype(jnp.int32)
    chip_arr = chip.reshape(1).astype(jnp.int32)
    name = "_".join(tags)
    run = hosts or [functools.partial(_comm_call, f"grads_{stage}_{name}") for stage in ("pair", "chips", "swap")]

    def plan_pair(me, in_refs, out_refs):
        remote = []
        for g, o in zip(in_refs, out_refs):
            half = g.shape[1] // 2
            remote += [(g.at[k, pl.ds((1 - me[2]) * half, half)], o.at[k], _flip(me, 1)) for k in range(N_CHIP)]
        return remote
    recv = run[0](_Exchange(tuple(grads), tuple(_sds((N_CHIP, g.shape[1] // 2, g.shape[2]), F32) for g in grads),
                            N_CHIP * len(grads), plan_pair))
    pair = [_pair_add(t, g, r, core_arr) for t, g, r in zip(tags, grads, recv)]

    def plan_chips(me, in_refs, out_refs):
        remote = []
        for p, o in zip(in_refs, out_refs):
            for j in range(N_CHIP - 1):
                peer = _flip(me, 2 * (j + 1))
                remote.append((p.at[2 * peer[0] + peer[1]], o.at[j], peer))
        return remote
    parts = run[1](_Exchange(tuple(pair), tuple(_sds((N_CHIP - 1,) + p.shape[1:], BF) for p in pair),
                             (N_CHIP - 1) * len(pair), plan_chips))
    mine = [_sum_chips(t, p, q, chip_arr) for t, p, q in zip(tags, pair, parts)]
    other = run[2](_swap_cores_ex(mine))
    return [_halves(core, h, o) for h, o in zip(mine, other)]


def _adamw_math(w, g, m, v):
    m = ADAM_B1 * m + (1.0 - ADAM_B1) * g
    v = ADAM_B2 * v + (1.0 - ADAM_B2) * (g * g)
    m_hat = m / (1.0 - ADAM_B1 ** ADAM_STEP)
    v_hat = v / (1.0 - ADAM_B2 ** ADAM_STEP)
    delta = -ADAM_LR * (m_hat / (jnp.sqrt(v_hat) + ADAM_EPS) + ADAM_WD * w)
    return delta, m, v


def _adamw(name, w, g, m, v, ex=None):
    r, cols = w.shape
    br = _rows(r)
    if br * cols * 4 > (1 << 20) and br % 16 == 0:
        br //= 2

    def body(w_ref, g_ref, m_ref, v_ref, d_ref, mo_ref, vo_ref):
        d_ref[...], mo_ref[...], vo_ref[...] = _adamw_math(w_ref[...], g_ref[...], m_ref[...], v_ref[...])

    spec = pl.BlockSpec((br, cols), lambda i: (i, 0))
    outs, landed = _hosted_call(
        body, f"adamw_{name}", (r // br,), in_specs=[spec] * 4, out_specs=[spec] * 3,
        out_shape=[_sds((r, cols), F32)] * 3, scratch_shapes=[], semantics=("parallel",), ins=(w, g, m, v), ex=ex)
    return outs if ex is None else (outs, landed)


def _ada_fwd(c_all, ada_w, ada_b_cols):
    n = c_all.shape[0]

    def body(c_ref, w_ref, b_ref, o_ref):
        o_ref[...] = _mmx(_silu(c_ref[...]), w_ref[...]) + b_ref[...]

    return pl.pallas_call(
        body, name="ada_fwd", out_shape=_sds((n, ada_w.shape[1]), F32), compiler_params=_cparams(),
    )(c_all, ada_w, ada_b_cols)


def _ada_bwd(c_all, dmod_cols, w, m, v, ex=None):
    n = c_all.shape[0]
    r, cols = w.shape
    br = 128

    def body(c_ref, d_ref, w_ref, m_ref, v_ref, g_ref, dl_ref, mo_ref, vo_ref):
        cond = _silu(c_ref[...])
        g = lax.dot_general(cond, d_ref[...], (((0,), (0,)), ((), ())), precision=lax.Precision.HIGHEST,
                            preferred_element_type=F32)
        g_ref[...] = g
        dl_ref[...], mo_ref[...], vo_ref[...] = _adamw_math(w_ref[...], g, m_ref[...], v_ref[...])

    spec = pl.BlockSpec((br, cols), lambda i: (i, 0))
    outs, landed = _hosted_call(
        body, "ada_bwd", (r // br,),
        in_specs=[pl.BlockSpec((n, br), lambda i: (0, i)), pl.BlockSpec((n, cols), lambda i: (0, 0)), spec, spec, spec],
        out_specs=[spec] * 4, out_shape=[_sds((r, cols), F32)] * 4, scratch_shapes=[], semantics=("parallel",),
        ins=(c_all, dmod_cols, w, m, v), ex=ex)
    return outs if ex is None else (outs, landed)


def _sum_devices(parts):
    n, r, cols = parts.shape

    def body(p_ref, o_ref):
        acc = p_ref[0]
        for k in range(1, n):
            acc = acc + p_ref[k]
        o_ref[...] = acc

    return pl.pallas_call(body, name="sum_devices", out_shape=_sds((r, cols), F32), compiler_params=_cparams())(parts)


SMALL_ROWS = 16
_SMALL_SLOTS = dict(norm1_g=(0, 0, D), norm2_g=(1, 0, D), q_norm_g=(2, 0, HD), k_norm_g=(2, 128, HD), sinks=(2, 256, HEADS),
                    a_log=(2, 384, DN_H), dt_bias=(2, 512, DN_H), dn_norm_g=(2, 640, DN_D))
_CONV_ROW = 4
_ADA_B_ROW = 8


def _pack_small(vals, conv, ada_b):
    def row(pieces):
        out, at = [], 0
        for col, val in pieces:
            out += [jnp.zeros((1, col - at), F32), val.reshape(1, -1)]
            at = col + val.size
        return jnp.concatenate(out + [jnp.zeros((1, CONVW - at), F32)], axis=1)
    rows = {}
    for name, (r, col, n) in _SMALL_SLOTS.items():
        rows.setdefault(r, []).append((col, vals[name]))
    blank = jnp.zeros((1, CONVW), F32)
    top = [row(sorted(rows[r], key=lambda p: p[0])) if r in rows else blank for r in range(_CONV_ROW)]
    conv_rows = jnp.concatenate([conv, jnp.zeros((CONV, CONVW - conv.shape[1]), F32)], axis=1)
    tail = jnp.zeros((SMALL_ROWS - _ADA_B_ROW - 4, CONVW), F32)
    return jnp.concatenate(top + [conv_rows, ada_b.reshape(4, CONVW), tail], axis=0)


def _unpack_small(sheet, conv_cols):
    out = {name: sheet[row, col:col + n].reshape(1, n) for name, (row, col, n) in _SMALL_SLOTS.items()}
    out["conv_w"] = sheet[_CONV_ROW:_CONV_ROW + CONV, 0:conv_cols].reshape(1, CONV, 1, conv_cols)
    out["ada_b"] = sheet[_ADA_B_ROW:_ADA_B_ROW + 4, :].reshape(1, 6 * D)
    return out


def _w_in_segments():
    shard = IN_WIDTH // N_CHIP
    cuts = sorted({0, IN_WIDTH, C_Z, C_Z + 2 * DN_H} | {k * shard for k in range(1, N_CHIP)})
    segs = []
    for a, b in zip(cuts[:-1], cuts[1:]):
        k = a // shard
        pad = a if a < C_Z else (C_BA + a - C_Z if a < C_Z + 2 * DN_H else a - 2 * DN_H)
        segs.append((k, a - k * shard, b - k * shard, pad))
    return segs


def _pad_w_in(f):
    parts = [f[k][:, lo:hi] for k, lo, hi, _ in sorted(_w_in_segments(), key=lambda s: s[3])]
    return jnp.concatenate(parts + [jnp.zeros((f.shape[1], IN_PAD - IN_WIDTH), f.dtype)], axis=1)


def _unpad_w_in(g):
    return jnp.stack([jnp.concatenate([g[:, pad:pad + hi - lo] for kk, lo, hi, pad in _w_in_segments() if kk == k], axis=1)
                      for k in range(N_CHIP)])


def _blocks_to_cols(f):
    return f.transpose(1, 0, 2).reshape(f.shape[1], N_CHIP * f.shape[2])


def kernel(x, c, positions, ada_w, ada_b, norm1_g, w_in, conv_w, q_norm_g, k_norm_g, sinks, a_log, dt_bias, dn_norm_g, w_branch, w_out, norm2_g, w_gate_up, w_down, loss_target, m_ada_w, m_ada_b, m_norm1_g, m_w_in, m_conv_w, m_q_norm_g, m_k_norm_g, m_sinks, m_a_log, m_dt_bias, m_dn_norm_g, m_w_branch, m_w_out, m_norm2_g, m_w_gate_up, m_w_down, v_ada_w, v_ada_b, v_norm1_g, v_w_in, v_conv_w, v_q_norm_g, v_k_norm_g, v_sinks, v_a_log, v_dt_bias, v_dn_norm_g, v_w_branch, v_w_out, v_norm2_g, v_w_gate_up, v_w_down):
    ix, iy, ic = lax.axis_index("x"), lax.axis_index("y"), lax.axis_index("c")
    dev = 4 * ix + 2 * iy + ic
    chip = 2 * ix + iy
    n_seq = x.shape[0]
    conv_cols = conv_w.shape[-1]

    c_all, conv_all = _gather_devices("gather_cond", [c, conv_w.reshape(CONV, conv_cols)], dev)
    c_all = c_all.reshape(N_DEV * n_seq, D)
    ada_cols = ada_w.shape[-1]
    ada_b_cols = lax.dynamic_slice(ada_b, (0, chip * ada_cols), (1, ada_cols))
    mod_cols = _ada_fwd(c_all, ada_w[0], ada_b_cols)
    (mod_blocks,) = _gather_chips("gather_mod", [mod_cols], chip)
    mod_all = _blocks_to_cols(mod_blocks)
    mod = lax.dynamic_slice(mod_all, (dev * n_seq, 0), (n_seq, 6 * D)).reshape(n_seq, 1, 6 * D)
    conv_full = _blocks_to_cols(conv_all[0::2])

    (f_in,) = _gather_weights("w_in", [w_in[0].astype(BF)], chip)
    w_in_pad = _pad_w_in(f_in)

    loss, grad_x, dmod, small, (w_in_grad, r_br, r_out, r_gu, r_dn) = _local_step(
        x, mod, positions, loss_target, norm1_g, w_in_pad, conv_full.reshape(CONV, 1, CONVW), q_norm_g, k_norm_g, sinks,
        a_log, dt_bias, dn_norm_g, w_branch[0].astype(BF), w_out[0].astype(BF), norm2_g, w_gate_up[0].astype(BF),
        w_down[0].astype(BF), dist=(chip, ic))
    loss = lax.psum(loss, ("x", "y", "c"))

    part = _pack_small(small, small["conv_w"], jnp.sum(dmod, axis=(0, 1)).reshape(1, 6 * D))
    dmod_all, parts = _gather_devices("gather_small", [dmod.reshape(n_seq, 6 * D), part], dev, host=w_in_grad)
    dmod_all = dmod_all.reshape(N_DEV * n_seq, 6 * D)
    dmod_cols = lax.dynamic_slice(dmod_all, (0, chip * ada_cols), (N_DEV * n_seq, ada_cols))

    up_gu = _Hosted(lambda ex: _adamw("w_gate_up", w_gate_up[0], r_gu, m_w_gate_up[0], v_w_gate_up[0], ex))
    up_ada = _Hosted(lambda ex: _ada_bwd(c_all, dmod_cols, ada_w[0], m_ada_w[0], v_ada_w[0], ex))
    up_dn = _Hosted(lambda ex: _adamw("w_down", w_down[0], r_dn, m_w_down[0], v_w_down[0], ex))
    (r_in,) = _reduce_grads(("w_in",), [_unpad_w_in(w_in_grad.outs)], chip, ic, hosts=[up_gu, up_ada, up_dn])
    ada = up_ada.outs
    big = {"w_gate_up": (r_gu,) + tuple(up_gu.outs), "w_down": (r_dn,) + tuple(up_dn.outs)}
    for name, w, g, m, v in (("w_in", w_in, r_in, m_w_in, v_w_in), ("w_branch", w_branch, r_br, m_w_branch, v_w_branch),
                             ("w_out", w_out, r_out, m_w_out, v_w_out)):
        big[name] = (g,) + tuple(_adamw(name, w[0], g, m[0], v[0]))
    g_small = _unpack_small(_sum_devices(parts), CONVW)
    g_conv = lax.dynamic_slice(g_small["conv_w"].reshape(CONV, CONVW), (0, chip * conv_cols), (CONV, conv_cols))
    g_small["conv_w"] = g_conv.reshape(1, CONV, 1, conv_cols)

    given = dict(norm1_g=(norm1_g, m_norm1_g, v_norm1_g), norm2_g=(norm2_g, m_norm2_g, v_norm2_g),
                 q_norm_g=(q_norm_g, m_q_norm_g, v_q_norm_g), k_norm_g=(k_norm_g, m_k_norm_g, v_k_norm_g),
                 sinks=(sinks, m_sinks, v_sinks), a_log=(a_log, m_a_log, v_a_log), dt_bias=(dt_bias, m_dt_bias, v_dt_bias),
                 dn_norm_g=(dn_norm_g, m_dn_norm_g, v_dn_norm_g))
    sheets = [_pack_small({k: t[j] for k, t in given.items()}, cw.reshape(CONV, conv_cols), ab)
              for j, (cw, ab) in enumerate(((conv_w, ada_b), (m_conv_w, m_ada_b), (v_conv_w, v_ada_b)))]
    g_local = _pack_small(g_small, g_conv, g_small["ada_b"])
    upd = [_unpack_small(s, conv_cols) for s in _adamw("small", sheets[0], g_local, sheets[1], sheets[2])]

    names = ["ada_w", "ada_b", "norm1_g", "w_in", "conv_w", "q_norm_g", "k_norm_g", "sinks", "a_log", "dt_bias", "dn_norm_g",
             "w_branch", "w_out", "norm2_g", "w_gate_up", "w_down"]

    def leaf(name, j):
        if name == "ada_w":
            return ada[j][None]
        if name in big:
            return big[name][j][None]
        return g_small[name] if j == 0 else upd[j - 1][name]

    return (loss, grad_x) + tuple(leaf(n, j) for j in range(4) for n in names)
```

```python
import functools
from typing import Callable, NamedTuple

import jax
import jax.numpy as jnp
import numpy as np
from jax import lax
from jax.experimental import pallas as pl
from jax.experimental.pallas import tpu as pltpu

F32 = jnp.float32
BF = jnp.bfloat16

D = 1024
HEADS = 8
KV_HEADS = 2
GROUP = 4
HD = 64
BLK = 128
ROT = 16
THETA = 500000.0
QW = 512
KVW = 128
DN_H = 4
DN_D = 128
CONV = 4
CHUNK = 64
DNW = 512
CONVW = 1536
FFN = 2816
EPS = 1e-6
IN_WIDTH = 4872
IN_PAD = 4992
C_KV = 512
C_DN = 768
C_Z = 2304
C_GA = 2816
C_GB = 3840
C_BA = 4864
NEG = -1e30
N_DEV = 8
N_CHIP = 4

ADAM_LR = 0.001
ADAM_B1 = 0.9
ADAM_B2 = 0.999
ADAM_EPS = 1e-08
ADAM_WD = 0.01
ADAM_STEP = 10

VMEM_LIMIT = 60 * 1024 * 1024


def _cparams(**kw):
    return pltpu.CompilerParams(vmem_limit_bytes=VMEM_LIMIT, **kw)


def _dg(a, b, ca, cb):
    return lax.dot_general(a.astype(BF), b.astype(BF), (((ca,), (cb,)), ((), ())),
                           preferred_element_type=F32)


@jax.custom_vjp
def _mm(a, b):
    return _dg(a, b, 1, 0)


def _mm_fwd(a, b):
    return _dg(a, b, 1, 0), (a, b)


def _mm_bwd(res, dy):
    a, b = res
    return _dg(dy, b, 1, 1).astype(a.dtype), _dg(a, dy, 0, 0).astype(b.dtype)


_mm.defvjp(_mm_fwd, _mm_bwd)


@jax.custom_vjp
def _mm_nt(a, b):
    return _dg(a, b, 1, 1)


def _mm_nt_fwd(a, b):
    return _dg(a, b, 1, 1), (a, b)


def _mm_nt_bwd(res, dy):
    a, b = res
    return _dg(dy, b, 1, 0).astype(a.dtype), _dg(dy, a, 0, 0).astype(b.dtype)


_mm_nt.defvjp(_mm_nt_fwd, _mm_nt_bwd)


@jax.custom_vjp
def _mm_tn(a, b):
    return _dg(a, b, 0, 0)


def _mm_tn_fwd(a, b):
    return _dg(a, b, 0, 0), (a, b)


def _mm_tn_bwd(res, dy):
    a, b = res
    return _dg(b, dy, 1, 1).astype(a.dtype), _dg(a, dy, 1, 0).astype(b.dtype)


_mm_tn.defvjp(_mm_tn_fwd, _mm_tn_bwd)


def _mmx(a, b):
    return jnp.dot(a, b, precision=lax.Precision.HIGHEST, preferred_element_type=F32)


def _mmx_nt(a, b):
    return lax.dot_general(a, b, (((1,), (1,)), ((), ())), precision=lax.Precision.HIGHEST,
                           preferred_element_type=F32)


def _iota(shape, dim):
    return lax.broadcasted_iota(jnp.int32, shape, dim)


def _sigmoid(x):
    return lax.logistic(x)


def _silu(x):
    return x * _sigmoid(x)


def _softplus(x):
    return jnp.maximum(x, 0.0) + jnp.log(1.0 + jnp.exp(-jnp.abs(x)))


def _rms(x, gain):
    return x * lax.rsqrt(jnp.mean(x * x, axis=-1, keepdims=True) + EPS) * gain


def _norm_mod(x, gain, shift, scale):
    return _rms(x, gain) * (1.0 + scale) + shift


def _split(a):
    hi = a.astype(BF)
    return hi, (a - hi.astype(F32)).astype(BF)


def _dg3(a, b, ca, cb):
    ah, al = _split(a)
    bh, bl = _split(b)

    def dg(x, y):
        return lax.dot_general(x, y, (((ca,), (cb,)), ((), ())), preferred_element_type=F32)
    return dg(ah, bh) + (dg(ah, bl) + dg(al, bh))


@jax.custom_vjp
def _mm3(a, b):
    return _dg3(a, b, 1, 0)


def _mm3_fwd(a, b):
    return _dg3(a, b, 1, 0), (a, b)


def _mm3_bwd(res, dy):
    a, b = res
    return _dg3(dy, b, 1, 1), _dg3(a, dy, 0, 0)


_mm3.defvjp(_mm3_fwd, _mm3_bwd)


def _qk_prep(slabs, gain, cos, sin):
    r = _iota((2 * HD, 2 * HD), 0)
    c = _iota((2 * HD, 2 * HD), 1)
    seg = jnp.where(r // HD == c // HD, 1.0 / HD, 0.0).astype(F32)
    half = ROT // 2
    cd = c % HD
    pair = jnp.where(((cd < half) & (r == c + half)) | ((cd >= half) & (cd < ROT) & (r == c - half)), 1.0, 0.0).astype(F32)
    out = []
    for x in slabs:
        y = x * lax.rsqrt(_mm3(x * x, seg) + EPS) * gain
        out.append(y * cos + _mm3(y, pair) * sin)
    return out


def _attn_block(qs, kc, kp, vc, vp, sinks, has_prev):
    rows = GROUP * BLK
    qi = _iota((rows, 2 * BLK), 0) % BLK + BLK
    kj = _iota((rows, 2 * BLK), 1)
    dist = qi - kj
    valid = (dist >= 0) & (dist < BLK) & ((kj >= BLK) | has_prev)
    grp = _iota((rows, HEADS), 0) // BLK
    col = _iota((rows, HEADS), 1)

    outs = []
    for h in range(KV_HEADS):
        q = jnp.concatenate([qs[h * GROUP + g] for g in range(GROUP)], axis=0)
        k = jnp.concatenate([kp[h], kc[h]], axis=0)
        v = jnp.concatenate([vp[h], vc[h]], axis=0)
        s = _mm_nt(q, k) * (HD ** -0.5)
        s = jnp.where(valid, s, NEG)
        sink = jnp.sum(jnp.where(col == h * GROUP + grp, sinks, 0.0), axis=-1, keepdims=True)
        m = lax.stop_gradient(jnp.maximum(jnp.max(s, axis=-1, keepdims=True), sink))
        p = jnp.exp(s - m)
        probs = p / (jnp.sum(p, axis=-1, keepdims=True) + jnp.exp(sink - m))
        o = _mm(probs, v)
        outs += [o[g * BLK:(g + 1) * BLK] for g in range(GROUP)]
    return outs


def _dn_act(y, normalize):
    s = _silu(y)
    return s * lax.rsqrt(jnp.sum(s * s, axis=-1, keepdims=True) + EPS) if normalize else s


def _dn_gates(ba, alog, dtb):
    lane = _iota(ba.shape, 1)
    beta = _sigmoid(ba)
    g = -jnp.exp(alog) * _softplus(ba + dtb)
    return jnp.where(lane < DN_H, beta, jnp.where(lane < 2 * DN_H, g, 0.0))


def _bdg(a, b, ca, cb):
    return lax.dot_general(a.astype(BF), b.astype(BF), (((ca,), (cb,)), ((0,), (0,))), preferred_element_type=F32)


@jax.custom_vjp
def _bmm(a, b):
    return _bdg(a, b, 2, 1)


def _bmm_fwd(a, b):
    return _bdg(a, b, 2, 1), (a, b)


def _bmm_bwd(res, dy):
    a, b = res
    return _bdg(dy, b, 2, 2), _bdg(a, dy, 1, 1)


_bmm.defvjp(_bmm_fwd, _bmm_bwd)


@jax.custom_vjp
def _bmm_nt(a, b):
    return _bdg(a, b, 2, 2)


def _bmm_nt_fwd(a, b):
    return _bdg(a, b, 2, 2), (a, b)


def _bmm_nt_bwd(res, dy):
    a, b = res
    return _bdg(dy, b, 2, 1), _bdg(dy, a, 1, 1)


_bmm_nt.defvjp(_bmm_nt_fwd, _bmm_nt_bwd)


def _bmmx(a, b):
    return lax.dot_general(a, b, (((2,), (1,)), ((0,), (0,))), precision=lax.Precision.HIGHEST,
                           preferred_element_type=F32)


def _neumann_inverse(lmat):
    C = CHUNK
    eye = jnp.where(_iota((C, C), 0) == _iota((C, C), 1), 1.0, 0.0).astype(F32)[None]
    a = -lmat
    tinv = eye + a
    pw = _bmmx(a, a)
    for _ in range(4):
        both = _bmmx(jnp.concatenate([pw, tinv], axis=1), pw)
        pw, tinv = both[:, :C], tinv + both[:, C:]
    return tinv + _bmmx(tinv, pw)


def _inverse_bwd(tinv, d_tinv):
    x = lax.dot_general(d_tinv, tinv, (((2,), (2,)), ((0,), (0,))), precision=lax.Precision.HIGHEST,
                        preferred_element_type=F32)
    return -lax.dot_general(tinv, x, (((1,), (1,)), ((0,), (0,))), precision=lax.Precision.HIGHEST,
                            preferred_element_type=F32)


@jax.custom_vjp
def _tri_inverse(lmat):
    return _neumann_inverse(lmat)


def _tri_inverse_fwd(lmat):
    tinv = _neumann_inverse(lmat)
    return tinv, tinv


def _tri_inverse_bwd(tinv, d_tinv):
    return (_inverse_bwd(tinv, d_tinv),)


_tri_inverse.defvjp(_tri_inverse_fwd, _tri_inverse_bwd)


@jax.custom_vjp
def _tri_inverse_known(lmat, tinv):
    return tinv


def _tri_inverse_known_fwd(lmat, tinv):
    return tinv, tinv


def _tri_inverse_known_bwd(tinv, d_tinv):
    return _inverse_bwd(tinv, d_tinv), jnp.zeros_like(tinv)


_tri_inverse_known.defvjp(_tri_inverse_known_fwd, _tri_inverse_known_bwd)


def _dn_intra(q, k, v, bg, tinv=None):
    C = CHUNK
    G = bg.shape[0]
    r = _iota((C, C), 0)
    c = _iota((C, C), 1)
    incl = (r >= c)[None]
    strict = (r > c)[None]
    eye = jnp.where(r == c, 1.0, 0.0).astype(F32)[None]
    tri = jnp.broadcast_to(jnp.where(r >= c, 1.0, 0.0).astype(F32)[None], (G, C, C))
    gc_all = _bmmx(tri, bg)
    lane = _iota((C, DN_D), 1)

    def per_head(x, offset):
        return jnp.concatenate([jnp.sum(jnp.where(lane == offset + h, x[g], 0.0), axis=-1, keepdims=True)[None]
                                for g in range(G) for h in range(DN_H)], axis=0)
    beta = per_head(bg, 0)
    gcol = per_head(gc_all, DN_H)
    grow = jnp.sum(eye * gcol, axis=1, keepdims=True)
    glast = jnp.sum(jnp.where(_iota((1, C, 1), 1) == C - 1, gcol, 0.0), axis=1, keepdims=True)
    decay = jnp.exp(jnp.where(incl, gcol - grow, NEG))
    q = q * (DN_D ** -0.5)
    kb = k * beta
    lmat = jnp.where(strict, _bmm_nt(kb, k) * decay, 0.0)
    tinv = _tri_inverse(lmat) if tinv is None else _tri_inverse_known(lmat, tinv)
    egc = jnp.exp(gcol)
    u = _bmm(tinv, v * beta)
    w = _bmm(tinv, kb * egc)
    a = _bmm_nt(q, k) * decay
    return u, w, q * egc, k * jnp.exp(glast - gcol), a, jnp.exp(glast), tinv


@jax.custom_vjp
def _bmm_tn(a, b):
    return _bdg(a, b, 1, 1)


def _bmm_tn_fwd(a, b):
    return _bdg(a, b, 1, 1), (a, b)


def _bmm_tn_bwd(res, dy):
    a, b = res
    return _bdg(b, dy, 2, 2), _bdg(a, dy, 2, 1)


_bmm_tn.defvjp(_bmm_tn_fwd, _bmm_tn_bwd)


def _dn_rec(state, u, w, qd, kd, a, cd):
    v_new = u - _bmm(w, state)
    out = _bmm(qd, state) + _bmm(a, v_new)
    return state * cd + _bmm_tn(kd, v_new), out


def _mix_tile(o_attn, o_raw, zs, ga, gb, x, gate1, dn_g, wb_a, wb_d, w_out, p_ya, p_yd, p_out):
    o_dn = jnp.concatenate([_rms(o_raw[h], dn_g) * _silu(zs[h]) for h in range(DN_H)], axis=-1)
    y_a = _mm(o_attn, wb_a) + p_ya
    y_d = _mm(o_dn, wb_d) + p_yd
    merged = _sigmoid(ga) * y_a + _sigmoid(gb) * y_d
    out = _mm(merged, w_out) + p_out
    return x + gate1 * out, o_dn, merged


def _mlp_tile(x1, gain, shift, scale, gate2, w_gu, w_dn, tgt, p_gu, p_yy):
    h2 = _norm_mod(x1, gain, shift, scale)
    gu = jnp.concatenate([_mm(h2, w) for w in w_gu], axis=-1) + p_gu
    act = _silu(gu[:, :FFN]) * gu[:, FFN:]
    yy = _mm(act, w_dn) + p_yy
    y = x1 + gate2 * yy
    err = y - tgt
    return 0.5 * jnp.sum(err * err) * (1.0 / D), (h2, act)


def _tok(bt, f):
    return pl.BlockSpec((None, bt, f), lambda b, i: (b, i, 0))


def _full(shape):
    return pl.BlockSpec(shape, lambda b, i: (0,) * len(shape))


def _resident(shape):
    return pl.BlockSpec(shape, lambda b, i: (0,) * len(shape), pipeline_mode=pl.Buffered(1))


def _per_batch(f):
    return pl.BlockSpec((None, 1, f), lambda b, i: (b, 0, 0))


def _sds(shape, dtype):
    return jax.ShapeDtypeStruct(shape, dtype)


class _Exchange(NamedTuple):
    ins: tuple
    out_shapes: tuple
    n_remote: int
    plan: Callable
    n_forward: int = 0
    forward: Callable = None


def _remote_copies(remote, send_sems, recv_sems):
    return [pltpu.make_async_remote_copy(src_ref=src, dst_ref=dst, send_sem=send_sems.at[i], recv_sem=recv_sems.at[i],
                                         device_id=peer, device_id_type=pl.DeviceIdType.MESH)
            for i, (src, dst, peer) in enumerate(remote)]


def _exchange_copies(ex, in_refs, out_refs, send_sems, recv_sems):
    remote = ex.plan((lax.axis_index("x"), lax.axis_index("y"), lax.axis_index("c")), in_refs, out_refs)
    assert len(remote) == ex.n_remote
    return _remote_copies(remote, send_sems, recv_sems)


def _forward_copies(ex, out_refs, send_sems, recv_sems):
    remote = ex.forward((lax.axis_index("x"), lax.axis_index("y"), lax.axis_index("c")), out_refs)
    assert len(remote) == ex.n_forward
    return _remote_copies(remote, send_sems, recv_sems)


def _exchange_sems(ex):
    sems = [pltpu.SemaphoreType.DMA((ex.n_remote,)), pltpu.SemaphoreType.DMA((ex.n_remote,))]
    if ex.n_forward:
        sems += [pltpu.SemaphoreType.DMA((ex.n_forward,)), pltpu.SemaphoreType.DMA((ex.n_forward,))]
    return sems


def _hosted_call(body, name, grid, in_specs, out_specs, out_shape, scratch_shapes, semantics, ins, ex=None):
    if ex is None:
        outs = pl.pallas_call(body, name=name, grid=grid, in_specs=in_specs, out_specs=out_specs, out_shape=out_shape,
                              scratch_shapes=scratch_shapes,
                              compiler_params=_cparams(dimension_semantics=semantics))(*ins)
        return outs, ()
    n_in, n_out, n_scr = len(ins), len(out_shape), len(scratch_shapes)
    c_in, c_out = len(ex.ins), len(ex.out_shapes)
    steps = 1
    for g in grid:
        steps *= g

    def wrapped(*refs):
        a, b, c, d = n_in, n_in + c_in, n_in + c_in + n_out, n_in + c_in + n_out + c_out
        scratch, sems = refs[d:d + n_scr], refs[d + n_scr:]
        step = 0
        for axis, g in enumerate(grid):
            step = step * g + pl.program_id(axis)

        def first_phase():
            return _exchange_copies(ex, refs[a:b], refs[c:d], sems[0], sems[1])

        @pl.when(step == 0)
        def _():
            for cp in first_phase():
                cp.start()
        body(*refs[:a], *refs[b:c], *scratch)

        if ex.n_forward:
            @pl.when(step == (3 * steps) // 4)
            def _():
                for cp in first_phase():
                    cp.wait_recv()
                for cp in _forward_copies(ex, refs[c:d], sems[2], sems[3]):
                    cp.start()

        @pl.when(step == steps - 1)
        def _():
            cps = first_phase()
            if ex.n_forward:
                fwd = _forward_copies(ex, refs[c:d], sems[2], sems[3])
                for cp in fwd:
                    cp.wait_recv()
                for cp in cps + fwd:
                    cp.wait_send()
            else:
                for cp in cps:
                    cp.wait_recv()
                for cp in cps:
                    cp.wait_send()

    any_spec = pl.BlockSpec(memory_space=pl.ANY)
    res = pl.pallas_call(
        wrapped, name=name, grid=grid, in_specs=list(in_specs) + [any_spec] * c_in,
        out_specs=list(out_specs) + [any_spec] * c_out, out_shape=list(out_shape) + list(ex.out_shapes),
        scratch_shapes=list(scratch_shapes) + _exchange_sems(ex),
        compiler_params=_cparams(dimension_semantics=("arbitrary",) * len(grid)),
    )(*ins, *ex.ins)
    return res[:n_out], res[n_out:]


def _acc(ref, val, first):
    @pl.when(first)
    def _():
        ref[...] = val

    @pl.when(jnp.logical_not(first))
    def _():
        ref[...] += val


def _in_proj(x, mod, norm1_g, w_in, bt):
    B, S, _ = x.shape

    def body(x_ref, mod_ref, g_ref, w_ref, q_ref, kv_ref, dn_ref, z_ref, ga_ref, gb_ref, ba_ref, h_ref):
        h = _norm_mod(x_ref[...], g_ref[...], mod_ref[:, 0:D], mod_ref[:, D:2 * D]).astype(BF)
        h_ref[...] = h

        def proj(c0, c1):
            return jnp.dot(h, w_ref[:, c0:c1], preferred_element_type=F32)
        q_ref[...] = proj(0, C_KV).astype(BF)
        kv_ref[...] = proj(C_KV, C_DN).astype(BF)
        dn_ref[...] = proj(C_DN, C_Z).astype(BF)
        z_ref[...] = proj(C_Z, C_GA).astype(BF)
        ga_ref[...] = proj(C_GA, C_GB).astype(BF)
        gb_ref[...] = proj(C_GB, C_BA).astype(BF)
        ba_ref[...] = proj(C_BA, IN_PAD)

    widths = (QW, 2 * KVW, CONVW, DNW, D, D)
    return pl.pallas_call(
        body, name="in_proj", grid=(B, S // bt),
        in_specs=[_tok(bt, D), _per_batch(6 * D), _full((1, D)), _resident((D, IN_PAD))],
        out_specs=[_tok(bt, w) for w in widths] + [_tok(bt, 128), _tok(bt, D)],
        out_shape=[_sds((B, S, w), BF) for w in widths] + [_sds((B, S, 128), F32), _sds((B, S, D), BF)],
        compiler_params=_cparams(dimension_semantics=("parallel", "parallel")),
    )(x, mod, norm1_g, w_in)


def _prev_blk(bt, f):
    return pl.BlockSpec((None, bt, f), lambda b, i: (b, jnp.maximum(i - 1, 0), 0))


QKV = QW + 2 * KVW


def _qk_slabs(q_ref, kv_ref):
    return ([q_ref[:, j * 2 * HD:(j + 1) * 2 * HD].astype(F32) for j in range(QW // (2 * HD))],
            [kv_ref[:, 0:KVW].astype(F32)])


def _qk_prep_fwd(q, kv, cos, sin, qg, kg, bt):
    B, S, _ = q.shape

    def body(q_ref, kv_ref, cos_ref, sin_ref, qg_ref, kg_ref, o_ref):
        qs, ks = _qk_slabs(q_ref, kv_ref)
        qn = _qk_prep(qs, qg_ref[...], cos_ref[...], sin_ref[...])
        kn = _qk_prep(ks, kg_ref[...], cos_ref[...], sin_ref[...])
        for j, t in enumerate(qn + kn):
            o_ref[:, j * 2 * HD:(j + 1) * 2 * HD] = t.astype(BF)
        o_ref[:, QW + KVW:QKV] = kv_ref[:, KVW:2 * KVW]

    return pl.pallas_call(
        body, name="qk_prep_fwd", grid=(B, S // bt),
        in_specs=[_tok(bt, QW), _tok(bt, 2 * KVW), _tok(bt, 2 * HD), _tok(bt, 2 * HD), _full((1, 2 * HD)), _full((1, 2 * HD))],
        out_specs=_tok(bt, QKV), out_shape=_sds((B, S, QKV), BF),
        compiler_params=_cparams(dimension_semantics=("parallel", "parallel")),
    )(q, kv, cos, sin, qg, kg)


def _qk_prep_bwd(q, kv, cos, sin, qg, kg, dqn, dkvn, bt, ex=None):
    B, S, _ = q.shape

    def body(q_ref, kv_ref, cos_ref, sin_ref, qg_ref, kg_ref, dqn_ref, dkvn_ref, dq_ref, dkv_ref, dqg_ref, dkg_ref):
        qs, ks = _qk_slabs(q_ref, kv_ref)
        cos, sin = cos_ref[...], sin_ref[...]

        def f(qs, ks, qg, kg):
            return _qk_prep(qs, qg, cos, sin), _qk_prep(ks, kg, cos, sin)
        _, vjp = jax.vjp(f, qs, ks, qg_ref[...], kg_ref[...])
        n_q = len(qs)
        d_q = [dqn_ref[:, j * 2 * HD:(j + 1) * 2 * HD].astype(F32) for j in range(n_q)]
        d_k = [dkvn_ref[:, 0:KVW].astype(F32)]
        dqs, dks, dqg, dkg = vjp((d_q, d_k))
        for j in range(n_q):
            dq_ref[:, j * 2 * HD:(j + 1) * 2 * HD] = dqs[j].astype(BF)
        dkv_ref[:, 0:KVW] = dks[0].astype(BF)
        dkv_ref[:, KVW:2 * KVW] = dkvn_ref[:, KVW:2 * KVW]
        first = (pl.program_id(0) == 0) & (pl.program_id(1) == 0)
        _acc(dqg_ref, dqg, first)
        _acc(dkg_ref, dkg, first)

    return _hosted_call(
        body, "qk_prep_bwd", (B, S // bt),
        in_specs=[_tok(bt, QW), _tok(bt, 2 * KVW), _tok(bt, 2 * HD), _tok(bt, 2 * HD), _full((1, 2 * HD)), _full((1, 2 * HD)),
                  _tok(bt, QW), _tok(bt, 2 * KVW)],
        out_specs=[_tok(bt, QW), _tok(bt, 2 * KVW), _full((1, 2 * HD)), _full((1, 2 * HD))],
        out_shape=[_sds((B, S, QW), BF), _sds((B, S, 2 * KVW), BF), _sds((1, 2 * HD), F32), _sds((1, 2 * HD), F32)],
        scratch_shapes=[], semantics=("arbitrary", "arbitrary"), ins=(q, kv, cos, sin, qg, kg, dqn, dkvn), ex=ex)


def _attn_load(qkv_ref, kvp_ref):
    qs = [qkv_ref[:, h * HD:(h + 1) * HD].astype(F32) for h in range(HEADS)]
    kc = [qkv_ref[:, QW + h * HD:QW + (h + 1) * HD].astype(F32) for h in range(KV_HEADS)]
    vc = [qkv_ref[:, QW + KVW + h * HD:QW + KVW + (h + 1) * HD].astype(F32) for h in range(KV_HEADS)]
    kp = [kvp_ref[:, h * HD:(h + 1) * HD].astype(F32) for h in range(KV_HEADS)]
    vp = [kvp_ref[:, KVW + h * HD:KVW + (h + 1) * HD].astype(F32) for h in range(KV_HEADS)]
    return qs, kc, kp, vc, vp


def _kv_prev_spec(index):
    return pl.BlockSpec((None, BLK, 2 * KVW), lambda b, i: (b, index(i), QW // (2 * KVW)))


def _attn_fwd(qkv, sinks):
    B, S, _ = qkv.shape

    def body(qkv_ref, kvp_ref, sk_ref, o_ref):
        qs, kc, kp, vc, vp = _attn_load(qkv_ref, kvp_ref)
        outs = _attn_block(qs, kc, kp, vc, vp, sk_ref[...], pl.program_id(1) > 0)
        for h in range(HEADS):
            o_ref[:, h * HD:(h + 1) * HD] = outs[h].astype(BF)

    return pl.pallas_call(
        body, name="attn_fwd", grid=(B, S // BLK),
        in_specs=[_tok(BLK, QKV), _kv_prev_spec(lambda i: jnp.maximum(i - 1, 0)), _full((1, HEADS))],
        out_specs=_tok(BLK, QW), out_shape=_sds((B, S, QW), BF),
        compiler_params=_cparams(dimension_semantics=("parallel", "parallel")),
    )(qkv, qkv, sinks)


def _halo_spec(bt):
    return pl.BlockSpec((None, 8, CONVW), lambda b, i: (b, jnp.maximum(i * (bt // 8) - 1, 0), 0))


def _dn_prep(dn, ba, conv_w, alog, dtb, bt):
    B, S, _ = dn.shape

    strip_rows = min(bt, 64)

    def body(x_ref, halo_ref, ba_ref, cw_ref, al_ref, dt_ref, qkv_ref, bg_ref, y_ref, xe_ref):
        xe_ref[0:8, :] = jnp.where(pl.program_id(1) == 0, 0.0, halo_ref[...].astype(F32))
        xe_ref[8:bt + 8, :] = x_ref[...].astype(F32)

        def strip(k, carry):
            r0 = pl.multiple_of(k * strip_rows, strip_rows)
            rows = pl.ds(r0, strip_rows)
            for j in range(3 * DN_H):
                cols = slice(j * DN_D, (j + 1) * DN_D)
                window = xe_ref[pl.ds(r0, strip_rows + 8), cols]
                y = cw_ref[0:1, cols] * window[5:strip_rows + 5]
                for t in range(1, CONV):
                    y = y + cw_ref[t:t + 1, cols] * window[5 + t:strip_rows + 5 + t]
                y_ref[rows, cols] = y.astype(BF)
                qkv_ref[rows, cols] = _dn_act(y, j < 2 * DN_H)
            bg_ref[rows, :] = _dn_gates(ba_ref[rows, :], al_ref[...], dt_ref[...])
            return carry
        lax.fori_loop(0, bt // strip_rows, strip, 0)

    return pl.pallas_call(
        body, name="dn_prep", grid=(B, S // bt),
        in_specs=[_tok(bt, CONVW), _halo_spec(bt), _tok(bt, 128), _full((CONV, CONVW)), _full((1, 128)), _full((1, 128))],
        out_specs=[_tok(bt, CONVW), _tok(bt, 128), _tok(bt, CONVW)],
        out_shape=[_sds((B, S, CONVW), F32), _sds((B, S, 128), F32), _sds((B, S, CONVW), BF)],
        scratch_shapes=[pltpu.VMEM((bt + 8, CONVW), F32)],
        compiler_params=_cparams(dimension_semantics=("parallel", "arbitrary")),
    )(dn, dn, ba, conv_w, alog, dtb)


def _dn_load(qkv_ref):
    qs = [qkv_ref[:, h * DN_D:(h + 1) * DN_D] for h in range(DN_H)]
    ks = [qkv_ref[:, DNW + h * DN_D:DNW + (h + 1) * DN_D] for h in range(DN_H)]
    vs = [qkv_ref[:, 2 * DNW + h * DN_D:2 * DNW + (h + 1) * DN_D] for h in range(DN_H)]
    return qs, ks, vs


DN_GROUP = 4
AW = DN_H * CHUNK


def _stack_heads(ref, G, offset, width):
    return jnp.stack([ref[g * CHUNK:(g + 1) * CHUNK, offset + h * width:offset + (h + 1) * width]
                      for g in range(G) for h in range(DN_H)])


def _dn_load_stack(qkv_ref, G):
    return tuple(_stack_heads(qkv_ref, G, j * DNW, DN_D) for j in range(3))


def _cd_spec(n):
    return pl.BlockSpec((None, n, 1, DN_D), lambda b, i: (b, i, 0, 0))


def _dn_intra_fwd(qkv, bg, ex=None):
    B, S, _ = qkv.shape
    nc = S // CHUNK
    G = min(DN_GROUP, nc)
    rows = G * CHUNK

    def body(qkv_ref, bg_ref, u_ref, w_ref, qd_ref, kd_ref, a_ref, cd_ref, t_ref):
        q, k, v = _dn_load_stack(qkv_ref, G)
        u, w, qd, kd, a, cd, tinv = _dn_intra(q, k, v, bg_ref[...].reshape(G, CHUNK, DN_D))
        lane_row = _iota((1, DN_D), 1)
        for g in range(G):
            rows = slice(g * CHUNK, (g + 1) * CHUNK)
            cd_row = jnp.zeros((1, DN_D), F32)
            for h in range(DN_H):
                n = g * DN_H + h
                cols = slice(h * DN_D, (h + 1) * DN_D)
                u_ref[rows, cols] = u[n]
                w_ref[rows, cols] = w[n].astype(BF)
                qd_ref[rows, cols] = qd[n].astype(BF)
                kd_ref[rows, cols] = kd[n].astype(BF)
                a_ref[rows, h * CHUNK:(h + 1) * CHUNK] = a[n].astype(BF)
                t_ref[rows, h * CHUNK:(h + 1) * CHUNK] = tinv[n]
                cd_row = cd_row + jnp.where(lane_row == h, cd[n], 0.0)
            cd_ref[g] = cd_row

    return _hosted_call(
        body, "dn_intra_fwd", (B, nc // G),
        in_specs=[_tok(rows, CONVW), _tok(rows, 128)],
        out_specs=[_tok(rows, DNW)] * 4 + [_tok(rows, AW), _cd_spec(G), _tok(rows, AW)],
        out_shape=[_sds((B, S, DNW), F32)] + [_sds((B, S, DNW), BF)] * 3 + [_sds((B, S, AW), BF), _sds((B, nc, 1, DN_D), F32),
                                                                            _sds((B, S, AW), F32)],
        scratch_shapes=[], semantics=("parallel", "parallel"), ins=(qkv, bg), ex=ex)


REC_GROUP = 4


def _rec_stack(ref, B, width, c):
    rows = slice(c * CHUNK, (c + 1) * CHUNK)
    return jnp.stack([ref[b, rows, h * width:(h + 1) * width].astype(F32) for b in range(B) for h in range(DN_H)])


def _rec_load(B, u_ref, w_ref, qd_ref, kd_ref, a_ref, cd_ref, c):
    lane_row = _iota((1, DN_D), 1)
    cd = jnp.stack([jnp.sum(jnp.where(lane_row == h, cd_ref[b, c], 0.0), axis=-1, keepdims=True)
                    for b in range(B) for h in range(DN_H)])
    return (_rec_stack(u_ref, B, DN_D, c), _rec_stack(w_ref, B, DN_D, c), _rec_stack(qd_ref, B, DN_D, c),
            _rec_stack(kd_ref, B, DN_D, c), _rec_stack(a_ref, B, CHUNK, c), cd)


def _rec_store(B, ref, val, width, c):
    for b in range(B):
        for h in range(DN_H):
            ref[b, c * CHUNK:(c + 1) * CHUNK, h * width:(h + 1) * width] = val[b * DN_H + h]


def _rec_specs(B, R, index):
    def tok(f):
        return pl.BlockSpec((B, R * CHUNK, f), lambda i: (0, index(i), 0))
    cd = pl.BlockSpec((B, R, 1, DN_D), lambda i: (0, index(i), 0, 0))
    st = pl.BlockSpec((B, R, DN_H, DN_D, DN_D), lambda i: (0, index(i), 0, 0, 0))
    return tok, cd, st


def _dn_rec_fwd(u, w, qd, kd, a, cd, ex=None):
    B, S, _ = u.shape
    nc = S // CHUNK
    R = REC_GROUP if nc % REC_GROUP == 0 else 1
    tok, cd_spec, st_spec = _rec_specs(B, R, lambda i: i)

    def body(u_ref, w_ref, qd_ref, kd_ref, a_ref, cd_ref, o_ref, st_ref, s_ref):
        @pl.when(pl.program_id(0) == 0)
        def _():
            s_ref[...] = jnp.zeros_like(s_ref)
        state = s_ref[...]
        for c in range(R):
            st_ref[:, c] = state.reshape(B, DN_H, DN_D, DN_D)
            state, out = _dn_rec(state, *_rec_load(B, u_ref, w_ref, qd_ref, kd_ref, a_ref, cd_ref, c))
            _rec_store(B, o_ref, out, DN_D, c)
        s_ref[...] = state

    return _hosted_call(
        body, "dn_rec_fwd", (nc // R,),
        in_specs=[tok(DNW)] * 4 + [tok(AW), cd_spec],
        out_specs=[tok(DNW), st_spec],
        out_shape=[_sds((B, S, DNW), F32), _sds((B, nc, DN_H, DN_D, DN_D), F32)],
        scratch_shapes=[pltpu.VMEM((B * DN_H, DN_D, DN_D), F32)],
        semantics=("arbitrary",), ins=(u, w, qd, kd, a, cd), ex=ex)


def _mix_load(oa_ref, or_ref, z_ref):
    o_raw = [or_ref[:, h * DN_D:(h + 1) * DN_D] for h in range(DN_H)]
    zs = [z_ref[:, h * DN_D:(h + 1) * DN_D].astype(F32) for h in range(DN_H)]
    return oa_ref[...].astype(F32), o_raw, zs


def _mix_fwd(o_attn, o_raw, z, ga, gb, x, mod, dn_g, w_branch, w_out, bt):
    B, S, _ = x.shape

    def body(oa_ref, or_ref, z_ref, ga_ref, gb_ref, x_ref, mod_ref, dg_ref, wb_ref, wo_ref, x1_ref, od_ref, mg_ref):
        oa, o_r, zs = _mix_load(oa_ref, or_ref, z_ref)
        x1, o_dn, merged = _mix_tile(oa, o_r, zs, ga_ref[...].astype(F32), gb_ref[...].astype(F32), x_ref[...],
                                     mod_ref[:, 2 * D:3 * D], dg_ref[...], wb_ref[0:QW, :], wb_ref[QW:2 * QW, :],
                                     wo_ref[...], 0.0, 0.0, 0.0)
        x1_ref[...] = x1
        od_ref[...] = o_dn.astype(BF)
        mg_ref[...] = merged.astype(BF)

    return pl.pallas_call(
        body, name="mix_fwd", grid=(B, S // bt),
        in_specs=[_tok(bt, QW), _tok(bt, DNW), _tok(bt, DNW), _tok(bt, D), _tok(bt, D), _tok(bt, D), _per_batch(6 * D),
                  _full((1, DN_D)), _resident((D, D)), _resident((D, D))],
        out_specs=[_tok(bt, D), _tok(bt, DNW), _tok(bt, D)],
        out_shape=[_sds((B, S, D), F32), _sds((B, S, DNW), BF), _sds((B, S, D), BF)],
        compiler_params=_cparams(dimension_semantics=("parallel", "parallel")),
    )(o_attn, o_raw, z, ga, gb, x, mod, dn_g, w_branch, w_out)


def _mlp(x1, tgt, mod, norm2_g, w_gu, w_dn, bt):
    B, S, _ = x1.shape

    def body(x1_ref, t_ref, mod_ref, g_ref, wgu_ref, wdn_ref,
             dx1_ref, h2_ref, act_ref, dgu_ref, dyy_ref, loss_ref, dmod_ref, dg_ref):
        w_gu_v, w_dn_v, t = [wgu_ref[k] for k in range(N_CHIP)], wdn_ref[...], t_ref[...]

        def f(x1, gain, shift, scale, gate2, p_gu, p_yy):
            return _mlp_tile(x1, gain, shift, scale, gate2, w_gu_v, w_dn_v, t, p_gu, p_yy)
        zero_gu = jnp.zeros((bt, 2 * FFN), F32)
        zero_yy = jnp.zeros((bt, D), F32)
        loss, vjp, (h2, act) = jax.vjp(f, x1_ref[...], g_ref[...], mod_ref[:, 3 * D:4 * D], mod_ref[:, 4 * D:5 * D],
                                       mod_ref[:, 5 * D:6 * D], zero_gu, zero_yy, has_aux=True)
        dx1, dgain, dshift, dscale, dgate2, dgu, dyy = vjp(jnp.ones((), F32))
        dx1_ref[...] = dx1
        h2_ref[...] = h2.astype(BF)
        act_ref[...] = act.astype(BF)
        dgu_ref[...] = dgu.astype(BF)
        dyy_ref[...] = dyy.astype(BF)
        first = pl.program_id(1) == 0
        _acc(loss_ref, jnp.reshape(loss, (1, 1)), first)
        _acc(dmod_ref, jnp.concatenate([dshift, dscale, dgate2], axis=-1), first)
        _acc(dg_ref, dgain, first)

    return pl.pallas_call(
        body, name="mlp", grid=(B, S // bt),
        in_specs=[_tok(bt, D), _tok(bt, D), _per_batch(6 * D), _full((1, D)), _resident((N_CHIP, D, 2 * FFN // N_CHIP)),
                  _resident((FFN, D))],
        out_specs=[_tok(bt, D), _tok(bt, D), _tok(bt, FFN), _tok(bt, 2 * FFN), _tok(bt, D),
                   _per_batch(1), _per_batch(3 * D), _per_batch(D)],
        out_shape=[_sds((B, S, D), F32), _sds((B, S, D), BF), _sds((B, S, FFN), BF), _sds((B, S, 2 * FFN), BF),
                   _sds((B, S, D), BF), _sds((B, 1, 1), F32), _sds((B, 1, 3 * D), F32), _sds((B, 1, D), F32)],
        compiler_params=_cparams(dimension_semantics=("parallel", "arbitrary")),
    )(x1, tgt, mod, norm2_g, w_gu, w_dn)


def _mix_bwd(o_attn, o_raw, z, ga, gb, x, mod, dn_g, w_branch, w_out, dx1, bt, ex=None):
    B, S, _ = x.shape

    def body(oa_ref, or_ref, z_ref, ga_ref, gb_ref, x_ref, mod_ref, dg_ref, wb_ref, wo_ref, dx1_ref,
             doa_ref, dor_ref, dz_ref, dga_ref, dgb_ref, dya_ref, dyd_ref, dout_ref, dgate_ref, ddg_ref):
        oa, o_r, zs = _mix_load(oa_ref, or_ref, z_ref)
        wb_a, wb_d, wo = wb_ref[0:QW, :], wb_ref[QW:2 * QW, :], wo_ref[...]

        def f(oa, o_r, zs, ga, gb, gate1, dn_g, p_ya, p_yd, p_out):
            return _mix_tile(oa, o_r, zs, ga, gb, x_ref[...], gate1, dn_g, wb_a, wb_d, wo, p_ya, p_yd, p_out)[0]
        zero = jnp.zeros((bt, D), F32)
        _, vjp = jax.vjp(f, oa, o_r, zs, ga_ref[...].astype(F32), gb_ref[...].astype(F32), mod_ref[:, 2 * D:3 * D],
                         dg_ref[...], zero, zero, zero)
        doa, dor, dzs, dga, dgb, dgate1, ddn_g, dya, dyd, dout = vjp(dx1_ref[...])
        doa_ref[...] = doa
        for h in range(DN_H):
            dor_ref[:, h * DN_D:(h + 1) * DN_D] = dor[h]
            dz_ref[:, h * DN_D:(h + 1) * DN_D] = dzs[h].astype(BF)
        dga_ref[...] = dga.astype(BF)
        dgb_ref[...] = dgb.astype(BF)
        dya_ref[...] = dya.astype(BF)
        dyd_ref[...] = dyd.astype(BF)
        dout_ref[...] = dout.astype(BF)
        first = pl.program_id(1) == 0
        _acc(dgate_ref, dgate1, first)
        _acc(ddg_ref, ddn_g, first)

    return _hosted_call(
        body, "mix_bwd", (B, S // bt),
        in_specs=[_tok(bt, QW), _tok(bt, DNW), _tok(bt, DNW), _tok(bt, D), _tok(bt, D), _tok(bt, D), _per_batch(6 * D),
                  _full((1, DN_D)), _resident((D, D)), _resident((D, D)), _tok(bt, D)],
        out_specs=[_tok(bt, QW), _tok(bt, DNW), _tok(bt, DNW), _tok(bt, D), _tok(bt, D), _tok(bt, D), _tok(bt, D), _tok(bt, D),
                   _per_batch(D), _per_batch(DN_D)],
        out_shape=[_sds((B, S, QW), F32), _sds((B, S, DNW), F32), _sds((B, S, DNW), BF), _sds((B, S, D), BF),
                   _sds((B, S, D), BF), _sds((B, S, D), BF), _sds((B, S, D), BF), _sds((B, S, D), BF),
                   _sds((B, 1, D), F32), _sds((B, 1, DN_D), F32)],
        scratch_shapes=[], semantics=("parallel", "arbitrary"),
        ins=(o_attn, o_raw, z, ga, gb, x, mod, dn_g, w_branch, w_out, dx1), ex=ex)


def _dn_rec_bwd(u, w, qd, kd, a, cd, states, d_o, ex=None):
    B, S, _ = u.shape
    nc = S // CHUNK
    R = REC_GROUP if nc % REC_GROUP == 0 else 1
    tok, cd_spec, st_spec = _rec_specs(B, R, lambda i: nc // R - 1 - i)

    def body(u_ref, w_ref, qd_ref, kd_ref, a_ref, cd_ref, st_ref, do_ref,
             du_ref, dw_ref, dqd_ref, dkd_ref, da_ref, dcd_ref, ds_ref):
        @pl.when(pl.program_id(0) == 0)
        def _():
            ds_ref[...] = jnp.zeros_like(ds_ref)
        lane_row = _iota((1, DN_D), 1)
        d_state = ds_ref[...]
        for c in reversed(range(R)):
            state = st_ref[:, c].reshape(B * DN_H, DN_D, DN_D)
            _, vjp = jax.vjp(_dn_rec, state, *_rec_load(B, u_ref, w_ref, qd_ref, kd_ref, a_ref, cd_ref, c))
            d_state, du, dw, dqd, dkd, da, dcd = vjp((d_state, _rec_stack(do_ref, B, DN_D, c)))
            for ref, val, width in ((du_ref, du, DN_D), (dw_ref, dw, DN_D), (dqd_ref, dqd, DN_D), (dkd_ref, dkd, DN_D),
                                    (da_ref, da, CHUNK)):
                _rec_store(B, ref, val, width, c)
            for b in range(B):
                row = jnp.zeros((1, DN_D), F32)
                for h in range(DN_H):
                    row = row + jnp.where(lane_row == h, dcd[b * DN_H + h], 0.0)
                dcd_ref[b, c] = row
        ds_ref[...] = d_state

    return _hosted_call(
        body, "dn_rec_bwd", (nc // R,),
        in_specs=[tok(DNW)] * 4 + [tok(AW), cd_spec, st_spec, tok(DNW)],
        out_specs=[tok(DNW)] * 4 + [tok(AW), cd_spec],
        out_shape=[_sds((B, S, DNW), F32)] * 4 + [_sds((B, S, AW), F32), _sds((B, nc, 1, DN_D), F32)],
        scratch_shapes=[pltpu.VMEM((B * DN_H, DN_D, DN_D), F32)],
        semantics=("arbitrary",), ins=(u, w, qd, kd, a, cd, states, d_o), ex=ex)


def _dn_intra_bwd(qkv, bg, tinv, du, dw, dqd, dkd, da, dcd, ex=None):
    B, S, _ = qkv.shape
    nc = S // CHUNK
    G = min(DN_GROUP, nc)
    rows = G * CHUNK

    def body(qkv_ref, bg_ref, t_ref, du_ref, dw_ref, dqd_ref, dkd_ref, da_ref, dcd_ref, dqkv_ref, dbg_ref):
        q, k, v = _dn_load_stack(qkv_ref, G)
        known = _stack_heads(t_ref, G, 0, CHUNK)
        _, vjp = jax.vjp(lambda q, k, v, bg: _dn_intra(q, k, v, bg, known)[:6], q, k, v,
                         bg_ref[...].reshape(G, CHUNK, DN_D))
        lane_row = _iota((1, DN_D), 1)
        dcd = jnp.stack([jnp.sum(jnp.where(lane_row == h, dcd_ref[g], 0.0), axis=-1, keepdims=True)
                         for g in range(G) for h in range(DN_H)])
        dq, dk, dv, dbg = vjp((_stack_heads(du_ref, G, 0, DN_D), _stack_heads(dw_ref, G, 0, DN_D),
                               _stack_heads(dqd_ref, G, 0, DN_D), _stack_heads(dkd_ref, G, 0, DN_D),
                               _stack_heads(da_ref, G, 0, CHUNK), dcd))
        for g in range(G):
            rows = slice(g * CHUNK, (g + 1) * CHUNK)
            for h in range(DN_H):
                n = g * DN_H + h
                dqkv_ref[rows, h * DN_D:(h + 1) * DN_D] = dq[n]
                dqkv_ref[rows, DNW + h * DN_D:DNW + (h + 1) * DN_D] = dk[n]
                dqkv_ref[rows, 2 * DNW + h * DN_D:2 * DNW + (h + 1) * DN_D] = dv[n]
        dbg_ref[...] = dbg.reshape(G * CHUNK, DN_D)

    return _hosted_call(
        body, "dn_intra_bwd", (B, nc // G),
        in_specs=[_tok(rows, CONVW), _tok(rows, 128), _tok(rows, AW)] + [_tok(rows, DNW)] * 4 + [_tok(rows, AW), _cd_spec(G)],
        out_specs=[_tok(rows, CONVW), _tok(rows, 128)],
        out_shape=[_sds((B, S, CONVW), F32), _sds((B, S, 128), F32)],
        scratch_shapes=[], semantics=("parallel", "parallel"), ins=(qkv, bg, tinv, du, dw, dqd, dkd, da, dcd), ex=ex)


def _dn_prep_bwd(dn, y, ba, conv_w, alog, dtb, dqkv, dbg, bt, ex=None):
    B, S, _ = dn.shape
    nt = S // bt
    strip_rows = min(bt, 64)

    def rev(f):
        return pl.BlockSpec((None, bt, f), lambda b, i: (b, nt - 1 - i, 0))

    def body(x_ref, y_ref, ba_ref, cw_ref, al_ref, dt_ref, dqkv_ref, dbg_ref,
             dx_ref, dba_ref, dcw_ref, dal_ref, ddt_ref, dye_ref):
        i = pl.program_id(1)
        @pl.when(i == 0)
        def _():
            dye_ref[bt:bt + 8, :] = jnp.zeros((8, CONVW), F32)

        @pl.when(i > 0)
        def _():
            dye_ref[bt:bt + 8, :] = dye_ref[0:8, :]

        n_strips = bt // strip_rows

        def strip(k, carry):
            dal, ddt, dcw = carry
            r0 = pl.multiple_of((n_strips - 1 - k) * strip_rows, strip_rows)
            rows = pl.ds(r0, strip_rows)
            dcw_slabs = []
            for j in range(3 * DN_H):
                cols = slice(j * DN_D, (j + 1) * DN_D)
                _, vjp = jax.vjp(functools.partial(_dn_act, normalize=j < 2 * DN_H), y_ref[rows, cols].astype(F32))
                (dye_ref[rows, cols],) = vjp(dqkv_ref[rows, cols])
                window = dye_ref[pl.ds(r0, strip_rows + 8), cols]
                shifted = [window[3 - t:strip_rows + 3 - t] for t in range(CONV)]
                dx = cw_ref[0:1, cols] * shifted[0]
                for t in range(1, CONV):
                    dx = dx + cw_ref[t:t + 1, cols] * shifted[t]
                dx_ref[rows, cols] = dx.astype(BF)
                x = x_ref[rows, cols].astype(F32)
                dcw_slabs.append(jnp.concatenate([jnp.sum(shifted[t] * x, axis=0, keepdims=True) for t in range(CONV)], axis=0))
            _, vjp = jax.vjp(_dn_gates, ba_ref[rows, :], al_ref[...], dt_ref[...])
            dba_ref[rows, :], da, dd = vjp(dbg_ref[rows, :])
            return dal + da, ddt + dd, dcw + jnp.concatenate(dcw_slabs, axis=1)
        zero = jnp.zeros((1, DN_D), F32)
        dal, ddt, dcw = lax.fori_loop(0, n_strips, strip, (zero, zero, jnp.zeros((CONV, CONVW), F32)))
        first = (i == 0) & (pl.program_id(0) == 0)
        _acc(dcw_ref, dcw, first)
        _acc(dal_ref, dal, first)
        _acc(ddt_ref, ddt, first)

    return _hosted_call(
        body, "dn_prep_bwd", (B, nt),
        in_specs=[rev(CONVW), rev(CONVW), rev(128), _full((CONV, CONVW)), _full((1, 128)), _full((1, 128)), rev(CONVW), rev(128)],
        out_specs=[rev(CONVW), rev(128), _full((CONV, CONVW)), _full((1, 128)), _full((1, 128))],
        out_shape=[_sds((B, S, CONVW), BF), _sds((B, S, 128), F32), _sds((CONV, CONVW), F32), _sds((1, 128), F32),
                   _sds((1, 128), F32)],
        scratch_shapes=[pltpu.VMEM((bt + 8, CONVW), F32)],
        semantics=("arbitrary", "arbitrary"), ins=(dn, y, ba, conv_w, alog, dtb, dqkv, dbg), ex=ex)


def _attn_bwd(qkv, sinks, d_o, ex=None):
    B, S, _ = qkv.shape
    nb = S // BLK

    def cur(f):
        return pl.BlockSpec((None, BLK, f), lambda b, i: (b, jnp.minimum(i, nb - 1), 0))

    def out_prev(f):
        return pl.BlockSpec((None, BLK, f), lambda b, i: (b, jnp.maximum(i - 1, 0), 0))

    def body(qkv_ref, kvp_ref, sk_ref, do_ref, dq_ref, dkv_ref, dsk_ref, carry_ref):
        n = pl.program_id(1)
        first = (n == 0) & (pl.program_id(0) == 0)

        @pl.when(n == 0)
        def _():
            carry_ref[...] = jnp.zeros_like(carry_ref)

        @pl.when(n < nb)
        def _():
            qs, kc, kp, vc, vp = _attn_load(qkv_ref, kvp_ref)

            def f(qs, kc, kp, vc, vp, sk):
                return _attn_block(qs, kc, kp, vc, vp, sk, n > 0)
            _, vjp = jax.vjp(f, qs, kc, kp, vc, vp, sk_ref[...])
            d_outs = [do_ref[:, h * HD:(h + 1) * HD] for h in range(HEADS)]
            dqs, dkc, dkp, dvc, dvp, dsk = vjp(d_outs)
            for h in range(HEADS):
                dq_ref[:, h * HD:(h + 1) * HD] = dqs[h].astype(BF)
            for h in range(KV_HEADS):
                ksl = slice(h * HD, (h + 1) * HD)
                vsl = slice(KVW + h * HD, KVW + (h + 1) * HD)
                dkv_ref[:, ksl] = (carry_ref[:, ksl] + dkp[h]).astype(BF)
                dkv_ref[:, vsl] = (carry_ref[:, vsl] + dvp[h]).astype(BF)
                carry_ref[:, ksl] = dkc[h]
                carry_ref[:, vsl] = dvc[h]
            _acc(dsk_ref, dsk, first)

        @pl.when(n == nb)
        def _():
            dkv_ref[...] = carry_ref[...].astype(BF)

    return _hosted_call(
        body, "attn_bwd", (B, nb + 1),
        in_specs=[cur(QKV), _kv_prev_spec(lambda i: jnp.maximum(jnp.minimum(i, nb - 1) - 1, 0)), _full((1, HEADS)), cur(QW)],
        out_specs=[cur(QW), out_prev(2 * KVW), _full((1, HEADS))],
        out_shape=[_sds((B, S, QW), BF), _sds((B, S, 2 * KVW), BF), _sds((1, HEADS), F32)],
        scratch_shapes=[pltpu.VMEM((BLK, 2 * KVW), F32)],
        semantics=("arbitrary", "arbitrary"), ins=(qkv, qkv, sinks, d_o), ex=ex)


def _in_proj_bwd(x, mod, norm1_g, w_in_t, pieces, dba, dx1, bt):
    B, S, _ = x.shape
    widths = (QW, 2 * KVW, CONVW, DNW, D, D)

    starts = (0, C_KV, C_DN, C_Z, C_GA, C_GB, C_BA, IN_PAD)

    def body(x_ref, mod_ref, g_ref, w_ref, dq_ref, dkv_ref, ddn_ref, dz_ref, dga_ref, dgb_ref, dba_ref, dx1_ref,
             gx_ref, dp_ref, dmod_ref, dg_ref):
        dh = 0.0
        for ref, c0, c1 in zip((dq_ref, dkv_ref, ddn_ref, dz_ref, dga_ref, dgb_ref, dba_ref), starts[:-1], starts[1:]):
            piece = ref[...].astype(BF)
            dp_ref[:, c0:c1] = piece
            dh = dh + jnp.dot(piece, w_ref[c0:c1, :], preferred_element_type=F32)
        _, vjp = jax.vjp(_norm_mod, x_ref[...], g_ref[...], mod_ref[:, 0:D], mod_ref[:, D:2 * D])
        dx, dgain, dshift, dscale = vjp(dh)
        gx_ref[...] = dx + dx1_ref[...]
        first = pl.program_id(1) == 0
        _acc(dmod_ref, jnp.concatenate([dshift, dscale], axis=-1), first)
        _acc(dg_ref, dgain, first)

    return pl.pallas_call(
        body, name="in_proj_bwd", grid=(B, S // bt),
        in_specs=[_tok(bt, D), _per_batch(6 * D), _full((1, D)), _resident((IN_PAD, D))] + [_tok(bt, w) for w in widths]
        + [_tok(bt, 128), _tok(bt, D)],
        out_specs=[_tok(bt, D), _tok(bt, IN_PAD), _per_batch(2 * D), _per_batch(D)],
        out_shape=[_sds((B, S, D), F32), _sds((B, S, IN_PAD), BF), _sds((B, 1, 2 * D), F32), _sds((B, 1, D), F32)],
        compiler_params=_cparams(dimension_semantics=("parallel", "arbitrary")),
    )(x, mod, norm1_g, w_in_t, *pieces, dba, dx1)


def _matmul_tn(tag, a, b, bk, bn, bt, col_blocks=False, ex=None):
    T, K = a.shape
    N = b.shape[1]
    nt = T // bt
    if col_blocks:
        assert bk == K
        out_spec = pl.BlockSpec((None, bk, bn), lambda i, j, t: (j, 0, 0))
        out_shape = _sds((N // bn, K, bn), F32)
    else:
        out_spec = pl.BlockSpec((bk, bn), lambda i, j, t: (i, j))
        out_shape = _sds((K, N), F32)

    def body(a_ref, b_ref, o_ref, acc_ref):
        t = pl.program_id(2)

        @pl.when(t == 0)
        def _():
            acc_ref[...] = jnp.zeros_like(acc_ref)
        acc_ref[...] += lax.dot_general(a_ref[...], b_ref[...], (((0,), (0,)), ((), ())), preferred_element_type=F32)

        @pl.when(t == nt - 1)
        def _():
            o_ref[...] = acc_ref[...]

    (out,), landed = _hosted_call(
        body, f"grad_{tag}", (K // bk, N // bn, nt),
        in_specs=[pl.BlockSpec((bt, bk), lambda i, j, t: (t, i)), pl.BlockSpec((bt, bn), lambda i, j, t: (t, j))],
        out_specs=[out_spec], out_shape=[out_shape],
        scratch_shapes=[pltpu.VMEM((bk, bn), F32)],
        semantics=("parallel", "parallel", "arbitrary"), ins=(a, b), ex=ex)
    return out if ex is None else (out, landed)


def _rope_table(positions):
    inv_freq = THETA ** (-jnp.arange(0, ROT, 2, dtype=F32) / ROT)
    rest = jnp.zeros((HD - ROT,), F32)
    freq = jnp.concatenate([inv_freq, inv_freq, rest] * 2)
    sign = jnp.concatenate([-jnp.ones_like(inv_freq), jnp.ones_like(inv_freq), rest] * 2)
    ang = positions.astype(F32)[..., None] * freq
    return jnp.cos(ang), jnp.sin(ang) * sign


def _lane_pad(v, offset, width=128):
    return jnp.zeros((1, width), F32).at[0, offset:offset + v.shape[-1]].set(v.reshape(-1))


def _tile(S, want):
    return min(S, want)


class _Hosted:
    def __init__(self, call):
        self.call = call
        self.outs = None

    def __call__(self, ex):
        self.outs, landed = self.call(ex)
        return landed


def _local_step(x, mod, positions, tgt, norm1_g, w_in_pad, conv_w, q_norm_g, k_norm_g, sinks, a_log, dt_bias,
                dn_norm_g, w_branch, w_out, norm2_g, w_gu, w_dn, dist=None):
    B, S, _ = x.shape
    T = B * S
    cos_t, sin_t = _rope_table(positions)
    qg2 = jnp.concatenate([q_norm_g, q_norm_g], axis=-1)
    kg2 = jnp.concatenate([k_norm_g, k_norm_g], axis=-1)
    alog = _lane_pad(a_log, DN_H)
    dtb = _lane_pad(dt_bias, DN_H)
    conv2 = conv_w.reshape(CONV, CONVW)
    bt = _tile(S, 512)
    bt_mlp = _tile(S, 256)

    q, kv, dn, z, ga, gb, ba, h1 = _in_proj(x, mod, norm1_g, w_in_pad, bt)
    qkv_n = _qk_prep_fwd(q, kv, cos_t, sin_t, qg2, kg2, bt)
    o_attn = _attn_fwd(qkv_n, sinks)
    dqkv, bg, dn_y = _dn_prep(dn, ba, conv2, alog, dtb, bt)
    intra = _Hosted(lambda ex: _dn_intra_fwd(dqkv, bg, ex))
    if dist is None:
        intra(None)
    else:
        f_br, f_out, w_gu, f_dn = _gather_weights("late", [w_branch, w_out, w_gu, w_dn], dist[0], host=intra)
        w_branch, w_out, w_dn = (f.reshape(N_CHIP * f.shape[1], f.shape[2]) for f in (f_br, f_out, f_dn))
    dn_u, dn_w, dn_qd, dn_kd, dn_a, dn_cd, dn_tinv = intra.outs
    (o_raw, states), _ = _dn_rec_fwd(dn_u, dn_w, dn_qd, dn_kd, dn_a, dn_cd)
    x1, o_dn, merged = _mix_fwd(o_attn, o_raw, z, ga, gb, x, mod, dn_norm_g, w_branch, w_out, bt)
    dx1, h2, act, dgu, dyy, loss, dmod2, dnorm2 = _mlp(x1, tgt, mod, norm2_g, w_gu, w_dn, bt_mlp)

    def flat(t):
        return t.reshape(T, t.shape[-1])
    tn = functools.partial(_matmul_tn, bt=_tile(T, 1024))
    g_w_dn = tn("w_down", flat(act), flat(dyy), bk=FFN, bn=D // 2)
    g_w_gu = tn("w_gate_up", flat(h2), flat(dgu), bk=D, bn=2 * FFN // N_CHIP, col_blocks=True)

    mix_b = _Hosted(lambda ex: _mix_bwd(o_attn, o_raw, z, ga, gb, x, mod, dn_norm_g, w_branch, w_out, dx1, bt_mlp, ex))
    rec_b = _Hosted(lambda ex: _dn_rec_bwd(dn_u, dn_w, dn_qd, dn_kd, dn_a, dn_cd, states, mix_b.outs[1], ex))
    intra_b = _Hosted(lambda ex: _dn_intra_bwd(dqkv, bg, dn_tinv, *rec_b.outs, ex))
    if dist is None:
        for host in (mix_b, rec_b, intra_b):
            host(None)
    else:
        g_w_gu, g_w_dn = _reduce_grads(("w_gate_up", "w_down"), [g_w_gu, g_w_dn.reshape(N_CHIP, -1, D)], *dist,
                                       hosts=[mix_b, rec_b, intra_b])
    d_oa, _, dz, dga, dgb, dya, dyd, dout, dgate1, ddn_g = mix_b.outs
    d_dqkv, dbg = intra_b.outs
    g_w_out = tn("w_out", flat(merged), flat(dout), bk=D, bn=D)
    g_w_br = jnp.concatenate([tn("w_branch_attn", flat(o_attn), flat(dya), bk=QW, bn=D),
                              tn("w_branch_dn", flat(o_dn), flat(dyd), bk=DNW, bn=D)], axis=0)
    prep_b = _Hosted(lambda ex: _dn_prep_bwd(dn, dn_y, ba, conv2, alog, dtb, d_dqkv, dbg, bt, ex))
    attn_b = _Hosted(lambda ex: _attn_bwd(qkv_n, sinks, d_oa, ex))
    qk_b = _Hosted(lambda ex: _qk_prep_bwd(q, kv, cos_t, sin_t, qg2, kg2, *attn_b.outs[:2], bt, ex))
    if dist is None:
        for host in (prep_b, attn_b, qk_b):
            host(None)
    else:
        g_w_br, g_w_out = _reduce_grads(("w_branch", "w_out"), [g_w_br.reshape(N_CHIP, -1, D), g_w_out.reshape(N_CHIP, -1, D)],
                                        *dist, hosts=[prep_b, attn_b, qk_b])
    d_dn, dba, dconv, dalog, ddtb = prep_b.outs
    dsk = attn_b.outs[2]
    dq, dkv, dqg2, dkg2 = qk_b.outs
    dqg = dqg2[:, :HD] + dqg2[:, HD:]
    dkg = dkg2[:, :HD] + dkg2[:, HD:]
    grad_x, dproj, dmod1, dnorm1 = _in_proj_bwd(x, mod, norm1_g, w_in_pad.T, (dq, dkv, d_dn, dz, dga, dgb), dba, dx1, bt)
    g_w_in = _Hosted(lambda ex: (tn("w_in", flat(h1), flat(dproj), bk=D, bn=IN_PAD // 3), ()) if ex is None
                     else tn("w_in", flat(h1), flat(dproj), bk=D, bn=IN_PAD // 3, ex=ex))
    if dist is None:
        g_w_in(None)
        g_w_in = g_w_in.outs

    dmod = jnp.concatenate([dmod1, dgate1, dmod2], axis=-1)
    small = dict(norm1_g=jnp.sum(dnorm1, axis=0), norm2_g=jnp.sum(dnorm2, axis=0), q_norm_g=dqg, k_norm_g=dkg,
                 sinks=dsk, a_log=dalog[:, DN_H:2 * DN_H], dt_bias=ddtb[:, DN_H:2 * DN_H],
                 dn_norm_g=jnp.sum(ddn_g, axis=0), conv_w=dconv)
    return jnp.sum(loss), grad_x, dmod, small, (g_w_in, g_w_br, g_w_out, g_w_gu, g_w_dn)


def _flip(me, f):
    return (me[0] ^ ((f >> 2) & 1), me[1] ^ ((f >> 1) & 1), me[2] ^ (f & 1))


def _comm_call(name, ex):
    n_in, n_out = len(ex.ins), len(ex.out_shapes)

    def body(*refs):
        out_refs, sems = refs[n_in:n_in + n_out], refs[n_in + n_out:]
        cps = _exchange_copies(ex, refs[:n_in], out_refs, sems[0], sems[1])
        for cp in cps:
            cp.start()
        for cp in cps:
            cp.wait_recv()
        if ex.n_forward:
            fwd = _forward_copies(ex, out_refs, sems[2], sems[3])
            for cp in fwd:
                cp.start()
            for cp in fwd:
                cp.wait_recv()
            cps = cps + fwd
        for cp in cps:
            cp.wait_send()

    any_spec = pl.BlockSpec(memory_space=pl.ANY)
    return pl.pallas_call(
        body, name=name, in_specs=[any_spec] * n_in, out_specs=[any_spec] * n_out, out_shape=list(ex.out_shapes),
        scratch_shapes=_exchange_sems(ex),
    )(*ex.ins)


def _by_origin(own, received, index):
    stack = jnp.concatenate([own[None], received], axis=0)
    n = stack.shape[0]
    return jnp.stack([lax.dynamic_index_in_dim(stack, k ^ index, 0, keepdims=False) for k in range(n)])


def _gather_devices(name, arrs, dev, host=None):
    def plan(me, in_refs, out_refs):
        return [(a, o.at[f - 1], _flip(me, f)) for a, o in zip(in_refs, out_refs) for f in range(1, N_DEV)]
    outs = tuple(_sds((N_DEV - 1,) + a.shape, a.dtype) for a in arrs)
    got = (host or functools.partial(_comm_call, name))(_Exchange(tuple(arrs), outs, (N_DEV - 1) * len(arrs), plan))
    return [_by_origin(a, g, dev) for a, g in zip(arrs, got)]


def _gather_chips(name, arrs, chip):
    def plan(me, in_refs, out_refs):
        return [(a, o.at[j], _flip(me, 2 * (j + 1))) for a, o in zip(in_refs, out_refs) for j in range(N_CHIP - 1)]
    outs = tuple(_sds((N_CHIP - 1,) + a.shape, a.dtype) for a in arrs)
    got = _comm_call(name, _Exchange(tuple(arrs), outs, (N_CHIP - 1) * len(arrs), plan))
    return [_by_origin(a, g, chip) for a, g in zip(arrs, got)]


def _halves(core, mine, other):
    lo = jnp.where(core == 0, mine, other)
    hi = jnp.where(core == 0, other, mine)
    return jnp.concatenate([lo, hi], axis=-2)


def _swap_cores_ex(arrs):
    def plan(me, in_refs, out_refs):
        return [(g, o, _flip(me, 1)) for g, o in zip(in_refs, out_refs)]
    return _Exchange(tuple(arrs), tuple(_sds(g.shape, g.dtype) for g in arrs), len(arrs), plan)


def _gather_weights(tag, shards, chip, host=None):
    def plan(me, in_refs, out_refs):
        chip_me = 2 * me[0] + me[1]
        remote = []
        for a, o in zip(in_refs, out_refs):
            half = a.shape[0] // 2
            mine = a.at[pl.ds(me[2] * half, half)]
            remote += [(mine, o.at[chip_me, me[2]], _flip(me, 2 * (j + 1))) for j in range(N_CHIP - 1)]
        return remote

    def forward(me, out_refs):
        chip_me = 2 * me[0] + me[1]
        return [(o.at[chip_me ^ (j + 1), me[2]], o.at[chip_me ^ (j + 1), me[2]], _flip(me, 1))
                for o in out_refs for j in range(N_CHIP - 1)]
    run = host or functools.partial(_comm_call, f"weights_{tag}")
    n = (N_CHIP - 1) * len(shards)
    landed = run(_Exchange(tuple(shards), tuple(_sds((N_CHIP, 2, a.shape[0] // 2, a.shape[1]), a.dtype) for a in shards),
                           n, plan, n, forward))
    return [lax.dynamic_update_slice(f.reshape((N_CHIP,) + a.shape), a[None], (chip, 0, 0)) for a, f in zip(shards, landed)]


def _rows(r):
    for br in (512, 352, 256, 128, 64, 32, 16, 8):
        if r % br == 0:
            return br
    raise ValueError(r)


def _pair_add(tag, g, recv, c):
    n, r, cols = g.shape
    half = r // 2
    br = _rows(half)
    nb = half // br

    def body(c_ref, g_ref, r_ref, o_ref):
        o_ref[...] = (g_ref[...] + r_ref[...]).astype(BF)

    return pl.pallas_call(
        body, name=f"pair_add_{tag}",
        grid_spec=pltpu.PrefetchScalarGridSpec(
            num_scalar_prefetch=1, grid=(n, nb),
            in_specs=[pl.BlockSpec((None, br, cols), lambda k, i, c_ref: (k, c_ref[0] * nb + i, 0)),
                      pl.BlockSpec((None, br, cols), lambda k, i, c_ref: (k, i, 0))],
            out_specs=pl.BlockSpec((None, br, cols), lambda k, i, c_ref: (k, i, 0))),
        out_shape=_sds((n, half, cols), BF),
        compiler_params=_cparams(dimension_semantics=("parallel", "parallel")),
    )(c, g, recv)


def _sum_chips(tag, p, q, chip):
    n, r, cols = q.shape
    br = _rows(r)

    def body(chip_ref, p_ref, q_ref, o_ref):
        acc = p_ref[...].astype(F32)
        for k in range(n):
            acc = acc + q_ref[k].astype(F32)
        o_ref[...] = acc

    return pl.pallas_call(
        body, name=f"sum_chips_{tag}",
        grid_spec=pltpu.PrefetchScalarGridSpec(
            num_scalar_prefetch=1, grid=(r // br,),
            in_specs=[pl.BlockSpec((None, br, cols), lambda i, chip_ref: (chip_ref[0], i, 0)),
                      pl.BlockSpec((n, br, cols), lambda i, chip_ref: (0, i, 0))],
            out_specs=pl.BlockSpec((br, cols), lambda i, chip_ref: (i, 0))),
        out_shape=_sds((r, cols), F32),
        compiler_params=_cparams(dimension_semantics=("parallel",)),
    )(chip, p, q)


def _reduce_grads(tags, grads, chip, core, hosts=None):
    core_arr = core.reshape(1).astype(jnp.int32)
    chip_arr = chip.reshape(1).astype(jnp.int32)
    name = "_".join(tags)
    run = hosts or [functools.partial(_comm_call, f"grads_{stage}_{name}") for stage in ("pair", "chips", "swap")]

    def plan_pair(me, in_refs, out_refs):
        remote = []
        for g, o in zip(in_refs, out_refs):
            half = g.shape[1] // 2
            remote += [(g.at[k, pl.ds((1 - me[2]) * half, half)], o.at[k], _flip(me, 1)) for k in range(N_CHIP)]
        return remote
    recv = run[0](_Exchange(tuple(grads), tuple(_sds((N_CHIP, g.shape[1] // 2, g.shape[2]), F32) for g in grads),
                            N_CHIP * len(grads), plan_pair))
    pair = [_pair_add(t, g, r, core_arr) for t, g, r in zip(tags, grads, recv)]

    def plan_chips(me, in_refs, out_refs):
        remote = []
        for p, o in zip(in_refs, out_refs):
            for j in range(N_CHIP - 1):
                peer = _flip(me, 2 * (j + 1))
                remote.append((p.at[2 * peer[0] + peer[1]], o.at[j], peer))
        return remote
    parts = run[1](_Exchange(tuple(pair), tuple(_sds((N_CHIP - 1,) + p.shape[1:], BF) for p in pair),
                             (N_CHIP - 1) * len(pair), plan_chips))
    mine = [_sum_chips(t, p, q, chip_arr) for t, p, q in zip(tags, pair, parts)]
    other = run[2](_swap_cores_ex(mine))
    return [_halves(core, h, o) for h, o in zip(mine, other)]


def _adamw_math(w, g, m, v):
    m = ADAM_B1 * m + (1.0 - ADAM_B1) * g
    v = ADAM_B2 * v + (1.0 - ADAM_B2) * (g * g)
    m_hat = m / (1.0 - ADAM_B1 ** ADAM_STEP)
    v_hat = v / (1.0 - ADAM_B2 ** ADAM_STEP)
    delta = -ADAM_LR * (m_hat / (jnp.sqrt(v_hat) + ADAM_EPS) + ADAM_WD * w)
    return delta, m, v


def _adamw(name, w, g, m, v, ex=None):
    r, cols = w.shape
    br = _rows(r)
    if br * cols * 4 > (1 << 20) and br % 16 == 0:
        br //= 2

    def body(w_ref, g_ref, m_ref, v_ref, d_ref, mo_ref, vo_ref):
        d_ref[...], mo_ref[...], vo_ref[...] = _adamw_math(w_ref[...], g_ref[...], m_ref[...], v_ref[...])

    spec = pl.BlockSpec((br, cols), lambda i: (i, 0))
    outs, landed = _hosted_call(
        body, f"adamw_{name}", (r // br,), in_specs=[spec] * 4, out_specs=[spec] * 3,
        out_shape=[_sds((r, cols), F32)] * 3, scratch_shapes=[], semantics=("parallel",), ins=(w, g, m, v), ex=ex)
    return outs if ex is None else (outs, landed)


def _ada_fwd(c_all, ada_w, ada_b_cols):
    n = c_all.shape[0]

    def body(c_ref, w_ref, b_ref, o_ref):
        o_ref[...] = _mmx(_silu(c_ref[...]), w_ref[...]) + b_ref[...]

    return pl.pallas_call(
        body, name="ada_fwd", out_shape=_sds((n, ada_w.shape[1]), F32), compiler_params=_cparams(),
    )(c_all, ada_w, ada_b_cols)


def _ada_bwd(c_all, dmod_cols, w, m, v, ex=None):
    n = c_all.shape[0]
    r, cols = w.shape
    br = 128

    def body(c_ref, d_ref, w_ref, m_ref, v_ref, g_ref, dl_ref, mo_ref, vo_ref):
        cond = _silu(c_ref[...])
        g = lax.dot_general(cond, d_ref[...], (((0,), (0,)), ((), ())), precision=lax.Precision.HIGHEST,
                            preferred_element_type=F32)
        g_ref[...] = g
        dl_ref[...], mo_ref[...], vo_ref[...] = _adamw_math(w_ref[...], g, m_ref[...], v_ref[...])

    spec = pl.BlockSpec((br, cols), lambda i: (i, 0))
    outs, landed = _hosted_call(
        body, "ada_bwd", (r // br,),
        in_specs=[pl.BlockSpec((n, br), lambda i: (0, i)), pl.BlockSpec((n, cols), lambda i: (0, 0)), spec, spec, spec],
        out_specs=[spec] * 4, out_shape=[_sds((r, cols), F32)] * 4, scratch_shapes=[], semantics=("parallel",),
        ins=(c_all, dmod_cols, w, m, v), ex=ex)
    return outs if ex is None else (outs, landed)


def _sum_devices(parts):
    n, r, cols = parts.shape

    def body(p_ref, o_ref):
        acc = p_ref[0]
        for k in range(1, n):
            acc = acc + p_ref[k]
        o_ref[...] = acc

    return pl.pallas_call(body, name="sum_devices", out_shape=_sds((r, cols), F32), compiler_params=_cparams())(parts)


SMALL_ROWS = 16
_SMALL_SLOTS = dict(norm1_g=(0, 0, D), norm2_g=(1, 0, D), q_norm_g=(2, 0, HD), k_norm_g=(2, 128, HD), sinks=(2, 256, HEADS),
                    a_log=(2, 384, DN_H), dt_bias=(2, 512, DN_H), dn_norm_g=(2, 640, DN_D))
_CONV_ROW = 4
_ADA_B_ROW = 8


def _pack_small(vals, conv, ada_b):
    def row(pieces):
        out, at = [], 0
        for col, val in pieces:
            out += [jnp.zeros((1, col - at), F32), val.reshape(1, -1)]
            at = col + val.size
        return jnp.concatenate(out + [jnp.zeros((1, CONVW - at), F32)], axis=1)
    rows = {}
    for name, (r, col, n) in _SMALL_SLOTS.items():
        rows.setdefault(r, []).append((col, vals[name]))
    blank = jnp.zeros((1, CONVW), F32)
    top = [row(sorted(rows[r], key=lambda p: p[0])) if r in rows else blank for r in range(_CONV_ROW)]
    conv_rows = jnp.concatenate([conv, jnp.zeros((CONV, CONVW - conv.shape[1]), F32)], axis=1)
    tail = jnp.zeros((SMALL_ROWS - _ADA_B_ROW - 4, CONVW), F32)
    return jnp.concatenate(top + [conv_rows, ada_b.reshape(4, CONVW), tail], axis=0)


def _unpack_small(sheet, conv_cols):
    out = {name: sheet[row, col:col + n].reshape(1, n) for name, (row, col, n) in _SMALL_SLOTS.items()}
    out["conv_w"] = sheet[_CONV_ROW:_CONV_ROW + CONV, 0:conv_cols].reshape(1, CONV, 1, conv_cols)
    out["ada_b"] = sheet[_ADA_B_ROW:_ADA_B_ROW + 4, :].reshape(1, 6 * D)
    return out


def _w_in_segments():
    shard = IN_WIDTH // N_CHIP
    cuts = sorted({0, IN_WIDTH, C_Z, C_Z + 2 * DN_H} | {k * shard for k in range(1, N_CHIP)})
    segs = []
    for a, b in zip(cuts[:-1], cuts[1:]):
        k = a // shard
        pad = a if a < C_Z else (C_BA + a - C_Z if a < C_Z + 2 * DN_H else a - 2 * DN_H)
        segs.append((k, a - k * shard, b - k * shard, pad))
    return segs


def _pad_w_in(f):
    parts = [f[k][:, lo:hi] for k, lo, hi, _ in sorted(_w_in_segments(), key=lambda s: s[3])]
    return jnp.concatenate(parts + [jnp.zeros((f.shape[1], IN_PAD - IN_WIDTH), f.dtype)], axis=1)


def _unpad_w_in(g):
    return jnp.stack([jnp.concatenate([g[:, pad:pad + hi - lo] for kk, lo, hi, pad in _w_in_segments() if kk == k], axis=1)
                      for k in range(N_CHIP)])


def _blocks_to_cols(f):
    return f.transpose(1, 0, 2).reshape(f.shape[1], N_CHIP * f.shape[2])


def kernel(x, c, positions, ada_w, ada_b, norm1_g, w_in, conv_w, q_norm_g, k_norm_g, sinks, a_log, dt_bias, dn_norm_g, w_branch, w_out, norm2_g, w_gate_up, w_down, loss_target, m_ada_w, m_ada_b, m_norm1_g, m_w_in, m_conv_w, m_q_norm_g, m_k_norm_g, m_sinks, m_a_log, m_dt_bias, m_dn_norm_g, m_w_branch, m_w_out, m_norm2_g, m_w_gate_up, m_w_down, v_ada_w, v_ada_b, v_norm1_g, v_w_in, v_conv_w, v_q_norm_g, v_k_norm_g, v_sinks, v_a_log, v_dt_bias, v_dn_norm_g, v_w_branch, v_w_out, v_norm2_g, v_w_gate_up, v_w_down):
    ix, iy, ic = lax.axis_index("x"), lax.axis_index("y"), lax.axis_index("c")
    dev = 4 * ix + 2 * iy + ic
    chip = 2 * ix + iy
    n_seq = x.shape[0]
    conv_cols = conv_w.shape[-1]

    c_all, conv_all = _gather_devices("gather_cond", [c, conv_w.reshape(CONV, conv_cols)], dev)
    c_all = c_all.reshape(N_DEV * n_seq, D)
    ada_cols = ada_w.shape[-1]
    ada_b_cols = lax.dynamic_slice(ada_b, (0, chip * ada_cols), (1, ada_cols))
    mod_cols = _ada_fwd(c_all, ada_w[0], ada_b_cols)
    (mod_blocks,) = _gather_chips("gather_mod", [mod_cols], chip)
    mod_all = _blocks_to_cols(mod_blocks)
    mod = lax.dynamic_slice(mod_all, (dev * n_seq, 0), (n_seq, 6 * D)).reshape(n_seq, 1, 6 * D)
    conv_full = _blocks_to_cols(conv_all[0::2])

    (f_in,) = _gather_weights("w_in", [w_in[0].astype(BF)], chip)
    w_in_pad = _pad_w_in(f_in)

    loss, grad_x, dmod, small, (w_in_grad, r_br, r_out, r_gu, r_dn) = _local_step(
        x, mod, positions, loss_target, norm1_g, w_in_pad, conv_full.reshape(CONV, 1, CONVW), q_norm_g, k_norm_g, sinks,
        a_log, dt_bias, dn_norm_g, w_branch[0].astype(BF), w_out[0].astype(BF), norm2_g, w_gate_up[0].astype(BF),
        w_down[0].astype(BF), dist=(chip, ic))
    loss = lax.psum(loss, ("x", "y", "c"))

    part = _pack_small(small, small["conv_w"], jnp.sum(dmod, axis=(0, 1)).reshape(1, 6 * D))
    dmod_all, parts = _gather_devices("gather_small", [dmod.reshape(n_seq, 6 * D), part], dev, host=w_in_grad)
    dmod_all = dmod_all.reshape(N_DEV * n_seq, 6 * D)
    dmod_cols = lax.dynamic_slice(dmod_all, (0, chip * ada_cols), (N_DEV * n_seq, ada_cols))

    up_gu = _Hosted(lambda ex: _adamw("w_gate_up", w_gate_up[0], r_gu, m_w_gate_up[0], v_w_gate_up[0], ex))
    up_ada = _Hosted(lambda ex: _ada_bwd(c_all, dmod_cols, ada_w[0], m_ada_w[0], v_ada_w[0], ex))
    up_dn = _Hosted(lambda ex: _adamw("w_down", w_down[0], r_dn, m_w_down[0], v_w_down[0], ex))
    (r_in,) = _reduce_grads(("w_in",), [_unpad_w_in(w_in_grad.outs)], chip, ic, hosts=[up_gu, up_ada, up_dn])
    ada = up_ada.outs
    big = {"w_gate_up": (r_gu,) + tuple(up_gu.outs), "w_down": (r_dn,) + tuple(up_dn.outs)}
    for name, w, g, m, v in (("w_in", w_in, r_in, m_w_in, v_w_in), ("w_branch", w_branch, r_br, m_w_branch, v_w_branch),
                             ("w_out", w_out, r_out, m_w_out, v_w_out)):
        big[name] = (g,) + tuple(_adamw(name, w[0], g, m[0], v[0]))
    g_small = _unpack_small(_sum_devices(parts), CONVW)
    g_conv = lax.dynamic_slice(g_small["conv_w"].reshape(CONV, CONVW), (0, chip * conv_cols), (CONV, conv_cols))
    g_small["conv_w"] = g_conv.reshape(1, CONV, 1, conv_cols)

    given = dict(norm1_g=(norm1_g, m_norm1_g, v_norm1_g), norm2_g=(norm2_g, m_norm2_g, v_norm2_g),
                 q_norm_g=(q_norm_g, m_q_norm_g, v_q_norm_g), k_norm_g=(k_norm_g, m_k_norm_g, v_k_norm_g),
                 sinks=(sinks, m_sinks, v_sinks), a_log=(a_log, m_a_log, v_a_log), dt_bias=(dt_bias, m_dt_bias, v_dt_bias),
                 dn_norm_g=(dn_norm_g, m_dn_norm_g, v_dn_norm_g))
    sheets = [_pack_small({k: t[j] for k, t in given.items()}, cw.reshape(CONV, conv_cols), ab)
              for j, (cw, ab) in enumerate(((conv_w, ada_b), (m_conv_w, m_ada_b), (v_conv_w, v_ada_b)))]
    g_local = _pack_small(g_small, g_conv, g_small["ada_b"])
    upd = [_unpack_small(s, conv_cols) for s in _adamw("small", sheets[0], g_local, sheets[1], sheets[2])]

    names = ["ada_w", "ada_b", "norm1_g", "w_in", "conv_w", "q_norm_g", "k_norm_g", "sinks", "a_log", "dt_bias", "dn_norm_g",
             "w_branch", "w_out", "norm2_g", "w_gate_up", "w_down"]

    def leaf(name, j):
        if name == "ada_w":
            return ada[j][None]
        if name in big:
            return big[name][j][None]
        return g_small[name] if j == 0 else upd[j - 1][name]

    return (loss, grad_x) + tuple(leaf(n, j) for j in range(4) for n in names)
```

```python
import functools
from typing import Callable, NamedTuple

import jax
import jax.numpy as jnp
import numpy as np
from jax import lax
from jax.experimental import pallas as pl
from jax.experimental.pallas import tpu as pltpu

F32 = jnp.float32
BF = jnp.bfloat16

D = 1024
HEADS = 8
KV_HEADS = 2
GROUP = 4
HD = 64
BLK = 128
ROT = 16
THETA = 500000.0
QW = 512
KVW = 128
DN_H = 4
DN_D = 128
CONV = 4
CHUNK = 64
DNW = 512
CONVW = 1536
FFN = 2816
EPS = 1e-6
IN_WIDTH = 4872
IN_PAD = 4992
C_KV = 512
C_DN = 768
C_Z = 2304
C_GA = 2816
C_GB = 3840
C_BA = 4864
NEG = -1e30
N_DEV = 8
N_CHIP = 4

ADAM_LR = 0.001
ADAM_B1 = 0.9
ADAM_B2 = 0.999
ADAM_EPS = 1e-08
ADAM_WD = 0.01
ADAM_STEP = 10

VMEM_LIMIT = 60 * 1024 * 1024


def _cparams(**kw):
    return pltpu.CompilerParams(vmem_limit_bytes=VMEM_LIMIT, **kw)


def _dg(a, b, ca, cb):
    return lax.dot_general(a.astype(BF), b.astype(BF), (((ca,), (cb,)), ((), ())),
                           preferred_element_type=F32)


@jax.custom_vjp
def _mm(a, b):
    return _dg(a, b, 1, 0)


def _mm_fwd(a, b):
    return _dg(a, b, 1, 0), (a, b)


def _mm_bwd(res, dy):
    a, b = res
    return _dg(dy, b, 1, 1).astype(a.dtype), _dg(a, dy, 0, 0).astype(b.dtype)


_mm.defvjp(_mm_fwd, _mm_bwd)


@jax.custom_vjp
def _mm_nt(a, b):
    return _dg(a, b, 1, 1)


def _mm_nt_fwd(a, b):
    return _dg(a, b, 1, 1), (a, b)


def _mm_nt_bwd(res, dy):
    a, b = res
    return _dg(dy, b, 1, 0).astype(a.dtype), _dg(dy, a, 0, 0).astype(b.dtype)


_mm_nt.defvjp(_mm_nt_fwd, _mm_nt_bwd)


@jax.custom_vjp
def _mm_tn(a, b):
    return _dg(a, b, 0, 0)


def _mm_tn_fwd(a, b):
    return _dg(a, b, 0, 0), (a, b)


def _mm_tn_bwd(res, dy):
    a, b = res
    return _dg(b, dy, 1, 1).astype(a.dtype), _dg(a, dy, 1, 0).astype(b.dtype)


_mm_tn.defvjp(_mm_tn_fwd, _mm_tn_bwd)


def _mmx(a, b):
    return jnp.dot(a, b, precision=lax.Precision.HIGHEST, preferred_element_type=F32)


def _mmx_nt(a, b):
    return lax.dot_general(a, b, (((1,), (1,)), ((), ())), precision=lax.Precision.HIGHEST,
                           preferred_element_type=F32)


def _iota(shape, dim):
    return lax.broadcasted_iota(jnp.int32, shape, dim)


def _sigmoid(x):
    return lax.logistic(x)


def _silu(x):
    return x * _sigmoid(x)


def _softplus(x):
    return jnp.maximum(x, 0.0) + jnp.log(1.0 + jnp.exp(-jnp.abs(x)))


def _rms(x, gain):
    return x * lax.rsqrt(jnp.mean(x * x, axis=-1, keepdims=True) + EPS) * gain


def _norm_mod(x, gain, shift, scale):
    return _rms(x, gain) * (1.0 + scale) + shift


def _split(a):
    hi = a.astype(BF)
    return hi, (a - hi.astype(F32)).astype(BF)


def _dg3(a, b, ca, cb):
    ah, al = _split(a)
    bh, bl = _split(b)

    def dg(x, y):
        return lax.dot_general(x, y, (((ca,), (cb,)), ((), ())), preferred_element_type=F32)
    return dg(ah, bh) + (dg(ah, bl) + dg(al, bh))


@jax.custom_vjp
def _mm3(a, b):
    return _dg3(a, b, 1, 0)


def _mm3_fwd(a, b):
    return _dg3(a, b, 1, 0), (a, b)


def _mm3_bwd(res, dy):
    a, b = res
    return _dg3(dy, b, 1, 1), _dg3(a, dy, 0, 0)


_mm3.defvjp(_mm3_fwd, _mm3_bwd)


def _qk_prep(slabs, gain, cos, sin):
    r = _iota((2 * HD, 2 * HD), 0)
    c = _iota((2 * HD, 2 * HD), 1)
    seg = jnp.where(r // HD == c // HD, 1.0 / HD, 0.0).astype(F32)
    half = ROT // 2
    cd = c % HD
    pair = jnp.where(((cd < half) & (r == c + half)) | ((cd >= half) & (cd < ROT) & (r == c - half)), 1.0, 0.0).astype(F32)
    out = []
    for x in slabs:
        y = x * lax.rsqrt(_mm3(x * x, seg) + EPS) * gain
        out.append(y * cos + _mm3(y, pair) * sin)
    return out


def _attn_block(qs, kc, kp, vc, vp, sinks, has_prev):
    rows = GROUP * BLK
    qi = _iota((rows, 2 * BLK), 0) % BLK + BLK
    kj = _iota((rows, 2 * BLK), 1)
    dist = qi - kj
    valid = (dist >= 0) & (dist < BLK) & ((kj >= BLK) | has_prev)
    grp = _iota((rows, HEADS), 0) // BLK
    col = _iota((rows, HEADS), 1)

    outs = []
    for h in range(KV_HEADS):
        q = jnp.concatenate([qs[h * GROUP + g] for g in range(GROUP)], axis=0)
        k = jnp.concatenate([kp[h], kc[h]], axis=0)
        v = jnp.concatenate([vp[h], vc[h]], axis=0)
        s = _mm_nt(q, k) * (HD ** -0.5)
        s = jnp.where(valid, s, NEG)
        sink = jnp.sum(jnp.where(col == h * GROUP + grp, sinks, 0.0), axis=-1, keepdims=True)
        m = lax.stop_gradient(jnp.maximum(jnp.max(s, axis=-1, keepdims=True), sink))
        p = jnp.exp(s - m)
        probs = p / (jnp.sum(p, axis=-1, keepdims=True) + jnp.exp(sink - m))
        o = _mm(probs, v)
        outs += [o[g * BLK:(g + 1) * BLK] for g in range(GROUP)]
    return outs


def _dn_act(y, normalize):
    s = _silu(y)
    return s * lax.rsqrt(jnp.sum(s * s, axis=-1, keepdims=True) + EPS) if normalize else s


def _dn_gates(ba, alog, dtb):
    lane = _iota(ba.shape, 1)
    beta = _sigmoid(ba)
    g = -jnp.exp(alog) * _softplus(ba + dtb)
    return jnp.where(lane < DN_H, beta, jnp.where(lane < 2 * DN_H, g, 0.0))


def _bdg(a, b, ca, cb):
    return lax.dot_general(a.astype(BF), b.astype(BF), (((ca,), (cb,)), ((0,), (0,))), preferred_element_type=F32)


@jax.custom_vjp
def _bmm(a, b):
    return _bdg(a, b, 2, 1)


def _bmm_fwd(a, b):
    return _bdg(a, b, 2, 1), (a, b)


def _bmm_bwd(res, dy):
    a, b = res
    return _bdg(dy, b, 2, 2), _bdg(a, dy, 1, 1)


_bmm.defvjp(_bmm_fwd, _bmm_bwd)


@jax.custom_vjp
def _bmm_nt(a, b):
    return _bdg(a, b, 2, 2)


def _bmm_nt_fwd(a, b):
    return _bdg(a, b, 2, 2), (a, b)


def _bmm_nt_bwd(res, dy):
    a, b = res
    return _bdg(dy, b, 2, 1), _bdg(dy, a, 1, 1)


_bmm_nt.defvjp(_bmm_nt_fwd, _bmm_nt_bwd)


def _bmmx(a, b):
    return lax.dot_general(a, b, (((2,), (1,)), ((0,), (0,))), precision=lax.Precision.HIGHEST,
                           preferred_element_type=F32)


def _neumann_inverse(lmat):
    C = CHUNK
    eye = jnp.where(_iota((C, C), 0) == _iota((C, C), 1), 1.0, 0.0).astype(F32)[None]
    a = -lmat
    tinv = eye + a
    pw = _bmmx(a, a)
    for _ in range(4):
        both = _bmmx(jnp.concatenate([pw, tinv], axis=1), pw)
        pw, tinv = both[:, :C], tinv + both[:, C:]
    return tinv + _bmmx(tinv, pw)


def _inverse_bwd(tinv, d_tinv):
    x = lax.dot_general(d_tinv, tinv, (((2,), (2,)), ((0,), (0,))), precision=lax.Precision.HIGHEST,
                        preferred_element_type=F32)
    return -lax.dot_general(tinv, x, (((1,), (1,)), ((0,), (0,))), precision=lax.Precision.HIGHEST,
                            preferred_element_type=F32)


@jax.custom_vjp
def _tri_inverse(lmat):
    return _neumann_inverse(lmat)


def _tri_inverse_fwd(lmat):
    tinv = _neumann_inverse(lmat)
    return tinv, tinv


def _tri_inverse_bwd(tinv, d_tinv):
    return (_inverse_bwd(tinv, d_tinv),)


_tri_inverse.defvjp(_tri_inverse_fwd, _tri_inverse_bwd)


@jax.custom_vjp
def _tri_inverse_known(lmat, tinv):
    return tinv


def _tri_inverse_known_fwd(lmat, tinv):
    return tinv, tinv


def _tri_inverse_known_bwd(tinv, d_tinv):
    return _inverse_bwd(tinv, d_tinv), jnp.zeros_like(tinv)


_tri_inverse_known.defvjp(_tri_inverse_known_fwd, _tri_inverse_known_bwd)


def _dn_intra(q, k, v, bg, tinv=None):
    C = CHUNK
    G = bg.shape[0]
    r = _iota((C, C), 0)
    c = _iota((C, C), 1)
    incl = (r >= c)[None]
    strict = (r > c)[None]
    eye = jnp.where(r == c, 1.0, 0.0).astype(F32)[None]
    tri = jnp.broadcast_to(jnp.where(r >= c, 1.0, 0.0).astype(F32)[None], (G, C, C))
    gc_all = _bmmx(tri, bg)
    lane = _iota((C, DN_D), 1)

    def per_head(x, offset):
        return jnp.concatenate([jnp.sum(jnp.where(lane == offset + h, x[g], 0.0), axis=-1, keepdims=True)[None]
                                for g in range(G) for h in range(DN_H)], axis=0)
    beta = per_head(bg, 0)
    gcol = per_head(gc_all, DN_H)
    grow = jnp.sum(eye * gcol, axis=1, keepdims=True)
    glast = jnp.sum(jnp.where(_iota((1, C, 1), 1) == C - 1, gcol, 0.0), axis=1, keepdims=True)
    decay = jnp.exp(jnp.where(incl, gcol - grow, NEG))
    q = q * (DN_D ** -0.5)
    kb = k * beta
    lmat = jnp.where(strict, _bmm_nt(kb, k) * decay, 0.0)
    tinv = _tri_inverse(lmat) if tinv is None else _tri_inverse_known(lmat, tinv)
    egc = jnp.exp(gcol)
    u = _bmm(tinv, v * beta)
    w = _bmm(tinv, kb * egc)
    a = _bmm_nt(q, k) * decay
    return u, w, q * egc, k * jnp.exp(glast - gcol), a, jnp.exp(glast), tinv


@jax.custom_vjp
def _bmm_tn(a, b):
    return _bdg(a, b, 1, 1)


def _bmm_tn_fwd(a, b):
    return _bdg(a, b, 1, 1), (a, b)


def _bmm_tn_bwd(res, dy):
    a, b = res
    return _bdg(b, dy, 2, 2), _bdg(a, dy, 2, 1)


_bmm_tn.defvjp(_bmm_tn_fwd, _bmm_tn_bwd)


def _dn_rec(state, u, w, qd, kd, a, cd):
    v_new = u - _bmm(w, state)
    out = _bmm(qd, state) + _bmm(a, v_new)
    return state * cd + _bmm_tn(kd, v_new), out


def _mix_tile(o_attn, o_raw, zs, ga, gb, x, gate1, dn_g, wb_a, wb_d, w_out, p_ya, p_yd, p_out):
    o_dn = jnp.concatenate([_rms(o_raw[h], dn_g) * _silu(zs[h]) for h in range(DN_H)], axis=-1)
    y_a = _mm(o_attn, wb_a) + p_ya
    y_d = _mm(o_dn, wb_d) + p_yd
    merged = _sigmoid(ga) * y_a + _sigmoid(gb) * y_d
    out = _mm(merged, w_out) + p_out
    return x + gate1 * out, o_dn, merged


def _mlp_tile(x1, gain, shift, scale, gate2, w_gu, w_dn, tgt, p_gu, p_yy):
    h2 = _norm_mod(x1, gain, shift, scale)
    gu = jnp.concatenate([_mm(h2, w) for w in w_gu], axis=-1) + p_gu
    act = _silu(gu[:, :FFN]) * gu[:, FFN:]
    yy = _mm(act, w_dn) + p_yy
    y = x1 + gate2 * yy
    err = y - tgt
    return 0.5 * jnp.sum(err * err) * (1.0 / D), (h2, act)


def _tok(bt, f):
    return pl.BlockSpec((None, bt, f), lambda b, i: (b, i, 0))


def _full(shape):
    return pl.BlockSpec(shape, lambda b, i: (0,) * len(shape))


def _resident(shape):
    return pl.BlockSpec(shape, lambda b, i: (0,) * len(shape), pipeline_mode=pl.Buffered(1))


def _per_batch(f):
    return pl.BlockSpec((None, 1, f), lambda b, i: (b, 0, 0))


def _sds(shape, dtype):
    return jax.ShapeDtypeStruct(shape, dtype)


class _Exchange(NamedTuple):
    ins: tuple
    out_shapes: tuple
    n_remote: int
    plan: Callable
    n_forward: int = 0
    forward: Callable = None


def _remote_copies(remote, send_sems, recv_sems):
    return [pltpu.make_async_remote_copy(src_ref=src, dst_ref=dst, send_sem=send_sems.at[i], recv_sem=recv_sems.at[i],
                                         device_id=peer, device_id_type=pl.DeviceIdType.MESH)
            for i, (src, dst, peer) in enumerate(remote)]


def _exchange_copies(ex, in_refs, out_refs, send_sems, recv_sems):
    remote = ex.plan((lax.axis_index("x"), lax.axis_index("y"), lax.axis_index("c")), in_refs, out_refs)
    assert len(remote) == ex.n_remote
    return _remote_copies(remote, send_sems, recv_sems)


def _forward_copies(ex, out_refs, send_sems, recv_sems):
    remote = ex.forward((lax.axis_index("x"), lax.axis_index("y"), lax.axis_index("c")), out_refs)
    assert len(remote) == ex.n_forward
    return _remote_copies(remote, send_sems, recv_sems)


def _exchange_sems(ex):
    sems = [pltpu.SemaphoreType.DMA((ex.n_remote,)), pltpu.SemaphoreType.DMA((ex.n_remote,))]
    if ex.n_forward:
        sems += [pltpu.SemaphoreType.DMA((ex.n_forward,)), pltpu.SemaphoreType.DMA((ex.n_forward,))]
    return sems


def _hosted_call(body, name, grid, in_specs, out_specs, out_shape, scratch_shapes, semantics, ins, ex=None):
    if ex is None:
        outs = pl.pallas_call(body, name=name, grid=grid, in_specs=in_specs, out_specs=out_specs, out_shape=out_shape,
                              scratch_shapes=scratch_shapes,
                              compiler_params=_cparams(dimension_semantics=semantics))(*ins)
        return outs, ()
    n_in, n_out, n_scr = len(ins), len(out_shape), len(scratch_shapes)
    c_in, c_out = len(ex.ins), len(ex.out_shapes)
    steps = 1
    for g in grid:
        steps *= g

    def wrapped(*refs):
        a, b, c, d = n_in, n_in + c_in, n_in + c_in + n_out, n_in + c_in + n_out + c_out
        scratch, sems = refs[d:d + n_scr], refs[d + n_scr:]
        step = 0
        for axis, g in enumerate(grid):
            step = step * g + pl.program_id(axis)

        def first_phase():
            return _exchange_copies(ex, refs[a:b], refs[c:d], sems[0], sems[1])

        @pl.when(step == 0)
        def _():
            for cp in first_phase():
                cp.start()
        body(*refs[:a], *refs[b:c], *scratch)

        if ex.n_forward:
            @pl.when(step == (3 * steps) // 4)
            def _():
                for cp in first_phase():
                    cp.wait_recv()
                for cp in _forward_copies(ex, refs[c:d], sems[2], sems[3]):
                    cp.start()

        @pl.when(step == steps - 1)
        def _():
            cps = first_phase()
            if ex.n_forward:
                fwd = _forward_copies(ex, refs[c:d], sems[2], sems[3])
                for cp in fwd:
                    cp.wait_recv()
                for cp in cps + fwd:
                    cp.wait_send()
            else:
                for cp in cps:
                    cp.wait_recv()
                for cp in cps:
                    cp.wait_send()

    any_spec = pl.BlockSpec(memory_space=pl.ANY)
    res = pl.pallas_call(
        wrapped, name=name, grid=grid, in_specs=list(in_specs) + [any_spec] * c_in,
        out_specs=list(out_specs) + [any_spec] * c_out, out_shape=list(out_shape) + list(ex.out_shapes),
        scratch_shapes=list(scratch_shapes) + _exchange_sems(ex),
        compiler_params=_cparams(dimension_semantics=("arbitrary",) * len(grid)),
    )(*ins, *ex.ins)
    return res[:n_out], res[n_out:]


def _acc(ref, val, first):
    @pl.when(first)
    def _():
        ref[...] = val

    @pl.when(jnp.logical_not(first))
    def _():
        ref[...] += val


def _in_proj(x, mod, norm1_g, w_in, bt):
    B, S, _ = x.shape

    def body(x_ref, mod_ref, g_ref, w_ref, q_ref, kv_ref, dn_ref, z_ref, ga_ref, gb_ref, ba_ref, h_ref):
        h = _norm_mod(x_ref[...], g_ref[...], mod_ref[:, 0:D], mod_ref[:, D:2 * D]).astype(BF)
        h_ref[...] = h

        def proj(c0, c1):
            return jnp.dot(h, w_ref[:, c0:c1], preferred_element_type=F32)
        q_ref[...] = proj(0, C_KV).astype(BF)
        kv_ref[...] = proj(C_KV, C_DN).astype(BF)
        dn_ref[...] = proj(C_DN, C_Z).astype(BF)
        z_ref[...] = proj(C_Z, C_GA).astype(BF)
        ga_ref[...] = proj(C_GA, C_GB).astype(BF)
        gb_ref[...] = proj(C_GB, C_BA).astype(BF)
        ba_ref[...] = proj(C_BA, IN_PAD)

    widths = (QW, 2 * KVW, CONVW, DNW, D, D)
    return pl.pallas_call(
        body, name="in_proj", grid=(B, S // bt),
        in_specs=[_tok(bt, D), _per_batch(6 * D), _full((1, D)), _resident((D, IN_PAD))],
        out_specs=[_tok(bt, w) for w in widths] + [_tok(bt, 128), _tok(bt, D)],
        out_shape=[_sds((B, S, w), BF) for w in widths] + [_sds((B, S, 128), F32), _sds((B, S, D), BF)],
        compiler_params=_cparams(dimension_semantics=("parallel", "parallel")),
    )(x, mod, norm1_g, w_in)


def _prev_blk(bt, f):
    return pl.BlockSpec((None, bt, f), lambda b, i: (b, jnp.maximum(i - 1, 0), 0))


QKV = QW + 2 * KVW


def _qk_slabs(q_ref, kv_ref):
    return ([q_ref[:, j * 2 * HD:(j + 1) * 2 * HD].astype(F32) for j in range(QW // (2 * HD))],
            [kv_ref[:, 0:KVW].astype(F32)])


def _qk_prep_fwd(q, kv, cos, sin, qg, kg, bt):
    B, S, _ = q.shape

    def body(q_ref, kv_ref, cos_ref, sin_ref, qg_ref, kg_ref, o_ref):
        qs, ks = _qk_slabs(q_ref, kv_ref)
        qn = _qk_prep(qs, qg_ref[...], cos_ref[...], sin_ref[...])
        kn = _qk_prep(ks, kg_ref[...], cos_ref[...], sin_ref[...])
        for j, t in enumerate(qn + kn):
            o_ref[:, j * 2 * HD:(j + 1) * 2 * HD] = t.astype(BF)
        o_ref[:, QW + KVW:QKV] = kv_ref[:, KVW:2 * KVW]

    return pl.pallas_call(
        body, name="qk_prep_fwd", grid=(B, S // bt),
        in_specs=[_tok(bt, QW), _tok(bt, 2 * KVW), _tok(bt, 2 * HD), _tok(bt, 2 * HD), _full((1, 2 * HD)), _full((1, 2 * HD))],
        out_specs=_tok(bt, QKV), out_shape=_sds((B, S, QKV), BF),
        compiler_params=_cparams(dimension_semantics=("parallel", "parallel")),
    )(q, kv, cos, sin, qg, kg)


def _qk_prep_bwd(q, kv, cos, sin, qg, kg, dqn, dkvn, bt, ex=None):
    B, S, _ = q.shape

    def body(q_ref, kv_ref, cos_ref, sin_ref, qg_ref, kg_ref, dqn_ref, dkvn_ref, dq_ref, dkv_ref, dqg_ref, dkg_ref):
        qs, ks = _qk_slabs(q_ref, kv_ref)
        cos, sin = cos_ref[...], sin_ref[...]

        def f(qs, ks, qg, kg):
            return _qk_prep(qs, qg, cos, sin), _qk_prep(ks, kg, cos, sin)
        _, vjp = jax.vjp(f, qs, ks, qg_ref[...], kg_ref[...])
        n_q = len(qs)
        d_q = [dqn_ref[:, j * 2 * HD:(j + 1) * 2 * HD].astype(F32) for j in range(n_q)]
        d_k = [dkvn_ref[:, 0:KVW].astype(F32)]
        dqs, dks, dqg, dkg = vjp((d_q, d_k))
        for j in range(n_q):
            dq_ref[:, j * 2 * HD:(j + 1) * 2 * HD] = dqs[j].astype(BF)
        dkv_ref[:, 0:KVW] = dks[0].astype(BF)
        dkv_ref[:, KVW:2 * KVW] = dkvn_ref[:, KVW:2 * KVW]
        first = (pl.program_id(0) == 0) & (pl.program_id(1) == 0)
        _acc(dqg_ref, dqg, first)
        _acc(dkg_ref, dkg, first)

    return _hosted_call(
        body, "qk_prep_bwd", (B, S // bt),
        in_specs=[_tok(bt, QW), _tok(bt, 2 * KVW), _tok(bt, 2 * HD), _tok(bt, 2 * HD), _full((1, 2 * HD)), _full((1, 2 * HD)),
                  _tok(bt, QW), _tok(bt, 2 * KVW)],
        out_specs=[_tok(bt, QW), _tok(bt, 2 * KVW), _full((1, 2 * HD)), _full((1, 2 * HD))],
        out_shape=[_sds((B, S, QW), BF), _sds((B, S, 2 * KVW), BF), _sds((1, 2 * HD), F32), _sds((1, 2 * HD), F32)],
        scratch_shapes=[], semantics=("arbitrary", "arbitrary"), ins=(q, kv, cos, sin, qg, kg, dqn, dkvn), ex=ex)


def _attn_load(qkv_ref, kvp_ref):
    qs = [qkv_ref[:, h * HD:(h + 1) * HD].astype(F32) for h in range(HEADS)]
    kc = [qkv_ref[:, QW + h * HD:QW + (h + 1) * HD].astype(F32) for h in range(KV_HEADS)]
    vc = [qkv_ref[:, QW + KVW + h * HD:QW + KVW + (h + 1) * HD].astype(F32) for h in range(KV_HEADS)]
    kp = [kvp_ref[:, h * HD:(h + 1) * HD].astype(F32) for h in range(KV_HEADS)]
    vp = [kvp_ref[:, KVW + h * HD:KVW + (h + 1) * HD].astype(F32) for h in range(KV_HEADS)]
    return qs, kc, kp, vc, vp


def _kv_prev_spec(index):
    return pl.BlockSpec((None, BLK, 2 * KVW), lambda b, i: (b, index(i), QW // (2 * KVW)))


def _attn_fwd(qkv, sinks):
    B, S, _ = qkv.shape

    def body(qkv_ref, kvp_ref, sk_ref, o_ref):
        qs, kc, kp, vc, vp = _attn_load(qkv_ref, kvp_ref)
        outs = _attn_block(qs, kc, kp, vc, vp, sk_ref[...], pl.program_id(1) > 0)
        for h in range(HEADS):
            o_ref[:, h * HD:(h + 1) * HD] = outs[h].astype(BF)

    return pl.pallas_call(
        body, name="attn_fwd", grid=(B, S // BLK),
        in_specs=[_tok(BLK, QKV), _kv_prev_spec(lambda i: jnp.maximum(i - 1, 0)), _full((1, HEADS))],
        out_specs=_tok(BLK, QW), out_shape=_sds((B, S, QW), BF),
        compiler_params=_cparams(dimension_semantics=("parallel", "parallel")),
    )(qkv, qkv, sinks)


def _halo_spec(bt):
    return pl.BlockSpec((None, 8, CONVW), lambda b, i: (b, jnp.maximum(i * (bt // 8) - 1, 0), 0))


def _dn_prep(dn, ba, conv_w, alog, dtb, bt):
    B, S, _ = dn.shape

    strip_rows = min(bt, 64)

    def body(x_ref, halo_ref, ba_ref, cw_ref, al_ref, dt_ref, qkv_ref, bg_ref, y_ref, xe_ref):
        xe_ref[0:8, :] = jnp.where(pl.program_id(1) == 0, 0.0, halo_ref[...].astype(F32))
        xe_ref[8:bt + 8, :] = x_ref[...].astype(F32)

        def strip(k, carry):
            r0 = pl.multiple_of(k * strip_rows, strip_rows)
            rows = pl.ds(r0, strip_rows)
            for j in range(3 * DN_H):
                cols = slice(j * DN_D, (j + 1) * DN_D)
                window = xe_ref[pl.ds(r0, strip_rows + 8), cols]
                y = cw_ref[0:1, cols] * window[5:strip_rows + 5]
                for t in range(1, CONV):
                    y = y + cw_ref[t:t + 1, cols] * window[5 + t:strip_rows + 5 + t]
                y_ref[rows, cols] = y.astype(BF)
                qkv_ref[rows, cols] = _dn_act(y, j < 2 * DN_H)
            bg_ref[rows, :] = _dn_gates(ba_ref[rows, :], al_ref[...], dt_ref[...])
            return carry
        lax.fori_loop(0, bt // strip_rows, strip, 0)

    return pl.pallas_call(
        body, name="dn_prep", grid=(B, S // bt),
        in_specs=[_tok(bt, CONVW), _halo_spec(bt), _tok(bt, 128), _full((CONV, CONVW)), _full((1, 128)), _full((1, 128))],
        out_specs=[_tok(bt, CONVW), _tok(bt, 128), _tok(bt, CONVW)],
        out_shape=[_sds((B, S, CONVW), F32), _sds((B, S, 128), F32), _sds((B, S, CONVW), BF)],
        scratch_shapes=[pltpu.VMEM((bt + 8, CONVW), F32)],
        compiler_params=_cparams(dimension_semantics=("parallel", "arbitrary")),
    )(dn, dn, ba, conv_w, alog, dtb)


def _dn_load(qkv_ref):
    qs = [qkv_ref[:, h * DN_D:(h + 1) * DN_D] for h in range(DN_H)]
    ks = [qkv_ref[:, DNW + h * DN_D:DNW + (h + 1) * DN_D] for h in range(DN_H)]
    vs = [qkv_ref[:, 2 * DNW + h * DN_D:2 * DNW + (h + 1) * DN_D] for h in range(DN_H)]
    return qs, ks, vs


DN_GROUP = 4
AW = DN_H * CHUNK


def _stack_heads(ref, G, offset, width):
    return jnp.stack([ref[g * CHUNK:(g + 1) * CHUNK, offset + h * width:offset + (h + 1) * width]
                      for g in range(G) for h in range(DN_H)])


def _dn_load_stack(qkv_ref, G):
    return tuple(_stack_heads(qkv_ref, G, j * DNW, DN_D) for j in range(3))


def _cd_spec(n):
    return pl.BlockSpec((None, n, 1, DN_D), lambda b, i: (b, i, 0, 0))


def _dn_intra_fwd(qkv, bg, ex=None):
    B, S, _ = qkv.shape
    nc = S // CHUNK
    G = min(DN_GROUP, nc)
    rows = G * CHUNK

    def body(qkv_ref, bg_ref, u_ref, w_ref, qd_ref, kd_ref, a_ref, cd_ref, t_ref):
        q, k, v = _dn_load_stack(qkv_ref, G)
        u, w, qd, kd, a, cd, tinv = _dn_intra(q, k, v, bg_ref[...].reshape(G, CHUNK, DN_D))
        lane_row = _iota((1, DN_D), 1)
        for g in range(G):
            rows = slice(g * CHUNK, (g + 1) * CHUNK)
            cd_row = jnp.zeros((1, DN_D), F32)
            for h in range(DN_H):
                n = g * DN_H + h
                cols = slice(h * DN_D, (h + 1) * DN_D)
                u_ref[rows, cols] = u[n]
                w_ref[rows, cols] = w[n].astype(BF)
                qd_ref[rows, cols] = qd[n].astype(BF)
                kd_ref[rows, cols] = kd[n].astype(BF)
                a_ref[rows, h * CHUNK:(h + 1) * CHUNK] = a[n].astype(BF)
                t_ref[rows, h * CHUNK:(h + 1) * CHUNK] = tinv[n]
                cd_row = cd_row + jnp.where(lane_row == h, cd[n], 0.0)
            cd_ref[g] = cd_row

    return _hosted_call(
        body, "dn_intra_fwd", (B, nc // G),
        in_specs=[_tok(rows, CONVW), _tok(rows, 128)],
        out_specs=[_tok(rows, DNW)] * 4 + [_tok(rows, AW), _cd_spec(G), _tok(rows, AW)],
        out_shape=[_sds((B, S, DNW), F32)] + [_sds((B, S, DNW), BF)] * 3 + [_sds((B, S, AW), BF), _sds((B, nc, 1, DN_D), F32),
                                                                            _sds((B, S, AW), F32)],
        scratch_shapes=[], semantics=("parallel", "parallel"), ins=(qkv, bg), ex=ex)


REC_GROUP = 4


def _rec_stack(ref, B, width, c):
    rows = slice(c * CHUNK, (c + 1) * CHUNK)
    return jnp.stack([ref[b, rows, h * width:(h + 1) * width].astype(F32) for b in range(B) for h in range(DN_H)])


def _rec_load(B, u_ref, w_ref, qd_ref, kd_ref, a_ref, cd_ref, c):
    lane_row = _iota((1, DN_D), 1)
    cd = jnp.stack([jnp.sum(jnp.where(lane_row == h, cd_ref[b, c], 0.0), axis=-1, keepdims=True)
                    for b in range(B) for h in range(DN_H)])
    return (_rec_stack(u_ref, B, DN_D, c), _rec_stack(w_ref, B, DN_D, c), _rec_stack(qd_ref, B, DN_D, c),
            _rec_stack(kd_ref, B, DN_D, c), _rec_stack(a_ref, B, CHUNK, c), cd)


def _rec_store(B, ref, val, width, c):
    for b in range(B):
        for h in range(DN_H):
            ref[b, c * CHUNK:(c + 1) * CHUNK, h * width:(h + 1) * width] = val[b * DN_H + h]


def _rec_specs(B, R, index):
    def tok(f):
        return pl.BlockSpec((B, R * CHUNK, f), lambda i: (0, index(i), 0))
    cd = pl.BlockSpec((B, R, 1, DN_D), lambda i: (0, index(i), 0, 0))
    st = pl.BlockSpec((B, R, DN_H, DN_D, DN_D), lambda i: (0, index(i), 0, 0, 0))
    return tok, cd, st


def _dn_rec_fwd(u, w, qd, kd, a, cd, ex=None):
    B, S, _ = u.shape
    nc = S // CHUNK
    R = REC_GROUP if nc % REC_GROUP == 0 else 1
    tok, cd_spec, st_spec = _rec_specs(B, R, lambda i: i)

    def body(u_ref, w_ref, qd_ref, kd_ref, a_ref, cd_ref, o_ref, st_ref, s_ref):
        @pl.when(pl.program_id(0) == 0)
        def _():
            s_ref[...] = jnp.zeros_like(s_ref)
        state = s_ref[...]
        for c in range(R):
            st_ref[:, c] = state.reshape(B, DN_H, DN_D, DN_D)
            state, out = _dn_rec(state, *_rec_load(B, u_ref, w_ref, qd_ref, kd_ref, a_ref, cd_ref, c))
            _rec_store(B, o_ref, out, DN_D, c)
        s_ref[...] = state

    return _hosted_call(
        body, "dn_rec_fwd", (nc // R,),
        in_specs=[tok(DNW)] * 4 + [tok(AW), cd_spec],
        out_specs=[tok(DNW), st_spec],
        out_shape=[_sds((B, S, DNW), F32), _sds((B, nc, DN_H, DN_D, DN_D), F32)],
        scratch_shapes=[pltpu.VMEM((B * DN_H, DN_D, DN_D), F32)],
        semantics=("arbitrary",), ins=(u, w, qd, kd, a, cd), ex=ex)


def _mix_load(oa_ref, or_ref, z_ref):
    o_raw = [or_ref[:, h * DN_D:(h + 1) * DN_D] for h in range(DN_H)]
    zs = [z_ref[:, h * DN_D:(h + 1) * DN_D].astype(F32) for h in range(DN_H)]
    return oa_ref[...].astype(F32), o_raw, zs


def _mix_fwd(o_attn, o_raw, z, ga, gb, x, mod, dn_g, w_branch, w_out, bt):
    B, S, _ = x.shape

    def body(oa_ref, or_ref, z_ref, ga_ref, gb_ref, x_ref, mod_ref, dg_ref, wb_ref, wo_ref, x1_ref, od_ref, mg_ref):
        oa, o_r, zs = _mix_load(oa_ref, or_ref, z_ref)
        x1, o_dn, merged = _mix_tile(oa, o_r, zs, ga_ref[...].astype(F32), gb_ref[...].astype(F32), x_ref[...],
                                     mod_ref[:, 2 * D:3 * D], dg_ref[...], wb_ref[0:QW, :], wb_ref[QW:2 * QW, :],
                                     wo_ref[...], 0.0, 0.0, 0.0)
        x1_ref[...] = x1
        od_ref[...] = o_dn.astype(BF)
        mg_ref[...] = merged.astype(BF)

    return pl.pallas_call(
        body, name="mix_fwd", grid=(B, S // bt),
        in_specs=[_tok(bt, QW), _tok(bt, DNW), _tok(bt, DNW), _tok(bt, D), _tok(bt, D), _tok(bt, D), _per_batch(6 * D),
                  _full((1, DN_D)), _resident((D, D)), _resident((D, D))],
        out_specs=[_tok(bt, D), _tok(bt, DNW), _tok(bt, D)],
        out_shape=[_sds((B, S, D), F32), _sds((B, S, DNW), BF), _sds((B, S, D), BF)],
        compiler_params=_cparams(dimension_semantics=("parallel", "parallel")),
    )(o_attn, o_raw, z, ga, gb, x, mod, dn_g, w_branch, w_out)


def _mlp(x1, tgt, mod, norm2_g, w_gu, w_dn, bt):
    B, S, _ = x1.shape

    def body(x1_ref, t_ref, mod_ref, g_ref, wgu_ref, wdn_ref,
             dx1_ref, h2_ref, act_ref, dgu_ref, dyy_ref, loss_ref, dmod_ref, dg_ref):
        w_gu_v, w_dn_v, t = [wgu_ref[k] for k in range(N_CHIP)], wdn_ref[...], t_ref[...]

        def f(x1, gain, shift, scale, gate2, p_gu, p_yy):
            return _mlp_tile(x1, gain, shift, scale, gate2, w_gu_v, w_dn_v, t, p_gu, p_yy)
        zero_gu = jnp.zeros((bt, 2 * FFN), F32)
        zero_yy = jnp.zeros((bt, D), F32)
        loss, vjp, (h2, act) = jax.vjp(f, x1_ref[...], g_ref[...], mod_ref[:, 3 * D:4 * D], mod_ref[:, 4 * D:5 * D],
                                       mod_ref[:, 5 * D:6 * D], zero_gu, zero_yy, has_aux=True)
        dx1, dgain, dshift, dscale, dgate2, dgu, dyy = vjp(jnp.ones((), F32))
        dx1_ref[...] = dx1
        h2_ref[...] = h2.astype(BF)
        act_ref[...] = act.astype(BF)
        dgu_ref[...] = dgu.astype(BF)
        dyy_ref[...] = dyy.astype(BF)
        first = pl.program_id(1) == 0
        _acc(loss_ref, jnp.reshape(loss, (1, 1)), first)
        _acc(dmod_ref, jnp.concatenate([dshift, dscale, dgate2], axis=-1), first)
        _acc(dg_ref, dgain, first)

    return pl.pallas_call(
        body, name="mlp", grid=(B, S // bt),
        in_specs=[_tok(bt, D), _tok(bt, D), _per_batch(6 * D), _full((1, D)), _resident((N_CHIP, D, 2 * FFN // N_CHIP)),
                  _resident((FFN, D))],
        out_specs=[_tok(bt, D), _tok(bt, D), _tok(bt, FFN), _tok(bt, 2 * FFN), _tok(bt, D),
                   _per_batch(1), _per_batch(3 * D), _per_batch(D)],
        out_shape=[_sds((B, S, D), F32), _sds((B, S, D), BF), _sds((B, S, FFN), BF), _sds((B, S, 2 * FFN), BF),
                   _sds((B, S, D), BF), _sds((B, 1, 1), F32), _sds((B, 1, 3 * D), F32), _sds((B, 1, D), F32)],
        compiler_params=_cparams(dimension_semantics=("parallel", "arbitrary")),
    )(x1, tgt, mod, norm2_g, w_gu, w_dn)


def _mix_bwd(o_attn, o_raw, z, ga, gb, x, mod, dn_g, w_branch, w_out, dx1, bt, ex=None):
    B, S, _ = x.shape

    def body(oa_ref, or_ref, z_ref, ga_ref, gb_ref, x_ref, mod_ref, dg_ref, wb_ref, wo_ref, dx1_ref,
             doa_ref, dor_ref, dz_ref, dga_ref, dgb_ref, dya_ref, dyd_ref, dout_ref, dgate_ref, ddg_ref):
        oa, o_r, zs = _mix_load(oa_ref, or_ref, z_ref)
        wb_a, wb_d, wo = wb_ref[0:QW, :], wb_ref[QW:2 * QW, :], wo_ref[...]

        def f(oa, o_r, zs, ga, gb, gate1, dn_g, p_ya, p_yd, p_out):
            return _mix_tile(oa, o_r, zs, ga, gb, x_ref[...], gate1, dn_g, wb_a, wb_d, wo, p_ya, p_yd, p_out)[0]
        zero = jnp.zeros((bt, D), F32)
        _, vjp = jax.vjp(f, oa, o_r, zs, ga_ref[...].astype(F32), gb_ref[...].astype(F32), mod_ref[:, 2 * D:3 * D],
                         dg_ref[...], zero, zero, zero)
        doa, dor, dzs, dga, dgb, dgate1, ddn_g, dya, dyd, dout = vjp(dx1_ref[...])
        doa_ref[...] = doa
        for h in range(DN_H):
            dor_ref[:, h * DN_D:(h + 1) * DN_D] = dor[h]
            dz_ref[:, h * DN_D:(h + 1) * DN_D] = dzs[h].astype(BF)
        dga_ref[...] = dga.astype(BF)
        dgb_ref[...] = dgb.astype(BF)
        dya_ref[...] = dya.astype(BF)
        dyd_ref[...] = dyd.astype(BF)
        dout_ref[...] = dout.astype(BF)
        first = pl.program_id(1) == 0
        _acc(dgate_ref, dgate1, first)
        _acc(ddg_ref, ddn_g, first)

    return _hosted_call(
        body, "mix_bwd", (B, S // bt),
        in_specs=[_tok(bt, QW), _tok(bt, DNW), _tok(bt, DNW), _tok(bt, D), _tok(bt, D), _tok(bt, D), _per_batch(6 * D),
                  _full((1, DN_D)), _resident((D, D)), _resident((D, D)), _tok(bt, D)],
        out_specs=[_tok(bt, QW), _tok(bt, DNW), _tok(bt, DNW), _tok(bt, D), _tok(bt, D), _tok(bt, D), _tok(bt, D), _tok(bt, D),
                   _per_batch(D), _per_batch(DN_D)],
        out_shape=[_sds((B, S, QW), F32), _sds((B, S, DNW), F32), _sds((B, S, DNW), BF), _sds((B, S, D), BF),
                   _sds((B, S, D), BF), _sds((B, S, D), BF), _sds((B, S, D), BF), _sds((B, S, D), BF),
                   _sds((B, 1, D), F32), _sds((B, 1, DN_D), F32)],
        scratch_shapes=[], semantics=("parallel", "arbitrary"),
        ins=(o_attn, o_raw, z, ga, gb, x, mod, dn_g, w_branch, w_out, dx1), ex=ex)


def _dn_rec_bwd(u, w, qd, kd, a, cd, states, d_o, ex=None):
    B, S, _ = u.shape
    nc = S // CHUNK
    R = REC_GROUP if nc % REC_GROUP == 0 else 1
    tok, cd_spec, st_spec = _rec_specs(B, R, lambda i: nc // R - 1 - i)

    def body(u_ref, w_ref, qd_ref, kd_ref, a_ref, cd_ref, st_ref, do_ref,
             du_ref, dw_ref, dqd_ref, dkd_ref, da_ref, dcd_ref, ds_ref):
        @pl.when(pl.program_id(0) == 0)
        def _():
            ds_ref[...] = jnp.zeros_like(ds_ref)
        lane_row = _iota((1, DN_D), 1)
        d_state = ds_ref[...]
        for c in reversed(range(R)):
            state = st_ref[:, c].reshape(B * DN_H, DN_D, DN_D)
            _, vjp = jax.vjp(_dn_rec, state, *_rec_load(B, u_ref, w_ref, qd_ref, kd_ref, a_ref, cd_ref, c))
            d_state, du, dw, dqd, dkd, da, dcd = vjp((d_state, _rec_stack(do_ref, B, DN_D, c)))
            for ref, val, width in ((du_ref, du, DN_D), (dw_ref, dw, DN_D), (dqd_ref, dqd, DN_D), (dkd_ref, dkd, DN_D),
                                    (da_ref, da, CHUNK)):
                _rec_store(B, ref, val, width, c)
            for b in range(B):
                row = jnp.zeros((1, DN_D), F32)
                for h in range(DN_H):
                    row = row + jnp.where(lane_row == h, dcd[b * DN_H + h], 0.0)
                dcd_ref[b, c] = row
        ds_ref[...] = d_state

    return _hosted_call(
        body, "dn_rec_bwd", (nc // R,),
        in_specs=[tok(DNW)] * 4 + [tok(AW), cd_spec, st_spec, tok(DNW)],
        out_specs=[tok(DNW)] * 4 + [tok(AW), cd_spec],
        out_shape=[_sds((B, S, DNW), F32)] * 4 + [_sds((B, S, AW), F32), _sds((B, nc, 1, DN_D), F32)],
        scratch_shapes=[pltpu.VMEM((B * DN_H, DN_D, DN_D), F32)],
        semantics=("arbitrary",), ins=(u, w, qd, kd, a, cd, states, d_o), ex=ex)


def _dn_intra_bwd(qkv, bg, tinv, du, dw, dqd, dkd, da, dcd, ex=None):
    B, S, _ = qkv.shape
    nc = S // CHUNK
    G = min(DN_GROUP, nc)
    rows = G * CHUNK

    def body(qkv_ref, bg_ref, t_ref, du_ref, dw_ref, dqd_ref, dkd_ref, da_ref, dcd_ref, dqkv_ref, dbg_ref):
        q, k, v = _dn_load_stack(qkv_ref, G)
        known = _stack_heads(t_ref, G, 0, CHUNK)
        _, vjp = jax.vjp(lambda q, k, v, bg: _dn_intra(q, k, v, bg, known)[:6], q, k, v,
                         bg_ref[...].reshape(G, CHUNK, DN_D))
        lane_row = _iota((1, DN_D), 1)
        dcd = jnp.stack([jnp.sum(jnp.where(lane_row == h, dcd_ref[g], 0.0), axis=-1, keepdims=True)
                         for g in range(G) for h in range(DN_H)])
        dq, dk, dv, dbg = vjp((_stack_heads(du_ref, G, 0, DN_D), _stack_heads(dw_ref, G, 0, DN_D),
                               _stack_heads(dqd_ref, G, 0, DN_D), _stack_heads(dkd_ref, G, 0, DN_D),
                               _stack_heads(da_ref, G, 0, CHUNK), dcd))
        for g in range(G):
            rows = slice(g * CHUNK, (g + 1) * CHUNK)
            for h in range(DN_H):
                n = g * DN_H + h
                dqkv_ref[rows, h * DN_D:(h + 1) * DN_D] = dq[n]
                dqkv_ref[rows, DNW + h * DN_D:DNW + (h + 1) * DN_D] = dk[n]
                dqkv_ref[rows, 2 * DNW + h * DN_D:2 * DNW + (h + 1) * DN_D] = dv[n]
        dbg_ref[...] = dbg.reshape(G * CHUNK, DN_D)

    return _hosted_call(
        body, "dn_intra_bwd", (B, nc // G),
        in_specs=[_tok(rows, CONVW), _tok(rows, 128), _tok(rows, AW)] + [_tok(rows, DNW)] * 4 + [_tok(rows, AW), _cd_spec(G)],
        out_specs=[_tok(rows, CONVW), _tok(rows, 128)],
        out_shape=[_sds((B, S, CONVW), F32), _sds((B, S, 128), F32)],
        scratch_shapes=[], semantics=("parallel", "parallel"), ins=(qkv, bg, tinv, du, dw, dqd, dkd, da, dcd), ex=ex)


def _dn_prep_bwd(dn, y, ba, conv_w, alog, dtb, dqkv, dbg, bt, ex=None):
    B, S, _ = dn.shape
    nt = S // bt
    strip_rows = min(bt, 64)

    def rev(f):
        return pl.BlockSpec((None, bt, f), lambda b, i: (b, nt - 1 - i, 0))

    def body(x_ref, y_ref, ba_ref, cw_ref, al_ref, dt_ref, dqkv_ref, dbg_ref,
             dx_ref, dba_ref, dcw_ref, dal_ref, ddt_ref, dye_ref):
        i = pl.program_id(1)
        @pl.when(i == 0)
        def _():
            dye_ref[bt:bt + 8, :] = jnp.zeros((8, CONVW), F32)

        @pl.when(i > 0)
        def _():
            dye_ref[bt:bt + 8, :] = dye_ref[0:8, :]

        n_strips = bt // strip_rows

        def strip(k, carry):
            dal, ddt, dcw = carry
            r0 = pl.multiple_of((n_strips - 1 - k) * strip_rows, strip_rows)
            rows = pl.ds(r0, strip_rows)
            dcw_slabs = []
            for j in range(3 * DN_H):
                cols = slice(j * DN_D, (j + 1) * DN_D)
                _, vjp = jax.vjp(functools.partial(_dn_act, normalize=j < 2 * DN_H), y_ref[rows, cols].astype(F32))
                (dye_ref[rows, cols],) = vjp(dqkv_ref[rows, cols])
                window = dye_ref[pl.ds(r0, strip_rows + 8), cols]
                shifted = [window[3 - t:strip_rows + 3 - t] for t in range(CONV)]
                dx = cw_ref[0:1, cols] * shifted[0]
                for t in range(1, CONV):
                    dx = dx + cw_ref[t:t + 1, cols] * shifted[t]
                dx_ref[rows, cols] = dx.astype(BF)
                x = x_ref[rows, cols].astype(F32)
                dcw_slabs.append(jnp.concatenate([jnp.sum(shifted[t] * x, axis=0, keepdims=True) for t in range(CONV)], axis=0))
            _, vjp = jax.vjp(_dn_gates, ba_ref[rows, :], al_ref[...], dt_ref[...])
            dba_ref[rows, :], da, dd = vjp(dbg_ref[rows, :])
            return dal + da, ddt + dd, dcw + jnp.concatenate(dcw_slabs, axis=1)
        zero = jnp.zeros((1, DN_D), F32)
        dal, ddt, dcw = lax.fori_loop(0, n_strips, strip, (zero, zero, jnp.zeros((CONV, CONVW), F32)))
        first = (i == 0) & (pl.program_id(0) == 0)
        _acc(dcw_ref, dcw, first)
        _acc(dal_ref, dal, first)
        _acc(ddt_ref, ddt, first)

    return _hosted_call(
        body, "dn_prep_bwd", (B, nt),
        in_specs=[rev(CONVW), rev(CONVW), rev(128), _full((CONV, CONVW)), _full((1, 128)), _full((1, 128)), rev(CONVW), rev(128)],
        out_specs=[rev(CONVW), rev(128), _full((CONV, CONVW)), _full((1, 128)), _full((1, 128))],
        out_shape=[_sds((B, S, CONVW), BF), _sds((B, S, 128), F32), _sds((CONV, CONVW), F32), _sds((1, 128), F32),
                   _sds((1, 128), F32)],
        scratch_shapes=[pltpu.VMEM((bt + 8, CONVW), F32)],
        semantics=("arbitrary", "arbitrary"), ins=(dn, y, ba, conv_w, alog, dtb, dqkv, dbg), ex=ex)


def _attn_bwd(qkv, sinks, d_o, ex=None):
    B, S, _ = qkv.shape
    nb = S // BLK

    def cur(f):
        return pl.BlockSpec((None, BLK, f), lambda b, i: (b, jnp.minimum(i, nb - 1), 0))

    def out_prev(f):
        return pl.BlockSpec((None, BLK, f), lambda b, i: (b, jnp.maximum(i - 1, 0), 0))

    def body(qkv_ref, kvp_ref, sk_ref, do_ref, dq_ref, dkv_ref, dsk_ref, carry_ref):
        n = pl.program_id(1)
        first = (n == 0) & (pl.program_id(0) == 0)

        @pl.when(n == 0)
        def _():
            carry_ref[...] = jnp.zeros_like(carry_ref)

        @pl.when(n < nb)
        def _():
            qs, kc, kp, vc, vp = _attn_load(qkv_ref, kvp_ref)

            def f(qs, kc, kp, vc, vp, sk):
                return _attn_block(qs, kc, kp, vc, vp, sk, n > 0)
            _, vjp = jax.vjp(f, qs, kc, kp, vc, vp, sk_ref[...])
            d_outs = [do_ref[:, h * HD:(h + 1) * HD] for h in range(HEADS)]
            dqs, dkc, dkp, dvc, dvp, dsk = vjp(d_outs)
            for h in range(HEADS):
                dq_ref[:, h * HD:(h + 1) * HD] = dqs[h].astype(BF)
            for h in range(KV_HEADS):
                ksl = slice(h * HD, (h + 1) * HD)
                vsl = slice(KVW + h * HD, KVW + (h + 1) * HD)
                dkv_ref[:, ksl] = (carry_ref[:, ksl] + dkp[h]).astype(BF)
                dkv_ref[:, vsl] = (carry_ref[:, vsl] + dvp[h]).astype(BF)
                carry_ref[:, ksl] = dkc[h]
                carry_ref[:, vsl] = dvc[h]
            _acc(dsk_ref, dsk, first)

        @pl.when(n == nb)
        def _():
            dkv_ref[...] = carry_ref[...].astype(BF)

    return _hosted_call(
        body, "attn_bwd", (B, nb + 1),
        in_specs=[cur(QKV), _kv_prev_spec(lambda i: jnp.maximum(jnp.minimum(i, nb - 1) - 1, 0)), _full((1, HEADS)), cur(QW)],
        out_specs=[cur(QW), out_prev(2 * KVW), _full((1, HEADS))],
        out_shape=[_sds((B, S, QW), BF), _sds((B, S, 2 * KVW), BF), _sds((1, HEADS), F32)],
        scratch_shapes=[pltpu.VMEM((BLK, 2 * KVW), F32)],
        semantics=("arbitrary", "arbitrary"), ins=(qkv, qkv, sinks, d_o), ex=ex)


def _in_proj_bwd(x, mod, norm1_g, w_in, pieces, dba, dx1, bt):
    B, S, _ = x.shape
    widths = (QW, 2 * KVW, CONVW, DNW, D, D)

    def body(x_ref, mod_ref, g_ref, w_ref, dq_ref, dkv_ref, ddn_ref, dz_ref, dga_ref, dgb_ref, dba_ref, dx1_ref,
             gx_ref, dp_ref, dmod_ref, dg_ref):
        dp = jnp.concatenate([r[...] for r in (dq_ref, dkv_ref, ddn_ref, dz_ref, dga_ref, dgb_ref)]
                             + [dba_ref[...].astype(BF)], axis=-1)
        dp_ref[...] = dp
        dh = lax.dot_general(dp, w_ref[...], (((1,), (1,)), ((), ())), preferred_element_type=F32)
        _, vjp = jax.vjp(_norm_mod, x_ref[...], g_ref[...], mod_ref[:, 0:D], mod_ref[:, D:2 * D])
        dx, dgain, dshift, dscale = vjp(dh)
        gx_ref[...] = dx + dx1_ref[...]
        first = pl.program_id(1) == 0
        _acc(dmod_ref, jnp.concatenate([dshift, dscale], axis=-1), first)
        _acc(dg_ref, dgain, first)

    return pl.pallas_call(
        body, name="in_proj_bwd", grid=(B, S // bt),
        in_specs=[_tok(bt, D), _per_batch(6 * D), _full((1, D)), _resident((D, IN_PAD))] + [_tok(bt, w) for w in widths]
        + [_tok(bt, 128), _tok(bt, D)],
        out_specs=[_tok(bt, D), _tok(bt, IN_PAD), _per_batch(2 * D), _per_batch(D)],
        out_shape=[_sds((B, S, D), F32), _sds((B, S, IN_PAD), BF), _sds((B, 1, 2 * D), F32), _sds((B, 1, D), F32)],
        compiler_params=_cparams(dimension_semantics=("parallel", "arbitrary")),
    )(x, mod, norm1_g, w_in, *pieces, dba, dx1)


def _matmul_tn(tag, a, b, bk, bn, bt, col_blocks=False, ex=None):
    T, K = a.shape
    N = b.shape[1]
    nt = T // bt
    if col_blocks:
        assert bk == K
        out_spec = pl.BlockSpec((None, bk, bn), lambda i, j, t: (j, 0, 0))
        out_shape = _sds((N // bn, K, bn), F32)
    else:
        out_spec = pl.BlockSpec((bk, bn), lambda i, j, t: (i, j))
        out_shape = _sds((K, N), F32)

    def body(a_ref, b_ref, o_ref, acc_ref):
        t = pl.program_id(2)

        @pl.when(t == 0)
        def _():
            acc_ref[...] = jnp.zeros_like(acc_ref)
        acc_ref[...] += lax.dot_general(a_ref[...], b_ref[...], (((0,), (0,)), ((), ())), preferred_element_type=F32)

        @pl.when(t == nt - 1)
        def _():
            o_ref[...] = acc_ref[...]

    (out,), landed = _hosted_call(
        body, f"grad_{tag}", (K // bk, N // bn, nt),
        in_specs=[pl.BlockSpec((bt, bk), lambda i, j, t: (t, i)), pl.BlockSpec((bt, bn), lambda i, j, t: (t, j))],
        out_specs=[out_spec], out_shape=[out_shape],
        scratch_shapes=[pltpu.VMEM((bk, bn), F32)],
        semantics=("parallel", "parallel", "arbitrary"), ins=(a, b), ex=ex)
    return out if ex is None else (out, landed)


def _rope_table(positions):
    inv_freq = THETA ** (-jnp.arange(0, ROT, 2, dtype=F32) / ROT)
    rest = jnp.zeros((HD - ROT,), F32)
    freq = jnp.concatenate([inv_freq, inv_freq, rest] * 2)
    sign = jnp.concatenate([-jnp.ones_like(inv_freq), jnp.ones_like(inv_freq), rest] * 2)
    ang = positions.astype(F32)[..., None] * freq
    return jnp.cos(ang), jnp.sin(ang) * sign


def _lane_pad(v, offset, width=128):
    return jnp.zeros((1, width), F32).at[0, offset:offset + v.shape[-1]].set(v.reshape(-1))


def _tile(S, want):
    return min(S, want)


class _Hosted:
    def __init__(self, call):
        self.call = call
        self.outs = None

    def __call__(self, ex):
        self.outs, landed = self.call(ex)
        return landed


def _local_step(x, mod, positions, tgt, norm1_g, w_in_pad, conv_w, q_norm_g, k_norm_g, sinks, a_log, dt_bias,
                dn_norm_g, w_branch, w_out, norm2_g, w_gu, w_dn, dist=None):
    B, S, _ = x.shape
    T = B * S
    cos_t, sin_t = _rope_table(positions)
    qg2 = jnp.concatenate([q_norm_g, q_norm_g], axis=-1)
    kg2 = jnp.concatenate([k_norm_g, k_norm_g], axis=-1)
    alog = _lane_pad(a_log, DN_H)
    dtb = _lane_pad(dt_bias, DN_H)
    conv2 = conv_w.reshape(CONV, CONVW)
    bt = _tile(S, 512)
    bt_mlp = _tile(S, 256)

    q, kv, dn, z, ga, gb, ba, h1 = _in_proj(x, mod, norm1_g, w_in_pad, bt)
    qkv_n = _qk_prep_fwd(q, kv, cos_t, sin_t, qg2, kg2, bt)
    o_attn = _attn_fwd(qkv_n, sinks)
    dqkv, bg, dn_y = _dn_prep(dn, ba, conv2, alog, dtb, bt)
    intra = _Hosted(lambda ex: _dn_intra_fwd(dqkv, bg, ex))
    if dist is None:
        intra(None)
    else:
        f_br, f_out, w_gu, f_dn = _gather_weights("late", [w_branch, w_out, w_gu, w_dn], dist[0], host=intra)
        w_branch, w_out, w_dn = (f.reshape(N_CHIP * f.shape[1], f.shape[2]) for f in (f_br, f_out, f_dn))
    dn_u, dn_w, dn_qd, dn_kd, dn_a, dn_cd, dn_tinv = intra.outs
    (o_raw, states), _ = _dn_rec_fwd(dn_u, dn_w, dn_qd, dn_kd, dn_a, dn_cd)
    x1, o_dn, merged = _mix_fwd(o_attn, o_raw, z, ga, gb, x, mod, dn_norm_g, w_branch, w_out, bt)
    dx1, h2, act, dgu, dyy, loss, dmod2, dnorm2 = _mlp(x1, tgt, mod, norm2_g, w_gu, w_dn, bt_mlp)

    def flat(t):
        return t.reshape(T, t.shape[-1])
    tn = functools.partial(_matmul_tn, bt=_tile(T, 2048))
    g_w_dn = tn("w_down", flat(act), flat(dyy), bk=FFN, bn=D // 2)
    g_w_gu = tn("w_gate_up", flat(h2), flat(dgu), bk=D, bn=2 * FFN // N_CHIP, col_blocks=True)

    mix_b = _Hosted(lambda ex: _mix_bwd(o_attn, o_raw, z, ga, gb, x, mod, dn_norm_g, w_branch, w_out, dx1, bt_mlp, ex))
    rec_b = _Hosted(lambda ex: _dn_rec_bwd(dn_u, dn_w, dn_qd, dn_kd, dn_a, dn_cd, states, mix_b.outs[1], ex))
    intra_b = _Hosted(lambda ex: _dn_intra_bwd(dqkv, bg, dn_tinv, *rec_b.outs, ex))
    if dist is None:
        for host in (mix_b, rec_b, intra_b):
            host(None)
    else:
        g_w_gu, g_w_dn = _reduce_grads(("w_gate_up", "w_down"), [g_w_gu, g_w_dn.reshape(N_CHIP, -1, D)], *dist,
                                       hosts=[mix_b, rec_b, intra_b])
    d_oa, _, dz, dga, dgb, dya, dyd, dout, dgate1, ddn_g = mix_b.outs
    d_dqkv, dbg = intra_b.outs
    g_w_out = tn("w_out", flat(merged), flat(dout), bk=D, bn=D)
    g_w_br = jnp.concatenate([tn("w_branch_attn", flat(o_attn), flat(dya), bk=QW, bn=D),
                              tn("w_branch_dn", flat(o_dn), flat(dyd), bk=DNW, bn=D)], axis=0)
    prep_b = _Hosted(lambda ex: _dn_prep_bwd(dn, dn_y, ba, conv2, alog, dtb, d_dqkv, dbg, bt, ex))
    attn_b = _Hosted(lambda ex: _attn_bwd(qkv_n, sinks, d_oa, ex))
    qk_b = _Hosted(lambda ex: _qk_prep_bwd(q, kv, cos_t, sin_t, qg2, kg2, *attn_b.outs[:2], bt, ex))
    if dist is None:
        for host in (prep_b, attn_b, qk_b):
            host(None)
    else:
        g_w_br, g_w_out = _reduce_grads(("w_branch", "w_out"), [g_w_br.reshape(N_CHIP, -1, D), g_w_out.reshape(N_CHIP, -1, D)],
                                        *dist, hosts=[prep_b, attn_b, qk_b])
    d_dn, dba, dconv, dalog, ddtb = prep_b.outs
    dsk = attn_b.outs[2]
    dq, dkv, dqg2, dkg2 = qk_b.outs
    dqg = dqg2[:, :HD] + dqg2[:, HD:]
    dkg = dkg2[:, :HD] + dkg2[:, HD:]
    grad_x, dproj, dmod1, dnorm1 = _in_proj_bwd(x, mod, norm1_g, w_in_pad, (dq, dkv, d_dn, dz, dga, dgb), dba, dx1, bt)
    g_w_in = _Hosted(lambda ex: (tn("w_in", flat(h1), flat(dproj), bk=D, bn=IN_PAD // 3), ()) if ex is None
                     else tn("w_in", flat(h1), flat(dproj), bk=D, bn=IN_PAD // 3, ex=ex))
    if dist is None:
        g_w_in(None)
        g_w_in = g_w_in.outs

    dmod = jnp.concatenate([dmod1, dgate1, dmod2], axis=-1)
    small = dict(norm1_g=jnp.sum(dnorm1, axis=0), norm2_g=jnp.sum(dnorm2, axis=0), q_norm_g=dqg, k_norm_g=dkg,
                 sinks=dsk, a_log=dalog[:, DN_H:2 * DN_H], dt_bias=ddtb[:, DN_H:2 * DN_H],
                 dn_norm_g=jnp.sum(ddn_g, axis=0), conv_w=dconv)
    return jnp.sum(loss), grad_x, dmod, small, (g_w_in, g_w_br, g_w_out, g_w_gu, g_w_dn)


def _flip(me, f):
    return (me[0] ^ ((f >> 2) & 1), me[1] ^ ((f >> 1) & 1), me[2] ^ (f & 1))


def _comm_call(name, ex):
    n_in, n_out = len(ex.ins), len(ex.out_shapes)

    def body(*refs):
        out_refs, sems = refs[n_in:n_in + n_out], refs[n_in + n_out:]
        cps = _exchange_copies(ex, refs[:n_in], out_refs, sems[0], sems[1])
        for cp in cps:
            cp.start()
        for cp in cps:
            cp.wait_recv()
        if ex.n_forward:
            fwd = _forward_copies(ex, out_refs, sems[2], sems[3])
            for cp in fwd:
                cp.start()
            for cp in fwd:
                cp.wait_recv()
            cps = cps + fwd
        for cp in cps:
            cp.wait_send()

    any_spec = pl.BlockSpec(memory_space=pl.ANY)
    return pl.pallas_call(
        body, name=name, in_specs=[any_spec] * n_in, out_specs=[any_spec] * n_out, out_shape=list(ex.out_shapes),
        scratch_shapes=_exchange_sems(ex),
    )(*ex.ins)


def _by_origin(own, received, index):
    stack = jnp.concatenate([own[None], received], axis=0)
    n = stack.shape[0]
    return jnp.stack([lax.dynamic_index_in_dim(stack, k ^ index, 0, keepdims=False) for k in range(n)])


def _gather_devices(name, arrs, dev, host=None):
    def plan(me, in_refs, out_refs):
        return [(a, o.at[f - 1], _flip(me, f)) for a, o in zip(in_refs, out_refs) for f in range(1, N_DEV)]
    outs = tuple(_sds((N_DEV - 1,) + a.shape, a.dtype) for a in arrs)
    got = (host or functools.partial(_comm_call, name))(_Exchange(tuple(arrs), outs, (N_DEV - 1) * len(arrs), plan))
    return [_by_origin(a, g, dev) for a, g in zip(arrs, got)]


def _gather_chips(name, arrs, chip):
    def plan(me, in_refs, out_refs):
        return [(a, o.at[j], _flip(me, 2 * (j + 1))) for a, o in zip(in_refs, out_refs) for j in range(N_CHIP - 1)]
    outs = tuple(_sds((N_CHIP - 1,) + a.shape, a.dtype) for a in arrs)
    got = _comm_call(name, _Exchange(tuple(arrs), outs, (N_CHIP - 1) * len(arrs), plan))
    return [_by_origin(a, g, chip) for a, g in zip(arrs, got)]


def _halves(core, mine, other):
    lo = jnp.where(core == 0, mine, other)
    hi = jnp.where(core == 0, other, mine)
    return jnp.concatenate([lo, hi], axis=-2)


def _swap_cores_ex(arrs):
    def plan(me, in_refs, out_refs):
        return [(g, o, _flip(me, 1)) for g, o in zip(in_refs, out_refs)]
    return _Exchange(tuple(arrs), tuple(_sds(g.shape, g.dtype) for g in arrs), len(arrs), plan)


def _gather_weights(tag, shards, chip, host=None):
    def plan(me, in_refs, out_refs):
        chip_me = 2 * me[0] + me[1]
        remote = []
        for a, o in zip(in_refs, out_refs):
            half = a.shape[0] // 2
            mine = a.at[pl.ds(me[2] * half, half)]
            remote += [(mine, o.at[chip_me, me[2]], _flip(me, 2 * (j + 1))) for j in range(N_CHIP - 1)]
        return remote

    def forward(me, out_refs):
        chip_me = 2 * me[0] + me[1]
        return [(o.at[chip_me ^ (j + 1), me[2]], o.at[chip_me ^ (j + 1), me[2]], _flip(me, 1))
                for o in out_refs for j in range(N_CHIP - 1)]
    run = host or functools.partial(_comm_call, f"weights_{tag}")
    n = (N_CHIP - 1) * len(shards)
    landed = run(_Exchange(tuple(shards), tuple(_sds((N_CHIP, 2, a.shape[0] // 2, a.shape[1]), a.dtype) for a in shards),
                           n, plan, n, forward))
    return [lax.dynamic_update_slice(f.reshape((N_CHIP,) + a.shape), a[None], (chip, 0, 0)) for a, f in zip(shards, landed)]


def _rows(r):
    for br in (512, 352, 256, 128, 64, 32, 16, 8):
        if r % br == 0:
            return br
    raise ValueError(r)


def _pair_add(tag, g, recv, c):
    n, r, cols = g.shape
    half = r // 2
    br = _rows(half)
    nb = half // br

    def body(c_ref, g_ref, r_ref, o_ref):
        o_ref[...] = (g_ref[...] + r_ref[...]).astype(BF)

    return pl.pallas_call(
        body, name=f"pair_add_{tag}",
        grid_spec=pltpu.PrefetchScalarGridSpec(
            num_scalar_prefetch=1, grid=(n, nb),
            in_specs=[pl.BlockSpec((None, br, cols), lambda k, i, c_ref: (k, c_ref[0] * nb + i, 0)),
                      pl.BlockSpec((None, br, cols), lambda k, i, c_ref: (k, i, 0))],
            out_specs=pl.BlockSpec((None, br, cols), lambda k, i, c_ref: (k, i, 0))),
        out_shape=_sds((n, half, cols), BF),
        compiler_params=_cparams(dimension_semantics=("parallel", "parallel")),
    )(c, g, recv)


def _sum_chips(tag, p, q, chip):
    n, r, cols = q.shape
    br = _rows(r)

    def body(chip_ref, p_ref, q_ref, o_ref):
        acc = p_ref[...].astype(F32)
        for k in range(n):
            acc = acc + q_ref[k].astype(F32)
        o_ref[...] = acc

    return pl.pallas_call(
        body, name=f"sum_chips_{tag}",
        grid_spec=pltpu.PrefetchScalarGridSpec(
            num_scalar_prefetch=1, grid=(r // br,),
            in_specs=[pl.BlockSpec((None, br, cols), lambda i, chip_ref: (chip_ref[0], i, 0)),
                      pl.BlockSpec((n, br, cols), lambda i, chip_ref: (0, i, 0))],
            out_specs=pl.BlockSpec((br, cols), lambda i, chip_ref: (i, 0))),
        out_shape=_sds((r, cols), F32),
        compiler_params=_cparams(dimension_semantics=("parallel",)),
    )(chip, p, q)


def _reduce_grads(tags, grads, chip, core, hosts=None):
    core_arr = core.reshape(1).astype(jnp.int32)
    chip_arr = chip.reshape(1).astype(jnp.int32)
    name = "_".join(tags)
    run = hosts or [functools.partial(_comm_call, f"grads_{stage}_{name}") for stage in ("pair", "chips", "swap")]

    def plan_pair(me, in_refs, out_refs):
        remote = []
        for g, o in zip(in_refs, out_refs):
            half = g.shape[1] // 2
            remote += [(g.at[k, pl.ds((1 - me[2]) * half, half)], o.at[k], _flip(me, 1)) for k in range(N_CHIP)]
        return remote
    recv = run[0](_Exchange(tuple(grads), tuple(_sds((N_CHIP, g.shape[1] // 2, g.shape[2]), F32) for g in grads),
                            N_CHIP * len(grads), plan_pair))
    pair = [_pair_add(t, g, r, core_arr) for t, g, r in zip(tags, grads, recv)]

    def plan_chips(me, in_refs, out_refs):
        remote = []
        for p, o in zip(in_refs, out_refs):
            for j in range(N_CHIP - 1):
                peer = _flip(me, 2 * (j + 1))
                remote.append((p.at[2 * peer[0] + peer[1]], o.at[j], peer))
        return remote
    parts = run[1](_Exchange(tuple(pair), tuple(_sds((N_CHIP - 1,) + p.shape[1:], BF) for p in pair),
                             (N_CHIP - 1) * len(pair), plan_chips))
    mine = [_sum_chips(t, p, q, chip_arr) for t, p, q in zip(tags, pair, parts)]
    other = run[2](_swap_cores_ex(mine))
    return [_halves(core, h, o) for h, o in zip(mine, other)]


def _adamw_math(w, g, m, v):
    m = ADAM_B1 * m + (1.0 - ADAM_B1) * g
    v = ADAM_B2 * v + (1.0 - ADAM_B2) * (g * g)
    m_hat = m / (1.0 - ADAM_B1 ** ADAM_STEP)
    v_hat = v / (1.0 - ADAM_B2 ** ADAM_STEP)
    delta = -ADAM_LR * (m_hat / (jnp.sqrt(v_hat) + ADAM_EPS) + ADAM_WD * w)
    return delta, m, v


def _adamw(name, w, g, m, v, ex=None):
    r, cols = w.shape
    br = _rows(r)
    if br * cols * 4 > (1 << 20) and br % 16 == 0:
        br //= 2

    def body(w_ref, g_ref, m_ref, v_ref, d_ref, mo_ref, vo_ref):
        d_ref[...], mo_ref[...], vo_ref[...] = _adamw_math(w_ref[...], g_ref[...], m_ref[...], v_ref[...])

    spec = pl.BlockSpec((br, cols), lambda i: (i, 0))
    outs, landed = _hosted_call(
        body, f"adamw_{name}", (r // br,), in_specs=[spec] * 4, out_specs=[spec] * 3,
        out_shape=[_sds((r, cols), F32)] * 3, scratch_shapes=[], semantics=("parallel",), ins=(w, g, m, v), ex=ex)
    return outs if ex is None else (outs, landed)


def _ada_fwd(c_all, ada_w, ada_b_cols):
    n = c_all.shape[0]

    def body(c_ref, w_ref, b_ref, o_ref):
        o_ref[...] = _mmx(_silu(c_ref[...]), w_ref[...]) + b_ref[...]

    return pl.pallas_call(
        body, name="ada_fwd", out_shape=_sds((n, ada_w.shape[1]), F32), compiler_params=_cparams(),
    )(c_all, ada_w, ada_b_cols)


def _ada_bwd(c_all, dmod_cols, w, m, v, ex=None):
    n = c_all.shape[0]
    r, cols = w.shape
    br = 128

    def body(c_ref, d_ref, w_ref, m_ref, v_ref, g_ref, dl_ref, mo_ref, vo_ref):
        cond = _silu(c_ref[...])
        g = lax.dot_general(cond, d_ref[...], (((0,), (0,)), ((), ())), precision=lax.Precision.HIGHEST,
                            preferred_element_type=F32)
        g_ref[...] = g
        dl_ref[...], mo_ref[...], vo_ref[...] = _adamw_math(w_ref[...], g, m_ref[...], v_ref[...])

    spec = pl.BlockSpec((br, cols), lambda i: (i, 0))
    outs, landed = _hosted_call(
        body, "ada_bwd", (r // br,),
        in_specs=[pl.BlockSpec((n, br), lambda i: (0, i)), pl.BlockSpec((n, cols), lambda i: (0, 0)), spec, spec, spec],
        out_specs=[spec] * 4, out_shape=[_sds((r, cols), F32)] * 4, scratch_shapes=[], semantics=("parallel",),
        ins=(c_all, dmod_cols, w, m, v), ex=ex)
    return outs if ex is None else (outs, landed)


def _sum_devices(parts):
    n, r, cols = parts.shape

    def body(p_ref, o_ref):
        acc = p_ref[0]
        for k in range(1, n):
            acc = acc + p_ref[k]
        o_ref[...] = acc

    return pl.pallas_call(body, name="sum_devices", out_shape=_sds((r, cols), F32), compiler_params=_cparams())(parts)


SMALL_ROWS = 16
_SMALL_SLOTS = dict(norm1_g=(0, 0, D), norm2_g=(1, 0, D), q_norm_g=(2, 0, HD), k_norm_g=(2, 128, HD), sinks=(2, 256, HEADS),
                    a_log=(2, 384, DN_H), dt_bias=(2, 512, DN_H), dn_norm_g=(2, 640, DN_D))
_CONV_ROW = 4
_ADA_B_ROW = 8


def _pack_small(vals, conv, ada_b):
    def row(pieces):
        out, at = [], 0
        for col, val in pieces:
            out += [jnp.zeros((1, col - at), F32), val.reshape(1, -1)]
            at = col + val.size
        return jnp.concatenate(out + [jnp.zeros((1, CONVW - at), F32)], axis=1)
    rows = {}
    for name, (r, col, n) in _SMALL_SLOTS.items():
        rows.setdefault(r, []).append((col, vals[name]))
    blank = jnp.zeros((1, CONVW), F32)
    top = [row(sorted(rows[r], key=lambda p: p[0])) if r in rows else blank for r in range(_CONV_ROW)]
    conv_rows = jnp.concatenate([conv, jnp.zeros((CONV, CONVW - conv.shape[1]), F32)], axis=1)
    tail = jnp.zeros((SMALL_ROWS - _ADA_B_ROW - 4, CONVW), F32)
    return jnp.concatenate(top + [conv_rows, ada_b.reshape(4, CONVW), tail], axis=0)


def _unpack_small(sheet, conv_cols):
    out = {name: sheet[row, col:col + n].reshape(1, n) for name, (row, col, n) in _SMALL_SLOTS.items()}
    out["conv_w"] = sheet[_CONV_ROW:_CONV_ROW + CONV, 0:conv_cols].reshape(1, CONV, 1, conv_cols)
    out["ada_b"] = sheet[_ADA_B_ROW:_ADA_B_ROW + 4, :].reshape(1, 6 * D)
    return out


def _w_in_segments():
    shard = IN_WIDTH // N_CHIP
    cuts = sorted({0, IN_WIDTH, C_Z, C_Z + 2 * DN_H} | {k * shard for k in range(1, N_CHIP)})
    segs = []
    for a, b in zip(cuts[:-1], cuts[1:]):
        k = a // shard
        pad = a if a < C_Z else (C_BA + a - C_Z if a < C_Z + 2 * DN_H else a - 2 * DN_H)
        segs.append((k, a - k * shard, b - k * shard, pad))
    return segs


def _pad_w_in(f):
    parts = [f[k][:, lo:hi] for k, lo, hi, _ in sorted(_w_in_segments(), key=lambda s: s[3])]
    return jnp.concatenate(parts + [jnp.zeros((f.shape[1], IN_PAD - IN_WIDTH), f.dtype)], axis=1)


def _unpad_w_in(g):
    return jnp.stack([jnp.concatenate([g[:, pad:pad + hi - lo] for kk, lo, hi, pad in _w_in_segments() if kk == k], axis=1)
                      for k in range(N_CHIP)])


def _blocks_to_cols(f):
    return f.transpose(1, 0, 2).reshape(f.shape[1], N_CHIP * f.shape[2])


def kernel(x, c, positions, ada_w, ada_b, norm1_g, w_in, conv_w, q_norm_g, k_norm_g, sinks, a_log, dt_bias, dn_norm_g, w_branch, w_out, norm2_g, w_gate_up, w_down, loss_target, m_ada_w, m_ada_b, m_norm1_g, m_w_in, m_conv_w, m_q_norm_g, m_k_norm_g, m_sinks, m_a_log, m_dt_bias, m_dn_norm_g, m_w_branch, m_w_out, m_norm2_g, m_w_gate_up, m_w_down, v_ada_w, v_ada_b, v_norm1_g, v_w_in, v_conv_w, v_q_norm_g, v_k_norm_g, v_sinks, v_a_log, v_dt_bias, v_dn_norm_g, v_w_branch, v_w_out, v_norm2_g, v_w_gate_up, v_w_down):
    ix, iy, ic = lax.axis_index("x"), lax.axis_index("y"), lax.axis_index("c")
    dev = 4 * ix + 2 * iy + ic
    chip = 2 * ix + iy
    n_seq = x.shape[0]
    conv_cols = conv_w.shape[-1]

    c_all, conv_all = _gather_devices("gather_cond", [c, conv_w.reshape(CONV, conv_cols)], dev)
    c_all = c_all.reshape(N_DEV * n_seq, D)
    ada_cols = ada_w.shape[-1]
    ada_b_cols = lax.dynamic_slice(ada_b, (0, chip * ada_cols), (1, ada_cols))
    mod_cols = _ada_fwd(c_all, ada_w[0], ada_b_cols)
    (mod_blocks,) = _gather_chips("gather_mod", [mod_cols], chip)
    mod_all = _blocks_to_cols(mod_blocks)
    mod = lax.dynamic_slice(mod_all, (dev * n_seq, 0), (n_seq, 6 * D)).reshape(n_seq, 1, 6 * D)
    conv_full = _blocks_to_cols(conv_all[0::2])

    (f_in,) = _gather_weights("w_in", [w_in[0].astype(BF)], chip)
    w_in_pad = _pad_w_in(f_in)

    loss, grad_x, dmod, small, (w_in_grad, r_br, r_out, r_gu, r_dn) = _local_step(
        x, mod, positions, loss_target, norm1_g, w_in_pad, conv_full.reshape(CONV, 1, CONVW), q_norm_g, k_norm_g, sinks,
        a_log, dt_bias, dn_norm_g, w_branch[0].astype(BF), w_out[0].astype(BF), norm2_g, w_gate_up[0].astype(BF),
        w_down[0].astype(BF), dist=(chip, ic))
    loss = lax.psum(loss, ("x", "y", "c"))

    part = _pack_small(small, small["conv_w"], jnp.sum(dmod, axis=(0, 1)).reshape(1, 6 * D))
    dmod_all, parts = _gather_devices("gather_small", [dmod.reshape(n_seq, 6 * D), part], dev, host=w_in_grad)
    dmod_all = dmod_all.reshape(N_DEV * n_seq, 6 * D)
    dmod_cols = lax.dynamic_slice(dmod_all, (0, chip * ada_cols), (N_DEV * n_seq, ada_cols))

    up_gu = _Hosted(lambda ex: _adamw("w_gate_up", w_gate_up[0], r_gu, m_w_gate_up[0], v_w_gate_up[0], ex))
    up_ada = _Hosted(lambda ex: _ada_bwd(c_all, dmod_cols, ada_w[0], m_ada_w[0], v_ada_w[0], ex))
    up_dn = _Hosted(lambda ex: _adamw("w_down", w_down[0], r_dn, m_w_down[0], v_w_down[0], ex))
    (r_in,) = _reduce_grads(("w_in",), [_unpad_w_in(w_in_grad.outs)], chip, ic, hosts=[up_gu, up_ada, up_dn])
    ada = up_ada.outs
    big = {"w_gate_up": (r_gu,) + tuple(up_gu.outs), "w_down": (r_dn,) + tuple(up_dn.outs)}
    for name, w, g, m, v in (("w_in", w_in, r_in, m_w_in, v_w_in), ("w_branch", w_branch, r_br, m_w_branch, v_w_branch),
                             ("w_out", w_out, r_out, m_w_out, v_w_out)):
        big[name] = (g,) + tuple(_adamw(name, w[0], g, m[0], v[0]))
    g_small = _unpack_small(_sum_devices(parts), CONVW)
    g_conv = lax.dynamic_slice(g_small["conv_w"].reshape(CONV, CONVW), (0, chip * conv_cols), (CONV, conv_cols))
    g_small["conv_w"] = g_conv.reshape(1, CONV, 1, conv_cols)

    given = dict(norm1_g=(norm1_g, m_norm1_g, v_norm1_g), norm2_g=(norm2_g, m_norm2_g, v_norm2_g),
                 q_norm_g=(q_norm_g, m_q_norm_g, v_q_norm_g), k_norm_g=(k_norm_g, m_k_norm_g, v_k_norm_g),
                 sinks=(sinks, m_sinks, v_sinks), a_log=(a_log, m_a_log, v_a_log), dt_bias=(dt_bias, m_dt_bias, v_dt_bias),
                 dn_norm_g=(dn_norm_g, m_dn_norm_g, v_dn_norm_g))
    sheets = [_pack_small({k: t[j] for k, t in given.items()}, cw.reshape(CONV, conv_cols), ab)
              for j, (cw, ab) in enumerate(((conv_w, ada_b), (m_conv_w, m_ada_b), (v_conv_w, v_ada_b)))]
    g_local = _pack_small(g_small, g_conv, g_small["ada_b"])
    upd = [_unpack_small(s, conv_cols) for s in _adamw("small", sheets[0], g_local, sheets[1], sheets[2])]

    names = ["ada_w", "ada_b", "norm1_g", "w_in", "conv_w", "q_norm_g", "k_norm_g", "sinks", "a_log", "dt_bias", "dn_norm_g",
             "w_branch", "w_out", "norm2_g", "w_gate_up", "w_down"]

    def leaf(name, j):
        if name == "ada_w":
            return ada[j][None]
        if name in big:
            return big[name][j][None]
        return g_small[name] if j == 0 else upd[j - 1][name]

    return (loss, grad_x) + tuple(leaf(n, j) for j in range(4) for n in names)
```

```python
import functools
from typing import Callable, NamedTuple

import jax
import jax.numpy as jnp
import numpy as np
from jax import lax
from jax.experimental import pallas as pl
from jax.experimental.pallas import tpu as pltpu

F32 = jnp.float32
BF = jnp.bfloat16

D = 1024
HEADS = 8
KV_HEADS = 2
GROUP = 4
HD = 64
BLK = 128
ROT = 16
THETA = 500000.0
QW = 512
KVW = 128
DN_H = 4
DN_D = 128
CONV = 4
CHUNK = 64
DNW = 512
CONVW = 1536
FFN = 2816
EPS = 1e-6
IN_WIDTH = 4872
IN_PAD = 4992
C_KV = 512
C_DN = 768
C_Z = 2304
C_GA = 2816
C_GB = 3840
C_BA = 4864
NEG = -1e30
N_DEV = 8
N_CHIP = 4

ADAM_LR = 0.001
ADAM_B1 = 0.9
ADAM_B2 = 0.999
ADAM_EPS = 1e-08
ADAM_WD = 0.01
ADAM_STEP = 10

VMEM_LIMIT = 60 * 1024 * 1024


def _cparams(**kw):
    return pltpu.CompilerParams(vmem_limit_bytes=VMEM_LIMIT, **kw)


def _dg(a, b, ca, cb):
    return lax.dot_general(a.astype(BF), b.astype(BF), (((ca,), (cb,)), ((), ())),
                           preferred_element_type=F32)


@jax.custom_vjp
def _mm(a, b):
    return _dg(a, b, 1, 0)


def _mm_fwd(a, b):
    return _dg(a, b, 1, 0), (a, b)


def _mm_bwd(res, dy):
    a, b = res
    return _dg(dy, b, 1, 1).astype(a.dtype), _dg(a, dy, 0, 0).astype(b.dtype)


_mm.defvjp(_mm_fwd, _mm_bwd)


@jax.custom_vjp
def _mm_nt(a, b):
    return _dg(a, b, 1, 1)


def _mm_nt_fwd(a, b):
    return _dg(a, b, 1, 1), (a, b)


def _mm_nt_bwd(res, dy):
    a, b = res
    return _dg(dy, b, 1, 0).astype(a.dtype), _dg(dy, a, 0, 0).astype(b.dtype)


_mm_nt.defvjp(_mm_nt_fwd, _mm_nt_bwd)


@jax.custom_vjp
def _mm_tn(a, b):
    return _dg(a, b, 0, 0)


def _mm_tn_fwd(a, b):
    return _dg(a, b, 0, 0), (a, b)


def _mm_tn_bwd(res, dy):
    a, b = res
    return _dg(b, dy, 1, 1).astype(a.dtype), _dg(a, dy, 1, 0).astype(b.dtype)


_mm_tn.defvjp(_mm_tn_fwd, _mm_tn_bwd)


def _mmx(a, b):
    return jnp.dot(a, b, precision=lax.Precision.HIGHEST, preferred_element_type=F32)


def _mmx_nt(a, b):
    return lax.dot_general(a, b, (((1,), (1,)), ((), ())), precision=lax.Precision.HIGHEST,
                           preferred_element_type=F32)


def _iota(shape, dim):
    return lax.broadcasted_iota(jnp.int32, shape, dim)


def _sigmoid(x):
    return lax.logistic(x)


def _silu(x):
    return x * _sigmoid(x)


def _softplus(x):
    return jnp.maximum(x, 0.0) + jnp.log(1.0 + jnp.exp(-jnp.abs(x)))


def _rms(x, gain):
    return x * lax.rsqrt(jnp.mean(x * x, axis=-1, keepdims=True) + EPS) * gain


def _norm_mod(x, gain, shift, scale):
    return _rms(x, gain) * (1.0 + scale) + shift


def _split(a):
    hi = a.astype(BF)
    return hi, (a - hi.astype(F32)).astype(BF)


def _dg3(a, b, ca, cb):
    ah, al = _split(a)
    bh, bl = _split(b)

    def dg(x, y):
        return lax.dot_general(x, y, (((ca,), (cb,)), ((), ())), preferred_element_type=F32)
    return dg(ah, bh) + (dg(ah, bl) + dg(al, bh))


@jax.custom_vjp
def _mm3(a, b):
    return _dg3(a, b, 1, 0)


def _mm3_fwd(a, b):
    return _dg3(a, b, 1, 0), (a, b)


def _mm3_bwd(res, dy):
    a, b = res
    return _dg3(dy, b, 1, 1), _dg3(a, dy, 0, 0)


_mm3.defvjp(_mm3_fwd, _mm3_bwd)


def _qk_prep(slabs, gain, cos, sin):
    r = _iota((2 * HD, 2 * HD), 0)
    c = _iota((2 * HD, 2 * HD), 1)
    seg = jnp.where(r // HD == c // HD, 1.0 / HD, 0.0).astype(F32)
    half = ROT // 2
    cd = c % HD
    pair = jnp.where(((cd < half) & (r == c + half)) | ((cd >= half) & (cd < ROT) & (r == c - half)), 1.0, 0.0).astype(F32)
    out = []
    for x in slabs:
        y = x * lax.rsqrt(_mm3(x * x, seg) + EPS) * gain
        out.append(y * cos + _mm3(y, pair) * sin)
    return out


def _attn_block(qs, kc, kp, vc, vp, sinks, has_prev):
    rows = GROUP * BLK
    qi = _iota((rows, 2 * BLK), 0) % BLK + BLK
    kj = _iota((rows, 2 * BLK), 1)
    dist = qi - kj
    valid = (dist >= 0) & (dist < BLK) & ((kj >= BLK) | has_prev)
    grp = _iota((rows, HEADS), 0) // BLK
    col = _iota((rows, HEADS), 1)

    outs = []
    for h in range(KV_HEADS):
        q = jnp.concatenate([qs[h * GROUP + g] for g in range(GROUP)], axis=0)
        k = jnp.concatenate([kp[h], kc[h]], axis=0)
        v = jnp.concatenate([vp[h], vc[h]], axis=0)
        s = _mm_nt(q, k) * (HD ** -0.5)
        s = jnp.where(valid, s, NEG)
        sink = jnp.sum(jnp.where(col == h * GROUP + grp, sinks, 0.0), axis=-1, keepdims=True)
        m = lax.stop_gradient(jnp.maximum(jnp.max(s, axis=-1, keepdims=True), sink))
        p = jnp.exp(s - m)
        probs = p / (jnp.sum(p, axis=-1, keepdims=True) + jnp.exp(sink - m))
        o = _mm(probs, v)
        outs += [o[g * BLK:(g + 1) * BLK] for g in range(GROUP)]
    return outs


def _dn_act(y, normalize):
    s = _silu(y)
    return s * lax.rsqrt(jnp.sum(s * s, axis=-1, keepdims=True) + EPS) if normalize else s


def _dn_gates(ba, alog, dtb):
    lane = _iota(ba.shape, 1)
    beta = _sigmoid(ba)
    g = -jnp.exp(alog) * _softplus(ba + dtb)
    return jnp.where(lane < DN_H, beta, jnp.where(lane < 2 * DN_H, g, 0.0))


def _bdg(a, b, ca, cb):
    return lax.dot_general(a.astype(BF), b.astype(BF), (((ca,), (cb,)), ((0,), (0,))), preferred_element_type=F32)


@jax.custom_vjp
def _bmm(a, b):
    return _bdg(a, b, 2, 1)


def _bmm_fwd(a, b):
    return _bdg(a, b, 2, 1), (a, b)


def _bmm_bwd(res, dy):
    a, b = res
    return _bdg(dy, b, 2, 2), _bdg(a, dy, 1, 1)


_bmm.defvjp(_bmm_fwd, _bmm_bwd)


@jax.custom_vjp
def _bmm_nt(a, b):
    return _bdg(a, b, 2, 2)


def _bmm_nt_fwd(a, b):
    return _bdg(a, b, 2, 2), (a, b)


def _bmm_nt_bwd(res, dy):
    a, b = res
    return _bdg(dy, b, 2, 1), _bdg(dy, a, 1, 1)


_bmm_nt.defvjp(_bmm_nt_fwd, _bmm_nt_bwd)


def _bmmx(a, b):
    return lax.dot_general(a, b, (((2,), (1,)), ((0,), (0,))), precision=lax.Precision.HIGHEST,
                           preferred_element_type=F32)


def _neumann_inverse(lmat):
    C = CHUNK
    eye = jnp.where(_iota((C, C), 0) == _iota((C, C), 1), 1.0, 0.0).astype(F32)[None]
    a = -lmat
    tinv = eye + a
    pw = _bmmx(a, a)
    for _ in range(4):
        both = _bmmx(jnp.concatenate([pw, tinv], axis=1), pw)
        pw, tinv = both[:, :C], tinv + both[:, C:]
    return tinv + _bmmx(tinv, pw)


def _inverse_bwd(tinv, d_tinv):
    x = lax.dot_general(d_tinv, tinv, (((2,), (2,)), ((0,), (0,))), precision=lax.Precision.HIGHEST,
                        preferred_element_type=F32)
    return -lax.dot_general(tinv, x, (((1,), (1,)), ((0,), (0,))), precision=lax.Precision.HIGHEST,
                            preferred_element_type=F32)


@jax.custom_vjp
def _tri_inverse(lmat):
    return _neumann_inverse(lmat)


def _tri_inverse_fwd(lmat):
    tinv = _neumann_inverse(lmat)
    return tinv, tinv


def _tri_inverse_bwd(tinv, d_tinv):
    return (_inverse_bwd(tinv, d_tinv),)


_tri_inverse.defvjp(_tri_inverse_fwd, _tri_inverse_bwd)


@jax.custom_vjp
def _tri_inverse_known(lmat, tinv):
    return tinv


def _tri_inverse_known_fwd(lmat, tinv):
    return tinv, tinv


def _tri_inverse_known_bwd(tinv, d_tinv):
    return _inverse_bwd(tinv, d_tinv), jnp.zeros_like(tinv)


_tri_inverse_known.defvjp(_tri_inverse_known_fwd, _tri_inverse_known_bwd)


def _dn_intra(q, k, v, bg, tinv=None):
    C = CHUNK
    G = bg.shape[0]
    r = _iota((C, C), 0)
    c = _iota((C, C), 1)
    incl = (r >= c)[None]
    strict = (r > c)[None]
    eye = jnp.where(r == c, 1.0, 0.0).astype(F32)[None]
    tri = jnp.broadcast_to(jnp.where(r >= c, 1.0, 0.0).astype(F32)[None], (G, C, C))
    gc_all = _bmmx(tri, bg)
    lane = _iota((C, DN_D), 1)

    def per_head(x, offset):
        return jnp.concatenate([jnp.sum(jnp.where(lane == offset + h, x[g], 0.0), axis=-1, keepdims=True)[None]
                                for g in range(G) for h in range(DN_H)], axis=0)
    beta = per_head(bg, 0)
    gcol = per_head(gc_all, DN_H)
    grow = jnp.sum(eye * gcol, axis=1, keepdims=True)
    glast = jnp.sum(jnp.where(_iota((1, C, 1), 1) == C - 1, gcol, 0.0), axis=1, keepdims=True)
    decay = jnp.exp(jnp.where(incl, gcol - grow, NEG))
    q = q * (DN_D ** -0.5)
    kb = k * beta
    lmat = jnp.where(strict, _bmm_nt(kb, k) * decay, 0.0)
    tinv = _tri_inverse(lmat) if tinv is None else _tri_inverse_known(lmat, tinv)
    egc = jnp.exp(gcol)
    u = _bmm(tinv, v * beta)
    w = _bmm(tinv, kb * egc)
    a = _bmm_nt(q, k) * decay
    return u, w, q * egc, k * jnp.exp(glast - gcol), a, jnp.exp(glast), tinv


@jax.custom_vjp
def _bmm_tn(a, b):
    return _bdg(a, b, 1, 1)


def _bmm_tn_fwd(a, b):
    return _bdg(a, b, 1, 1), (a, b)


def _bmm_tn_bwd(res, dy):
    a, b = res
    return _bdg(b, dy, 2, 2), _bdg(a, dy, 2, 1)


_bmm_tn.defvjp(_bmm_tn_fwd, _bmm_tn_bwd)


def _dn_rec(state, u, w, qd, kd, a, cd):
    v_new = u - _bmm(w, state)
    out = _bmm(qd, state) + _bmm(a, v_new)
    return state * cd + _bmm_tn(kd, v_new), out


def _mix_tile(o_attn, o_raw, zs, ga, gb, x, gate1, dn_g, wb_a, wb_d, w_out, p_ya, p_yd, p_out):
    o_dn = jnp.concatenate([_rms(o_raw[h], dn_g) * _silu(zs[h]) for h in range(DN_H)], axis=-1)
    y_a = _mm(o_attn, wb_a) + p_ya
    y_d = _mm(o_dn, wb_d) + p_yd
    merged = _sigmoid(ga) * y_a + _sigmoid(gb) * y_d
    out = _mm(merged, w_out) + p_out
    return x + gate1 * out, o_dn, merged


def _mlp_tile(x1, gain, shift, scale, gate2, w_gu, w_dn, tgt, p_gu, p_yy):
    h2 = _norm_mod(x1, gain, shift, scale)
    gu = jnp.concatenate([_mm(h2, w) for w in w_gu], axis=-1) + p_gu
    act = _silu(gu[:, :FFN]) * gu[:, FFN:]
    yy = _mm(act, w_dn) + p_yy
    y = x1 + gate2 * yy
    err = y - tgt
    return 0.5 * jnp.sum(err * err) * (1.0 / D), (h2, act)


def _tok(bt, f):
    return pl.BlockSpec((None, bt, f), lambda b, i: (b, i, 0))


def _full(shape):
    return pl.BlockSpec(shape, lambda b, i: (0,) * len(shape))


def _resident(shape):
    return pl.BlockSpec(shape, lambda b, i: (0,) * len(shape), pipeline_mode=pl.Buffered(1))


def _per_batch(f):
    return pl.BlockSpec((None, 1, f), lambda b, i: (b, 0, 0))


def _sds(shape, dtype):
    return jax.ShapeDtypeStruct(shape, dtype)


class _Exchange(NamedTuple):
    ins: tuple
    out_shapes: tuple
    n_remote: int
    plan: Callable
    n_forward: int = 0
    forward: Callable = None


def _remote_copies(remote, send_sems, recv_sems):
    return [pltpu.make_async_remote_copy(src_ref=src, dst_ref=dst, send_sem=send_sems.at[i], recv_sem=recv_sems.at[i],
                                         device_id=peer, device_id_type=pl.DeviceIdType.MESH)
            for i, (src, dst, peer) in enumerate(remote)]


def _exchange_copies(ex, in_refs, out_refs, send_sems, recv_sems):
    remote = ex.plan((lax.axis_index("x"), lax.axis_index("y"), lax.axis_index("c")), in_refs, out_refs)
    assert len(remote) == ex.n_remote
    return _remote_copies(remote, send_sems, recv_sems)


def _forward_copies(ex, out_refs, send_sems, recv_sems):
    remote = ex.forward((lax.axis_index("x"), lax.axis_index("y"), lax.axis_index("c")), out_refs)
    assert len(remote) == ex.n_forward
    return _remote_copies(remote, send_sems, recv_sems)


def _exchange_sems(ex):
    sems = [pltpu.SemaphoreType.DMA((ex.n_remote,)), pltpu.SemaphoreType.DMA((ex.n_remote,))]
    if ex.n_forward:
        sems += [pltpu.SemaphoreType.DMA((ex.n_forward,)), pltpu.SemaphoreType.DMA((ex.n_forward,))]
    return sems


def _hosted_call(body, name, grid, in_specs, out_specs, out_shape, scratch_shapes, semantics, ins, ex=None):
    if ex is None:
        outs = pl.pallas_call(body, name=name, grid=grid, in_specs=in_specs, out_specs=out_specs, out_shape=out_shape,
                              scratch_shapes=scratch_shapes,
                              compiler_params=_cparams(dimension_semantics=semantics))(*ins)
        return outs, ()
    n_in, n_out, n_scr = len(ins), len(out_shape), len(scratch_shapes)
    c_in, c_out = len(ex.ins), len(ex.out_shapes)
    steps = 1
    for g in grid:
        steps *= g

    def wrapped(*refs):
        a, b, c, d = n_in, n_in + c_in, n_in + c_in + n_out, n_in + c_in + n_out + c_out
        scratch, sems = refs[d:d + n_scr], refs[d + n_scr:]
        step = 0
        for axis, g in enumerate(grid):
            step = step * g + pl.program_id(axis)

        def first_phase():
            return _exchange_copies(ex, refs[a:b], refs[c:d], sems[0], sems[1])

        @pl.when(step == 0)
        def _():
            for cp in first_phase():
                cp.start()
        body(*refs[:a], *refs[b:c], *scratch)

        if ex.n_forward:
            @pl.when(step == (3 * steps) // 4)
            def _():
                for cp in first_phase():
                    cp.wait_recv()
                for cp in _forward_copies(ex, refs[c:d], sems[2], sems[3]):
                    cp.start()

        @pl.when(step == steps - 1)
        def _():
            cps = first_phase()
            if ex.n_forward:
                fwd = _forward_copies(ex, refs[c:d], sems[2], sems[3])
                for cp in fwd:
                    cp.wait_recv()
                for cp in cps + fwd:
                    cp.wait_send()
            else:
                for cp in cps:
                    cp.wait_recv()
                for cp in cps:
                    cp.wait_send()

    any_spec = pl.BlockSpec(memory_space=pl.ANY)
    res = pl.pallas_call(
        wrapped, name=name, grid=grid, in_specs=list(in_specs) + [any_spec] * c_in,
        out_specs=list(out_specs) + [any_spec] * c_out, out_shape=list(out_shape) + list(ex.out_shapes),
        scratch_shapes=list(scratch_shapes) + _exchange_sems(ex),
        compiler_params=_cparams(dimension_semantics=("arbitrary",) * len(grid)),
    )(*ins, *ex.ins)
    return res[:n_out], res[n_out:]


def _acc(ref, val, first):
    @pl.when(first)
    def _():
        ref[...] = val

    @pl.when(jnp.logical_not(first))
    def _():
        ref[...] += val


def _in_proj(x, mod, norm1_g, w_in, bt):
    B, S, _ = x.shape

    def body(x_ref, mod_ref, g_ref, w_ref, q_ref, kv_ref, dn_ref, z_ref, ga_ref, gb_ref, ba_ref, h_ref):
        h = _norm_mod(x_ref[...], g_ref[...], mod_ref[:, 0:D], mod_ref[:, D:2 * D]).astype(BF)
        h_ref[...] = h

        def proj(c0, c1):
            return jnp.dot(h, w_ref[:, c0:c1], preferred_element_type=F32)
        q_ref[...] = proj(0, C_KV).astype(BF)
        kv_ref[...] = proj(C_KV, C_DN).astype(BF)
        dn_ref[...] = proj(C_DN, C_Z).astype(BF)
        z_ref[...] = proj(C_Z, C_GA).astype(BF)
        ga_ref[...] = proj(C_GA, C_GB).astype(BF)
        gb_ref[...] = proj(C_GB, C_BA).astype(BF)
        ba_ref[...] = proj(C_BA, IN_PAD)

    widths = (QW, 2 * KVW, CONVW, DNW, D, D)
    return pl.pallas_call(
        body, name="in_proj", grid=(B, S // bt),
        in_specs=[_tok(bt, D), _per_batch(6 * D), _full((1, D)), _resident((D, IN_PAD))],
        out_specs=[_tok(bt, w) for w in widths] + [_tok(bt, 128), _tok(bt, D)],
        out_shape=[_sds((B, S, w), BF) for w in widths] + [_sds((B, S, 128), F32), _sds((B, S, D), BF)],
        compiler_params=_cparams(dimension_semantics=("parallel", "parallel")),
    )(x, mod, norm1_g, w_in)


def _prev_blk(bt, f):
    return pl.BlockSpec((None, bt, f), lambda b, i: (b, jnp.maximum(i - 1, 0), 0))


QKV = QW + 2 * KVW


def _qk_slabs(q_ref, kv_ref):
    return ([q_ref[:, j * 2 * HD:(j + 1) * 2 * HD].astype(F32) for j in range(QW // (2 * HD))],
            [kv_ref[:, 0:KVW].astype(F32)])


def _qk_prep_fwd(q, kv, cos, sin, qg, kg, bt):
    B, S, _ = q.shape

    def body(q_ref, kv_ref, cos_ref, sin_ref, qg_ref, kg_ref, o_ref):
        qs, ks = _qk_slabs(q_ref, kv_ref)
        qn = _qk_prep(qs, qg_ref[...], cos_ref[...], sin_ref[...])
        kn = _qk_prep(ks, kg_ref[...], cos_ref[...], sin_ref[...])
        for j, t in enumerate(qn + kn):
            o_ref[:, j * 2 * HD:(j + 1) * 2 * HD] = t.astype(BF)
        o_ref[:, QW + KVW:QKV] = kv_ref[:, KVW:2 * KVW]

    return pl.pallas_call(
        body, name="qk_prep_fwd", grid=(B, S // bt),
        in_specs=[_tok(bt, QW), _tok(bt, 2 * KVW), _tok(bt, 2 * HD), _tok(bt, 2 * HD), _full((1, 2 * HD)), _full((1, 2 * HD))],
        out_specs=_tok(bt, QKV), out_shape=_sds((B, S, QKV), BF),
        compiler_params=_cparams(dimension_semantics=("parallel", "parallel")),
    )(q, kv, cos, sin, qg, kg)


def _qk_prep_bwd(q, kv, cos, sin, qg, kg, dqn, dkvn, bt, ex=None):
    B, S, _ = q.shape

    def body(q_ref, kv_ref, cos_ref, sin_ref, qg_ref, kg_ref, dqn_ref, dkvn_ref, dq_ref, dkv_ref, dqg_ref, dkg_ref):
        qs, ks = _qk_slabs(q_ref, kv_ref)
        cos, sin = cos_ref[...], sin_ref[...]

        def f(qs, ks, qg, kg):
            return _qk_prep(qs, qg, cos, sin), _qk_prep(ks, kg, cos, sin)
        _, vjp = jax.vjp(f, qs, ks, qg_ref[...], kg_ref[...])
        n_q = len(qs)
        d_q = [dqn_ref[:, j * 2 * HD:(j + 1) * 2 * HD].astype(F32) for j in range(n_q)]
        d_k = [dkvn_ref[:, 0:KVW].astype(F32)]
        dqs, dks, dqg, dkg = vjp((d_q, d_k))
        for j in range(n_q):
            dq_ref[:, j * 2 * HD:(j + 1) * 2 * HD] = dqs[j].astype(BF)
        dkv_ref[:, 0:KVW] = dks[0].astype(BF)
        dkv_ref[:, KVW:2 * KVW] = dkvn_ref[:, KVW:2 * KVW]
        first = (pl.program_id(0) == 0) & (pl.program_id(1) == 0)
        _acc(dqg_ref, dqg, first)
        _acc(dkg_ref, dkg, first)

    return _hosted_call(
        body, "qk_prep_bwd", (B, S // bt),
        in_specs=[_tok(bt, QW), _tok(bt, 2 * KVW), _tok(bt, 2 * HD), _tok(bt, 2 * HD), _full((1, 2 * HD)), _full((1, 2 * HD)),
                  _tok(bt, QW), _tok(bt, 2 * KVW)],
        out_specs=[_tok(bt, QW), _tok(bt, 2 * KVW), _full((1, 2 * HD)), _full((1, 2 * HD))],
        out_shape=[_sds((B, S, QW), BF), _sds((B, S, 2 * KVW), BF), _sds((1, 2 * HD), F32), _sds((1, 2 * HD), F32)],
        scratch_shapes=[], semantics=("arbitrary", "arbitrary"), ins=(q, kv, cos, sin, qg, kg, dqn, dkvn), ex=ex)


def _attn_load(qkv_ref, kvp_ref):
    qs = [qkv_ref[:, h * HD:(h + 1) * HD].astype(F32) for h in range(HEADS)]
    kc = [qkv_ref[:, QW + h * HD:QW + (h + 1) * HD].astype(F32) for h in range(KV_HEADS)]
    vc = [qkv_ref[:, QW + KVW + h * HD:QW + KVW + (h + 1) * HD].astype(F32) for h in range(KV_HEADS)]
    kp = [kvp_ref[:, h * HD:(h + 1) * HD].astype(F32) for h in range(KV_HEADS)]
    vp = [kvp_ref[:, KVW + h * HD:KVW + (h + 1) * HD].astype(F32) for h in range(KV_HEADS)]
    return qs, kc, kp, vc, vp


def _kv_prev_spec(index):
    return pl.BlockSpec((None, BLK, 2 * KVW), lambda b, i: (b, index(i), QW // (2 * KVW)))


def _attn_fwd(qkv, sinks):
    B, S, _ = qkv.shape

    def body(qkv_ref, kvp_ref, sk_ref, o_ref):
        qs, kc, kp, vc, vp = _attn_load(qkv_ref, kvp_ref)
        outs = _attn_block(qs, kc, kp, vc, vp, sk_ref[...], pl.program_id(1) > 0)
        for h in range(HEADS):
            o_ref[:, h * HD:(h + 1) * HD] = outs[h].astype(BF)

    return pl.pallas_call(
        body, name="attn_fwd", grid=(B, S // BLK),
        in_specs=[_tok(BLK, QKV), _kv_prev_spec(lambda i: jnp.maximum(i - 1, 0)), _full((1, HEADS))],
        out_specs=_tok(BLK, QW), out_shape=_sds((B, S, QW), BF),
        compiler_params=_cparams(dimension_semantics=("parallel", "parallel")),
    )(qkv, qkv, sinks)


def _halo_spec(bt):
    return pl.BlockSpec((None, 8, CONVW), lambda b, i: (b, jnp.maximum(i * (bt // 8) - 1, 0), 0))


def _dn_prep(dn, ba, conv_w, alog, dtb, bt):
    B, S, _ = dn.shape

    strip_rows = min(bt, 64)

    def body(x_ref, halo_ref, ba_ref, cw_ref, al_ref, dt_ref, qkv_ref, bg_ref, y_ref, xe_ref):
        xe_ref[0:8, :] = jnp.where(pl.program_id(1) == 0, 0.0, halo_ref[...].astype(F32))
        xe_ref[8:bt + 8, :] = x_ref[...].astype(F32)

        def strip(k, carry):
            r0 = pl.multiple_of(k * strip_rows, strip_rows)
            rows = pl.ds(r0, strip_rows)
            for j in range(3 * DN_H):
                cols = slice(j * DN_D, (j + 1) * DN_D)
                window = xe_ref[pl.ds(r0, strip_rows + 8), cols]
                y = cw_ref[0:1, cols] * window[5:strip_rows + 5]
                for t in range(1, CONV):
                    y = y + cw_ref[t:t + 1, cols] * window[5 + t:strip_rows + 5 + t]
                y_ref[rows, cols] = y.astype(BF)
                qkv_ref[rows, cols] = _dn_act(y, j < 2 * DN_H)
            bg_ref[rows, :] = _dn_gates(ba_ref[rows, :], al_ref[...], dt_ref[...])
            return carry
        lax.fori_loop(0, bt // strip_rows, strip, 0)

    return pl.pallas_call(
        body, name="dn_prep", grid=(B, S // bt),
        in_specs=[_tok(bt, CONVW), _halo_spec(bt), _tok(bt, 128), _full((CONV, CONVW)), _full((1, 128)), _full((1, 128))],
        out_specs=[_tok(bt, CONVW), _tok(bt, 128), _tok(bt, CONVW)],
        out_shape=[_sds((B, S, CONVW), F32), _sds((B, S, 128), F32), _sds((B, S, CONVW), BF)],
        scratch_shapes=[pltpu.VMEM((bt + 8, CONVW), F32)],
        compiler_params=_cparams(dimension_semantics=("parallel", "arbitrary")),
    )(dn, dn, ba, conv_w, alog, dtb)


def _dn_load(qkv_ref):
    qs = [qkv_ref[:, h * DN_D:(h + 1) * DN_D] for h in range(DN_H)]
    ks = [qkv_ref[:, DNW + h * DN_D:DNW + (h + 1) * DN_D] for h in range(DN_H)]
    vs = [qkv_ref[:, 2 * DNW + h * DN_D:2 * DNW + (h + 1) * DN_D] for h in range(DN_H)]
    return qs, ks, vs


DN_GROUP = 4
AW = DN_H * CHUNK


def _stack_heads(ref, G, offset, width):
    return jnp.stack([ref[g * CHUNK:(g + 1) * CHUNK, offset + h * width:offset + (h + 1) * width]
                      for g in range(G) for h in range(DN_H)])


def _dn_load_stack(qkv_ref, G):
    return tuple(_stack_heads(qkv_ref, G, j * DNW, DN_D) for j in range(3))


def _cd_spec(n):
    return pl.BlockSpec((None, n, 1, DN_D), lambda b, i: (b, i, 0, 0))


def _dn_intra_fwd(qkv, bg, ex=None):
    B, S, _ = qkv.shape
    nc = S // CHUNK
    G = min(DN_GROUP, nc)
    rows = G * CHUNK

    def body(qkv_ref, bg_ref, u_ref, w_ref, qd_ref, kd_ref, a_ref, cd_ref, t_ref):
        q, k, v = _dn_load_stack(qkv_ref, G)
        u, w, qd, kd, a, cd, tinv = _dn_intra(q, k, v, bg_ref[...].reshape(G, CHUNK, DN_D))
        lane_row = _iota((1, DN_D), 1)
        for g in range(G):
            rows = slice(g * CHUNK, (g + 1) * CHUNK)
            cd_row = jnp.zeros((1, DN_D), F32)
            for h in range(DN_H):
                n = g * DN_H + h
                cols = slice(h * DN_D, (h + 1) * DN_D)
                u_ref[rows, cols] = u[n]
                w_ref[rows, cols] = w[n].astype(BF)
                qd_ref[rows, cols] = qd[n].astype(BF)
                kd_ref[rows, cols] = kd[n].astype(BF)
                a_ref[rows, h * CHUNK:(h + 1) * CHUNK] = a[n].astype(BF)
                t_ref[rows, h * CHUNK:(h + 1) * CHUNK] = tinv[n]
                cd_row = cd_row + jnp.where(lane_row == h, cd[n], 0.0)
            cd_ref[g] = cd_row

    return _hosted_call(
        body, "dn_intra_fwd", (B, nc // G),
        in_specs=[_tok(rows, CONVW), _tok(rows, 128)],
        out_specs=[_tok(rows, DNW)] * 4 + [_tok(rows, AW), _cd_spec(G), _tok(rows, AW)],
        out_shape=[_sds((B, S, DNW), F32)] + [_sds((B, S, DNW), BF)] * 3 + [_sds((B, S, AW), BF), _sds((B, nc, 1, DN_D), F32),
                                                                            _sds((B, S, AW), F32)],
        scratch_shapes=[], semantics=("parallel", "parallel"), ins=(qkv, bg), ex=ex)


REC_GROUP = 4


def _rec_stack(ref, B, width, c):
    rows = slice(c * CHUNK, (c + 1) * CHUNK)
    return jnp.stack([ref[b, rows, h * width:(h + 1) * width].astype(F32) for b in range(B) for h in range(DN_H)])


def _rec_load(B, u_ref, w_ref, qd_ref, kd_ref, a_ref, cd_ref, c):
    lane_row = _iota((1, DN_D), 1)
    cd = jnp.stack([jnp.sum(jnp.where(lane_row == h, cd_ref[b, c], 0.0), axis=-1, keepdims=True)
                    for b in range(B) for h in range(DN_H)])
    return (_rec_stack(u_ref, B, DN_D, c), _rec_stack(w_ref, B, DN_D, c), _rec_stack(qd_ref, B, DN_D, c),
            _rec_stack(kd_ref, B, DN_D, c), _rec_stack(a_ref, B, CHUNK, c), cd)


def _rec_store(B, ref, val, width, c):
    for b in range(B):
        for h in range(DN_H):
            ref[b, c * CHUNK:(c + 1) * CHUNK, h * width:(h + 1) * width] = val[b * DN_H + h]


def _rec_specs(B, R, index):
    def tok(f):
        return pl.BlockSpec((B, R * CHUNK, f), lambda i: (0, index(i), 0))
    cd = pl.BlockSpec((B, R, 1, DN_D), lambda i: (0, index(i), 0, 0))
    st = pl.BlockSpec((B, R, DN_H, DN_D, DN_D), lambda i: (0, index(i), 0, 0, 0))
    return tok, cd, st


def _dn_rec_fwd(u, w, qd, kd, a, cd, ex=None):
    B, S, _ = u.shape
    nc = S // CHUNK
    R = REC_GROUP if nc % REC_GROUP == 0 else 1
    tok, cd_spec, st_spec = _rec_specs(B, R, lambda i: i)

    def body(u_ref, w_ref, qd_ref, kd_ref, a_ref, cd_ref, o_ref, st_ref, s_ref):
        @pl.when(pl.program_id(0) == 0)
        def _():
            s_ref[...] = jnp.zeros_like(s_ref)
        state = s_ref[...]
        for c in range(R):
            st_ref[:, c] = state.reshape(B, DN_H, DN_D, DN_D)
            state, out = _dn_rec(state, *_rec_load(B, u_ref, w_ref, qd_ref, kd_ref, a_ref, cd_ref, c))
            _rec_store(B, o_ref, out, DN_D, c)
        s_ref[...] = state

    return _hosted_call(
        body, "dn_rec_fwd", (nc // R,),
        in_specs=[tok(DNW)] * 4 + [tok(AW), cd_spec],
        out_specs=[tok(DNW), st_spec],
        out_shape=[_sds((B, S, DNW), F32), _sds((B, nc, DN_H, DN_D, DN_D), F32)],
        scratch_shapes=[pltpu.VMEM((B * DN_H, DN_D, DN_D), F32)],
        semantics=("arbitrary",), ins=(u, w, qd, kd, a, cd), ex=ex)


def _mix_load(oa_ref, or_ref, z_ref):
    o_raw = [or_ref[:, h * DN_D:(h + 1) * DN_D] for h in range(DN_H)]
    zs = [z_ref[:, h * DN_D:(h + 1) * DN_D].astype(F32) for h in range(DN_H)]
    return oa_ref[...].astype(F32), o_raw, zs


def _mix_fwd(o_attn, o_raw, z, ga, gb, x, mod, dn_g, w_branch, w_out, bt):
    B, S, _ = x.shape

    def body(oa_ref, or_ref, z_ref, ga_ref, gb_ref, x_ref, mod_ref, dg_ref, wb_ref, wo_ref, x1_ref, od_ref, mg_ref):
        oa, o_r, zs = _mix_load(oa_ref, or_ref, z_ref)
        x1, o_dn, merged = _mix_tile(oa, o_r, zs, ga_ref[...].astype(F32), gb_ref[...].astype(F32), x_ref[...],
                                     mod_ref[:, 2 * D:3 * D], dg_ref[...], wb_ref[0:QW, :], wb_ref[QW:2 * QW, :],
                                     wo_ref[...], 0.0, 0.0, 0.0)
        x1_ref[...] = x1
        od_ref[...] = o_dn.astype(BF)
        mg_ref[...] = merged.astype(BF)

    return pl.pallas_call(
        body, name="mix_fwd", grid=(B, S // bt),
        in_specs=[_tok(bt, QW), _tok(bt, DNW), _tok(bt, DNW), _tok(bt, D), _tok(bt, D), _tok(bt, D), _per_batch(6 * D),
                  _full((1, DN_D)), _resident((D, D)), _resident((D, D))],
        out_specs=[_tok(bt, D), _tok(bt, DNW), _tok(bt, D)],
        out_shape=[_sds((B, S, D), F32), _sds((B, S, DNW), BF), _sds((B, S, D), BF)],
        compiler_params=_cparams(dimension_semantics=("parallel", "parallel")),
    )(o_attn, o_raw, z, ga, gb, x, mod, dn_g, w_branch, w_out)


def _mlp(x1, tgt, mod, norm2_g, w_gu, w_dn, bt):
    B, S, _ = x1.shape

    def body(x1_ref, t_ref, mod_ref, g_ref, wgu_ref, wdn_ref,
             dx1_ref, h2_ref, act_ref, dgu_ref, dyy_ref, loss_ref, dmod_ref, dg_ref):
        w_gu_v, w_dn_v, t = [wgu_ref[k] for k in range(N_CHIP)], wdn_ref[...], t_ref[...]

        def f(x1, gain, shift, scale, gate2, p_gu, p_yy):
            return _mlp_tile(x1, gain, shift, scale, gate2, w_gu_v, w_dn_v, t, p_gu, p_yy)
        zero_gu = jnp.zeros((bt, 2 * FFN), F32)
        zero_yy = jnp.zeros((bt, D), F32)
        loss, vjp, (h2, act) = jax.vjp(f, x1_ref[...], g_ref[...], mod_ref[:, 3 * D:4 * D], mod_ref[:, 4 * D:5 * D],
                                       mod_ref[:, 5 * D:6 * D], zero_gu, zero_yy, has_aux=True)
        dx1, dgain, dshift, dscale, dgate2, dgu, dyy = vjp(jnp.ones((), F32))
        dx1_ref[...] = dx1
        h2_ref[...] = h2.astype(BF)
        act_ref[...] = act.astype(BF)
        dgu_ref[...] = dgu.astype(BF)
        dyy_ref[...] = dyy.astype(BF)
        first = pl.program_id(1) == 0
        _acc(loss_ref, jnp.reshape(loss, (1, 1)), first)
        _acc(dmod_ref, jnp.concatenate([dshift, dscale, dgate2], axis=-1), first)
        _acc(dg_ref, dgain, first)

    return pl.pallas_call(
        body, name="mlp", grid=(B, S // bt),
        in_specs=[_tok(bt, D), _tok(bt, D), _per_batch(6 * D), _full((1, D)), _resident((N_CHIP, D, 2 * FFN // N_CHIP)),
                  _resident((FFN, D))],
        out_specs=[_tok(bt, D), _tok(bt, D), _tok(bt, FFN), _tok(bt, 2 * FFN), _tok(bt, D),
                   _per_batch(1), _per_batch(3 * D), _per_batch(D)],
        out_shape=[_sds((B, S, D), F32), _sds((B, S, D), BF), _sds((B, S, FFN), BF), _sds((B, S, 2 * FFN), BF),
                   _sds((B, S, D), BF), _sds((B, 1, 1), F32), _sds((B, 1, 3 * D), F32), _sds((B, 1, D), F32)],
        compiler_params=_cparams(dimension_semantics=("parallel", "arbitrary")),
    )(x1, tgt, mod, norm2_g, w_gu, w_dn)


def _mix_bwd(o_attn, o_raw, z, ga, gb, x, mod, dn_g, w_branch, w_out, dx1, bt, ex=None):
    B, S, _ = x.shape

    def body(oa_ref, or_ref, z_ref, ga_ref, gb_ref, x_ref, mod_ref, dg_ref, wb_ref, wo_ref, dx1_ref,
             doa_ref, dor_ref, dz_ref, dga_ref, dgb_ref, dya_ref, dyd_ref, dout_ref, dgate_ref, ddg_ref):
        oa, o_r, zs = _mix_load(oa_ref, or_ref, z_ref)
        wb_a, wb_d, wo = wb_ref[0:QW, :], wb_ref[QW:2 * QW, :], wo_ref[...]

        def f(oa, o_r, zs, ga, gb, gate1, dn_g, p_ya, p_yd, p_out):
            return _mix_tile(oa, o_r, zs, ga, gb, x_ref[...], gate1, dn_g, wb_a, wb_d, wo, p_ya, p_yd, p_out)[0]
        zero = jnp.zeros((bt, D), F32)
        _, vjp = jax.vjp(f, oa, o_r, zs, ga_ref[...].astype(F32), gb_ref[...].astype(F32), mod_ref[:, 2 * D:3 * D],
                         dg_ref[...], zero, zero, zero)
        doa, dor, dzs, dga, dgb, dgate1, ddn_g, dya, dyd, dout = vjp(dx1_ref[...])
        doa_ref[...] = doa
        for h in range(DN_H):
            dor_ref[:, h * DN_D:(h + 1) * DN_D] = dor[h]
            dz_ref[:, h * DN_D:(h + 1) * DN_D] = dzs[h].astype(BF)
        dga_ref[...] = dga.astype(BF)
        dgb_ref[...] = dgb.astype(BF)
        dya_ref[...] = dya.astype(BF)
        dyd_ref[...] = dyd.astype(BF)
        dout_ref[...] = dout.astype(BF)
        first = pl.program_id(1) == 0
        _acc(dgate_ref, dgate1, first)
        _acc(ddg_ref, ddn_g, first)

    return _hosted_call(
        body, "mix_bwd", (B, S // bt),
        in_specs=[_tok(bt, QW), _tok(bt, DNW), _tok(bt, DNW), _tok(bt, D), _tok(bt, D), _tok(bt, D), _per_batch(6 * D),
                  _full((1, DN_D)), _resident((D, D)), _resident((D, D)), _tok(bt, D)],
        out_specs=[_tok(bt, QW), _tok(bt, DNW), _tok(bt, DNW), _tok(bt, D), _tok(bt, D), _tok(bt, D), _tok(bt, D), _tok(bt, D),
                   _per_batch(D), _per_batch(DN_D)],
        out_shape=[_sds((B, S, QW), F32), _sds((B, S, DNW), F32), _sds((B, S, DNW), BF), _sds((B, S, D), BF),
                   _sds((B, S, D), BF), _sds((B, S, D), BF), _sds((B, S, D), BF), _sds((B, S, D), BF),
                   _sds((B, 1, D), F32), _sds((B, 1, DN_D), F32)],
        scratch_shapes=[], semantics=("parallel", "arbitrary"),
        ins=(o_attn, o_raw, z, ga, gb, x, mod, dn_g, w_branch, w_out, dx1), ex=ex)


def _dn_rec_bwd(u, w, qd, kd, a, cd, states, d_o, ex=None):
    B, S, _ = u.shape
    nc = S // CHUNK
    R = REC_GROUP if nc % REC_GROUP == 0 else 1
    tok, cd_spec, st_spec = _rec_specs(B, R, lambda i: nc // R - 1 - i)

    def body(u_ref, w_ref, qd_ref, kd_ref, a_ref, cd_ref, st_ref, do_ref,
             du_ref, dw_ref, dqd_ref, dkd_ref, da_ref, dcd_ref, ds_ref):
        @pl.when(pl.program_id(0) == 0)
        def _():
            ds_ref[...] = jnp.zeros_like(ds_ref)
        lane_row = _iota((1, DN_D), 1)
        d_state = ds_ref[...]
        for c in reversed(range(R)):
            state = st_ref[:, c].reshape(B * DN_H, DN_D, DN_D)
            _, vjp = jax.vjp(_dn_rec, state, *_rec_load(B, u_ref, w_ref, qd_ref, kd_ref, a_ref, cd_ref, c))
            d_state, du, dw, dqd, dkd, da, dcd = vjp((d_state, _rec_stack(do_ref, B, DN_D, c)))
            for ref, val, width in ((du_ref, du, DN_D), (dw_ref, dw, DN_D), (dqd_ref, dqd, DN_D), (dkd_ref, dkd, DN_D),
                                    (da_ref, da, CHUNK)):
                _rec_store(B, ref, val, width, c)
            for b in range(B):
                row = jnp.zeros((1, DN_D), F32)
                for h in range(DN_H):
                    row = row + jnp.where(lane_row == h, dcd[b * DN_H + h], 0.0)
                dcd_ref[b, c] = row
        ds_ref[...] = d_state

    return _hosted_call(
        body, "dn_rec_bwd", (nc // R,),
        in_specs=[tok(DNW)] * 4 + [tok(AW), cd_spec, st_spec, tok(DNW)],
        out_specs=[tok(DNW)] * 4 + [tok(AW), cd_spec],
        out_shape=[_sds((B, S, DNW), F32)] * 4 + [_sds((B, S, AW), F32), _sds((B, nc, 1, DN_D), F32)],
        scratch_shapes=[pltpu.VMEM((B * DN_H, DN_D, DN_D), F32)],
        semantics=("arbitrary",), ins=(u, w, qd, kd, a, cd, states, d_o), ex=ex)


def _dn_intra_bwd(qkv, bg, tinv, du, dw, dqd, dkd, da, dcd, ex=None):
    B, S, _ = qkv.shape
    nc = S // CHUNK
    G = min(DN_GROUP, nc)
    rows = G * CHUNK

    def body(qkv_ref, bg_ref, t_ref, du_ref, dw_ref, dqd_ref, dkd_ref, da_ref, dcd_ref, dqkv_ref, dbg_ref):
        q, k, v = _dn_load_stack(qkv_ref, G)
        known = _stack_heads(t_ref, G, 0, CHUNK)
        _, vjp = jax.vjp(lambda q, k, v, bg: _dn_intra(q, k, v, bg, known)[:6], q, k, v,
                         bg_ref[...].reshape(G, CHUNK, DN_D))
        lane_row = _iota((1, DN_D), 1)
        dcd = jnp.stack([jnp.sum(jnp.where(lane_row == h, dcd_ref[g], 0.0), axis=-1, keepdims=True)
                         for g in range(G) for h in range(DN_H)])
        dq, dk, dv, dbg = vjp((_stack_heads(du_ref, G, 0, DN_D), _stack_heads(dw_ref, G, 0, DN_D),
                               _stack_heads(dqd_ref, G, 0, DN_D), _stack_heads(dkd_ref, G, 0, DN_D),
                               _stack_heads(da_ref, G, 0, CHUNK), dcd))
        for g in range(G):
            rows = slice(g * CHUNK, (g + 1) * CHUNK)
            for h in range(DN_H):
                n = g * DN_H + h
                dqkv_ref[rows, h * DN_D:(h + 1) * DN_D] = dq[n]
                dqkv_ref[rows, DNW + h * DN_D:DNW + (h + 1) * DN_D] = dk[n]
                dqkv_ref[rows, 2 * DNW + h * DN_D:2 * DNW + (h + 1) * DN_D] = dv[n]
        dbg_ref[...] = dbg.reshape(G * CHUNK, DN_D)

    return _hosted_call(
        body, "dn_intra_bwd", (B, nc // G),
        in_specs=[_tok(rows, CONVW), _tok(rows, 128), _tok(rows, AW)] + [_tok(rows, DNW)] * 4 + [_tok(rows, AW), _cd_spec(G)],
        out_specs=[_tok(rows, CONVW), _tok(rows, 128)],
        out_shape=[_sds((B, S, CONVW), F32), _sds((B, S, 128), F32)],
        scratch_shapes=[], semantics=("parallel", "parallel"), ins=(qkv, bg, tinv, du, dw, dqd, dkd, da, dcd), ex=ex)


def _dn_prep_bwd(dn, y, ba, conv_w, alog, dtb, dqkv, dbg, bt, ex=None):
    B, S, _ = dn.shape
    nt = S // bt
    strip_rows = min(bt, 64)

    def rev(f):
        return pl.BlockSpec((None, bt, f), lambda b, i: (b, nt - 1 - i, 0))

    def body(x_ref, y_ref, ba_ref, cw_ref, al_ref, dt_ref, dqkv_ref, dbg_ref,
             dx_ref, dba_ref, dcw_ref, dal_ref, ddt_ref, dye_ref):
        i = pl.program_id(1)
        @pl.when(i == 0)
        def _():
            dye_ref[bt:bt + 8, :] = jnp.zeros((8, CONVW), F32)

        @pl.when(i > 0)
        def _():
            dye_ref[bt:bt + 8, :] = dye_ref[0:8, :]

        n_strips = bt // strip_rows

        def strip(k, carry):
            dal, ddt, dcw = carry
            r0 = pl.multiple_of((n_strips - 1 - k) * strip_rows, strip_rows)
            rows = pl.ds(r0, strip_rows)
            dcw_slabs = []
            for j in range(3 * DN_H):
                cols = slice(j * DN_D, (j + 1) * DN_D)
                _, vjp = jax.vjp(functools.partial(_dn_act, normalize=j < 2 * DN_H), y_ref[rows, cols].astype(F32))
                (dye_ref[rows, cols],) = vjp(dqkv_ref[rows, cols])
                window = dye_ref[pl.ds(r0, strip_rows + 8), cols]
                shifted = [window[3 - t:strip_rows + 3 - t] for t in range(CONV)]
                dx = cw_ref[0:1, cols] * shifted[0]
                for t in range(1, CONV):
                    dx = dx + cw_ref[t:t + 1, cols] * shifted[t]
                dx_ref[rows, cols] = dx.astype(BF)
                x = x_ref[rows, cols].astype(F32)
                dcw_slabs.append(jnp.concatenate([jnp.sum(shifted[t] * x, axis=0, keepdims=True) for t in range(CONV)], axis=0))
            _, vjp = jax.vjp(_dn_gates, ba_ref[rows, :], al_ref[...], dt_ref[...])
            dba_ref[rows, :], da, dd = vjp(dbg_ref[rows, :])
            return dal + da, ddt + dd, dcw + jnp.concatenate(dcw_slabs, axis=1)
        zero = jnp.zeros((1, DN_D), F32)
        dal, ddt, dcw = lax.fori_loop(0, n_strips, strip, (zero, zero, jnp.zeros((CONV, CONVW), F32)))
        first = (i == 0) & (pl.program_id(0) == 0)
        _acc(dcw_ref, dcw, first)
        _acc(dal_ref, dal, first)
        _acc(ddt_ref, ddt, first)

    return _hosted_call(
        body, "dn_prep_bwd", (B, nt),
        in_specs=[rev(CONVW), rev(CONVW), rev(128), _full((CONV, CONVW)), _full((1, 128)), _full((1, 128)), rev(CONVW), rev(128)],
        out_specs=[rev(CONVW), rev(128), _full((CONV, CONVW)), _full((1, 128)), _full((1, 128))],
        out_shape=[_sds((B, S, CONVW), BF), _sds((B, S, 128), F32), _sds((CONV, CONVW), F32), _sds((1, 128), F32),
                   _sds((1, 128), F32)],
        scratch_shapes=[pltpu.VMEM((bt + 8, CONVW), F32)],
        semantics=("arbitrary", "arbitrary"), ins=(dn, y, ba, conv_w, alog, dtb, dqkv, dbg), ex=ex)


def _attn_bwd(qkv, sinks, d_o, ex=None):
    B, S, _ = qkv.shape
    nb = S // BLK

    def cur(f):
        return pl.BlockSpec((None, BLK, f), lambda b, i: (b, jnp.minimum(i, nb - 1), 0))

    def out_prev(f):
        return pl.BlockSpec((None, BLK, f), lambda b, i: (b, jnp.maximum(i - 1, 0), 0))

    def body(qkv_ref, kvp_ref, sk_ref, do_ref, dq_ref, dkv_ref, dsk_ref, carry_ref):
        n = pl.program_id(1)
        first = (n == 0) & (pl.program_id(0) == 0)

        @pl.when(n == 0)
        def _():
            carry_ref[...] = jnp.zeros_like(carry_ref)

        @pl.when(n < nb)
        def _():
            qs, kc, kp, vc, vp = _attn_load(qkv_ref, kvp_ref)

            def f(qs, kc, kp, vc, vp, sk):
                return _attn_block(qs, kc, kp, vc, vp, sk, n > 0)
            _, vjp = jax.vjp(f, qs, kc, kp, vc, vp, sk_ref[...])
            d_outs = [do_ref[:, h * HD:(h + 1) * HD] for h in range(HEADS)]
            dqs, dkc, dkp, dvc, dvp, dsk = vjp(d_outs)
            for h in range(HEADS):
                dq_ref[:, h * HD:(h + 1) * HD] = dqs[h].astype(BF)
            for h in range(KV_HEADS):
                ksl = slice(h * HD, (h + 1) * HD)
                vsl = slice(KVW + h * HD, KVW + (h + 1) * HD)
                dkv_ref[:, ksl] = (carry_ref[:, ksl] + dkp[h]).astype(BF)
                dkv_ref[:, vsl] = (carry_ref[:, vsl] + dvp[h]).astype(BF)
                carry_ref[:, ksl] = dkc[h]
                carry_ref[:, vsl] = dvc[h]
            _acc(dsk_ref, dsk, first)

        @pl.when(n == nb)
        def _():
            dkv_ref[...] = carry_ref[...].astype(BF)

    return _hosted_call(
        body, "attn_bwd", (B, nb + 1),
        in_specs=[cur(QKV), _kv_prev_spec(lambda i: jnp.maximum(jnp.minimum(i, nb - 1) - 1, 0)), _full((1, HEADS)), cur(QW)],
        out_specs=[cur(QW), out_prev(2 * KVW), _full((1, HEADS))],
        out_shape=[_sds((B, S, QW), BF), _sds((B, S, 2 * KVW), BF), _sds((1, HEADS), F32)],
        scratch_shapes=[pltpu.VMEM((BLK, 2 * KVW), F32)],
        semantics=("arbitrary", "arbitrary"), ins=(qkv, qkv, sinks, d_o), ex=ex)


def _in_proj_bwd(x, mod, norm1_g, w_in, pieces, dba, dx1, bt):
    B, S, _ = x.shape
    widths = (QW, 2 * KVW, CONVW, DNW, D, D)

    def body(x_ref, mod_ref, g_ref, w_ref, dq_ref, dkv_ref, ddn_ref, dz_ref, dga_ref, dgb_ref, dba_ref, dx1_ref,
             gx_ref, dp_ref, dmod_ref, dg_ref):
        dp = jnp.concatenate([r[...] for r in (dq_ref, dkv_ref, ddn_ref, dz_ref, dga_ref, dgb_ref)]
                             + [dba_ref[...].astype(BF)], axis=-1)
        dp_ref[...] = dp
        dh = lax.dot_general(dp, w_ref[...], (((1,), (1,)), ((), ())), preferred_element_type=F32)
        _, vjp = jax.vjp(_norm_mod, x_ref[...], g_ref[...], mod_ref[:, 0:D], mod_ref[:, D:2 * D])
        dx, dgain, dshift, dscale = vjp(dh)
        gx_ref[...] = dx + dx1_ref[...]
        first = pl.program_id(1) == 0
        _acc(dmod_ref, jnp.concatenate([dshift, dscale], axis=-1), first)
        _acc(dg_ref, dgain, first)

    return pl.pallas_call(
        body, name="in_proj_bwd", grid=(B, S // bt),
        in_specs=[_tok(bt, D), _per_batch(6 * D), _full((1, D)), _resident((D, IN_PAD))] + [_tok(bt, w) for w in widths]
        + [_tok(bt, 128), _tok(bt, D)],
        out_specs=[_tok(bt, D), _tok(bt, IN_PAD), _per_batch(2 * D), _per_batch(D)],
        out_shape=[_sds((B, S, D), F32), _sds((B, S, IN_PAD), BF), _sds((B, 1, 2 * D), F32), _sds((B, 1, D), F32)],
        compiler_params=_cparams(dimension_semantics=("parallel", "arbitrary")),
    )(x, mod, norm1_g, w_in, *pieces, dba, dx1)


def _matmul_tn(tag, a, b, bk, bn, bt, col_blocks=False, ex=None):
    T, K = a.shape
    N = b.shape[1]
    nt = T // bt
    if col_blocks:
        assert bk == K
        out_spec = pl.BlockSpec((None, bk, bn), lambda i, j, t: (j, 0, 0))
        out_shape = _sds((N // bn, K, bn), F32)
    else:
        out_spec = pl.BlockSpec((bk, bn), lambda i, j, t: (i, j))
        out_shape = _sds((K, N), F32)

    def body(a_ref, b_ref, o_ref, acc_ref):
        t = pl.program_id(2)

        @pl.when(t == 0)
        def _():
            acc_ref[...] = jnp.zeros_like(acc_ref)
        acc_ref[...] += lax.dot_general(a_ref[...], b_ref[...], (((0,), (0,)), ((), ())), preferred_element_type=F32)

        @pl.when(t == nt - 1)
        def _():
            o_ref[...] = acc_ref[...]

    (out,), landed = _hosted_call(
        body, f"grad_{tag}", (K // bk, N // bn, nt),
        in_specs=[pl.BlockSpec((bt, bk), lambda i, j, t: (t, i)), pl.BlockSpec((bt, bn), lambda i, j, t: (t, j))],
        out_specs=[out_spec], out_shape=[out_shape],
        scratch_shapes=[pltpu.VMEM((bk, bn), F32)],
        semantics=("parallel", "parallel", "arbitrary"), ins=(a, b), ex=ex)
    return out if ex is None else (out, landed)


def _rope_table(positions):
    inv_freq = THETA ** (-jnp.arange(0, ROT, 2, dtype=F32) / ROT)
    rest = jnp.zeros((HD - ROT,), F32)
    freq = jnp.concatenate([inv_freq, inv_freq, rest] * 2)
    sign = jnp.concatenate([-jnp.ones_like(inv_freq), jnp.ones_like(inv_freq), rest] * 2)
    ang = positions.astype(F32)[..., None] * freq
    return jnp.cos(ang), jnp.sin(ang) * sign


def _lane_pad(v, offset, width=128):
    return jnp.zeros((1, width), F32).at[0, offset:offset + v.shape[-1]].set(v.reshape(-1))


def _tile(S, want):
    return min(S, want)


class _Hosted:
    def __init__(self, call):
        self.call = call
        self.outs = None

    def __call__(self, ex):
        self.outs, landed = self.call(ex)
        return landed


def _local_step(x, mod, positions, tgt, norm1_g, w_in_pad, conv_w, q_norm_g, k_norm_g, sinks, a_log, dt_bias,
                dn_norm_g, w_branch, w_out, norm2_g, w_gu, w_dn, dist=None):
    B, S, _ = x.shape
    T = B * S
    cos_t, sin_t = _rope_table(positions)
    qg2 = jnp.concatenate([q_norm_g, q_norm_g], axis=-1)
    kg2 = jnp.concatenate([k_norm_g, k_norm_g], axis=-1)
    alog = _lane_pad(a_log, DN_H)
    dtb = _lane_pad(dt_bias, DN_H)
    conv2 = conv_w.reshape(CONV, CONVW)
    bt = _tile(S, 512)
    bt_mlp = _tile(S, 256)

    q, kv, dn, z, ga, gb, ba, h1 = _in_proj(x, mod, norm1_g, w_in_pad, bt)
    qkv_n = _qk_prep_fwd(q, kv, cos_t, sin_t, qg2, kg2, bt)
    o_attn = _attn_fwd(qkv_n, sinks)
    dqkv, bg, dn_y = _dn_prep(dn, ba, conv2, alog, dtb, bt)
    intra = _Hosted(lambda ex: _dn_intra_fwd(dqkv, bg, ex))
    if dist is None:
        intra(None)
    else:
        f_br, f_out, w_gu, f_dn = _gather_weights("late", [w_branch, w_out, w_gu, w_dn], dist[0], host=intra)
        w_branch, w_out, w_dn = (f.reshape(N_CHIP * f.shape[1], f.shape[2]) for f in (f_br, f_out, f_dn))
    dn_u, dn_w, dn_qd, dn_kd, dn_a, dn_cd, dn_tinv = intra.outs
    (o_raw, states), _ = _dn_rec_fwd(dn_u, dn_w, dn_qd, dn_kd, dn_a, dn_cd)
    x1, o_dn, merged = _mix_fwd(o_attn, o_raw, z, ga, gb, x, mod, dn_norm_g, w_branch, w_out, bt)
    dx1, h2, act, dgu, dyy, loss, dmod2, dnorm2 = _mlp(x1, tgt, mod, norm2_g, w_gu, w_dn, bt_mlp)

    def flat(t):
        return t.reshape(T, t.shape[-1])
    tn = functools.partial(_matmul_tn, bt=_tile(T, 2048))
    g_w_dn = tn("w_down", flat(act), flat(dyy), bk=FFN, bn=D // 2)
    g_w_gu = tn("w_gate_up", flat(h2), flat(dgu), bk=D, bn=2 * FFN // N_CHIP, col_blocks=True)

    mix_b = _Hosted(lambda ex: _mix_bwd(o_attn, o_raw, z, ga, gb, x, mod, dn_norm_g, w_branch, w_out, dx1, bt_mlp, ex))
    rec_b = _Hosted(lambda ex: _dn_rec_bwd(dn_u, dn_w, dn_qd, dn_kd, dn_a, dn_cd, states, mix_b.outs[1], ex))
    intra_b = _Hosted(lambda ex: _dn_intra_bwd(dqkv, bg, dn_tinv, *rec_b.outs, ex))
    if dist is None:
        for host in (mix_b, rec_b, intra_b):
            host(None)
    else:
        g_w_gu, g_w_dn = _reduce_grads(("w_gate_up", "w_down"), [g_w_gu, g_w_dn.reshape(N_CHIP, -1, D)], *dist,
                                       hosts=[mix_b, rec_b, intra_b])
    d_oa, _, dz, dga, dgb, dya, dyd, dout, dgate1, ddn_g = mix_b.outs
    d_dqkv, dbg = intra_b.outs
    g_w_out = tn("w_out", flat(merged), flat(dout), bk=D, bn=D)
    g_w_br = jnp.concatenate([tn("w_branch_attn", flat(o_attn), flat(dya), bk=QW, bn=D),
                              tn("w_branch_dn", flat(o_dn), flat(dyd), bk=DNW, bn=D)], axis=0)
    prep_b = _Hosted(lambda ex: _dn_prep_bwd(dn, dn_y, ba, conv2, alog, dtb, d_dqkv, dbg, bt, ex))
    attn_b = _Hosted(lambda ex: _attn_bwd(qkv_n, sinks, d_oa, ex))
    qk_b = _Hosted(lambda ex: _qk_prep_bwd(q, kv, cos_t, sin_t, qg2, kg2, *attn_b.outs[:2], bt, ex))
    if dist is None:
        for host in (prep_b, attn_b, qk_b):
            host(None)
    else:
        g_w_br, g_w_out = _reduce_grads(("w_branch", "w_out"), [g_w_br.reshape(N_CHIP, -1, D), g_w_out.reshape(N_CHIP, -1, D)],
                                        *dist, hosts=[prep_b, attn_b, qk_b])
    d_dn, dba, dconv, dalog, ddtb = prep_b.outs
    dsk = attn_b.outs[2]
    dq, dkv, dqg2, dkg2 = qk_b.outs
    dqg = dqg2[:, :HD] + dqg2[:, HD:]
    dkg = dkg2[:, :HD] + dkg2[:, HD:]
    grad_x, dproj, dmod1, dnorm1 = _in_proj_bwd(x, mod, norm1_g, w_in_pad, (dq, dkv, d_dn, dz, dga, dgb), dba, dx1, bt)
    g_w_in = _Hosted(lambda ex: (tn("w_in", flat(h1), flat(dproj), bk=D, bn=IN_PAD // 3), ()) if ex is None
                     else tn("w_in", flat(h1), flat(dproj), bk=D, bn=IN_PAD // 3, ex=ex))
    if dist is None:
        g_w_in(None)
        g_w_in = g_w_in.outs

    dmod = jnp.concatenate([dmod1, dgate1, dmod2], axis=-1)
    small = dict(norm1_g=jnp.sum(dnorm1, axis=0), norm2_g=jnp.sum(dnorm2, axis=0), q_norm_g=dqg, k_norm_g=dkg,
                 sinks=dsk, a_log=dalog[:, DN_H:2 * DN_H], dt_bias=ddtb[:, DN_H:2 * DN_H],
                 dn_norm_g=jnp.sum(ddn_g, axis=0), conv_w=dconv)
    return jnp.sum(loss), grad_x, dmod, small, (g_w_in, g_w_br, g_w_out, g_w_gu, g_w_dn)


def _flip(me, f):
    return (me[0] ^ ((f >> 2) & 1), me[1] ^ ((f >> 1) & 1), me[2] ^ (f & 1))


def _comm_call(name, ex):
    n_in, n_out = len(ex.ins), len(ex.out_shapes)

    def body(*refs):
        out_refs, sems = refs[n_in:n_in + n_out], refs[n_in + n_out:]
        cps = _exchange_copies(ex, refs[:n_in], out_refs, sems[0], sems[1])
        for cp in cps:
            cp.start()
        for cp in cps:
            cp.wait_recv()
        if ex.n_forward:
            fwd = _forward_copies(ex, out_refs, sems[2], sems[3])
            for cp in fwd:
                cp.start()
            for cp in fwd:
                cp.wait_recv()
            cps = cps + fwd
        for cp in cps:
            cp.wait_send()

    any_spec = pl.BlockSpec(memory_space=pl.ANY)
    return pl.pallas_call(
        body, name=name, in_specs=[any_spec] * n_in, out_specs=[any_spec] * n_out, out_shape=list(ex.out_shapes),
        scratch_shapes=_exchange_sems(ex),
    )(*ex.ins)


def _by_origin(own, received, index):
    stack = jnp.concatenate([own[None], received], axis=0)
    n = stack.shape[0]
    return jnp.stack([lax.dynamic_index_in_dim(stack, k ^ index, 0, keepdims=False) for k in range(n)])


def _gather_devices(name, arrs, dev, host=None):
    def plan(me, in_refs, out_refs):
        return [(a, o.at[f - 1], _flip(me, f)) for a, o in zip(in_refs, out_refs) for f in range(1, N_DEV)]
    outs = tuple(_sds((N_DEV - 1,) + a.shape, a.dtype) for a in arrs)
    got = (host or functools.partial(_comm_call, name))(_Exchange(tuple(arrs), outs, (N_DEV - 1) * len(arrs), plan))
    return [_by_origin(a, g, dev) for a, g in zip(arrs, got)]


def _gather_chips(name, arrs, chip):
    def plan(me, in_refs, out_refs):
        return [(a, o.at[j], _flip(me, 2 * (j + 1))) for a, o in zip(in_refs, out_refs) for j in range(N_CHIP - 1)]
    outs = tuple(_sds((N_CHIP - 1,) + a.shape, a.dtype) for a in arrs)
    got = _comm_call(name, _Exchange(tuple(arrs), outs, (N_CHIP - 1) * len(arrs), plan))
    return [_by_origin(a, g, chip) for a, g in zip(arrs, got)]


def _swap_cores_ex(arrs):
    def plan(me, in_refs, out_refs):
        return [(g, o, _flip(me, 1)) for g, o in zip(in_refs, out_refs)]
    return _Exchange(tuple(arrs), tuple(_sds(g.shape, g.dtype) for g in arrs), len(arrs), plan)


def _gather_weights(tag, shards, chip, host=None):
    def plan(me, in_refs, out_refs):
        chip_me = 2 * me[0] + me[1]
        remote = []
        for a, o in zip(in_refs, out_refs):
            half = a.shape[0] // 2
            mine = a.at[pl.ds(me[2] * half, half)]
            remote += [(mine, o.at[chip_me, me[2]], _flip(me, 2 * (j + 1))) for j in range(N_CHIP - 1)]
        return remote

    def forward(me, out_refs):
        chip_me = 2 * me[0] + me[1]
        return [(o.at[chip_me ^ (j + 1), me[2]], o.at[chip_me ^ (j + 1), me[2]], _flip(me, 1))
                for o in out_refs for j in range(N_CHIP - 1)]
    run = host or functools.partial(_comm_call, f"weights_{tag}")
    n = (N_CHIP - 1) * len(shards)
    landed = run(_Exchange(tuple(shards), tuple(_sds((N_CHIP, 2, a.shape[0] // 2, a.shape[1]), a.dtype) for a in shards),
                           n, plan, n, forward))
    return [lax.dynamic_update_slice(f.reshape((N_CHIP,) + a.shape), a[None], (chip, 0, 0)) for a, f in zip(shards, landed)]


def _rows(r):
    for br in (512, 352, 256, 128, 64, 32, 16, 8):
        if r % br == 0:
            return br
    raise ValueError(r)


def _pair_add(tag, g, recv, c):
    n, r, cols = g.shape
    half = r // 2
    br = _rows(half)
    nb = half // br

    def body(c_ref, g_ref, r_ref, o_ref):
        o_ref[...] = (g_ref[...] + r_ref[...]).astype(BF)

    return pl.pallas_call(
        body, name=f"pair_add_{tag}",
        grid_spec=pltpu.PrefetchScalarGridSpec(
            num_scalar_prefetch=1, grid=(n, nb),
            in_specs=[pl.BlockSpec((None, br, cols), lambda k, i, c_ref: (k, c_ref[0] * nb + i, 0)),
                      pl.BlockSpec((None, br, cols), lambda k, i, c_ref: (k, i, 0))],
            out_specs=pl.BlockSpec((None, br, cols), lambda k, i, c_ref: (k, i, 0))),
        out_shape=_sds((n, half, cols), BF),
        compiler_params=_cparams(dimension_semantics=("parallel", "parallel")),
    )(c, g, recv)


def _sum_chips(tag, p, q, chip):
    n, r, cols = q.shape
    br = _rows(r)

    def body(chip_ref, p_ref, q_ref, o_ref):
        acc = p_ref[...].astype(F32)
        for k in range(n):
            acc = acc + q_ref[k].astype(F32)
        o_ref[...] = acc

    return pl.pallas_call(
        body, name=f"sum_chips_{tag}",
        grid_spec=pltpu.PrefetchScalarGridSpec(
            num_scalar_prefetch=1, grid=(r // br,),
            in_specs=[pl.BlockSpec((None, br, cols), lambda i, chip_ref: (chip_ref[0], i, 0)),
                      pl.BlockSpec((n, br, cols), lambda i, chip_ref: (0, i, 0))],
            out_specs=pl.BlockSpec((br, cols), lambda i, chip_ref: (i, 0))),
        out_shape=_sds((r, cols), F32),
        compiler_params=_cparams(dimension_semantics=("parallel",)),
    )(chip, p, q)


def _reduce_grads(tags, grads, chip, core, hosts=None):
    core_arr = core.reshape(1).astype(jnp.int32)
    chip_arr = chip.reshape(1).astype(jnp.int32)
    name = "_".join(tags)
    run = hosts or [functools.partial(_comm_call, f"grads_{stage}_{name}") for stage in ("pair", "chips", "swap")]

    def plan_pair(me, in_refs, out_refs):
        remote = []
        for g, o in zip(in_refs, out_refs):
            half = g.shape[1] // 2
            remote += [(g.at[k, pl.ds((1 - me[2]) * half, half)], o.at[k], _flip(me, 1)) for k in range(N_CHIP)]
        return remote
    recv = run[0](_Exchange(tuple(grads), tuple(_sds((N_CHIP, g.shape[1] // 2, g.shape[2]), F32) for g in grads),
                            N_CHIP * len(grads), plan_pair))
    pair = [_pair_add(t, g, r, core_arr) for t, g, r in zip(tags, grads, recv)]

    def plan_chips(me, in_refs, out_refs):
        remote = []
        for p, o in zip(in_refs, out_refs):
            for j in range(N_CHIP - 1):
                peer = _flip(me, 2 * (j + 1))
                remote.append((p.at[2 * peer[0] + peer[1]], o.at[j], peer))
        return remote
    parts = run[1](_Exchange(tuple(pair), tuple(_sds((N_CHIP - 1,) + p.shape[1:], BF) for p in pair),
                             (N_CHIP - 1) * len(pair), plan_chips))
    mine = [_sum_chips(t, p, q, chip_arr) for t, p, q in zip(tags, pair, parts)]
    other = run[2](_swap_cores_ex(mine))
    return list(zip(mine, other))


def _adamw_math(w, g, m, v):
    m = ADAM_B1 * m + (1.0 - ADAM_B1) * g
    v = ADAM_B2 * v + (1.0 - ADAM_B2) * (g * g)
    m_hat = m / (1.0 - ADAM_B1 ** ADAM_STEP)
    v_hat = v / (1.0 - ADAM_B2 ** ADAM_STEP)
    delta = -ADAM_LR * (m_hat / (jnp.sqrt(v_hat) + ADAM_EPS) + ADAM_WD * w)
    return delta, m, v


def _adamw(name, w, g, m, v, ex=None):
    r, cols = w.shape
    br = _rows(r)
    if br * cols * 4 > (1 << 20) and br % 16 == 0:
        br //= 2

    def body(w_ref, g_ref, m_ref, v_ref, d_ref, mo_ref, vo_ref):
        d_ref[...], mo_ref[...], vo_ref[...] = _adamw_math(w_ref[...], g_ref[...], m_ref[...], v_ref[...])

    spec = pl.BlockSpec((br, cols), lambda i: (i, 0))
    outs, landed = _hosted_call(
        body, f"adamw_{name}", (r // br,), in_specs=[spec] * 4, out_specs=[spec] * 3,
        out_shape=[_sds((r, cols), F32)] * 3, scratch_shapes=[], semantics=("parallel",), ins=(w, g, m, v), ex=ex)
    return outs if ex is None else (outs, landed)


def _adamw_halves(name, w, mine, other, m, v, core, ex=None):
    r, cols = w.shape
    br = _rows(r // 2)
    if br * cols * 4 > (1 << 20) and br % 16 == 0:
        br //= 2
    per_half = r // 2 // br

    def body(core_ref, w_ref, a_ref, b_ref, m_ref, v_ref, g_ref, d_ref, mo_ref, vo_ref):
        g = jnp.where(pl.program_id(0) // per_half == core_ref[0], a_ref[...], b_ref[...])
        g_ref[...] = g
        d_ref[...], mo_ref[...], vo_ref[...] = _adamw_math(w_ref[...], g, m_ref[...], v_ref[...])

    whole = pl.BlockSpec((br, cols), lambda i: (i, 0))
    half = pl.BlockSpec((br, cols), lambda i: (i % per_half, 0))
    outs, landed = _hosted_call(
        body, f"adamw_{name}", (r // br,),
        in_specs=[pl.BlockSpec(memory_space=pltpu.SMEM), whole, half, half, whole, whole], out_specs=[whole] * 4,
        out_shape=[_sds((r, cols), F32)] * 4, scratch_shapes=[], semantics=("parallel",),
        ins=(core.reshape(1).astype(jnp.int32), w, mine, other, m, v), ex=ex)
    return outs if ex is None else (outs, landed)


def _ada_fwd(c_all, ada_w, ada_b_cols):
    n = c_all.shape[0]

    def body(c_ref, w_ref, b_ref, o_ref):
        o_ref[...] = _mmx(_silu(c_ref[...]), w_ref[...]) + b_ref[...]

    return pl.pallas_call(
        body, name="ada_fwd", out_shape=_sds((n, ada_w.shape[1]), F32), compiler_params=_cparams(),
    )(c_all, ada_w, ada_b_cols)


def _ada_bwd(c_all, dmod_cols, w, m, v, ex=None):
    n = c_all.shape[0]
    r, cols = w.shape
    br = 128

    def body(c_ref, d_ref, w_ref, m_ref, v_ref, g_ref, dl_ref, mo_ref, vo_ref):
        cond = _silu(c_ref[...])
        g = lax.dot_general(cond, d_ref[...], (((0,), (0,)), ((), ())), precision=lax.Precision.HIGHEST,
                            preferred_element_type=F32)
        g_ref[...] = g
        dl_ref[...], mo_ref[...], vo_ref[...] = _adamw_math(w_ref[...], g, m_ref[...], v_ref[...])

    spec = pl.BlockSpec((br, cols), lambda i: (i, 0))
    outs, landed = _hosted_call(
        body, "ada_bwd", (r // br,),
        in_specs=[pl.BlockSpec((n, br), lambda i: (0, i)), pl.BlockSpec((n, cols), lambda i: (0, 0)), spec, spec, spec],
        out_specs=[spec] * 4, out_shape=[_sds((r, cols), F32)] * 4, scratch_shapes=[], semantics=("parallel",),
        ins=(c_all, dmod_cols, w, m, v), ex=ex)
    return outs if ex is None else (outs, landed)


def _sum_devices(parts):
    n, r, cols = parts.shape

    def body(p_ref, o_ref):
        acc = p_ref[0]
        for k in range(1, n):
            acc = acc + p_ref[k]
        o_ref[...] = acc

    return pl.pallas_call(body, name="sum_devices", out_shape=_sds((r, cols), F32), compiler_params=_cparams())(parts)


SMALL_ROWS = 16
_SMALL_SLOTS = dict(norm1_g=(0, 0, D), norm2_g=(1, 0, D), q_norm_g=(2, 0, HD), k_norm_g=(2, 128, HD), sinks=(2, 256, HEADS),
                    a_log=(2, 384, DN_H), dt_bias=(2, 512, DN_H), dn_norm_g=(2, 640, DN_D))
_CONV_ROW = 4
_ADA_B_ROW = 8


def _pack_small(vals, conv, ada_b):
    def row(pieces):
        out, at = [], 0
        for col, val in pieces:
            out += [jnp.zeros((1, col - at), F32), val.reshape(1, -1)]
            at = col + val.size
        return jnp.concatenate(out + [jnp.zeros((1, CONVW - at), F32)], axis=1)
    rows = {}
    for name, (r, col, n) in _SMALL_SLOTS.items():
        rows.setdefault(r, []).append((col, vals[name]))
    blank = jnp.zeros((1, CONVW), F32)
    top = [row(sorted(rows[r], key=lambda p: p[0])) if r in rows else blank for r in range(_CONV_ROW)]
    conv_rows = jnp.concatenate([conv, jnp.zeros((CONV, CONVW - conv.shape[1]), F32)], axis=1)
    tail = jnp.zeros((SMALL_ROWS - _ADA_B_ROW - 4, CONVW), F32)
    return jnp.concatenate(top + [conv_rows, ada_b.reshape(4, CONVW), tail], axis=0)


def _unpack_small(sheet, conv_cols):
    out = {name: sheet[row, col:col + n].reshape(1, n) for name, (row, col, n) in _SMALL_SLOTS.items()}
    out["conv_w"] = sheet[_CONV_ROW:_CONV_ROW + CONV, 0:conv_cols].reshape(1, CONV, 1, conv_cols)
    out["ada_b"] = sheet[_ADA_B_ROW:_ADA_B_ROW + 4, :].reshape(1, 6 * D)
    return out


def _w_in_segments():
    shard = IN_WIDTH // N_CHIP
    cuts = sorted({0, IN_WIDTH, C_Z, C_Z + 2 * DN_H} | {k * shard for k in range(1, N_CHIP)})
    segs = []
    for a, b in zip(cuts[:-1], cuts[1:]):
        k = a // shard
        pad = a if a < C_Z else (C_BA + a - C_Z if a < C_Z + 2 * DN_H else a - 2 * DN_H)
        segs.append((k, a - k * shard, b - k * shard, pad))
    return segs


def _pad_w_in(f):
    parts = [f[k][:, lo:hi] for k, lo, hi, _ in sorted(_w_in_segments(), key=lambda s: s[3])]
    return jnp.concatenate(parts + [jnp.zeros((f.shape[1], IN_PAD - IN_WIDTH), f.dtype)], axis=1)


def _unpad_w_in(g):
    return jnp.stack([jnp.concatenate([g[:, pad:pad + hi - lo] for kk, lo, hi, pad in _w_in_segments() if kk == k], axis=1)
                      for k in range(N_CHIP)])


def _blocks_to_cols(f):
    return f.transpose(1, 0, 2).reshape(f.shape[1], N_CHIP * f.shape[2])


def kernel(x, c, positions, ada_w, ada_b, norm1_g, w_in, conv_w, q_norm_g, k_norm_g, sinks, a_log, dt_bias, dn_norm_g, w_branch, w_out, norm2_g, w_gate_up, w_down, loss_target, m_ada_w, m_ada_b, m_norm1_g, m_w_in, m_conv_w, m_q_norm_g, m_k_norm_g, m_sinks, m_a_log, m_dt_bias, m_dn_norm_g, m_w_branch, m_w_out, m_norm2_g, m_w_gate_up, m_w_down, v_ada_w, v_ada_b, v_norm1_g, v_w_in, v_conv_w, v_q_norm_g, v_k_norm_g, v_sinks, v_a_log, v_dt_bias, v_dn_norm_g, v_w_branch, v_w_out, v_norm2_g, v_w_gate_up, v_w_down):
    ix, iy, ic = lax.axis_index("x"), lax.axis_index("y"), lax.axis_index("c")
    dev = 4 * ix + 2 * iy + ic
    chip = 2 * ix + iy
    n_seq = x.shape[0]
    conv_cols = conv_w.shape[-1]

    c_all, conv_all = _gather_devices("gather_cond", [c, conv_w.reshape(CONV, conv_cols)], dev)
    c_all = c_all.reshape(N_DEV * n_seq, D)
    ada_cols = ada_w.shape[-1]
    ada_b_cols = lax.dynamic_slice(ada_b, (0, chip * ada_cols), (1, ada_cols))
    mod_cols = _ada_fwd(c_all, ada_w[0], ada_b_cols)
    (mod_blocks,) = _gather_chips("gather_mod", [mod_cols], chip)
    mod_all = _blocks_to_cols(mod_blocks)
    mod = lax.dynamic_slice(mod_all, (dev * n_seq, 0), (n_seq, 6 * D)).reshape(n_seq, 1, 6 * D)
    conv_full = _blocks_to_cols(conv_all[0::2])

    (f_in,) = _gather_weights("w_in", [w_in[0].astype(BF)], chip)
    w_in_pad = _pad_w_in(f_in)

    loss, grad_x, dmod, small, (w_in_grad, r_br, r_out, r_gu, r_dn) = _local_step(
        x, mod, positions, loss_target, norm1_g, w_in_pad, conv_full.reshape(CONV, 1, CONVW), q_norm_g, k_norm_g, sinks,
        a_log, dt_bias, dn_norm_g, w_branch[0].astype(BF), w_out[0].astype(BF), norm2_g, w_gate_up[0].astype(BF),
        w_down[0].astype(BF), dist=(chip, ic))
    loss = lax.psum(loss, ("x", "y", "c"))

    part = _pack_small(small, small["conv_w"], jnp.sum(dmod, axis=(0, 1)).reshape(1, 6 * D))
    dmod_all, parts = _gather_devices("gather_small", [dmod.reshape(n_seq, 6 * D), part], dev, host=w_in_grad)
    dmod_all = dmod_all.reshape(N_DEV * n_seq, 6 * D)
    dmod_cols = lax.dynamic_slice(dmod_all, (0, chip * ada_cols), (N_DEV * n_seq, ada_cols))

    up_gu = _Hosted(lambda ex: _adamw_halves("w_gate_up", w_gate_up[0], *r_gu, m_w_gate_up[0], v_w_gate_up[0], ic, ex))
    up_ada = _Hosted(lambda ex: _ada_bwd(c_all, dmod_cols, ada_w[0], m_ada_w[0], v_ada_w[0], ex))
    up_dn = _Hosted(lambda ex: _adamw_halves("w_down", w_down[0], *r_dn, m_w_down[0], v_w_down[0], ic, ex))
    (r_in,) = _reduce_grads(("w_in",), [_unpad_w_in(w_in_grad.outs)], chip, ic, hosts=[up_gu, up_ada, up_dn])
    ada = up_ada.outs
    big = {"w_gate_up": tuple(up_gu.outs), "w_down": tuple(up_dn.outs)}
    for name, w, g, m, v in (("w_in", w_in, r_in, m_w_in, v_w_in), ("w_branch", w_branch, r_br, m_w_branch, v_w_branch),
                             ("w_out", w_out, r_out, m_w_out, v_w_out)):
        big[name] = tuple(_adamw_halves(name, w[0], *g, m[0], v[0], ic))
    g_small = _unpack_small(_sum_devices(parts), CONVW)
    g_conv = lax.dynamic_slice(g_small["conv_w"].reshape(CONV, CONVW), (0, chip * conv_cols), (CONV, conv_cols))
    g_small["conv_w"] = g_conv.reshape(1, CONV, 1, conv_cols)

    given = dict(norm1_g=(norm1_g, m_norm1_g, v_norm1_g), norm2_g=(norm2_g, m_norm2_g, v_norm2_g),
                 q_norm_g=(q_norm_g, m_q_norm_g, v_q_norm_g), k_norm_g=(k_norm_g, m_k_norm_g, v_k_norm_g),
                 sinks=(sinks, m_sinks, v_sinks), a_log=(a_log, m_a_log, v_a_log), dt_bias=(dt_bias, m_dt_bias, v_dt_bias),
                 dn_norm_g=(dn_norm_g, m_dn_norm_g, v_dn_norm_g))
    sheets = [_pack_small({k: t[j] for k, t in given.items()}, cw.reshape(CONV, conv_cols), ab)
              for j, (cw, ab) in enumerate(((conv_w, ada_b), (m_conv_w, m_ada_b), (v_conv_w, v_ada_b)))]
    g_local = _pack_small(g_small, g_conv, g_small["ada_b"])
    upd = [_unpack_small(s, conv_cols) for s in _adamw("small", sheets[0], g_local, sheets[1], sheets[2])]

    names = ["ada_w", "ada_b", "norm1_g", "w_in", "conv_w", "q_norm_g", "k_norm_g", "sinks", "a_log", "dt_bias", "dn_norm_g",
             "w_branch", "w_out", "norm2_g", "w_gate_up", "w_down"]

    def leaf(name, j):
        if name == "ada_w":
            return ada[j][None]
        if name in big:
            return big[name][j][None]
        return g_small[name] if j == 0 else upd[j - 1][name]

    return (loss, grad_x) + tuple(leaf(n, j) for j in range(4) for n in names)
```

```python
import functools
from typing import Callable, NamedTuple

import jax
import jax.numpy as jnp
import numpy as np
from jax import lax
from jax.experimental import pallas as pl
from jax.experimental.pallas import tpu as pltpu

F32 = jnp.float32
BF = jnp.bfloat16

D = 1024
HEADS = 8
KV_HEADS = 2
GROUP = 4
HD = 64
BLK = 128
ROT = 16
THETA = 500000.0
QW = 512
KVW = 128
DN_H = 4
DN_D = 128
CONV = 4
CHUNK = 64
DNW = 512
CONVW = 1536
FFN = 2816
EPS = 1e-6
IN_WIDTH = 4872
IN_PAD = 4992
C_KV = 512
C_DN = 768
C_Z = 2304
C_GA = 2816
C_GB = 3840
C_BA = 4864
NEG = -1e30
N_DEV = 8
N_CHIP = 4

ADAM_LR = 0.001
ADAM_B1 = 0.9
ADAM_B2 = 0.999
ADAM_EPS = 1e-08
ADAM_WD = 0.01
ADAM_STEP = 10

VMEM_LIMIT = 60 * 1024 * 1024


def _cparams(**kw):
    return pltpu.CompilerParams(vmem_limit_bytes=VMEM_LIMIT, **kw)


def _dg(a, b, ca, cb):
    return lax.dot_general(a.astype(BF), b.astype(BF), (((ca,), (cb,)), ((), ())),
                           preferred_element_type=F32)


@jax.custom_vjp
def _mm(a, b):
    return _dg(a, b, 1, 0)


def _mm_fwd(a, b):
    return _dg(a, b, 1, 0), (a, b)


def _mm_bwd(res, dy):
    a, b = res
    return _dg(dy, b, 1, 1).astype(a.dtype), _dg(a, dy, 0, 0).astype(b.dtype)


_mm.defvjp(_mm_fwd, _mm_bwd)


@jax.custom_vjp
def _mm_nt(a, b):
    return _dg(a, b, 1, 1)


def _mm_nt_fwd(a, b):
    return _dg(a, b, 1, 1), (a, b)


def _mm_nt_bwd(res, dy):
    a, b = res
    return _dg(dy, b, 1, 0).astype(a.dtype), _dg(dy, a, 0, 0).astype(b.dtype)


_mm_nt.defvjp(_mm_nt_fwd, _mm_nt_bwd)


@jax.custom_vjp
def _mm_tn(a, b):
    return _dg(a, b, 0, 0)


def _mm_tn_fwd(a, b):
    return _dg(a, b, 0, 0), (a, b)


def _mm_tn_bwd(res, dy):
    a, b = res
    return _dg(b, dy, 1, 1).astype(a.dtype), _dg(a, dy, 1, 0).astype(b.dtype)


_mm_tn.defvjp(_mm_tn_fwd, _mm_tn_bwd)


def _mmx(a, b):
    return jnp.dot(a, b, precision=lax.Precision.HIGHEST, preferred_element_type=F32)


def _mmx_nt(a, b):
    return lax.dot_general(a, b, (((1,), (1,)), ((), ())), precision=lax.Precision.HIGHEST,
                           preferred_element_type=F32)


def _iota(shape, dim):
    return lax.broadcasted_iota(jnp.int32, shape, dim)


def _sigmoid(x):
    return lax.logistic(x)


def _silu(x):
    return x * _sigmoid(x)


def _softplus(x):
    return jnp.maximum(x, 0.0) + jnp.log(1.0 + jnp.exp(-jnp.abs(x)))


def _rms(x, gain):
    return x * lax.rsqrt(jnp.mean(x * x, axis=-1, keepdims=True) + EPS) * gain


def _norm_mod(x, gain, shift, scale):
    return _rms(x, gain) * (1.0 + scale) + shift


def _split(a):
    hi = a.astype(BF)
    return hi, (a - hi.astype(F32)).astype(BF)


def _dg2(a, c, ca, cb):
    ah, al = _split(a)
    c = c.astype(BF)

    def dg(x):
        return lax.dot_general(x, c, (((ca,), (cb,)), ((), ())), preferred_element_type=F32)
    return dg(ah) + dg(al)


@jax.custom_vjp
def _mmc(a, c):
    return _dg2(a, c, 1, 0)


def _mmc_fwd(a, c):
    return _dg2(a, c, 1, 0), c


def _mmc_bwd(c, dy):
    return _dg2(dy, c, 1, 1), jnp.zeros_like(c)


_mmc.defvjp(_mmc_fwd, _mmc_bwd)


def _qk_prep(slabs, gain, cos, sin):
    r = _iota((2 * HD, 2 * HD), 0)
    c = _iota((2 * HD, 2 * HD), 1)
    seg = jnp.where(r // HD == c // HD, 1.0 / HD, 0.0).astype(F32)
    half = ROT // 2
    cd = c % HD
    pair = jnp.where(((cd < half) & (r == c + half)) | ((cd >= half) & (cd < ROT) & (r == c - half)), 1.0, 0.0).astype(F32)
    out = []
    for x in slabs:
        y = x * lax.rsqrt(_mmc(x * x, seg) + EPS) * gain
        out.append(y * cos + _mmc(y, pair) * sin)
    return out


def _attn_block(qs, kc, kp, vc, vp, sinks, has_prev):
    rows = GROUP * BLK
    qi = _iota((rows, 2 * BLK), 0) % BLK + BLK
    kj = _iota((rows, 2 * BLK), 1)
    dist = qi - kj
    valid = (dist >= 0) & (dist < BLK) & ((kj >= BLK) | has_prev)
    grp = _iota((rows, HEADS), 0) // BLK
    col = _iota((rows, HEADS), 1)

    outs = []
    for h in range(KV_HEADS):
        q = jnp.concatenate([qs[h * GROUP + g] for g in range(GROUP)], axis=0)
        k = jnp.concatenate([kp[h], kc[h]], axis=0)
        v = jnp.concatenate([vp[h], vc[h]], axis=0)
        s = _mm_nt(q, k) * (HD ** -0.5)
        s = jnp.where(valid, s, NEG)
        sink = jnp.sum(jnp.where(col == h * GROUP + grp, sinks, 0.0), axis=-1, keepdims=True)
        m = lax.stop_gradient(jnp.maximum(jnp.max(s, axis=-1, keepdims=True), sink))
        p = jnp.exp(s - m)
        probs = p / (jnp.sum(p, axis=-1, keepdims=True) + jnp.exp(sink - m))
        o = _mm(probs, v)
        outs += [o[g * BLK:(g + 1) * BLK] for g in range(GROUP)]
    return outs


def _dn_act(y, normalize):
    s = _silu(y)
    return s * lax.rsqrt(jnp.sum(s * s, axis=-1, keepdims=True) + EPS) if normalize else s


def _dn_gates(ba, alog, dtb):
    lane = _iota(ba.shape, 1)
    beta = _sigmoid(ba)
    g = -jnp.exp(alog) * _softplus(ba + dtb)
    return jnp.where(lane < DN_H, beta, jnp.where(lane < 2 * DN_H, g, 0.0))


def _bdg(a, b, ca, cb):
    return lax.dot_general(a.astype(BF), b.astype(BF), (((ca,), (cb,)), ((0,), (0,))), preferred_element_type=F32)


@jax.custom_vjp
def _bmm(a, b):
    return _bdg(a, b, 2, 1)


def _bmm_fwd(a, b):
    return _bdg(a, b, 2, 1), (a, b)


def _bmm_bwd(res, dy):
    a, b = res
    return _bdg(dy, b, 2, 2), _bdg(a, dy, 1, 1)


_bmm.defvjp(_bmm_fwd, _bmm_bwd)


@jax.custom_vjp
def _bmm_nt(a, b):
    return _bdg(a, b, 2, 2)


def _bmm_nt_fwd(a, b):
    return _bdg(a, b, 2, 2), (a, b)


def _bmm_nt_bwd(res, dy):
    a, b = res
    return _bdg(dy, b, 2, 1), _bdg(dy, a, 1, 1)


_bmm_nt.defvjp(_bmm_nt_fwd, _bmm_nt_bwd)


def _bmmx(a, b):
    return lax.dot_general(a, b, (((2,), (1,)), ((0,), (0,))), precision=lax.Precision.HIGHEST,
                           preferred_element_type=F32)


def _neumann_inverse(lmat):
    C = CHUNK
    eye = jnp.where(_iota((C, C), 0) == _iota((C, C), 1), 1.0, 0.0).astype(F32)[None]
    a = -lmat
    tinv = eye + a
    pw = _bmmx(a, a)
    for _ in range(4):
        both = _bmmx(jnp.concatenate([pw, tinv], axis=1), pw)
        pw, tinv = both[:, :C], tinv + both[:, C:]
    return tinv + _bmmx(tinv, pw)


def _inverse_bwd(tinv, d_tinv):
    x = lax.dot_general(d_tinv, tinv, (((2,), (2,)), ((0,), (0,))), precision=lax.Precision.HIGHEST,
                        preferred_element_type=F32)
    return -lax.dot_general(tinv, x, (((1,), (1,)), ((0,), (0,))), precision=lax.Precision.HIGHEST,
                            preferred_element_type=F32)


@jax.custom_vjp
def _tri_inverse(lmat):
    return _neumann_inverse(lmat)


def _tri_inverse_fwd(lmat):
    tinv = _neumann_inverse(lmat)
    return tinv, tinv


def _tri_inverse_bwd(tinv, d_tinv):
    return (_inverse_bwd(tinv, d_tinv),)


_tri_inverse.defvjp(_tri_inverse_fwd, _tri_inverse_bwd)


@jax.custom_vjp
def _tri_inverse_known(lmat, tinv):
    return tinv


def _tri_inverse_known_fwd(lmat, tinv):
    return tinv, tinv


def _tri_inverse_known_bwd(tinv, d_tinv):
    return _inverse_bwd(tinv, d_tinv), jnp.zeros_like(tinv)


_tri_inverse_known.defvjp(_tri_inverse_known_fwd, _tri_inverse_known_bwd)


def _dn_intra(q, k, v, bg, tinv=None):
    C = CHUNK
    G = bg.shape[0]
    r = _iota((C, C), 0)
    c = _iota((C, C), 1)
    incl = (r >= c)[None]
    strict = (r > c)[None]
    eye = jnp.where(r == c, 1.0, 0.0).astype(F32)[None]
    tri = jnp.broadcast_to(jnp.where(r >= c, 1.0, 0.0).astype(F32)[None], (G, C, C))
    gc_all = _bmmx(tri, bg)
    lane = _iota((C, DN_D), 1)

    def per_head(x, offset):
        return jnp.concatenate([jnp.sum(jnp.where(lane == offset + h, x[g], 0.0), axis=-1, keepdims=True)[None]
                                for g in range(G) for h in range(DN_H)], axis=0)
    beta = per_head(bg, 0)
    gcol = per_head(gc_all, DN_H)
    grow = jnp.sum(eye * gcol, axis=1, keepdims=True)
    glast = jnp.sum(jnp.where(_iota((1, C, 1), 1) == C - 1, gcol, 0.0), axis=1, keepdims=True)
    decay = jnp.exp(jnp.where(incl, gcol - grow, NEG))
    q = q * (DN_D ** -0.5)
    kb = k * beta
    lmat = jnp.where(strict, _bmm_nt(kb, k) * decay, 0.0)
    tinv = _tri_inverse(lmat) if tinv is None else _tri_inverse_known(lmat, tinv)
    egc = jnp.exp(gcol)
    u = _bmm(tinv, v * beta)
    w = _bmm(tinv, kb * egc)
    a = _bmm_nt(q, k) * decay
    return u, w, q * egc, k * jnp.exp(glast - gcol), a, jnp.exp(glast), tinv


@jax.custom_vjp
def _bmm_tn(a, b):
    return _bdg(a, b, 1, 1)


def _bmm_tn_fwd(a, b):
    return _bdg(a, b, 1, 1), (a, b)


def _bmm_tn_bwd(res, dy):
    a, b = res
    return _bdg(b, dy, 2, 2), _bdg(a, dy, 2, 1)


_bmm_tn.defvjp(_bmm_tn_fwd, _bmm_tn_bwd)


def _dn_rec(state, u, w, qd, kd, a, cd):
    v_new = u - _bmm(w, state)
    out = _bmm(qd, state) + _bmm(a, v_new)
    return state * cd + _bmm_tn(kd, v_new), out


def _mix_tile(o_attn, o_raw, zs, ga, gb, x, gate1, dn_g, wb_a, wb_d, w_out, p_ya, p_yd, p_out):
    o_dn = jnp.concatenate([_rms(o_raw[h], dn_g) * _silu(zs[h]) for h in range(DN_H)], axis=-1)
    y_a = _mm(o_attn, wb_a) + p_ya
    y_d = _mm(o_dn, wb_d) + p_yd
    merged = _sigmoid(ga) * y_a + _sigmoid(gb) * y_d
    out = _mm(merged, w_out) + p_out
    return x + gate1 * out, o_dn, merged


def _mlp_tile(x1, gain, shift, scale, gate2, w_gu, w_dn, tgt, p_gu, p_yy):
    h2 = _norm_mod(x1, gain, shift, scale)
    gu = jnp.concatenate([_mm(h2, w) for w in w_gu], axis=-1) + p_gu
    act = _silu(gu[:, :FFN]) * gu[:, FFN:]
    yy = _mm(act, w_dn) + p_yy
    y = x1 + gate2 * yy
    err = y - tgt
    return 0.5 * jnp.sum(err * err) * (1.0 / D), (h2, act)


def _tok(bt, f):
    return pl.BlockSpec((None, bt, f), lambda b, i: (b, i, 0))


def _full(shape):
    return pl.BlockSpec(shape, lambda b, i: (0,) * len(shape))


def _resident(shape):
    return pl.BlockSpec(shape, lambda b, i: (0,) * len(shape), pipeline_mode=pl.Buffered(1))


def _per_batch(f):
    return pl.BlockSpec((None, 1, f), lambda b, i: (b, 0, 0))


def _sds(shape, dtype):
    return jax.ShapeDtypeStruct(shape, dtype)


class _Exchange(NamedTuple):
    ins: tuple
    out_shapes: tuple
    n_remote: int
    plan: Callable
    n_forward: int = 0
    forward: Callable = None


def _remote_copies(remote, send_sems, recv_sems):
    return [pltpu.make_async_remote_copy(src_ref=src, dst_ref=dst, send_sem=send_sems.at[i], recv_sem=recv_sems.at[i],
                                         device_id=peer, device_id_type=pl.DeviceIdType.MESH)
            for i, (src, dst, peer) in enumerate(remote)]


def _exchange_copies(ex, in_refs, out_refs, send_sems, recv_sems):
    remote = ex.plan((lax.axis_index("x"), lax.axis_index("y"), lax.axis_index("c")), in_refs, out_refs)
    assert len(remote) == ex.n_remote
    return _remote_copies(remote, send_sems, recv_sems)


def _forward_copies(ex, out_refs, send_sems, recv_sems):
    remote = ex.forward((lax.axis_index("x"), lax.axis_index("y"), lax.axis_index("c")), out_refs)
    assert len(remote) == ex.n_forward
    return _remote_copies(remote, send_sems, recv_sems)


def _exchange_sems(ex):
    sems = [pltpu.SemaphoreType.DMA((ex.n_remote,)), pltpu.SemaphoreType.DMA((ex.n_remote,))]
    if ex.n_forward:
        sems += [pltpu.SemaphoreType.DMA((ex.n_forward,)), pltpu.SemaphoreType.DMA((ex.n_forward,))]
    return sems


def _hosted_call(body, name, grid, in_specs, out_specs, out_shape, scratch_shapes, semantics, ins, ex=None):
    if ex is None:
        outs = pl.pallas_call(body, name=name, grid=grid, in_specs=in_specs, out_specs=out_specs, out_shape=out_shape,
                              scratch_shapes=scratch_shapes,
                              compiler_params=_cparams(dimension_semantics=semantics))(*ins)
        return outs, ()
    n_in, n_out, n_scr = len(ins), len(out_shape), len(scratch_shapes)
    c_in, c_out = len(ex.ins), len(ex.out_shapes)
    steps = 1
    for g in grid:
        steps *= g

    def wrapped(*refs):
        a, b, c, d = n_in, n_in + c_in, n_in + c_in + n_out, n_in + c_in + n_out + c_out
        scratch, sems = refs[d:d + n_scr], refs[d + n_scr:]
        step = 0
        for axis, g in enumerate(grid):
            step = step * g + pl.program_id(axis)

        def first_phase():
            return _exchange_copies(ex, refs[a:b], refs[c:d], sems[0], sems[1])

        @pl.when(step == 0)
        def _():
            for cp in first_phase():
                cp.start()
        body(*refs[:a], *refs[b:c], *scratch)

        if ex.n_forward:
            @pl.when(step == (3 * steps) // 4)
            def _():
                for cp in first_phase():
                    cp.wait_recv()
                for cp in _forward_copies(ex, refs[c:d], sems[2], sems[3]):
                    cp.start()

        @pl.when(step == steps - 1)
        def _():
            cps = first_phase()
            if ex.n_forward:
                fwd = _forward_copies(ex, refs[c:d], sems[2], sems[3])
                for cp in fwd:
                    cp.wait_recv()
                for cp in cps + fwd:
                    cp.wait_send()
            else:
                for cp in cps:
                    cp.wait_recv()
                for cp in cps:
                    cp.wait_send()

    any_spec = pl.BlockSpec(memory_space=pl.ANY)
    res = pl.pallas_call(
        wrapped, name=name, grid=grid, in_specs=list(in_specs) + [any_spec] * c_in,
        out_specs=list(out_specs) + [any_spec] * c_out, out_shape=list(out_shape) + list(ex.out_shapes),
        scratch_shapes=list(scratch_shapes) + _exchange_sems(ex),
        compiler_params=_cparams(dimension_semantics=("arbitrary",) * len(grid)),
    )(*ins, *ex.ins)
    return res[:n_out], res[n_out:]


def _acc(ref, val, first):
    @pl.when(first)
    def _():
        ref[...] = val

    @pl.when(jnp.logical_not(first))
    def _():
        ref[...] += val


def _in_proj(x, mod, norm1_g, w_in, bt):
    B, S, _ = x.shape

    def body(x_ref, mod_ref, g_ref, w_ref, q_ref, kv_ref, dn_ref, z_ref, ga_ref, gb_ref, ba_ref, h_ref):
        h = _norm_mod(x_ref[...], g_ref[...], mod_ref[:, 0:D], mod_ref[:, D:2 * D]).astype(BF)
        h_ref[...] = h

        def proj(c0, c1):
            return jnp.dot(h, w_ref[:, c0:c1], preferred_element_type=F32)
        q_ref[...] = proj(0, C_KV).astype(BF)
        kv_ref[...] = proj(C_KV, C_DN).astype(BF)
        dn_ref[...] = proj(C_DN, C_Z).astype(BF)
        z_ref[...] = proj(C_Z, C_GA).astype(BF)
        ga_ref[...] = proj(C_GA, C_GB).astype(BF)
        gb_ref[...] = proj(C_GB, C_BA).astype(BF)
        ba_ref[...] = proj(C_BA, IN_PAD)

    widths = (QW, 2 * KVW, CONVW, DNW, D, D)
    return pl.pallas_call(
        body, name="in_proj", grid=(B, S // bt),
        in_specs=[_tok(bt, D), _per_batch(6 * D), _full((1, D)), _resident((D, IN_PAD))],
        out_specs=[_tok(bt, w) for w in widths] + [_tok(bt, 128), _tok(bt, D)],
        out_shape=[_sds((B, S, w), BF) for w in widths] + [_sds((B, S, 128), F32), _sds((B, S, D), BF)],
        compiler_params=_cparams(dimension_semantics=("parallel", "parallel")),
    )(x, mod, norm1_g, w_in)


def _prev_blk(bt, f):
    return pl.BlockSpec((None, bt, f), lambda b, i: (b, jnp.maximum(i - 1, 0), 0))


QKV = QW + 2 * KVW


def _qk_slabs(q_ref, kv_ref):
    return ([q_ref[:, j * 2 * HD:(j + 1) * 2 * HD].astype(F32) for j in range(QW // (2 * HD))],
            [kv_ref[:, 0:KVW].astype(F32)])


def _qk_prep_fwd(q, kv, cos, sin, qg, kg, bt):
    B, S, _ = q.shape

    def body(q_ref, kv_ref, cos_ref, sin_ref, qg_ref, kg_ref, o_ref):
        qs, ks = _qk_slabs(q_ref, kv_ref)
        qn = _qk_prep(qs, qg_ref[...], cos_ref[...], sin_ref[...])
        kn = _qk_prep(ks, kg_ref[...], cos_ref[...], sin_ref[...])
        for j, t in enumerate(qn + kn):
            o_ref[:, j * 2 * HD:(j + 1) * 2 * HD] = t.astype(BF)
        o_ref[:, QW + KVW:QKV] = kv_ref[:, KVW:2 * KVW]

    return pl.pallas_call(
        body, name="qk_prep_fwd", grid=(B, S // bt),
        in_specs=[_tok(bt, QW), _tok(bt, 2 * KVW), _tok(bt, 2 * HD), _tok(bt, 2 * HD), _full((1, 2 * HD)), _full((1, 2 * HD))],
        out_specs=_tok(bt, QKV), out_shape=_sds((B, S, QKV), BF),
        compiler_params=_cparams(dimension_semantics=("parallel", "parallel")),
    )(q, kv, cos, sin, qg, kg)


def _qk_prep_bwd(q, kv, cos, sin, qg, kg, dqn, dkvn, bt, ex=None):
    B, S, _ = q.shape

    def body(q_ref, kv_ref, cos_ref, sin_ref, qg_ref, kg_ref, dqn_ref, dkvn_ref, dq_ref, dkv_ref, dqg_ref, dkg_ref):
        qs, ks = _qk_slabs(q_ref, kv_ref)
        cos, sin = cos_ref[...], sin_ref[...]

        def f(qs, ks, qg, kg):
            return _qk_prep(qs, qg, cos, sin), _qk_prep(ks, kg, cos, sin)
        _, vjp = jax.vjp(f, qs, ks, qg_ref[...], kg_ref[...])
        n_q = len(qs)
        d_q = [dqn_ref[:, j * 2 * HD:(j + 1) * 2 * HD].astype(F32) for j in range(n_q)]
        d_k = [dkvn_ref[:, 0:KVW].astype(F32)]
        dqs, dks, dqg, dkg = vjp((d_q, d_k))
        for j in range(n_q):
            dq_ref[:, j * 2 * HD:(j + 1) * 2 * HD] = dqs[j].astype(BF)
        dkv_ref[:, 0:KVW] = dks[0].astype(BF)
        dkv_ref[:, KVW:2 * KVW] = dkvn_ref[:, KVW:2 * KVW]
        first = (pl.program_id(0) == 0) & (pl.program_id(1) == 0)
        _acc(dqg_ref, dqg, first)
        _acc(dkg_ref, dkg, first)

    return _hosted_call(
        body, "qk_prep_bwd", (B, S // bt),
        in_specs=[_tok(bt, QW), _tok(bt, 2 * KVW), _tok(bt, 2 * HD), _tok(bt, 2 * HD), _full((1, 2 * HD)), _full((1, 2 * HD)),
                  _tok(bt, QW), _tok(bt, 2 * KVW)],
        out_specs=[_tok(bt, QW), _tok(bt, 2 * KVW), _full((1, 2 * HD)), _full((1, 2 * HD))],
        out_shape=[_sds((B, S, QW), BF), _sds((B, S, 2 * KVW), BF), _sds((1, 2 * HD), F32), _sds((1, 2 * HD), F32)],
        scratch_shapes=[], semantics=("arbitrary", "arbitrary"), ins=(q, kv, cos, sin, qg, kg, dqn, dkvn), ex=ex)


def _attn_load(qkv_ref, kvp_ref):
    qs = [qkv_ref[:, h * HD:(h + 1) * HD].astype(F32) for h in range(HEADS)]
    kc = [qkv_ref[:, QW + h * HD:QW + (h + 1) * HD].astype(F32) for h in range(KV_HEADS)]
    vc = [qkv_ref[:, QW + KVW + h * HD:QW + KVW + (h + 1) * HD].astype(F32) for h in range(KV_HEADS)]
    kp = [kvp_ref[:, h * HD:(h + 1) * HD].astype(F32) for h in range(KV_HEADS)]
    vp = [kvp_ref[:, KVW + h * HD:KVW + (h + 1) * HD].astype(F32) for h in range(KV_HEADS)]
    return qs, kc, kp, vc, vp


def _kv_prev_spec(index):
    return pl.BlockSpec((None, BLK, 2 * KVW), lambda b, i: (b, index(i), QW // (2 * KVW)))


def _attn_fwd(qkv, sinks):
    B, S, _ = qkv.shape

    def body(qkv_ref, kvp_ref, sk_ref, o_ref):
        qs, kc, kp, vc, vp = _attn_load(qkv_ref, kvp_ref)
        outs = _attn_block(qs, kc, kp, vc, vp, sk_ref[...], pl.program_id(1) > 0)
        for h in range(HEADS):
            o_ref[:, h * HD:(h + 1) * HD] = outs[h].astype(BF)

    return pl.pallas_call(
        body, name="attn_fwd", grid=(B, S // BLK),
        in_specs=[_tok(BLK, QKV), _kv_prev_spec(lambda i: jnp.maximum(i - 1, 0)), _full((1, HEADS))],
        out_specs=_tok(BLK, QW), out_shape=_sds((B, S, QW), BF),
        compiler_params=_cparams(dimension_semantics=("parallel", "parallel")),
    )(qkv, qkv, sinks)


def _halo_spec(bt):
    return pl.BlockSpec((None, 8, CONVW), lambda b, i: (b, jnp.maximum(i * (bt // 8) - 1, 0), 0))


def _dn_prep(dn, ba, conv_w, alog, dtb, bt):
    B, S, _ = dn.shape

    strip_rows = min(bt, 64)

    def body(x_ref, halo_ref, ba_ref, cw_ref, al_ref, dt_ref, qkv_ref, bg_ref, y_ref, xe_ref):
        xe_ref[0:8, :] = jnp.where(pl.program_id(1) == 0, 0.0, halo_ref[...].astype(F32))
        xe_ref[8:bt + 8, :] = x_ref[...].astype(F32)

        def strip(k, carry):
            r0 = pl.multiple_of(k * strip_rows, strip_rows)
            rows = pl.ds(r0, strip_rows)
            for j in range(3 * DN_H):
                cols = slice(j * DN_D, (j + 1) * DN_D)
                window = xe_ref[pl.ds(r0, strip_rows + 8), cols]
                y = cw_ref[0:1, cols] * window[5:strip_rows + 5]
                for t in range(1, CONV):
                    y = y + cw_ref[t:t + 1, cols] * window[5 + t:strip_rows + 5 + t]
                y_ref[rows, cols] = y.astype(BF)
                qkv_ref[rows, cols] = _dn_act(y, j < 2 * DN_H)
            bg_ref[rows, :] = _dn_gates(ba_ref[rows, :], al_ref[...], dt_ref[...])
            return carry
        lax.fori_loop(0, bt // strip_rows, strip, 0)

    return pl.pallas_call(
        body, name="dn_prep", grid=(B, S // bt),
        in_specs=[_tok(bt, CONVW), _halo_spec(bt), _tok(bt, 128), _full((CONV, CONVW)), _full((1, 128)), _full((1, 128))],
        out_specs=[_tok(bt, CONVW), _tok(bt, 128), _tok(bt, CONVW)],
        out_shape=[_sds((B, S, CONVW), F32), _sds((B, S, 128), F32), _sds((B, S, CONVW), BF)],
        scratch_shapes=[pltpu.VMEM((bt + 8, CONVW), F32)],
        compiler_params=_cparams(dimension_semantics=("parallel", "arbitrary")),
    )(dn, dn, ba, conv_w, alog, dtb)


def _dn_load(qkv_ref):
    qs = [qkv_ref[:, h * DN_D:(h + 1) * DN_D] for h in range(DN_H)]
    ks = [qkv_ref[:, DNW + h * DN_D:DNW + (h + 1) * DN_D] for h in range(DN_H)]
    vs = [qkv_ref[:, 2 * DNW + h * DN_D:2 * DNW + (h + 1) * DN_D] for h in range(DN_H)]
    return qs, ks, vs


DN_GROUP = 4
AW = DN_H * CHUNK


def _stack_heads(ref, G, offset, width):
    return jnp.stack([ref[g * CHUNK:(g + 1) * CHUNK, offset + h * width:offset + (h + 1) * width]
                      for g in range(G) for h in range(DN_H)])


def _dn_load_stack(qkv_ref, G):
    return tuple(_stack_heads(qkv_ref, G, j * DNW, DN_D) for j in range(3))


def _cd_spec(n):
    return pl.BlockSpec((None, n, 1, DN_D), lambda b, i: (b, i, 0, 0))


def _dn_intra_fwd(qkv, bg, ex=None):
    B, S, _ = qkv.shape
    nc = S // CHUNK
    G = min(DN_GROUP, nc)
    rows = G * CHUNK

    def body(qkv_ref, bg_ref, u_ref, w_ref, qd_ref, kd_ref, a_ref, cd_ref, t_ref):
        q, k, v = _dn_load_stack(qkv_ref, G)
        u, w, qd, kd, a, cd, tinv = _dn_intra(q, k, v, bg_ref[...].reshape(G, CHUNK, DN_D))
        lane_row = _iota((1, DN_D), 1)
        for g in range(G):
            rows = slice(g * CHUNK, (g + 1) * CHUNK)
            cd_row = jnp.zeros((1, DN_D), F32)
            for h in range(DN_H):
                n = g * DN_H + h
                cols = slice(h * DN_D, (h + 1) * DN_D)
                u_ref[rows, cols] = u[n]
                w_ref[rows, cols] = w[n].astype(BF)
                qd_ref[rows, cols] = qd[n].astype(BF)
                kd_ref[rows, cols] = kd[n].astype(BF)
                a_ref[rows, h * CHUNK:(h + 1) * CHUNK] = a[n].astype(BF)
                t_ref[rows, h * CHUNK:(h + 1) * CHUNK] = tinv[n]
                cd_row = cd_row + jnp.where(lane_row == h, cd[n], 0.0)
            cd_ref[g] = cd_row

    return _hosted_call(
        body, "dn_intra_fwd", (B, nc // G),
        in_specs=[_tok(rows, CONVW), _tok(rows, 128)],
        out_specs=[_tok(rows, DNW)] * 4 + [_tok(rows, AW), _cd_spec(G), _tok(rows, AW)],
        out_shape=[_sds((B, S, DNW), F32)] + [_sds((B, S, DNW), BF)] * 3 + [_sds((B, S, AW), BF), _sds((B, nc, 1, DN_D), F32),
                                                                            _sds((B, S, AW), F32)],
        scratch_shapes=[], semantics=("parallel", "parallel"), ins=(qkv, bg), ex=ex)


REC_GROUP = 4


def _rec_stack(ref, B, width, c):
    rows = slice(c * CHUNK, (c + 1) * CHUNK)
    return jnp.stack([ref[b, rows, h * width:(h + 1) * width].astype(F32) for b in range(B) for h in range(DN_H)])


def _rec_load(B, u_ref, w_ref, qd_ref, kd_ref, a_ref, cd_ref, c):
    lane_row = _iota((1, DN_D), 1)
    cd = jnp.stack([jnp.sum(jnp.where(lane_row == h, cd_ref[b, c], 0.0), axis=-1, keepdims=True)
                    for b in range(B) for h in range(DN_H)])
    return (_rec_stack(u_ref, B, DN_D, c), _rec_stack(w_ref, B, DN_D, c), _rec_stack(qd_ref, B, DN_D, c),
            _rec_stack(kd_ref, B, DN_D, c), _rec_stack(a_ref, B, CHUNK, c), cd)


def _rec_store(B, ref, val, width, c):
    for b in range(B):
        for h in range(DN_H):
            ref[b, c * CHUNK:(c + 1) * CHUNK, h * width:(h + 1) * width] = val[b * DN_H + h]


def _rec_specs(B, R, index):
    def tok(f):
        return pl.BlockSpec((B, R * CHUNK, f), lambda i: (0, index(i), 0))
    cd = pl.BlockSpec((B, R, 1, DN_D), lambda i: (0, index(i), 0, 0))
    st = pl.BlockSpec((B, R, DN_H, DN_D, DN_D), lambda i: (0, index(i), 0, 0, 0))
    return tok, cd, st


def _dn_rec_fwd(u, w, qd, kd, a, cd, ex=None):
    B, S, _ = u.shape
    nc = S // CHUNK
    R = REC_GROUP if nc % REC_GROUP == 0 else 1
    tok, cd_spec, st_spec = _rec_specs(B, R, lambda i: i)

    def body(u_ref, w_ref, qd_ref, kd_ref, a_ref, cd_ref, o_ref, st_ref, s_ref):
        @pl.when(pl.program_id(0) == 0)
        def _():
            s_ref[...] = jnp.zeros_like(s_ref)
        state = s_ref[...]
        for c in range(R):
            st_ref[:, c] = state.reshape(B, DN_H, DN_D, DN_D)
            state, out = _dn_rec(state, *_rec_load(B, u_ref, w_ref, qd_ref, kd_ref, a_ref, cd_ref, c))
            _rec_store(B, o_ref, out, DN_D, c)
        s_ref[...] = state

    return _hosted_call(
        body, "dn_rec_fwd", (nc // R,),
        in_specs=[tok(DNW)] * 4 + [tok(AW), cd_spec],
        out_specs=[tok(DNW), st_spec],
        out_shape=[_sds((B, S, DNW), F32), _sds((B, nc, DN_H, DN_D, DN_D), F32)],
        scratch_shapes=[pltpu.VMEM((B * DN_H, DN_D, DN_D), F32)],
        semantics=("arbitrary",), ins=(u, w, qd, kd, a, cd), ex=ex)


def _mix_load(oa_ref, or_ref, z_ref):
    o_raw = [or_ref[:, h * DN_D:(h + 1) * DN_D] for h in range(DN_H)]
    zs = [z_ref[:, h * DN_D:(h + 1) * DN_D].astype(F32) for h in range(DN_H)]
    return oa_ref[...].astype(F32), o_raw, zs


def _mix_fwd(o_attn, o_raw, z, ga, gb, x, mod, dn_g, w_branch, w_out, bt):
    B, S, _ = x.shape

    def body(oa_ref, or_ref, z_ref, ga_ref, gb_ref, x_ref, mod_ref, dg_ref, wb_ref, wo_ref, x1_ref, od_ref, mg_ref):
        oa, o_r, zs = _mix_load(oa_ref, or_ref, z_ref)
        x1, o_dn, merged = _mix_tile(oa, o_r, zs, ga_ref[...].astype(F32), gb_ref[...].astype(F32), x_ref[...],
                                     mod_ref[:, 2 * D:3 * D], dg_ref[...], wb_ref[0:QW, :], wb_ref[QW:2 * QW, :],
                                     wo_ref[...], 0.0, 0.0, 0.0)
        x1_ref[...] = x1
        od_ref[...] = o_dn.astype(BF)
        mg_ref[...] = merged.astype(BF)

    return pl.pallas_call(
        body, name="mix_fwd", grid=(B, S // bt),
        in_specs=[_tok(bt, QW), _tok(bt, DNW), _tok(bt, DNW), _tok(bt, D), _tok(bt, D), _tok(bt, D), _per_batch(6 * D),
                  _full((1, DN_D)), _resident((D, D)), _resident((D, D))],
        out_specs=[_tok(bt, D), _tok(bt, DNW), _tok(bt, D)],
        out_shape=[_sds((B, S, D), F32), _sds((B, S, DNW), BF), _sds((B, S, D), BF)],
        compiler_params=_cparams(dimension_semantics=("parallel", "parallel")),
    )(o_attn, o_raw, z, ga, gb, x, mod, dn_g, w_branch, w_out)


def _mlp(x1, tgt, mod, norm2_g, w_gu, w_dn, bt):
    B, S, _ = x1.shape

    def body(x1_ref, t_ref, mod_ref, g_ref, wgu_ref, wdn_ref,
             dx1_ref, h2_ref, act_ref, dgu_ref, dyy_ref, loss_ref, dmod_ref, dg_ref):
        w_gu_v, w_dn_v, t = [wgu_ref[k] for k in range(N_CHIP)], wdn_ref[...], t_ref[...]

        def f(x1, gain, shift, scale, gate2, p_gu, p_yy):
            return _mlp_tile(x1, gain, shift, scale, gate2, w_gu_v, w_dn_v, t, p_gu, p_yy)
        zero_gu = jnp.zeros((bt, 2 * FFN), F32)
        zero_yy = jnp.zeros((bt, D), F32)
        loss, vjp, (h2, act) = jax.vjp(f, x1_ref[...], g_ref[...], mod_ref[:, 3 * D:4 * D], mod_ref[:, 4 * D:5 * D],
                                       mod_ref[:, 5 * D:6 * D], zero_gu, zero_yy, has_aux=True)
        dx1, dgain, dshift, dscale, dgate2, dgu, dyy = vjp(jnp.ones((), F32))
        dx1_ref[...] = dx1
        h2_ref[...] = h2.astype(BF)
        act_ref[...] = act.astype(BF)
        dgu_ref[...] = dgu.astype(BF)
        dyy_ref[...] = dyy.astype(BF)
        first = pl.program_id(1) == 0
        _acc(loss_ref, jnp.reshape(loss, (1, 1)), first)
        _acc(dmod_ref, jnp.concatenate([dshift, dscale, dgate2], axis=-1), first)
        _acc(dg_ref, dgain, first)

    return pl.pallas_call(
        body, name="mlp", grid=(B, S // bt),
        in_specs=[_tok(bt, D), _tok(bt, D), _per_batch(6 * D), _full((1, D)), _resident((N_CHIP, D, 2 * FFN // N_CHIP)),
                  _resident((FFN, D))],
        out_specs=[_tok(bt, D), _tok(bt, D), _tok(bt, FFN), _tok(bt, 2 * FFN), _tok(bt, D),
                   _per_batch(1), _per_batch(3 * D), _per_batch(D)],
        out_shape=[_sds((B, S, D), F32), _sds((B, S, D), BF), _sds((B, S, FFN), BF), _sds((B, S, 2 * FFN), BF),
                   _sds((B, S, D), BF), _sds((B, 1, 1), F32), _sds((B, 1, 3 * D), F32), _sds((B, 1, D), F32)],
        compiler_params=_cparams(dimension_semantics=("parallel", "arbitrary")),
    )(x1, tgt, mod, norm2_g, w_gu, w_dn)


def _mix_bwd(o_attn, o_raw, z, ga, gb, x, mod, dn_g, w_branch, w_out, dx1, bt, ex=None):
    B, S, _ = x.shape

    def body(oa_ref, or_ref, z_ref, ga_ref, gb_ref, x_ref, mod_ref, dg_ref, wb_ref, wo_ref, dx1_ref,
             doa_ref, dor_ref, dz_ref, dga_ref, dgb_ref, dya_ref, dyd_ref, dout_ref, dgate_ref, ddg_ref):
        oa, o_r, zs = _mix_load(oa_ref, or_ref, z_ref)
        wb_a, wb_d, wo = wb_ref[0:QW, :], wb_ref[QW:2 * QW, :], wo_ref[...]

        def f(oa, o_r, zs, ga, gb, gate1, dn_g, p_ya, p_yd, p_out):
            return _mix_tile(oa, o_r, zs, ga, gb, x_ref[...], gate1, dn_g, wb_a, wb_d, wo, p_ya, p_yd, p_out)[0]
        zero = jnp.zeros((bt, D), F32)
        _, vjp = jax.vjp(f, oa, o_r, zs, ga_ref[...].astype(F32), gb_ref[...].astype(F32), mod_ref[:, 2 * D:3 * D],
                         dg_ref[...], zero, zero, zero)
        doa, dor, dzs, dga, dgb, dgate1, ddn_g, dya, dyd, dout = vjp(dx1_ref[...])
        doa_ref[...] = doa
        for h in range(DN_H):
            dor_ref[:, h * DN_D:(h + 1) * DN_D] = dor[h]
            dz_ref[:, h * DN_D:(h + 1) * DN_D] = dzs[h].astype(BF)
        dga_ref[...] = dga.astype(BF)
        dgb_ref[...] = dgb.astype(BF)
        dya_ref[...] = dya.astype(BF)
        dyd_ref[...] = dyd.astype(BF)
        dout_ref[...] = dout.astype(BF)
        first = pl.program_id(1) == 0
        _acc(dgate_ref, dgate1, first)
        _acc(ddg_ref, ddn_g, first)

    return _hosted_call(
        body, "mix_bwd", (B, S // bt),
        in_specs=[_tok(bt, QW), _tok(bt, DNW), _tok(bt, DNW), _tok(bt, D), _tok(bt, D), _tok(bt, D), _per_batch(6 * D),
                  _full((1, DN_D)), _resident((D, D)), _resident((D, D)), _tok(bt, D)],
        out_specs=[_tok(bt, QW), _tok(bt, DNW), _tok(bt, DNW), _tok(bt, D), _tok(bt, D), _tok(bt, D), _tok(bt, D), _tok(bt, D),
                   _per_batch(D), _per_batch(DN_D)],
        out_shape=[_sds((B, S, QW), F32), _sds((B, S, DNW), F32), _sds((B, S, DNW), BF), _sds((B, S, D), BF),
                   _sds((B, S, D), BF), _sds((B, S, D), BF), _sds((B, S, D), BF), _sds((B, S, D), BF),
                   _sds((B, 1, D), F32), _sds((B, 1, DN_D), F32)],
        scratch_shapes=[], semantics=("parallel", "arbitrary"),
        ins=(o_attn, o_raw, z, ga, gb, x, mod, dn_g, w_branch, w_out, dx1), ex=ex)


def _dn_rec_bwd(u, w, qd, kd, a, cd, states, d_o, ex=None):
    B, S, _ = u.shape
    nc = S // CHUNK
    R = REC_GROUP if nc % REC_GROUP == 0 else 1
    tok, cd_spec, st_spec = _rec_specs(B, R, lambda i: nc // R - 1 - i)

    def body(u_ref, w_ref, qd_ref, kd_ref, a_ref, cd_ref, st_ref, do_ref,
             du_ref, dw_ref, dqd_ref, dkd_ref, da_ref, dcd_ref, ds_ref):
        @pl.when(pl.program_id(0) == 0)
        def _():
            ds_ref[...] = jnp.zeros_like(ds_ref)
        lane_row = _iota((1, DN_D), 1)
        d_state = ds_ref[...]
        for c in reversed(range(R)):
            state = st_ref[:, c].reshape(B * DN_H, DN_D, DN_D)
            _, vjp = jax.vjp(_dn_rec, state, *_rec_load(B, u_ref, w_ref, qd_ref, kd_ref, a_ref, cd_ref, c))
            d_state, du, dw, dqd, dkd, da, dcd = vjp((d_state, _rec_stack(do_ref, B, DN_D, c)))
            for ref, val, width in ((du_ref, du, DN_D), (dw_ref, dw, DN_D), (dqd_ref, dqd, DN_D), (dkd_ref, dkd, DN_D),
                                    (da_ref, da, CHUNK)):
                _rec_store(B, ref, val, width, c)
            for b in range(B):
                row = jnp.zeros((1, DN_D), F32)
                for h in range(DN_H):
                    row = row + jnp.where(lane_row == h, dcd[b * DN_H + h], 0.0)
                dcd_ref[b, c] = row
        ds_ref[...] = d_state

    return _hosted_call(
        body, "dn_rec_bwd", (nc // R,),
        in_specs=[tok(DNW)] * 4 + [tok(AW), cd_spec, st_spec, tok(DNW)],
        out_specs=[tok(DNW)] * 4 + [tok(AW), cd_spec],
        out_shape=[_sds((B, S, DNW), F32)] * 4 + [_sds((B, S, AW), F32), _sds((B, nc, 1, DN_D), F32)],
        scratch_shapes=[pltpu.VMEM((B * DN_H, DN_D, DN_D), F32)],
        semantics=("arbitrary",), ins=(u, w, qd, kd, a, cd, states, d_o), ex=ex)


def _dn_intra_bwd(qkv, bg, tinv, du, dw, dqd, dkd, da, dcd, ex=None):
    B, S, _ = qkv.shape
    nc = S // CHUNK
    G = min(DN_GROUP, nc)
    rows = G * CHUNK

    def body(qkv_ref, bg_ref, t_ref, du_ref, dw_ref, dqd_ref, dkd_ref, da_ref, dcd_ref, dqkv_ref, dbg_ref):
        q, k, v = _dn_load_stack(qkv_ref, G)
        known = _stack_heads(t_ref, G, 0, CHUNK)
        _, vjp = jax.vjp(lambda q, k, v, bg: _dn_intra(q, k, v, bg, known)[:6], q, k, v,
                         bg_ref[...].reshape(G, CHUNK, DN_D))
        lane_row = _iota((1, DN_D), 1)
        dcd = jnp.stack([jnp.sum(jnp.where(lane_row == h, dcd_ref[g], 0.0), axis=-1, keepdims=True)
                         for g in range(G) for h in range(DN_H)])
        dq, dk, dv, dbg = vjp((_stack_heads(du_ref, G, 0, DN_D), _stack_heads(dw_ref, G, 0, DN_D),
                               _stack_heads(dqd_ref, G, 0, DN_D), _stack_heads(dkd_ref, G, 0, DN_D),
                               _stack_heads(da_ref, G, 0, CHUNK), dcd))
        for g in range(G):
            rows = slice(g * CHUNK, (g + 1) * CHUNK)
            for h in range(DN_H):
                n = g * DN_H + h
                dqkv_ref[rows, h * DN_D:(h + 1) * DN_D] = dq[n]
                dqkv_ref[rows, DNW + h * DN_D:DNW + (h + 1) * DN_D] = dk[n]
                dqkv_ref[rows, 2 * DNW + h * DN_D:2 * DNW + (h + 1) * DN_D] = dv[n]
        dbg_ref[...] = dbg.reshape(G * CHUNK, DN_D)

    return _hosted_call(
        body, "dn_intra_bwd", (B, nc // G),
        in_specs=[_tok(rows, CONVW), _tok(rows, 128), _tok(rows, AW)] + [_tok(rows, DNW)] * 4 + [_tok(rows, AW), _cd_spec(G)],
        out_specs=[_tok(rows, CONVW), _tok(rows, 128)],
        out_shape=[_sds((B, S, CONVW), F32), _sds((B, S, 128), F32)],
        scratch_shapes=[], semantics=("parallel", "parallel"), ins=(qkv, bg, tinv, du, dw, dqd, dkd, da, dcd), ex=ex)


def _dn_prep_bwd(dn, y, ba, conv_w, alog, dtb, dqkv, dbg, bt, ex=None):
    B, S, _ = dn.shape
    nt = S // bt
    strip_rows = min(bt, 64)

    def rev(f):
        return pl.BlockSpec((None, bt, f), lambda b, i: (b, nt - 1 - i, 0))

    def body(x_ref, y_ref, ba_ref, cw_ref, al_ref, dt_ref, dqkv_ref, dbg_ref,
             dx_ref, dba_ref, dcw_ref, dal_ref, ddt_ref, dye_ref):
        i = pl.program_id(1)
        @pl.when(i == 0)
        def _():
            dye_ref[bt:bt + 8, :] = jnp.zeros((8, CONVW), F32)

        @pl.when(i > 0)
        def _():
            dye_ref[bt:bt + 8, :] = dye_ref[0:8, :]

        n_strips = bt // strip_rows

        def strip(k, carry):
            dal, ddt, dcw = carry
            r0 = pl.multiple_of((n_strips - 1 - k) * strip_rows, strip_rows)
            rows = pl.ds(r0, strip_rows)
            dcw_slabs = []
            for j in range(3 * DN_H):
                cols = slice(j * DN_D, (j + 1) * DN_D)
                _, vjp = jax.vjp(functools.partial(_dn_act, normalize=j < 2 * DN_H), y_ref[rows, cols].astype(F32))
                (dye_ref[rows, cols],) = vjp(dqkv_ref[rows, cols])
                window = dye_ref[pl.ds(r0, strip_rows + 8), cols]
                shifted = [window[3 - t:strip_rows + 3 - t] for t in range(CONV)]
                dx = cw_ref[0:1, cols] * shifted[0]
                for t in range(1, CONV):
                    dx = dx + cw_ref[t:t + 1, cols] * shifted[t]
                dx_ref[rows, cols] = dx.astype(BF)
                x = x_ref[rows, cols].astype(F32)
                dcw_slabs.append(jnp.concatenate([jnp.sum(shifted[t] * x, axis=0, keepdims=True) for t in range(CONV)], axis=0))
            _, vjp = jax.vjp(_dn_gates, ba_ref[rows, :], al_ref[...], dt_ref[...])
            dba_ref[rows, :], da, dd = vjp(dbg_ref[rows, :])
            return dal + da, ddt + dd, dcw + jnp.concatenate(dcw_slabs, axis=1)
        zero = jnp.zeros((1, DN_D), F32)
        dal, ddt, dcw = lax.fori_loop(0, n_strips, strip, (zero, zero, jnp.zeros((CONV, CONVW), F32)))
        first = (i == 0) & (pl.program_id(0) == 0)
        _acc(dcw_ref, dcw, first)
        _acc(dal_ref, dal, first)
        _acc(ddt_ref, ddt, first)

    return _hosted_call(
        body, "dn_prep_bwd", (B, nt),
        in_specs=[rev(CONVW), rev(CONVW), rev(128), _full((CONV, CONVW)), _full((1, 128)), _full((1, 128)), rev(CONVW), rev(128)],
        out_specs=[rev(CONVW), rev(128), _full((CONV, CONVW)), _full((1, 128)), _full((1, 128))],
        out_shape=[_sds((B, S, CONVW), BF), _sds((B, S, 128), F32), _sds((CONV, CONVW), F32), _sds((1, 128), F32),
                   _sds((1, 128), F32)],
        scratch_shapes=[pltpu.VMEM((bt + 8, CONVW), F32)],
        semantics=("arbitrary", "arbitrary"), ins=(dn, y, ba, conv_w, alog, dtb, dqkv, dbg), ex=ex)


def _attn_bwd(qkv, sinks, d_o, ex=None):
    B, S, _ = qkv.shape
    nb = S // BLK

    def cur(f):
        return pl.BlockSpec((None, BLK, f), lambda b, i: (b, jnp.minimum(i, nb - 1), 0))

    def out_prev(f):
        return pl.BlockSpec((None, BLK, f), lambda b, i: (b, jnp.maximum(i - 1, 0), 0))

    def body(qkv_ref, kvp_ref, sk_ref, do_ref, dq_ref, dkv_ref, dsk_ref, carry_ref):
        n = pl.program_id(1)
        first = (n == 0) & (pl.program_id(0) == 0)

        @pl.when(n == 0)
        def _():
            carry_ref[...] = jnp.zeros_like(carry_ref)

        @pl.when(n < nb)
        def _():
            qs, kc, kp, vc, vp = _attn_load(qkv_ref, kvp_ref)

            def f(qs, kc, kp, vc, vp, sk):
                return _attn_block(qs, kc, kp, vc, vp, sk, n > 0)
            _, vjp = jax.vjp(f, qs, kc, kp, vc, vp, sk_ref[...])
            d_outs = [do_ref[:, h * HD:(h + 1) * HD] for h in range(HEADS)]
            dqs, dkc, dkp, dvc, dvp, dsk = vjp(d_outs)
            for h in range(HEADS):
                dq_ref[:, h * HD:(h + 1) * HD] = dqs[h].astype(BF)
            for h in range(KV_HEADS):
                ksl = slice(h * HD, (h + 1) * HD)
                vsl = slice(KVW + h * HD, KVW + (h + 1) * HD)
                dkv_ref[:, ksl] = (carry_ref[:, ksl] + dkp[h]).astype(BF)
                dkv_ref[:, vsl] = (carry_ref[:, vsl] + dvp[h]).astype(BF)
                carry_ref[:, ksl] = dkc[h]
                carry_ref[:, vsl] = dvc[h]
            _acc(dsk_ref, dsk, first)

        @pl.when(n == nb)
        def _():
            dkv_ref[...] = carry_ref[...].astype(BF)

    return _hosted_call(
        body, "attn_bwd", (B, nb + 1),
        in_specs=[cur(QKV), _kv_prev_spec(lambda i: jnp.maximum(jnp.minimum(i, nb - 1) - 1, 0)), _full((1, HEADS)), cur(QW)],
        out_specs=[cur(QW), out_prev(2 * KVW), _full((1, HEADS))],
        out_shape=[_sds((B, S, QW), BF), _sds((B, S, 2 * KVW), BF), _sds((1, HEADS), F32)],
        scratch_shapes=[pltpu.VMEM((BLK, 2 * KVW), F32)],
        semantics=("arbitrary", "arbitrary"), ins=(qkv, qkv, sinks, d_o), ex=ex)


def _in_proj_bwd(x, mod, norm1_g, w_in, pieces, dba, dx1, bt):
    B, S, _ = x.shape
    widths = (QW, 2 * KVW, CONVW, DNW, D, D)

    def body(x_ref, mod_ref, g_ref, w_ref, dq_ref, dkv_ref, ddn_ref, dz_ref, dga_ref, dgb_ref, dba_ref, dx1_ref,
             gx_ref, dp_ref, dmod_ref, dg_ref):
        dp = jnp.concatenate([r[...] for r in (dq_ref, dkv_ref, ddn_ref, dz_ref, dga_ref, dgb_ref)]
                             + [dba_ref[...].astype(BF)], axis=-1)
        dp_ref[...] = dp
        dh = lax.dot_general(dp, w_ref[...], (((1,), (1,)), ((), ())), preferred_element_type=F32)
        _, vjp = jax.vjp(_norm_mod, x_ref[...], g_ref[...], mod_ref[:, 0:D], mod_ref[:, D:2 * D])
        dx, dgain, dshift, dscale = vjp(dh)
        gx_ref[...] = dx + dx1_ref[...]
        first = pl.program_id(1) == 0
        _acc(dmod_ref, jnp.concatenate([dshift, dscale], axis=-1), first)
        _acc(dg_ref, dgain, first)

    return pl.pallas_call(
        body, name="in_proj_bwd", grid=(B, S // bt),
        in_specs=[_tok(bt, D), _per_batch(6 * D), _full((1, D)), _resident((D, IN_PAD))] + [_tok(bt, w) for w in widths]
        + [_tok(bt, 128), _tok(bt, D)],
        out_specs=[_tok(bt, D), _tok(bt, IN_PAD), _per_batch(2 * D), _per_batch(D)],
        out_shape=[_sds((B, S, D), F32), _sds((B, S, IN_PAD), BF), _sds((B, 1, 2 * D), F32), _sds((B, 1, D), F32)],
        compiler_params=_cparams(dimension_semantics=("parallel", "arbitrary")),
    )(x, mod, norm1_g, w_in, *pieces, dba, dx1)


def _matmul_tn(tag, a, b, bk, bn, bt, col_blocks=False, ex=None):
    T, K = a.shape
    N = b.shape[1]
    nt = T // bt
    if col_blocks:
        assert bk == K
        out_spec = pl.BlockSpec((None, bk, bn), lambda i, j, t: (j, 0, 0))
        out_shape = _sds((N // bn, K, bn), F32)
    else:
        out_spec = pl.BlockSpec((bk, bn), lambda i, j, t: (i, j))
        out_shape = _sds((K, N), F32)

    def body(a_ref, b_ref, o_ref, acc_ref):
        t = pl.program_id(2)

        @pl.when(t == 0)
        def _():
            acc_ref[...] = jnp.zeros_like(acc_ref)
        acc_ref[...] += lax.dot_general(a_ref[...], b_ref[...], (((0,), (0,)), ((), ())), preferred_element_type=F32)

        @pl.when(t == nt - 1)
        def _():
            o_ref[...] = acc_ref[...]

    (out,), landed = _hosted_call(
        body, f"grad_{tag}", (K // bk, N // bn, nt),
        in_specs=[pl.BlockSpec((bt, bk), lambda i, j, t: (t, i)), pl.BlockSpec((bt, bn), lambda i, j, t: (t, j))],
        out_specs=[out_spec], out_shape=[out_shape],
        scratch_shapes=[pltpu.VMEM((bk, bn), F32)],
        semantics=("parallel", "parallel", "arbitrary"), ins=(a, b), ex=ex)
    return out if ex is None else (out, landed)


def _rope_table(positions):
    inv_freq = THETA ** (-jnp.arange(0, ROT, 2, dtype=F32) / ROT)
    rest = jnp.zeros((HD - ROT,), F32)
    freq = jnp.concatenate([inv_freq, inv_freq, rest] * 2)
    sign = jnp.concatenate([-jnp.ones_like(inv_freq), jnp.ones_like(inv_freq), rest] * 2)
    ang = positions.astype(F32)[..., None] * freq
    return jnp.cos(ang), jnp.sin(ang) * sign


def _lane_pad(v, offset, width=128):
    return jnp.zeros((1, width), F32).at[0, offset:offset + v.shape[-1]].set(v.reshape(-1))


def _tile(S, want):
    return min(S, want)


class _Hosted:
    def __init__(self, call):
        self.call = call
        self.outs = None

    def __call__(self, ex):
        self.outs, landed = self.call(ex)
        return landed


def _local_step(x, mod, positions, tgt, norm1_g, w_in_pad, conv_w, q_norm_g, k_norm_g, sinks, a_log, dt_bias,
                dn_norm_g, w_branch, w_out, norm2_g, w_gu, w_dn, dist=None):
    B, S, _ = x.shape
    T = B * S
    cos_t, sin_t = _rope_table(positions)
    qg2 = jnp.concatenate([q_norm_g, q_norm_g], axis=-1)
    kg2 = jnp.concatenate([k_norm_g, k_norm_g], axis=-1)
    alog = _lane_pad(a_log, DN_H)
    dtb = _lane_pad(dt_bias, DN_H)
    conv2 = conv_w.reshape(CONV, CONVW)
    bt = _tile(S, 512)
    bt_mlp = _tile(S, 256)

    q, kv, dn, z, ga, gb, ba, h1 = _in_proj(x, mod, norm1_g, w_in_pad, bt)
    qkv_n = _qk_prep_fwd(q, kv, cos_t, sin_t, qg2, kg2, bt)
    o_attn = _attn_fwd(qkv_n, sinks)
    dqkv, bg, dn_y = _dn_prep(dn, ba, conv2, alog, dtb, bt)
    intra = _Hosted(lambda ex: _dn_intra_fwd(dqkv, bg, ex))
    if dist is None:
        intra(None)
    else:
        f_br, f_out, w_gu, f_dn = _gather_weights("late", [w_branch, w_out, w_gu, w_dn], dist[0], host=intra)
        w_branch, w_out, w_dn = (f.reshape(N_CHIP * f.shape[1], f.shape[2]) for f in (f_br, f_out, f_dn))
    dn_u, dn_w, dn_qd, dn_kd, dn_a, dn_cd, dn_tinv = intra.outs
    (o_raw, states), _ = _dn_rec_fwd(dn_u, dn_w, dn_qd, dn_kd, dn_a, dn_cd)
    x1, o_dn, merged = _mix_fwd(o_attn, o_raw, z, ga, gb, x, mod, dn_norm_g, w_branch, w_out, bt)
    dx1, h2, act, dgu, dyy, loss, dmod2, dnorm2 = _mlp(x1, tgt, mod, norm2_g, w_gu, w_dn, bt_mlp)

    def flat(t):
        return t.reshape(T, t.shape[-1])
    tn = functools.partial(_matmul_tn, bt=_tile(T, 2048))
    g_w_dn = tn("w_down", flat(act), flat(dyy), bk=FFN, bn=D // 2)
    g_w_gu = tn("w_gate_up", flat(h2), flat(dgu), bk=D, bn=2 * FFN // N_CHIP, col_blocks=True)

    mix_b = _Hosted(lambda ex: _mix_bwd(o_attn, o_raw, z, ga, gb, x, mod, dn_norm_g, w_branch, w_out, dx1, bt_mlp, ex))
    rec_b = _Hosted(lambda ex: _dn_rec_bwd(dn_u, dn_w, dn_qd, dn_kd, dn_a, dn_cd, states, mix_b.outs[1], ex))
    intra_b = _Hosted(lambda ex: _dn_intra_bwd(dqkv, bg, dn_tinv, *rec_b.outs, ex))
    if dist is None:
        for host in (mix_b, rec_b, intra_b):
            host(None)
    else:
        g_w_gu, g_w_dn = _reduce_grads(("w_gate_up", "w_down"), [g_w_gu, g_w_dn.reshape(N_CHIP, -1, D)], *dist,
                                       hosts=[mix_b, rec_b, intra_b])
    d_oa, _, dz, dga, dgb, dya, dyd, dout, dgate1, ddn_g = mix_b.outs
    d_dqkv, dbg = intra_b.outs
    g_w_out = tn("w_out", flat(merged), flat(dout), bk=D, bn=D)
    g_w_br = jnp.concatenate([tn("w_branch_attn", flat(o_attn), flat(dya), bk=QW, bn=D),
                              tn("w_branch_dn", flat(o_dn), flat(dyd), bk=DNW, bn=D)], axis=0)
    prep_b = _Hosted(lambda ex: _dn_prep_bwd(dn, dn_y, ba, conv2, alog, dtb, d_dqkv, dbg, bt, ex))
    attn_b = _Hosted(lambda ex: _attn_bwd(qkv_n, sinks, d_oa, ex))
    qk_b = _Hosted(lambda ex: _qk_prep_bwd(q, kv, cos_t, sin_t, qg2, kg2, *attn_b.outs[:2], bt, ex))
    if dist is None:
        for host in (prep_b, attn_b, qk_b):
            host(None)
    else:
        g_w_br, g_w_out = _reduce_grads(("w_branch", "w_out"), [g_w_br.reshape(N_CHIP, -1, D), g_w_out.reshape(N_CHIP, -1, D)],
                                        *dist, hosts=[prep_b, attn_b, qk_b])
    d_dn, dba, dconv, dalog, ddtb = prep_b.outs
    dsk = attn_b.outs[2]
    dq, dkv, dqg2, dkg2 = qk_b.outs
    dqg = dqg2[:, :HD] + dqg2[:, HD:]
    dkg = dkg2[:, :HD] + dkg2[:, HD:]
    grad_x, dproj, dmod1, dnorm1 = _in_proj_bwd(x, mod, norm1_g, w_in_pad, (dq, dkv, d_dn, dz, dga, dgb), dba, dx1, bt)
    g_w_in = _Hosted(lambda ex: (tn("w_in", flat(h1), flat(dproj), bk=D, bn=IN_PAD // 3), ()) if ex is None
                     else tn("w_in", flat(h1), flat(dproj), bk=D, bn=IN_PAD // 3, ex=ex))
    if dist is None:
        g_w_in(None)
        g_w_in = g_w_in.outs

    dmod = jnp.concatenate([dmod1, dgate1, dmod2], axis=-1)
    small = dict(norm1_g=jnp.sum(dnorm1, axis=0), norm2_g=jnp.sum(dnorm2, axis=0), q_norm_g=dqg, k_norm_g=dkg,
                 sinks=dsk, a_log=dalog[:, DN_H:2 * DN_H], dt_bias=ddtb[:, DN_H:2 * DN_H],
                 dn_norm_g=jnp.sum(ddn_g, axis=0), conv_w=dconv)
    return jnp.sum(loss), grad_x, dmod, small, (g_w_in, g_w_br, g_w_out, g_w_gu, g_w_dn)


def _flip(me, f):
    return (me[0] ^ ((f >> 2) & 1), me[1] ^ ((f >> 1) & 1), me[2] ^ (f & 1))


def _comm_call(name, ex):
    n_in, n_out = len(ex.ins), len(ex.out_shapes)

    def body(*refs):
        out_refs, sems = refs[n_in:n_in + n_out], refs[n_in + n_out:]
        cps = _exchange_copies(ex, refs[:n_in], out_refs, sems[0], sems[1])
        for cp in cps:
            cp.start()
        for cp in cps:
            cp.wait_recv()
        if ex.n_forward:
            fwd = _forward_copies(ex, out_refs, sems[2], sems[3])
            for cp in fwd:
                cp.start()
            for cp in fwd:
                cp.wait_recv()
            cps = cps + fwd
        for cp in cps:
            cp.wait_send()

    any_spec = pl.BlockSpec(memory_space=pl.ANY)
    return pl.pallas_call(
        body, name=name, in_specs=[any_spec] * n_in, out_specs=[any_spec] * n_out, out_shape=list(ex.out_shapes),
        scratch_shapes=_exchange_sems(ex),
    )(*ex.ins)


def _by_origin(own, received, index):
    stack = jnp.concatenate([own[None], received], axis=0)
    n = stack.shape[0]
    return jnp.stack([lax.dynamic_index_in_dim(stack, k ^ index, 0, keepdims=False) for k in range(n)])


def _gather_devices(name, arrs, dev, host=None):
    def plan(me, in_refs, out_refs):
        return [(a, o.at[f - 1], _flip(me, f)) for a, o in zip(in_refs, out_refs) for f in range(1, N_DEV)]
    outs = tuple(_sds((N_DEV - 1,) + a.shape, a.dtype) for a in arrs)
    got = (host or functools.partial(_comm_call, name))(_Exchange(tuple(arrs), outs, (N_DEV - 1) * len(arrs), plan))
    return [_by_origin(a, g, dev) for a, g in zip(arrs, got)]


def _gather_chips(name, arrs, chip):
    def plan(me, in_refs, out_refs):
        return [(a, o.at[j], _flip(me, 2 * (j + 1))) for a, o in zip(in_refs, out_refs) for j in range(N_CHIP - 1)]
    outs = tuple(_sds((N_CHIP - 1,) + a.shape, a.dtype) for a in arrs)
    got = _comm_call(name, _Exchange(tuple(arrs), outs, (N_CHIP - 1) * len(arrs), plan))
    return [_by_origin(a, g, chip) for a, g in zip(arrs, got)]


def _swap_cores_ex(arrs):
    def plan(me, in_refs, out_refs):
        return [(g, o, _flip(me, 1)) for g, o in zip(in_refs, out_refs)]
    return _Exchange(tuple(arrs), tuple(_sds(g.shape, g.dtype) for g in arrs), len(arrs), plan)


def _gather_weights(tag, shards, chip, host=None):
    def plan(me, in_refs, out_refs):
        chip_me = 2 * me[0] + me[1]
        remote = []
        for a, o in zip(in_refs, out_refs):
            half = a.shape[0] // 2
            mine = a.at[pl.ds(me[2] * half, half)]
            remote += [(mine, o.at[chip_me, me[2]], _flip(me, 2 * (j + 1))) for j in range(N_CHIP - 1)]
        return remote

    def forward(me, out_refs):
        chip_me = 2 * me[0] + me[1]
        return [(o.at[chip_me ^ (j + 1), me[2]], o.at[chip_me ^ (j + 1), me[2]], _flip(me, 1))
                for o in out_refs for j in range(N_CHIP - 1)]
    run = host or functools.partial(_comm_call, f"weights_{tag}")
    n = (N_CHIP - 1) * len(shards)
    landed = run(_Exchange(tuple(shards), tuple(_sds((N_CHIP, 2, a.shape[0] // 2, a.shape[1]), a.dtype) for a in shards),
                           n, plan, n, forward))
    return [lax.dynamic_update_slice(f.reshape((N_CHIP,) + a.shape), a[None], (chip, 0, 0)) for a, f in zip(shards, landed)]


def _rows(r):
    for br in (512, 352, 256, 128, 64, 32, 16, 8):
        if r % br == 0:
            return br
    raise ValueError(r)


def _pair_add(tag, g, recv, c):
    n, r, cols = g.shape
    half = r // 2
    br = _rows(half)
    nb = half // br

    def body(c_ref, g_ref, r_ref, o_ref):
        o_ref[...] = (g_ref[...] + r_ref[...]).astype(BF)

    return pl.pallas_call(
        body, name=f"pair_add_{tag}",
        grid_spec=pltpu.PrefetchScalarGridSpec(
            num_scalar_prefetch=1, grid=(n, nb),
            in_specs=[pl.BlockSpec((None, br, cols), lambda k, i, c_ref: (k, c_ref[0] * nb + i, 0)),
                      pl.BlockSpec((None, br, cols), lambda k, i, c_ref: (k, i, 0))],
            out_specs=pl.BlockSpec((None, br, cols), lambda k, i, c_ref: (k, i, 0))),
        out_shape=_sds((n, half, cols), BF),
        compiler_params=_cparams(dimension_semantics=("parallel", "parallel")),
    )(c, g, recv)


def _sum_chips(tag, p, q, chip):
    n, r, cols = q.shape
    br = _rows(r)

    def body(chip_ref, p_ref, q_ref, o_ref):
        acc = p_ref[...].astype(F32)
        for k in range(n):
            acc = acc + q_ref[k].astype(F32)
        o_ref[...] = acc

    return pl.pallas_call(
        body, name=f"sum_chips_{tag}",
        grid_spec=pltpu.PrefetchScalarGridSpec(
            num_scalar_prefetch=1, grid=(r // br,),
            in_specs=[pl.BlockSpec((None, br, cols), lambda i, chip_ref: (chip_ref[0], i, 0)),
                      pl.BlockSpec((n, br, cols), lambda i, chip_ref: (0, i, 0))],
            out_specs=pl.BlockSpec((br, cols), lambda i, chip_ref: (i, 0))),
        out_shape=_sds((r, cols), F32),
        compiler_params=_cparams(dimension_semantics=("parallel",)),
    )(chip, p, q)


def _reduce_grads(tags, grads, chip, core, hosts=None):
    core_arr = core.reshape(1).astype(jnp.int32)
    chip_arr = chip.reshape(1).astype(jnp.int32)
    name = "_".join(tags)
    run = hosts or [functools.partial(_comm_call, f"grads_{stage}_{name}") for stage in ("pair", "chips", "swap")]

    def plan_pair(me, in_refs, out_refs):
        remote = []
        for g, o in zip(in_refs, out_refs):
            half = g.shape[1] // 2
            remote += [(g.at[k, pl.ds((1 - me[2]) * half, half)], o.at[k], _flip(me, 1)) for k in range(N_CHIP)]
        return remote
    recv = run[0](_Exchange(tuple(grads), tuple(_sds((N_CHIP, g.shape[1] // 2, g.shape[2]), F32) for g in grads),
                            N_CHIP * len(grads), plan_pair))
    pair = [_pair_add(t, g, r, core_arr) for t, g, r in zip(tags, grads, recv)]

    def plan_chips(me, in_refs, out_refs):
        remote = []
        for p, o in zip(in_refs, out_refs):
            for j in range(N_CHIP - 1):
                peer = _flip(me, 2 * (j + 1))
                remote.append((p.at[2 * peer[0] + peer[1]], o.at[j], peer))
        return remote
    parts = run[1](_Exchange(tuple(pair), tuple(_sds((N_CHIP - 1,) + p.shape[1:], BF) for p in pair),
                             (N_CHIP - 1) * len(pair), plan_chips))
    mine = [_sum_chips(t, p, q, chip_arr) for t, p, q in zip(tags, pair, parts)]
    other = run[2](_swap_cores_ex(mine))
    return list(zip(mine, other))


def _adamw_math(w, g, m, v):
    m = ADAM_B1 * m + (1.0 - ADAM_B1) * g
    v = ADAM_B2 * v + (1.0 - ADAM_B2) * (g * g)
    m_hat = m / (1.0 - ADAM_B1 ** ADAM_STEP)
    v_hat = v / (1.0 - ADAM_B2 ** ADAM_STEP)
    delta = -ADAM_LR * (m_hat / (jnp.sqrt(v_hat) + ADAM_EPS) + ADAM_WD * w)
    return delta, m, v


def _adamw(name, w, g, m, v, ex=None):
    r, cols = w.shape
    br = _rows(r)
    if br * cols * 4 > (1 << 20) and br % 16 == 0:
        br //= 2

    def body(w_ref, g_ref, m_ref, v_ref, d_ref, mo_ref, vo_ref):
        d_ref[...], mo_ref[...], vo_ref[...] = _adamw_math(w_ref[...], g_ref[...], m_ref[...], v_ref[...])

    spec = pl.BlockSpec((br, cols), lambda i: (i, 0))
    outs, landed = _hosted_call(
        body, f"adamw_{name}", (r // br,), in_specs=[spec] * 4, out_specs=[spec] * 3,
        out_shape=[_sds((r, cols), F32)] * 3, scratch_shapes=[], semantics=("parallel",), ins=(w, g, m, v), ex=ex)
    return outs if ex is None else (outs, landed)


def _adamw_halves(name, w, mine, other, m, v, core, ex=None):
    r, cols = w.shape
    br = _rows(r // 2)
    if br * cols * 4 > (1 << 20) and br % 16 == 0:
        br //= 2
    per_half = r // 2 // br

    def body(core_ref, w_ref, a_ref, b_ref, m_ref, v_ref, g_ref, d_ref, mo_ref, vo_ref):
        g = jnp.where(pl.program_id(0) // per_half == core_ref[0], a_ref[...], b_ref[...])
        g_ref[...] = g
        d_ref[...], mo_ref[...], vo_ref[...] = _adamw_math(w_ref[...], g, m_ref[...], v_ref[...])

    whole = pl.BlockSpec((br, cols), lambda i: (i, 0))
    half = pl.BlockSpec((br, cols), lambda i: (i % per_half, 0))
    outs, landed = _hosted_call(
        body, f"adamw_{name}", (r // br,),
        in_specs=[pl.BlockSpec(memory_space=pltpu.SMEM), whole, half, half, whole, whole], out_specs=[whole] * 4,
        out_shape=[_sds((r, cols), F32)] * 4, scratch_shapes=[], semantics=("parallel",),
        ins=(core.reshape(1).astype(jnp.int32), w, mine, other, m, v), ex=ex)
    return outs if ex is None else (outs, landed)


def _ada_fwd(c_all, ada_w, ada_b_cols):
    n = c_all.shape[0]

    def body(c_ref, w_ref, b_ref, o_ref):
        o_ref[...] = _mmx(_silu(c_ref[...]), w_ref[...]) + b_ref[...]

    return pl.pallas_call(
        body, name="ada_fwd", out_shape=_sds((n, ada_w.shape[1]), F32), compiler_params=_cparams(),
    )(c_all, ada_w, ada_b_cols)


def _ada_bwd(c_all, dmod_cols, w, m, v, ex=None):
    n = c_all.shape[0]
    r, cols = w.shape
    br = 128

    def body(c_ref, d_ref, w_ref, m_ref, v_ref, g_ref, dl_ref, mo_ref, vo_ref):
        cond = _silu(c_ref[...])
        g = lax.dot_general(cond, d_ref[...], (((0,), (0,)), ((), ())), precision=lax.Precision.HIGHEST,
                            preferred_element_type=F32)
        g_ref[...] = g
        dl_ref[...], mo_ref[...], vo_ref[...] = _adamw_math(w_ref[...], g, m_ref[...], v_ref[...])

    spec = pl.BlockSpec((br, cols), lambda i: (i, 0))
    outs, landed = _hosted_call(
        body, "ada_bwd", (r // br,),
        in_specs=[pl.BlockSpec((n, br), lambda i: (0, i)), pl.BlockSpec((n, cols), lambda i: (0, 0)), spec, spec, spec],
        out_specs=[spec] * 4, out_shape=[_sds((r, cols), F32)] * 4, scratch_shapes=[], semantics=("parallel",),
        ins=(c_all, dmod_cols, w, m, v), ex=ex)
    return outs if ex is None else (outs, landed)


def _sum_devices(parts):
    n, r, cols = parts.shape

    def body(p_ref, o_ref):
        acc = p_ref[0]
        for k in range(1, n):
            acc = acc + p_ref[k]
        o_ref[...] = acc

    return pl.pallas_call(body, name="sum_devices", out_shape=_sds((r, cols), F32), compiler_params=_cparams())(parts)


SMALL_ROWS = 16
_SMALL_SLOTS = dict(norm1_g=(0, 0, D), norm2_g=(1, 0, D), q_norm_g=(2, 0, HD), k_norm_g=(2, 128, HD), sinks=(2, 256, HEADS),
                    a_log=(2, 384, DN_H), dt_bias=(2, 512, DN_H), dn_norm_g=(2, 640, DN_D))
_CONV_ROW = 4
_ADA_B_ROW = 8


def _pack_small(vals, conv, ada_b):
    def row(pieces):
        out, at = [], 0
        for col, val in pieces:
            out += [jnp.zeros((1, col - at), F32), val.reshape(1, -1)]
            at = col + val.size
        return jnp.concatenate(out + [jnp.zeros((1, CONVW - at), F32)], axis=1)
    rows = {}
    for name, (r, col, n) in _SMALL_SLOTS.items():
        rows.setdefault(r, []).append((col, vals[name]))
    blank = jnp.zeros((1, CONVW), F32)
    top = [row(sorted(rows[r], key=lambda p: p[0])) if r in rows else blank for r in range(_CONV_ROW)]
    conv_rows = jnp.concatenate([conv, jnp.zeros((CONV, CONVW - conv.shape[1]), F32)], axis=1)
    tail = jnp.zeros((SMALL_ROWS - _ADA_B_ROW - 4, CONVW), F32)
    return jnp.concatenate(top + [conv_rows, ada_b.reshape(4, CONVW), tail], axis=0)


def _unpack_small(sheet, conv_cols):
    out = {name: sheet[row, col:col + n].reshape(1, n) for name, (row, col, n) in _SMALL_SLOTS.items()}
    out["conv_w"] = sheet[_CONV_ROW:_CONV_ROW + CONV, 0:conv_cols].reshape(1, CONV, 1, conv_cols)
    out["ada_b"] = sheet[_ADA_B_ROW:_ADA_B_ROW + 4, :].reshape(1, 6 * D)
    return out


def _w_in_segments():
    shard = IN_WIDTH // N_CHIP
    cuts = sorted({0, IN_WIDTH, C_Z, C_Z + 2 * DN_H} | {k * shard for k in range(1, N_CHIP)})
    segs = []
    for a, b in zip(cuts[:-1], cuts[1:]):
        k = a // shard
        pad = a if a < C_Z else (C_BA + a - C_Z if a < C_Z + 2 * DN_H else a - 2 * DN_H)
        segs.append((k, a - k * shard, b - k * shard, pad))
    return segs


def _pad_w_in(f):
    parts = [f[k][:, lo:hi] for k, lo, hi, _ in sorted(_w_in_segments(), key=lambda s: s[3])]
    return jnp.concatenate(parts + [jnp.zeros((f.shape[1], IN_PAD - IN_WIDTH), f.dtype)], axis=1)


def _unpad_w_in(g):
    return jnp.stack([jnp.concatenate([g[:, pad:pad + hi - lo] for kk, lo, hi, pad in _w_in_segments() if kk == k], axis=1)
                      for k in range(N_CHIP)])


def _blocks_to_cols(f):
    return f.transpose(1, 0, 2).reshape(f.shape[1], N_CHIP * f.shape[2])


def kernel(x, c, positions, ada_w, ada_b, norm1_g, w_in, conv_w, q_norm_g, k_norm_g, sinks, a_log, dt_bias, dn_norm_g, w_branch, w_out, norm2_g, w_gate_up, w_down, loss_target, m_ada_w, m_ada_b, m_norm1_g, m_w_in, m_conv_w, m_q_norm_g, m_k_norm_g, m_sinks, m_a_log, m_dt_bias, m_dn_norm_g, m_w_branch, m_w_out, m_norm2_g, m_w_gate_up, m_w_down, v_ada_w, v_ada_b, v_norm1_g, v_w_in, v_conv_w, v_q_norm_g, v_k_norm_g, v_sinks, v_a_log, v_dt_bias, v_dn_norm_g, v_w_branch, v_w_out, v_norm2_g, v_w_gate_up, v_w_down):
    ix, iy, ic = lax.axis_index("x"), lax.axis_index("y"), lax.axis_index("c")
    dev = 4 * ix + 2 * iy + ic
    chip = 2 * ix + iy
    n_seq = x.shape[0]
    conv_cols = conv_w.shape[-1]

    c_all, conv_all = _gather_devices("gather_cond", [c, conv_w.reshape(CONV, conv_cols)], dev)
    c_all = c_all.reshape(N_DEV * n_seq, D)
    ada_cols = ada_w.shape[-1]
    ada_b_cols = lax.dynamic_slice(ada_b, (0, chip * ada_cols), (1, ada_cols))
    mod_cols = _ada_fwd(c_all, ada_w[0], ada_b_cols)
    (mod_blocks,) = _gather_chips("gather_mod", [mod_cols], chip)
    mod_all = _blocks_to_cols(mod_blocks)
    mod = lax.dynamic_slice(mod_all, (dev * n_seq, 0), (n_seq, 6 * D)).reshape(n_seq, 1, 6 * D)
    conv_full = _blocks_to_cols(conv_all[0::2])

    (f_in,) = _gather_weights("w_in", [w_in[0].astype(BF)], chip)
    w_in_pad = _pad_w_in(f_in)

    loss, grad_x, dmod, small, (w_in_grad, r_br, r_out, r_gu, r_dn) = _local_step(
        x, mod, positions, loss_target, norm1_g, w_in_pad, conv_full.reshape(CONV, 1, CONVW), q_norm_g, k_norm_g, sinks,
        a_log, dt_bias, dn_norm_g, w_branch[0].astype(BF), w_out[0].astype(BF), norm2_g, w_gate_up[0].astype(BF),
        w_down[0].astype(BF), dist=(chip, ic))
    loss = lax.psum(loss, ("x", "y", "c"))

    part = _pack_small(small, small["conv_w"], jnp.sum(dmod, axis=(0, 1)).reshape(1, 6 * D))
    dmod_all, parts = _gather_devices("gather_small", [dmod.reshape(n_seq, 6 * D), part], dev, host=w_in_grad)
    dmod_all = dmod_all.reshape(N_DEV * n_seq, 6 * D)
    dmod_cols = lax.dynamic_slice(dmod_all, (0, chip * ada_cols), (N_DEV * n_seq, ada_cols))

    up_gu = _Hosted(lambda ex: _adamw_halves("w_gate_up", w_gate_up[0], *r_gu, m_w_gate_up[0], v_w_gate_up[0], ic, ex))
    up_ada = _Hosted(lambda ex: _ada_bwd(c_all, dmod_cols, ada_w[0], m_ada_w[0], v_ada_w[0], ex))
    up_dn = _Hosted(lambda ex: _adamw_halves("w_down", w_down[0], *r_dn, m_w_down[0], v_w_down[0], ic, ex))
    (r_in,) = _reduce_grads(("w_in",), [_unpad_w_in(w_in_grad.outs)], chip, ic, hosts=[up_gu, up_ada, up_dn])
    ada = up_ada.outs
    big = {"w_gate_up": tuple(up_gu.outs), "w_down": tuple(up_dn.outs)}
    for name, w, g, m, v in (("w_in", w_in, r_in, m_w_in, v_w_in), ("w_branch", w_branch, r_br, m_w_branch, v_w_branch),
                             ("w_out", w_out, r_out, m_w_out, v_w_out)):
        big[name] = tuple(_adamw_halves(name, w[0], *g, m[0], v[0], ic))
    g_small = _unpack_small(_sum_devices(parts), CONVW)
    g_conv = lax.dynamic_slice(g_small["conv_w"].reshape(CONV, CONVW), (0, chip * conv_cols), (CONV, conv_cols))
    g_small["conv_w"] = g_conv.reshape(1, CONV, 1, conv_cols)

    given = dict(norm1_g=(norm1_g, m_norm1_g, v_norm1_g), norm2_g=(norm2_g, m_norm2_g, v_norm2_g),
                 q_norm_g=(q_norm_g, m_q_norm_g, v_q_norm_g), k_norm_g=(k_norm_g, m_k_norm_g, v_k_norm_g),
                 sinks=(sinks, m_sinks, v_sinks), a_log=(a_log, m_a_log, v_a_log), dt_bias=(dt_bias, m_dt_bias, v_dt_bias),
                 dn_norm_g=(dn_norm_g, m_dn_norm_g, v_dn_norm_g))
    sheets = [_pack_small({k: t[j] for k, t in given.items()}, cw.reshape(CONV, conv_cols), ab)
              for j, (cw, ab) in enumerate(((conv_w, ada_b), (m_conv_w, m_ada_b), (v_conv_w, v_ada_b)))]
    g_local = _pack_small(g_small, g_conv, g_small["ada_b"])
    upd = [_unpack_small(s, conv_cols) for s in _adamw("small", sheets[0], g_local, sheets[1], sheets[2])]

    names = ["ada_w", "ada_b", "norm1_g", "w_in", "conv_w", "q_norm_g", "k_norm_g", "sinks", "a_log", "dt_bias", "dn_norm_g",
             "w_branch", "w_out", "norm2_g", "w_gate_up", "w_down"]

    def leaf(name, j):
        if name == "ada_w":
            return ada[j][None]
        if name in big:
            return big[name][j][None]
        return g_small[name] if j == 0 else upd[j - 1][name]

    return (loss, grad_x) + tuple(leaf(n, j) for j in range(4) for n in names)
```

```python
import functools
from typing import Callable, NamedTuple

import jax
import jax.numpy as jnp
import numpy as np
from jax import lax
from jax.experimental import pallas as pl
from jax.experimental.pallas import tpu as pltpu

F32 = jnp.float32
BF = jnp.bfloat16

D = 1024
HEADS = 8
KV_HEADS = 2
GROUP = 4
HD = 64
BLK = 128
ROT = 16
THETA = 500000.0
QW = 512
KVW = 128
DN_H = 4
DN_D = 128
CONV = 4
CHUNK = 64
DNW = 512
CONVW = 1536
FFN = 2816
EPS = 1e-6
IN_WIDTH = 4872
IN_PAD = 4992
C_KV = 512
C_DN = 768
C_Z = 2304
C_GA = 2816
C_GB = 3840
C_BA = 4864
NEG = -1e30
N_DEV = 8
N_CHIP = 4

ADAM_LR = 0.001
ADAM_B1 = 0.9
ADAM_B2 = 0.999
ADAM_EPS = 1e-08
ADAM_WD = 0.01
ADAM_STEP = 10

VMEM_LIMIT = 60 * 1024 * 1024


def _cparams(**kw):
    return pltpu.CompilerParams(vmem_limit_bytes=VMEM_LIMIT, **kw)


def _dg(a, b, ca, cb):
    return lax.dot_general(a.astype(BF), b.astype(BF), (((ca,), (cb,)), ((), ())),
                           preferred_element_type=F32)


@jax.custom_vjp
def _mm(a, b):
    return _dg(a, b, 1, 0)


def _mm_fwd(a, b):
    return _dg(a, b, 1, 0), (a, b)


def _mm_bwd(res, dy):
    a, b = res
    return _dg(dy, b, 1, 1).astype(a.dtype), _dg(a, dy, 0, 0).astype(b.dtype)


_mm.defvjp(_mm_fwd, _mm_bwd)


@jax.custom_vjp
def _mm_nt(a, b):
    return _dg(a, b, 1, 1)


def _mm_nt_fwd(a, b):
    return _dg(a, b, 1, 1), (a, b)


def _mm_nt_bwd(res, dy):
    a, b = res
    return _dg(dy, b, 1, 0).astype(a.dtype), _dg(dy, a, 0, 0).astype(b.dtype)


_mm_nt.defvjp(_mm_nt_fwd, _mm_nt_bwd)


@jax.custom_vjp
def _mm_tn(a, b):
    return _dg(a, b, 0, 0)


def _mm_tn_fwd(a, b):
    return _dg(a, b, 0, 0), (a, b)


def _mm_tn_bwd(res, dy):
    a, b = res
    return _dg(b, dy, 1, 1).astype(a.dtype), _dg(a, dy, 1, 0).astype(b.dtype)


_mm_tn.defvjp(_mm_tn_fwd, _mm_tn_bwd)


def _mmx(a, b):
    return jnp.dot(a, b, precision=lax.Precision.HIGHEST, preferred_element_type=F32)


def _mmx_nt(a, b):
    return lax.dot_general(a, b, (((1,), (1,)), ((), ())), precision=lax.Precision.HIGHEST,
                           preferred_element_type=F32)


def _iota(shape, dim):
    return lax.broadcasted_iota(jnp.int32, shape, dim)


def _sigmoid(x):
    return lax.logistic(x)


def _silu(x):
    return x * _sigmoid(x)


def _softplus(x):
    return jnp.maximum(x, 0.0) + jnp.log(1.0 + jnp.exp(-jnp.abs(x)))


def _rms(x, gain):
    return x * lax.rsqrt(jnp.mean(x * x, axis=-1, keepdims=True) + EPS) * gain


def _norm_mod(x, gain, shift, scale):
    return _rms(x, gain) * (1.0 + scale) + shift


def _split(a):
    hi = a.astype(BF)
    return hi, (a - hi.astype(F32)).astype(BF)


def _dg2(a, c, ca, cb):
    ah, al = _split(a)
    c = c.astype(BF)

    def dg(x):
        return lax.dot_general(x, c, (((ca,), (cb,)), ((), ())), preferred_element_type=F32)
    return dg(ah) + dg(al)


@jax.custom_vjp
def _mmc(a, c):
    return _dg2(a, c, 1, 0)


def _mmc_fwd(a, c):
    return _dg2(a, c, 1, 0), c


def _mmc_bwd(c, dy):
    return _dg2(dy, c, 1, 1), jnp.zeros_like(c)


_mmc.defvjp(_mmc_fwd, _mmc_bwd)


def _qk_prep(slabs, gain, cos, sin):
    r = _iota((2 * HD, 2 * HD), 0)
    c = _iota((2 * HD, 2 * HD), 1)
    seg = jnp.where(r // HD == c // HD, 1.0 / HD, 0.0).astype(F32)
    half = ROT // 2
    cd = c % HD
    pair = jnp.where(((cd < half) & (r == c + half)) | ((cd >= half) & (cd < ROT) & (r == c - half)), 1.0, 0.0).astype(F32)
    out = []
    for x in slabs:
        y = x * lax.rsqrt(_mmc(x * x, seg) + EPS) * gain
        out.append(y * cos + _mmc(y, pair) * sin)
    return out


def _attn_block(qs, kc, kp, vc, vp, sinks, has_prev):
    rows = GROUP * BLK
    qi = _iota((rows, 2 * BLK), 0) % BLK + BLK
    kj = _iota((rows, 2 * BLK), 1)
    dist = qi - kj
    valid = (dist >= 0) & (dist < BLK) & ((kj >= BLK) | has_prev)
    grp = _iota((rows, HEADS), 0) // BLK
    col = _iota((rows, HEADS), 1)

    outs = []
    for h in range(KV_HEADS):
        q = jnp.concatenate([qs[h * GROUP + g] for g in range(GROUP)], axis=0)
        k = jnp.concatenate([kp[h], kc[h]], axis=0)
        v = jnp.concatenate([vp[h], vc[h]], axis=0)
        s = _mm_nt(q, k) * (HD ** -0.5)
        s = jnp.where(valid, s, NEG)
        sink = jnp.sum(jnp.where(col == h * GROUP + grp, sinks, 0.0), axis=-1, keepdims=True)
        m = lax.stop_gradient(jnp.maximum(jnp.max(s, axis=-1, keepdims=True), sink))
        p = jnp.exp(s - m)
        probs = p / (jnp.sum(p, axis=-1, keepdims=True) + jnp.exp(sink - m))
        o = _mm(probs, v)
        outs += [o[g * BLK:(g + 1) * BLK] for g in range(GROUP)]
    return outs


def _dn_act(y, normalize):
    s = _silu(y)
    return s * lax.rsqrt(jnp.sum(s * s, axis=-1, keepdims=True) + EPS) if normalize else s


def _dn_gates(ba, alog, dtb):
    lane = _iota(ba.shape, 1)
    beta = _sigmoid(ba)
    g = -jnp.exp(alog) * _softplus(ba + dtb)
    return jnp.where(lane < DN_H, beta, jnp.where(lane < 2 * DN_H, g, 0.0))


def _bdg(a, b, ca, cb):
    return lax.dot_general(a.astype(BF), b.astype(BF), (((ca,), (cb,)), ((0,), (0,))), preferred_element_type=F32)


@jax.custom_vjp
def _bmm(a, b):
    return _bdg(a, b, 2, 1)


def _bmm_fwd(a, b):
    return _bdg(a, b, 2, 1), (a, b)


def _bmm_bwd(res, dy):
    a, b = res
    return _bdg(dy, b, 2, 2), _bdg(a, dy, 1, 1)


_bmm.defvjp(_bmm_fwd, _bmm_bwd)


@jax.custom_vjp
def _bmm_nt(a, b):
    return _bdg(a, b, 2, 2)


def _bmm_nt_fwd(a, b):
    return _bdg(a, b, 2, 2), (a, b)


def _bmm_nt_bwd(res, dy):
    a, b = res
    return _bdg(dy, b, 2, 1), _bdg(dy, a, 1, 1)


_bmm_nt.defvjp(_bmm_nt_fwd, _bmm_nt_bwd)


def _bmmx(a, b):
    return lax.dot_general(a, b, (((2,), (1,)), ((0,), (0,))), precision=lax.Precision.HIGHEST,
                           preferred_element_type=F32)


def _tri_times(x, transpose):
    C = x.shape[1]
    tri = jnp.broadcast_to((_iota((C, C), 0) >= _iota((C, C), 1)).astype(BF)[None], (x.shape[0], C, C))
    hi = x.astype(BF)
    mid, lo = _split(x - hi.astype(F32))
    dims = (((1,) if transpose else (2,), (1,)), ((0,), (0,)))
    return sum(lax.dot_general(tri, part, dims, preferred_element_type=F32) for part in (hi, mid, lo))


@jax.custom_vjp
def _running_sum(x):
    return _tri_times(x, False)


def _running_sum_fwd(x):
    return _tri_times(x, False), None


def _running_sum_bwd(_, dy):
    return (_tri_times(dy, True),)


_running_sum.defvjp(_running_sum_fwd, _running_sum_bwd)


def _neumann_inverse(lmat):
    C = CHUNK
    eye = jnp.where(_iota((C, C), 0) == _iota((C, C), 1), 1.0, 0.0).astype(F32)[None]
    a = -lmat
    tinv = eye + a
    pw = _bmmx(a, a)
    for _ in range(4):
        both = _bmmx(jnp.concatenate([pw, tinv], axis=1), pw)
        pw, tinv = both[:, :C], tinv + both[:, C:]
    return tinv + _bmmx(tinv, pw)


def _inverse_bwd(tinv, d_tinv):
    x = lax.dot_general(d_tinv, tinv, (((2,), (2,)), ((0,), (0,))), precision=lax.Precision.HIGHEST,
                        preferred_element_type=F32)
    return -lax.dot_general(tinv, x, (((1,), (1,)), ((0,), (0,))), precision=lax.Precision.HIGHEST,
                            preferred_element_type=F32)


@jax.custom_vjp
def _tri_inverse(lmat):
    return _neumann_inverse(lmat)


def _tri_inverse_fwd(lmat):
    tinv = _neumann_inverse(lmat)
    return tinv, tinv


def _tri_inverse_bwd(tinv, d_tinv):
    return (_inverse_bwd(tinv, d_tinv),)


_tri_inverse.defvjp(_tri_inverse_fwd, _tri_inverse_bwd)


@jax.custom_vjp
def _tri_inverse_known(lmat, tinv):
    return tinv


def _tri_inverse_known_fwd(lmat, tinv):
    return tinv, tinv


def _tri_inverse_known_bwd(tinv, d_tinv):
    return _inverse_bwd(tinv, d_tinv), jnp.zeros_like(tinv)


_tri_inverse_known.defvjp(_tri_inverse_known_fwd, _tri_inverse_known_bwd)


def _dn_intra(q, k, v, bg, tinv=None):
    C = CHUNK
    G = bg.shape[0]
    r = _iota((C, C), 0)
    c = _iota((C, C), 1)
    incl = (r >= c)[None]
    strict = (r > c)[None]
    eye = jnp.where(r == c, 1.0, 0.0).astype(F32)[None]
    gc_all = _running_sum(bg)
    lane = _iota((C, DN_D), 1)

    def per_head(x, offset):
        return jnp.concatenate([jnp.sum(jnp.where(lane == offset + h, x[g], 0.0), axis=-1, keepdims=True)[None]
                                for g in range(G) for h in range(DN_H)], axis=0)
    beta = per_head(bg, 0)
    gcol = per_head(gc_all, DN_H)
    grow = jnp.sum(eye * gcol, axis=1, keepdims=True)
    glast = jnp.sum(jnp.where(_iota((1, C, 1), 1) == C - 1, gcol, 0.0), axis=1, keepdims=True)
    decay = jnp.exp(jnp.where(incl, gcol - grow, NEG))
    q = q * (DN_D ** -0.5)
    kb = k * beta
    lmat = jnp.where(strict, _bmm_nt(kb, k) * decay, 0.0)
    tinv = _tri_inverse(lmat) if tinv is None else _tri_inverse_known(lmat, tinv)
    egc = jnp.exp(gcol)
    u = _bmm(tinv, v * beta)
    w = _bmm(tinv, kb * egc)
    a = _bmm_nt(q, k) * decay
    return u, w, q * egc, k * jnp.exp(glast - gcol), a, jnp.exp(glast), tinv


@jax.custom_vjp
def _bmm_tn(a, b):
    return _bdg(a, b, 1, 1)


def _bmm_tn_fwd(a, b):
    return _bdg(a, b, 1, 1), (a, b)


def _bmm_tn_bwd(res, dy):
    a, b = res
    return _bdg(b, dy, 2, 2), _bdg(a, dy, 2, 1)


_bmm_tn.defvjp(_bmm_tn_fwd, _bmm_tn_bwd)


def _dn_rec(state, u, w, qd, kd, a, cd):
    v_new = u - _bmm(w, state)
    out = _bmm(qd, state) + _bmm(a, v_new)
    return state * cd + _bmm_tn(kd, v_new), out


def _mix_tile(o_attn, o_raw, zs, ga, gb, x, gate1, dn_g, wb_a, wb_d, w_out, p_ya, p_yd, p_out):
    o_dn = jnp.concatenate([_rms(o_raw[h], dn_g) * _silu(zs[h]) for h in range(DN_H)], axis=-1)
    y_a = _mm(o_attn, wb_a) + p_ya
    y_d = _mm(o_dn, wb_d) + p_yd
    merged = _sigmoid(ga) * y_a + _sigmoid(gb) * y_d
    out = _mm(merged, w_out) + p_out
    return x + gate1 * out, o_dn, merged


def _mlp_tile(x1, gain, shift, scale, gate2, w_gu, w_dn, tgt, p_gu, p_yy):
    h2 = _norm_mod(x1, gain, shift, scale)
    gu = jnp.concatenate([_mm(h2, w) for w in w_gu], axis=-1) + p_gu
    act = _silu(gu[:, :FFN]) * gu[:, FFN:]
    yy = _mm(act, w_dn) + p_yy
    y = x1 + gate2 * yy
    err = y - tgt
    return 0.5 * jnp.sum(err * err) * (1.0 / D), (h2, act)


def _tok(bt, f):
    return pl.BlockSpec((None, bt, f), lambda b, i: (b, i, 0))


def _full(shape):
    return pl.BlockSpec(shape, lambda b, i: (0,) * len(shape))


def _resident(shape):
    return pl.BlockSpec(shape, lambda b, i: (0,) * len(shape), pipeline_mode=pl.Buffered(1))


def _per_batch(f):
    return pl.BlockSpec((None, 1, f), lambda b, i: (b, 0, 0))


def _sds(shape, dtype):
    return jax.ShapeDtypeStruct(shape, dtype)


class _Exchange(NamedTuple):
    ins: tuple
    out_shapes: tuple
    n_remote: int
    plan: Callable
    n_forward: int = 0
    forward: Callable = None


def _remote_copies(remote, send_sems, recv_sems):
    return [pltpu.make_async_remote_copy(src_ref=src, dst_ref=dst, send_sem=send_sems.at[i], recv_sem=recv_sems.at[i],
                                         device_id=peer, device_id_type=pl.DeviceIdType.MESH)
            for i, (src, dst, peer) in enumerate(remote)]


def _exchange_copies(ex, in_refs, out_refs, send_sems, recv_sems):
    remote = ex.plan((lax.axis_index("x"), lax.axis_index("y"), lax.axis_index("c")), in_refs, out_refs)
    assert len(remote) == ex.n_remote
    return _remote_copies(remote, send_sems, recv_sems)


def _forward_copies(ex, out_refs, send_sems, recv_sems):
    remote = ex.forward((lax.axis_index("x"), lax.axis_index("y"), lax.axis_index("c")), out_refs)
    assert len(remote) == ex.n_forward
    return _remote_copies(remote, send_sems, recv_sems)


def _exchange_sems(ex):
    sems = [pltpu.SemaphoreType.DMA((ex.n_remote,)), pltpu.SemaphoreType.DMA((ex.n_remote,))]
    if ex.n_forward:
        sems += [pltpu.SemaphoreType.DMA((ex.n_forward,)), pltpu.SemaphoreType.DMA((ex.n_forward,))]
    return sems


def _hosted_call(body, name, grid, in_specs, out_specs, out_shape, scratch_shapes, semantics, ins, ex=None):
    if ex is None:
        outs = pl.pallas_call(body, name=name, grid=grid, in_specs=in_specs, out_specs=out_specs, out_shape=out_shape,
                              scratch_shapes=scratch_shapes,
                              compiler_params=_cparams(dimension_semantics=semantics))(*ins)
        return outs, ()
    n_in, n_out, n_scr = len(ins), len(out_shape), len(scratch_shapes)
    c_in, c_out = len(ex.ins), len(ex.out_shapes)
    steps = 1
    for g in grid:
        steps *= g

    def wrapped(*refs):
        a, b, c, d = n_in, n_in + c_in, n_in + c_in + n_out, n_in + c_in + n_out + c_out
        scratch, sems = refs[d:d + n_scr], refs[d + n_scr:]
        step = 0
        for axis, g in enumerate(grid):
            step = step * g + pl.program_id(axis)

        def first_phase():
            return _exchange_copies(ex, refs[a:b], refs[c:d], sems[0], sems[1])

        @pl.when(step == 0)
        def _():
            for cp in first_phase():
                cp.start()
        body(*refs[:a], *refs[b:c], *scratch)

        if ex.n_forward:
            @pl.when(step == (3 * steps) // 4)
            def _():
                for cp in first_phase():
                    cp.wait_recv()
                for cp in _forward_copies(ex, refs[c:d], sems[2], sems[3]):
                    cp.start()

        @pl.when(step == steps - 1)
        def _():
            cps = first_phase()
            if ex.n_forward:
                fwd = _forward_copies(ex, refs[c:d], sems[2], sems[3])
                for cp in fwd:
                    cp.wait_recv()
                for cp in cps + fwd:
                    cp.wait_send()
            else:
                for cp in cps:
                    cp.wait_recv()
                for cp in cps:
                    cp.wait_send()

    any_spec = pl.BlockSpec(memory_space=pl.ANY)
    res = pl.pallas_call(
        wrapped, name=name, grid=grid, in_specs=list(in_specs) + [any_spec] * c_in,
        out_specs=list(out_specs) + [any_spec] * c_out, out_shape=list(out_shape) + list(ex.out_shapes),
        scratch_shapes=list(scratch_shapes) + _exchange_sems(ex),
        compiler_params=_cparams(dimension_semantics=("arbitrary",) * len(grid)),
    )(*ins, *ex.ins)
    return res[:n_out], res[n_out:]


def _acc(ref, val, first):
    @pl.when(first)
    def _():
        ref[...] = val

    @pl.when(jnp.logical_not(first))
    def _():
        ref[...] += val


def _in_proj(x, mod, norm1_g, w_in, bt):
    B, S, _ = x.shape

    def body(x_ref, mod_ref, g_ref, w_ref, q_ref, kv_ref, dn_ref, z_ref, ga_ref, gb_ref, ba_ref, h_ref):
        h = _norm_mod(x_ref[...], g_ref[...], mod_ref[:, 0:D], mod_ref[:, D:2 * D]).astype(BF)
        h_ref[...] = h

        def proj(c0, c1):
            return jnp.dot(h, w_ref[:, c0:c1], preferred_element_type=F32)
        q_ref[...] = proj(0, C_KV).astype(BF)
        kv_ref[...] = proj(C_KV, C_DN).astype(BF)
        dn_ref[...] = proj(C_DN, C_Z).astype(BF)
        z_ref[...] = proj(C_Z, C_GA).astype(BF)
        ga_ref[...] = proj(C_GA, C_GB).astype(BF)
        gb_ref[...] = proj(C_GB, C_BA).astype(BF)
        ba_ref[...] = proj(C_BA, IN_PAD)

    widths = (QW, 2 * KVW, CONVW, DNW, D, D)
    return pl.pallas_call(
        body, name="in_proj", grid=(B, S // bt),
        in_specs=[_tok(bt, D), _per_batch(6 * D), _full((1, D)), _resident((D, IN_PAD))],
        out_specs=[_tok(bt, w) for w in widths] + [_tok(bt, 128), _tok(bt, D)],
        out_shape=[_sds((B, S, w), BF) for w in widths] + [_sds((B, S, 128), F32), _sds((B, S, D), BF)],
        compiler_params=_cparams(dimension_semantics=("parallel", "parallel")),
    )(x, mod, norm1_g, w_in)


def _prev_blk(bt, f):
    return pl.BlockSpec((None, bt, f), lambda b, i: (b, jnp.maximum(i - 1, 0), 0))


QKV = QW + 2 * KVW


def _qk_slabs(q_ref, kv_ref):
    return ([q_ref[:, j * 2 * HD:(j + 1) * 2 * HD].astype(F32) for j in range(QW // (2 * HD))],
            [kv_ref[:, 0:KVW].astype(F32)])


def _qk_prep_fwd(q, kv, cos, sin, qg, kg, bt):
    B, S, _ = q.shape

    def body(q_ref, kv_ref, cos_ref, sin_ref, qg_ref, kg_ref, o_ref):
        qs, ks = _qk_slabs(q_ref, kv_ref)
        qn = _qk_prep(qs, qg_ref[...], cos_ref[...], sin_ref[...])
        kn = _qk_prep(ks, kg_ref[...], cos_ref[...], sin_ref[...])
        for j, t in enumerate(qn + kn):
            o_ref[:, j * 2 * HD:(j + 1) * 2 * HD] = t.astype(BF)
        o_ref[:, QW + KVW:QKV] = kv_ref[:, KVW:2 * KVW]

    return pl.pallas_call(
        body, name="qk_prep_fwd", grid=(B, S // bt),
        in_specs=[_tok(bt, QW), _tok(bt, 2 * KVW), _tok(bt, 2 * HD), _tok(bt, 2 * HD), _full((1, 2 * HD)), _full((1, 2 * HD))],
        out_specs=_tok(bt, QKV), out_shape=_sds((B, S, QKV), BF),
        compiler_params=_cparams(dimension_semantics=("parallel", "parallel")),
    )(q, kv, cos, sin, qg, kg)


def _qk_prep_bwd(q, kv, cos, sin, qg, kg, dqn, dkvn, bt, ex=None):
    B, S, _ = q.shape

    def body(q_ref, kv_ref, cos_ref, sin_ref, qg_ref, kg_ref, dqn_ref, dkvn_ref, dq_ref, dkv_ref, dqg_ref, dkg_ref):
        qs, ks = _qk_slabs(q_ref, kv_ref)
        cos, sin = cos_ref[...], sin_ref[...]

        def f(qs, ks, qg, kg):
            return _qk_prep(qs, qg, cos, sin), _qk_prep(ks, kg, cos, sin)
        _, vjp = jax.vjp(f, qs, ks, qg_ref[...], kg_ref[...])
        n_q = len(qs)
        d_q = [dqn_ref[:, j * 2 * HD:(j + 1) * 2 * HD].astype(F32) for j in range(n_q)]
        d_k = [dkvn_ref[:, 0:KVW].astype(F32)]
        dqs, dks, dqg, dkg = vjp((d_q, d_k))
        for j in range(n_q):
            dq_ref[:, j * 2 * HD:(j + 1) * 2 * HD] = dqs[j].astype(BF)
        dkv_ref[:, 0:KVW] = dks[0].astype(BF)
        dkv_ref[:, KVW:2 * KVW] = dkvn_ref[:, KVW:2 * KVW]
        first = (pl.program_id(0) == 0) & (pl.program_id(1) == 0)
        _acc(dqg_ref, dqg, first)
        _acc(dkg_ref, dkg, first)

    return _hosted_call(
        body, "qk_prep_bwd", (B, S // bt),
        in_specs=[_tok(bt, QW), _tok(bt, 2 * KVW), _tok(bt, 2 * HD), _tok(bt, 2 * HD), _full((1, 2 * HD)), _full((1, 2 * HD)),
                  _tok(bt, QW), _tok(bt, 2 * KVW)],
        out_specs=[_tok(bt, QW), _tok(bt, 2 * KVW), _full((1, 2 * HD)), _full((1, 2 * HD))],
        out_shape=[_sds((B, S, QW), BF), _sds((B, S, 2 * KVW), BF), _sds((1, 2 * HD), F32), _sds((1, 2 * HD), F32)],
        scratch_shapes=[], semantics=("arbitrary", "arbitrary"), ins=(q, kv, cos, sin, qg, kg, dqn, dkvn), ex=ex)


def _attn_load(qkv_ref, kvp_ref):
    qs = [qkv_ref[:, h * HD:(h + 1) * HD].astype(F32) for h in range(HEADS)]
    kc = [qkv_ref[:, QW + h * HD:QW + (h + 1) * HD].astype(F32) for h in range(KV_HEADS)]
    vc = [qkv_ref[:, QW + KVW + h * HD:QW + KVW + (h + 1) * HD].astype(F32) for h in range(KV_HEADS)]
    kp = [kvp_ref[:, h * HD:(h + 1) * HD].astype(F32) for h in range(KV_HEADS)]
    vp = [kvp_ref[:, KVW + h * HD:KVW + (h + 1) * HD].astype(F32) for h in range(KV_HEADS)]
    return qs, kc, kp, vc, vp


def _kv_prev_spec(index):
    return pl.BlockSpec((None, BLK, 2 * KVW), lambda b, i: (b, index(i), QW // (2 * KVW)))


def _attn_fwd(qkv, sinks):
    B, S, _ = qkv.shape

    def body(qkv_ref, kvp_ref, sk_ref, o_ref):
        qs, kc, kp, vc, vp = _attn_load(qkv_ref, kvp_ref)
        outs = _attn_block(qs, kc, kp, vc, vp, sk_ref[...], pl.program_id(1) > 0)
        for h in range(HEADS):
            o_ref[:, h * HD:(h + 1) * HD] = outs[h].astype(BF)

    return pl.pallas_call(
        body, name="attn_fwd", grid=(B, S // BLK),
        in_specs=[_tok(BLK, QKV), _kv_prev_spec(lambda i: jnp.maximum(i - 1, 0)), _full((1, HEADS))],
        out_specs=_tok(BLK, QW), out_shape=_sds((B, S, QW), BF),
        compiler_params=_cparams(dimension_semantics=("parallel", "parallel")),
    )(qkv, qkv, sinks)


def _halo_spec(bt):
    return pl.BlockSpec((None, 8, CONVW), lambda b, i: (b, jnp.maximum(i * (bt // 8) - 1, 0), 0))


def _dn_prep(dn, ba, conv_w, alog, dtb, bt):
    B, S, _ = dn.shape

    strip_rows = min(bt, 64)

    def body(x_ref, halo_ref, ba_ref, cw_ref, al_ref, dt_ref, qkv_ref, bg_ref, y_ref, xe_ref):
        xe_ref[0:8, :] = jnp.where(pl.program_id(1) == 0, 0.0, halo_ref[...].astype(F32))
        xe_ref[8:bt + 8, :] = x_ref[...].astype(F32)

        def strip(k, carry):
            r0 = pl.multiple_of(k * strip_rows, strip_rows)
            rows = pl.ds(r0, strip_rows)
            for j in range(3 * DN_H):
                cols = slice(j * DN_D, (j + 1) * DN_D)
                window = xe_ref[pl.ds(r0, strip_rows + 8), cols]
                y = cw_ref[0:1, cols] * window[5:strip_rows + 5]
                for t in range(1, CONV):
                    y = y + cw_ref[t:t + 1, cols] * window[5 + t:strip_rows + 5 + t]
                y_ref[rows, cols] = y.astype(BF)
                qkv_ref[rows, cols] = _dn_act(y, j < 2 * DN_H)
            bg_ref[rows, :] = _dn_gates(ba_ref[rows, :], al_ref[...], dt_ref[...])
            return carry
        lax.fori_loop(0, bt // strip_rows, strip, 0)

    return pl.pallas_call(
        body, name="dn_prep", grid=(B, S // bt),
        in_specs=[_tok(bt, CONVW), _halo_spec(bt), _tok(bt, 128), _full((CONV, CONVW)), _full((1, 128)), _full((1, 128))],
        out_specs=[_tok(bt, CONVW), _tok(bt, 128), _tok(bt, CONVW)],
        out_shape=[_sds((B, S, CONVW), F32), _sds((B, S, 128), F32), _sds((B, S, CONVW), BF)],
        scratch_shapes=[pltpu.VMEM((bt + 8, CONVW), F32)],
        compiler_params=_cparams(dimension_semantics=("parallel", "arbitrary")),
    )(dn, dn, ba, conv_w, alog, dtb)


def _dn_load(qkv_ref):
    qs = [qkv_ref[:, h * DN_D:(h + 1) * DN_D] for h in range(DN_H)]
    ks = [qkv_ref[:, DNW + h * DN_D:DNW + (h + 1) * DN_D] for h in range(DN_H)]
    vs = [qkv_ref[:, 2 * DNW + h * DN_D:2 * DNW + (h + 1) * DN_D] for h in range(DN_H)]
    return qs, ks, vs


DN_GROUP = 4
AW = DN_H * CHUNK


def _stack_heads(ref, G, offset, width):
    return jnp.stack([ref[g * CHUNK:(g + 1) * CHUNK, offset + h * width:offset + (h + 1) * width]
                      for g in range(G) for h in range(DN_H)])


def _dn_load_stack(qkv_ref, G):
    return tuple(_stack_heads(qkv_ref, G, j * DNW, DN_D) for j in range(3))


def _cd_spec(n):
    return pl.BlockSpec((None, n, 1, DN_D), lambda b, i: (b, i, 0, 0))


def _dn_intra_fwd(qkv, bg, ex=None):
    B, S, _ = qkv.shape
    nc = S // CHUNK
    G = min(DN_GROUP, nc)
    rows = G * CHUNK

    def body(qkv_ref, bg_ref, u_ref, w_ref, qd_ref, kd_ref, a_ref, cd_ref, t_ref):
        q, k, v = _dn_load_stack(qkv_ref, G)
        u, w, qd, kd, a, cd, tinv = _dn_intra(q, k, v, bg_ref[...].reshape(G, CHUNK, DN_D))
        lane_row = _iota((1, DN_D), 1)
        for g in range(G):
            rows = slice(g * CHUNK, (g + 1) * CHUNK)
            cd_row = jnp.zeros((1, DN_D), F32)
            for h in range(DN_H):
                n = g * DN_H + h
                cols = slice(h * DN_D, (h + 1) * DN_D)
                u_ref[rows, cols] = u[n]
                w_ref[rows, cols] = w[n].astype(BF)
                qd_ref[rows, cols] = qd[n].astype(BF)
                kd_ref[rows, cols] = kd[n].astype(BF)
                a_ref[rows, h * CHUNK:(h + 1) * CHUNK] = a[n].astype(BF)
                t_ref[rows, h * CHUNK:(h + 1) * CHUNK] = tinv[n]
                cd_row = cd_row + jnp.where(lane_row == h, cd[n], 0.0)
            cd_ref[g] = cd_row

    return _hosted_call(
        body, "dn_intra_fwd", (B, nc // G),
        in_specs=[_tok(rows, CONVW), _tok(rows, 128)],
        out_specs=[_tok(rows, DNW)] * 4 + [_tok(rows, AW), _cd_spec(G), _tok(rows, AW)],
        out_shape=[_sds((B, S, DNW), F32)] + [_sds((B, S, DNW), BF)] * 3 + [_sds((B, S, AW), BF), _sds((B, nc, 1, DN_D), F32),
                                                                            _sds((B, S, AW), F32)],
        scratch_shapes=[], semantics=("parallel", "parallel"), ins=(qkv, bg), ex=ex)


REC_GROUP = 4


def _rec_stack(ref, B, width, c):
    rows = slice(c * CHUNK, (c + 1) * CHUNK)
    return jnp.stack([ref[b, rows, h * width:(h + 1) * width].astype(F32) for b in range(B) for h in range(DN_H)])


def _rec_load(B, u_ref, w_ref, qd_ref, kd_ref, a_ref, cd_ref, c):
    lane_row = _iota((1, DN_D), 1)
    cd = jnp.stack([jnp.sum(jnp.where(lane_row == h, cd_ref[b, c], 0.0), axis=-1, keepdims=True)
                    for b in range(B) for h in range(DN_H)])
    return (_rec_stack(u_ref, B, DN_D, c), _rec_stack(w_ref, B, DN_D, c), _rec_stack(qd_ref, B, DN_D, c),
            _rec_stack(kd_ref, B, DN_D, c), _rec_stack(a_ref, B, CHUNK, c), cd)


def _rec_store(B, ref, val, width, c):
    for b in range(B):
        for h in range(DN_H):
            ref[b, c * CHUNK:(c + 1) * CHUNK, h * width:(h + 1) * width] = val[b * DN_H + h]


def _rec_specs(B, R, index):
    def tok(f):
        return pl.BlockSpec((B, R * CHUNK, f), lambda i: (0, index(i), 0))
    cd = pl.BlockSpec((B, R, 1, DN_D), lambda i: (0, index(i), 0, 0))
    st = pl.BlockSpec((B, R, DN_H, DN_D, DN_D), lambda i: (0, index(i), 0, 0, 0))
    return tok, cd, st


def _dn_rec_fwd(u, w, qd, kd, a, cd, ex=None):
    B, S, _ = u.shape
    nc = S // CHUNK
    R = REC_GROUP if nc % REC_GROUP == 0 else 1
    tok, cd_spec, st_spec = _rec_specs(B, R, lambda i: i)

    def body(u_ref, w_ref, qd_ref, kd_ref, a_ref, cd_ref, o_ref, st_ref, s_ref):
        @pl.when(pl.program_id(0) == 0)
        def _():
            s_ref[...] = jnp.zeros_like(s_ref)
        state = s_ref[...]
        for c in range(R):
            st_ref[:, c] = state.reshape(B, DN_H, DN_D, DN_D)
            state, out = _dn_rec(state, *_rec_load(B, u_ref, w_ref, qd_ref, kd_ref, a_ref, cd_ref, c))
            _rec_store(B, o_ref, out, DN_D, c)
        s_ref[...] = state

    return _hosted_call(
        body, "dn_rec_fwd", (nc // R,),
        in_specs=[tok(DNW)] * 4 + [tok(AW), cd_spec],
        out_specs=[tok(DNW), st_spec],
        out_shape=[_sds((B, S, DNW), F32), _sds((B, nc, DN_H, DN_D, DN_D), F32)],
        scratch_shapes=[pltpu.VMEM((B * DN_H, DN_D, DN_D), F32)],
        semantics=("arbitrary",), ins=(u, w, qd, kd, a, cd), ex=ex)


def _mix_load(oa_ref, or_ref, z_ref):
    o_raw = [or_ref[:, h * DN_D:(h + 1) * DN_D] for h in range(DN_H)]
    zs = [z_ref[:, h * DN_D:(h + 1) * DN_D].astype(F32) for h in range(DN_H)]
    return oa_ref[...].astype(F32), o_raw, zs


def _mix_fwd(o_attn, o_raw, z, ga, gb, x, mod, dn_g, w_branch, w_out, bt):
    B, S, _ = x.shape

    def body(oa_ref, or_ref, z_ref, ga_ref, gb_ref, x_ref, mod_ref, dg_ref, wb_ref, wo_ref, x1_ref, od_ref, mg_ref):
        oa, o_r, zs = _mix_load(oa_ref, or_ref, z_ref)
        x1, o_dn, merged = _mix_tile(oa, o_r, zs, ga_ref[...].astype(F32), gb_ref[...].astype(F32), x_ref[...],
                                     mod_ref[:, 2 * D:3 * D], dg_ref[...], wb_ref[0:QW, :], wb_ref[QW:2 * QW, :],
                                     wo_ref[...], 0.0, 0.0, 0.0)
        x1_ref[...] = x1
        od_ref[...] = o_dn.astype(BF)
        mg_ref[...] = merged.astype(BF)

    return pl.pallas_call(
        body, name="mix_fwd", grid=(B, S // bt),
        in_specs=[_tok(bt, QW), _tok(bt, DNW), _tok(bt, DNW), _tok(bt, D), _tok(bt, D), _tok(bt, D), _per_batch(6 * D),
                  _full((1, DN_D)), _resident((D, D)), _resident((D, D))],
        out_specs=[_tok(bt, D), _tok(bt, DNW), _tok(bt, D)],
        out_shape=[_sds((B, S, D), F32), _sds((B, S, DNW), BF), _sds((B, S, D), BF)],
        compiler_params=_cparams(dimension_semantics=("parallel", "parallel")),
    )(o_attn, o_raw, z, ga, gb, x, mod, dn_g, w_branch, w_out)


def _mlp(x1, tgt, mod, norm2_g, w_gu, w_dn, bt):
    B, S, _ = x1.shape

    def body(x1_ref, t_ref, mod_ref, g_ref, wgu_ref, wdn_ref,
             dx1_ref, h2_ref, act_ref, dgu_ref, dyy_ref, loss_ref, dmod_ref, dg_ref):
        w_gu_v, w_dn_v, t = [wgu_ref[k] for k in range(N_CHIP)], wdn_ref[...], t_ref[...]

        def f(x1, gain, shift, scale, gate2, p_gu, p_yy):
            return _mlp_tile(x1, gain, shift, scale, gate2, w_gu_v, w_dn_v, t, p_gu, p_yy)
        zero_gu = jnp.zeros((bt, 2 * FFN), F32)
        zero_yy = jnp.zeros((bt, D), F32)
        loss, vjp, (h2, act) = jax.vjp(f, x1_ref[...], g_ref[...], mod_ref[:, 3 * D:4 * D], mod_ref[:, 4 * D:5 * D],
                                       mod_ref[:, 5 * D:6 * D], zero_gu, zero_yy, has_aux=True)
        dx1, dgain, dshift, dscale, dgate2, dgu, dyy = vjp(jnp.ones((), F32))
        dx1_ref[...] = dx1
        h2_ref[...] = h2.astype(BF)
        act_ref[...] = act.astype(BF)
        dgu_ref[...] = dgu.astype(BF)
        dyy_ref[...] = dyy.astype(BF)
        first = pl.program_id(1) == 0
        _acc(loss_ref, jnp.reshape(loss, (1, 1)), first)
        _acc(dmod_ref, jnp.concatenate([dshift, dscale, dgate2], axis=-1), first)
        _acc(dg_ref, dgain, first)

    return pl.pallas_call(
        body, name="mlp", grid=(B, S // bt),
        in_specs=[_tok(bt, D), _tok(bt, D), _per_batch(6 * D), _full((1, D)), _resident((N_CHIP, D, 2 * FFN // N_CHIP)),
                  _resident((FFN, D))],
        out_specs=[_tok(bt, D), _tok(bt, D), _tok(bt, FFN), _tok(bt, 2 * FFN), _tok(bt, D),
                   _per_batch(1), _per_batch(3 * D), _per_batch(D)],
        out_shape=[_sds((B, S, D), F32), _sds((B, S, D), BF), _sds((B, S, FFN), BF), _sds((B, S, 2 * FFN), BF),
                   _sds((B, S, D), BF), _sds((B, 1, 1), F32), _sds((B, 1, 3 * D), F32), _sds((B, 1, D), F32)],
        compiler_params=_cparams(dimension_semantics=("parallel", "arbitrary")),
    )(x1, tgt, mod, norm2_g, w_gu, w_dn)


def _mix_bwd(o_attn, o_raw, z, ga, gb, x, mod, dn_g, w_branch, w_out, dx1, bt, ex=None):
    B, S, _ = x.shape

    def body(oa_ref, or_ref, z_ref, ga_ref, gb_ref, x_ref, mod_ref, dg_ref, wb_ref, wo_ref, dx1_ref,
             doa_ref, dor_ref, dz_ref, dga_ref, dgb_ref, dya_ref, dyd_ref, dout_ref, dgate_ref, ddg_ref):
        oa, o_r, zs = _mix_load(oa_ref, or_ref, z_ref)
        wb_a, wb_d, wo = wb_ref[0:QW, :], wb_ref[QW:2 * QW, :], wo_ref[...]

        def f(oa, o_r, zs, ga, gb, gate1, dn_g, p_ya, p_yd, p_out):
            return _mix_tile(oa, o_r, zs, ga, gb, x_ref[...], gate1, dn_g, wb_a, wb_d, wo, p_ya, p_yd, p_out)[0]
        zero = jnp.zeros((bt, D), F32)
        _, vjp = jax.vjp(f, oa, o_r, zs, ga_ref[...].astype(F32), gb_ref[...].astype(F32), mod_ref[:, 2 * D:3 * D],
                         dg_ref[...], zero, zero, zero)
        doa, dor, dzs, dga, dgb, dgate1, ddn_g, dya, dyd, dout = vjp(dx1_ref[...])
        doa_ref[...] = doa
        for h in range(DN_H):
            dor_ref[:, h * DN_D:(h + 1) * DN_D] = dor[h]
            dz_ref[:, h * DN_D:(h + 1) * DN_D] = dzs[h].astype(BF)
        dga_ref[...] = dga.astype(BF)
        dgb_ref[...] = dgb.astype(BF)
        dya_ref[...] = dya.astype(BF)
        dyd_ref[...] = dyd.astype(BF)
        dout_ref[...] = dout.astype(BF)
        first = pl.program_id(1) == 0
        _acc(dgate_ref, dgate1, first)
        _acc(ddg_ref, ddn_g, first)

    return _hosted_call(
        body, "mix_bwd", (B, S // bt),
        in_specs=[_tok(bt, QW), _tok(bt, DNW), _tok(bt, DNW), _tok(bt, D), _tok(bt, D), _tok(bt, D), _per_batch(6 * D),
                  _full((1, DN_D)), _resident((D, D)), _resident((D, D)), _tok(bt, D)],
        out_specs=[_tok(bt, QW), _tok(bt, DNW), _tok(bt, DNW), _tok(bt, D), _tok(bt, D), _tok(bt, D), _tok(bt, D), _tok(bt, D),
                   _per_batch(D), _per_batch(DN_D)],
        out_shape=[_sds((B, S, QW), F32), _sds((B, S, DNW), F32), _sds((B, S, DNW), BF), _sds((B, S, D), BF),
                   _sds((B, S, D), BF), _sds((B, S, D), BF), _sds((B, S, D), BF), _sds((B, S, D), BF),
                   _sds((B, 1, D), F32), _sds((B, 1, DN_D), F32)],
        scratch_shapes=[], semantics=("parallel", "arbitrary"),
        ins=(o_attn, o_raw, z, ga, gb, x, mod, dn_g, w_branch, w_out, dx1), ex=ex)


def _dn_rec_bwd(u, w, qd, kd, a, cd, states, d_o, ex=None):
    B, S, _ = u.shape
    nc = S // CHUNK
    R = REC_GROUP if nc % REC_GROUP == 0 else 1
    tok, cd_spec, st_spec = _rec_specs(B, R, lambda i: nc // R - 1 - i)

    def body(u_ref, w_ref, qd_ref, kd_ref, a_ref, cd_ref, st_ref, do_ref,
             du_ref, dw_ref, dqd_ref, dkd_ref, da_ref, dcd_ref, ds_ref):
        @pl.when(pl.program_id(0) == 0)
        def _():
            ds_ref[...] = jnp.zeros_like(ds_ref)
        lane_row = _iota((1, DN_D), 1)
        d_state = ds_ref[...]
        for c in reversed(range(R)):
            state = st_ref[:, c].reshape(B * DN_H, DN_D, DN_D)
            _, vjp = jax.vjp(_dn_rec, state, *_rec_load(B, u_ref, w_ref, qd_ref, kd_ref, a_ref, cd_ref, c))
            d_state, du, dw, dqd, dkd, da, dcd = vjp((d_state, _rec_stack(do_ref, B, DN_D, c)))
            for ref, val, width in ((du_ref, du, DN_D), (dw_ref, dw, DN_D), (dqd_ref, dqd, DN_D), (dkd_ref, dkd, DN_D),
                                    (da_ref, da, CHUNK)):
                _rec_store(B, ref, val, width, c)
            for b in range(B):
                row = jnp.zeros((1, DN_D), F32)
                for h in range(DN_H):
                    row = row + jnp.where(lane_row == h, dcd[b * DN_H + h], 0.0)
                dcd_ref[b, c] = row
        ds_ref[...] = d_state

    return _hosted_call(
        body, "dn_rec_bwd", (nc // R,),
        in_specs=[tok(DNW)] * 4 + [tok(AW), cd_spec, st_spec, tok(DNW)],
        out_specs=[tok(DNW)] * 4 + [tok(AW), cd_spec],
        out_shape=[_sds((B, S, DNW), F32)] * 4 + [_sds((B, S, AW), F32), _sds((B, nc, 1, DN_D), F32)],
        scratch_shapes=[pltpu.VMEM((B * DN_H, DN_D, DN_D), F32)],
        semantics=("arbitrary",), ins=(u, w, qd, kd, a, cd, states, d_o), ex=ex)


def _dn_intra_bwd(qkv, bg, tinv, du, dw, dqd, dkd, da, dcd, ex=None):
    B, S, _ = qkv.shape
    nc = S // CHUNK
    G = min(DN_GROUP, nc)
    rows = G * CHUNK

    def body(qkv_ref, bg_ref, t_ref, du_ref, dw_ref, dqd_ref, dkd_ref, da_ref, dcd_ref, dqkv_ref, dbg_ref):
        q, k, v = _dn_load_stack(qkv_ref, G)
        known = _stack_heads(t_ref, G, 0, CHUNK)
        _, vjp = jax.vjp(lambda q, k, v, bg: _dn_intra(q, k, v, bg, known)[:6], q, k, v,
                         bg_ref[...].reshape(G, CHUNK, DN_D))
        lane_row = _iota((1, DN_D), 1)
        dcd = jnp.stack([jnp.sum(jnp.where(lane_row == h, dcd_ref[g], 0.0), axis=-1, keepdims=True)
                         for g in range(G) for h in range(DN_H)])
        dq, dk, dv, dbg = vjp((_stack_heads(du_ref, G, 0, DN_D), _stack_heads(dw_ref, G, 0, DN_D),
                               _stack_heads(dqd_ref, G, 0, DN_D), _stack_heads(dkd_ref, G, 0, DN_D),
                               _stack_heads(da_ref, G, 0, CHUNK), dcd))
        for g in range(G):
            rows = slice(g * CHUNK, (g + 1) * CHUNK)
            for h in range(DN_H):
                n = g * DN_H + h
                dqkv_ref[rows, h * DN_D:(h + 1) * DN_D] = dq[n]
                dqkv_ref[rows, DNW + h * DN_D:DNW + (h + 1) * DN_D] = dk[n]
                dqkv_ref[rows, 2 * DNW + h * DN_D:2 * DNW + (h + 1) * DN_D] = dv[n]
        dbg_ref[...] = dbg.reshape(G * CHUNK, DN_D)

    return _hosted_call(
        body, "dn_intra_bwd", (B, nc // G),
        in_specs=[_tok(rows, CONVW), _tok(rows, 128), _tok(rows, AW)] + [_tok(rows, DNW)] * 4 + [_tok(rows, AW), _cd_spec(G)],
        out_specs=[_tok(rows, CONVW), _tok(rows, 128)],
        out_shape=[_sds((B, S, CONVW), F32), _sds((B, S, 128), F32)],
        scratch_shapes=[], semantics=("parallel", "parallel"), ins=(qkv, bg, tinv, du, dw, dqd, dkd, da, dcd), ex=ex)


def _dn_prep_bwd(dn, y, ba, conv_w, alog, dtb, dqkv, dbg, bt, ex=None):
    B, S, _ = dn.shape
    nt = S // bt
    strip_rows = min(bt, 64)

    def rev(f):
        return pl.BlockSpec((None, bt, f), lambda b, i: (b, nt - 1 - i, 0))

    def body(x_ref, y_ref, ba_ref, cw_ref, al_ref, dt_ref, dqkv_ref, dbg_ref,
             dx_ref, dba_ref, dcw_ref, dal_ref, ddt_ref, dye_ref):
        i = pl.program_id(1)
        @pl.when(i == 0)
        def _():
            dye_ref[bt:bt + 8, :] = jnp.zeros((8, CONVW), F32)

        @pl.when(i > 0)
        def _():
            dye_ref[bt:bt + 8, :] = dye_ref[0:8, :]

        n_strips = bt // strip_rows

        def strip(k, carry):
            dal, ddt, dcw = carry
            r0 = pl.multiple_of((n_strips - 1 - k) * strip_rows, strip_rows)
            rows = pl.ds(r0, strip_rows)
            dcw_slabs = []
            for j in range(3 * DN_H):
                cols = slice(j * DN_D, (j + 1) * DN_D)
                _, vjp = jax.vjp(functools.partial(_dn_act, normalize=j < 2 * DN_H), y_ref[rows, cols].astype(F32))
                (dye_ref[rows, cols],) = vjp(dqkv_ref[rows, cols])
                window = dye_ref[pl.ds(r0, strip_rows + 8), cols]
                shifted = [window[3 - t:strip_rows + 3 - t] for t in range(CONV)]
                dx = cw_ref[0:1, cols] * shifted[0]
                for t in range(1, CONV):
                    dx = dx + cw_ref[t:t + 1, cols] * shifted[t]
                dx_ref[rows, cols] = dx.astype(BF)
                x = x_ref[rows, cols].astype(F32)
                dcw_slabs.append(jnp.concatenate([jnp.sum(shifted[t] * x, axis=0, keepdims=True) for t in range(CONV)], axis=0))
            _, vjp = jax.vjp(_dn_gates, ba_ref[rows, :], al_ref[...], dt_ref[...])
            dba_ref[rows, :], da, dd = vjp(dbg_ref[rows, :])
            return dal + da, ddt + dd, dcw + jnp.concatenate(dcw_slabs, axis=1)
        zero = jnp.zeros((1, DN_D), F32)
        dal, ddt, dcw = lax.fori_loop(0, n_strips, strip, (zero, zero, jnp.zeros((CONV, CONVW), F32)))
        first = (i == 0) & (pl.program_id(0) == 0)
        _acc(dcw_ref, dcw, first)
        _acc(dal_ref, dal, first)
        _acc(ddt_ref, ddt, first)

    return _hosted_call(
        body, "dn_prep_bwd", (B, nt),
        in_specs=[rev(CONVW), rev(CONVW), rev(128), _full((CONV, CONVW)), _full((1, 128)), _full((1, 128)), rev(CONVW), rev(128)],
        out_specs=[rev(CONVW), rev(128), _full((CONV, CONVW)), _full((1, 128)), _full((1, 128))],
        out_shape=[_sds((B, S, CONVW), BF), _sds((B, S, 128), F32), _sds((CONV, CONVW), F32), _sds((1, 128), F32),
                   _sds((1, 128), F32)],
        scratch_shapes=[pltpu.VMEM((bt + 8, CONVW), F32)],
        semantics=("arbitrary", "arbitrary"), ins=(dn, y, ba, conv_w, alog, dtb, dqkv, dbg), ex=ex)


def _attn_bwd(qkv, sinks, d_o, ex=None):
    B, S, _ = qkv.shape
    nb = S // BLK

    def cur(f):
        return pl.BlockSpec((None, BLK, f), lambda b, i: (b, jnp.minimum(i, nb - 1), 0))

    def out_prev(f):
        return pl.BlockSpec((None, BLK, f), lambda b, i: (b, jnp.maximum(i - 1, 0), 0))

    def body(qkv_ref, kvp_ref, sk_ref, do_ref, dq_ref, dkv_ref, dsk_ref, carry_ref):
        n = pl.program_id(1)
        first = (n == 0) & (pl.program_id(0) == 0)

        @pl.when(n == 0)
        def _():
            carry_ref[...] = jnp.zeros_like(carry_ref)

        @pl.when(n < nb)
        def _():
            qs, kc, kp, vc, vp = _attn_load(qkv_ref, kvp_ref)

            def f(qs, kc, kp, vc, vp, sk):
                return _attn_block(qs, kc, kp, vc, vp, sk, n > 0)
            _, vjp = jax.vjp(f, qs, kc, kp, vc, vp, sk_ref[...])
            d_outs = [do_ref[:, h * HD:(h + 1) * HD] for h in range(HEADS)]
            dqs, dkc, dkp, dvc, dvp, dsk = vjp(d_outs)
            for h in range(HEADS):
                dq_ref[:, h * HD:(h + 1) * HD] = dqs[h].astype(BF)
            for h in range(KV_HEADS):
                ksl = slice(h * HD, (h + 1) * HD)
                vsl = slice(KVW + h * HD, KVW + (h + 1) * HD)
                dkv_ref[:, ksl] = (carry_ref[:, ksl] + dkp[h]).astype(BF)
                dkv_ref[:, vsl] = (carry_ref[:, vsl] + dvp[h]).astype(BF)
                carry_ref[:, ksl] = dkc[h]
                carry_ref[:, vsl] = dvc[h]
            _acc(dsk_ref, dsk, first)

        @pl.when(n == nb)
        def _():
            dkv_ref[...] = carry_ref[...].astype(BF)

    return _hosted_call(
        body, "attn_bwd", (B, nb + 1),
        in_specs=[cur(QKV), _kv_prev_spec(lambda i: jnp.maximum(jnp.minimum(i, nb - 1) - 1, 0)), _full((1, HEADS)), cur(QW)],
        out_specs=[cur(QW), out_prev(2 * KVW), _full((1, HEADS))],
        out_shape=[_sds((B, S, QW), BF), _sds((B, S, 2 * KVW), BF), _sds((1, HEADS), F32)],
        scratch_shapes=[pltpu.VMEM((BLK, 2 * KVW), F32)],
        semantics=("arbitrary", "arbitrary"), ins=(qkv, qkv, sinks, d_o), ex=ex)


def _in_proj_bwd(x, mod, norm1_g, w_in, pieces, dba, dx1, bt):
    B, S, _ = x.shape
    widths = (QW, 2 * KVW, CONVW, DNW, D, D)

    def body(x_ref, mod_ref, g_ref, w_ref, dq_ref, dkv_ref, ddn_ref, dz_ref, dga_ref, dgb_ref, dba_ref, dx1_ref,
             gx_ref, dp_ref, dmod_ref, dg_ref):
        dp = jnp.concatenate([r[...] for r in (dq_ref, dkv_ref, ddn_ref, dz_ref, dga_ref, dgb_ref)]
                             + [dba_ref[...].astype(BF)], axis=-1)
        dp_ref[...] = dp
        dh = lax.dot_general(dp, w_ref[...], (((1,), (1,)), ((), ())), preferred_element_type=F32)
        _, vjp = jax.vjp(_norm_mod, x_ref[...], g_ref[...], mod_ref[:, 0:D], mod_ref[:, D:2 * D])
        dx, dgain, dshift, dscale = vjp(dh)
        gx_ref[...] = dx + dx1_ref[...]
        first = pl.program_id(1) == 0
        _acc(dmod_ref, jnp.concatenate([dshift, dscale], axis=-1), first)
        _acc(dg_ref, dgain, first)

    return pl.pallas_call(
        body, name="in_proj_bwd", grid=(B, S // bt),
        in_specs=[_tok(bt, D), _per_batch(6 * D), _full((1, D)), _resident((D, IN_PAD))] + [_tok(bt, w) for w in widths]
        + [_tok(bt, 128), _tok(bt, D)],
        out_specs=[_tok(bt, D), _tok(bt, IN_PAD), _per_batch(2 * D), _per_batch(D)],
        out_shape=[_sds((B, S, D), F32), _sds((B, S, IN_PAD), BF), _sds((B, 1, 2 * D), F32), _sds((B, 1, D), F32)],
        compiler_params=_cparams(dimension_semantics=("parallel", "arbitrary")),
    )(x, mod, norm1_g, w_in, *pieces, dba, dx1)


def _matmul_tn(tag, a, b, bk, bn, bt, col_blocks=False, ex=None):
    T, K = a.shape
    N = b.shape[1]
    nt = T // bt
    if col_blocks:
        assert bk == K
        out_spec = pl.BlockSpec((None, bk, bn), lambda i, j, t: (j, 0, 0))
        out_shape = _sds((N // bn, K, bn), F32)
    else:
        out_spec = pl.BlockSpec((bk, bn), lambda i, j, t: (i, j))
        out_shape = _sds((K, N), F32)

    def body(a_ref, b_ref, o_ref, acc_ref):
        t = pl.program_id(2)

        @pl.when(t == 0)
        def _():
            acc_ref[...] = jnp.zeros_like(acc_ref)
        acc_ref[...] += lax.dot_general(a_ref[...], b_ref[...], (((0,), (0,)), ((), ())), preferred_element_type=F32)

        @pl.when(t == nt - 1)
        def _():
            o_ref[...] = acc_ref[...]

    (out,), landed = _hosted_call(
        body, f"grad_{tag}", (K // bk, N // bn, nt),
        in_specs=[pl.BlockSpec((bt, bk), lambda i, j, t: (t, i)), pl.BlockSpec((bt, bn), lambda i, j, t: (t, j))],
        out_specs=[out_spec], out_shape=[out_shape],
        scratch_shapes=[pltpu.VMEM((bk, bn), F32)],
        semantics=("parallel", "parallel", "arbitrary"), ins=(a, b), ex=ex)
    return out if ex is None else (out, landed)


def _rope_table(positions):
    inv_freq = THETA ** (-jnp.arange(0, ROT, 2, dtype=F32) / ROT)
    rest = jnp.zeros((HD - ROT,), F32)
    freq = jnp.concatenate([inv_freq, inv_freq, rest] * 2)
    sign = jnp.concatenate([-jnp.ones_like(inv_freq), jnp.ones_like(inv_freq), rest] * 2)
    ang = positions.astype(F32)[..., None] * freq
    return jnp.cos(ang), jnp.sin(ang) * sign


def _lane_pad(v, offset, width=128):
    return jnp.zeros((1, width), F32).at[0, offset:offset + v.shape[-1]].set(v.reshape(-1))


def _tile(S, want):
    return min(S, want)


class _Hosted:
    def __init__(self, call):
        self.call = call
        self.outs = None

    def __call__(self, ex):
        self.outs, landed = self.call(ex)
        return landed


def _local_step(x, mod, positions, tgt, norm1_g, w_in_pad, conv_w, q_norm_g, k_norm_g, sinks, a_log, dt_bias,
                dn_norm_g, w_branch, w_out, norm2_g, w_gu, w_dn, dist=None):
    B, S, _ = x.shape
    T = B * S
    cos_t, sin_t = _rope_table(positions)
    qg2 = jnp.concatenate([q_norm_g, q_norm_g], axis=-1)
    kg2 = jnp.concatenate([k_norm_g, k_norm_g], axis=-1)
    alog = _lane_pad(a_log, DN_H)
    dtb = _lane_pad(dt_bias, DN_H)
    conv2 = conv_w.reshape(CONV, CONVW)
    bt = _tile(S, 512)
    bt_mlp = _tile(S, 256)

    q, kv, dn, z, ga, gb, ba, h1 = _in_proj(x, mod, norm1_g, w_in_pad, bt)
    qkv_n = _qk_prep_fwd(q, kv, cos_t, sin_t, qg2, kg2, bt)
    o_attn = _attn_fwd(qkv_n, sinks)
    dqkv, bg, dn_y = _dn_prep(dn, ba, conv2, alog, dtb, bt)
    intra = _Hosted(lambda ex: _dn_intra_fwd(dqkv, bg, ex))
    if dist is None:
        intra(None)
    else:
        f_br, f_out, w_gu, f_dn = _gather_weights("late", [w_branch, w_out, w_gu, w_dn], dist[0], host=intra)
        w_branch, w_out, w_dn = (f.reshape(N_CHIP * f.shape[1], f.shape[2]) for f in (f_br, f_out, f_dn))
    dn_u, dn_w, dn_qd, dn_kd, dn_a, dn_cd, dn_tinv = intra.outs
    (o_raw, states), _ = _dn_rec_fwd(dn_u, dn_w, dn_qd, dn_kd, dn_a, dn_cd)
    x1, o_dn, merged = _mix_fwd(o_attn, o_raw, z, ga, gb, x, mod, dn_norm_g, w_branch, w_out, bt)
    dx1, h2, act, dgu, dyy, loss, dmod2, dnorm2 = _mlp(x1, tgt, mod, norm2_g, w_gu, w_dn, bt_mlp)

    def flat(t):
        return t.reshape(T, t.shape[-1])
    tn = functools.partial(_matmul_tn, bt=_tile(T, 2048))
    g_w_dn = tn("w_down", flat(act), flat(dyy), bk=FFN, bn=D // 2)
    g_w_gu = tn("w_gate_up", flat(h2), flat(dgu), bk=D, bn=2 * FFN // N_CHIP, col_blocks=True)

    mix_b = _Hosted(lambda ex: _mix_bwd(o_attn, o_raw, z, ga, gb, x, mod, dn_norm_g, w_branch, w_out, dx1, bt_mlp, ex))
    rec_b = _Hosted(lambda ex: _dn_rec_bwd(dn_u, dn_w, dn_qd, dn_kd, dn_a, dn_cd, states, mix_b.outs[1], ex))
    intra_b = _Hosted(lambda ex: _dn_intra_bwd(dqkv, bg, dn_tinv, *rec_b.outs, ex))
    if dist is None:
        for host in (mix_b, rec_b, intra_b):
            host(None)
    else:
        g_w_gu, g_w_dn = _reduce_grads(("w_gate_up", "w_down"), [g_w_gu, g_w_dn.reshape(N_CHIP, -1, D)], *dist,
                                       hosts=[mix_b, rec_b, intra_b])
    d_oa, _, dz, dga, dgb, dya, dyd, dout, dgate1, ddn_g = mix_b.outs
    d_dqkv, dbg = intra_b.outs
    g_w_out = tn("w_out", flat(merged), flat(dout), bk=D, bn=D)
    g_w_br = jnp.concatenate([tn("w_branch_attn", flat(o_attn), flat(dya), bk=QW, bn=D),
                              tn("w_branch_dn", flat(o_dn), flat(dyd), bk=DNW, bn=D)], axis=0)
    prep_b = _Hosted(lambda ex: _dn_prep_bwd(dn, dn_y, ba, conv2, alog, dtb, d_dqkv, dbg, bt, ex))
    attn_b = _Hosted(lambda ex: _attn_bwd(qkv_n, sinks, d_oa, ex))
    qk_b = _Hosted(lambda ex: _qk_prep_bwd(q, kv, cos_t, sin_t, qg2, kg2, *attn_b.outs[:2], bt, ex))
    if dist is None:
        for host in (prep_b, attn_b, qk_b):
            host(None)
    else:
        g_w_br, g_w_out = _reduce_grads(("w_branch", "w_out"), [g_w_br.reshape(N_CHIP, -1, D), g_w_out.reshape(N_CHIP, -1, D)],
                                        *dist, hosts=[prep_b, attn_b, qk_b])
    d_dn, dba, dconv, dalog, ddtb = prep_b.outs
    dsk = attn_b.outs[2]
    dq, dkv, dqg2, dkg2 = qk_b.outs
    dqg = dqg2[:, :HD] + dqg2[:, HD:]
    dkg = dkg2[:, :HD] + dkg2[:, HD:]
    grad_x, dproj, dmod1, dnorm1 = _in_proj_bwd(x, mod, norm1_g, w_in_pad, (dq, dkv, d_dn, dz, dga, dgb), dba, dx1, bt)
    g_w_in = _Hosted(lambda ex: (tn("w_in", flat(h1), flat(dproj), bk=D, bn=IN_PAD // 3), ()) if ex is None
                     else tn("w_in", flat(h1), flat(dproj), bk=D, bn=IN_PAD // 3, ex=ex))
    if dist is None:
        g_w_in(None)
        g_w_in = g_w_in.outs

    dmod = jnp.concatenate([dmod1, dgate1, dmod2], axis=-1)
    small = dict(norm1_g=jnp.sum(dnorm1, axis=0), norm2_g=jnp.sum(dnorm2, axis=0), q_norm_g=dqg, k_norm_g=dkg,
                 sinks=dsk, a_log=dalog[:, DN_H:2 * DN_H], dt_bias=ddtb[:, DN_H:2 * DN_H],
                 dn_norm_g=jnp.sum(ddn_g, axis=0), conv_w=dconv)
    return jnp.sum(loss), grad_x, dmod, small, (g_w_in, g_w_br, g_w_out, g_w_gu, g_w_dn)


def _flip(me, f):
    return (me[0] ^ ((f >> 2) & 1), me[1] ^ ((f >> 1) & 1), me[2] ^ (f & 1))


def _comm_call(name, ex):
    n_in, n_out = len(ex.ins), len(ex.out_shapes)

    def body(*refs):
        out_refs, sems = refs[n_in:n_in + n_out], refs[n_in + n_out:]
        cps = _exchange_copies(ex, refs[:n_in], out_refs, sems[0], sems[1])
        for cp in cps:
            cp.start()
        for cp in cps:
            cp.wait_recv()
        if ex.n_forward:
            fwd = _forward_copies(ex, out_refs, sems[2], sems[3])
            for cp in fwd:
                cp.start()
            for cp in fwd:
                cp.wait_recv()
            cps = cps + fwd
        for cp in cps:
            cp.wait_send()

    any_spec = pl.BlockSpec(memory_space=pl.ANY)
    return pl.pallas_call(
        body, name=name, in_specs=[any_spec] * n_in, out_specs=[any_spec] * n_out, out_shape=list(ex.out_shapes),
        scratch_shapes=_exchange_sems(ex),
    )(*ex.ins)


def _by_origin(own, received, index):
    stack = jnp.concatenate([own[None], received], axis=0)
    n = stack.shape[0]
    return jnp.stack([lax.dynamic_index_in_dim(stack, k ^ index, 0, keepdims=False) for k in range(n)])


def _gather_devices(name, arrs, dev, host=None):
    def plan(me, in_refs, out_refs):
        return [(a, o.at[f - 1], _flip(me, f)) for a, o in zip(in_refs, out_refs) for f in range(1, N_DEV)]
    outs = tuple(_sds((N_DEV - 1,) + a.shape, a.dtype) for a in arrs)
    got = (host or functools.partial(_comm_call, name))(_Exchange(tuple(arrs), outs, (N_DEV - 1) * len(arrs), plan))
    return [_by_origin(a, g, dev) for a, g in zip(arrs, got)]


def _gather_chips(name, arrs, chip):
    def plan(me, in_refs, out_refs):
        return [(a, o.at[j], _flip(me, 2 * (j + 1))) for a, o in zip(in_refs, out_refs) for j in range(N_CHIP - 1)]
    outs = tuple(_sds((N_CHIP - 1,) + a.shape, a.dtype) for a in arrs)
    got = _comm_call(name, _Exchange(tuple(arrs), outs, (N_CHIP - 1) * len(arrs), plan))
    return [_by_origin(a, g, chip) for a, g in zip(arrs, got)]


def _swap_cores_ex(arrs):
    def plan(me, in_refs, out_refs):
        return [(g, o, _flip(me, 1)) for g, o in zip(in_refs, out_refs)]
    return _Exchange(tuple(arrs), tuple(_sds(g.shape, g.dtype) for g in arrs), len(arrs), plan)


def _gather_weights(tag, shards, chip, host=None):
    def plan(me, in_refs, out_refs):
        chip_me = 2 * me[0] + me[1]
        remote = []
        for a, o in zip(in_refs, out_refs):
            half = a.shape[0] // 2
            mine = a.at[pl.ds(me[2] * half, half)]
            remote += [(mine, o.at[chip_me, me[2]], _flip(me, 2 * (j + 1))) for j in range(N_CHIP - 1)]
        return remote

    def forward(me, out_refs):
        chip_me = 2 * me[0] + me[1]
        return [(o.at[chip_me ^ (j + 1), me[2]], o.at[chip_me ^ (j + 1), me[2]], _flip(me, 1))
                for o in out_refs for j in range(N_CHIP - 1)]
    run = host or functools.partial(_comm_call, f"weights_{tag}")
    n = (N_CHIP - 1) * len(shards)
    landed = run(_Exchange(tuple(shards), tuple(_sds((N_CHIP, 2, a.shape[0] // 2, a.shape[1]), a.dtype) for a in shards),
                           n, plan, n, forward))
    return [lax.dynamic_update_slice(f.reshape((N_CHIP,) + a.shape), a[None], (chip, 0, 0)) for a, f in zip(shards, landed)]


def _rows(r):
    for br in (512, 352, 256, 128, 64, 32, 16, 8):
        if r % br == 0:
            return br
    raise ValueError(r)


def _pair_add(tag, g, recv, c):
    n, r, cols = g.shape
    half = r // 2
    br = _rows(half)
    nb = half // br

    def body(c_ref, g_ref, r_ref, o_ref):
        o_ref[...] = (g_ref[...] + r_ref[...]).astype(BF)

    return pl.pallas_call(
        body, name=f"pair_add_{tag}",
        grid_spec=pltpu.PrefetchScalarGridSpec(
            num_scalar_prefetch=1, grid=(n, nb),
            in_specs=[pl.BlockSpec((None, br, cols), lambda k, i, c_ref: (k, c_ref[0] * nb + i, 0)),
                      pl.BlockSpec((None, br, cols), lambda k, i, c_ref: (k, i, 0))],
            out_specs=pl.BlockSpec((None, br, cols), lambda k, i, c_ref: (k, i, 0))),
        out_shape=_sds((n, half, cols), BF),
        compiler_params=_cparams(dimension_semantics=("parallel", "parallel")),
    )(c, g, recv)


def _sum_chips(tag, p, q, chip):
    n, r, cols = q.shape
    br = _rows(r)

    def body(chip_ref, p_ref, q_ref, o_ref):
        acc = p_ref[...].astype(F32)
        for k in range(n):
            acc = acc + q_ref[k].astype(F32)
        o_ref[...] = acc

    return pl.pallas_call(
        body, name=f"sum_chips_{tag}",
        grid_spec=pltpu.PrefetchScalarGridSpec(
            num_scalar_prefetch=1, grid=(r // br,),
            in_specs=[pl.BlockSpec((None, br, cols), lambda i, chip_ref: (chip_ref[0], i, 0)),
                      pl.BlockSpec((n, br, cols), lambda i, chip_ref: (0, i, 0))],
            out_specs=pl.BlockSpec((br, cols), lambda i, chip_ref: (i, 0))),
        out_shape=_sds((r, cols), F32),
        compiler_params=_cparams(dimension_semantics=("parallel",)),
    )(chip, p, q)


def _reduce_grads(tags, grads, chip, core, hosts=None):
    core_arr = core.reshape(1).astype(jnp.int32)
    chip_arr = chip.reshape(1).astype(jnp.int32)
    name = "_".join(tags)
    run = hosts or [functools.partial(_comm_call, f"grads_{stage}_{name}") for stage in ("pair", "chips", "swap")]

    def plan_pair(me, in_refs, out_refs):
        remote = []
        for g, o in zip(in_refs, out_refs):
            half = g.shape[1] // 2
            remote += [(g.at[k, pl.ds((1 - me[2]) * half, half)], o.at[k], _flip(me, 1)) for k in range(N_CHIP)]
        return remote
    recv = run[0](_Exchange(tuple(grads), tuple(_sds((N_CHIP, g.shape[1] // 2, g.shape[2]), F32) for g in grads),
                            N_CHIP * len(grads), plan_pair))
    pair = [_pair_add(t, g, r, core_arr) for t, g, r in zip(tags, grads, recv)]

    def plan_chips(me, in_refs, out_refs):
        remote = []
        for p, o in zip(in_refs, out_refs):
            for j in range(N_CHIP - 1):
                peer = _flip(me, 2 * (j + 1))
                remote.append((p.at[2 * peer[0] + peer[1]], o.at[j], peer))
        return remote
    parts = run[1](_Exchange(tuple(pair), tuple(_sds((N_CHIP - 1,) + p.shape[1:], BF) for p in pair),
                             (N_CHIP - 1) * len(pair), plan_chips))
    mine = [_sum_chips(t, p, q, chip_arr) for t, p, q in zip(tags, pair, parts)]
    other = run[2](_swap_cores_ex(mine))
    return list(zip(mine, other))


def _adamw_math(w, g, m, v):
    m = ADAM_B1 * m + (1.0 - ADAM_B1) * g
    v = ADAM_B2 * v + (1.0 - ADAM_B2) * (g * g)
    m_hat = m / (1.0 - ADAM_B1 ** ADAM_STEP)
    v_hat = v / (1.0 - ADAM_B2 ** ADAM_STEP)
    delta = -ADAM_LR * (m_hat / (jnp.sqrt(v_hat) + ADAM_EPS) + ADAM_WD * w)
    return delta, m, v


def _adamw(name, w, g, m, v, ex=None):
    r, cols = w.shape
    br = _rows(r)
    if br * cols * 4 > (1 << 20) and br % 16 == 0:
        br //= 2

    def body(w_ref, g_ref, m_ref, v_ref, d_ref, mo_ref, vo_ref):
        d_ref[...], mo_ref[...], vo_ref[...] = _adamw_math(w_ref[...], g_ref[...], m_ref[...], v_ref[...])

    spec = pl.BlockSpec((br, cols), lambda i: (i, 0))
    outs, landed = _hosted_call(
        body, f"adamw_{name}", (r // br,), in_specs=[spec] * 4, out_specs=[spec] * 3,
        out_shape=[_sds((r, cols), F32)] * 3, scratch_shapes=[], semantics=("parallel",), ins=(w, g, m, v), ex=ex)
    return outs if ex is None else (outs, landed)


def _adamw_halves(name, w, mine, other, m, v, core, ex=None):
    r, cols = w.shape
    br = _rows(r // 2)
    if br * cols * 4 > (1 << 20) and br % 16 == 0:
        br //= 2
    per_half = r // 2 // br

    def body(core_ref, w_ref, a_ref, b_ref, m_ref, v_ref, g_ref, d_ref, mo_ref, vo_ref):
        g = jnp.where(pl.program_id(0) // per_half == core_ref[0], a_ref[...], b_ref[...])
        g_ref[...] = g
        d_ref[...], mo_ref[...], vo_ref[...] = _adamw_math(w_ref[...], g, m_ref[...], v_ref[...])

    whole = pl.BlockSpec((br, cols), lambda i: (i, 0))
    half = pl.BlockSpec((br, cols), lambda i: (i % per_half, 0))
    outs, landed = _hosted_call(
        body, f"adamw_{name}", (r // br,),
        in_specs=[pl.BlockSpec(memory_space=pltpu.SMEM), whole, half, half, whole, whole], out_specs=[whole] * 4,
        out_shape=[_sds((r, cols), F32)] * 4, scratch_shapes=[], semantics=("parallel",),
        ins=(core.reshape(1).astype(jnp.int32), w, mine, other, m, v), ex=ex)
    return outs if ex is None else (outs, landed)


def _ada_fwd(c_all, ada_w, ada_b_cols):
    n = c_all.shape[0]

    def body(c_ref, w_ref, b_ref, o_ref):
        o_ref[...] = _mmx(_silu(c_ref[...]), w_ref[...]) + b_ref[...]

    return pl.pallas_call(
        body, name="ada_fwd", out_shape=_sds((n, ada_w.shape[1]), F32), compiler_params=_cparams(),
    )(c_all, ada_w, ada_b_cols)


def _ada_bwd(c_all, dmod_cols, w, m, v, ex=None):
    n = c_all.shape[0]
    r, cols = w.shape
    br = 128

    def body(c_ref, d_ref, w_ref, m_ref, v_ref, g_ref, dl_ref, mo_ref, vo_ref):
        cond = _silu(c_ref[...])
        g = lax.dot_general(cond, d_ref[...], (((0,), (0,)), ((), ())), precision=lax.Precision.HIGHEST,
                            preferred_element_type=F32)
        g_ref[...] = g
        dl_ref[...], mo_ref[...], vo_ref[...] = _adamw_math(w_ref[...], g, m_ref[...], v_ref[...])

    spec = pl.BlockSpec((br, cols), lambda i: (i, 0))
    outs, landed = _hosted_call(
        body, "ada_bwd", (r // br,),
        in_specs=[pl.BlockSpec((n, br), lambda i: (0, i)), pl.BlockSpec((n, cols), lambda i: (0, 0)), spec, spec, spec],
        out_specs=[spec] * 4, out_shape=[_sds((r, cols), F32)] * 4, scratch_shapes=[], semantics=("parallel",),
        ins=(c_all, dmod_cols, w, m, v), ex=ex)
    return outs if ex is None else (outs, landed)


def _sum_devices(parts):
    n, r, cols = parts.shape

    def body(p_ref, o_ref):
        acc = p_ref[0]
        for k in range(1, n):
            acc = acc + p_ref[k]
        o_ref[...] = acc

    return pl.pallas_call(body, name="sum_devices", out_shape=_sds((r, cols), F32), compiler_params=_cparams())(parts)


SMALL_ROWS = 16
_SMALL_SLOTS = dict(norm1_g=(0, 0, D), norm2_g=(1, 0, D), q_norm_g=(2, 0, HD), k_norm_g=(2, 128, HD), sinks=(2, 256, HEADS),
                    a_log=(2, 384, DN_H), dt_bias=(2, 512, DN_H), dn_norm_g=(2, 640, DN_D))
_CONV_ROW = 4
_ADA_B_ROW = 8


def _pack_small(vals, conv, ada_b):
    def row(pieces):
        out, at = [], 0
        for col, val in pieces:
            out += [jnp.zeros((1, col - at), F32), val.reshape(1, -1)]
            at = col + val.size
        return jnp.concatenate(out + [jnp.zeros((1, CONVW - at), F32)], axis=1)
    rows = {}
    for name, (r, col, n) in _SMALL_SLOTS.items():
        rows.setdefault(r, []).append((col, vals[name]))
    blank = jnp.zeros((1, CONVW), F32)
    top = [row(sorted(rows[r], key=lambda p: p[0])) if r in rows else blank for r in range(_CONV_ROW)]
    conv_rows = jnp.concatenate([conv, jnp.zeros((CONV, CONVW - conv.shape[1]), F32)], axis=1)
    tail = jnp.zeros((SMALL_ROWS - _ADA_B_ROW - 4, CONVW), F32)
    return jnp.concatenate(top + [conv_rows, ada_b.reshape(4, CONVW), tail], axis=0)


def _unpack_small(sheet, conv_cols):
    out = {name: sheet[row, col:col + n].reshape(1, n) for name, (row, col, n) in _SMALL_SLOTS.items()}
    out["conv_w"] = sheet[_CONV_ROW:_CONV_ROW + CONV, 0:conv_cols].reshape(1, CONV, 1, conv_cols)
    out["ada_b"] = sheet[_ADA_B_ROW:_ADA_B_ROW + 4, :].reshape(1, 6 * D)
    return out


def _w_in_segments():
    shard = IN_WIDTH // N_CHIP
    cuts = sorted({0, IN_WIDTH, C_Z, C_Z + 2 * DN_H} | {k * shard for k in range(1, N_CHIP)})
    segs = []
    for a, b in zip(cuts[:-1], cuts[1:]):
        k = a // shard
        pad = a if a < C_Z else (C_BA + a - C_Z if a < C_Z + 2 * DN_H else a - 2 * DN_H)
        segs.append((k, a - k * shard, b - k * shard, pad))
    return segs


def _pad_w_in(f):
    parts = [f[k][:, lo:hi] for k, lo, hi, _ in sorted(_w_in_segments(), key=lambda s: s[3])]
    return jnp.concatenate(parts + [jnp.zeros((f.shape[1], IN_PAD - IN_WIDTH), f.dtype)], axis=1)


def _unpad_w_in(g):
    return jnp.stack([jnp.concatenate([g[:, pad:pad + hi - lo] for kk, lo, hi, pad in _w_in_segments() if kk == k], axis=1)
                      for k in range(N_CHIP)])


def _blocks_to_cols(f):
    return f.transpose(1, 0, 2).reshape(f.shape[1], N_CHIP * f.shape[2])


def kernel(x, c, positions, ada_w, ada_b, norm1_g, w_in, conv_w, q_norm_g, k_norm_g, sinks, a_log, dt_bias, dn_norm_g, w_branch, w_out, norm2_g, w_gate_up, w_down, loss_target, m_ada_w, m_ada_b, m_norm1_g, m_w_in, m_conv_w, m_q_norm_g, m_k_norm_g, m_sinks, m_a_log, m_dt_bias, m_dn_norm_g, m_w_branch, m_w_out, m_norm2_g, m_w_gate_up, m_w_down, v_ada_w, v_ada_b, v_norm1_g, v_w_in, v_conv_w, v_q_norm_g, v_k_norm_g, v_sinks, v_a_log, v_dt_bias, v_dn_norm_g, v_w_branch, v_w_out, v_norm2_g, v_w_gate_up, v_w_down):
    ix, iy, ic = lax.axis_index("x"), lax.axis_index("y"), lax.axis_index("c")
    dev = 4 * ix + 2 * iy + ic
    chip = 2 * ix + iy
    n_seq = x.shape[0]
    conv_cols = conv_w.shape[-1]

    c_all, conv_all = _gather_devices("gather_cond", [c, conv_w.reshape(CONV, conv_cols)], dev)
    c_all = c_all.reshape(N_DEV * n_seq, D)
    ada_cols = ada_w.shape[-1]
    ada_b_cols = lax.dynamic_slice(ada_b, (0, chip * ada_cols), (1, ada_cols))
    mod_cols = _ada_fwd(c_all, ada_w[0], ada_b_cols)
    (mod_blocks,) = _gather_chips("gather_mod", [mod_cols], chip)
    mod_all = _blocks_to_cols(mod_blocks)
    mod = lax.dynamic_slice(mod_all, (dev * n_seq, 0), (n_seq, 6 * D)).reshape(n_seq, 1, 6 * D)
    conv_full = _blocks_to_cols(conv_all[0::2])

    (f_in,) = _gather_weights("w_in", [w_in[0].astype(BF)], chip)
    w_in_pad = _pad_w_in(f_in)

    loss, grad_x, dmod, small, (w_in_grad, r_br, r_out, r_gu, r_dn) = _local_step(
        x, mod, positions, loss_target, norm1_g, w_in_pad, conv_full.reshape(CONV, 1, CONVW), q_norm_g, k_norm_g, sinks,
        a_log, dt_bias, dn_norm_g, w_branch[0].astype(BF), w_out[0].astype(BF), norm2_g, w_gate_up[0].astype(BF),
        w_down[0].astype(BF), dist=(chip, ic))
    loss = lax.psum(loss, ("x", "y", "c"))

    part = _pack_small(small, small["conv_w"], jnp.sum(dmod, axis=(0, 1)).reshape(1, 6 * D))
    dmod_all, parts = _gather_devices("gather_small", [dmod.reshape(n_seq, 6 * D), part], dev, host=w_in_grad)
    dmod_all = dmod_all.reshape(N_DEV * n_seq, 6 * D)
    dmod_cols = lax.dynamic_slice(dmod_all, (0, chip * ada_cols), (N_DEV * n_seq, ada_cols))

    up_gu = _Hosted(lambda ex: _adamw_halves("w_gate_up", w_gate_up[0], *r_gu, m_w_gate_up[0], v_w_gate_up[0], ic, ex))
    up_ada = _Hosted(lambda ex: _ada_bwd(c_all, dmod_cols, ada_w[0], m_ada_w[0], v_ada_w[0], ex))
    up_dn = _Hosted(lambda ex: _adamw_halves("w_down", w_down[0], *r_dn, m_w_down[0], v_w_down[0], ic, ex))
    (r_in,) = _reduce_grads(("w_in",), [_unpad_w_in(w_in_grad.outs)], chip, ic, hosts=[up_gu, up_ada, up_dn])
    ada = up_ada.outs
    big = {"w_gate_up": tuple(up_gu.outs), "w_down": tuple(up_dn.outs)}
    for name, w, g, m, v in (("w_in", w_in, r_in, m_w_in, v_w_in), ("w_branch", w_branch, r_br, m_w_branch, v_w_branch),
                             ("w_out", w_out, r_out, m_w_out, v_w_out)):
        big[name] = tuple(_adamw_halves(name, w[0], *g, m[0], v[0], ic))
    g_small = _unpack_small(_sum_devices(parts), CONVW)
    g_conv = lax.dynamic_slice(g_small["conv_w"].reshape(CONV, CONVW), (0, chip * conv_cols), (CONV, conv_cols))
    g_small["conv_w"] = g_conv.reshape(1, CONV, 1, conv_cols)

    given = dict(norm1_g=(norm1_g, m_norm1_g, v_norm1_g), norm2_g=(norm2_g, m_norm2_g, v_norm2_g),
                 q_norm_g=(q_norm_g, m_q_norm_g, v_q_norm_g), k_norm_g=(k_norm_g, m_k_norm_g, v_k_norm_g),
                 sinks=(sinks, m_sinks, v_sinks), a_log=(a_log, m_a_log, v_a_log), dt_bias=(dt_bias, m_dt_bias, v_dt_bias),
                 dn_norm_g=(dn_norm_g, m_dn_norm_g, v_dn_norm_g))
    sheets = [_pack_small({k: t[j] for k, t in given.items()}, cw.reshape(CONV, conv_cols), ab)
              for j, (cw, ab) in enumerate(((conv_w, ada_b), (m_conv_w, m_ada_b), (v_conv_w, v_ada_b)))]
    g_local = _pack_small(g_small, g_conv, g_small["ada_b"])
    upd = [_unpack_small(s, conv_cols) for s in _adamw("small", sheets[0], g_local, sheets[1], sheets[2])]

    names = ["ada_w", "ada_b", "norm1_g", "w_in", "conv_w", "q_norm_g", "k_norm_g", "sinks", "a_log", "dt_bias", "dn_norm_g",
             "w_branch", "w_out", "norm2_g", "w_gate_up", "w_down"]

    def leaf(name, j):
        if name == "ada_w":
            return ada[j][None]
        if name in big:
            return big[name][j][None]
        return g_small[name] if j == 0 else upd[j - 1][name]

    return (loss, grad_x) + tuple(leaf(n, j) for j in range(4) for n in names)
```

```python
import functools
from typing import Callable, NamedTuple

import jax
import jax.numpy as jnp
import numpy as np
from jax import lax
from jax.experimental import pallas as pl
from jax.experimental.pallas import tpu as pltpu

F32 = jnp.float32
BF = jnp.bfloat16

D = 1024
HEADS = 8
KV_HEADS = 2
GROUP = 4
HD = 64
BLK = 128
ROT = 16
THETA = 500000.0
QW = 512
KVW = 128
DN_H = 4
DN_D = 128
CONV = 4
CHUNK = 64
DNW = 512
CONVW = 1536
FFN = 2816
EPS = 1e-6
IN_WIDTH = 4872
IN_PAD = 4992
C_KV = 512
C_DN = 768
C_Z = 2304
C_GA = 2816
C_GB = 3840
C_BA = 4864
NEG = -1e30
N_DEV = 8
N_CHIP = 4

ADAM_LR = 0.001
ADAM_B1 = 0.9
ADAM_B2 = 0.999
ADAM_EPS = 1e-08
ADAM_WD = 0.01
ADAM_STEP = 10

VMEM_LIMIT = 60 * 1024 * 1024


def _cparams(**kw):
    return pltpu.CompilerParams(vmem_limit_bytes=VMEM_LIMIT, **kw)


def _dg(a, b, ca, cb):
    return lax.dot_general(a.astype(BF), b.astype(BF), (((ca,), (cb,)), ((), ())),
                           preferred_element_type=F32)


@jax.custom_vjp
def _mm(a, b):
    return _dg(a, b, 1, 0)


def _mm_fwd(a, b):
    return _dg(a, b, 1, 0), (a, b)


def _mm_bwd(res, dy):
    a, b = res
    return _dg(dy, b, 1, 1).astype(a.dtype), _dg(a, dy, 0, 0).astype(b.dtype)


_mm.defvjp(_mm_fwd, _mm_bwd)


@jax.custom_vjp
def _mm_nt(a, b):
    return _dg(a, b, 1, 1)


def _mm_nt_fwd(a, b):
    return _dg(a, b, 1, 1), (a, b)


def _mm_nt_bwd(res, dy):
    a, b = res
    return _dg(dy, b, 1, 0).astype(a.dtype), _dg(dy, a, 0, 0).astype(b.dtype)


_mm_nt.defvjp(_mm_nt_fwd, _mm_nt_bwd)


@jax.custom_vjp
def _mm_tn(a, b):
    return _dg(a, b, 0, 0)


def _mm_tn_fwd(a, b):
    return _dg(a, b, 0, 0), (a, b)


def _mm_tn_bwd(res, dy):
    a, b = res
    return _dg(b, dy, 1, 1).astype(a.dtype), _dg(a, dy, 1, 0).astype(b.dtype)


_mm_tn.defvjp(_mm_tn_fwd, _mm_tn_bwd)


def _mmx(a, b):
    return jnp.dot(a, b, precision=lax.Precision.HIGHEST, preferred_element_type=F32)


def _mmx_nt(a, b):
    return lax.dot_general(a, b, (((1,), (1,)), ((), ())), precision=lax.Precision.HIGHEST,
                           preferred_element_type=F32)


def _iota(shape, dim):
    return lax.broadcasted_iota(jnp.int32, shape, dim)


def _sigmoid(x):
    return lax.logistic(x)


def _silu(x):
    return x * _sigmoid(x)


def _softplus(x):
    return jnp.maximum(x, 0.0) + jnp.log(1.0 + jnp.exp(-jnp.abs(x)))


def _rms(x, gain):
    return x * lax.rsqrt(jnp.mean(x * x, axis=-1, keepdims=True) + EPS) * gain


def _norm_mod(x, gain, shift, scale):
    return _rms(x, gain) * (1.0 + scale) + shift


def _split(a):
    hi = a.astype(BF)
    return hi, (a - hi.astype(F32)).astype(BF)


def _dg2(a, c, ca, cb):
    ah, al = _split(a)
    c = c.astype(BF)

    def dg(x):
        return lax.dot_general(x, c, (((ca,), (cb,)), ((), ())), preferred_element_type=F32)
    return dg(ah) + dg(al)


@jax.custom_vjp
def _mmc(a, c):
    return _dg2(a, c, 1, 0)


def _mmc_fwd(a, c):
    return _dg2(a, c, 1, 0), c


def _mmc_bwd(c, dy):
    return _dg2(dy, c, 1, 1), jnp.zeros_like(c)


_mmc.defvjp(_mmc_fwd, _mmc_bwd)


def _qk_prep(slabs, gain, cos, sin):
    r = _iota((2 * HD, 2 * HD), 0)
    c = _iota((2 * HD, 2 * HD), 1)
    seg = jnp.where(r // HD == c // HD, 1.0 / HD, 0.0).astype(F32)
    half = ROT // 2
    cd = c % HD
    pair = jnp.where(((cd < half) & (r == c + half)) | ((cd >= half) & (cd < ROT) & (r == c - half)), 1.0, 0.0).astype(F32)
    out = []
    for x in slabs:
        y = x * lax.rsqrt(_mmc(x * x, seg) + EPS) * gain
        out.append(y * cos + _mmc(y, pair) * sin)
    return out


def _attn_block(qs, kc, kp, vc, vp, sinks, has_prev):
    rows = GROUP * BLK
    qi = _iota((rows, 2 * BLK), 0) % BLK + BLK
    kj = _iota((rows, 2 * BLK), 1)
    dist = qi - kj
    valid = (dist >= 0) & (dist < BLK) & ((kj >= BLK) | has_prev)
    grp = _iota((rows, HEADS), 0) // BLK
    col = _iota((rows, HEADS), 1)

    outs = []
    for h in range(KV_HEADS):
        q = jnp.concatenate([qs[h * GROUP + g] for g in range(GROUP)], axis=0)
        k = jnp.concatenate([kp[h], kc[h]], axis=0)
        v = jnp.concatenate([vp[h], vc[h]], axis=0)
        s = _mm_nt(q, k) * (HD ** -0.5)
        s = jnp.where(valid, s, NEG)
        sink = jnp.sum(jnp.where(col == h * GROUP + grp, sinks, 0.0), axis=-1, keepdims=True)
        m = lax.stop_gradient(jnp.maximum(jnp.max(s, axis=-1, keepdims=True), sink))
        p = jnp.exp(s - m)
        probs = p / (jnp.sum(p, axis=-1, keepdims=True) + jnp.exp(sink - m))
        o = _mm(probs, v)
        outs += [o[g * BLK:(g + 1) * BLK] for g in range(GROUP)]
    return outs


def _dn_act(y, normalize):
    s = _silu(y)
    return s * lax.rsqrt(jnp.sum(s * s, axis=-1, keepdims=True) + EPS) if normalize else s


def _dn_gates(ba, alog, dtb):
    lane = _iota(ba.shape, 1)
    beta = _sigmoid(ba)
    g = -jnp.exp(alog) * _softplus(ba + dtb)
    return jnp.where(lane < DN_H, beta, jnp.where(lane < 2 * DN_H, g, 0.0))


def _bdg(a, b, ca, cb):
    return lax.dot_general(a.astype(BF), b.astype(BF), (((ca,), (cb,)), ((0,), (0,))), preferred_element_type=F32)


@jax.custom_vjp
def _bmm(a, b):
    return _bdg(a, b, 2, 1)


def _bmm_fwd(a, b):
    return _bdg(a, b, 2, 1), (a, b)


def _bmm_bwd(res, dy):
    a, b = res
    return _bdg(dy, b, 2, 2), _bdg(a, dy, 1, 1)


_bmm.defvjp(_bmm_fwd, _bmm_bwd)


@jax.custom_vjp
def _bmm_nt(a, b):
    return _bdg(a, b, 2, 2)


def _bmm_nt_fwd(a, b):
    return _bdg(a, b, 2, 2), (a, b)


def _bmm_nt_bwd(res, dy):
    a, b = res
    return _bdg(dy, b, 2, 1), _bdg(dy, a, 1, 1)


_bmm_nt.defvjp(_bmm_nt_fwd, _bmm_nt_bwd)


def _bmmx(a, b):
    return lax.dot_general(a, b, (((2,), (1,)), ((0,), (0,))), precision=lax.Precision.HIGHEST,
                           preferred_element_type=F32)


def _tri_times(x, transpose):
    C = x.shape[1]
    tri = jnp.broadcast_to((_iota((C, C), 0) >= _iota((C, C), 1)).astype(BF)[None], (x.shape[0], C, C))
    hi = x.astype(BF)
    mid, lo = _split(x - hi.astype(F32))
    dims = (((1,) if transpose else (2,), (1,)), ((0,), (0,)))
    return sum(lax.dot_general(tri, part, dims, preferred_element_type=F32) for part in (hi, mid, lo))


@jax.custom_vjp
def _running_sum(x):
    return _tri_times(x, False)


def _running_sum_fwd(x):
    return _tri_times(x, False), None


def _running_sum_bwd(_, dy):
    return (_tri_times(dy, True),)


_running_sum.defvjp(_running_sum_fwd, _running_sum_bwd)


def _neumann_inverse(lmat):
    C = CHUNK
    eye = jnp.where(_iota((C, C), 0) == _iota((C, C), 1), 1.0, 0.0).astype(F32)[None]
    a = -lmat
    tinv = eye + a
    pw = _bmmx(a, a)
    for _ in range(4):
        both = _bmmx(jnp.concatenate([pw, tinv], axis=1), pw)
        pw, tinv = both[:, :C], tinv + both[:, C:]
    return tinv + _bmmx(tinv, pw)


def _inverse_bwd(tinv, d_tinv):
    x = lax.dot_general(d_tinv, tinv, (((2,), (2,)), ((0,), (0,))), precision=lax.Precision.HIGHEST,
                        preferred_element_type=F32)
    return -lax.dot_general(tinv, x, (((1,), (1,)), ((0,), (0,))), precision=lax.Precision.HIGHEST,
                            preferred_element_type=F32)


@jax.custom_vjp
def _tri_inverse(lmat):
    return _neumann_inverse(lmat)


def _tri_inverse_fwd(lmat):
    tinv = _neumann_inverse(lmat)
    return tinv, tinv


def _tri_inverse_bwd(tinv, d_tinv):
    return (_inverse_bwd(tinv, d_tinv),)


_tri_inverse.defvjp(_tri_inverse_fwd, _tri_inverse_bwd)


@jax.custom_vjp
def _tri_inverse_known(lmat, tinv):
    return tinv


def _tri_inverse_known_fwd(lmat, tinv):
    return tinv, tinv


def _tri_inverse_known_bwd(tinv, d_tinv):
    return _inverse_bwd(tinv, d_tinv), jnp.zeros_like(tinv)


_tri_inverse_known.defvjp(_tri_inverse_known_fwd, _tri_inverse_known_bwd)


def _dn_intra(q, k, v, bg, tinv=None):
    C = CHUNK
    G = bg.shape[0]
    r = _iota((C, C), 0)
    c = _iota((C, C), 1)
    incl = (r >= c)[None]
    strict = (r > c)[None]
    eye = jnp.where(r == c, 1.0, 0.0).astype(F32)[None]
    gc_all = _running_sum(bg)
    lane = _iota((C, DN_D), 1)

    def per_head(x, offset):
        return jnp.concatenate([jnp.sum(jnp.where(lane == offset + h, x[g], 0.0), axis=-1, keepdims=True)[None]
                                for g in range(G) for h in range(DN_H)], axis=0)
    beta = per_head(bg, 0)
    gcol = per_head(gc_all, DN_H)
    grow = jnp.sum(eye * gcol, axis=1, keepdims=True)
    glast = jnp.sum(jnp.where(_iota((1, C, 1), 1) == C - 1, gcol, 0.0), axis=1, keepdims=True)
    decay = jnp.exp(jnp.where(incl, gcol - grow, NEG))
    q = q * (DN_D ** -0.5)
    kb = k * beta
    lmat = jnp.where(strict, _bmm_nt(kb, k) * decay, 0.0)
    tinv = _tri_inverse(lmat) if tinv is None else _tri_inverse_known(lmat, tinv)
    egc = jnp.exp(gcol)
    u = _bmm(tinv, v * beta)
    w = _bmm(tinv, kb * egc)
    a = _bmm_nt(q, k) * decay
    return u, w, q * egc, k * jnp.exp(glast - gcol), a, jnp.exp(glast), tinv


@jax.custom_vjp
def _bmm_tn(a, b):
    return _bdg(a, b, 1, 1)


def _bmm_tn_fwd(a, b):
    return _bdg(a, b, 1, 1), (a, b)


def _bmm_tn_bwd(res, dy):
    a, b = res
    return _bdg(b, dy, 2, 2), _bdg(a, dy, 2, 1)


_bmm_tn.defvjp(_bmm_tn_fwd, _bmm_tn_bwd)


def _dn_rec(state, u, w, qd, kd, a, cd):
    v_new = u - _bmm(w, state)
    out = _bmm(qd, state) + _bmm(a, v_new)
    return state * cd + _bmm_tn(kd, v_new), out


def _mix_tile(o_attn, o_raw, zs, ga, gb, x, gate1, dn_g, wb_a, wb_d, w_out, p_ya, p_yd, p_out):
    o_dn = jnp.concatenate([_rms(o_raw[h], dn_g) * _silu(zs[h]) for h in range(DN_H)], axis=-1)
    y_a = _mm(o_attn, wb_a) + p_ya
    y_d = _mm(o_dn, wb_d) + p_yd
    merged = _sigmoid(ga) * y_a + _sigmoid(gb) * y_d
    out = _mm(merged, w_out) + p_out
    return x + gate1 * out, o_dn, merged


def _mlp_tile(x1, gain, shift, scale, gate2, w_gu, w_dn, tgt, p_gu, p_yy):
    h2 = _norm_mod(x1, gain, shift, scale)
    gu = jnp.concatenate([_mm(h2, w) for w in w_gu], axis=-1) + p_gu
    act = _silu(gu[:, :FFN]) * gu[:, FFN:]
    yy = _mm(act, w_dn) + p_yy
    y = x1 + gate2 * yy
    err = y - tgt
    return 0.5 * jnp.sum(err * err) * (1.0 / D), (h2, act)


def _tok(bt, f):
    return pl.BlockSpec((None, bt, f), lambda b, i: (b, i, 0))


def _full(shape):
    return pl.BlockSpec(shape, lambda b, i: (0,) * len(shape))


def _resident(shape):
    return pl.BlockSpec(shape, lambda b, i: (0,) * len(shape), pipeline_mode=pl.Buffered(1))


def _per_batch(f):
    return pl.BlockSpec((None, 1, f), lambda b, i: (b, 0, 0))


def _sds(shape, dtype):
    return jax.ShapeDtypeStruct(shape, dtype)


class _Exchange(NamedTuple):
    ins: tuple
    out_shapes: tuple
    n_remote: int
    plan: Callable
    n_forward: int = 0
    forward: Callable = None


def _remote_copies(remote, send_sems, recv_sems):
    return [pltpu.make_async_remote_copy(src_ref=src, dst_ref=dst, send_sem=send_sems.at[i], recv_sem=recv_sems.at[i],
                                         device_id=peer, device_id_type=pl.DeviceIdType.MESH)
            for i, (src, dst, peer) in enumerate(remote)]


def _exchange_copies(ex, in_refs, out_refs, send_sems, recv_sems):
    remote = ex.plan((lax.axis_index("x"), lax.axis_index("y"), lax.axis_index("c")), in_refs, out_refs)
    assert len(remote) == ex.n_remote
    return _remote_copies(remote, send_sems, recv_sems)


def _forward_copies(ex, out_refs, send_sems, recv_sems):
    remote = ex.forward((lax.axis_index("x"), lax.axis_index("y"), lax.axis_index("c")), out_refs)
    assert len(remote) == ex.n_forward
    return _remote_copies(remote, send_sems, recv_sems)


def _exchange_sems(ex):
    sems = [pltpu.SemaphoreType.DMA((ex.n_remote,)), pltpu.SemaphoreType.DMA((ex.n_remote,))]
    if ex.n_forward:
        sems += [pltpu.SemaphoreType.DMA((ex.n_forward,)), pltpu.SemaphoreType.DMA((ex.n_forward,))]
    return sems


def _hosted_call(body, name, grid, in_specs, out_specs, out_shape, scratch_shapes, semantics, ins, ex=None):
    if ex is None:
        outs = pl.pallas_call(body, name=name, grid=grid, in_specs=in_specs, out_specs=out_specs, out_shape=out_shape,
                              scratch_shapes=scratch_shapes,
                              compiler_params=_cparams(dimension_semantics=semantics))(*ins)
        return outs, ()
    n_in, n_out, n_scr = len(ins), len(out_shape), len(scratch_shapes)
    c_in, c_out = len(ex.ins), len(ex.out_shapes)
    steps = 1
    for g in grid:
        steps *= g

    def wrapped(*refs):
        a, b, c, d = n_in, n_in + c_in, n_in + c_in + n_out, n_in + c_in + n_out + c_out
        scratch, sems = refs[d:d + n_scr], refs[d + n_scr:]
        step = 0
        for axis, g in enumerate(grid):
            step = step * g + pl.program_id(axis)

        def first_phase():
            return _exchange_copies(ex, refs[a:b], refs[c:d], sems[0], sems[1])

        @pl.when(step == 0)
        def _():
            for cp in first_phase():
                cp.start()
        body(*refs[:a], *refs[b:c], *scratch)

        if ex.n_forward:
            @pl.when(step == (3 * steps) // 4)
            def _():
                for cp in first_phase():
                    cp.wait_recv()
                for cp in _forward_copies(ex, refs[c:d], sems[2], sems[3]):
                    cp.start()

        @pl.when(step == steps - 1)
        def _():
            cps = first_phase()
            if ex.n_forward:
                fwd = _forward_copies(ex, refs[c:d], sems[2], sems[3])
                for cp in fwd:
                    cp.wait_recv()
                for cp in cps + fwd:
                    cp.wait_send()
            else:
                for cp in cps:
                    cp.wait_recv()
                for cp in cps:
                    cp.wait_send()

    any_spec = pl.BlockSpec(memory_space=pl.ANY)
    res = pl.pallas_call(
        wrapped, name=name, grid=grid, in_specs=list(in_specs) + [any_spec] * c_in,
        out_specs=list(out_specs) + [any_spec] * c_out, out_shape=list(out_shape) + list(ex.out_shapes),
        scratch_shapes=list(scratch_shapes) + _exchange_sems(ex),
        compiler_params=_cparams(dimension_semantics=("arbitrary",) * len(grid)),
    )(*ins, *ex.ins)
    return res[:n_out], res[n_out:]


def _acc(ref, val, first):
    @pl.when(first)
    def _():
        ref[...] = val

    @pl.when(jnp.logical_not(first))
    def _():
        ref[...] += val


def _in_proj(x, mod, norm1_g, w_in, bt):
    B, S, _ = x.shape

    def body(x_ref, mod_ref, g_ref, w_ref, q_ref, kv_ref, dn_ref, z_ref, ga_ref, gb_ref, ba_ref, h_ref):
        h = _norm_mod(x_ref[...], g_ref[...], mod_ref[:, 0:D], mod_ref[:, D:2 * D]).astype(BF)
        h_ref[...] = h

        def proj(c0, c1):
            return jnp.dot(h, w_ref[:, c0:c1], preferred_element_type=F32)
        q_ref[...] = proj(0, C_KV).astype(BF)
        kv_ref[...] = proj(C_KV, C_DN).astype(BF)
        dn_ref[...] = proj(C_DN, C_Z).astype(BF)
        z_ref[...] = proj(C_Z, C_GA).astype(BF)
        ga_ref[...] = proj(C_GA, C_GB).astype(BF)
        gb_ref[...] = proj(C_GB, C_BA).astype(BF)
        ba_ref[...] = proj(C_BA, IN_PAD)

    widths = (QW, 2 * KVW, CONVW, DNW, D, D)
    return pl.pallas_call(
        body, name="in_proj", grid=(B, S // bt),
        in_specs=[_tok(bt, D), _per_batch(6 * D), _full((1, D)), _resident((D, IN_PAD))],
        out_specs=[_tok(bt, w) for w in widths] + [_tok(bt, 128), _tok(bt, D)],
        out_shape=[_sds((B, S, w), BF) for w in widths] + [_sds((B, S, 128), F32), _sds((B, S, D), BF)],
        compiler_params=_cparams(dimension_semantics=("parallel", "parallel")),
    )(x, mod, norm1_g, w_in)


def _prev_blk(bt, f):
    return pl.BlockSpec((None, bt, f), lambda b, i: (b, jnp.maximum(i - 1, 0), 0))


QKV = QW + 2 * KVW


def _qk_slabs(q_ref, kv_ref):
    return ([q_ref[:, j * 2 * HD:(j + 1) * 2 * HD].astype(F32) for j in range(QW // (2 * HD))],
            [kv_ref[:, 0:KVW].astype(F32)])


def _qk_prep_fwd(q, kv, cos, sin, qg, kg, bt):
    B, S, _ = q.shape

    def body(q_ref, kv_ref, cos_ref, sin_ref, qg_ref, kg_ref, o_ref):
        qs, ks = _qk_slabs(q_ref, kv_ref)
        qn = _qk_prep(qs, qg_ref[...], cos_ref[...], sin_ref[...])
        kn = _qk_prep(ks, kg_ref[...], cos_ref[...], sin_ref[...])
        for j, t in enumerate(qn + kn):
            o_ref[:, j * 2 * HD:(j + 1) * 2 * HD] = t.astype(BF)
        o_ref[:, QW + KVW:QKV] = kv_ref[:, KVW:2 * KVW]

    return pl.pallas_call(
        body, name="qk_prep_fwd", grid=(B, S // bt),
        in_specs=[_tok(bt, QW), _tok(bt, 2 * KVW), _tok(bt, 2 * HD), _tok(bt, 2 * HD), _full((1, 2 * HD)), _full((1, 2 * HD))],
        out_specs=_tok(bt, QKV), out_shape=_sds((B, S, QKV), BF),
        compiler_params=_cparams(dimension_semantics=("parallel", "parallel")),
    )(q, kv, cos, sin, qg, kg)


def _qk_prep_bwd(q, kv, cos, sin, qg, kg, dqn, dkvn, bt, ex=None):
    B, S, _ = q.shape

    def body(q_ref, kv_ref, cos_ref, sin_ref, qg_ref, kg_ref, dqn_ref, dkvn_ref, dq_ref, dkv_ref, dqg_ref, dkg_ref):
        qs, ks = _qk_slabs(q_ref, kv_ref)
        cos, sin = cos_ref[...], sin_ref[...]

        def f(qs, ks, qg, kg):
            return _qk_prep(qs, qg, cos, sin), _qk_prep(ks, kg, cos, sin)
        _, vjp = jax.vjp(f, qs, ks, qg_ref[...], kg_ref[...])
        n_q = len(qs)
        d_q = [dqn_ref[:, j * 2 * HD:(j + 1) * 2 * HD].astype(F32) for j in range(n_q)]
        d_k = [dkvn_ref[:, 0:KVW].astype(F32)]
        dqs, dks, dqg, dkg = vjp((d_q, d_k))
        for j in range(n_q):
            dq_ref[:, j * 2 * HD:(j + 1) * 2 * HD] = dqs[j].astype(BF)
        dkv_ref[:, 0:KVW] = dks[0].astype(BF)
        dkv_ref[:, KVW:2 * KVW] = dkvn_ref[:, KVW:2 * KVW]
        first = (pl.program_id(0) == 0) & (pl.program_id(1) == 0)
        _acc(dqg_ref, dqg, first)
        _acc(dkg_ref, dkg, first)

    return _hosted_call(
        body, "qk_prep_bwd", (B, S // bt),
        in_specs=[_tok(bt, QW), _tok(bt, 2 * KVW), _tok(bt, 2 * HD), _tok(bt, 2 * HD), _full((1, 2 * HD)), _full((1, 2 * HD)),
                  _tok(bt, QW), _tok(bt, 2 * KVW)],
        out_specs=[_tok(bt, QW), _tok(bt, 2 * KVW), _full((1, 2 * HD)), _full((1, 2 * HD))],
        out_shape=[_sds((B, S, QW), BF), _sds((B, S, 2 * KVW), BF), _sds((1, 2 * HD), F32), _sds((1, 2 * HD), F32)],
        scratch_shapes=[], semantics=("arbitrary", "arbitrary"), ins=(q, kv, cos, sin, qg, kg, dqn, dkvn), ex=ex)


def _attn_load(qkv_ref, kvp_ref):
    qs = [qkv_ref[:, h * HD:(h + 1) * HD].astype(F32) for h in range(HEADS)]
    kc = [qkv_ref[:, QW + h * HD:QW + (h + 1) * HD].astype(F32) for h in range(KV_HEADS)]
    vc = [qkv_ref[:, QW + KVW + h * HD:QW + KVW + (h + 1) * HD].astype(F32) for h in range(KV_HEADS)]
    kp = [kvp_ref[:, h * HD:(h + 1) * HD].astype(F32) for h in range(KV_HEADS)]
    vp = [kvp_ref[:, KVW + h * HD:KVW + (h + 1) * HD].astype(F32) for h in range(KV_HEADS)]
    return qs, kc, kp, vc, vp


def _kv_prev_spec(index):
    return pl.BlockSpec((None, BLK, 2 * KVW), lambda b, i: (b, index(i), QW // (2 * KVW)))


def _attn_fwd(qkv, sinks):
    B, S, _ = qkv.shape

    def body(qkv_ref, kvp_ref, sk_ref, o_ref):
        qs, kc, kp, vc, vp = _attn_load(qkv_ref, kvp_ref)
        outs = _attn_block(qs, kc, kp, vc, vp, sk_ref[...], pl.program_id(1) > 0)
        for h in range(HEADS):
            o_ref[:, h * HD:(h + 1) * HD] = outs[h].astype(BF)

    return pl.pallas_call(
        body, name="attn_fwd", grid=(B, S // BLK),
        in_specs=[_tok(BLK, QKV), _kv_prev_spec(lambda i: jnp.maximum(i - 1, 0)), _full((1, HEADS))],
        out_specs=_tok(BLK, QW), out_shape=_sds((B, S, QW), BF),
        compiler_params=_cparams(dimension_semantics=("parallel", "parallel")),
    )(qkv, qkv, sinks)


def _halo_spec(bt):
    return pl.BlockSpec((None, 8, CONVW), lambda b, i: (b, jnp.maximum(i * (bt // 8) - 1, 0), 0))


def _dn_prep(dn, ba, conv_w, alog, dtb, bt):
    B, S, _ = dn.shape

    strip_rows = min(bt, 64)

    def body(x_ref, halo_ref, ba_ref, cw_ref, al_ref, dt_ref, qkv_ref, bg_ref, y_ref, xe_ref):
        xe_ref[0:8, :] = jnp.where(pl.program_id(1) == 0, 0.0, halo_ref[...].astype(F32))
        xe_ref[8:bt + 8, :] = x_ref[...].astype(F32)

        def strip(k, carry):
            r0 = pl.multiple_of(k * strip_rows, strip_rows)
            rows = pl.ds(r0, strip_rows)
            for j in range(3 * DN_H):
                cols = slice(j * DN_D, (j + 1) * DN_D)
                window = xe_ref[pl.ds(r0, strip_rows + 8), cols]
                y = cw_ref[0:1, cols] * window[5:strip_rows + 5]
                for t in range(1, CONV):
                    y = y + cw_ref[t:t + 1, cols] * window[5 + t:strip_rows + 5 + t]
                y_ref[rows, cols] = y.astype(BF)
                qkv_ref[rows, cols] = _dn_act(y, j < 2 * DN_H)
            bg_ref[rows, :] = _dn_gates(ba_ref[rows, :], al_ref[...], dt_ref[...])
            return carry
        lax.fori_loop(0, bt // strip_rows, strip, 0)

    return pl.pallas_call(
        body, name="dn_prep", grid=(B, S // bt),
        in_specs=[_tok(bt, CONVW), _halo_spec(bt), _tok(bt, 128), _full((CONV, CONVW)), _full((1, 128)), _full((1, 128))],
        out_specs=[_tok(bt, CONVW), _tok(bt, 128), _tok(bt, CONVW)],
        out_shape=[_sds((B, S, CONVW), F32), _sds((B, S, 128), F32), _sds((B, S, CONVW), BF)],
        scratch_shapes=[pltpu.VMEM((bt + 8, CONVW), F32)],
        compiler_params=_cparams(dimension_semantics=("parallel", "arbitrary")),
    )(dn, dn, ba, conv_w, alog, dtb)


def _dn_load(qkv_ref):
    qs = [qkv_ref[:, h * DN_D:(h + 1) * DN_D] for h in range(DN_H)]
    ks = [qkv_ref[:, DNW + h * DN_D:DNW + (h + 1) * DN_D] for h in range(DN_H)]
    vs = [qkv_ref[:, 2 * DNW + h * DN_D:2 * DNW + (h + 1) * DN_D] for h in range(DN_H)]
    return qs, ks, vs


DN_GROUP = 4
AW = DN_H * CHUNK


def _stack_heads(ref, G, offset, width):
    return jnp.stack([ref[g * CHUNK:(g + 1) * CHUNK, offset + h * width:offset + (h + 1) * width]
                      for g in range(G) for h in range(DN_H)])


def _dn_load_stack(qkv_ref, G):
    return tuple(_stack_heads(qkv_ref, G, j * DNW, DN_D) for j in range(3))


def _cd_spec(n):
    return pl.BlockSpec((None, n, 1, DN_D), lambda b, i: (b, i, 0, 0))


def _dn_intra_fwd(qkv, bg, ex=None):
    B, S, _ = qkv.shape
    nc = S // CHUNK
    G = min(DN_GROUP, nc)
    rows = G * CHUNK

    def body(qkv_ref, bg_ref, u_ref, w_ref, qd_ref, kd_ref, a_ref, cd_ref, t_ref):
        q, k, v = _dn_load_stack(qkv_ref, G)
        u, w, qd, kd, a, cd, tinv = _dn_intra(q, k, v, bg_ref[...].reshape(G, CHUNK, DN_D))
        lane_row = _iota((1, DN_D), 1)
        for g in range(G):
            rows = slice(g * CHUNK, (g + 1) * CHUNK)
            cd_row = jnp.zeros((1, DN_D), F32)
            for h in range(DN_H):
                n = g * DN_H + h
                cols = slice(h * DN_D, (h + 1) * DN_D)
                u_ref[rows, cols] = u[n]
                w_ref[rows, cols] = w[n].astype(BF)
                qd_ref[rows, cols] = qd[n].astype(BF)
                kd_ref[rows, cols] = kd[n].astype(BF)
                a_ref[rows, h * CHUNK:(h + 1) * CHUNK] = a[n].astype(BF)
                t_ref[rows, h * CHUNK:(h + 1) * CHUNK] = tinv[n]
                cd_row = cd_row + jnp.where(lane_row == h, cd[n], 0.0)
            cd_ref[g] = cd_row

    return _hosted_call(
        body, "dn_intra_fwd", (B, nc // G),
        in_specs=[_tok(rows, CONVW), _tok(rows, 128)],
        out_specs=[_tok(rows, DNW)] * 4 + [_tok(rows, AW), _cd_spec(G), _tok(rows, AW)],
        out_shape=[_sds((B, S, DNW), F32)] + [_sds((B, S, DNW), BF)] * 3 + [_sds((B, S, AW), BF), _sds((B, nc, 1, DN_D), F32),
                                                                            _sds((B, S, AW), F32)],
        scratch_shapes=[], semantics=("parallel", "parallel"), ins=(qkv, bg), ex=ex)


REC_GROUP = 8


def _rec_stack(ref, B, width, c):
    rows = slice(c * CHUNK, (c + 1) * CHUNK)
    return jnp.stack([ref[b, rows, h * width:(h + 1) * width].astype(F32) for b in range(B) for h in range(DN_H)])


def _rec_load(B, u_ref, w_ref, qd_ref, kd_ref, a_ref, cd_ref, c):
    lane_row = _iota((1, DN_D), 1)
    cd = jnp.stack([jnp.sum(jnp.where(lane_row == h, cd_ref[b, c], 0.0), axis=-1, keepdims=True)
                    for b in range(B) for h in range(DN_H)])
    return (_rec_stack(u_ref, B, DN_D, c), _rec_stack(w_ref, B, DN_D, c), _rec_stack(qd_ref, B, DN_D, c),
            _rec_stack(kd_ref, B, DN_D, c), _rec_stack(a_ref, B, CHUNK, c), cd)


def _rec_store(B, ref, val, width, c):
    for b in range(B):
        for h in range(DN_H):
            ref[b, c * CHUNK:(c + 1) * CHUNK, h * width:(h + 1) * width] = val[b * DN_H + h]


def _rec_specs(B, R, index):
    def tok(f):
        return pl.BlockSpec((B, R * CHUNK, f), lambda i: (0, index(i), 0))
    cd = pl.BlockSpec((B, R, 1, DN_D), lambda i: (0, index(i), 0, 0))
    st = pl.BlockSpec((B, R, DN_H, DN_D, DN_D), lambda i: (0, index(i), 0, 0, 0))
    return tok, cd, st


def _dn_rec_fwd(u, w, qd, kd, a, cd, ex=None):
    B, S, _ = u.shape
    nc = S // CHUNK
    R = REC_GROUP if nc % REC_GROUP == 0 else 1
    tok, cd_spec, st_spec = _rec_specs(B, R, lambda i: i)

    def body(u_ref, w_ref, qd_ref, kd_ref, a_ref, cd_ref, o_ref, st_ref, s_ref):
        @pl.when(pl.program_id(0) == 0)
        def _():
            s_ref[...] = jnp.zeros_like(s_ref)
        state = s_ref[...]
        for c in range(R):
            st_ref[:, c] = state.reshape(B, DN_H, DN_D, DN_D)
            state, out = _dn_rec(state, *_rec_load(B, u_ref, w_ref, qd_ref, kd_ref, a_ref, cd_ref, c))
            _rec_store(B, o_ref, out, DN_D, c)
        s_ref[...] = state

    return _hosted_call(
        body, "dn_rec_fwd", (nc // R,),
        in_specs=[tok(DNW)] * 4 + [tok(AW), cd_spec],
        out_specs=[tok(DNW), st_spec],
        out_shape=[_sds((B, S, DNW), F32), _sds((B, nc, DN_H, DN_D, DN_D), F32)],
        scratch_shapes=[pltpu.VMEM((B * DN_H, DN_D, DN_D), F32)],
        semantics=("arbitrary",), ins=(u, w, qd, kd, a, cd), ex=ex)


def _mix_load(oa_ref, or_ref, z_ref):
    o_raw = [or_ref[:, h * DN_D:(h + 1) * DN_D] for h in range(DN_H)]
    zs = [z_ref[:, h * DN_D:(h + 1) * DN_D].astype(F32) for h in range(DN_H)]
    return oa_ref[...].astype(F32), o_raw, zs


def _mix_fwd(o_attn, o_raw, z, ga, gb, x, mod, dn_g, w_branch, w_out, bt):
    B, S, _ = x.shape

    def body(oa_ref, or_ref, z_ref, ga_ref, gb_ref, x_ref, mod_ref, dg_ref, wb_ref, wo_ref, x1_ref, od_ref, mg_ref):
        oa, o_r, zs = _mix_load(oa_ref, or_ref, z_ref)
        x1, o_dn, merged = _mix_tile(oa, o_r, zs, ga_ref[...].astype(F32), gb_ref[...].astype(F32), x_ref[...],
                                     mod_ref[:, 2 * D:3 * D], dg_ref[...], wb_ref[0:QW, :], wb_ref[QW:2 * QW, :],
                                     wo_ref[...], 0.0, 0.0, 0.0)
        x1_ref[...] = x1
        od_ref[...] = o_dn.astype(BF)
        mg_ref[...] = merged.astype(BF)

    return pl.pallas_call(
        body, name="mix_fwd", grid=(B, S // bt),
        in_specs=[_tok(bt, QW), _tok(bt, DNW), _tok(bt, DNW), _tok(bt, D), _tok(bt, D), _tok(bt, D), _per_batch(6 * D),
                  _full((1, DN_D)), _resident((D, D)), _resident((D, D))],
        out_specs=[_tok(bt, D), _tok(bt, DNW), _tok(bt, D)],
        out_shape=[_sds((B, S, D), F32), _sds((B, S, DNW), BF), _sds((B, S, D), BF)],
        compiler_params=_cparams(dimension_semantics=("parallel", "parallel")),
    )(o_attn, o_raw, z, ga, gb, x, mod, dn_g, w_branch, w_out)


def _mlp(x1, tgt, mod, norm2_g, w_gu, w_dn, bt):
    B, S, _ = x1.shape

    def body(x1_ref, t_ref, mod_ref, g_ref, wgu_ref, wdn_ref,
             dx1_ref, h2_ref, act_ref, dgu_ref, dyy_ref, loss_ref, dmod_ref, dg_ref):
        w_gu_v, w_dn_v, t = [wgu_ref[k] for k in range(N_CHIP)], wdn_ref[...], t_ref[...]

        def f(x1, gain, shift, scale, gate2, p_gu, p_yy):
            return _mlp_tile(x1, gain, shift, scale, gate2, w_gu_v, w_dn_v, t, p_gu, p_yy)
        zero_gu = jnp.zeros((bt, 2 * FFN), F32)
        zero_yy = jnp.zeros((bt, D), F32)
        loss, vjp, (h2, act) = jax.vjp(f, x1_ref[...], g_ref[...], mod_ref[:, 3 * D:4 * D], mod_ref[:, 4 * D:5 * D],
                                       mod_ref[:, 5 * D:6 * D], zero_gu, zero_yy, has_aux=True)
        dx1, dgain, dshift, dscale, dgate2, dgu, dyy = vjp(jnp.ones((), F32))
        dx1_ref[...] = dx1
        h2_ref[...] = h2.astype(BF)
        act_ref[...] = act.astype(BF)
        dgu_ref[...] = dgu.astype(BF)
        dyy_ref[...] = dyy.astype(BF)
        first = pl.program_id(1) == 0
        _acc(loss_ref, jnp.reshape(loss, (1, 1)), first)
        _acc(dmod_ref, jnp.concatenate([dshift, dscale, dgate2], axis=-1), first)
        _acc(dg_ref, dgain, first)

    return pl.pallas_call(
        body, name="mlp", grid=(B, S // bt),
        in_specs=[_tok(bt, D), _tok(bt, D), _per_batch(6 * D), _full((1, D)), _resident((N_CHIP, D, 2 * FFN // N_CHIP)),
                  _resident((FFN, D))],
        out_specs=[_tok(bt, D), _tok(bt, D), _tok(bt, FFN), _tok(bt, 2 * FFN), _tok(bt, D),
                   _per_batch(1), _per_batch(3 * D), _per_batch(D)],
        out_shape=[_sds((B, S, D), F32), _sds((B, S, D), BF), _sds((B, S, FFN), BF), _sds((B, S, 2 * FFN), BF),
                   _sds((B, S, D), BF), _sds((B, 1, 1), F32), _sds((B, 1, 3 * D), F32), _sds((B, 1, D), F32)],
        compiler_params=_cparams(dimension_semantics=("parallel", "arbitrary")),
    )(x1, tgt, mod, norm2_g, w_gu, w_dn)


def _mix_bwd(o_attn, o_raw, z, ga, gb, x, mod, dn_g, w_branch, w_out, dx1, bt, ex=None):
    B, S, _ = x.shape

    def body(oa_ref, or_ref, z_ref, ga_ref, gb_ref, x_ref, mod_ref, dg_ref, wb_ref, wo_ref, dx1_ref,
             doa_ref, dor_ref, dz_ref, dga_ref, dgb_ref, dya_ref, dyd_ref, dout_ref, dgate_ref, ddg_ref):
        oa, o_r, zs = _mix_load(oa_ref, or_ref, z_ref)
        wb_a, wb_d, wo = wb_ref[0:QW, :], wb_ref[QW:2 * QW, :], wo_ref[...]

        def f(oa, o_r, zs, ga, gb, gate1, dn_g, p_ya, p_yd, p_out):
            return _mix_tile(oa, o_r, zs, ga, gb, x_ref[...], gate1, dn_g, wb_a, wb_d, wo, p_ya, p_yd, p_out)[0]
        zero = jnp.zeros((bt, D), F32)
        _, vjp = jax.vjp(f, oa, o_r, zs, ga_ref[...].astype(F32), gb_ref[...].astype(F32), mod_ref[:, 2 * D:3 * D],
                         dg_ref[...], zero, zero, zero)
        doa, dor, dzs, dga, dgb, dgate1, ddn_g, dya, dyd, dout = vjp(dx1_ref[...])
        doa_ref[...] = doa
        for h in range(DN_H):
            dor_ref[:, h * DN_D:(h + 1) * DN_D] = dor[h]
            dz_ref[:, h * DN_D:(h + 1) * DN_D] = dzs[h].astype(BF)
        dga_ref[...] = dga.astype(BF)
        dgb_ref[...] = dgb.astype(BF)
        dya_ref[...] = dya.astype(BF)
        dyd_ref[...] = dyd.astype(BF)
        dout_ref[...] = dout.astype(BF)
        first = pl.program_id(1) == 0
        _acc(dgate_ref, dgate1, first)
        _acc(ddg_ref, ddn_g, first)

    return _hosted_call(
        body, "mix_bwd", (B, S // bt),
        in_specs=[_tok(bt, QW), _tok(bt, DNW), _tok(bt, DNW), _tok(bt, D), _tok(bt, D), _tok(bt, D), _per_batch(6 * D),
                  _full((1, DN_D)), _resident((D, D)), _resident((D, D)), _tok(bt, D)],
        out_specs=[_tok(bt, QW), _tok(bt, DNW), _tok(bt, DNW), _tok(bt, D), _tok(bt, D), _tok(bt, D), _tok(bt, D), _tok(bt, D),
                   _per_batch(D), _per_batch(DN_D)],
        out_shape=[_sds((B, S, QW), F32), _sds((B, S, DNW), F32), _sds((B, S, DNW), BF), _sds((B, S, D), BF),
                   _sds((B, S, D), BF), _sds((B, S, D), BF), _sds((B, S, D), BF), _sds((B, S, D), BF),
                   _sds((B, 1, D), F32), _sds((B, 1, DN_D), F32)],
        scratch_shapes=[], semantics=("parallel", "arbitrary"),
        ins=(o_attn, o_raw, z, ga, gb, x, mod, dn_g, w_branch, w_out, dx1), ex=ex)


def _dn_rec_bwd(u, w, qd, kd, a, cd, states, d_o, ex=None):
    B, S, _ = u.shape
    nc = S // CHUNK
    R = REC_GROUP if nc % REC_GROUP == 0 else 1
    tok, cd_spec, st_spec = _rec_specs(B, R, lambda i: nc // R - 1 - i)

    def body(u_ref, w_ref, qd_ref, kd_ref, a_ref, cd_ref, st_ref, do_ref,
             du_ref, dw_ref, dqd_ref, dkd_ref, da_ref, dcd_ref, ds_ref):
        @pl.when(pl.program_id(0) == 0)
        def _():
            ds_ref[...] = jnp.zeros_like(ds_ref)
        lane_row = _iota((1, DN_D), 1)
        d_state = ds_ref[...]
        for c in reversed(range(R)):
            state = st_ref[:, c].reshape(B * DN_H, DN_D, DN_D)
            _, vjp = jax.vjp(_dn_rec, state, *_rec_load(B, u_ref, w_ref, qd_ref, kd_ref, a_ref, cd_ref, c))
            d_state, du, dw, dqd, dkd, da, dcd = vjp((d_state, _rec_stack(do_ref, B, DN_D, c)))
            for ref, val, width in ((du_ref, du, DN_D), (dw_ref, dw, DN_D), (dqd_ref, dqd, DN_D), (dkd_ref, dkd, DN_D),
                                    (da_ref, da, CHUNK)):
                _rec_store(B, ref, val, width, c)
            for b in range(B):
                row = jnp.zeros((1, DN_D), F32)
                for h in range(DN_H):
                    row = row + jnp.where(lane_row == h, dcd[b * DN_H + h], 0.0)
                dcd_ref[b, c] = row
        ds_ref[...] = d_state

    return _hosted_call(
        body, "dn_rec_bwd", (nc // R,),
        in_specs=[tok(DNW)] * 4 + [tok(AW), cd_spec, st_spec, tok(DNW)],
        out_specs=[tok(DNW)] * 4 + [tok(AW), cd_spec],
        out_shape=[_sds((B, S, DNW), F32)] * 4 + [_sds((B, S, AW), F32), _sds((B, nc, 1, DN_D), F32)],
        scratch_shapes=[pltpu.VMEM((B * DN_H, DN_D, DN_D), F32)],
        semantics=("arbitrary",), ins=(u, w, qd, kd, a, cd, states, d_o), ex=ex)


def _dn_intra_bwd(qkv, bg, tinv, du, dw, dqd, dkd, da, dcd, ex=None):
    B, S, _ = qkv.shape
    nc = S // CHUNK
    G = min(DN_GROUP, nc)
    rows = G * CHUNK

    def body(qkv_ref, bg_ref, t_ref, du_ref, dw_ref, dqd_ref, dkd_ref, da_ref, dcd_ref, dqkv_ref, dbg_ref):
        q, k, v = _dn_load_stack(qkv_ref, G)
        known = _stack_heads(t_ref, G, 0, CHUNK)
        _, vjp = jax.vjp(lambda q, k, v, bg: _dn_intra(q, k, v, bg, known)[:6], q, k, v,
                         bg_ref[...].reshape(G, CHUNK, DN_D))
        lane_row = _iota((1, DN_D), 1)
        dcd = jnp.stack([jnp.sum(jnp.where(lane_row == h, dcd_ref[g], 0.0), axis=-1, keepdims=True)
                         for g in range(G) for h in range(DN_H)])
        dq, dk, dv, dbg = vjp((_stack_heads(du_ref, G, 0, DN_D), _stack_heads(dw_ref, G, 0, DN_D),
                               _stack_heads(dqd_ref, G, 0, DN_D), _stack_heads(dkd_ref, G, 0, DN_D),
                               _stack_heads(da_ref, G, 0, CHUNK), dcd))
        for g in range(G):
            rows = slice(g * CHUNK, (g + 1) * CHUNK)
            for h in range(DN_H):
                n = g * DN_H + h
                dqkv_ref[rows, h * DN_D:(h + 1) * DN_D] = dq[n]
                dqkv_ref[rows, DNW + h * DN_D:DNW + (h + 1) * DN_D] = dk[n]
                dqkv_ref[rows, 2 * DNW + h * DN_D:2 * DNW + (h + 1) * DN_D] = dv[n]
        dbg_ref[...] = dbg.reshape(G * CHUNK, DN_D)

    return _hosted_call(
        body, "dn_intra_bwd", (B, nc // G),
        in_specs=[_tok(rows, CONVW), _tok(rows, 128), _tok(rows, AW)] + [_tok(rows, DNW)] * 4 + [_tok(rows, AW), _cd_spec(G)],
        out_specs=[_tok(rows, CONVW), _tok(rows, 128)],
        out_shape=[_sds((B, S, CONVW), F32), _sds((B, S, 128), F32)],
        scratch_shapes=[], semantics=("parallel", "parallel"), ins=(qkv, bg, tinv, du, dw, dqd, dkd, da, dcd), ex=ex)


def _dn_prep_bwd(dn, y, ba, conv_w, alog, dtb, dqkv, dbg, bt, ex=None):
    B, S, _ = dn.shape
    nt = S // bt
    strip_rows = min(bt, 64)

    def rev(f):
        return pl.BlockSpec((None, bt, f), lambda b, i: (b, nt - 1 - i, 0))

    def body(x_ref, y_ref, ba_ref, cw_ref, al_ref, dt_ref, dqkv_ref, dbg_ref,
             dx_ref, dba_ref, dcw_ref, dal_ref, ddt_ref, dye_ref):
        i = pl.program_id(1)
        @pl.when(i == 0)
        def _():
            dye_ref[bt:bt + 8, :] = jnp.zeros((8, CONVW), F32)

        @pl.when(i > 0)
        def _():
            dye_ref[bt:bt + 8, :] = dye_ref[0:8, :]

        n_strips = bt // strip_rows

        def strip(k, carry):
            dal, ddt, dcw = carry
            r0 = pl.multiple_of((n_strips - 1 - k) * strip_rows, strip_rows)
            rows = pl.ds(r0, strip_rows)
            dcw_slabs = []
            for j in range(3 * DN_H):
                cols = slice(j * DN_D, (j + 1) * DN_D)
                _, vjp = jax.vjp(functools.partial(_dn_act, normalize=j < 2 * DN_H), y_ref[rows, cols].astype(F32))
                (dye_ref[rows, cols],) = vjp(dqkv_ref[rows, cols])
                window = dye_ref[pl.ds(r0, strip_rows + 8), cols]
                shifted = [window[3 - t:strip_rows + 3 - t] for t in range(CONV)]
                dx = cw_ref[0:1, cols] * shifted[0]
                for t in range(1, CONV):
                    dx = dx + cw_ref[t:t + 1, cols] * shifted[t]
                dx_ref[rows, cols] = dx.astype(BF)
                x = x_ref[rows, cols].astype(F32)
                dcw_slabs.append(jnp.concatenate([jnp.sum(shifted[t] * x, axis=0, keepdims=True) for t in range(CONV)], axis=0))
            _, vjp = jax.vjp(_dn_gates, ba_ref[rows, :], al_ref[...], dt_ref[...])
            dba_ref[rows, :], da, dd = vjp(dbg_ref[rows, :])
            return dal + da, ddt + dd, dcw + jnp.concatenate(dcw_slabs, axis=1)
        zero = jnp.zeros((1, DN_D), F32)
        dal, ddt, dcw = lax.fori_loop(0, n_strips, strip, (zero, zero, jnp.zeros((CONV, CONVW), F32)))
        first = (i == 0) & (pl.program_id(0) == 0)
        _acc(dcw_ref, dcw, first)
        _acc(dal_ref, dal, first)
        _acc(ddt_ref, ddt, first)

    return _hosted_call(
        body, "dn_prep_bwd", (B, nt),
        in_specs=[rev(CONVW), rev(CONVW), rev(128), _full((CONV, CONVW)), _full((1, 128)), _full((1, 128)), rev(CONVW), rev(128)],
        out_specs=[rev(CONVW), rev(128), _full((CONV, CONVW)), _full((1, 128)), _full((1, 128))],
        out_shape=[_sds((B, S, CONVW), BF), _sds((B, S, 128), F32), _sds((CONV, CONVW), F32), _sds((1, 128), F32),
                   _sds((1, 128), F32)],
        scratch_shapes=[pltpu.VMEM((bt + 8, CONVW), F32)],
        semantics=("arbitrary", "arbitrary"), ins=(dn, y, ba, conv_w, alog, dtb, dqkv, dbg), ex=ex)


def _attn_bwd(qkv, sinks, d_o, ex=None):
    B, S, _ = qkv.shape
    nb = S // BLK

    def cur(f):
        return pl.BlockSpec((None, BLK, f), lambda b, i: (b, jnp.minimum(i, nb - 1), 0))

    def out_prev(f):
        return pl.BlockSpec((None, BLK, f), lambda b, i: (b, jnp.maximum(i - 1, 0), 0))

    def body(qkv_ref, kvp_ref, sk_ref, do_ref, dq_ref, dkv_ref, dsk_ref, carry_ref):
        n = pl.program_id(1)
        first = (n == 0) & (pl.program_id(0) == 0)

        @pl.when(n == 0)
        def _():
            carry_ref[...] = jnp.zeros_like(carry_ref)

        @pl.when(n < nb)
        def _():
            qs, kc, kp, vc, vp = _attn_load(qkv_ref, kvp_ref)

            def f(qs, kc, kp, vc, vp, sk):
                return _attn_block(qs, kc, kp, vc, vp, sk, n > 0)
            _, vjp = jax.vjp(f, qs, kc, kp, vc, vp, sk_ref[...])
            d_outs = [do_ref[:, h * HD:(h + 1) * HD] for h in range(HEADS)]
            dqs, dkc, dkp, dvc, dvp, dsk = vjp(d_outs)
            for h in range(HEADS):
                dq_ref[:, h * HD:(h + 1) * HD] = dqs[h].astype(BF)
            for h in range(KV_HEADS):
                ksl = slice(h * HD, (h + 1) * HD)
                vsl = slice(KVW + h * HD, KVW + (h + 1) * HD)
                dkv_ref[:, ksl] = (carry_ref[:, ksl] + dkp[h]).astype(BF)
                dkv_ref[:, vsl] = (carry_ref[:, vsl] + dvp[h]).astype(BF)
                carry_ref[:, ksl] = dkc[h]
                carry_ref[:, vsl] = dvc[h]
            _acc(dsk_ref, dsk, first)

        @pl.when(n == nb)
        def _():
            dkv_ref[...] = carry_ref[...].astype(BF)

    return _hosted_call(
        body, "attn_bwd", (B, nb + 1),
        in_specs=[cur(QKV), _kv_prev_spec(lambda i: jnp.maximum(jnp.minimum(i, nb - 1) - 1, 0)), _full((1, HEADS)), cur(QW)],
        out_specs=[cur(QW), out_prev(2 * KVW), _full((1, HEADS))],
        out_shape=[_sds((B, S, QW), BF), _sds((B, S, 2 * KVW), BF), _sds((1, HEADS), F32)],
        scratch_shapes=[pltpu.VMEM((BLK, 2 * KVW), F32)],
        semantics=("arbitrary", "arbitrary"), ins=(qkv, qkv, sinks, d_o), ex=ex)


def _in_proj_bwd(x, mod, norm1_g, w_in, pieces, dba, dx1, bt):
    B, S, _ = x.shape
    widths = (QW, 2 * KVW, CONVW, DNW, D, D)

    def body(x_ref, mod_ref, g_ref, w_ref, dq_ref, dkv_ref, ddn_ref, dz_ref, dga_ref, dgb_ref, dba_ref, dx1_ref,
             gx_ref, dp_ref, dmod_ref, dg_ref):
        dp = jnp.concatenate([r[...] for r in (dq_ref, dkv_ref, ddn_ref, dz_ref, dga_ref, dgb_ref)]
                             + [dba_ref[...].astype(BF)], axis=-1)
        dp_ref[...] = dp
        dh = lax.dot_general(dp, w_ref[...], (((1,), (1,)), ((), ())), preferred_element_type=F32)
        _, vjp = jax.vjp(_norm_mod, x_ref[...], g_ref[...], mod_ref[:, 0:D], mod_ref[:, D:2 * D])
        dx, dgain, dshift, dscale = vjp(dh)
        gx_ref[...] = dx + dx1_ref[...]
        first = pl.program_id(1) == 0
        _acc(dmod_ref, jnp.concatenate([dshift, dscale], axis=-1), first)
        _acc(dg_ref, dgain, first)

    return pl.pallas_call(
        body, name="in_proj_bwd", grid=(B, S // bt),
        in_specs=[_tok(bt, D), _per_batch(6 * D), _full((1, D)), _resident((D, IN_PAD))] + [_tok(bt, w) for w in widths]
        + [_tok(bt, 128), _tok(bt, D)],
        out_specs=[_tok(bt, D), _tok(bt, IN_PAD), _per_batch(2 * D), _per_batch(D)],
        out_shape=[_sds((B, S, D), F32), _sds((B, S, IN_PAD), BF), _sds((B, 1, 2 * D), F32), _sds((B, 1, D), F32)],
        compiler_params=_cparams(dimension_semantics=("parallel", "arbitrary")),
    )(x, mod, norm1_g, w_in, *pieces, dba, dx1)


def _matmul_tn(tag, a, b, bk, bn, bt, col_blocks=False, ex=None):
    T, K = a.shape
    N = b.shape[1]
    nt = T // bt
    if col_blocks:
        assert bk == K
        out_spec = pl.BlockSpec((None, bk, bn), lambda i, j, t: (j, 0, 0))
        out_shape = _sds((N // bn, K, bn), F32)
    else:
        out_spec = pl.BlockSpec((bk, bn), lambda i, j, t: (i, j))
        out_shape = _sds((K, N), F32)

    def body(a_ref, b_ref, o_ref, acc_ref):
        t = pl.program_id(2)

        @pl.when(t == 0)
        def _():
            acc_ref[...] = jnp.zeros_like(acc_ref)
        acc_ref[...] += lax.dot_general(a_ref[...], b_ref[...], (((0,), (0,)), ((), ())), preferred_element_type=F32)

        @pl.when(t == nt - 1)
        def _():
            o_ref[...] = acc_ref[...]

    (out,), landed = _hosted_call(
        body, f"grad_{tag}", (K // bk, N // bn, nt),
        in_specs=[pl.BlockSpec((bt, bk), lambda i, j, t: (t, i)), pl.BlockSpec((bt, bn), lambda i, j, t: (t, j))],
        out_specs=[out_spec], out_shape=[out_shape],
        scratch_shapes=[pltpu.VMEM((bk, bn), F32)],
        semantics=("parallel", "parallel", "arbitrary"), ins=(a, b), ex=ex)
    return out if ex is None else (out, landed)


def _rope_table(positions):
    inv_freq = THETA ** (-jnp.arange(0, ROT, 2, dtype=F32) / ROT)
    rest = jnp.zeros((HD - ROT,), F32)
    freq = jnp.concatenate([inv_freq, inv_freq, rest] * 2)
    sign = jnp.concatenate([-jnp.ones_like(inv_freq), jnp.ones_like(inv_freq), rest] * 2)
    ang = positions.astype(F32)[..., None] * freq
    return jnp.cos(ang), jnp.sin(ang) * sign


def _lane_pad(v, offset, width=128):
    return jnp.zeros((1, width), F32).at[0, offset:offset + v.shape[-1]].set(v.reshape(-1))


def _tile(S, want):
    return min(S, want)


class _Hosted:
    def __init__(self, call):
        self.call = call
        self.outs = None

    def __call__(self, ex):
        self.outs, landed = self.call(ex)
        return landed


def _local_step(x, mod, positions, tgt, norm1_g, w_in_pad, conv_w, q_norm_g, k_norm_g, sinks, a_log, dt_bias,
                dn_norm_g, w_branch, w_out, norm2_g, w_gu, w_dn, dist=None):
    B, S, _ = x.shape
    T = B * S
    cos_t, sin_t = _rope_table(positions)
    qg2 = jnp.concatenate([q_norm_g, q_norm_g], axis=-1)
    kg2 = jnp.concatenate([k_norm_g, k_norm_g], axis=-1)
    alog = _lane_pad(a_log, DN_H)
    dtb = _lane_pad(dt_bias, DN_H)
    conv2 = conv_w.reshape(CONV, CONVW)
    bt = _tile(S, 512)
    bt_mlp = _tile(S, 256)

    q, kv, dn, z, ga, gb, ba, h1 = _in_proj(x, mod, norm1_g, w_in_pad, bt)
    qkv_n = _qk_prep_fwd(q, kv, cos_t, sin_t, qg2, kg2, bt)
    o_attn = _attn_fwd(qkv_n, sinks)
    dqkv, bg, dn_y = _dn_prep(dn, ba, conv2, alog, dtb, bt)
    intra = _Hosted(lambda ex: _dn_intra_fwd(dqkv, bg, ex))
    if dist is None:
        intra(None)
    else:
        f_br, f_out, w_gu, f_dn = _gather_weights("late", [w_branch, w_out, w_gu, w_dn], dist[0], host=intra)
        w_branch, w_out, w_dn = (f.reshape(N_CHIP * f.shape[1], f.shape[2]) for f in (f_br, f_out, f_dn))
    dn_u, dn_w, dn_qd, dn_kd, dn_a, dn_cd, dn_tinv = intra.outs
    (o_raw, states), _ = _dn_rec_fwd(dn_u, dn_w, dn_qd, dn_kd, dn_a, dn_cd)
    x1, o_dn, merged = _mix_fwd(o_attn, o_raw, z, ga, gb, x, mod, dn_norm_g, w_branch, w_out, bt)
    dx1, h2, act, dgu, dyy, loss, dmod2, dnorm2 = _mlp(x1, tgt, mod, norm2_g, w_gu, w_dn, bt_mlp)

    def flat(t):
        return t.reshape(T, t.shape[-1])
    tn = functools.partial(_matmul_tn, bt=_tile(T, 2048))
    g_w_dn = tn("w_down", flat(act), flat(dyy), bk=FFN, bn=D // 2)
    g_w_gu = tn("w_gate_up", flat(h2), flat(dgu), bk=D, bn=2 * FFN // N_CHIP, col_blocks=True)

    mix_b = _Hosted(lambda ex: _mix_bwd(o_attn, o_raw, z, ga, gb, x, mod, dn_norm_g, w_branch, w_out, dx1, bt_mlp, ex))
    rec_b = _Hosted(lambda ex: _dn_rec_bwd(dn_u, dn_w, dn_qd, dn_kd, dn_a, dn_cd, states, mix_b.outs[1], ex))
    intra_b = _Hosted(lambda ex: _dn_intra_bwd(dqkv, bg, dn_tinv, *rec_b.outs, ex))
    if dist is None:
        for host in (mix_b, rec_b, intra_b):
            host(None)
    else:
        g_w_gu, g_w_dn = _reduce_grads(("w_gate_up", "w_down"), [g_w_gu, g_w_dn.reshape(N_CHIP, -1, D)], *dist,
                                       hosts=[mix_b, rec_b, intra_b])
    d_oa, _, dz, dga, dgb, dya, dyd, dout, dgate1, ddn_g = mix_b.outs
    d_dqkv, dbg = intra_b.outs
    g_w_out = tn("w_out", flat(merged), flat(dout), bk=D, bn=D)
    g_w_br = jnp.concatenate([tn("w_branch_attn", flat(o_attn), flat(dya), bk=QW, bn=D),
                              tn("w_branch_dn", flat(o_dn), flat(dyd), bk=DNW, bn=D)], axis=0)
    prep_b = _Hosted(lambda ex: _dn_prep_bwd(dn, dn_y, ba, conv2, alog, dtb, d_dqkv, dbg, bt, ex))
    attn_b = _Hosted(lambda ex: _attn_bwd(qkv_n, sinks, d_oa, ex))
    qk_b = _Hosted(lambda ex: _qk_prep_bwd(q, kv, cos_t, sin_t, qg2, kg2, *attn_b.outs[:2], bt, ex))
    if dist is None:
        for host in (prep_b, attn_b, qk_b):
            host(None)
    else:
        g_w_br, g_w_out = _reduce_grads(("w_branch", "w_out"), [g_w_br.reshape(N_CHIP, -1, D), g_w_out.reshape(N_CHIP, -1, D)],
                                        *dist, hosts=[prep_b, attn_b, qk_b])
    d_dn, dba, dconv, dalog, ddtb = prep_b.outs
    dsk = attn_b.outs[2]
    dq, dkv, dqg2, dkg2 = qk_b.outs
    dqg = dqg2[:, :HD] + dqg2[:, HD:]
    dkg = dkg2[:, :HD] + dkg2[:, HD:]
    grad_x, dproj, dmod1, dnorm1 = _in_proj_bwd(x, mod, norm1_g, w_in_pad, (dq, dkv, d_dn, dz, dga, dgb), dba, dx1, bt)
    g_w_in = _Hosted(lambda ex: (tn("w_in", flat(h1), flat(dproj), bk=D, bn=IN_PAD // 3), ()) if ex is None
                     else tn("w_in", flat(h1), flat(dproj), bk=D, bn=IN_PAD // 3, ex=ex))
    if dist is None:
        g_w_in(None)
        g_w_in = g_w_in.outs

    dmod = jnp.concatenate([dmod1, dgate1, dmod2], axis=-1)
    small = dict(norm1_g=jnp.sum(dnorm1, axis=0), norm2_g=jnp.sum(dnorm2, axis=0), q_norm_g=dqg, k_norm_g=dkg,
                 sinks=dsk, a_log=dalog[:, DN_H:2 * DN_H], dt_bias=ddtb[:, DN_H:2 * DN_H],
                 dn_norm_g=jnp.sum(ddn_g, axis=0), conv_w=dconv)
    return jnp.sum(loss), grad_x, dmod, small, (g_w_in, g_w_br, g_w_out, g_w_gu, g_w_dn)


def _flip(me, f):
    return (me[0] ^ ((f >> 2) & 1), me[1] ^ ((f >> 1) & 1), me[2] ^ (f & 1))


def _comm_call(name, ex):
    n_in, n_out = len(ex.ins), len(ex.out_shapes)

    def body(*refs):
        out_refs, sems = refs[n_in:n_in + n_out], refs[n_in + n_out:]
        cps = _exchange_copies(ex, refs[:n_in], out_refs, sems[0], sems[1])
        for cp in cps:
            cp.start()
        for cp in cps:
            cp.wait_recv()
        if ex.n_forward:
            fwd = _forward_copies(ex, out_refs, sems[2], sems[3])
            for cp in fwd:
                cp.start()
            for cp in fwd:
                cp.wait_recv()
            cps = cps + fwd
        for cp in cps:
            cp.wait_send()

    any_spec = pl.BlockSpec(memory_space=pl.ANY)
    return pl.pallas_call(
        body, name=name, in_specs=[any_spec] * n_in, out_specs=[any_spec] * n_out, out_shape=list(ex.out_shapes),
        scratch_shapes=_exchange_sems(ex),
    )(*ex.ins)


def _by_origin(own, received, index):
    stack = jnp.concatenate([own[None], received], axis=0)
    n = stack.shape[0]
    return jnp.stack([lax.dynamic_index_in_dim(stack, k ^ index, 0, keepdims=False) for k in range(n)])


def _gather_devices(name, arrs, dev, host=None):
    def plan(me, in_refs, out_refs):
        return [(a, o.at[f - 1], _flip(me, f)) for a, o in zip(in_refs, out_refs) for f in range(1, N_DEV)]
    outs = tuple(_sds((N_DEV - 1,) + a.shape, a.dtype) for a in arrs)
    got = (host or functools.partial(_comm_call, name))(_Exchange(tuple(arrs), outs, (N_DEV - 1) * len(arrs), plan))
    return [_by_origin(a, g, dev) for a, g in zip(arrs, got)]


def _gather_chips(name, arrs, chip):
    def plan(me, in_refs, out_refs):
        return [(a, o.at[j], _flip(me, 2 * (j + 1))) for a, o in zip(in_refs, out_refs) for j in range(N_CHIP - 1)]
    outs = tuple(_sds((N_CHIP - 1,) + a.shape, a.dtype) for a in arrs)
    got = _comm_call(name, _Exchange(tuple(arrs), outs, (N_CHIP - 1) * len(arrs), plan))
    return [_by_origin(a, g, chip) for a, g in zip(arrs, got)]


def _swap_cores_ex(arrs):
    def plan(me, in_refs, out_refs):
        return [(g, o, _flip(me, 1)) for g, o in zip(in_refs, out_refs)]
    return _Exchange(tuple(arrs), tuple(_sds(g.shape, g.dtype) for g in arrs), len(arrs), plan)


def _gather_weights(tag, shards, chip, host=None):
    def plan(me, in_refs, out_refs):
        chip_me = 2 * me[0] + me[1]
        remote = []
        for a, o in zip(in_refs, out_refs):
            half = a.shape[0] // 2
            mine = a.at[pl.ds(me[2] * half, half)]
            remote += [(mine, o.at[chip_me, me[2]], _flip(me, 2 * (j + 1))) for j in range(N_CHIP - 1)]
        return remote

    def forward(me, out_refs):
        chip_me = 2 * me[0] + me[1]
        return [(o.at[chip_me ^ (j + 1), me[2]], o.at[chip_me ^ (j + 1), me[2]], _flip(me, 1))
                for o in out_refs for j in range(N_CHIP - 1)]
    run = host or functools.partial(_comm_call, f"weights_{tag}")
    n = (N_CHIP - 1) * len(shards)
    landed = run(_Exchange(tuple(shards), tuple(_sds((N_CHIP, 2, a.shape[0] // 2, a.shape[1]), a.dtype) for a in shards),
                           n, plan, n, forward))
    return [lax.dynamic_update_slice(f.reshape((N_CHIP,) + a.shape), a[None], (chip, 0, 0)) for a, f in zip(shards, landed)]


def _rows(r):
    for br in (512, 352, 256, 128, 64, 32, 16, 8):
        if r % br == 0:
            return br
    raise ValueError(r)


def _pair_add(tag, g, recv, c):
    n, r, cols = g.shape
    half = r // 2
    br = _rows(half)
    nb = half // br

    def body(c_ref, g_ref, r_ref, o_ref):
        o_ref[...] = (g_ref[...] + r_ref[...]).astype(BF)

    return pl.pallas_call(
        body, name=f"pair_add_{tag}",
        grid_spec=pltpu.PrefetchScalarGridSpec(
            num_scalar_prefetch=1, grid=(n, nb),
            in_specs=[pl.BlockSpec((None, br, cols), lambda k, i, c_ref: (k, c_ref[0] * nb + i, 0)),
                      pl.BlockSpec((None, br, cols), lambda k, i, c_ref: (k, i, 0))],
            out_specs=pl.BlockSpec((None, br, cols), lambda k, i, c_ref: (k, i, 0))),
        out_shape=_sds((n, half, cols), BF),
        compiler_params=_cparams(dimension_semantics=("parallel", "parallel")),
    )(c, g, recv)


def _sum_chips(tag, p, q, chip):
    n, r, cols = q.shape
    br = _rows(r)

    def body(chip_ref, p_ref, q_ref, o_ref):
        acc = p_ref[...].astype(F32)
        for k in range(n):
            acc = acc + q_ref[k].astype(F32)
        o_ref[...] = acc

    return pl.pallas_call(
        body, name=f"sum_chips_{tag}",
        grid_spec=pltpu.PrefetchScalarGridSpec(
            num_scalar_prefetch=1, grid=(r // br,),
            in_specs=[pl.BlockSpec((None, br, cols), lambda i, chip_ref: (chip_ref[0], i, 0)),
                      pl.BlockSpec((n, br, cols), lambda i, chip_ref: (0, i, 0))],
            out_specs=pl.BlockSpec((br, cols), lambda i, chip_ref: (i, 0))),
        out_shape=_sds((r, cols), F32),
        compiler_params=_cparams(dimension_semantics=("parallel",)),
    )(chip, p, q)


def _reduce_grads(tags, grads, chip, core, hosts=None):
    core_arr = core.reshape(1).astype(jnp.int32)
    chip_arr = chip.reshape(1).astype(jnp.int32)
    name = "_".join(tags)
    run = hosts or [functools.partial(_comm_call, f"grads_{stage}_{name}") for stage in ("pair", "chips", "swap")]

    def plan_pair(me, in_refs, out_refs):
        remote = []
        for g, o in zip(in_refs, out_refs):
            half = g.shape[1] // 2
            remote += [(g.at[k, pl.ds((1 - me[2]) * half, half)], o.at[k], _flip(me, 1)) for k in range(N_CHIP)]
        return remote
    recv = run[0](_Exchange(tuple(grads), tuple(_sds((N_CHIP, g.shape[1] // 2, g.shape[2]), F32) for g in grads),
                            N_CHIP * len(grads), plan_pair))
    pair = [_pair_add(t, g, r, core_arr) for t, g, r in zip(tags, grads, recv)]

    def plan_chips(me, in_refs, out_refs):
        remote = []
        for p, o in zip(in_refs, out_refs):
            for j in range(N_CHIP - 1):
                peer = _flip(me, 2 * (j + 1))
                remote.append((p.at[2 * peer[0] + peer[1]], o.at[j], peer))
        return remote
    parts = run[1](_Exchange(tuple(pair), tuple(_sds((N_CHIP - 1,) + p.shape[1:], BF) for p in pair),
                             (N_CHIP - 1) * len(pair), plan_chips))
    mine = [_sum_chips(t, p, q, chip_arr) for t, p, q in zip(tags, pair, parts)]
    other = run[2](_swap_cores_ex(mine))
    return list(zip(mine, other))


def _adamw_math(w, g, m, v):
    m = ADAM_B1 * m + (1.0 - ADAM_B1) * g
    v = ADAM_B2 * v + (1.0 - ADAM_B2) * (g * g)
    m_hat = m / (1.0 - ADAM_B1 ** ADAM_STEP)
    v_hat = v / (1.0 - ADAM_B2 ** ADAM_STEP)
    delta = -ADAM_LR * (m_hat / (jnp.sqrt(v_hat) + ADAM_EPS) + ADAM_WD * w)
    return delta, m, v


def _adamw(name, w, g, m, v, ex=None):
    r, cols = w.shape
    br = _rows(r)
    if br * cols * 4 > (1 << 20) and br % 16 == 0:
        br //= 2

    def body(w_ref, g_ref, m_ref, v_ref, d_ref, mo_ref, vo_ref):
        d_ref[...], mo_ref[...], vo_ref[...] = _adamw_math(w_ref[...], g_ref[...], m_ref[...], v_ref[...])

    spec = pl.BlockSpec((br, cols), lambda i: (i, 0))
    outs, landed = _hosted_call(
        body, f"adamw_{name}", (r // br,), in_specs=[spec] * 4, out_specs=[spec] * 3,
        out_shape=[_sds((r, cols), F32)] * 3, scratch_shapes=[], semantics=("parallel",), ins=(w, g, m, v), ex=ex)
    return outs if ex is None else (outs, landed)


def _adamw_halves(name, w, mine, other, m, v, core, ex=None):
    r, cols = w.shape
    br = _rows(r // 2)
    if br * cols * 4 > (1 << 20) and br % 16 == 0:
        br //= 2
    per_half = r // 2 // br

    def body(core_ref, w_ref, a_ref, b_ref, m_ref, v_ref, g_ref, d_ref, mo_ref, vo_ref):
        g = jnp.where(pl.program_id(0) // per_half == core_ref[0], a_ref[...], b_ref[...])
        g_ref[...] = g
        d_ref[...], mo_ref[...], vo_ref[...] = _adamw_math(w_ref[...], g, m_ref[...], v_ref[...])

    whole = pl.BlockSpec((br, cols), lambda i: (i, 0))
    half = pl.BlockSpec((br, cols), lambda i: (i % per_half, 0))
    outs, landed = _hosted_call(
        body, f"adamw_{name}", (r // br,),
        in_specs=[pl.BlockSpec(memory_space=pltpu.SMEM), whole, half, half, whole, whole], out_specs=[whole] * 4,
        out_shape=[_sds((r, cols), F32)] * 4, scratch_shapes=[], semantics=("parallel",),
        ins=(core.reshape(1).astype(jnp.int32), w, mine, other, m, v), ex=ex)
    return outs if ex is None else (outs, landed)


def _ada_fwd(c_all, ada_w, ada_b_cols):
    n = c_all.shape[0]

    def body(c_ref, w_ref, b_ref, o_ref):
        o_ref[...] = _mmx(_silu(c_ref[...]), w_ref[...]) + b_ref[...]

    return pl.pallas_call(
        body, name="ada_fwd", out_shape=_sds((n, ada_w.shape[1]), F32), compiler_params=_cparams(),
    )(c_all, ada_w, ada_b_cols)


def _ada_bwd(c_all, dmod_cols, w, m, v, ex=None):
    n = c_all.shape[0]
    r, cols = w.shape
    br = 128

    def body(c_ref, d_ref, w_ref, m_ref, v_ref, g_ref, dl_ref, mo_ref, vo_ref):
        cond = _silu(c_ref[...])
        g = lax.dot_general(cond, d_ref[...], (((0,), (0,)), ((), ())), precision=lax.Precision.HIGHEST,
                            preferred_element_type=F32)
        g_ref[...] = g
        dl_ref[...], mo_ref[...], vo_ref[...] = _adamw_math(w_ref[...], g, m_ref[...], v_ref[...])

    spec = pl.BlockSpec((br, cols), lambda i: (i, 0))
    outs, landed = _hosted_call(
        body, "ada_bwd", (r // br,),
        in_specs=[pl.BlockSpec((n, br), lambda i: (0, i)), pl.BlockSpec((n, cols), lambda i: (0, 0)), spec, spec, spec],
        out_specs=[spec] * 4, out_shape=[_sds((r, cols), F32)] * 4, scratch_shapes=[], semantics=("parallel",),
        ins=(c_all, dmod_cols, w, m, v), ex=ex)
    return outs if ex is None else (outs, landed)


def _sum_devices(parts):
    n, r, cols = parts.shape

    def body(p_ref, o_ref):
        acc = p_ref[0]
        for k in range(1, n):
            acc = acc + p_ref[k]
        o_ref[...] = acc

    return pl.pallas_call(body, name="sum_devices", out_shape=_sds((r, cols), F32), compiler_params=_cparams())(parts)


SMALL_ROWS = 16
_SMALL_SLOTS = dict(norm1_g=(0, 0, D), norm2_g=(1, 0, D), q_norm_g=(2, 0, HD), k_norm_g=(2, 128, HD), sinks=(2, 256, HEADS),
                    a_log=(2, 384, DN_H), dt_bias=(2, 512, DN_H), dn_norm_g=(2, 640, DN_D))
_CONV_ROW = 4
_ADA_B_ROW = 8


def _pack_small(vals, conv, ada_b):
    def row(pieces):
        out, at = [], 0
        for col, val in pieces:
            out += [jnp.zeros((1, col - at), F32), val.reshape(1, -1)]
            at = col + val.size
        return jnp.concatenate(out + [jnp.zeros((1, CONVW - at), F32)], axis=1)
    rows = {}
    for name, (r, col, n) in _SMALL_SLOTS.items():
        rows.setdefault(r, []).append((col, vals[name]))
    blank = jnp.zeros((1, CONVW), F32)
    top = [row(sorted(rows[r], key=lambda p: p[0])) if r in rows else blank for r in range(_CONV_ROW)]
    conv_rows = jnp.concatenate([conv, jnp.zeros((CONV, CONVW - conv.shape[1]), F32)], axis=1)
    tail = jnp.zeros((SMALL_ROWS - _ADA_B_ROW - 4, CONVW), F32)
    return jnp.concatenate(top + [conv_rows, ada_b.reshape(4, CONVW), tail], axis=0)


def _unpack_small(sheet, conv_cols):
    out = {name: sheet[row, col:col + n].reshape(1, n) for name, (row, col, n) in _SMALL_SLOTS.items()}
    out["conv_w"] = sheet[_CONV_ROW:_CONV_ROW + CONV, 0:conv_cols].reshape(1, CONV, 1, conv_cols)
    out["ada_b"] = sheet[_ADA_B_ROW:_ADA_B_ROW + 4, :].reshape(1, 6 * D)
    return out


def _w_in_segments():
    shard = IN_WIDTH // N_CHIP
    cuts = sorted({0, IN_WIDTH, C_Z, C_Z + 2 * DN_H} | {k * shard for k in range(1, N_CHIP)})
    segs = []
    for a, b in zip(cuts[:-1], cuts[1:]):
        k = a // shard
        pad = a if a < C_Z else (C_BA + a - C_Z if a < C_Z + 2 * DN_H else a - 2 * DN_H)
        segs.append((k, a - k * shard, b - k * shard, pad))
    return segs


def _pad_w_in(f):
    parts = [f[k][:, lo:hi] for k, lo, hi, _ in sorted(_w_in_segments(), key=lambda s: s[3])]
    return jnp.concatenate(parts + [jnp.zeros((f.shape[1], IN_PAD - IN_WIDTH), f.dtype)], axis=1)


def _unpad_w_in(g):
    return jnp.stack([jnp.concatenate([g[:, pad:pad + hi - lo] for kk, lo, hi, pad in _w_in_segments() if kk == k], axis=1)
                      for k in range(N_CHIP)])


def _blocks_to_cols(f):
    return f.transpose(1, 0, 2).reshape(f.shape[1], N_CHIP * f.shape[2])


def kernel(x, c, positions, ada_w, ada_b, norm1_g, w_in, conv_w, q_norm_g, k_norm_g, sinks, a_log, dt_bias, dn_norm_g, w_branch, w_out, norm2_g, w_gate_up, w_down, loss_target, m_ada_w, m_ada_b, m_norm1_g, m_w_in, m_conv_w, m_q_norm_g, m_k_norm_g, m_sinks, m_a_log, m_dt_bias, m_dn_norm_g, m_w_branch, m_w_out, m_norm2_g, m_w_gate_up, m_w_down, v_ada_w, v_ada_b, v_norm1_g, v_w_in, v_conv_w, v_q_norm_g, v_k_norm_g, v_sinks, v_a_log, v_dt_bias, v_dn_norm_g, v_w_branch, v_w_out, v_norm2_g, v_w_gate_up, v_w_down):
    ix, iy, ic = lax.axis_index("x"), lax.axis_index("y"), lax.axis_index("c")
    dev = 4 * ix + 2 * iy + ic
    chip = 2 * ix + iy
    n_seq = x.shape[0]
    conv_cols = conv_w.shape[-1]

    c_all, conv_all = _gather_devices("gather_cond", [c, conv_w.reshape(CONV, conv_cols)], dev)
    c_all = c_all.reshape(N_DEV * n_seq, D)
    ada_cols = ada_w.shape[-1]
    ada_b_cols = lax.dynamic_slice(ada_b, (0, chip * ada_cols), (1, ada_cols))
    mod_cols = _ada_fwd(c_all, ada_w[0], ada_b_cols)
    (mod_blocks,) = _gather_chips("gather_mod", [mod_cols], chip)
    mod_all = _blocks_to_cols(mod_blocks)
    mod = lax.dynamic_slice(mod_all, (dev * n_seq, 0), (n_seq, 6 * D)).reshape(n_seq, 1, 6 * D)
    conv_full = _blocks_to_cols(conv_all[0::2])

    (f_in,) = _gather_weights("w_in", [w_in[0].astype(BF)], chip)
    w_in_pad = _pad_w_in(f_in)

    loss, grad_x, dmod, small, (w_in_grad, r_br, r_out, r_gu, r_dn) = _local_step(
        x, mod, positions, loss_target, norm1_g, w_in_pad, conv_full.reshape(CONV, 1, CONVW), q_norm_g, k_norm_g, sinks,
        a_log, dt_bias, dn_norm_g, w_branch[0].astype(BF), w_out[0].astype(BF), norm2_g, w_gate_up[0].astype(BF),
        w_down[0].astype(BF), dist=(chip, ic))
    loss = lax.psum(loss, ("x", "y", "c"))

    part = _pack_small(small, small["conv_w"], jnp.sum(dmod, axis=(0, 1)).reshape(1, 6 * D))
    dmod_all, parts = _gather_devices("gather_small", [dmod.reshape(n_seq, 6 * D), part], dev, host=w_in_grad)
    dmod_all = dmod_all.reshape(N_DEV * n_seq, 6 * D)
    dmod_cols = lax.dynamic_slice(dmod_all, (0, chip * ada_cols), (N_DEV * n_seq, ada_cols))

    up_gu = _Hosted(lambda ex: _adamw_halves("w_gate_up", w_gate_up[0], *r_gu, m_w_gate_up[0], v_w_gate_up[0], ic, ex))
    up_ada = _Hosted(lambda ex: _ada_bwd(c_all, dmod_cols, ada_w[0], m_ada_w[0], v_ada_w[0], ex))
    up_dn = _Hosted(lambda ex: _adamw_halves("w_down", w_down[0], *r_dn, m_w_down[0], v_w_down[0], ic, ex))
    (r_in,) = _reduce_grads(("w_in",), [_unpad_w_in(w_in_grad.outs)], chip, ic, hosts=[up_gu, up_ada, up_dn])
    ada = up_ada.outs
    big = {"w_gate_up": tuple(up_gu.outs), "w_down": tuple(up_dn.outs)}
    for name, w, g, m, v in (("w_in", w_in, r_in, m_w_in, v_w_in), ("w_branch", w_branch, r_br, m_w_branch, v_w_branch),
                             ("w_out", w_out, r_out, m_w_out, v_w_out)):
        big[name] = tuple(_adamw_halves(name, w[0], *g, m[0], v[0], ic))
    g_small = _unpack_small(_sum_devices(parts), CONVW)
    g_conv = lax.dynamic_slice(g_small["conv_w"].reshape(CONV, CONVW), (0, chip * conv_cols), (CONV, conv_cols))
    g_small["conv_w"] = g_conv.reshape(1, CONV, 1, conv_cols)

    given = dict(norm1_g=(norm1_g, m_norm1_g, v_norm1_g), norm2_g=(norm2_g, m_norm2_g, v_norm2_g),
                 q_norm_g=(q_norm_g, m_q_norm_g, v_q_norm_g), k_norm_g=(k_norm_g, m_k_norm_g, v_k_norm_g),
                 sinks=(sinks, m_sinks, v_sinks), a_log=(a_log, m_a_log, v_a_log), dt_bias=(dt_bias, m_dt_bias, v_dt_bias),
                 dn_norm_g=(dn_norm_g, m_dn_norm_g, v_dn_norm_g))
    sheets = [_pack_small({k: t[j] for k, t in given.items()}, cw.reshape(CONV, conv_cols), ab)
              for j, (cw, ab) in enumerate(((conv_w, ada_b), (m_conv_w, m_ada_b), (v_conv_w, v_ada_b)))]
    g_local = _pack_small(g_small, g_conv, g_small["ada_b"])
    upd = [_unpack_small(s, conv_cols) for s in _adamw("small", sheets[0], g_local, sheets[1], sheets[2])]

    names = ["ada_w", "ada_b", "norm1_g", "w_in", "conv_w", "q_norm_g", "k_norm_g", "sinks", "a_log", "dt_bias", "dn_norm_g",
             "w_branch", "w_out", "norm2_g", "w_gate_up", "w_down"]

    def leaf(name, j):
        if name == "ada_w":
            return ada[j][None]
        if name in big:
            return big[name][j][None]
        return g_small[name] if j == 0 else upd[j - 1][name]

    return (loss, grad_x) + tuple(leaf(n, j) for j in range(4) for n in names)
```

```python
import functools
from typing import Callable, NamedTuple

import jax
import jax.numpy as jnp
import numpy as np
from jax import lax
from jax.experimental import pallas as pl
from jax.experimental.pallas import tpu as pltpu

F32 = jnp.float32
BF = jnp.bfloat16

D = 1024
HEADS = 8
KV_HEADS = 2
GROUP = 4
HD = 64
BLK = 128
ROT = 16
THETA = 500000.0
QW = 512
KVW = 128
DN_H = 4
DN_D = 128
CONV = 4
CHUNK = 64
DNW = 512
CONVW = 1536
FFN = 2816
EPS = 1e-6
IN_WIDTH = 4872
IN_PAD = 4992
C_KV = 512
C_DN = 768
C_Z = 2304
C_GA = 2816
C_GB = 3840
C_BA = 4864
NEG = -1e30
N_DEV = 8
N_CHIP = 4

ADAM_LR = 0.001
ADAM_B1 = 0.9
ADAM_B2 = 0.999
ADAM_EPS = 1e-08
ADAM_WD = 0.01
ADAM_STEP = 10

VMEM_LIMIT = 60 * 1024 * 1024


def _cparams(**kw):
    return pltpu.CompilerParams(vmem_limit_bytes=VMEM_LIMIT, **kw)


def _dg(a, b, ca, cb):
    return lax.dot_general(a.astype(BF), b.astype(BF), (((ca,), (cb,)), ((), ())),
                           preferred_element_type=F32)


@jax.custom_vjp
def _mm(a, b):
    return _dg(a, b, 1, 0)


def _mm_fwd(a, b):
    return _dg(a, b, 1, 0), (a, b)


def _mm_bwd(res, dy):
    a, b = res
    return _dg(dy, b, 1, 1).astype(a.dtype), _dg(a, dy, 0, 0).astype(b.dtype)


_mm.defvjp(_mm_fwd, _mm_bwd)


@jax.custom_vjp
def _mm_nt(a, b):
    return _dg(a, b, 1, 1)


def _mm_nt_fwd(a, b):
    return _dg(a, b, 1, 1), (a, b)


def _mm_nt_bwd(res, dy):
    a, b = res
    return _dg(dy, b, 1, 0).astype(a.dtype), _dg(dy, a, 0, 0).astype(b.dtype)


_mm_nt.defvjp(_mm_nt_fwd, _mm_nt_bwd)


@jax.custom_vjp
def _mm_tn(a, b):
    return _dg(a, b, 0, 0)


def _mm_tn_fwd(a, b):
    return _dg(a, b, 0, 0), (a, b)


def _mm_tn_bwd(res, dy):
    a, b = res
    return _dg(b, dy, 1, 1).astype(a.dtype), _dg(a, dy, 1, 0).astype(b.dtype)


_mm_tn.defvjp(_mm_tn_fwd, _mm_tn_bwd)


def _mmx(a, b):
    return jnp.dot(a, b, precision=lax.Precision.HIGHEST, preferred_element_type=F32)


def _mmx_nt(a, b):
    return lax.dot_general(a, b, (((1,), (1,)), ((), ())), precision=lax.Precision.HIGHEST,
                           preferred_element_type=F32)


def _iota(shape, dim):
    return lax.broadcasted_iota(jnp.int32, shape, dim)


def _sigmoid(x):
    return lax.logistic(x)


def _silu(x):
    return x * _sigmoid(x)


def _softplus(x):
    return jnp.maximum(x, 0.0) + jnp.log(1.0 + jnp.exp(-jnp.abs(x)))


def _rms(x, gain):
    return x * lax.rsqrt(jnp.mean(x * x, axis=-1, keepdims=True) + EPS) * gain


def _norm_mod(x, gain, shift, scale):
    return _rms(x, gain) * (1.0 + scale) + shift


def _split(a):
    hi = a.astype(BF)
    return hi, (a - hi.astype(F32)).astype(BF)


def _dg2(a, c, ca, cb):
    ah, al = _split(a)
    c = c.astype(BF)

    def dg(x):
        return lax.dot_general(x, c, (((ca,), (cb,)), ((), ())), preferred_element_type=F32)
    return dg(ah) + dg(al)


@jax.custom_vjp
def _mmc(a, c):
    return _dg2(a, c, 1, 0)


def _mmc_fwd(a, c):
    return _dg2(a, c, 1, 0), c


def _mmc_bwd(c, dy):
    return _dg2(dy, c, 1, 1), jnp.zeros_like(c)


_mmc.defvjp(_mmc_fwd, _mmc_bwd)


def _qk_prep(slabs, gain, cos, sin):
    r = _iota((2 * HD, 2 * HD), 0)
    c = _iota((2 * HD, 2 * HD), 1)
    seg = jnp.where(r // HD == c // HD, 1.0 / HD, 0.0).astype(F32)
    half = ROT // 2
    cd = c % HD
    pair = jnp.where(((cd < half) & (r == c + half)) | ((cd >= half) & (cd < ROT) & (r == c - half)), 1.0, 0.0).astype(F32)
    out = []
    for x in slabs:
        y = x * lax.rsqrt(_mmc(x * x, seg) + EPS) * gain
        out.append(y * cos + _mmc(y, pair) * sin)
    return out


def _attn_block(qs, kc, kp, vc, vp, sinks, has_prev):
    rows = GROUP * BLK
    qi = _iota((rows, 2 * BLK), 0) % BLK + BLK
    kj = _iota((rows, 2 * BLK), 1)
    dist = qi - kj
    valid = (dist >= 0) & (dist < BLK) & ((kj >= BLK) | has_prev)
    grp = _iota((rows, HEADS), 0) // BLK
    col = _iota((rows, HEADS), 1)

    outs = []
    for h in range(KV_HEADS):
        q = jnp.concatenate([qs[h * GROUP + g] for g in range(GROUP)], axis=0)
        k = jnp.concatenate([kp[h], kc[h]], axis=0)
        v = jnp.concatenate([vp[h], vc[h]], axis=0)
        s = _mm_nt(q, k) * (HD ** -0.5)
        s = jnp.where(valid, s, NEG)
        sink = jnp.sum(jnp.where(col == h * GROUP + grp, sinks, 0.0), axis=-1, keepdims=True)
        m = lax.stop_gradient(jnp.maximum(jnp.max(s, axis=-1, keepdims=True), sink))
        p = jnp.exp(s - m)
        probs = p / (jnp.sum(p, axis=-1, keepdims=True) + jnp.exp(sink - m))
        o = _mm(probs, v)
        outs += [o[g * BLK:(g + 1) * BLK] for g in range(GROUP)]
    return outs


def _dn_act(y, normalize):
    s = _silu(y)
    return s * lax.rsqrt(jnp.sum(s * s, axis=-1, keepdims=True) + EPS) if normalize else s


def _dn_gates(ba, alog, dtb):
    lane = _iota(ba.shape, 1)
    beta = _sigmoid(ba)
    g = -jnp.exp(alog) * _softplus(ba + dtb)
    return jnp.where(lane < DN_H, beta, jnp.where(lane < 2 * DN_H, g, 0.0))


def _bdg(a, b, ca, cb):
    return lax.dot_general(a.astype(BF), b.astype(BF), (((ca,), (cb,)), ((0,), (0,))), preferred_element_type=F32)


@jax.custom_vjp
def _bmm(a, b):
    return _bdg(a, b, 2, 1)


def _bmm_fwd(a, b):
    return _bdg(a, b, 2, 1), (a, b)


def _bmm_bwd(res, dy):
    a, b = res
    return _bdg(dy, b, 2, 2), _bdg(a, dy, 1, 1)


_bmm.defvjp(_bmm_fwd, _bmm_bwd)


@jax.custom_vjp
def _bmm_nt(a, b):
    return _bdg(a, b, 2, 2)


def _bmm_nt_fwd(a, b):
    return _bdg(a, b, 2, 2), (a, b)


def _bmm_nt_bwd(res, dy):
    a, b = res
    return _bdg(dy, b, 2, 1), _bdg(dy, a, 1, 1)


_bmm_nt.defvjp(_bmm_nt_fwd, _bmm_nt_bwd)


def _bmmx(a, b):
    return lax.dot_general(a, b, (((2,), (1,)), ((0,), (0,))), precision=lax.Precision.HIGHEST,
                           preferred_element_type=F32)


def _tri_times(x, transpose):
    C = x.shape[1]
    tri = jnp.broadcast_to((_iota((C, C), 0) >= _iota((C, C), 1)).astype(BF)[None], (x.shape[0], C, C))
    hi = x.astype(BF)
    mid, lo = _split(x - hi.astype(F32))
    dims = (((1,) if transpose else (2,), (1,)), ((0,), (0,)))
    return sum(lax.dot_general(tri, part, dims, preferred_element_type=F32) for part in (hi, mid, lo))


@jax.custom_vjp
def _running_sum(x):
    return _tri_times(x, False)


def _running_sum_fwd(x):
    return _tri_times(x, False), None


def _running_sum_bwd(_, dy):
    return (_tri_times(dy, True),)


_running_sum.defvjp(_running_sum_fwd, _running_sum_bwd)


def _neumann_inverse(lmat):
    C = CHUNK
    eye = jnp.where(_iota((C, C), 0) == _iota((C, C), 1), 1.0, 0.0).astype(F32)[None]
    a = -lmat
    tinv = eye + a
    pw = _bmmx(a, a)
    for _ in range(4):
        both = _bmmx(jnp.concatenate([pw, tinv], axis=1), pw)
        pw, tinv = both[:, :C], tinv + both[:, C:]
    return tinv + _bmmx(tinv, pw)


def _inverse_bwd(tinv, d_tinv):
    x = lax.dot_general(d_tinv, tinv, (((2,), (2,)), ((0,), (0,))), precision=lax.Precision.HIGHEST,
                        preferred_element_type=F32)
    return -lax.dot_general(tinv, x, (((1,), (1,)), ((0,), (0,))), precision=lax.Precision.HIGHEST,
                            preferred_element_type=F32)


@jax.custom_vjp
def _tri_inverse(lmat):
    return _neumann_inverse(lmat)


def _tri_inverse_fwd(lmat):
    tinv = _neumann_inverse(lmat)
    return tinv, tinv


def _tri_inverse_bwd(tinv, d_tinv):
    return (_inverse_bwd(tinv, d_tinv),)


_tri_inverse.defvjp(_tri_inverse_fwd, _tri_inverse_bwd)


@jax.custom_vjp
def _tri_inverse_known(lmat, tinv):
    return tinv


def _tri_inverse_known_fwd(lmat, tinv):
    return tinv, tinv


def _tri_inverse_known_bwd(tinv, d_tinv):
    return _inverse_bwd(tinv, d_tinv), jnp.zeros_like(tinv)


_tri_inverse_known.defvjp(_tri_inverse_known_fwd, _tri_inverse_known_bwd)


def _dn_intra(q, k, v, bg, tinv=None):
    C = CHUNK
    G = bg.shape[0]
    r = _iota((C, C), 0)
    c = _iota((C, C), 1)
    incl = (r >= c)[None]
    strict = (r > c)[None]
    eye = jnp.where(r == c, 1.0, 0.0).astype(F32)[None]
    gc_all = _running_sum(bg)
    lane = _iota((C, DN_D), 1)

    def per_head(x, offset):
        return jnp.concatenate([jnp.sum(jnp.where(lane == offset + h, x[g], 0.0), axis=-1, keepdims=True)[None]
                                for g in range(G) for h in range(DN_H)], axis=0)
    beta = per_head(bg, 0)
    gcol = per_head(gc_all, DN_H)
    grow = jnp.sum(eye * gcol, axis=1, keepdims=True)
    glast = jnp.sum(jnp.where(_iota((1, C, 1), 1) == C - 1, gcol, 0.0), axis=1, keepdims=True)
    decay = jnp.exp(jnp.where(incl, gcol - grow, NEG))
    q = q * (DN_D ** -0.5)
    kb = k * beta
    lmat = jnp.where(strict, _bmm_nt(kb, k) * decay, 0.0)
    tinv = _tri_inverse(lmat) if tinv is None else _tri_inverse_known(lmat, tinv)
    egc = jnp.exp(gcol)
    u = _bmm(tinv, v * beta)
    w = _bmm(tinv, kb * egc)
    a = _bmm_nt(q, k) * decay
    return u, w, q * egc, k * jnp.exp(glast - gcol), a, jnp.exp(glast), tinv


@jax.custom_vjp
def _bmm_tn(a, b):
    return _bdg(a, b, 1, 1)


def _bmm_tn_fwd(a, b):
    return _bdg(a, b, 1, 1), (a, b)


def _bmm_tn_bwd(res, dy):
    a, b = res
    return _bdg(b, dy, 2, 2), _bdg(a, dy, 2, 1)


_bmm_tn.defvjp(_bmm_tn_fwd, _bmm_tn_bwd)


def _dn_rec(state, u, w, qd, kd, a, cd):
    v_new = u - _bmm(w, state)
    out = _bmm(qd, state) + _bmm(a, v_new)
    return state * cd + _bmm_tn(kd, v_new), out


def _mix_tile(o_attn, o_raw, zs, ga, gb, x, gate1, dn_g, wb_a, wb_d, w_out, p_ya, p_yd, p_out):
    o_dn = jnp.concatenate([_rms(o_raw[h], dn_g) * _silu(zs[h]) for h in range(DN_H)], axis=-1)
    y_a = _mm(o_attn, wb_a) + p_ya
    y_d = _mm(o_dn, wb_d) + p_yd
    merged = _sigmoid(ga) * y_a + _sigmoid(gb) * y_d
    out = _mm(merged, w_out) + p_out
    return x + gate1 * out, o_dn, merged


def _mlp_tile(x1, gain, shift, scale, gate2, w_gu, w_dn, tgt, p_gu, p_yy):
    h2 = _norm_mod(x1, gain, shift, scale)
    half = len(w_gu) // 2
    acts = [_silu(_mm(h2, w_gu[j]) + p_gu[j]) * (_mm(h2, w_gu[j + half]) + p_gu[j + half]) for j in range(half)]
    yy = sum(_mm(a, w) for a, w in zip(acts, w_dn)) + p_yy
    y = x1 + gate2 * yy
    err = y - tgt
    return 0.5 * jnp.sum(err * err) * (1.0 / D), (h2, acts)


def _tok(bt, f):
    return pl.BlockSpec((None, bt, f), lambda b, i: (b, i, 0))


def _full(shape):
    return pl.BlockSpec(shape, lambda b, i: (0,) * len(shape))


def _resident(shape):
    return pl.BlockSpec(shape, lambda b, i: (0,) * len(shape), pipeline_mode=pl.Buffered(1))


def _per_batch(f):
    return pl.BlockSpec((None, 1, f), lambda b, i: (b, 0, 0))


def _sds(shape, dtype):
    return jax.ShapeDtypeStruct(shape, dtype)


class _Exchange(NamedTuple):
    ins: tuple
    out_shapes: tuple
    n_remote: int
    plan: Callable
    n_forward: int = 0
    forward: Callable = None


def _remote_copies(remote, send_sems, recv_sems):
    return [pltpu.make_async_remote_copy(src_ref=src, dst_ref=dst, send_sem=send_sems.at[i], recv_sem=recv_sems.at[i],
                                         device_id=peer, device_id_type=pl.DeviceIdType.MESH)
            for i, (src, dst, peer) in enumerate(remote)]


def _exchange_copies(ex, in_refs, out_refs, send_sems, recv_sems):
    remote = ex.plan((lax.axis_index("x"), lax.axis_index("y"), lax.axis_index("c")), in_refs, out_refs)
    assert len(remote) == ex.n_remote
    return _remote_copies(remote, send_sems, recv_sems)


def _forward_copies(ex, out_refs, send_sems, recv_sems):
    remote = ex.forward((lax.axis_index("x"), lax.axis_index("y"), lax.axis_index("c")), out_refs)
    assert len(remote) == ex.n_forward
    return _remote_copies(remote, send_sems, recv_sems)


def _exchange_sems(ex):
    sems = [pltpu.SemaphoreType.DMA((ex.n_remote,)), pltpu.SemaphoreType.DMA((ex.n_remote,))]
    if ex.n_forward:
        sems += [pltpu.SemaphoreType.DMA((ex.n_forward,)), pltpu.SemaphoreType.DMA((ex.n_forward,))]
    return sems


def _hosted_call(body, name, grid, in_specs, out_specs, out_shape, scratch_shapes, semantics, ins, ex=None):
    if ex is None:
        outs = pl.pallas_call(body, name=name, grid=grid, in_specs=in_specs, out_specs=out_specs, out_shape=out_shape,
                              scratch_shapes=scratch_shapes,
                              compiler_params=_cparams(dimension_semantics=semantics))(*ins)
        return outs, ()
    n_in, n_out, n_scr = len(ins), len(out_shape), len(scratch_shapes)
    c_in, c_out = len(ex.ins), len(ex.out_shapes)
    steps = 1
    for g in grid:
        steps *= g

    def wrapped(*refs):
        a, b, c, d = n_in, n_in + c_in, n_in + c_in + n_out, n_in + c_in + n_out + c_out
        scratch, sems = refs[d:d + n_scr], refs[d + n_scr:]
        step = 0
        for axis, g in enumerate(grid):
            step = step * g + pl.program_id(axis)

        def first_phase():
            return _exchange_copies(ex, refs[a:b], refs[c:d], sems[0], sems[1])

        @pl.when(step == 0)
        def _():
            for cp in first_phase():
                cp.start()
        body(*refs[:a], *refs[b:c], *scratch)

        if ex.n_forward:
            @pl.when(step == (3 * steps) // 4)
            def _():
                for cp in first_phase():
                    cp.wait_recv()
                for cp in _forward_copies(ex, refs[c:d], sems[2], sems[3]):
                    cp.start()

        @pl.when(step == steps - 1)
        def _():
            cps = first_phase()
            if ex.n_forward:
                fwd = _forward_copies(ex, refs[c:d], sems[2], sems[3])
                for cp in fwd:
                    cp.wait_recv()
                for cp in cps + fwd:
                    cp.wait_send()
            else:
                for cp in cps:
                    cp.wait_recv()
                for cp in cps:
                    cp.wait_send()

    any_spec = pl.BlockSpec(memory_space=pl.ANY)
    res = pl.pallas_call(
        wrapped, name=name, grid=grid, in_specs=list(in_specs) + [any_spec] * c_in,
        out_specs=list(out_specs) + [any_spec] * c_out, out_shape=list(out_shape) + list(ex.out_shapes),
        scratch_shapes=list(scratch_shapes) + _exchange_sems(ex),
        compiler_params=_cparams(dimension_semantics=("arbitrary",) * len(grid)),
    )(*ins, *ex.ins)
    return res[:n_out], res[n_out:]


def _acc(ref, val, first):
    @pl.when(first)
    def _():
        ref[...] = val

    @pl.when(jnp.logical_not(first))
    def _():
        ref[...] += val


def _in_proj(x, mod, norm1_g, w_in, bt):
    B, S, _ = x.shape

    def body(x_ref, mod_ref, g_ref, w_ref, q_ref, kv_ref, dn_ref, z_ref, ga_ref, gb_ref, ba_ref, h_ref):
        h = _norm_mod(x_ref[...], g_ref[...], mod_ref[:, 0:D], mod_ref[:, D:2 * D]).astype(BF)
        h_ref[...] = h

        def proj(c0, c1):
            return jnp.dot(h, w_ref[:, c0:c1], preferred_element_type=F32)
        q_ref[...] = proj(0, C_KV).astype(BF)
        kv_ref[...] = proj(C_KV, C_DN).astype(BF)
        dn_ref[...] = proj(C_DN, C_Z).astype(BF)
        z_ref[...] = proj(C_Z, C_GA).astype(BF)
        ga_ref[...] = proj(C_GA, C_GB).astype(BF)
        gb_ref[...] = proj(C_GB, C_BA).astype(BF)
        ba_ref[...] = proj(C_BA, IN_PAD)

    widths = (QW, 2 * KVW, CONVW, DNW, D, D)
    return pl.pallas_call(
        body, name="in_proj", grid=(B, S // bt),
        in_specs=[_tok(bt, D), _per_batch(6 * D), _full((1, D)), _resident((D, IN_PAD))],
        out_specs=[_tok(bt, w) for w in widths] + [_tok(bt, 128), _tok(bt, D)],
        out_shape=[_sds((B, S, w), BF) for w in widths] + [_sds((B, S, 128), F32), _sds((B, S, D), BF)],
        compiler_params=_cparams(dimension_semantics=("parallel", "parallel")),
    )(x, mod, norm1_g, w_in)


def _prev_blk(bt, f):
    return pl.BlockSpec((None, bt, f), lambda b, i: (b, jnp.maximum(i - 1, 0), 0))


QKV = QW + 2 * KVW


def _qk_slabs(q_ref, kv_ref):
    return ([q_ref[:, j * 2 * HD:(j + 1) * 2 * HD].astype(F32) for j in range(QW // (2 * HD))],
            [kv_ref[:, 0:KVW].astype(F32)])


def _qk_prep_fwd(q, kv, cos, sin, qg, kg, bt):
    B, S, _ = q.shape

    def body(q_ref, kv_ref, cos_ref, sin_ref, qg_ref, kg_ref, o_ref):
        qs, ks = _qk_slabs(q_ref, kv_ref)
        qn = _qk_prep(qs, qg_ref[...], cos_ref[...], sin_ref[...])
        kn = _qk_prep(ks, kg_ref[...], cos_ref[...], sin_ref[...])
        for j, t in enumerate(qn + kn):
            o_ref[:, j * 2 * HD:(j + 1) * 2 * HD] = t.astype(BF)
        o_ref[:, QW + KVW:QKV] = kv_ref[:, KVW:2 * KVW]

    return pl.pallas_call(
        body, name="qk_prep_fwd", grid=(B, S // bt),
        in_specs=[_tok(bt, QW), _tok(bt, 2 * KVW), _tok(bt, 2 * HD), _tok(bt, 2 * HD), _full((1, 2 * HD)), _full((1, 2 * HD))],
        out_specs=_tok(bt, QKV), out_shape=_sds((B, S, QKV), BF),
        compiler_params=_cparams(dimension_semantics=("parallel", "parallel")),
    )(q, kv, cos, sin, qg, kg)


def _qk_prep_bwd(q, kv, cos, sin, qg, kg, dqn, dkvn, bt, ex=None):
    B, S, _ = q.shape

    def body(q_ref, kv_ref, cos_ref, sin_ref, qg_ref, kg_ref, dqn_ref, dkvn_ref, dq_ref, dkv_ref, dqg_ref, dkg_ref):
        qs, ks = _qk_slabs(q_ref, kv_ref)
        cos, sin = cos_ref[...], sin_ref[...]

        def f(qs, ks, qg, kg):
            return _qk_prep(qs, qg, cos, sin), _qk_prep(ks, kg, cos, sin)
        _, vjp = jax.vjp(f, qs, ks, qg_ref[...], kg_ref[...])
        n_q = len(qs)
        d_q = [dqn_ref[:, j * 2 * HD:(j + 1) * 2 * HD].astype(F32) for j in range(n_q)]
        d_k = [dkvn_ref[:, 0:KVW].astype(F32)]
        dqs, dks, dqg, dkg = vjp((d_q, d_k))
        for j in range(n_q):
            dq_ref[:, j * 2 * HD:(j + 1) * 2 * HD] = dqs[j].astype(BF)
        dkv_ref[:, 0:KVW] = dks[0].astype(BF)
        dkv_ref[:, KVW:2 * KVW] = dkvn_ref[:, KVW:2 * KVW]
        first = (pl.program_id(0) == 0) & (pl.program_id(1) == 0)
        _acc(dqg_ref, dqg, first)
        _acc(dkg_ref, dkg, first)

    return _hosted_call(
        body, "qk_prep_bwd", (B, S // bt),
        in_specs=[_tok(bt, QW), _tok(bt, 2 * KVW), _tok(bt, 2 * HD), _tok(bt, 2 * HD), _full((1, 2 * HD)), _full((1, 2 * HD)),
                  _tok(bt, QW), _tok(bt, 2 * KVW)],
        out_specs=[_tok(bt, QW), _tok(bt, 2 * KVW), _full((1, 2 * HD)), _full((1, 2 * HD))],
        out_shape=[_sds((B, S, QW), BF), _sds((B, S, 2 * KVW), BF), _sds((1, 2 * HD), F32), _sds((1, 2 * HD), F32)],
        scratch_shapes=[], semantics=("arbitrary", "arbitrary"), ins=(q, kv, cos, sin, qg, kg, dqn, dkvn), ex=ex)


def _attn_load(qkv_ref, kvp_ref):
    qs = [qkv_ref[:, h * HD:(h + 1) * HD].astype(F32) for h in range(HEADS)]
    kc = [qkv_ref[:, QW + h * HD:QW + (h + 1) * HD].astype(F32) for h in range(KV_HEADS)]
    vc = [qkv_ref[:, QW + KVW + h * HD:QW + KVW + (h + 1) * HD].astype(F32) for h in range(KV_HEADS)]
    kp = [kvp_ref[:, h * HD:(h + 1) * HD].astype(F32) for h in range(KV_HEADS)]
    vp = [kvp_ref[:, KVW + h * HD:KVW + (h + 1) * HD].astype(F32) for h in range(KV_HEADS)]
    return qs, kc, kp, vc, vp


def _kv_prev_spec(index):
    return pl.BlockSpec((None, BLK, 2 * KVW), lambda b, i: (b, index(i), QW // (2 * KVW)))


def _attn_fwd(qkv, sinks):
    B, S, _ = qkv.shape

    def body(qkv_ref, kvp_ref, sk_ref, o_ref):
        qs, kc, kp, vc, vp = _attn_load(qkv_ref, kvp_ref)
        outs = _attn_block(qs, kc, kp, vc, vp, sk_ref[...], pl.program_id(1) > 0)
        for h in range(HEADS):
            o_ref[:, h * HD:(h + 1) * HD] = outs[h].astype(BF)

    return pl.pallas_call(
        body, name="attn_fwd", grid=(B, S // BLK),
        in_specs=[_tok(BLK, QKV), _kv_prev_spec(lambda i: jnp.maximum(i - 1, 0)), _full((1, HEADS))],
        out_specs=_tok(BLK, QW), out_shape=_sds((B, S, QW), BF),
        compiler_params=_cparams(dimension_semantics=("parallel", "parallel")),
    )(qkv, qkv, sinks)


def _halo_spec(bt):
    return pl.BlockSpec((None, 8, CONVW), lambda b, i: (b, jnp.maximum(i * (bt // 8) - 1, 0), 0))


def _dn_prep(dn, ba, conv_w, alog, dtb, bt):
    B, S, _ = dn.shape

    strip_rows = min(bt, 64)

    def body(x_ref, halo_ref, ba_ref, cw_ref, al_ref, dt_ref, qkv_ref, bg_ref, y_ref, xe_ref):
        xe_ref[0:8, :] = jnp.where(pl.program_id(1) == 0, 0.0, halo_ref[...].astype(F32))
        xe_ref[8:bt + 8, :] = x_ref[...].astype(F32)

        def strip(k, carry):
            r0 = pl.multiple_of(k * strip_rows, strip_rows)
            rows = pl.ds(r0, strip_rows)
            for j in range(3 * DN_H):
                cols = slice(j * DN_D, (j + 1) * DN_D)
                window = xe_ref[pl.ds(r0, strip_rows + 8), cols]
                y = cw_ref[0:1, cols] * window[5:strip_rows + 5]
                for t in range(1, CONV):
                    y = y + cw_ref[t:t + 1, cols] * window[5 + t:strip_rows + 5 + t]
                y_ref[rows, cols] = y.astype(BF)
                qkv_ref[rows, cols] = _dn_act(y, j < 2 * DN_H)
            bg_ref[rows, :] = _dn_gates(ba_ref[rows, :], al_ref[...], dt_ref[...])
            return carry
        lax.fori_loop(0, bt // strip_rows, strip, 0)

    return pl.pallas_call(
        body, name="dn_prep", grid=(B, S // bt),
        in_specs=[_tok(bt, CONVW), _halo_spec(bt), _tok(bt, 128), _full((CONV, CONVW)), _full((1, 128)), _full((1, 128))],
        out_specs=[_tok(bt, CONVW), _tok(bt, 128), _tok(bt, CONVW)],
        out_shape=[_sds((B, S, CONVW), F32), _sds((B, S, 128), F32), _sds((B, S, CONVW), BF)],
        scratch_shapes=[pltpu.VMEM((bt + 8, CONVW), F32)],
        compiler_params=_cparams(dimension_semantics=("parallel", "arbitrary")),
    )(dn, dn, ba, conv_w, alog, dtb)


def _dn_load(qkv_ref):
    qs = [qkv_ref[:, h * DN_D:(h + 1) * DN_D] for h in range(DN_H)]
    ks = [qkv_ref[:, DNW + h * DN_D:DNW + (h + 1) * DN_D] for h in range(DN_H)]
    vs = [qkv_ref[:, 2 * DNW + h * DN_D:2 * DNW + (h + 1) * DN_D] for h in range(DN_H)]
    return qs, ks, vs


DN_GROUP = 4
AW = DN_H * CHUNK


def _stack_heads(ref, G, offset, width):
    return jnp.stack([ref[g * CHUNK:(g + 1) * CHUNK, offset + h * width:offset + (h + 1) * width]
                      for g in range(G) for h in range(DN_H)])


def _dn_load_stack(qkv_ref, G):
    return tuple(_stack_heads(qkv_ref, G, j * DNW, DN_D) for j in range(3))


def _cd_spec(n):
    return pl.BlockSpec((None, n, 1, DN_D), lambda b, i: (b, i, 0, 0))


def _dn_intra_fwd(qkv, bg, ex=None):
    B, S, _ = qkv.shape
    nc = S // CHUNK
    G = min(DN_GROUP, nc)
    rows = G * CHUNK

    def body(qkv_ref, bg_ref, u_ref, w_ref, qd_ref, kd_ref, a_ref, cd_ref, t_ref):
        q, k, v = _dn_load_stack(qkv_ref, G)
        u, w, qd, kd, a, cd, tinv = _dn_intra(q, k, v, bg_ref[...].reshape(G, CHUNK, DN_D))
        lane_row = _iota((1, DN_D), 1)
        for g in range(G):
            rows = slice(g * CHUNK, (g + 1) * CHUNK)
            cd_row = jnp.zeros((1, DN_D), F32)
            for h in range(DN_H):
                n = g * DN_H + h
                cols = slice(h * DN_D, (h + 1) * DN_D)
                u_ref[rows, cols] = u[n]
                w_ref[rows, cols] = w[n].astype(BF)
                qd_ref[rows, cols] = qd[n].astype(BF)
                kd_ref[rows, cols] = kd[n].astype(BF)
                a_ref[rows, h * CHUNK:(h + 1) * CHUNK] = a[n].astype(BF)
                t_ref[rows, h * CHUNK:(h + 1) * CHUNK] = tinv[n]
                cd_row = cd_row + jnp.where(lane_row == h, cd[n], 0.0)
            cd_ref[g] = cd_row

    return _hosted_call(
        body, "dn_intra_fwd", (B, nc // G),
        in_specs=[_tok(rows, CONVW), _tok(rows, 128)],
        out_specs=[_tok(rows, DNW)] * 4 + [_tok(rows, AW), _cd_spec(G), _tok(rows, AW)],
        out_shape=[_sds((B, S, DNW), F32)] + [_sds((B, S, DNW), BF)] * 3 + [_sds((B, S, AW), BF), _sds((B, nc, 1, DN_D), F32),
                                                                            _sds((B, S, AW), F32)],
        scratch_shapes=[], semantics=("parallel", "parallel"), ins=(qkv, bg), ex=ex)


REC_GROUP = 4


def _rec_stack(ref, B, width, c):
    rows = slice(c * CHUNK, (c + 1) * CHUNK)
    return jnp.stack([ref[b, rows, h * width:(h + 1) * width].astype(F32) for b in range(B) for h in range(DN_H)])


def _rec_load(B, u_ref, w_ref, qd_ref, kd_ref, a_ref, cd_ref, c):
    lane_row = _iota((1, DN_D), 1)
    cd = jnp.stack([jnp.sum(jnp.where(lane_row == h, cd_ref[b, c], 0.0), axis=-1, keepdims=True)
                    for b in range(B) for h in range(DN_H)])
    return (_rec_stack(u_ref, B, DN_D, c), _rec_stack(w_ref, B, DN_D, c), _rec_stack(qd_ref, B, DN_D, c),
            _rec_stack(kd_ref, B, DN_D, c), _rec_stack(a_ref, B, CHUNK, c), cd)


def _rec_store(B, ref, val, width, c):
    for b in range(B):
        for h in range(DN_H):
            ref[b, c * CHUNK:(c + 1) * CHUNK, h * width:(h + 1) * width] = val[b * DN_H + h]


def _rec_specs(B, R, index):
    def tok(f):
        return pl.BlockSpec((B, R * CHUNK, f), lambda i: (0, index(i), 0))
    cd = pl.BlockSpec((B, R, 1, DN_D), lambda i: (0, index(i), 0, 0))
    st = pl.BlockSpec((B, R, DN_H, DN_D, DN_D), lambda i: (0, index(i), 0, 0, 0))
    return tok, cd, st


def _dn_rec_fwd(u, w, qd, kd, a, cd, ex=None):
    B, S, _ = u.shape
    nc = S // CHUNK
    R = REC_GROUP if nc % REC_GROUP == 0 else 1
    tok, cd_spec, st_spec = _rec_specs(B, R, lambda i: i)

    def body(u_ref, w_ref, qd_ref, kd_ref, a_ref, cd_ref, o_ref, st_ref, s_ref):
        @pl.when(pl.program_id(0) == 0)
        def _():
            s_ref[...] = jnp.zeros_like(s_ref)
        state = s_ref[...]
        for c in range(R):
            st_ref[:, c] = state.reshape(B, DN_H, DN_D, DN_D)
            state, out = _dn_rec(state, *_rec_load(B, u_ref, w_ref, qd_ref, kd_ref, a_ref, cd_ref, c))
            _rec_store(B, o_ref, out, DN_D, c)
        s_ref[...] = state

    return _hosted_call(
        body, "dn_rec_fwd", (nc // R,),
        in_specs=[tok(DNW)] * 4 + [tok(AW), cd_spec],
        out_specs=[tok(DNW), st_spec],
        out_shape=[_sds((B, S, DNW), F32), _sds((B, nc, DN_H, DN_D, DN_D), F32)],
        scratch_shapes=[pltpu.VMEM((B * DN_H, DN_D, DN_D), F32)],
        semantics=("arbitrary",), ins=(u, w, qd, kd, a, cd), ex=ex)


def _mix_load(oa_ref, or_ref, z_ref):
    o_raw = [or_ref[:, h * DN_D:(h + 1) * DN_D] for h in range(DN_H)]
    zs = [z_ref[:, h * DN_D:(h + 1) * DN_D].astype(F32) for h in range(DN_H)]
    return oa_ref[...].astype(F32), o_raw, zs


def _mix_fwd(o_attn, o_raw, z, ga, gb, x, mod, dn_g, w_branch, w_out, bt):
    B, S, _ = x.shape

    def body(oa_ref, or_ref, z_ref, ga_ref, gb_ref, x_ref, mod_ref, dg_ref, wb_ref, wo_ref, x1_ref, od_ref, mg_ref):
        oa, o_r, zs = _mix_load(oa_ref, or_ref, z_ref)
        x1, o_dn, merged = _mix_tile(oa, o_r, zs, ga_ref[...].astype(F32), gb_ref[...].astype(F32), x_ref[...],
                                     mod_ref[:, 2 * D:3 * D], dg_ref[...], wb_ref[0:QW, :], wb_ref[QW:2 * QW, :],
                                     wo_ref[...], 0.0, 0.0, 0.0)
        x1_ref[...] = x1
        od_ref[...] = o_dn.astype(BF)
        mg_ref[...] = merged.astype(BF)

    return pl.pallas_call(
        body, name="mix_fwd", grid=(B, S // bt),
        in_specs=[_tok(bt, QW), _tok(bt, DNW), _tok(bt, DNW), _tok(bt, D), _tok(bt, D), _tok(bt, D), _per_batch(6 * D),
                  _full((1, DN_D)), _resident((D, D)), _resident((D, D))],
        out_specs=[_tok(bt, D), _tok(bt, DNW), _tok(bt, D)],
        out_shape=[_sds((B, S, D), F32), _sds((B, S, DNW), BF), _sds((B, S, D), BF)],
        compiler_params=_cparams(dimension_semantics=("parallel", "parallel")),
    )(o_attn, o_raw, z, ga, gb, x, mod, dn_g, w_branch, w_out)


def _mlp(x1, tgt, mod, norm2_g, w_gu, w_dn, bt):
    B, S, _ = x1.shape

    def body(x1_ref, t_ref, mod_ref, g_ref, wgu_ref, wdn_ref,
             dx1_ref, h2_ref, act_ref, dgu_ref, dyy_ref, loss_ref, dmod_ref, dg_ref):
        blk = 2 * FFN // N_CHIP
        w_gu_v, t = [wgu_ref[k] for k in range(N_CHIP)], t_ref[...]
        w_dn_v = [wdn_ref[j * blk:(j + 1) * blk, :] for j in range(N_CHIP // 2)]

        def f(x1, gain, shift, scale, gate2, p_gu, p_yy):
            return _mlp_tile(x1, gain, shift, scale, gate2, w_gu_v, w_dn_v, t, p_gu, p_yy)
        zero_gu = [jnp.zeros((bt, blk), F32)] * N_CHIP
        zero_yy = jnp.zeros((bt, D), F32)
        loss, vjp, (h2, acts) = jax.vjp(f, x1_ref[...], g_ref[...], mod_ref[:, 3 * D:4 * D], mod_ref[:, 4 * D:5 * D],
                                        mod_ref[:, 5 * D:6 * D], zero_gu, zero_yy, has_aux=True)
        dx1, dgain, dshift, dscale, dgate2, dgu, dyy = vjp(jnp.ones((), F32))
        dx1_ref[...] = dx1
        h2_ref[...] = h2.astype(BF)
        for j, a in enumerate(acts):
            act_ref[:, j * blk:(j + 1) * blk] = a.astype(BF)
        for k, d in enumerate(dgu):
            dgu_ref[:, k * blk:(k + 1) * blk] = d.astype(BF)
        dyy_ref[...] = dyy.astype(BF)
        first = pl.program_id(1) == 0
        _acc(loss_ref, jnp.reshape(loss, (1, 1)), first)
        _acc(dmod_ref, jnp.concatenate([dshift, dscale, dgate2], axis=-1), first)
        _acc(dg_ref, dgain, first)

    return pl.pallas_call(
        body, name="mlp", grid=(B, S // bt),
        in_specs=[_tok(bt, D), _tok(bt, D), _per_batch(6 * D), _full((1, D)), _resident((N_CHIP, D, 2 * FFN // N_CHIP)),
                  _resident((FFN, D))],
        out_specs=[_tok(bt, D), _tok(bt, D), _tok(bt, FFN), _tok(bt, 2 * FFN), _tok(bt, D),
                   _per_batch(1), _per_batch(3 * D), _per_batch(D)],
        out_shape=[_sds((B, S, D), F32), _sds((B, S, D), BF), _sds((B, S, FFN), BF), _sds((B, S, 2 * FFN), BF),
                   _sds((B, S, D), BF), _sds((B, 1, 1), F32), _sds((B, 1, 3 * D), F32), _sds((B, 1, D), F32)],
        compiler_params=_cparams(dimension_semantics=("parallel", "arbitrary")),
    )(x1, tgt, mod, norm2_g, w_gu, w_dn)


def _mix_bwd(o_attn, o_raw, z, ga, gb, x, mod, dn_g, w_branch, w_out, dx1, bt, ex=None):
    B, S, _ = x.shape

    def body(oa_ref, or_ref, z_ref, ga_ref, gb_ref, x_ref, mod_ref, dg_ref, wb_ref, wo_ref, dx1_ref,
             doa_ref, dor_ref, dz_ref, dga_ref, dgb_ref, dya_ref, dyd_ref, dout_ref, dgate_ref, ddg_ref):
        oa, o_r, zs = _mix_load(oa_ref, or_ref, z_ref)
        wb_a, wb_d, wo = wb_ref[0:QW, :], wb_ref[QW:2 * QW, :], wo_ref[...]

        def f(oa, o_r, zs, ga, gb, gate1, dn_g, p_ya, p_yd, p_out):
            return _mix_tile(oa, o_r, zs, ga, gb, x_ref[...], gate1, dn_g, wb_a, wb_d, wo, p_ya, p_yd, p_out)[0]
        zero = jnp.zeros((bt, D), F32)
        _, vjp = jax.vjp(f, oa, o_r, zs, ga_ref[...].astype(F32), gb_ref[...].astype(F32), mod_ref[:, 2 * D:3 * D],
                         dg_ref[...], zero, zero, zero)
        doa, dor, dzs, dga, dgb, dgate1, ddn_g, dya, dyd, dout = vjp(dx1_ref[...])
        doa_ref[...] = doa
        for h in range(DN_H):
            dor_ref[:, h * DN_D:(h + 1) * DN_D] = dor[h]
            dz_ref[:, h * DN_D:(h + 1) * DN_D] = dzs[h].astype(BF)
        dga_ref[...] = dga.astype(BF)
        dgb_ref[...] = dgb.astype(BF)
        dya_ref[...] = dya.astype(BF)
        dyd_ref[...] = dyd.astype(BF)
        dout_ref[...] = dout.astype(BF)
        first = pl.program_id(1) == 0
        _acc(dgate_ref, dgate1, first)
        _acc(ddg_ref, ddn_g, first)

    return _hosted_call(
        body, "mix_bwd", (B, S // bt),
        in_specs=[_tok(bt, QW), _tok(bt, DNW), _tok(bt, DNW), _tok(bt, D), _tok(bt, D), _tok(bt, D), _per_batch(6 * D),
                  _full((1, DN_D)), _resident((D, D)), _resident((D, D)), _tok(bt, D)],
        out_specs=[_tok(bt, QW), _tok(bt, DNW), _tok(bt, DNW), _tok(bt, D), _tok(bt, D), _tok(bt, D), _tok(bt, D), _tok(bt, D),
                   _per_batch(D), _per_batch(DN_D)],
        out_shape=[_sds((B, S, QW), F32), _sds((B, S, DNW), F32), _sds((B, S, DNW), BF), _sds((B, S, D), BF),
                   _sds((B, S, D), BF), _sds((B, S, D), BF), _sds((B, S, D), BF), _sds((B, S, D), BF),
                   _sds((B, 1, D), F32), _sds((B, 1, DN_D), F32)],
        scratch_shapes=[], semantics=("parallel", "arbitrary"),
        ins=(o_attn, o_raw, z, ga, gb, x, mod, dn_g, w_branch, w_out, dx1), ex=ex)


def _dn_rec_bwd(u, w, qd, kd, a, cd, states, d_o, ex=None):
    B, S, _ = u.shape
    nc = S // CHUNK
    R = REC_GROUP if nc % REC_GROUP == 0 else 1
    tok, cd_spec, st_spec = _rec_specs(B, R, lambda i: nc // R - 1 - i)

    def body(u_ref, w_ref, qd_ref, kd_ref, a_ref, cd_ref, st_ref, do_ref,
             du_ref, dw_ref, dqd_ref, dkd_ref, da_ref, dcd_ref, ds_ref):
        @pl.when(pl.program_id(0) == 0)
        def _():
            ds_ref[...] = jnp.zeros_like(ds_ref)
        lane_row = _iota((1, DN_D), 1)
        d_state = ds_ref[...]
        for c in reversed(range(R)):
            state = st_ref[:, c].reshape(B * DN_H, DN_D, DN_D)
            _, vjp = jax.vjp(_dn_rec, state, *_rec_load(B, u_ref, w_ref, qd_ref, kd_ref, a_ref, cd_ref, c))
            d_state, du, dw, dqd, dkd, da, dcd = vjp((d_state, _rec_stack(do_ref, B, DN_D, c)))
            for ref, val, width in ((du_ref, du, DN_D), (dw_ref, dw, DN_D), (dqd_ref, dqd, DN_D), (dkd_ref, dkd, DN_D),
                                    (da_ref, da, CHUNK)):
                _rec_store(B, ref, val, width, c)
            for b in range(B):
                row = jnp.zeros((1, DN_D), F32)
                for h in range(DN_H):
                    row = row + jnp.where(lane_row == h, dcd[b * DN_H + h], 0.0)
                dcd_ref[b, c] = row
        ds_ref[...] = d_state

    return _hosted_call(
        body, "dn_rec_bwd", (nc // R,),
        in_specs=[tok(DNW)] * 4 + [tok(AW), cd_spec, st_spec, tok(DNW)],
        out_specs=[tok(DNW)] * 4 + [tok(AW), cd_spec],
        out_shape=[_sds((B, S, DNW), F32)] * 4 + [_sds((B, S, AW), F32), _sds((B, nc, 1, DN_D), F32)],
        scratch_shapes=[pltpu.VMEM((B * DN_H, DN_D, DN_D), F32)],
        semantics=("arbitrary",), ins=(u, w, qd, kd, a, cd, states, d_o), ex=ex)


def _dn_intra_bwd(qkv, bg, tinv, du, dw, dqd, dkd, da, dcd, ex=None):
    B, S, _ = qkv.shape
    nc = S // CHUNK
    G = min(DN_GROUP, nc)
    rows = G * CHUNK

    def body(qkv_ref, bg_ref, t_ref, du_ref, dw_ref, dqd_ref, dkd_ref, da_ref, dcd_ref, dqkv_ref, dbg_ref):
        q, k, v = _dn_load_stack(qkv_ref, G)
        known = _stack_heads(t_ref, G, 0, CHUNK)
        _, vjp = jax.vjp(lambda q, k, v, bg: _dn_intra(q, k, v, bg, known)[:6], q, k, v,
                         bg_ref[...].reshape(G, CHUNK, DN_D))
        lane_row = _iota((1, DN_D), 1)
        dcd = jnp.stack([jnp.sum(jnp.where(lane_row == h, dcd_ref[g], 0.0), axis=-1, keepdims=True)
                         for g in range(G) for h in range(DN_H)])
        dq, dk, dv, dbg = vjp((_stack_heads(du_ref, G, 0, DN_D), _stack_heads(dw_ref, G, 0, DN_D),
                               _stack_heads(dqd_ref, G, 0, DN_D), _stack_heads(dkd_ref, G, 0, DN_D),
                               _stack_heads(da_ref, G, 0, CHUNK), dcd))
        for g in range(G):
            rows = slice(g * CHUNK, (g + 1) * CHUNK)
            for h in range(DN_H):
                n = g * DN_H + h
                dqkv_ref[rows, h * DN_D:(h + 1) * DN_D] = dq[n]
                dqkv_ref[rows, DNW + h * DN_D:DNW + (h + 1) * DN_D] = dk[n]
                dqkv_ref[rows, 2 * DNW + h * DN_D:2 * DNW + (h + 1) * DN_D] = dv[n]
        dbg_ref[...] = dbg.reshape(G * CHUNK, DN_D)

    return _hosted_call(
        body, "dn_intra_bwd", (B, nc // G),
        in_specs=[_tok(rows, CONVW), _tok(rows, 128), _tok(rows, AW)] + [_tok(rows, DNW)] * 4 + [_tok(rows, AW), _cd_spec(G)],
        out_specs=[_tok(rows, CONVW), _tok(rows, 128)],
        out_shape=[_sds((B, S, CONVW), F32), _sds((B, S, 128), F32)],
        scratch_shapes=[], semantics=("parallel", "parallel"), ins=(qkv, bg, tinv, du, dw, dqd, dkd, da, dcd), ex=ex)


def _dn_prep_bwd(dn, y, ba, conv_w, alog, dtb, dqkv, dbg, bt, ex=None):
    B, S, _ = dn.shape
    nt = S // bt
    strip_rows = min(bt, 64)

    def rev(f):
        return pl.BlockSpec((None, bt, f), lambda b, i: (b, nt - 1 - i, 0))

    def body(x_ref, y_ref, ba_ref, cw_ref, al_ref, dt_ref, dqkv_ref, dbg_ref,
             dx_ref, dba_ref, dcw_ref, dal_ref, ddt_ref, dye_ref):
        i = pl.program_id(1)
        @pl.when(i == 0)
        def _():
            dye_ref[bt:bt + 8, :] = jnp.zeros((8, CONVW), F32)

        @pl.when(i > 0)
        def _():
            dye_ref[bt:bt + 8, :] = dye_ref[0:8, :]

        n_strips = bt // strip_rows

        def strip(k, carry):
            dal, ddt, dcw = carry
            r0 = pl.multiple_of((n_strips - 1 - k) * strip_rows, strip_rows)
            rows = pl.ds(r0, strip_rows)
            dcw_slabs = []
            for j in range(3 * DN_H):
                cols = slice(j * DN_D, (j + 1) * DN_D)
                _, vjp = jax.vjp(functools.partial(_dn_act, normalize=j < 2 * DN_H), y_ref[rows, cols].astype(F32))
                (dye_ref[rows, cols],) = vjp(dqkv_ref[rows, cols])
                window = dye_ref[pl.ds(r0, strip_rows + 8), cols]
                shifted = [window[3 - t:strip_rows + 3 - t] for t in range(CONV)]
                dx = cw_ref[0:1, cols] * shifted[0]
                for t in range(1, CONV):
                    dx = dx + cw_ref[t:t + 1, cols] * shifted[t]
                dx_ref[rows, cols] = dx.astype(BF)
                x = x_ref[rows, cols].astype(F32)
                dcw_slabs.append(jnp.concatenate([jnp.sum(shifted[t] * x, axis=0, keepdims=True) for t in range(CONV)], axis=0))
            _, vjp = jax.vjp(_dn_gates, ba_ref[rows, :], al_ref[...], dt_ref[...])
            dba_ref[rows, :], da, dd = vjp(dbg_ref[rows, :])
            return dal + da, ddt + dd, dcw + jnp.concatenate(dcw_slabs, axis=1)
        zero = jnp.zeros((1, DN_D), F32)
        dal, ddt, dcw = lax.fori_loop(0, n_strips, strip, (zero, zero, jnp.zeros((CONV, CONVW), F32)))
        first = (i == 0) & (pl.program_id(0) == 0)
        _acc(dcw_ref, dcw, first)
        _acc(dal_ref, dal, first)
        _acc(ddt_ref, ddt, first)

    return _hosted_call(
        body, "dn_prep_bwd", (B, nt),
        in_specs=[rev(CONVW), rev(CONVW), rev(128), _full((CONV, CONVW)), _full((1, 128)), _full((1, 128)), rev(CONVW), rev(128)],
        out_specs=[rev(CONVW), rev(128), _full((CONV, CONVW)), _full((1, 128)), _full((1, 128))],
        out_shape=[_sds((B, S, CONVW), BF), _sds((B, S, 128), F32), _sds((CONV, CONVW), F32), _sds((1, 128), F32),
                   _sds((1, 128), F32)],
        scratch_shapes=[pltpu.VMEM((bt + 8, CONVW), F32)],
        semantics=("arbitrary", "arbitrary"), ins=(dn, y, ba, conv_w, alog, dtb, dqkv, dbg), ex=ex)


def _attn_bwd(qkv, sinks, d_o, ex=None):
    B, S, _ = qkv.shape
    nb = S // BLK

    def cur(f):
        return pl.BlockSpec((None, BLK, f), lambda b, i: (b, jnp.minimum(i, nb - 1), 0))

    def out_prev(f):
        return pl.BlockSpec((None, BLK, f), lambda b, i: (b, jnp.maximum(i - 1, 0), 0))

    def body(qkv_ref, kvp_ref, sk_ref, do_ref, dq_ref, dkv_ref, dsk_ref, carry_ref):
        n = pl.program_id(1)
        first = (n == 0) & (pl.program_id(0) == 0)

        @pl.when(n == 0)
        def _():
            carry_ref[...] = jnp.zeros_like(carry_ref)

        @pl.when(n < nb)
        def _():
            qs, kc, kp, vc, vp = _attn_load(qkv_ref, kvp_ref)

            def f(qs, kc, kp, vc, vp, sk):
                return _attn_block(qs, kc, kp, vc, vp, sk, n > 0)
            _, vjp = jax.vjp(f, qs, kc, kp, vc, vp, sk_ref[...])
            d_outs = [do_ref[:, h * HD:(h + 1) * HD] for h in range(HEADS)]
            dqs, dkc, dkp, dvc, dvp, dsk = vjp(d_outs)
            for h in range(HEADS):
                dq_ref[:, h * HD:(h + 1) * HD] = dqs[h].astype(BF)
            for h in range(KV_HEADS):
                ksl = slice(h * HD, (h + 1) * HD)
                vsl = slice(KVW + h * HD, KVW + (h + 1) * HD)
                dkv_ref[:, ksl] = (carry_ref[:, ksl] + dkp[h]).astype(BF)
                dkv_ref[:, vsl] = (carry_ref[:, vsl] + dvp[h]).astype(BF)
                carry_ref[:, ksl] = dkc[h]
                carry_ref[:, vsl] = dvc[h]
            _acc(dsk_ref, dsk, first)

        @pl.when(n == nb)
        def _():
            dkv_ref[...] = carry_ref[...].astype(BF)

    return _hosted_call(
        body, "attn_bwd", (B, nb + 1),
        in_specs=[cur(QKV), _kv_prev_spec(lambda i: jnp.maximum(jnp.minimum(i, nb - 1) - 1, 0)), _full((1, HEADS)), cur(QW)],
        out_specs=[cur(QW), out_prev(2 * KVW), _full((1, HEADS))],
        out_shape=[_sds((B, S, QW), BF), _sds((B, S, 2 * KVW), BF), _sds((1, HEADS), F32)],
        scratch_shapes=[pltpu.VMEM((BLK, 2 * KVW), F32)],
        semantics=("arbitrary", "arbitrary"), ins=(qkv, qkv, sinks, d_o), ex=ex)


def _in_proj_bwd(x, mod, norm1_g, w_in, pieces, dba, dx1, bt):
    B, S, _ = x.shape
    widths = (QW, 2 * KVW, CONVW, DNW, D, D)

    def body(x_ref, mod_ref, g_ref, w_ref, dq_ref, dkv_ref, ddn_ref, dz_ref, dga_ref, dgb_ref, dba_ref, dx1_ref,
             gx_ref, dp_ref, dmod_ref, dg_ref):
        dp = jnp.concatenate([r[...] for r in (dq_ref, dkv_ref, ddn_ref, dz_ref, dga_ref, dgb_ref)]
                             + [dba_ref[...].astype(BF)], axis=-1)
        dp_ref[...] = dp
        dh = lax.dot_general(dp, w_ref[...], (((1,), (1,)), ((), ())), preferred_element_type=F32)
        _, vjp = jax.vjp(_norm_mod, x_ref[...], g_ref[...], mod_ref[:, 0:D], mod_ref[:, D:2 * D])
        dx, dgain, dshift, dscale = vjp(dh)
        gx_ref[...] = dx + dx1_ref[...]
        first = pl.program_id(1) == 0
        _acc(dmod_ref, jnp.concatenate([dshift, dscale], axis=-1), first)
        _acc(dg_ref, dgain, first)

    return pl.pallas_call(
        body, name="in_proj_bwd", grid=(B, S // bt),
        in_specs=[_tok(bt, D), _per_batch(6 * D), _full((1, D)), _resident((D, IN_PAD))] + [_tok(bt, w) for w in widths]
        + [_tok(bt, 128), _tok(bt, D)],
        out_specs=[_tok(bt, D), _tok(bt, IN_PAD), _per_batch(2 * D), _per_batch(D)],
        out_shape=[_sds((B, S, D), F32), _sds((B, S, IN_PAD), BF), _sds((B, 1, 2 * D), F32), _sds((B, 1, D), F32)],
        compiler_params=_cparams(dimension_semantics=("parallel", "arbitrary")),
    )(x, mod, norm1_g, w_in, *pieces, dba, dx1)


def _matmul_tn(tag, a, b, bk, bn, bt, col_blocks=False, ex=None):
    T, K = a.shape
    N = b.shape[1]
    nt = T // bt
    if col_blocks:
        assert bk == K
        out_spec = pl.BlockSpec((None, bk, bn), lambda i, j, t: (j, 0, 0))
        out_shape = _sds((N // bn, K, bn), F32)
    else:
        out_spec = pl.BlockSpec((bk, bn), lambda i, j, t: (i, j))
        out_shape = _sds((K, N), F32)

    def body(a_ref, b_ref, o_ref, acc_ref):
        t = pl.program_id(2)

        @pl.when(t == 0)
        def _():
            acc_ref[...] = jnp.zeros_like(acc_ref)
        acc_ref[...] += lax.dot_general(a_ref[...], b_ref[...], (((0,), (0,)), ((), ())), preferred_element_type=F32)

        @pl.when(t == nt - 1)
        def _():
            o_ref[...] = acc_ref[...]

    (out,), landed = _hosted_call(
        body, f"grad_{tag}", (K // bk, N // bn, nt),
        in_specs=[pl.BlockSpec((bt, bk), lambda i, j, t: (t, i)), pl.BlockSpec((bt, bn), lambda i, j, t: (t, j))],
        out_specs=[out_spec], out_shape=[out_shape],
        scratch_shapes=[pltpu.VMEM((bk, bn), F32)],
        semantics=("parallel", "parallel", "arbitrary"), ins=(a, b), ex=ex)
    return out if ex is None else (out, landed)


def _rope_table(positions):
    inv_freq = THETA ** (-jnp.arange(0, ROT, 2, dtype=F32) / ROT)
    rest = jnp.zeros((HD - ROT,), F32)
    freq = jnp.concatenate([inv_freq, inv_freq, rest] * 2)
    sign = jnp.concatenate([-jnp.ones_like(inv_freq), jnp.ones_like(inv_freq), rest] * 2)
    ang = positions.astype(F32)[..., None] * freq
    return jnp.cos(ang), jnp.sin(ang) * sign


def _lane_pad(v, offset, width=128):
    return jnp.zeros((1, width), F32).at[0, offset:offset + v.shape[-1]].set(v.reshape(-1))


def _tile(S, want):
    return min(S, want)


class _Hosted:
    def __init__(self, call):
        self.call = call
        self.outs = None

    def __call__(self, ex):
        self.outs, landed = self.call(ex)
        return landed


def _local_step(x, mod, positions, tgt, norm1_g, w_in_pad, conv_w, q_norm_g, k_norm_g, sinks, a_log, dt_bias,
                dn_norm_g, w_branch, w_out, norm2_g, w_gu, w_dn, dist=None):
    B, S, _ = x.shape
    T = B * S
    cos_t, sin_t = _rope_table(positions)
    qg2 = jnp.concatenate([q_norm_g, q_norm_g], axis=-1)
    kg2 = jnp.concatenate([k_norm_g, k_norm_g], axis=-1)
    alog = _lane_pad(a_log, DN_H)
    dtb = _lane_pad(dt_bias, DN_H)
    conv2 = conv_w.reshape(CONV, CONVW)
    bt = _tile(S, 512)
    bt_mlp = _tile(S, 256)

    q, kv, dn, z, ga, gb, ba, h1 = _in_proj(x, mod, norm1_g, w_in_pad, bt)
    qkv_n = _qk_prep_fwd(q, kv, cos_t, sin_t, qg2, kg2, bt)
    o_attn = _attn_fwd(qkv_n, sinks)
    dqkv, bg, dn_y = _dn_prep(dn, ba, conv2, alog, dtb, bt)
    intra = _Hosted(lambda ex: _dn_intra_fwd(dqkv, bg, ex))
    if dist is None:
        intra(None)
    else:
        f_br, f_out, w_gu, f_dn = _gather_weights("late", [w_branch, w_out, w_gu, w_dn], dist[0], host=intra)
        w_branch, w_out, w_dn = (f.reshape(N_CHIP * f.shape[1], f.shape[2]) for f in (f_br, f_out, f_dn))
    dn_u, dn_w, dn_qd, dn_kd, dn_a, dn_cd, dn_tinv = intra.outs
    (o_raw, states), _ = _dn_rec_fwd(dn_u, dn_w, dn_qd, dn_kd, dn_a, dn_cd)
    x1, o_dn, merged = _mix_fwd(o_attn, o_raw, z, ga, gb, x, mod, dn_norm_g, w_branch, w_out, bt)
    dx1, h2, act, dgu, dyy, loss, dmod2, dnorm2 = _mlp(x1, tgt, mod, norm2_g, w_gu, w_dn, bt_mlp)

    def flat(t):
        return t.reshape(T, t.shape[-1])
    tn = functools.partial(_matmul_tn, bt=_tile(T, 2048))
    g_w_dn = tn("w_down", flat(act), flat(dyy), bk=FFN, bn=D // 2)
    g_w_gu = tn("w_gate_up", flat(h2), flat(dgu), bk=D, bn=2 * FFN // N_CHIP, col_blocks=True)

    mix_b = _Hosted(lambda ex: _mix_bwd(o_attn, o_raw, z, ga, gb, x, mod, dn_norm_g, w_branch, w_out, dx1, bt_mlp, ex))
    rec_b = _Hosted(lambda ex: _dn_rec_bwd(dn_u, dn_w, dn_qd, dn_kd, dn_a, dn_cd, states, mix_b.outs[1], ex))
    intra_b = _Hosted(lambda ex: _dn_intra_bwd(dqkv, bg, dn_tinv, *rec_b.outs, ex))
    if dist is None:
        for host in (mix_b, rec_b, intra_b):
            host(None)
    else:
        g_w_gu, g_w_dn = _reduce_grads(("w_gate_up", "w_down"), [g_w_gu, g_w_dn.reshape(N_CHIP, -1, D)], *dist,
                                       hosts=[mix_b, rec_b, intra_b])
    d_oa, _, dz, dga, dgb, dya, dyd, dout, dgate1, ddn_g = mix_b.outs
    d_dqkv, dbg = intra_b.outs
    g_w_out = tn("w_out", flat(merged), flat(dout), bk=D, bn=D)
    g_w_br = jnp.concatenate([tn("w_branch_attn", flat(o_attn), flat(dya), bk=QW, bn=D),
                              tn("w_branch_dn", flat(o_dn), flat(dyd), bk=DNW, bn=D)], axis=0)
    prep_b = _Hosted(lambda ex: _dn_prep_bwd(dn, dn_y, ba, conv2, alog, dtb, d_dqkv, dbg, bt, ex))
    attn_b = _Hosted(lambda ex: _attn_bwd(qkv_n, sinks, d_oa, ex))
    qk_b = _Hosted(lambda ex: _qk_prep_bwd(q, kv, cos_t, sin_t, qg2, kg2, *attn_b.outs[:2], bt, ex))
    if dist is None:
        for host in (prep_b, attn_b, qk_b):
            host(None)
    else:
        g_w_br, g_w_out = _reduce_grads(("w_branch", "w_out"), [g_w_br.reshape(N_CHIP, -1, D), g_w_out.reshape(N_CHIP, -1, D)],
                                        *dist, hosts=[prep_b, attn_b, qk_b])
    d_dn, dba, dconv, dalog, ddtb = prep_b.outs
    dsk = attn_b.outs[2]
    dq, dkv, dqg2, dkg2 = qk_b.outs
    dqg = dqg2[:, :HD] + dqg2[:, HD:]
    dkg = dkg2[:, :HD] + dkg2[:, HD:]
    grad_x, dproj, dmod1, dnorm1 = _in_proj_bwd(x, mod, norm1_g, w_in_pad, (dq, dkv, d_dn, dz, dga, dgb), dba, dx1, bt)
    g_w_in = _Hosted(lambda ex: (tn("w_in", flat(h1), flat(dproj), bk=D, bn=IN_PAD // 3), ()) if ex is None
                     else tn("w_in", flat(h1), flat(dproj), bk=D, bn=IN_PAD // 3, ex=ex))
    if dist is None:
        g_w_in(None)
        g_w_in = g_w_in.outs

    dmod = jnp.concatenate([dmod1, dgate1, dmod2], axis=-1)
    small = dict(norm1_g=jnp.sum(dnorm1, axis=0), norm2_g=jnp.sum(dnorm2, axis=0), q_norm_g=dqg, k_norm_g=dkg,
                 sinks=dsk, a_log=dalog[:, DN_H:2 * DN_H], dt_bias=ddtb[:, DN_H:2 * DN_H],
                 dn_norm_g=jnp.sum(ddn_g, axis=0), conv_w=dconv)
    return jnp.sum(loss), grad_x, dmod, small, (g_w_in, g_w_br, g_w_out, g_w_gu, g_w_dn)


def _flip(me, f):
    return (me[0] ^ ((f >> 2) & 1), me[1] ^ ((f >> 1) & 1), me[2] ^ (f & 1))


def _comm_call(name, ex):
    n_in, n_out = len(ex.ins), len(ex.out_shapes)

    def body(*refs):
        out_refs, sems = refs[n_in:n_in + n_out], refs[n_in + n_out:]
        cps = _exchange_copies(ex, refs[:n_in], out_refs, sems[0], sems[1])
        for cp in cps:
            cp.start()
        for cp in cps:
            cp.wait_recv()
        if ex.n_forward:
            fwd = _forward_copies(ex, out_refs, sems[2], sems[3])
            for cp in fwd:
                cp.start()
            for cp in fwd:
                cp.wait_recv()
            cps = cps + fwd
        for cp in cps:
            cp.wait_send()

    any_spec = pl.BlockSpec(memory_space=pl.ANY)
    return pl.pallas_call(
        body, name=name, in_specs=[any_spec] * n_in, out_specs=[any_spec] * n_out, out_shape=list(ex.out_shapes),
        scratch_shapes=_exchange_sems(ex),
    )(*ex.ins)


def _by_origin(own, received, index):
    stack = jnp.concatenate([own[None], received], axis=0)
    n = stack.shape[0]
    return jnp.stack([lax.dynamic_index_in_dim(stack, k ^ index, 0, keepdims=False) for k in range(n)])


def _gather_devices(name, arrs, dev, host=None):
    def plan(me, in_refs, out_refs):
        return [(a, o.at[f - 1], _flip(me, f)) for a, o in zip(in_refs, out_refs) for f in range(1, N_DEV)]
    outs = tuple(_sds((N_DEV - 1,) + a.shape, a.dtype) for a in arrs)
    got = (host or functools.partial(_comm_call, name))(_Exchange(tuple(arrs), outs, (N_DEV - 1) * len(arrs), plan))
    return [_by_origin(a, g, dev) for a, g in zip(arrs, got)]


def _gather_chips(name, arrs, chip):
    def plan(me, in_refs, out_refs):
        return [(a, o.at[j], _flip(me, 2 * (j + 1))) for a, o in zip(in_refs, out_refs) for j in range(N_CHIP - 1)]
    outs = tuple(_sds((N_CHIP - 1,) + a.shape, a.dtype) for a in arrs)
    got = _comm_call(name, _Exchange(tuple(arrs), outs, (N_CHIP - 1) * len(arrs), plan))
    return [_by_origin(a, g, chip) for a, g in zip(arrs, got)]


def _swap_cores_ex(arrs):
    def plan(me, in_refs, out_refs):
        return [(g, o, _flip(me, 1)) for g, o in zip(in_refs, out_refs)]
    return _Exchange(tuple(arrs), tuple(_sds(g.shape, g.dtype) for g in arrs), len(arrs), plan)


def _gather_weights(tag, shards, chip, host=None):
    def plan(me, in_refs, out_refs):
        chip_me = 2 * me[0] + me[1]
        remote = []
        for a, o in zip(in_refs, out_refs):
            half = a.shape[0] // 2
            mine = a.at[pl.ds(me[2] * half, half)]
            remote += [(mine, o.at[chip_me, me[2]], _flip(me, 2 * (j + 1))) for j in range(N_CHIP - 1)]
        return remote

    def forward(me, out_refs):
        chip_me = 2 * me[0] + me[1]
        return [(o.at[chip_me ^ (j + 1), me[2]], o.at[chip_me ^ (j + 1), me[2]], _flip(me, 1))
                for o in out_refs for j in range(N_CHIP - 1)]
    run = host or functools.partial(_comm_call, f"weights_{tag}")
    n = (N_CHIP - 1) * len(shards)
    landed = run(_Exchange(tuple(shards), tuple(_sds((N_CHIP, 2, a.shape[0] // 2, a.shape[1]), a.dtype) for a in shards),
                           n, plan, n, forward))
    return [lax.dynamic_update_slice(f.reshape((N_CHIP,) + a.shape), a[None], (chip, 0, 0)) for a, f in zip(shards, landed)]


def _rows(r):
    for br in (512, 352, 256, 128, 64, 32, 16, 8):
        if r % br == 0:
            return br
    raise ValueError(r)


def _pair_add(tag, g, recv, c):
    n, r, cols = g.shape
    half = r // 2
    br = _rows(half)
    nb = half // br

    def body(c_ref, g_ref, r_ref, o_ref):
        o_ref[...] = (g_ref[...] + r_ref[...]).astype(BF)

    return pl.pallas_call(
        body, name=f"pair_add_{tag}",
        grid_spec=pltpu.PrefetchScalarGridSpec(
            num_scalar_prefetch=1, grid=(n, nb),
            in_specs=[pl.BlockSpec((None, br, cols), lambda k, i, c_ref: (k, c_ref[0] * nb + i, 0)),
                      pl.BlockSpec((None, br, cols), lambda k, i, c_ref: (k, i, 0))],
            out_specs=pl.BlockSpec((None, br, cols), lambda k, i, c_ref: (k, i, 0))),
        out_shape=_sds((n, half, cols), BF),
        compiler_params=_cparams(dimension_semantics=("parallel", "parallel")),
    )(c, g, recv)


def _sum_chips(tag, p, q, chip):
    n, r, cols = q.shape
    br = _rows(r)

    def body(chip_ref, p_ref, q_ref, o_ref):
        acc = p_ref[...].astype(F32)
        for k in range(n):
            acc = acc + q_ref[k].astype(F32)
        o_ref[...] = acc

    return pl.pallas_call(
        body, name=f"sum_chips_{tag}",
        grid_spec=pltpu.PrefetchScalarGridSpec(
            num_scalar_prefetch=1, grid=(r // br,),
            in_specs=[pl.BlockSpec((None, br, cols), lambda i, chip_ref: (chip_ref[0], i, 0)),
                      pl.BlockSpec((n, br, cols), lambda i, chip_ref: (0, i, 0))],
            out_specs=pl.BlockSpec((br, cols), lambda i, chip_ref: (i, 0))),
        out_shape=_sds((r, cols), F32),
        compiler_params=_cparams(dimension_semantics=("parallel",)),
    )(chip, p, q)


def _reduce_grads(tags, grads, chip, core, hosts=None):
    core_arr = core.reshape(1).astype(jnp.int32)
    chip_arr = chip.reshape(1).astype(jnp.int32)
    name = "_".join(tags)
    run = hosts or [functools.partial(_comm_call, f"grads_{stage}_{name}") for stage in ("pair", "chips", "swap")]

    def plan_pair(me, in_refs, out_refs):
        remote = []
        for g, o in zip(in_refs, out_refs):
            half = g.shape[1] // 2
            remote += [(g.at[k, pl.ds((1 - me[2]) * half, half)], o.at[k], _flip(me, 1)) for k in range(N_CHIP)]
        return remote
    recv = run[0](_Exchange(tuple(grads), tuple(_sds((N_CHIP, g.shape[1] // 2, g.shape[2]), F32) for g in grads),
                            N_CHIP * len(grads), plan_pair))
    pair = [_pair_add(t, g, r, core_arr) for t, g, r in zip(tags, grads, recv)]

    def plan_chips(me, in_refs, out_refs):
        remote = []
        for p, o in zip(in_refs, out_refs):
            for j in range(N_CHIP - 1):
                peer = _flip(me, 2 * (j + 1))
                remote.append((p.at[2 * peer[0] + peer[1]], o.at[j], peer))
        return remote
    parts = run[1](_Exchange(tuple(pair), tuple(_sds((N_CHIP - 1,) + p.shape[1:], BF) for p in pair),
                             (N_CHIP - 1) * len(pair), plan_chips))
    mine = [_sum_chips(t, p, q, chip_arr) for t, p, q in zip(tags, pair, parts)]
    other = run[2](_swap_cores_ex(mine))
    return list(zip(mine, other))


def _adamw_math(w, g, m, v):
    m = ADAM_B1 * m + (1.0 - ADAM_B1) * g
    v = ADAM_B2 * v + (1.0 - ADAM_B2) * (g * g)
    m_hat = m / (1.0 - ADAM_B1 ** ADAM_STEP)
    v_hat = v / (1.0 - ADAM_B2 ** ADAM_STEP)
    delta = -ADAM_LR * (m_hat / (jnp.sqrt(v_hat) + ADAM_EPS) + ADAM_WD * w)
    return delta, m, v


def _adamw(name, w, g, m, v, ex=None):
    r, cols = w.shape
    br = _rows(r)
    if br * cols * 4 > (1 << 20) and br % 16 == 0:
        br //= 2

    def body(w_ref, g_ref, m_ref, v_ref, d_ref, mo_ref, vo_ref):
        d_ref[...], mo_ref[...], vo_ref[...] = _adamw_math(w_ref[...], g_ref[...], m_ref[...], v_ref[...])

    spec = pl.BlockSpec((br, cols), lambda i: (i, 0))
    outs, landed = _hosted_call(
        body, f"adamw_{name}", (r // br,), in_specs=[spec] * 4, out_specs=[spec] * 3,
        out_shape=[_sds((r, cols), F32)] * 3, scratch_shapes=[], semantics=("parallel",), ins=(w, g, m, v), ex=ex)
    return outs if ex is None else (outs, landed)


def _adamw_halves(name, w, mine, other, m, v, core, ex=None):
    r, cols = w.shape
    br = _rows(r // 2)
    if br * cols * 4 > (1 << 20) and br % 16 == 0:
        br //= 2
    per_half = r // 2 // br

    def body(core_ref, w_ref, a_ref, b_ref, m_ref, v_ref, g_ref, d_ref, mo_ref, vo_ref):
        g = jnp.where(pl.program_id(0) // per_half == core_ref[0], a_ref[...], b_ref[...])
        g_ref[...] = g
        d_ref[...], mo_ref[...], vo_ref[...] = _adamw_math(w_ref[...], g, m_ref[...], v_ref[...])

    whole = pl.BlockSpec((br, cols), lambda i: (i, 0))
    half = pl.BlockSpec((br, cols), lambda i: (i % per_half, 0))
    outs, landed = _hosted_call(
        body, f"adamw_{name}", (r // br,),
        in_specs=[pl.BlockSpec(memory_space=pltpu.SMEM), whole, half, half, whole, whole], out_specs=[whole] * 4,
        out_shape=[_sds((r, cols), F32)] * 4, scratch_shapes=[], semantics=("parallel",),
        ins=(core.reshape(1).astype(jnp.int32), w, mine, other, m, v), ex=ex)
    return outs if ex is None else (outs, landed)


def _ada_fwd(c_all, ada_w, ada_b_cols):
    n = c_all.shape[0]

    def body(c_ref, w_ref, b_ref, o_ref):
        o_ref[...] = _mmx(_silu(c_ref[...]), w_ref[...]) + b_ref[...]

    return pl.pallas_call(
        body, name="ada_fwd", out_shape=_sds((n, ada_w.shape[1]), F32), compiler_params=_cparams(),
    )(c_all, ada_w, ada_b_cols)


def _ada_bwd(c_all, dmod_cols, w, m, v, ex=None):
    n = c_all.shape[0]
    r, cols = w.shape
    br = 128

    def body(c_ref, d_ref, w_ref, m_ref, v_ref, g_ref, dl_ref, mo_ref, vo_ref):
        cond = _silu(c_ref[...])
        g = lax.dot_general(cond, d_ref[...], (((0,), (0,)), ((), ())), precision=lax.Precision.HIGHEST,
                            preferred_element_type=F32)
        g_ref[...] = g
        dl_ref[...], mo_ref[...], vo_ref[...] = _adamw_math(w_ref[...], g, m_ref[...], v_ref[...])

    spec = pl.BlockSpec((br, cols), lambda i: (i, 0))
    outs, landed = _hosted_call(
        body, "ada_bwd", (r // br,),
        in_specs=[pl.BlockSpec((n, br), lambda i: (0, i)), pl.BlockSpec((n, cols), lambda i: (0, 0)), spec, spec, spec],
        out_specs=[spec] * 4, out_shape=[_sds((r, cols), F32)] * 4, scratch_shapes=[], semantics=("parallel",),
        ins=(c_all, dmod_cols, w, m, v), ex=ex)
    return outs if ex is None else (outs, landed)


def _sum_devices(parts):
    n, r, cols = parts.shape

    def body(p_ref, o_ref):
        acc = p_ref[0]
        for k in range(1, n):
            acc = acc + p_ref[k]
        o_ref[...] = acc

    return pl.pallas_call(body, name="sum_devices", out_shape=_sds((r, cols), F32), compiler_params=_cparams())(parts)


SMALL_ROWS = 16
_SMALL_SLOTS = dict(norm1_g=(0, 0, D), norm2_g=(1, 0, D), q_norm_g=(2, 0, HD), k_norm_g=(2, 128, HD), sinks=(2, 256, HEADS),
                    a_log=(2, 384, DN_H), dt_bias=(2, 512, DN_H), dn_norm_g=(2, 640, DN_D))
_CONV_ROW = 4
_ADA_B_ROW = 8


def _pack_small(vals, conv, ada_b):
    def row(pieces):
        out, at = [], 0
        for col, val in pieces:
            out += [jnp.zeros((1, col - at), F32), val.reshape(1, -1)]
            at = col + val.size
        return jnp.concatenate(out + [jnp.zeros((1, CONVW - at), F32)], axis=1)
    rows = {}
    for name, (r, col, n) in _SMALL_SLOTS.items():
        rows.setdefault(r, []).append((col, vals[name]))
    blank = jnp.zeros((1, CONVW), F32)
    top = [row(sorted(rows[r], key=lambda p: p[0])) if r in rows else blank for r in range(_CONV_ROW)]
    conv_rows = jnp.concatenate([conv, jnp.zeros((CONV, CONVW - conv.shape[1]), F32)], axis=1)
    tail = jnp.zeros((SMALL_ROWS - _ADA_B_ROW - 4, CONVW), F32)
    return jnp.concatenate(top + [conv_rows, ada_b.reshape(4, CONVW), tail], axis=0)


def _unpack_small(sheet, conv_cols):
    out = {name: sheet[row, col:col + n].reshape(1, n) for name, (row, col, n) in _SMALL_SLOTS.items()}
    out["conv_w"] = sheet[_CONV_ROW:_CONV_ROW + CONV, 0:conv_cols].reshape(1, CONV, 1, conv_cols)
    out["ada_b"] = sheet[_ADA_B_ROW:_ADA_B_ROW + 4, :].reshape(1, 6 * D)
    return out


def _w_in_segments():
    shard = IN_WIDTH // N_CHIP
    cuts = sorted({0, IN_WIDTH, C_Z, C_Z + 2 * DN_H} | {k * shard for k in range(1, N_CHIP)})
    segs = []
    for a, b in zip(cuts[:-1], cuts[1:]):
        k = a // shard
        pad = a if a < C_Z else (C_BA + a - C_Z if a < C_Z + 2 * DN_H else a - 2 * DN_H)
        segs.append((k, a - k * shard, b - k * shard, pad))
    return segs


def _pad_w_in(f):
    parts = [f[k][:, lo:hi] for k, lo, hi, _ in sorted(_w_in_segments(), key=lambda s: s[3])]
    return jnp.concatenate(parts + [jnp.zeros((f.shape[1], IN_PAD - IN_WIDTH), f.dtype)], axis=1)


def _unpad_w_in(g):
    return jnp.stack([jnp.concatenate([g[:, pad:pad + hi - lo] for kk, lo, hi, pad in _w_in_segments() if kk == k], axis=1)
                      for k in range(N_CHIP)])


def _blocks_to_cols(f):
    return f.transpose(1, 0, 2).reshape(f.shape[1], N_CHIP * f.shape[2])


def kernel(x, c, positions, ada_w, ada_b, norm1_g, w_in, conv_w, q_norm_g, k_norm_g, sinks, a_log, dt_bias, dn_norm_g, w_branch, w_out, norm2_g, w_gate_up, w_down, loss_target, m_ada_w, m_ada_b, m_norm1_g, m_w_in, m_conv_w, m_q_norm_g, m_k_norm_g, m_sinks, m_a_log, m_dt_bias, m_dn_norm_g, m_w_branch, m_w_out, m_norm2_g, m_w_gate_up, m_w_down, v_ada_w, v_ada_b, v_norm1_g, v_w_in, v_conv_w, v_q_norm_g, v_k_norm_g, v_sinks, v_a_log, v_dt_bias, v_dn_norm_g, v_w_branch, v_w_out, v_norm2_g, v_w_gate_up, v_w_down):
    ix, iy, ic = lax.axis_index("x"), lax.axis_index("y"), lax.axis_index("c")
    dev = 4 * ix + 2 * iy + ic
    chip = 2 * ix + iy
    n_seq = x.shape[0]
    conv_cols = conv_w.shape[-1]

    c_all, conv_all = _gather_devices("gather_cond", [c, conv_w.reshape(CONV, conv_cols)], dev)
    c_all = c_all.reshape(N_DEV * n_seq, D)
    ada_cols = ada_w.shape[-1]
    ada_b_cols = lax.dynamic_slice(ada_b, (0, chip * ada_cols), (1, ada_cols))
    mod_cols = _ada_fwd(c_all, ada_w[0], ada_b_cols)
    (mod_blocks,) = _gather_chips("gather_mod", [mod_cols], chip)
    mod_all = _blocks_to_cols(mod_blocks)
    mod = lax.dynamic_slice(mod_all, (dev * n_seq, 0), (n_seq, 6 * D)).reshape(n_seq, 1, 6 * D)
    conv_full = _blocks_to_cols(conv_all[0::2])

    (f_in,) = _gather_weights("w_in", [w_in[0].astype(BF)], chip)
    w_in_pad = _pad_w_in(f_in)

    loss, grad_x, dmod, small, (w_in_grad, r_br, r_out, r_gu, r_dn) = _local_step(
        x, mod, positions, loss_target, norm1_g, w_in_pad, conv_full.reshape(CONV, 1, CONVW), q_norm_g, k_norm_g, sinks,
        a_log, dt_bias, dn_norm_g, w_branch[0].astype(BF), w_out[0].astype(BF), norm2_g, w_gate_up[0].astype(BF),
        w_down[0].astype(BF), dist=(chip, ic))
    loss = lax.psum(loss, ("x", "y", "c"))

    part = _pack_small(small, small["conv_w"], jnp.sum(dmod, axis=(0, 1)).reshape(1, 6 * D))
    dmod_all, parts = _gather_devices("gather_small", [dmod.reshape(n_seq, 6 * D), part], dev, host=w_in_grad)
    dmod_all = dmod_all.reshape(N_DEV * n_seq, 6 * D)
    dmod_cols = lax.dynamic_slice(dmod_all, (0, chip * ada_cols), (N_DEV * n_seq, ada_cols))

    up_gu = _Hosted(lambda ex: _adamw_halves("w_gate_up", w_gate_up[0], *r_gu, m_w_gate_up[0], v_w_gate_up[0], ic, ex))
    up_ada = _Hosted(lambda ex: _ada_bwd(c_all, dmod_cols, ada_w[0], m_ada_w[0], v_ada_w[0], ex))
    up_dn = _Hosted(lambda ex: _adamw_halves("w_down", w_down[0], *r_dn, m_w_down[0], v_w_down[0], ic, ex))
    (r_in,) = _reduce_grads(("w_in",), [_unpad_w_in(w_in_grad.outs)], chip, ic, hosts=[up_gu, up_ada, up_dn])
    ada = up_ada.outs
    big = {"w_gate_up": tuple(up_gu.outs), "w_down": tuple(up_dn.outs)}
    for name, w, g, m, v in (("w_in", w_in, r_in, m_w_in, v_w_in), ("w_branch", w_branch, r_br, m_w_branch, v_w_branch),
                             ("w_out", w_out, r_out, m_w_out, v_w_out)):
        big[name] = tuple(_adamw_halves(name, w[0], *g, m[0], v[0], ic))
    g_small = _unpack_small(_sum_devices(parts), CONVW)
    g_conv = lax.dynamic_slice(g_small["conv_w"].reshape(CONV, CONVW), (0, chip * conv_cols), (CONV, conv_cols))
    g_small["conv_w"] = g_conv.reshape(1, CONV, 1, conv_cols)

    given = dict(norm1_g=(norm1_g, m_norm1_g, v_norm1_g), norm2_g=(norm2_g, m_norm2_g, v_norm2_g),
                 q_norm_g=(q_norm_g, m_q_norm_g, v_q_norm_g), k_norm_g=(k_norm_g, m_k_norm_g, v_k_norm_g),
                 sinks=(sinks, m_sinks, v_sinks), a_log=(a_log, m_a_log, v_a_log), dt_bias=(dt_bias, m_dt_bias, v_dt_bias),
                 dn_norm_g=(dn_norm_g, m_dn_norm_g, v_dn_norm_g))
    sheets = [_pack_small({k: t[j] for k, t in given.items()}, cw.reshape(CONV, conv_cols), ab)
              for j, (cw, ab) in enumerate(((conv_w, ada_b), (m_conv_w, m_ada_b), (v_conv_w, v_ada_b)))]
    g_local = _pack_small(g_small, g_conv, g_small["ada_b"])
    upd = [_unpack_small(s, conv_cols) for s in _adamw("small", sheets[0], g_local, sheets[1], sheets[2])]

    names = ["ada_w", "ada_b", "norm1_g", "w_in", "conv_w", "q_norm_g", "k_norm_g", "sinks", "a_log", "dt_bias", "dn_norm_g",
             "w_branch", "w_out", "norm2_g", "w_gate_up", "w_down"]

    def leaf(name, j):
        if name == "ada_w":
            return ada[j][None]
        if name in big:
            return big[name][j][None]
        return g_small[name] if j == 0 else upd[j - 1][name]

    return (loss, grad_x) + tuple(leaf(n, j) for j in range(4) for n in names)
```
